```python
import jax, jax.numpy as jnp
from jax import lax
import numpy as np

D_MODEL = 2048
BATCH = 8
SEQ = 4096
DEPTH = 1

PLE_DIM = 256
RW_HEADS = 16
RW_HEAD_DIM = 64
RW_WIDTH = RW_HEADS * RW_HEAD_DIM
DECAY_LORA = 64
AAA_LORA = 64
FOX_HEADS = 16
FOX_HEAD_DIM = 64
FOX_WIDTH = FOX_HEADS * FOX_HEAD_DIM
Q_BLOCK = 128
NORM_EPS = 1e-6
GN_EPS = 64e-5

RW_COLS = 4 * RW_WIDTH + DECAY_LORA + AAA_LORA
FOX_COLS = 4 * FOX_WIDTH + FOX_HEADS
GATE_COLS = 2 * D_MODEL
N_IN = RW_COLS + FOX_COLS + GATE_COLS

kernel_name = 'hybrid_rwkv7_fox_gated_merge'


def _rmsnorm(x, g):
    xf = x.astype(jnp.float32)
    y = xf * lax.rsqrt(jnp.mean(xf * xf, axis=-1, keepdims=True) + NORM_EPS)
    return (y * g.astype(jnp.float32)).astype(x.dtype)


def _token_shift(z, mu):
    z_prev = jnp.pad(z, ((0, 0), (1, 0), (0, 0)))[:, :-1]
    return z + (z_prev - z) * mu


def _rwkv7_scan(r, w, k, v, kk, a):
    def step(S, inp):
        r_t, w_t, k_t, v_t, kk_t, a_t = inp
        s_kk = jnp.einsum('bhvk,bhk->bhv', S, kk_t)
        S = (S * w_t[:, :, None, :]
             - s_kk[..., None] * (kk_t * a_t)[:, :, None, :]
             + v_t[..., None] * k_t[:, :, None, :])
        y_t = jnp.einsum('bhvk,bhk->bhv', S, r_t)
        return S, y_t
    B, T, H, N = r.shape
    S0 = jnp.zeros((B, H, N, N), jnp.float32)
    xs = tuple(jnp.swapaxes(t, 0, 1) for t in (r, w, k, v, kk, a))
    _, ys = lax.scan(step, S0, xs)
    return jnp.swapaxes(ys, 0, 1)


def _rwkv7_branch(z, w0, w_lora_up, a0, a_lora_up, k_k, k_a, r_k, ln_g, ln_b):
    B, T, _ = z.shape
    f32 = jnp.float32
    C = RW_WIDTH
    r = z[..., 0:C]
    k = z[..., C:2 * C]
    v = z[..., 2 * C:3 * C]
    g = z[..., 3 * C:4 * C]
    wl = z[..., 4 * C:4 * C + DECAY_LORA]
    al = z[..., 4 * C + DECAY_LORA:]
    w_raw = (w0 + jnp.tanh(wl) @ w_lora_up).astype(f32)
    decay = jnp.exp(-jnp.exp(-jax.nn.softplus(-w_raw) - 0.5))
    a = jax.nn.sigmoid((a0 + al @ a_lora_up).astype(f32))
    hs = lambda t: t.astype(f32).reshape(B, T, RW_HEADS, RW_HEAD_DIM)
    r_h, k_h, v_h, a_h, w_h = hs(r), hs(k), hs(v), hs(a), hs(decay)
    kk = k_h * k_k.astype(f32).reshape(RW_HEADS, RW_HEAD_DIM)
    kk = kk / jnp.maximum(jnp.sqrt(jnp.sum(kk * kk, axis=-1, keepdims=True)), 1e-12)
    k_h = k_h * (1.0 + (a_h - 1.0) * k_a.astype(f32).reshape(RW_HEADS, RW_HEAD_DIM))
    y = _rwkv7_scan(r_h, w_h, k_h, v_h, kk, a_h)
    mu = jnp.mean(y, axis=-1, keepdims=True)
    var = jnp.mean(jnp.square(y - mu), axis=-1, keepdims=True)
    y = ((y - mu) * lax.rsqrt(var + GN_EPS)).reshape(B, T, C)
    y = y * ln_g.astype(f32) + ln_b.astype(f32)
    bonus = jnp.sum(r_h * k_h * r_k.astype(f32), axis=-1, keepdims=True) * v_h
    y = y + bonus.reshape(B, T, C)
    return (y * jax.nn.silu(g.astype(f32))).astype(z.dtype)


def _fox_branch(z, b_f):
    B, T, _ = z.shape
    f32 = jnp.float32
    C = FOX_WIDTH
    to_heads = lambda t: t.astype(f32).reshape(B, T, FOX_HEADS, FOX_HEAD_DIM).transpose(0, 2, 1, 3)
    qh = to_heads(z[..., 0:C])
    kh = to_heads(z[..., C:2 * C])
    vh = to_heads(z[..., 2 * C:3 * C])
    g = z[..., 3 * C:4 * C]
    fl = z[..., 4 * C:]
    log_f = jax.nn.log_sigmoid((fl + b_f).astype(f32))
    c = jnp.cumsum(log_f, axis=1).transpose(0, 2, 1)
    scale = FOX_HEAD_DIM ** -0.5
    outs = []
    for blk in range(T // Q_BLOCK):
        s = blk * Q_BLOCK
        e = s + Q_BLOCK
        logits = jnp.einsum('bhqd,bhkd->bhqk', qh[:, :, s:e], kh[:, :, :e]) * scale
        logits = logits + (c[:, :, s:e, None] - c[:, :, None, :e])
        causal = jnp.arange(s, e)[:, None] >= jnp.arange(e)[None, :]
        logits = jnp.where(causal, logits, -jnp.inf)
        probs = jax.nn.softmax(logits, axis=-1)
        outs.append(jnp.einsum('bhqk,bhkd->bhqd', probs, vh[:, :, :e]))
    o = jnp.concatenate(outs, axis=2).transpose(0, 2, 1, 3).reshape(B, T, C)
    return (o * jax.nn.silu(g.astype(f32))).astype(z.dtype)


def _fwd_setup_inputs(seed: int = 0) -> dict:
    key = jax.random.key(seed)
    ks = jax.random.split(key, 24)
    f32 = jnp.float32
    nrm = lambda k, shape, s: jax.random.normal(k, shape, f32) * s
    C = RW_WIDTH
    return {
        'x': nrm(ks[0], (BATCH, SEQ, D_MODEL), 1.0),
        'p': nrm(ks[1], (DEPTH, BATCH, SEQ, PLE_DIM), 1.0),
        'norm_g': 1.0 + nrm(ks[2], (DEPTH, D_MODEL), 0.02),
        'w_in': nrm(ks[3], (DEPTH, D_MODEL, N_IN), D_MODEL ** -0.5),
        'rw_shift_mu': jax.random.uniform(ks[4], (DEPTH, RW_COLS), f32),
        'rw_w0': jax.random.uniform(ks[5], (DEPTH, C), f32, -6.0, -1.0),
        'rw_w_lora_up': nrm(ks[6], (DEPTH, DECAY_LORA, C), 0.1 * DECAY_LORA ** -0.5),
        'rw_a0': nrm(ks[7], (DEPTH, C), 0.1),
        'rw_a_lora_up': nrm(ks[8], (DEPTH, AAA_LORA, C), 0.1 * AAA_LORA ** -0.5),
        'rw_k_k': 0.85 + nrm(ks[9], (DEPTH, C), 0.02),
        'rw_k_a': 1.0 + nrm(ks[10], (DEPTH, C), 0.02),
        'rw_r_k': nrm(ks[11], (DEPTH, RW_HEADS, RW_HEAD_DIM), 0.1),
        'rw_ln_g': 1.0 + nrm(ks[12], (DEPTH, C), 0.02),
        'rw_ln_b': nrm(ks[13], (DEPTH, C), 0.02),
        'fox_b_f': jax.random.uniform(ks[14], (DEPTH, FOX_HEADS), f32, 1.0, 5.0),
        'w_up_rwkv': nrm(ks[15], (DEPTH, RW_WIDTH, D_MODEL), RW_WIDTH ** -0.5),
        'w_up_fox': nrm(ks[16], (DEPTH, FOX_WIDTH, D_MODEL), FOX_WIDTH ** -0.5),
        'w_out': nrm(ks[17], (DEPTH, D_MODEL, D_MODEL), D_MODEL ** -0.5),
        'ple_proj': nrm(ks[18], (DEPTH, PLE_DIM, D_MODEL), PLE_DIM ** -0.5),
        'ple_gate_w': nrm(ks[19], (DEPTH, D_MODEL, D_MODEL), D_MODEL ** -0.5),
        'ple_norm_g': 1.0 + nrm(ks[20], (DEPTH, D_MODEL), 0.02),
        'final_norm_g': 1.0 + nrm(ks[21], (D_MODEL,), 0.02),
    }


def _fwd_reference(x, p, norm_g, w_in, rw_shift_mu, rw_w0, rw_w_lora_up, rw_a0, rw_a_lora_up,
              rw_k_k, rw_k_a, rw_r_k, rw_ln_g, rw_ln_b, fox_b_f, w_up_rwkv, w_up_fox, w_out,
              ple_proj, ple_gate_w, ple_norm_g, final_norm_g):
    for i in range(DEPTH):
        h = _rmsnorm(x, norm_g[i])
        z = h @ w_in[i]
        z_rw = _token_shift(z[..., :RW_COLS], rw_shift_mu[i])
        z_fox = z[..., RW_COLS:RW_COLS + FOX_COLS]
        z_gate = z[..., RW_COLS + FOX_COLS:]
        y_rw = _rwkv7_branch(z_rw, rw_w0[i], rw_w_lora_up[i], rw_a0[i], rw_a_lora_up[i],
                             rw_k_k[i], rw_k_a[i], rw_r_k[i], rw_ln_g[i], rw_ln_b[i])
        y_fox = _fox_branch(z_fox, fox_b_f[i])
        u_rw = y_rw @ w_up_rwkv[i]
        u_fox = y_fox @ w_up_fox[i]
        merged = (jax.nn.sigmoid(z_gate[..., :D_MODEL]) * u_rw
                  + jax.nn.sigmoid(z_gate[..., D_MODEL:]) * u_fox)
        x = x + merged @ w_out[i]
        ple = p[i] @ ple_proj[i]
        x = x + ple * jax.nn.sigmoid(_rmsnorm(x, ple_norm_g[i]) @ ple_gate_w[i])
    return _rmsnorm(x, final_norm_g)


import jax as _jax
import jax.numpy as _jnp

TWIN_FORMAT = 'train_step'
FWD_PARAMS = ['x', 'p', 'norm_g', 'w_in', 'rw_shift_mu', 'rw_w0', 'rw_w_lora_up', 'rw_a0', 'rw_a_lora_up', 'rw_k_k', 'rw_k_a', 'rw_r_k', 'rw_ln_g', 'rw_ln_b', 'fox_b_f', 'w_up_rwkv', 'w_up_fox', 'w_out', 'ple_proj', 'ple_gate_w', 'ple_norm_g', 'final_norm_g']
TWIN_WEIGHTS = ['norm_g', 'w_in', 'rw_shift_mu', 'rw_w0', 'rw_w_lora_up', 'rw_a0', 'rw_a_lora_up', 'rw_k_k', 'rw_k_a', 'rw_r_k', 'rw_ln_g', 'rw_ln_b', 'fox_b_f', 'w_up_rwkv', 'w_up_fox', 'w_out', 'ple_proj', 'ple_gate_w', 'ple_norm_g', 'final_norm_g']
TWIN_DIFF_INPUT = 'x'
TWIN_INPUTS = ['x', 'p', 'norm_g', 'w_in', 'rw_shift_mu', 'rw_w0', 'rw_w_lora_up', 'rw_a0', 'rw_a_lora_up', 'rw_k_k', 'rw_k_a', 'rw_r_k', 'rw_ln_g', 'rw_ln_b', 'fox_b_f', 'w_up_rwkv', 'w_up_fox', 'w_out', 'ple_proj', 'ple_gate_w', 'ple_norm_g', 'final_norm_g', 'loss_target', 'm_norm_g', 'm_w_in', 'm_rw_shift_mu', 'm_rw_w0', 'm_rw_w_lora_up', 'm_rw_a0', 'm_rw_a_lora_up', 'm_rw_k_k', 'm_rw_k_a', 'm_rw_r_k', 'm_rw_ln_g', 'm_rw_ln_b', 'm_fox_b_f', 'm_w_up_rwkv', 'm_w_up_fox', 'm_w_out', 'm_ple_proj', 'm_ple_gate_w', 'm_ple_norm_g', 'm_final_norm_g', 'v_norm_g', 'v_w_in', 'v_rw_shift_mu', 'v_rw_w0', 'v_rw_w_lora_up', 'v_rw_a0', 'v_rw_a_lora_up', 'v_rw_k_k', 'v_rw_k_a', 'v_rw_r_k', 'v_rw_ln_g', 'v_rw_ln_b', 'v_fox_b_f', 'v_w_up_rwkv', 'v_w_up_fox', 'v_w_out', 'v_ple_proj', 'v_ple_gate_w', 'v_ple_norm_g', 'v_final_norm_g']
TWIN_OUTPUTS = ['loss', 'grad_x', 'grad_norm_g', 'grad_w_in', 'grad_rw_shift_mu', 'grad_rw_w0', 'grad_rw_w_lora_up', 'grad_rw_a0', 'grad_rw_a_lora_up', 'grad_rw_k_k', 'grad_rw_k_a', 'grad_rw_r_k', 'grad_rw_ln_g', 'grad_rw_ln_b', 'grad_fox_b_f', 'grad_w_up_rwkv', 'grad_w_up_fox', 'grad_w_out', 'grad_ple_proj', 'grad_ple_gate_w', 'grad_ple_norm_g', 'grad_final_norm_g', 'delta_norm_g', 'delta_w_in', 'delta_rw_shift_mu', 'delta_rw_w0', 'delta_rw_w_lora_up', 'delta_rw_a0', 'delta_rw_a_lora_up', 'delta_rw_k_k', 'delta_rw_k_a', 'delta_rw_r_k', 'delta_rw_ln_g', 'delta_rw_ln_b', 'delta_fox_b_f', 'delta_w_up_rwkv', 'delta_w_up_fox', 'delta_w_out', 'delta_ple_proj', 'delta_ple_gate_w', 'delta_ple_norm_g', 'delta_final_norm_g', 'new_m_norm_g', 'new_m_w_in', 'new_m_rw_shift_mu', 'new_m_rw_w0', 'new_m_rw_w_lora_up', 'new_m_rw_a0', 'new_m_rw_a_lora_up', 'new_m_rw_k_k', 'new_m_rw_k_a', 'new_m_rw_r_k', 'new_m_rw_ln_g', 'new_m_rw_ln_b', 'new_m_fox_b_f', 'new_m_w_up_rwkv', 'new_m_w_up_fox', 'new_m_w_out', 'new_m_ple_proj', 'new_m_ple_gate_w', 'new_m_ple_norm_g', 'new_m_final_norm_g', 'new_v_norm_g', 'new_v_w_in', 'new_v_rw_shift_mu', 'new_v_rw_w0', 'new_v_rw_w_lora_up', 'new_v_rw_a0', 'new_v_rw_a_lora_up', 'new_v_rw_k_k', 'new_v_rw_k_a', 'new_v_rw_r_k', 'new_v_rw_ln_g', 'new_v_rw_ln_b', 'new_v_fox_b_f', 'new_v_w_up_rwkv', 'new_v_w_up_fox', 'new_v_w_out', 'new_v_ple_proj', 'new_v_ple_gate_w', 'new_v_ple_norm_g', 'new_v_final_norm_g']
TWIN_LEAF_KINDS = {'loss': 'loss', 'grad_x': 'grad_x', 'grad_norm_g': 'grad_w', 'grad_w_in': 'grad_w', 'grad_rw_shift_mu': 'grad_w', 'grad_rw_w0': 'grad_w', 'grad_rw_w_lora_up': 'grad_w', 'grad_rw_a0': 'grad_w', 'grad_rw_a_lora_up': 'grad_w', 'grad_rw_k_k': 'grad_w', 'grad_rw_k_a': 'grad_w', 'grad_rw_r_k': 'grad_w', 'grad_rw_ln_g': 'grad_w', 'grad_rw_ln_b': 'grad_w', 'grad_fox_b_f': 'grad_w', 'grad_w_up_rwkv': 'grad_w', 'grad_w_up_fox': 'grad_w', 'grad_w_out': 'grad_w', 'grad_ple_proj': 'grad_w', 'grad_ple_gate_w': 'grad_w', 'grad_ple_norm_g': 'grad_w', 'grad_final_norm_g': 'grad_w', 'delta_norm_g': 'delta_w', 'delta_w_in': 'delta_w', 'delta_rw_shift_mu': 'delta_w', 'delta_rw_w0': 'delta_w', 'delta_rw_w_lora_up': 'delta_w', 'delta_rw_a0': 'delta_w', 'delta_rw_a_lora_up': 'delta_w', 'delta_rw_k_k': 'delta_w', 'delta_rw_k_a': 'delta_w', 'delta_rw_r_k': 'delta_w', 'delta_rw_ln_g': 'delta_w', 'delta_rw_ln_b': 'delta_w', 'delta_fox_b_f': 'delta_w', 'delta_w_up_rwkv': 'delta_w', 'delta_w_up_fox': 'delta_w', 'delta_w_out': 'delta_w', 'delta_ple_proj': 'delta_w', 'delta_ple_gate_w': 'delta_w', 'delta_ple_norm_g': 'delta_w', 'delta_final_norm_g': 'delta_w', 'new_m_norm_g': 'new_m', 'new_m_w_in': 'new_m', 'new_m_rw_shift_mu': 'new_m', 'new_m_rw_w0': 'new_m', 'new_m_rw_w_lora_up': 'new_m', 'new_m_rw_a0': 'new_m', 'new_m_rw_a_lora_up': 'new_m', 'new_m_rw_k_k': 'new_m', 'new_m_rw_k_a': 'new_m', 'new_m_rw_r_k': 'new_m', 'new_m_rw_ln_g': 'new_m', 'new_m_rw_ln_b': 'new_m', 'new_m_fox_b_f': 'new_m', 'new_m_w_up_rwkv': 'new_m', 'new_m_w_up_fox': 'new_m', 'new_m_w_out': 'new_m', 'new_m_ple_proj': 'new_m', 'new_m_ple_gate_w': 'new_m', 'new_m_ple_norm_g': 'new_m', 'new_m_final_norm_g': 'new_m', 'new_v_norm_g': 'new_v', 'new_v_w_in': 'new_v', 'new_v_rw_shift_mu': 'new_v', 'new_v_rw_w0': 'new_v', 'new_v_rw_w_lora_up': 'new_v', 'new_v_rw_a0': 'new_v', 'new_v_rw_a_lora_up': 'new_v', 'new_v_rw_k_k': 'new_v', 'new_v_rw_k_a': 'new_v', 'new_v_rw_r_k': 'new_v', 'new_v_rw_ln_g': 'new_v', 'new_v_rw_ln_b': 'new_v', 'new_v_fox_b_f': 'new_v', 'new_v_w_up_rwkv': 'new_v', 'new_v_w_up_fox': 'new_v', 'new_v_w_out': 'new_v', 'new_v_ple_proj': 'new_v', 'new_v_ple_gate_w': 'new_v', 'new_v_ple_norm_g': 'new_v', 'new_v_final_norm_g': 'new_v'}


def _forward(args):
    return _fwd_reference(*[args[k] for k in FWD_PARAMS])


def _output_shape():
    def fwd():
        inp = _fwd_setup_inputs(0)
        return _fwd_reference(*[inp[k] for k in FWD_PARAMS])
    out = _jax.eval_shape(fwd)
    return out.shape, out.dtype

N_MICROBATCH = 1
ADAM_LR = 0.001
ADAM_B1 = 0.9
ADAM_B2 = 0.999
ADAM_EPS = 1e-08
ADAM_WD = 0.01
ADAM_STEP = 10
PER_EXAMPLE_BATCH_AXIS = {'x': 0, 'p': 1, 'loss_target': 0}
SHARED_INPUTS = []
_WEIGHT_DTYPES = {'norm_g': _jnp.float32, 'w_in': _jnp.float32, 'rw_shift_mu': _jnp.float32, 'rw_w0': _jnp.float32, 'rw_w_lora_up': _jnp.float32, 'rw_a0': _jnp.float32, 'rw_a_lora_up': _jnp.float32, 'rw_k_k': _jnp.float32, 'rw_k_a': _jnp.float32, 'rw_r_k': _jnp.float32, 'rw_ln_g': _jnp.float32, 'rw_ln_b': _jnp.float32, 'fox_b_f': _jnp.float32, 'w_up_rwkv': _jnp.float32, 'w_up_fox': _jnp.float32, 'w_out': _jnp.float32, 'ple_proj': _jnp.float32, 'ple_gate_w': _jnp.float32, 'ple_norm_g': _jnp.float32, 'final_norm_g': _jnp.float32}
MOMENT_SCALE = {'norm_g': 4.694995e-02, 'w_in': 1.878621e-02, 'rw_shift_mu': 5.000725e-02, 'rw_w0': 1.131445e-02, 'rw_w_lora_up': 1.291505e-03, 'rw_a0': 1.237992e-02, 'rw_a_lora_up': 1.198647e-02, 'rw_k_k': 3.174152e-02, 'rw_k_a': 3.028573e-02, 'rw_r_k': 6.289321e-02, 'rw_ln_g': 2.961617e-02, 'rw_ln_b': 2.830672e-02, 'fox_b_f': 6.444254e-02, 'w_up_rwkv': 2.041340e-02, 'w_up_fox': 9.316507e-03, 'w_out': 2.221708e-02, 'ple_proj': 4.130214e-02, 'ple_gate_w': 1.601031e-02, 'ple_norm_g': 1.594707e-02, 'final_norm_g': 1.599738e+01}


def _to_microbatches(a, axis):
    t = _jnp.moveaxis(a, axis, 0)
    t = t.reshape((N_MICROBATCH, t.shape[0] // N_MICROBATCH) + t.shape[1:])
    return _jnp.moveaxis(t, 1, axis + 1)


def setup_inputs(seed: int = 0) -> dict:
    inp = _fwd_setup_inputs(seed)
    key = _jax.random.fold_in(_jax.random.key(seed), 7919)
    shape, _ = _output_shape()
    out = dict(inp)
    out["loss_target"] = _jax.random.normal(_jax.random.fold_in(key, 0), shape, _jnp.float32)
    for i, name in enumerate(TWIN_WEIGHTS):
        w = inp[name].astype(_jnp.float32)
        if MOMENT_SCALE is None:
            s = _jnp.sqrt(_jnp.mean(_jnp.square(w)) + 1e-30)
        else:
            s = MOMENT_SCALE[name]
        km, kv = _jax.random.split(_jax.random.fold_in(key, i + 1))
        out[name] = w
        out["m_" + name] = s * _jax.random.normal(km, w.shape, _jnp.float32)
        out["v_" + name] = (s * s) * _jax.random.uniform(kv, w.shape, _jnp.float32, 0.5, 1.5)
    if N_MICROBATCH > 1:
        for name, axis in PER_EXAMPLE_BATCH_AXIS.items():
            out[name] = _to_microbatches(out[name], axis)
    return {'x': out['x'], 'p': out['p'], 'norm_g': out['norm_g'], 'w_in': out['w_in'], 'rw_shift_mu': out['rw_shift_mu'], 'rw_w0': out['rw_w0'], 'rw_w_lora_up': out['rw_w_lora_up'], 'rw_a0': out['rw_a0'], 'rw_a_lora_up': out['rw_a_lora_up'], 'rw_k_k': out['rw_k_k'], 'rw_k_a': out['rw_k_a'], 'rw_r_k': out['rw_r_k'], 'rw_ln_g': out['rw_ln_g'], 'rw_ln_b': out['rw_ln_b'], 'fox_b_f': out['fox_b_f'], 'w_up_rwkv': out['w_up_rwkv'], 'w_up_fox': out['w_up_fox'], 'w_out': out['w_out'], 'ple_proj': out['ple_proj'], 'ple_gate_w': out['ple_gate_w'], 'ple_norm_g': out['ple_norm_g'], 'final_norm_g': out['final_norm_g'], 'loss_target': out['loss_target'], 'm_norm_g': out['m_norm_g'], 'm_w_in': out['m_w_in'], 'm_rw_shift_mu': out['m_rw_shift_mu'], 'm_rw_w0': out['m_rw_w0'], 'm_rw_w_lora_up': out['m_rw_w_lora_up'], 'm_rw_a0': out['m_rw_a0'], 'm_rw_a_lora_up': out['m_rw_a_lora_up'], 'm_rw_k_k': out['m_rw_k_k'], 'm_rw_k_a': out['m_rw_k_a'], 'm_rw_r_k': out['m_rw_r_k'], 'm_rw_ln_g': out['m_rw_ln_g'], 'm_rw_ln_b': out['m_rw_ln_b'], 'm_fox_b_f': out['m_fox_b_f'], 'm_w_up_rwkv': out['m_w_up_rwkv'], 'm_w_up_fox': out['m_w_up_fox'], 'm_w_out': out['m_w_out'], 'm_ple_proj': out['m_ple_proj'], 'm_ple_gate_w': out['m_ple_gate_w'], 'm_ple_norm_g': out['m_ple_norm_g'], 'm_final_norm_g': out['m_final_norm_g'], 'v_norm_g': out['v_norm_g'], 'v_w_in': out['v_w_in'], 'v_rw_shift_mu': out['v_rw_shift_mu'], 'v_rw_w0': out['v_rw_w0'], 'v_rw_w_lora_up': out['v_rw_w_lora_up'], 'v_rw_a0': out['v_rw_a0'], 'v_rw_a_lora_up': out['v_rw_a_lora_up'], 'v_rw_k_k': out['v_rw_k_k'], 'v_rw_k_a': out['v_rw_k_a'], 'v_rw_r_k': out['v_rw_r_k'], 'v_rw_ln_g': out['v_rw_ln_g'], 'v_rw_ln_b': out['v_rw_ln_b'], 'v_fox_b_f': out['v_fox_b_f'], 'v_w_up_rwkv': out['v_w_up_rwkv'], 'v_w_up_fox': out['v_w_up_fox'], 'v_w_out': out['v_w_out'], 'v_ple_proj': out['v_ple_proj'], 'v_ple_gate_w': out['v_ple_gate_w'], 'v_ple_norm_g': out['v_ple_norm_g'], 'v_final_norm_g': out['v_final_norm_g']}


def _loss(weights, diff, rest, loss_target):
    with _jax.named_scope("forward"):
        args = {**rest, TWIN_DIFF_INPUT: diff, **{k: w.astype(_WEIGHT_DTYPES[k]) for k, w in weights.items()}}
        y = _forward(args)
    with _jax.named_scope("loss_head"):
        err = _jnp.square(y.astype(_jnp.float32) - loss_target)
        return 0.5 * _jnp.sum(_jnp.mean(err, axis=-1)) if err.ndim else 0.5 * err


def _adamw(w, g, m, v):
    m = ADAM_B1 * m + (1.0 - ADAM_B1) * g
    v = ADAM_B2 * v + (1.0 - ADAM_B2) * _jnp.square(g)
    m_hat = m / (1.0 - ADAM_B1 ** ADAM_STEP)
    v_hat = v / (1.0 - ADAM_B2 ** ADAM_STEP)
    delta = -ADAM_LR * (m_hat / (_jnp.sqrt(v_hat) + ADAM_EPS) + ADAM_WD * w)
    return delta, m, v


def reference(x, p, norm_g, w_in, rw_shift_mu, rw_w0, rw_w_lora_up, rw_a0, rw_a_lora_up, rw_k_k, rw_k_a, rw_r_k, rw_ln_g, rw_ln_b, fox_b_f, w_up_rwkv, w_up_fox, w_out, ple_proj, ple_gate_w, ple_norm_g, final_norm_g, loss_target, m_norm_g, m_w_in, m_rw_shift_mu, m_rw_w0, m_rw_w_lora_up, m_rw_a0, m_rw_a_lora_up, m_rw_k_k, m_rw_k_a, m_rw_r_k, m_rw_ln_g, m_rw_ln_b, m_fox_b_f, m_w_up_rwkv, m_w_up_fox, m_w_out, m_ple_proj, m_ple_gate_w, m_ple_norm_g, m_final_norm_g, v_norm_g, v_w_in, v_rw_shift_mu, v_rw_w0, v_rw_w_lora_up, v_rw_a0, v_rw_a_lora_up, v_rw_k_k, v_rw_k_a, v_rw_r_k, v_rw_ln_g, v_rw_ln_b, v_fox_b_f, v_w_up_rwkv, v_w_up_fox, v_w_out, v_ple_proj, v_ple_gate_w, v_ple_norm_g, v_final_norm_g):
    given = dict(x=x, p=p, norm_g=norm_g, w_in=w_in, rw_shift_mu=rw_shift_mu, rw_w0=rw_w0, rw_w_lora_up=rw_w_lora_up, rw_a0=rw_a0, rw_a_lora_up=rw_a_lora_up, rw_k_k=rw_k_k, rw_k_a=rw_k_a, rw_r_k=rw_r_k, rw_ln_g=rw_ln_g, rw_ln_b=rw_ln_b, fox_b_f=fox_b_f, w_up_rwkv=w_up_rwkv, w_up_fox=w_up_fox, w_out=w_out, ple_proj=ple_proj, ple_gate_w=ple_gate_w, ple_norm_g=ple_norm_g, final_norm_g=final_norm_g, loss_target=loss_target, m_norm_g=m_norm_g, m_w_in=m_w_in, m_rw_shift_mu=m_rw_shift_mu, m_rw_w0=m_rw_w0, m_rw_w_lora_up=m_rw_w_lora_up, m_rw_a0=m_rw_a0, m_rw_a_lora_up=m_rw_a_lora_up, m_rw_k_k=m_rw_k_k, m_rw_k_a=m_rw_k_a, m_rw_r_k=m_rw_r_k, m_rw_ln_g=m_rw_ln_g, m_rw_ln_b=m_rw_ln_b, m_fox_b_f=m_fox_b_f, m_w_up_rwkv=m_w_up_rwkv, m_w_up_fox=m_w_up_fox, m_w_out=m_w_out, m_ple_proj=m_ple_proj, m_ple_gate_w=m_ple_gate_w, m_ple_norm_g=m_ple_norm_g, m_final_norm_g=m_final_norm_g, v_norm_g=v_norm_g, v_w_in=v_w_in, v_rw_shift_mu=v_rw_shift_mu, v_rw_w0=v_rw_w0, v_rw_w_lora_up=v_rw_w_lora_up, v_rw_a0=v_rw_a0, v_rw_a_lora_up=v_rw_a_lora_up, v_rw_k_k=v_rw_k_k, v_rw_k_a=v_rw_k_a, v_rw_r_k=v_rw_r_k, v_rw_ln_g=v_rw_ln_g, v_rw_ln_b=v_rw_ln_b, v_fox_b_f=v_fox_b_f, v_w_up_rwkv=v_w_up_rwkv, v_w_up_fox=v_w_up_fox, v_w_out=v_w_out, v_ple_proj=v_ple_proj, v_ple_gate_w=v_ple_gate_w, v_ple_norm_g=v_ple_norm_g, v_final_norm_g=v_final_norm_g)
    weights = {n: given[n] for n in TWIN_WEIGHTS}
    shared = {n: given[n] for n in SHARED_INPUTS}
    per_example = {n: given[n] for n in ['x', 'p']}
    grad_fn = _jax.value_and_grad(_loss, argnums=(0, 1))

    def one_microbatch(ex, loss_target):
        ex = dict(ex)
        diff = ex.pop(TWIN_DIFF_INPUT)
        return grad_fn(weights, diff, {**shared, **ex}, loss_target)

    if N_MICROBATCH == 1:
        loss, (grad_w, grad_x) = one_microbatch(per_example, given["loss_target"])
    else:
        def body(carry, xs):
            loss_sum, grad_sum = carry
            l_k, (gw_k, gx_k) = one_microbatch(xs[0], xs[1])
            with _jax.named_scope("update"):
                return (loss_sum + l_k, _jax.tree.map(_jnp.add, grad_sum, gw_k)), gx_k

        init = (_jnp.zeros((), _jnp.float32), _jax.tree.map(_jnp.zeros_like, weights))
        (loss, grad_w), grad_x = _jax.lax.scan(body, init, (per_example, given["loss_target"]))
    with _jax.named_scope("update"):
        delta_w, new_m, new_v = {}, {}, {}
        for n in TWIN_WEIGHTS:
            delta_w[n], new_m[n], new_v[n] = _adamw(weights[n], grad_w[n], given["m_" + n], given["v_" + n])
    return (loss, grad_x, *[grad_w[n] for n in TWIN_WEIGHTS], *[delta_w[n] for n in TWIN_WEIGHTS],
            *[new_m[n] for n in TWIN_WEIGHTS], *[new_v[n] for n in TWIN_WEIGHTS])
```

```python
import functools
import math

import jax
import jax.numpy as jnp
from jax import lax
from jax.experimental import pallas as pl
from jax.experimental.pallas import tpu as pltpu

F32, BF16 = jnp.float32, jnp.bfloat16
HI = lax.Precision.HIGHEST
LANES = 128
SUBLANES = 8
HEAD = 64
NORM_EPS = 1e-6
GN_EPS = 64e-5
VMEM_LIMIT = 56 * 1024 * 1024
NDEV = 8
PACK_W = 1024
PACK_ALIGN = 16 * PACK_W
MESH_AXES = ("x", "y", "c")
MESH = pl.DeviceIdType.MESH

ADAM_LR, ADAM_B1, ADAM_B2, ADAM_EPS, ADAM_WD, ADAM_STEP = 0.001, 0.9, 0.999, 1e-08, 0.01, 10


def _round_up(n, m):
    return (n + m - 1) // m * m


def _pick(dim, pref, align=LANES):
    if dim <= pref:
        return dim
    best = None
    for cand in range(align, pref + 1, align):
        if dim % cand == 0:
            best = cand
    return dim if best is None else best


def _params(*sem):
    return pltpu.CompilerParams(dimension_semantics=sem, vmem_limit_bytes=VMEM_LIMIT)


def _sigmoid(v):
    return jax.nn.sigmoid(v)


def _log_sigmoid(v):
    return jnp.minimum(v, 0.0) - jnp.log(1.0 + jnp.exp(-jnp.abs(v)))


_DOT_DIMS = {"nn": (((1,), (0,)), ((), ())), "nt": (((1,), (1,)), ((), ())), "tn": (((0,), (0,)), ((), ()))}


def _matmul(a, b, mode, name, out_dtype=F32, bm=512, bn=1024, bk=512):
    if mode == "tn":
        k_dim, m_dim = a.shape
    else:
        m_dim, k_dim = a.shape
    n_dim = b.shape[0] if mode == "nt" else b.shape[1]
    bm, bn, bk = _pick(m_dim, bm), _pick(n_dim, bn), _pick(k_dim, bk)
    nk = k_dim // bk

    def body(a_ref, b_ref, o_ref, acc_ref):
        k = pl.program_id(2)

        @pl.when(k == 0)
        def _():
            acc_ref[...] = jnp.zeros_like(acc_ref)

        acc_ref[...] += lax.dot_general(a_ref[...].astype(BF16), b_ref[...].astype(BF16), _DOT_DIMS[mode],
                                        preferred_element_type=F32)

        @pl.when(k == nk - 1)
        def _():
            o_ref[...] = acc_ref[...].astype(o_ref.dtype)

    if mode == "tn":
        a_spec = pl.BlockSpec((bk, bm), lambda i, j, k: (k, i))
    else:
        a_spec = pl.BlockSpec((bm, bk), lambda i, j, k: (i, k))
    if mode == "nt":
        b_spec = pl.BlockSpec((bn, bk), lambda i, j, k: (j, k))
    else:
        b_spec = pl.BlockSpec((bk, bn), lambda i, j, k: (k, j))
    return pl.pallas_call(
        body, name=name, grid=(m_dim // bm, n_dim // bn, nk),
        in_specs=[a_spec, b_spec], out_specs=pl.BlockSpec((bm, bn), lambda i, j, k: (i, j)),
        out_shape=jax.ShapeDtypeStruct((m_dim, n_dim), out_dtype),
        scratch_shapes=[pltpu.VMEM((bm, bn), F32)],
        compiler_params=_params("parallel", "parallel", "arbitrary"),
    )(a, b)


def _rowcall(body, name, t_dim, bt, row_ins, const_ins, row_outs, acc_outs=(), scratch=(), reverse=False):
    nt = t_dim // bt

    def rmap(i):
        return nt - 1 - i if reverse else i

    in_specs, args = [], []
    for item in row_ins:
        arr, cb, w = item if isinstance(item, tuple) else (item, 0, item.shape[1])
        in_specs.append(pl.BlockSpec((bt, w), lambda i, cb=cb: (rmap(i), cb)))
        args.append(arr)
    for arr in const_ins:
        in_specs.append(pl.BlockSpec(arr.shape, lambda i, nd=arr.ndim: (0,) * nd))
        args.append(arr)
    out_specs = [pl.BlockSpec((bt, w), lambda i: (rmap(i), 0)) for w, _ in row_outs]
    out_shape = [jax.ShapeDtypeStruct((t_dim, w), dt) for w, dt in row_outs]
    for shp in acc_outs:
        out_specs.append(pl.BlockSpec(shp, lambda i, nd=len(shp): (0,) * nd))
        out_shape.append(jax.ShapeDtypeStruct(shp, F32))
    return pl.pallas_call(
        body, name=name, grid=(nt,), in_specs=in_specs, out_specs=out_specs, out_shape=out_shape,
        scratch_shapes=list(scratch), compiler_params=_params("arbitrary"),
    )(*args)


def _first_step_zero(*refs):
    @pl.when(pl.program_id(0) == 0)
    def _():
        for r in refs:
            r[...] = jnp.zeros_like(r)


def _colsum(v):
    return jnp.sum(v, axis=0, keepdims=True)


def _headsum(v, e):
    parts = [jnp.dot(v[:, p * LANES:(p + 1) * LANES], e, precision=HI, preferred_element_type=F32)
             for p in range(v.shape[1] // LANES)]
    return parts[0] if len(parts) == 1 else jnp.concatenate(parts, axis=1)


def _shift_down(v, carry_ref):
    bt = v.shape[0]
    prev = pltpu.roll(v, 1, 0)
    row = lax.broadcasted_iota(jnp.int32, v.shape, 0)
    prev = jnp.where(row == 0, carry_ref[...], prev)
    carry_ref[...] = v[bt - 1:bt, :]
    return prev


def _pair_consts():
    lane = lax.broadcasted_iota(jnp.int32, (1, LANES), 1)
    m0 = (lane < HEAD).astype(F32)
    m1 = 1.0 - m0
    sub = lax.broadcasted_iota(jnp.int32, (HEAD, LANES), 0)
    lane2 = lax.broadcasted_iota(jnp.int32, (HEAD, LANES), 1)
    i0 = (lane2 == sub).astype(F32)
    i1 = (lane2 == sub + HEAD).astype(F32)
    return m0, m1, i0, i1


def _lanesum(v):
    return jnp.sum(v, axis=1, keepdims=True)


def _col_pair(row, i0, i1, m0, m1):
    return _lanesum(row * i0) * m0 + _lanesum(row * i1) * m1


def _row_pair(c0, c1, i0, i1):
    return _colsum(c0 * i0 + c1 * i1)


def _scan_fwd(r, w, k, v, kk, a, tc, npb):
    t_dim, c_dim = r.shape
    wb = LANES * npb
    tc = _pick(t_dim, tc, SUBLANES)

    def body(r_ref, w_ref, k_ref, v_ref, kk_ref, a_ref, y_ref, sall_ref, sfin_ref, s_ref):
        @pl.when(pl.program_id(1) == 0)
        def _():
            s_ref[...] = jnp.zeros_like(s_ref)

        m0, m1, i0, i1 = _pair_consts()

        sub8 = lax.broadcasted_iota(jnp.int32, (SUBLANES, LANES), 0)

        def group(gi, carry):
            base = pl.multiple_of(gi * SUBLANES, SUBLANES)
            rows = pl.ds(base, SUBLANES)
            out = []
            for q in range(npb):
                sl = slice(q * LANES, (q + 1) * LANES)
                s = carry[q]
                r8, w8, k8 = r_ref[rows, sl], w_ref[rows, sl], k_ref[rows, sl]
                v8, kk8, a8 = v_ref[rows, sl], kk_ref[rows, sl], a_ref[rows, sl]
                y8 = jnp.zeros((SUBLANES, LANES), F32)
                for j in range(SUBLANES):
                    one = slice(j, j + 1)
                    sall_ref[base + j, :, sl] = s
                    kkr = kk8[one]
                    sb = _lanesum(s * (kkr * m0)) * m0 + _lanesum(s * (kkr * m1)) * m1
                    vb = _col_pair(v8[one], i0, i1, m0, m1)
                    s = s * w8[one] - sb * (kkr * a8[one]) + vb * k8[one]
                    rr = r8[one]
                    yrow = _row_pair(_lanesum(s * (rr * m0)), _lanesum(s * (rr * m1)), i0, i1)
                    y8 = jnp.where(sub8 == j, yrow, y8)
                y_ref[rows, sl] = y8
                out.append(s)
            return tuple(out)

        init = tuple(s_ref[:, q * LANES:(q + 1) * LANES] for q in range(npb))
        fin = lax.fori_loop(0, tc // SUBLANES, group, init)
        for q in range(npb):
            s_ref[:, q * LANES:(q + 1) * LANES] = fin[q]
            sfin_ref[:, q * LANES:(q + 1) * LANES] = fin[q]

    row = pl.BlockSpec((tc, wb), lambda p, c: (c, p))
    return pl.pallas_call(
        body, name="rwkv_scan_fwd", grid=(c_dim // wb, t_dim // tc),
        in_specs=[row] * 6,
        out_specs=[row, pl.BlockSpec((tc, HEAD, wb), lambda p, c: (c, 0, p)), pl.BlockSpec((HEAD, wb), lambda p, c: (0, p))],
        out_shape=[jax.ShapeDtypeStruct((t_dim, c_dim), F32), jax.ShapeDtypeStruct((t_dim, HEAD, c_dim), F32),
                   jax.ShapeDtypeStruct((HEAD, c_dim), F32)],
        scratch_shapes=[pltpu.VMEM((HEAD, wb), F32)],
        compiler_params=_params("parallel", "arbitrary"),
    )(r, w, k, v, kk, a)


def _scan_bwd(r, w, k, v, kk, a, dy, sall, sfin, tc, npb):
    t_dim, c_dim = r.shape
    wb = LANES * npb
    tc = _pick(t_dim, tc, SUBLANES)
    nc = t_dim // tc

    def body(r_ref, w_ref, k_ref, v_ref, kk_ref, a_ref, dy_ref, sall_ref, sfin_ref,
             dr_ref, dw_ref, dk_ref, dv_ref, dkk_ref, da_ref, ds_ref, sn_ref):
        @pl.when(pl.program_id(1) == 0)
        def _():
            ds_ref[...] = jnp.zeros_like(ds_ref)
            sn_ref[...] = sfin_ref[...]

        m0, m1, i0, i1 = _pair_consts()

        sub8 = lax.broadcasted_iota(jnp.int32, (SUBLANES, LANES), 0)
        ng = tc // SUBLANES

        def group(gi, carry):
            base = pl.multiple_of((ng - 1 - gi) * SUBLANES, SUBLANES)
            rows = pl.ds(base, SUBLANES)
            out = []
            for q in range(npb):
                sl = slice(q * LANES, (q + 1) * LANES)
                ds, s_next = carry[q]
                r8, w8, k8, v8 = r_ref[rows, sl], w_ref[rows, sl], k_ref[rows, sl], v_ref[rows, sl]
                kk8, a8, dy8 = kk_ref[rows, sl], a_ref[rows, sl], dy_ref[rows, sl]
                zero8 = jnp.zeros((SUBLANES, LANES), F32)
                dr8, dw8, dk8, dv8, dkk8, da8 = zero8, zero8, zero8, zero8, zero8, zero8
                for j in reversed(range(SUBLANES)):
                    one = slice(j, j + 1)
                    here = sub8 == j
                    s_prev = sall_ref[base + j, :, sl]
                    kkr, ar, kr = kk8[one], a8[one], k8[one]
                    dyb = _col_pair(dy8[one], i0, i1, m0, m1)
                    vb = _col_pair(v8[one], i0, i1, m0, m1)
                    br = kkr * ar
                    dr8 = jnp.where(here, _colsum(s_next * dyb), dr8)
                    ds = ds + dyb * r8[one]
                    dk8 = jnp.where(here, _colsum(ds * vb), dk8)
                    dv8 = jnp.where(here, _row_pair(_lanesum(ds * (kr * m0)), _lanesum(ds * (kr * m1)), i0, i1), dv8)
                    dw8 = jnp.where(here, _colsum(ds * s_prev), dw8)
                    sb = _lanesum(s_prev * (kkr * m0)) * m0 + _lanesum(s_prev * (kkr * m1)) * m1
                    db = -_colsum(ds * sb)
                    dsb = -(_lanesum(ds * (br * m0)) * m0 + _lanesum(ds * (br * m1)) * m1)
                    dkk8 = jnp.where(here, _colsum(s_prev * dsb) + db * ar, dkk8)
                    da8 = jnp.where(here, db * kkr, da8)
                    ds = ds * w8[one] + dsb * kkr
                    s_next = s_prev
                dr_ref[rows, sl], dw_ref[rows, sl], dk_ref[rows, sl] = dr8, dw8, dk8
                dv_ref[rows, sl], dkk_ref[rows, sl], da_ref[rows, sl] = dv8, dkk8, da8
                out.append((ds, s_next))
            return tuple(out)

        init = tuple((ds_ref[:, q * LANES:(q + 1) * LANES], sn_ref[:, q * LANES:(q + 1) * LANES]) for q in range(npb))
        fin = lax.fori_loop(0, ng, group, init)
        for q in range(npb):
            ds_ref[:, q * LANES:(q + 1) * LANES] = fin[q][0]
            sn_ref[:, q * LANES:(q + 1) * LANES] = fin[q][1]

    row = pl.BlockSpec((tc, wb), lambda p, c: (nc - 1 - c, p))
    return pl.pallas_call(
        body, name="rwkv_scan_bwd", grid=(c_dim // wb, nc),
        in_specs=[row] * 7 + [pl.BlockSpec((tc, HEAD, wb), lambda p, c: (nc - 1 - c, 0, p)),
                              pl.BlockSpec((HEAD, wb), lambda p, c: (0, p))],
        out_specs=[row] * 6, out_shape=[jax.ShapeDtypeStruct((t_dim, c_dim), F32)] * 6,
        scratch_shapes=[pltpu.VMEM((HEAD, wb), F32), pltpu.VMEM((HEAD, wb), F32)],
        compiler_params=_params("parallel", "arbitrary"),
    )(r, w, k, v, kk, a, dy, sall, sfin)


def _fox_fwd(z, ct, cq, q_cb, k_cb, v_cb, n_pairs, blk):
    t_dim = z.shape[0]
    blk = _pick(t_dim, blk)
    nq = t_dim // blk
    scale = HEAD ** -0.5

    def body(q_ref, k_ref, v_ref, ct_ref, cq_ref, o_ref, lse_ref):
        i = pl.program_id(1)
        rowi = lax.broadcasted_iota(jnp.int32, (blk, blk), 0)
        coli = lax.broadcasted_iota(jnp.int32, (blk, blk), 1)
        for hh in range(2):
            hs = slice(hh * HEAD, (hh + 1) * HEAD)
            qh = q_ref[:, hs].astype(BF16)
            cqh = cq_ref[:, hh:hh + 1]

            def kv_step(j, carry, masked):
                m, l, acc = carry
                rows = pl.ds(pl.multiple_of(j * blk, blk), blk)
                kh = k_ref[rows, hs].astype(BF16)
                vh = v_ref[rows, hs].astype(BF16)
                ck = ct_ref[hh:hh + 1, rows]
                s = lax.dot_general(qh, kh, _DOT_DIMS["nt"], preferred_element_type=F32) * scale + (cqh - ck)
                if masked:
                    s = jnp.where(rowi >= coli, s, -jnp.inf)
                m_new = jnp.maximum(m, jnp.max(s, axis=1, keepdims=True))
                alpha = jnp.exp(m - m_new)
                pr = jnp.exp(s - m_new)
                l = l * alpha + jnp.sum(pr, axis=1, keepdims=True)
                acc = acc * alpha + jnp.dot(pr.astype(BF16), vh, preferred_element_type=F32)
                return m_new, l, acc

            init = (jnp.full((blk, 1), -jnp.inf, F32), jnp.zeros((blk, 1), F32), jnp.zeros((blk, HEAD), F32))
            carry = lax.fori_loop(0, i, functools.partial(kv_step, masked=False), init)
            m, l, acc = kv_step(i, carry, True)
            o_ref[:, hs] = acc / l
            lse_ref[:, hh:hh + 1] = m + jnp.log(l)

    full = lambda cb: pl.BlockSpec((t_dim, LANES), lambda p, i, cb=cb: (0, cb + p))
    return pl.pallas_call(
        body, name="fox_attn_fwd", grid=(n_pairs, nq),
        in_specs=[pl.BlockSpec((blk, LANES), lambda p, i: (i, q_cb + p)), full(k_cb), full(v_cb),
                  pl.BlockSpec((None, SUBLANES, t_dim), lambda p, i: (p, 0, 0)),
                  pl.BlockSpec((None, blk, 2), lambda p, i: (p, i, 0))],
        out_specs=[pl.BlockSpec((blk, LANES), lambda p, i: (i, p)), pl.BlockSpec((None, blk, 2), lambda p, i: (p, i, 0))],
        out_shape=[jax.ShapeDtypeStruct((t_dim, n_pairs * LANES), F32), jax.ShapeDtypeStruct((n_pairs, t_dim, 2), F32)],
        compiler_params=_params("parallel", "arbitrary"),
    )(z, z, z, ct, cq)


def _fox_rowdot(z, ct, cq, do, lse, q_cb, k_cb, v_cb, n_pairs, blk):
    t_dim = z.shape[0]
    blk = _pick(t_dim, blk)
    scale = HEAD ** -0.5

    def body(q_ref, k_ref, v_ref, ct_ref, cq_ref, do_ref, lse_ref, out_ref):
        i = pl.program_id(1)
        rowi = lax.broadcasted_iota(jnp.int32, (blk, blk), 0)
        coli = lax.broadcasted_iota(jnp.int32, (blk, blk), 1)
        for hh in range(2):
            hs = slice(hh * HEAD, (hh + 1) * HEAD)
            qh = q_ref[:, hs].astype(BF16)
            dob = do_ref[:, hs].astype(BF16)

            def kv_step(j, carry, masked):
                num, den = carry
                rows = pl.ds(pl.multiple_of(j * blk, blk), blk)
                kh = k_ref[rows, hs].astype(BF16)
                vh = v_ref[rows, hs].astype(BF16)
                ck = ct_ref[hh:hh + 1, rows]
                s = lax.dot_general(qh, kh, _DOT_DIMS["nt"], preferred_element_type=F32) * scale + (cq_ref[:, hh:hh + 1] - ck)
                pr = jnp.exp(s - lse_ref[:, hh:hh + 1])
                if masked:
                    pr = jnp.where(rowi >= coli, pr, 0.0)
                dp = lax.dot_general(dob, vh, _DOT_DIMS["nt"], preferred_element_type=F32)
                return num + jnp.sum(pr * dp, axis=1, keepdims=True), den + jnp.sum(pr, axis=1, keepdims=True)

            zero = jnp.zeros((blk, 1), F32)
            carry = lax.fori_loop(0, i, functools.partial(kv_step, masked=False), (zero, zero))
            num, den = kv_step(i, carry, True)
            out_ref[:, hh:hh + 1] = num / den

    full = lambda cb: pl.BlockSpec((t_dim, LANES), lambda p, i, cb=cb: (0, cb + p))
    return pl.pallas_call(
        body, name="fox_attn_rowdot", grid=(n_pairs, t_dim // blk),
        in_specs=[pl.BlockSpec((blk, LANES), lambda p, i: (i, q_cb + p)), full(k_cb), full(v_cb),
                  pl.BlockSpec((None, SUBLANES, t_dim), lambda p, i: (p, 0, 0)),
                  pl.BlockSpec((None, blk, 2), lambda p, i: (p, i, 0)),
                  pl.BlockSpec((blk, LANES), lambda p, i: (i, p)),
                  pl.BlockSpec((None, blk, 2), lambda p, i: (p, i, 0))],
        out_specs=pl.BlockSpec((None, blk, 2), lambda p, i: (p, i, 0)),
        out_shape=jax.ShapeDtypeStruct((n_pairs, t_dim, 2), F32),
        compiler_params=_params("parallel", "arbitrary"),
    )(z, z, z, ct, cq, do, lse)


def _fox_bwd(z, ct, cq, rowdot, do, lse, q_cb, k_cb, v_cb, n_pairs, blk):
    t_dim = z.shape[0]
    blk = _pick(t_dim, blk)
    nb = t_dim // blk
    scale = HEAD ** -0.5

    def body(q_ref, k_ref, v_ref, ct_ref, cq_ref, rd_ref, do_ref, lse_ref, dq_ref, dk_ref, dv_ref, dc_ref):
        j = pl.program_id(1)

        @pl.when(j == 0)
        def _():
            dq_ref[...] = jnp.zeros_like(dq_ref)

        rowi = lax.broadcasted_iota(jnp.int32, (blk, blk), 0)
        coli = lax.broadcasted_iota(jnp.int32, (blk, blk), 1)
        krows = pl.ds(pl.multiple_of(j * blk, blk), blk)
        dc_rows = []
        for hh in range(2):
            hs = slice(hh * HEAD, (hh + 1) * HEAD)
            kh = k_ref[:, hs].astype(BF16)
            vh = v_ref[:, hs].astype(BF16)
            ck = ct_ref[hh:hh + 1, krows]

            def q_step(i, carry, masked):
                dk, dv, dc = carry
                rows = pl.ds(pl.multiple_of(i * blk, blk), blk)
                qh = q_ref[rows, hs].astype(BF16)
                dob = do_ref[rows, hs].astype(BF16)
                dcol = rd_ref[rows, hh:hh + 1]
                s = lax.dot_general(qh, kh, _DOT_DIMS["nt"], preferred_element_type=F32) * scale + (cq_ref[rows, hh:hh + 1] - ck)
                pr = jnp.exp(s - lse_ref[rows, hh:hh + 1])
                if masked:
                    pr = jnp.where(rowi >= coli, pr, 0.0)
                dv = dv + lax.dot_general(pr.astype(BF16), dob, _DOT_DIMS["tn"], preferred_element_type=F32)
                dp = lax.dot_general(dob, vh, _DOT_DIMS["nt"], preferred_element_type=F32)
                ds = pr * (dp - dcol)
                dsb = ds.astype(BF16)
                dq_ref[rows, hs] += jnp.dot(dsb, kh, preferred_element_type=F32) * scale
                dk = dk + lax.dot_general(dsb, qh, _DOT_DIMS["tn"], preferred_element_type=F32) * scale
                dc = dc - _colsum(ds)
                return dk, dv, dc

            init = (jnp.zeros((blk, HEAD), F32), jnp.zeros((blk, HEAD), F32), jnp.zeros((1, blk), F32))
            carry = q_step(j, init, True)
            dk, dv, dc = lax.fori_loop(j + 1, nb, functools.partial(q_step, masked=False), carry)
            dk_ref[:, hs] = dk
            dv_ref[:, hs] = dv
            dc_rows.append(dc)
        dc_ref[0:1, :] = dc_rows[0]
        dc_ref[1:2, :] = dc_rows[1]
        dc_ref[2:SUBLANES, :] = jnp.zeros((SUBLANES - 2, blk), F32)

    full = lambda: pl.BlockSpec((t_dim, LANES), lambda p, j: (0, p))
    blkspec = pl.BlockSpec((blk, LANES), lambda p, j: (j, p))
    return pl.pallas_call(
        body, name="fox_attn_bwd", grid=(n_pairs, nb),
        in_specs=[pl.BlockSpec((t_dim, LANES), lambda p, j: (0, q_cb + p)),
                  pl.BlockSpec((blk, LANES), lambda p, j: (j, k_cb + p)),
                  pl.BlockSpec((blk, LANES), lambda p, j: (j, v_cb + p)),
                  pl.BlockSpec((None, SUBLANES, t_dim), lambda p, j: (p, 0, 0)),
                  pl.BlockSpec((None, t_dim, 2), lambda p, j: (p, 0, 0)),
                  pl.BlockSpec((None, t_dim, 2), lambda p, j: (p, 0, 0)), full(),
                  pl.BlockSpec((None, t_dim, 2), lambda p, j: (p, 0, 0))],
        out_specs=[full(), blkspec, blkspec, pl.BlockSpec((None, SUBLANES, blk), lambda p, j: (p, 0, j))],
        out_shape=[jax.ShapeDtypeStruct((t_dim, n_pairs * LANES), F32)] * 3
                  + [jax.ShapeDtypeStruct((n_pairs, SUBLANES, t_dim), F32)],
        compiler_params=_params("parallel", "arbitrary"),
    )(z, z, z, ct, cq, rowdot, do, lse)


HBM_SPEC = pl.BlockSpec(memory_space=pltpu.HBM)


def _all_gather(shard, name):
    rows, width = shard.shape

    def body(x_ref, out_ref, send_sems, recv_sems, local_sem):
        x, y, c = lax.axis_index("x"), lax.axis_index("y"), lax.axis_index("c")
        me, sibling = (x, y, c), (x, y, 1 - c)
        chips = [(1 - x, y), (x, 1 - y), (1 - x, 1 - y)]

        def slot(px, py, pc):
            return out_ref.at[4 * px + 2 * py + pc]

        def copy(k, block, to, src=None):
            return pltpu.make_async_remote_copy(
                src_ref=slot(*block) if src is None else src, dst_ref=slot(*block),
                send_sem=send_sems.at[k], recv_sem=recv_sems.at[k], device_id=to, device_id_type=MESH)

        mine = pltpu.make_async_copy(x_ref, slot(*me), local_sem)
        mine.start()
        first = [copy(0, me, sibling, src=x_ref)]
        first += [copy(1 + j, me, (*chip, c), src=x_ref) for j, chip in enumerate(chips)]
        for cp in first:
            cp.start()
        passed = [copy(4 + j, (*chip, c), sibling) for j, chip in enumerate(chips)]
        for j, chip in enumerate(chips):
            copy(1 + j, (*chip, c), me).wait_recv()
            passed[j].start()
        copy(0, sibling, me).wait_recv()
        for j, chip in enumerate(chips):
            copy(4 + j, (*chip, 1 - c), me).wait_recv()
        for cp in first + passed:
            cp.wait_send()
        mine.wait()

    return pl.pallas_call(
        body, name=name, out_shape=jax.ShapeDtypeStruct((NDEV, rows, width), shard.dtype),
        in_specs=[HBM_SPEC], out_specs=HBM_SPEC,
        scratch_shapes=[pltpu.SemaphoreType.DMA((7,)), pltpu.SemaphoreType.DMA((7,)), pltpu.SemaphoreType.DMA],
    )(shard)


def _grad_exchange(parts, small, name):
    def body(g_ref, s_ref, recv_ref, sall_ref, send_sems, recv_sems, local_sems):
        x, y, c = lax.axis_index("x"), lax.axis_index("y"), lax.axis_index("c")
        me = 4 * x + 2 * y + c
        own = [pltpu.make_async_copy(g_ref.at[me], recv_ref.at[me], local_sems.at[0]),
               pltpu.make_async_copy(s_ref, sall_ref.at[me], local_sems.at[1])]
        for cp in own:
            cp.start()
        sends, recvs = [], []
        for k in range(1, NDEV):
            px = 1 - x if k & 4 else x
            py = 1 - y if k & 2 else y
            pc = 1 - c if k & 1 else c
            peer = 4 * px + 2 * py + pc
            dev = (px, py, pc)
            big = dict(send_sem=send_sems.at[k - 1], recv_sem=recv_sems.at[k - 1], device_id=dev, device_id_type=MESH)
            sml = dict(send_sem=send_sems.at[6 + k], recv_sem=recv_sems.at[6 + k], device_id=dev, device_id_type=MESH)
            sends.append(pltpu.make_async_remote_copy(src_ref=g_ref.at[peer], dst_ref=recv_ref.at[me], **big))
            sends.append(pltpu.make_async_remote_copy(src_ref=s_ref, dst_ref=sall_ref.at[me], **sml))
            recvs.append(pltpu.make_async_remote_copy(src_ref=g_ref.at[peer], dst_ref=recv_ref.at[peer], **big))
            recvs.append(pltpu.make_async_remote_copy(src_ref=s_ref, dst_ref=sall_ref.at[peer], **sml))
        for cp in sends:
            cp.start()
        for cp in recvs:
            cp.wait_recv()
        for cp in sends:
            cp.wait_send()
        for cp in own:
            cp.wait()

    return pl.pallas_call(
        body, name=name,
        out_shape=[jax.ShapeDtypeStruct(parts.shape, parts.dtype), jax.ShapeDtypeStruct((NDEV,) + small.shape, small.dtype)],
        in_specs=[HBM_SPEC, HBM_SPEC], out_specs=[HBM_SPEC, HBM_SPEC],
        scratch_shapes=[pltpu.SemaphoreType.DMA((14,)), pltpu.SemaphoreType.DMA((14,)), pltpu.SemaphoreType.DMA((2,))],
    )(parts, small)


def _adamw(partials, w, m, v, name):
    rows, width = w.shape
    br = _pick(rows, 256, SUBLANES)

    def body(p_ref, w_ref, m_ref, v_ref, g_out, d_out, m_out, v_out):
        g = p_ref[0]
        for d in range(1, NDEV):
            g = g + p_ref[d]
        mn = ADAM_B1 * m_ref[...] + (1.0 - ADAM_B1) * g
        vn = ADAM_B2 * v_ref[...] + (1.0 - ADAM_B2) * jnp.square(g)
        m_hat = mn / (1.0 - ADAM_B1 ** ADAM_STEP)
        v_hat = vn / (1.0 - ADAM_B2 ** ADAM_STEP)
        g_out[...] = g
        d_out[...] = -ADAM_LR * (m_hat / (jnp.sqrt(v_hat) + ADAM_EPS) + ADAM_WD * w_ref[...])
        m_out[...] = mn
        v_out[...] = vn

    blk = pl.BlockSpec((br, width), lambda i: (i, 0))
    return pl.pallas_call(
        body, name=name, grid=(rows // br,),
        in_specs=[pl.BlockSpec((NDEV, br, width), lambda i: (0, i, 0)), blk, blk, blk],
        out_specs=[blk] * 4, out_shape=[jax.ShapeDtypeStruct((rows, width), F32)] * 4,
        compiler_params=_params("parallel"),
    )(partials, w, m, v)


def _pack_rows(flat, lead=()):
    n = flat.shape[-1]
    padded = _round_up(n, PACK_ALIGN)
    flat = jnp.pad(flat, [(0, 0)] * len(lead) + [(0, padded - n)])
    return flat.reshape(lead + (padded // PACK_W, PACK_W))


def _split_shards(full, axis):
    shp = full.shape
    blocks = full.reshape(shp[:axis] + (NDEV, shp[axis] // NDEV) + shp[axis + 1:])
    return jnp.moveaxis(blocks, axis, 0).reshape(NDEV, -1)


def _join_shards(flat, shard_shape, axis):
    blocks = jnp.moveaxis(flat.reshape((NDEV,) + shard_shape), 0, axis)
    shp = blocks.shape
    return blocks.reshape(shp[:axis] + (NDEV * shp[axis + 1],) + shp[axis + 2:])


SHARDED = (("w_in", 1), ("rw_w_lora_up", 1), ("rw_a_lora_up", 1), ("w_up_rwkv", 1), ("w_up_fox", 1),
           ("w_out", 0), ("ple_proj", 1), ("ple_gate_w", 0))
REPLICATED = ("norm_g", "rw_shift_mu", "rw_w0", "rw_a0", "rw_k_k", "rw_k_a", "rw_r_k", "rw_ln_g", "rw_ln_b",
              "fox_b_f", "ple_norm_g", "final_norm_g")
WEIGHTS = ("norm_g", "w_in", "rw_shift_mu", "rw_w0", "rw_w_lora_up", "rw_a0", "rw_a_lora_up", "rw_k_k", "rw_k_a",
           "rw_r_k", "rw_ln_g", "rw_ln_b", "fox_b_f", "w_up_rwkv", "w_up_fox", "w_out", "ple_proj", "ple_gate_w",
           "ple_norm_g", "final_norm_g")


def _local_step(x, p, tgt, wz, wl, wa, wur, wuf, wo, pp, pg, rep, dims):
    t_dim, d_model, c_rw, lora, c_fox, h_fox, sec = dims
    bt = _pick(t_dim, 128, SUBLANES)
    n_pairs = c_fox // LANES
    row = lambda a: a.reshape(1, -1)
    norm_g, mu, w0, a0 = row(rep["norm_g"]), row(rep["rw_shift_mu"]), row(rep["rw_w0"]), row(rep["rw_a0"])
    k_k, k_a, r_k = row(rep["rw_k_k"]), row(rep["rw_k_a"]), row(rep["rw_r_k"])
    ln_g, ln_b = row(rep["rw_ln_g"]), row(rep["rw_ln_b"])
    g2, g3 = row(rep["ple_norm_g"]), row(rep["final_norm_g"])
    b_f = jnp.pad(row(rep["fox_b_f"]), ((0, 0), (0, LANES - h_fox)))
    lane = jnp.arange(LANES)
    e_head = (lane[:, None] // HEAD == lane[None, :] // HEAD).astype(F32)
    c4 = 4 * c_rw
    inv_d = 1.0 / d_model
    decay_k = math.exp(-0.5)

    def norm_in(x_ref, g_ref, h_ref):
        xv = x_ref[...]
        rms = lax.rsqrt(jnp.mean(xv * xv, axis=-1, keepdims=True) + NORM_EPS)
        h_ref[...] = (xv * rms * g_ref[...]).astype(BF16)

    (h,) = _rowcall(norm_in, "norm_in", t_dim, bt, [x], [norm_g], [(d_model, BF16)])
    z = _matmul(h, wz, "nn", "proj_in", bn=1408)

    def rw_values(zs, w0_ref, wl_ref, a0_ref, wa_ref, kk_ref, ka_ref, e_ref):
        k = zs[:, c_rw:2 * c_rw]
        tw = jnp.tanh(zs[:, c4:c4 + lora])
        al = zs[:, c4 + lora:c4 + 2 * lora]
        sw = _sigmoid(w0_ref[...] + jnp.dot(tw.astype(BF16), wl_ref[...], preferred_element_type=F32))
        decay = jnp.exp(-decay_k * sw)
        a = _sigmoid(a0_ref[...] + jnp.dot(al.astype(BF16), wa_ref[...], preferred_element_type=F32))
        kk0 = k * kk_ref[...]
        nrm = jnp.sqrt(_headsum(kk0 * kk0, e_ref[...]))
        inv = 1.0 / jnp.maximum(nrm, 1e-12)
        k2 = k * (1.0 + (a - 1.0) * ka_ref[...])
        return k, tw, al, sw, decay, a, kk0, nrm, inv, k2

    def rw_prep(z_ref, mu_ref, w0_ref, wl_ref, a0_ref, wa_ref, kk_ref, ka_ref, e_ref,
                r_o, w_o, k_o, v_o, kk_o, a_o, g_o, carry):
        _first_step_zero(carry)
        zv = z_ref[...]
        zs = zv + (_shift_down(zv, carry) - zv) * mu_ref[...]
        k, tw, al, sw, decay, a, kk0, nrm, inv, k2 = rw_values(zs, w0_ref, wl_ref, a0_ref, wa_ref, kk_ref, ka_ref, e_ref)
        r_o[...] = zs[:, 0:c_rw]
        w_o[...] = decay
        k_o[...] = k2
        v_o[...] = zs[:, 2 * c_rw:3 * c_rw]
        kk_o[...] = kk0 * inv
        a_o[...] = a
        g_o[...] = zs[:, 3 * c_rw:c4]

    rw_consts = [mu, w0, wl, a0, wa, k_k, k_a, e_head]
    r_s, w_s, k_s, v_s, kk_s, a_s, g_s = _rowcall(
        rw_prep, "rwkv_prep", t_dim, bt, [(z, 0, sec)], rw_consts, [(c_rw, F32)] * 7,
        scratch=[pltpu.VMEM((1, sec), F32)])
    npb = 2 if c_rw % (2 * LANES) == 0 else 1
    y_s, s_all, s_fin = _scan_fwd(r_s, w_s, k_s, v_s, kk_s, a_s, 128, npb)

    def rw_post_values(y, r, k2, v, g, lng_ref, lnb_ref, rk_ref, e):
        mean = _headsum(y, e) * (1.0 / HEAD)
        d = y - mean
        rstd = lax.rsqrt(_headsum(d * d, e) * (1.0 / HEAD) + GN_EPS)
        yh = d * rstd
        rk = _headsum(r * k2 * rk_ref[...], e)
        yo = yh * lng_ref[...] + lnb_ref[...] + rk * v
        sg = _sigmoid(g)
        return rstd, yh, rk, yo, sg

    def rw_post(y_ref, r_ref, k_ref, v_ref, g_ref, lng_ref, lnb_ref, rk_ref, e_ref, out_ref):
        g = g_ref[...]
        _, _, _, yo, sg = rw_post_values(y_ref[...], r_ref[...], k_ref[...], v_ref[...], g, lng_ref, lnb_ref, rk_ref, e_ref[...])
        out_ref[...] = (yo * g * sg).astype(BF16)

    (y_rw,) = _rowcall(rw_post, "rwkv_post", t_dim, bt, [y_s, r_s, k_s, v_s, g_s], [ln_g, ln_b, r_k, e_head], [(c_rw, BF16)])

    fl_cb = (sec + 4 * c_fox) // LANES
    hp = _round_up(h_fox, SUBLANES)
    bt_c = _pick(t_dim, 256)
    tri = (jnp.arange(bt_c)[:, None] >= jnp.arange(bt_c)[None, :]).astype(F32)

    rows8 = n_pairs * SUBLANES
    pair_rows = (jnp.arange(rows8)[:, None] // SUBLANES * 2 + jnp.arange(rows8)[:, None] % SUBLANES
                 == jnp.arange(LANES)[None, :]) & (jnp.arange(rows8)[:, None] % SUBLANES < 2)
    pair_rows = pair_rows.astype(F32)

    def fox_decay(fl_ref, bf_ref, tri_ref, sel_ref, ct_ref, cq_ref, carry):
        _first_step_zero(carry)
        lf = _log_sigmoid(fl_ref[...] + bf_ref[...])
        c = jnp.dot(tri_ref[...], lf, precision=HI, preferred_element_type=F32) + carry[...]
        carry[...] = c[bt_c - 1:bt_c, :]
        ct = jnp.dot(sel_ref[...], jnp.transpose(c), precision=HI, preferred_element_type=F32)
        ct_ref[...] = ct.reshape(n_pairs, SUBLANES, bt_c)
        for pair in range(n_pairs):
            cq_ref[pair] = c[:, 2 * pair:2 * pair + 2]

    ct, cq = pl.pallas_call(
        fox_decay, name="fox_decay", grid=(t_dim // bt_c,),
        in_specs=[pl.BlockSpec((bt_c, LANES), lambda i: (i, fl_cb)), pl.BlockSpec((1, LANES), lambda i: (0, 0)),
                  pl.BlockSpec((bt_c, bt_c), lambda i: (0, 0)), pl.BlockSpec((rows8, LANES), lambda i: (0, 0))],
        out_specs=[pl.BlockSpec((n_pairs, SUBLANES, bt_c), lambda i: (0, 0, i)),
                   pl.BlockSpec((n_pairs, bt_c, 2), lambda i: (0, i, 0))],
        out_shape=[jax.ShapeDtypeStruct((n_pairs, SUBLANES, t_dim), F32), jax.ShapeDtypeStruct((n_pairs, t_dim, 2), F32)],
        scratch_shapes=[pltpu.VMEM((1, LANES), F32)], compiler_params=_params("arbitrary"),
    )(z, b_f, tri, pair_rows)
    q_cb = sec // LANES
    k_cb, v_cb = q_cb + n_pairs, q_cb + 2 * n_pairs
    o_fox, lse = _fox_fwd(z, ct, cq, q_cb, k_cb, v_cb, n_pairs, 256)

    def fox_post(o_ref, z_ref, out_ref):
        g = z_ref[:, 3 * c_fox:4 * c_fox]
        out_ref[...] = (o_ref[...] * g * _sigmoid(g)).astype(BF16)

    (y_fox,) = _rowcall(fox_post, "fox_post", t_dim, bt, [o_fox, (z, 1, sec)], [], [(c_fox, BF16)])

    u_rw = _matmul(y_rw, wur, "nn", "up_rwkv")
    u_fox = _matmul(y_fox, wuf, "nn", "up_fox")

    def merge(ur_ref, uf_ref, z_ref, out_ref):
        s1 = _sigmoid(z_ref[:, 0:d_model])
        s2 = _sigmoid(z_ref[:, d_model:2 * d_model])
        out_ref[...] = (s1 * ur_ref[...] + s2 * uf_ref[...]).astype(BF16)

    (merged,) = _rowcall(merge, "merge", t_dim, bt, [u_rw, u_fox, (z, 2, sec)], [], [(d_model, BF16)])
    mo = _matmul(merged, wo, "nn", "proj_out")

    def resid_norm(x_ref, mo_ref, g_ref, x1_ref, n2_ref):
        x1 = x_ref[...] + mo_ref[...]
        rms = lax.rsqrt(jnp.mean(x1 * x1, axis=-1, keepdims=True) + NORM_EPS)
        x1_ref[...] = x1
        n2_ref[...] = (x1 * rms * g_ref[...]).astype(BF16)

    x1, n2 = _rowcall(resid_norm, "resid_norm", t_dim, bt, [x, mo], [g2], [(d_model, F32), (d_model, BF16)])
    ple = _matmul(p, pp, "nn", "ple_proj")
    gl = _matmul(n2, pg, "nn", "ple_gate")

    def head(x1_ref, ple_ref, gl_ref, tgt_ref, g_ref, dx2_ref, dple_ref, dgl_ref, loss_ref, dg3_ref):
        _first_step_zero(loss_ref, dg3_ref)
        sg = _sigmoid(gl_ref[...])
        pl_v = ple_ref[...]
        x2 = x1_ref[...] + pl_v * sg
        rms = lax.rsqrt(jnp.mean(x2 * x2, axis=-1, keepdims=True) + NORM_EPS)
        xn = x2 * rms
        diff = xn * g_ref[...] - tgt_ref[...]
        loss_ref[...] += 0.5 * jnp.sum(jnp.mean(diff * diff, axis=-1, keepdims=True), axis=0, keepdims=True)
        dyf = diff * inv_d
        dg3_ref[...] += _colsum(dyf * xn)
        gy = dyf * g_ref[...]
        dx2 = rms * (gy - xn * jnp.mean(xn * gy, axis=-1, keepdims=True))
        dx2_ref[...] = dx2
        dple_ref[...] = (dx2 * sg).astype(BF16)
        dgl_ref[...] = (dx2 * pl_v * sg * (1.0 - sg)).astype(BF16)

    dx2, dple, dgl, loss, d_g3 = _rowcall(
        head, "head", t_dim, bt, [x1, ple, gl, tgt], [g3], [(d_model, F32), (d_model, BF16), (d_model, BF16)],
        acc_outs=[(1, 1), (1, d_model)])

    d_pp = _matmul(p, dple, "tn", "d_ple_proj")
    d_pg = _matmul(n2, dgl, "tn", "d_ple_gate")
    dn2 = _matmul(dgl, pg, "nt", "d_n2")

    def resid_norm_bwd(dx2_ref, dn2_ref, x1_ref, g_ref, dx1_ref, dx1b_ref, dg2_ref):
        _first_step_zero(dg2_ref)
        x1 = x1_ref[...]
        rms = lax.rsqrt(jnp.mean(x1 * x1, axis=-1, keepdims=True) + NORM_EPS)
        xn = x1 * rms
        dn = dn2_ref[...]
        dg2_ref[...] += _colsum(dn * xn)
        gy = dn * g_ref[...]
        dx1 = dx2_ref[...] + rms * (gy - xn * jnp.mean(xn * gy, axis=-1, keepdims=True))
        dx1_ref[...] = dx1
        dx1b_ref[...] = dx1.astype(BF16)

    dx1, dx1b, d_g2 = _rowcall(resid_norm_bwd, "resid_norm_bwd", t_dim, bt, [dx2, dn2, x1], [g2],
                               [(d_model, F32), (d_model, BF16)], acc_outs=[(1, d_model)])
    d_wo = _matmul(merged, dx1b, "tn", "d_w_out")
    dmerged = _matmul(dx1b, wo, "nt", "d_merged")

    def merge_bwd(dm_ref, ur_ref, uf_ref, z_ref, dur_ref, duf_ref, dzg_ref):
        dm = dm_ref[...]
        s1 = _sigmoid(z_ref[:, 0:d_model])
        s2 = _sigmoid(z_ref[:, d_model:2 * d_model])
        dur_ref[...] = (dm * s1).astype(BF16)
        duf_ref[...] = (dm * s2).astype(BF16)
        dzg_ref[:, 0:d_model] = (dm * ur_ref[...] * s1 * (1.0 - s1)).astype(BF16)
        dzg_ref[:, d_model:2 * d_model] = (dm * uf_ref[...] * s2 * (1.0 - s2)).astype(BF16)
        if sec > 2 * d_model:
            dzg_ref[:, 2 * d_model:sec] = jnp.zeros((dm.shape[0], sec - 2 * d_model), BF16)

    du_rw, du_fox, dz_gate = _rowcall(merge_bwd, "merge_bwd", t_dim, bt, [dmerged, u_rw, u_fox, (z, 2, sec)], [],
                                      [(d_model, BF16), (d_model, BF16), (sec, BF16)])
    d_wur = _matmul(y_rw, du_rw, "tn", "d_w_up_rwkv")
    d_wuf = _matmul(y_fox, du_fox, "tn", "d_w_up_fox")
    dy_rw = _matmul(du_rw, wur, "nt", "d_y_rwkv")
    dy_fox = _matmul(du_fox, wuf, "nt", "d_y_fox")

    def fox_post_bwd(dy_ref, o_ref, z_ref, do_ref, dg_ref):
        g = z_ref[:, 3 * c_fox:4 * c_fox]
        sg = _sigmoid(g)
        dy = dy_ref[...]
        do_ref[...] = dy * g * sg
        dg_ref[...] = (dy * o_ref[...] * sg * (1.0 + g * (1.0 - sg))).astype(BF16)

    do_fox, dg_fox = _rowcall(fox_post_bwd, "fox_post_bwd", t_dim, bt, [dy_fox, o_fox, (z, 1, sec)], [],
                              [(c_fox, F32), (c_fox, BF16)])
    rowdot = _fox_rowdot(z, ct, cq, do_fox, lse, q_cb, k_cb, v_cb, n_pairs, 256)
    dq_f, dk_f, dv_f, dc_t = _fox_bwd(z, ct, cq, rowdot, do_fox, lse, q_cb, k_cb, v_cb, n_pairs, 256)
    sel = (jnp.arange(hp)[:, None] // 2 * SUBLANES + jnp.arange(hp)[:, None] % 2 == jnp.arange(rows8)[None, :]).astype(F32)
    tri_rev = (jnp.arange(bt_c)[:, None] >= jnp.arange(bt_c)[None, :]).astype(F32)
    bf_col = b_f.reshape(LANES, 1)[0:hp]
    nbc = t_dim // bt_c

    def fox_decay_bwd(dc_ref, fl_ref, sel_ref, tri_ref, bf_ref, dfl_ref, dbf_ref, carry):
        _first_step_zero(carry, dbf_ref)
        dc = jnp.dot(sel_ref[...], dc_ref[...].reshape(rows8, bt_c), precision=HI, preferred_element_type=F32)
        dlf = jnp.dot(dc, tri_ref[...], precision=HI, preferred_element_type=F32) + carry[...]
        carry[...] = dlf[:, 0:1]
        flt = jnp.transpose(fl_ref[...])[0:hp, :]
        dfl = dlf * _sigmoid(-(flt + bf_ref[...]))
        head_row = lax.broadcasted_iota(jnp.int32, (hp, bt_c), 0)
        dfl = jnp.where(head_row < h_fox, dfl, 0.0)
        dbf_ref[...] += jnp.sum(dfl, axis=1, keepdims=True)
        full = jnp.concatenate([dfl, jnp.zeros((LANES - hp, bt_c), F32)], axis=0) if hp < LANES else dfl
        dfl_ref[...] = jnp.transpose(full).astype(BF16)

    dz_fl, d_bf = pl.pallas_call(
        fox_decay_bwd, name="fox_decay_bwd", grid=(nbc,),
        in_specs=[pl.BlockSpec((n_pairs, SUBLANES, bt_c), lambda i: (0, 0, nbc - 1 - i)),
                  pl.BlockSpec((bt_c, LANES), lambda i: (nbc - 1 - i, fl_cb)),
                  pl.BlockSpec(sel.shape, lambda i: (0, 0)), pl.BlockSpec((bt_c, bt_c), lambda i: (0, 0)),
                  pl.BlockSpec((hp, 1), lambda i: (0, 0))],
        out_specs=[pl.BlockSpec((bt_c, LANES), lambda i: (nbc - 1 - i, 0)), pl.BlockSpec((hp, 1), lambda i: (0, 0))],
        out_shape=[jax.ShapeDtypeStruct((t_dim, LANES), BF16), jax.ShapeDtypeStruct((hp, 1), F32)],
        scratch_shapes=[pltpu.VMEM((hp, 1), F32)], compiler_params=_params("arbitrary"),
    )(dc_t, z, sel, tri_rev, bf_col)

    def rw_post_bwd(dy_ref, y_ref, r_ref, k_ref, v_ref, g_ref, lng_ref, lnb_ref, rk_ref, e_ref,
                    dg_ref, dys_ref, dr_ref, dk_ref, dv_ref, dlng_ref, dlnb_ref, drk_ref):
        _first_step_zero(dlng_ref, dlnb_ref, drk_ref)
        e = e_ref[...]
        dy, r, k2, v, g = dy_ref[...], r_ref[...], k_ref[...], v_ref[...], g_ref[...]
        rstd, yh, rk, yo, sg = rw_post_values(y_ref[...], r, k2, v, g, lng_ref, lnb_ref, rk_ref, e)
        dg_ref[...] = dy * yo * sg * (1.0 + g * (1.0 - sg))
        dyo = dy * g * sg
        dlnb_ref[...] += _colsum(dyo)
        dlng_ref[...] += _colsum(dyo * yh)
        dyh = dyo * lng_ref[...]
        dys_ref[...] = rstd * (dyh - _headsum(dyh, e) * (1.0 / HEAD) - yh * _headsum(dyh * yh, e) * (1.0 / HEAD))
        drk = _headsum(dyo * v, e)
        dv_ref[...] = dyo * rk
        dr_ref[...] = drk * k2 * rk_ref[...]
        dk_ref[...] = drk * r * rk_ref[...]
        drk_ref[...] += _colsum(drk * r * k2)

    dg_rw, dy_s, dr_b, dk_b, dv_b, d_lng, d_lnb, d_rk = _rowcall(
        rw_post_bwd, "rwkv_post_bwd", t_dim, bt, [dy_rw, y_s, r_s, k_s, v_s, g_s], [ln_g, ln_b, r_k, e_head],
        [(c_rw, F32)] * 5, acc_outs=[(1, c_rw)] * 3)
    dr_c, dw_c, dk_c, dv_c, dkk_c, da_c = _scan_bwd(r_s, w_s, k_s, v_s, kk_s, a_s, dy_s, s_all, s_fin, 128, npb)

    def rw_prep_bwd(z_ref, dr1, dr2, dw_ref, dk1, dk2_ref, dv1, dv2, dkk_ref, da_ref, dg_ref,
                    mu_ref, w0_ref, wl_ref, a0_ref, wa_ref, kk_ref, ka_ref, e_ref,
                    dzs_ref, tw_ref, al_ref, dwr_ref, dar_ref, dmu_ref, dw0_ref, da0_ref, dkk_acc, dka_acc, carry):
        _first_step_zero(carry, dmu_ref, dw0_ref, da0_ref, dkk_acc, dka_acc)
        e = e_ref[...]
        zv = z_ref[...]
        zp = _shift_down(zv, carry)
        zs = zv + (zp - zv) * mu_ref[...]
        k, tw, al, sw, decay, a, kk0, nrm, inv, k2 = rw_values(zs, w0_ref, wl_ref, a0_ref, wa_ref, kk_ref, ka_ref, e_ref)
        dk2 = dk1[...] + dk2_ref[...]
        da = da_ref[...] + dk2 * k * ka_ref[...]
        dk = dk2 * (1.0 + (a - 1.0) * ka_ref[...])
        dka_acc[...] += _colsum(dk2 * k * (a - 1.0))
        kk = kk0 * inv
        dkk = dkk_ref[...]
        dkk0 = inv * jnp.where(nrm > 1e-12, dkk - kk * _headsum(dkk * kk, e), dkk)
        dk = dk + dkk0 * kk_ref[...]
        dkk_acc[...] += _colsum(dkk0 * k)
        da_raw = da * a * (1.0 - a)
        da0_ref[...] += _colsum(da_raw)
        dw_raw = dw_ref[...] * decay * (-decay_k) * sw * (1.0 - sw)
        dw0_ref[...] += _colsum(dw_raw)
        dar_b, dwr_b = da_raw.astype(BF16), dw_raw.astype(BF16)
        dal = lax.dot_general(dar_b, wa_ref[...], _DOT_DIMS["nt"], preferred_element_type=F32)
        dtw = lax.dot_general(dwr_b, wl_ref[...], _DOT_DIMS["nt"], preferred_element_type=F32)
        dzs_ref[:, 0:c_rw] = dr1[...] + dr2[...]
        dzs_ref[:, c_rw:2 * c_rw] = dk
        dzs_ref[:, 2 * c_rw:3 * c_rw] = dv1[...] + dv2[...]
        dzs_ref[:, 3 * c_rw:c4] = dg_ref[...]
        dzs_ref[:, c4:c4 + lora] = dtw * (1.0 - tw * tw)
        dzs_ref[:, c4 + lora:c4 + 2 * lora] = dal
        if sec > c4 + 2 * lora:
            dzs_ref[:, c4 + 2 * lora:sec] = jnp.zeros((zv.shape[0], sec - c4 - 2 * lora), F32)
        tw_ref[...] = tw.astype(BF16)
        al_ref[...] = al.astype(BF16)
        dwr_ref[...] = dwr_b
        dar_ref[...] = dar_b
        dmu_ref[...] += _colsum(dzs_ref[...] * (zp - zv))

    dzs, tw_b, al_b, dwr_b, dar_b, d_mu, d_w0, d_a0, d_kk, d_ka = _rowcall(
        rw_prep_bwd, "rwkv_prep_bwd", t_dim, bt,
        [(z, 0, sec), dr_c, dr_b, dw_c, dk_c, dk_b, dv_c, dv_b, dkk_c, da_c, dg_rw], rw_consts,
        [(sec, F32), (lora, BF16), (lora, BF16), (c_rw, BF16), (c_rw, BF16)],
        acc_outs=[(1, sec), (1, c_rw), (1, c_rw), (1, c_rw), (1, c_rw)], scratch=[pltpu.VMEM((1, sec), F32)])
    d_wl = _matmul(tw_b, dwr_b, "tn", "d_w_lora")
    d_wa = _matmul(al_b, dar_b, "tn", "d_a_lora")

    def shift_bwd(dzs_ref, mu_ref, dz_ref, carry):
        _first_step_zero(carry)
        d = dzs_ref[...]
        nbt = d.shape[0]
        nxt = pltpu.roll(d, nbt - 1, 0)
        rowi = lax.broadcasted_iota(jnp.int32, d.shape, 0)
        nxt = jnp.where(rowi == nbt - 1, carry[...], nxt)
        carry[...] = d[0:1, :]
        m = mu_ref[...]
        dz_ref[...] = (d * (1.0 - m) + nxt * m).astype(BF16)

    (dz_rw,) = _rowcall(shift_bwd, "shift_bwd", t_dim, bt, [dzs], [mu], [(sec, BF16)],
                        scratch=[pltpu.VMEM((1, sec), F32)], reverse=True)

    fox_parts = [dq_f.astype(BF16), dk_f.astype(BF16), dv_f.astype(BF16), dg_fox, dz_fl]
    if sec > 4 * c_fox + LANES:
        fox_parts.append(jnp.zeros((t_dim, sec - 4 * c_fox - LANES), BF16))
    dz = jnp.concatenate([dz_rw] + fox_parts + [dz_gate], axis=1)
    d_wz = _matmul(h, dz, "tn", "d_w_in", bn=1408)
    dh = _matmul(dz, wz, "nt", "d_h", bk=1408)

    def norm_in_bwd(dh_ref, x_ref, dx1_ref, g_ref, dx_ref, dg1_ref):
        _first_step_zero(dg1_ref)
        xv = x_ref[...]
        rms = lax.rsqrt(jnp.mean(xv * xv, axis=-1, keepdims=True) + NORM_EPS)
        xn = xv * rms
        d = dh_ref[...]
        dg1_ref[...] += _colsum(d * xn)
        gy = d * g_ref[...]
        dx_ref[...] = dx1_ref[...] + rms * (gy - xn * jnp.mean(xn * gy, axis=-1, keepdims=True))

    dx, d_g1 = _rowcall(norm_in_bwd, "norm_in_bwd", t_dim, bt, [dh, x, dx1], [norm_g], [(d_model, F32)],
                        acc_outs=[(1, d_model)])

    full_grads = {"w_in": d_wz, "rw_w_lora_up": d_wl, "rw_a_lora_up": d_wa, "w_up_rwkv": d_wur, "w_up_fox": d_wuf,
                  "w_out": d_wo, "ple_proj": d_pp, "ple_gate_w": d_pg}
    rep_grads = {"norm_g": d_g1, "rw_shift_mu": d_mu[:, 0:c4 + 2 * lora], "rw_w0": d_w0, "rw_a0": d_a0, "rw_k_k": d_kk,
                 "rw_k_a": d_ka, "rw_r_k": d_rk, "rw_ln_g": d_lng, "rw_ln_b": d_lnb, "fox_b_f": d_bf[0:h_fox, 0],
                 "ple_norm_g": d_g2, "final_norm_g": d_g3}
    return loss[0, 0], dx, full_grads, rep_grads


def kernel(x, p, norm_g, w_in, rw_shift_mu, rw_w0, rw_w_lora_up, rw_a0, rw_a_lora_up, rw_k_k, rw_k_a, rw_r_k, rw_ln_g, rw_ln_b, fox_b_f, w_up_rwkv, w_up_fox, w_out, ple_proj, ple_gate_w, ple_norm_g, final_norm_g, loss_target, m_norm_g, m_w_in, m_rw_shift_mu, m_rw_w0, m_rw_w_lora_up, m_rw_a0, m_rw_a_lora_up, m_rw_k_k, m_rw_k_a, m_rw_r_k, m_rw_ln_g, m_rw_ln_b, m_fox_b_f, m_w_up_rwkv, m_w_up_fox, m_w_out, m_ple_proj, m_ple_gate_w, m_ple_norm_g, m_final_norm_g, v_norm_g, v_w_in, v_rw_shift_mu, v_rw_w0, v_rw_w_lora_up, v_rw_a0, v_rw_a_lora_up, v_rw_k_k, v_rw_k_a, v_rw_r_k, v_rw_ln_g, v_rw_ln_b, v_fox_b_f, v_w_up_rwkv, v_w_up_fox, v_w_out, v_ple_proj, v_ple_gate_w, v_ple_norm_g, v_final_norm_g):
    args = locals()
    w = {n: args[n] for n in WEIGHTS}
    mom = {n: args["m_" + n] for n in WEIGHTS}
    var = {n: args["v_" + n] for n in WEIGHTS}

    t_dim, d_model = x.shape[1], x.shape[2]
    c_rw, lora = rw_w0.shape[1], rw_w_lora_up.shape[1]
    h_fox = fox_b_f.shape[1]
    c_fox = h_fox * HEAD
    rw_cols, fox_cols, gate_cols = 4 * c_rw + 2 * lora, 4 * c_fox + h_fox, 2 * d_model
    sec = max(rw_cols, 4 * c_fox + LANES, _round_up(gate_cols, LANES))
    assert c_rw % LANES == 0 and c_fox % LANES == 0 and rw_cols % LANES == 0 and h_fox <= LANES and d_model % LANES == 0
    assert rw_a_lora_up.shape[1] == lora and w_in.shape[2] * NDEV == rw_cols + fox_cols + gate_cols

    shards = {n: w[n][0] for n, _ in SHARDED}
    sizes = [shards[n].size for n, _ in SHARDED]
    offs = [sum(sizes[:i]) for i in range(len(sizes))]
    packed = _pack_rows(jnp.concatenate([shards[n].astype(BF16).reshape(-1) for n, _ in SHARDED]))
    gathered = _all_gather(packed, "gather_weights").reshape(NDEV, -1)
    full = {n: _join_shards(gathered[:, o:o + s], shards[n].shape, ax) for (n, ax), o, s in zip(SHARDED, offs, sizes)}

    def to_sections(wi):
        pad = lambda a, width: jnp.pad(a, ((0, 0), (0, width - a.shape[1])))
        return jnp.concatenate([pad(wi[:, :rw_cols], sec), pad(wi[:, rw_cols:rw_cols + fox_cols], sec),
                                pad(wi[:, rw_cols + fox_cols:], sec)], axis=1)

    def from_sections(g):
        return jnp.concatenate([g[:, :rw_cols], g[:, sec:sec + fox_cols], g[:, 2 * sec:2 * sec + gate_cols]], axis=1)

    rep = {n: w[n] for n in REPLICATED}
    dims = (t_dim, d_model, c_rw, lora, c_fox, h_fox, sec)
    loss_local, grad_x, full_grads, rep_grads = _local_step(
        x[0], p[0, 0], loss_target[0], to_sections(full["w_in"]), full["rw_w_lora_up"], full["rw_a_lora_up"],
        full["w_up_rwkv"], full["w_up_fox"], full["w_out"], full["ple_proj"], full["ple_gate_w"], rep, dims)
    full_grads["w_in"] = from_sections(full_grads["w_in"])

    parts = _pack_rows(jnp.concatenate([_split_shards(full_grads[n], ax) for n, ax in SHARDED], axis=1), (NDEV,))
    rep_sizes = [w[n].size for n in REPLICATED]
    rep_offs = [sum(rep_sizes[:i]) for i in range(len(rep_sizes))]
    pack_rep = lambda tree: _pack_rows(jnp.concatenate([tree[n].astype(F32).reshape(-1) for n in REPLICATED]))
    recv, small_all = _grad_exchange(parts, pack_rep(rep_grads), "exchange_grads")
    pack_own = lambda tree: _pack_rows(jnp.concatenate([tree[n][0].reshape(-1) for n, _ in SHARDED]))
    upd = _adamw(recv, pack_own(w), pack_own(mom), pack_own(var), "adamw_sharded")
    upd_rep = _adamw(small_all, pack_rep(w), pack_rep(mom), pack_rep(var), "adamw_replicated")

    outs = {}
    for kind, buf, buf_rep in zip(("grad", "delta", "new_m", "new_v"), upd, upd_rep):
        flat, flat_rep = buf.reshape(-1), buf_rep.reshape(-1)
        for (n, _), o, s in zip(SHARDED, offs, sizes):
            outs[kind, n] = flat[o:o + s].reshape(w[n].shape)
        for n, o, s in zip(REPLICATED, rep_offs, rep_sizes):
            outs[kind, n] = flat_rep[o:o + s].reshape(w[n].shape)
    loss = lax.psum(loss_local, MESH_AXES)
    return (loss, grad_x[None], *[outs[kind, n] for kind in ("grad", "delta", "new_m", "new_v") for n in WEIGHTS])
```

```python
import functools
import math

import jax
import jax.numpy as jnp
from jax import lax
from jax.experimental import pallas as pl
from jax.experimental.pallas import tpu as pltpu

F32, BF16 = jnp.float32, jnp.bfloat16
HI = lax.Precision.HIGHEST
LANES = 128
SUBLANES = 8
HEAD = 64
NORM_EPS = 1e-6
GN_EPS = 64e-5
VMEM_LIMIT = 56 * 1024 * 1024
NDEV = 8
PACK_W = 1024
PACK_ALIGN = 16 * PACK_W
MESH_AXES = ("x", "y", "c")
MESH = pl.DeviceIdType.MESH

ADAM_LR, ADAM_B1, ADAM_B2, ADAM_EPS, ADAM_WD, ADAM_STEP = 0.001, 0.9, 0.999, 1e-08, 0.01, 10


def _round_up(n, m):
    return (n + m - 1) // m * m


def _pick(dim, pref, align=LANES):
    if dim <= pref:
        return dim
    best = None
    for cand in range(align, pref + 1, align):
        if dim % cand == 0:
            best = cand
    return dim if best is None else best


def _params(*sem):
    return pltpu.CompilerParams(dimension_semantics=sem, vmem_limit_bytes=VMEM_LIMIT)


def _sigmoid(v):
    return jax.nn.sigmoid(v)


def _log_sigmoid(v):
    return jnp.minimum(v, 0.0) - jnp.log(1.0 + jnp.exp(-jnp.abs(v)))


_DOT_DIMS = {"nn": (((1,), (0,)), ((), ())), "nt": (((1,), (1,)), ((), ())), "tn": (((0,), (0,)), ((), ()))}


def _matmul(a, b, mode, name, out_dtype=F32, bm=512, bn=1024, bk=512):
    if mode == "tn":
        k_dim, m_dim = a.shape
    else:
        m_dim, k_dim = a.shape
    n_dim = b.shape[0] if mode == "nt" else b.shape[1]
    bm, bn, bk = _pick(m_dim, bm), _pick(n_dim, bn), _pick(k_dim, bk)
    nk = k_dim // bk

    def body(a_ref, b_ref, o_ref, acc_ref):
        k = pl.program_id(2)

        @pl.when(k == 0)
        def _():
            acc_ref[...] = jnp.zeros_like(acc_ref)

        acc_ref[...] += lax.dot_general(a_ref[...].astype(BF16), b_ref[...].astype(BF16), _DOT_DIMS[mode],
                                        preferred_element_type=F32)

        @pl.when(k == nk - 1)
        def _():
            o_ref[...] = acc_ref[...].astype(o_ref.dtype)

    if mode == "tn":
        a_spec = pl.BlockSpec((bk, bm), lambda i, j, k: (k, i))
    else:
        a_spec = pl.BlockSpec((bm, bk), lambda i, j, k: (i, k))
    if mode == "nt":
        b_spec = pl.BlockSpec((bn, bk), lambda i, j, k: (j, k))
    else:
        b_spec = pl.BlockSpec((bk, bn), lambda i, j, k: (k, j))
    return pl.pallas_call(
        body, name=name, grid=(m_dim // bm, n_dim // bn, nk),
        in_specs=[a_spec, b_spec], out_specs=pl.BlockSpec((bm, bn), lambda i, j, k: (i, j)),
        out_shape=jax.ShapeDtypeStruct((m_dim, n_dim), out_dtype),
        scratch_shapes=[pltpu.VMEM((bm, bn), F32)],
        compiler_params=_params("parallel", "parallel", "arbitrary"),
    )(a, b)


def _rowcall(body, name, t_dim, bt, row_ins, const_ins, row_outs, acc_outs=(), scratch=(), reverse=False):
    nt = t_dim // bt

    def rmap(i):
        return nt - 1 - i if reverse else i

    in_specs, args = [], []
    for item in row_ins:
        arr, cb, w = item if isinstance(item, tuple) else (item, 0, item.shape[1])
        in_specs.append(pl.BlockSpec((bt, w), lambda i, cb=cb: (rmap(i), cb)))
        args.append(arr)
    for arr in const_ins:
        in_specs.append(pl.BlockSpec(arr.shape, lambda i, nd=arr.ndim: (0,) * nd))
        args.append(arr)
    out_specs = [pl.BlockSpec((bt, w), lambda i: (rmap(i), 0)) for w, _ in row_outs]
    out_shape = [jax.ShapeDtypeStruct((t_dim, w), dt) for w, dt in row_outs]
    for shp in acc_outs:
        out_specs.append(pl.BlockSpec(shp, lambda i, nd=len(shp): (0,) * nd))
        out_shape.append(jax.ShapeDtypeStruct(shp, F32))
    return pl.pallas_call(
        body, name=name, grid=(nt,), in_specs=in_specs, out_specs=out_specs, out_shape=out_shape,
        scratch_shapes=list(scratch), compiler_params=_params("arbitrary"),
    )(*args)


def _first_step_zero(*refs):
    @pl.when(pl.program_id(0) == 0)
    def _():
        for r in refs:
            r[...] = jnp.zeros_like(r)


def _colsum(v):
    return jnp.sum(v, axis=0, keepdims=True)


def _headsum(v, e):
    parts = [jnp.dot(v[:, p * LANES:(p + 1) * LANES], e, precision=HI, preferred_element_type=F32)
             for p in range(v.shape[1] // LANES)]
    return parts[0] if len(parts) == 1 else jnp.concatenate(parts, axis=1)


def _shift_down(v, carry_ref):
    bt = v.shape[0]
    prev = pltpu.roll(v, 1, 0)
    row = lax.broadcasted_iota(jnp.int32, v.shape, 0)
    prev = jnp.where(row == 0, carry_ref[...], prev)
    carry_ref[...] = v[bt - 1:bt, :]
    return prev


def _pair_consts():
    lane = lax.broadcasted_iota(jnp.int32, (1, LANES), 1)
    m0 = (lane < HEAD).astype(F32)
    m1 = 1.0 - m0
    sub = lax.broadcasted_iota(jnp.int32, (HEAD, LANES), 0)
    lane2 = lax.broadcasted_iota(jnp.int32, (HEAD, LANES), 1)
    i0 = (lane2 == sub).astype(F32)
    i1 = (lane2 == sub + HEAD).astype(F32)
    return m0, m1, i0, i1


def _head_ones(dtype=BF16):
    lane = jnp.arange(LANES)
    return (lane[:, None] // HEAD == lane[None, :] // HEAD).astype(dtype)


def _lanesum(v):
    return jnp.sum(v, axis=1, keepdims=True)


def _split_bf16(v):
    hi = v.astype(BF16).astype(F32)
    rest = v - hi
    mid = rest.astype(BF16).astype(F32)
    return hi, mid, (rest - mid).astype(BF16).astype(F32)


def _col_pair(parts, j, i01, e_bf16):
    lhs = jnp.concatenate([(part[j:j + 1] * i01).astype(BF16) for part in parts], axis=0)
    out = jnp.dot(lhs, e_bf16, preferred_element_type=F32)
    return (out[0:HEAD] + out[HEAD:2 * HEAD]) + out[2 * HEAD:3 * HEAD]


def _col_tiles(out_ref, src_ref, rows, lanes, i01, e_bf16):
    for sl in lanes:
        parts = _split_bf16(src_ref[rows, sl])
        for j in range(SUBLANES):
            out_ref[j, :, sl] = _col_pair(parts, j, i01, e_bf16)


def _row_pair(c0, c1, i0, i1):
    return _colsum(c0 * i0 + c1 * i1)


def _scan_fwd(r, w, k, v, kk, a, tc, npb):
    t_dim, c_dim = r.shape
    wb = LANES * npb
    tc = _pick(t_dim, tc, SUBLANES)

    def body(r_ref, w_ref, k_ref, v_ref, kk_ref, a_ref, e_ref, y_ref, sall_ref, sfin_ref, s_ref):
        @pl.when(pl.program_id(1) == 0)
        def _():
            s_ref[...] = jnp.zeros_like(s_ref)

        m0, m1, i0, i1 = _pair_consts()
        i01 = i0 + i1
        e = e_ref[...]
        sub8 = lax.broadcasted_iota(jnp.int32, (SUBLANES, LANES), 0)
        lanes = [slice(q * LANES, (q + 1) * LANES) for q in range(npb)]
        ng = tc // SUBLANES

        def lanesums(tiles):
            sums = _lanesum(jnp.concatenate(tiles, axis=0))
            return [sums[i * HEAD:(i + 1) * HEAD] for i in range(len(tiles))]

        def halves(s, row):
            return [s * (row * m0), s * (row * m1)]

        def group(gi, carry):
            base = pl.multiple_of(gi * SUBLANES, SUBLANES)
            rows = pl.ds(base, SUBLANES)
            s = list(carry)
            r8 = [r_ref[rows, sl] for sl in lanes]
            w8 = [w_ref[rows, sl] for sl in lanes]
            k8 = [k_ref[rows, sl] for sl in lanes]
            kk8 = [kk_ref[rows, sl] for sl in lanes]
            b8 = [kk8[q] * a_ref[rows, lanes[q]] for q in range(npb)]
            v8 = [_split_bf16(v_ref[rows, sl]) for sl in lanes]
            y8 = [jnp.zeros((SUBLANES, LANES), F32)] * npb
            for j in range(SUBLANES + 1):
                one, before = slice(j, j + 1), slice(j - 1, j)
                tiles = []
                for q in range(npb):
                    if j < SUBLANES:
                        tiles += halves(s[q], kk8[q][one])
                    if j > 0:
                        tiles += halves(s[q], r8[q][before])
                cols = lanesums(tiles)
                per = len(tiles) // npb
                for q in range(npb):
                    mine = cols[q * per:(q + 1) * per]
                    if j > 0:
                        y8[q] = jnp.where(sub8 == j - 1, _row_pair(mine[-2], mine[-1], i0, i1), y8[q])
                    if j < SUBLANES:
                        sall_ref[base + j, :, lanes[q]] = s[q]
                        sb = mine[0] * m0 + mine[1] * m1
                        s[q] = s[q] * w8[q][one] - sb * b8[q][one] + _col_pair(v8[q], j, i01, e) * k8[q][one]
            for q in range(npb):
                y_ref[rows, lanes[q]] = y8[q]
            return tuple(s)

        init = tuple(s_ref[:, q * LANES:(q + 1) * LANES] for q in range(npb))
        fin = lax.fori_loop(0, ng, group, init)
        for q in range(npb):
            s_ref[:, q * LANES:(q + 1) * LANES] = fin[q]
            sfin_ref[:, q * LANES:(q + 1) * LANES] = fin[q]

    row = pl.BlockSpec((tc, wb), lambda p, c: (c, p))
    return pl.pallas_call(
        body, name="rwkv_scan_fwd", grid=(c_dim // wb, t_dim // tc),
        in_specs=[row] * 6 + [pl.BlockSpec((LANES, LANES), lambda p, c: (0, 0))],
        out_specs=[row, pl.BlockSpec((tc, HEAD, wb), lambda p, c: (c, 0, p)), pl.BlockSpec((HEAD, wb), lambda p, c: (0, p))],
        out_shape=[jax.ShapeDtypeStruct((t_dim, c_dim), F32), jax.ShapeDtypeStruct((t_dim, HEAD, c_dim), F32),
                   jax.ShapeDtypeStruct((HEAD, c_dim), F32)],
        scratch_shapes=[pltpu.VMEM((HEAD, wb), F32)],
        compiler_params=_params("parallel", "arbitrary"),
    )(r, w, k, v, kk, a, _head_ones())


def _scan_bwd(r, w, k, v, kk, a, dy, sall, sfin, tc, npb):
    t_dim, c_dim = r.shape
    wb = LANES * npb
    tc = _pick(t_dim, tc, SUBLANES)
    nc = t_dim // tc

    def body(r_ref, w_ref, k_ref, v_ref, kk_ref, a_ref, dy_ref, sall_ref, sfin_ref, e_ref,
             dr_ref, dw_ref, dk_ref, dv_ref, dkk_ref, da_ref, ds_ref, sn_ref):
        @pl.when(pl.program_id(1) == 0)
        def _():
            ds_ref[...] = jnp.zeros_like(ds_ref)
            sn_ref[...] = sfin_ref[...]

        m0, m1, i0, i1 = _pair_consts()
        i01 = i0 + i1
        e = e_ref[...]
        sub8 = lax.broadcasted_iota(jnp.int32, (SUBLANES, LANES), 0)
        lanes = [slice(q * LANES, (q + 1) * LANES) for q in range(npb)]
        ng = tc // SUBLANES

        def halves(s, row):
            return [s * (row * m0), s * (row * m1)]

        def group(gi, carry):
            base = pl.multiple_of((ng - 1 - gi) * SUBLANES, SUBLANES)
            rows = pl.ds(base, SUBLANES)
            ds = list(carry)
            r8 = [r_ref[rows, sl] for sl in lanes]
            w8 = [w_ref[rows, sl] for sl in lanes]
            k8 = [k_ref[rows, sl] for sl in lanes]
            kk8 = [kk_ref[rows, sl] for sl in lanes]
            a8 = [a_ref[rows, sl] for sl in lanes]
            v8 = [_split_bf16(v_ref[rows, sl]) for sl in lanes]
            dy8 = [_split_bf16(dy_ref[rows, sl]) for sl in lanes]
            zero8 = jnp.zeros((SUBLANES, LANES), F32)
            dr8, dw8, dk8, dv8, dkk8, da8 = ([zero8] * npb for _ in range(6))
            for j in reversed(range(SUBLANES)):
                one = slice(j, j + 1)
                here = sub8 == j
                d, s_prev, tiles = [], [], []
                for q in range(npb):
                    dyb = _col_pair(dy8[q], j, i01, e)
                    s_prev.append(sall_ref[base + j, :, lanes[q]])
                    dr8[q] = jnp.where(here, _colsum(sn_ref[:, lanes[q]] * dyb), dr8[q])
                    sn_ref[:, lanes[q]] = s_prev[q]
                    d.append(ds[q] + dyb * r8[q][one])
                for q in range(npb):
                    tiles += halves(d[q], kk8[q][one] * a8[q][one])
                for q in range(npb):
                    tiles += halves(d[q], k8[q][one]) + halves(s_prev[q], kk8[q][one])
                sums = _lanesum(jnp.concatenate(tiles, axis=0))
                cols = [sums[i * HEAD:(i + 1) * HEAD] for i in range(len(tiles))]
                for q in range(npb):
                    kkr, ar = kk8[q][one], a8[q][one]
                    dsb = -(cols[2 * q] * m0 + cols[2 * q + 1] * m1)
                    ds[q] = d[q] * w8[q][one] + dsb * kkr
                    rest = cols[2 * npb + 4 * q:2 * npb + 4 * q + 4]
                    dv8[q] = jnp.where(here, _row_pair(rest[0], rest[1], i0, i1), dv8[q])
                    sb = rest[2] * m0 + rest[3] * m1
                    db = -_colsum(d[q] * sb)
                    dk8[q] = jnp.where(here, _colsum(d[q] * _col_pair(v8[q], j, i01, e)), dk8[q])
                    dw8[q] = jnp.where(here, _colsum(d[q] * s_prev[q]), dw8[q])
                    dkk8[q] = jnp.where(here, _colsum(s_prev[q] * dsb) + db * ar, dkk8[q])
                    da8[q] = jnp.where(here, db * kkr, da8[q])
            for q in range(npb):
                sl = lanes[q]
                dr_ref[rows, sl], dw_ref[rows, sl], dk_ref[rows, sl] = dr8[q], dw8[q], dk8[q]
                dv_ref[rows, sl], dkk_ref[rows, sl], da_ref[rows, sl] = dv8[q], dkk8[q], da8[q]
            return tuple(ds)

        init = tuple(ds_ref[:, q * LANES:(q + 1) * LANES] for q in range(npb))
        fin = lax.fori_loop(0, ng, group, init)
        for q in range(npb):
            ds_ref[:, q * LANES:(q + 1) * LANES] = fin[q]

    row = pl.BlockSpec((tc, wb), lambda p, c: (nc - 1 - c, p))
    return pl.pallas_call(
        body, name="rwkv_scan_bwd", grid=(c_dim // wb, nc),
        in_specs=[row] * 7 + [pl.BlockSpec((tc, HEAD, wb), lambda p, c: (nc - 1 - c, 0, p)),
                              pl.BlockSpec((HEAD, wb), lambda p, c: (0, p)),
                              pl.BlockSpec((LANES, LANES), lambda p, c: (0, 0))],
        out_specs=[row] * 6, out_shape=[jax.ShapeDtypeStruct((t_dim, c_dim), F32)] * 6,
        scratch_shapes=[pltpu.VMEM((HEAD, wb), F32)] * 2,
        compiler_params=_params("parallel", "arbitrary"),
    )(r, w, k, v, kk, a, dy, sall, sfin, _head_ones())


def _fox_fwd(z, ct, cq, q_cb, k_cb, v_cb, n_pairs, blk):
    t_dim = z.shape[0]
    blk = _pick(t_dim, blk)
    nq = t_dim // blk
    scale = HEAD ** -0.5

    def body(q_ref, k_ref, v_ref, ct_ref, cq_ref, o_ref, lse_ref):
        i = pl.program_id(1)
        rowi = lax.broadcasted_iota(jnp.int32, (blk, blk), 0)
        coli = lax.broadcasted_iota(jnp.int32, (blk, blk), 1)
        for hh in range(2):
            hs = slice(hh * HEAD, (hh + 1) * HEAD)
            qh = q_ref[:, hs].astype(BF16)
            cqh = cq_ref[:, hh:hh + 1]

            def kv_step(j, carry, masked):
                m, l, acc = carry
                rows = pl.ds(pl.multiple_of(j * blk, blk), blk)
                kh = k_ref[rows, hs].astype(BF16)
                vh = v_ref[rows, hs].astype(BF16)
                ck = ct_ref[hh:hh + 1, rows]
                s = lax.dot_general(qh, kh, _DOT_DIMS["nt"], preferred_element_type=F32) * scale + (cqh - ck)
                if masked:
                    s = jnp.where(rowi >= coli, s, -jnp.inf)
                m_new = jnp.maximum(m, jnp.max(s, axis=1, keepdims=True))
                alpha = jnp.exp(m - m_new)
                pr = jnp.exp(s - m_new)
                l = l * alpha + jnp.sum(pr, axis=1, keepdims=True)
                acc = acc * alpha + jnp.dot(pr.astype(BF16), vh, preferred_element_type=F32)
                return m_new, l, acc

            init = (jnp.full((blk, 1), -jnp.inf, F32), jnp.zeros((blk, 1), F32), jnp.zeros((blk, HEAD), F32))
            carry = lax.fori_loop(0, i, functools.partial(kv_step, masked=False), init)
            m, l, acc = kv_step(i, carry, True)
            o_ref[:, hs] = acc / l
            lse_ref[:, hh:hh + 1] = m + jnp.log(l)

    full = lambda cb: pl.BlockSpec((t_dim, LANES), lambda p, i, cb=cb: (0, cb + p))
    return pl.pallas_call(
        body, name="fox_attn_fwd", grid=(n_pairs, nq),
        in_specs=[pl.BlockSpec((blk, LANES), lambda p, i: (i, q_cb + p)), full(k_cb), full(v_cb),
                  pl.BlockSpec((None, SUBLANES, t_dim), lambda p, i: (p, 0, 0)),
                  pl.BlockSpec((None, blk, 2), lambda p, i: (p, i, 0))],
        out_specs=[pl.BlockSpec((blk, LANES), lambda p, i: (i, p)), pl.BlockSpec((None, blk, 2), lambda p, i: (p, i, 0))],
        out_shape=[jax.ShapeDtypeStruct((t_dim, n_pairs * LANES), F32), jax.ShapeDtypeStruct((n_pairs, t_dim, 2), F32)],
        compiler_params=_params("parallel", "arbitrary"),
    )(z, z, z, ct, cq)


def _fox_rowdot(z, ct, cq, do, lse, q_cb, k_cb, v_cb, n_pairs, blk):
    t_dim = z.shape[0]
    blk = _pick(t_dim, blk)
    scale = HEAD ** -0.5

    def body(q_ref, k_ref, v_ref, ct_ref, cq_ref, do_ref, lse_ref, out_ref):
        i = pl.program_id(1)
        rowi = lax.broadcasted_iota(jnp.int32, (blk, blk), 0)
        coli = lax.broadcasted_iota(jnp.int32, (blk, blk), 1)
        for hh in range(2):
            hs = slice(hh * HEAD, (hh + 1) * HEAD)
            qh = q_ref[:, hs].astype(BF16)
            dob = do_ref[:, hs].astype(BF16)

            def kv_step(j, carry, masked):
                num, den = carry
                rows = pl.ds(pl.multiple_of(j * blk, blk), blk)
                kh = k_ref[rows, hs].astype(BF16)
                vh = v_ref[rows, hs].astype(BF16)
                ck = ct_ref[hh:hh + 1, rows]
                s = lax.dot_general(qh, kh, _DOT_DIMS["nt"], preferred_element_type=F32) * scale + (cq_ref[:, hh:hh + 1] - ck)
                pr = jnp.exp(s - lse_ref[:, hh:hh + 1])
                if masked:
                    pr = jnp.where(rowi >= coli, pr, 0.0)
                dp = lax.dot_general(dob, vh, _DOT_DIMS["nt"], preferred_element_type=F32)
                return num + jnp.sum(pr * dp, axis=1, keepdims=True), den + jnp.sum(pr, axis=1, keepdims=True)

            zero = jnp.zeros((blk, 1), F32)
            carry = lax.fori_loop(0, i, functools.partial(kv_step, masked=False), (zero, zero))
            num, den = kv_step(i, carry, True)
            out_ref[:, hh:hh + 1] = num / den

    full = lambda cb: pl.BlockSpec((t_dim, LANES), lambda p, i, cb=cb: (0, cb + p))
    return pl.pallas_call(
        body, name="fox_attn_rowdot", grid=(n_pairs, t_dim // blk),
        in_specs=[pl.BlockSpec((blk, LANES), lambda p, i: (i, q_cb + p)), full(k_cb), full(v_cb),
                  pl.BlockSpec((None, SUBLANES, t_dim), lambda p, i: (p, 0, 0)),
                  pl.BlockSpec((None, blk, 2), lambda p, i: (p, i, 0)),
                  pl.BlockSpec((blk, LANES), lambda p, i: (i, p)),
                  pl.BlockSpec((None, blk, 2), lambda p, i: (p, i, 0))],
        out_specs=pl.BlockSpec((None, blk, 2), lambda p, i: (p, i, 0)),
        out_shape=jax.ShapeDtypeStruct((n_pairs, t_dim, 2), F32),
        compiler_params=_params("parallel", "arbitrary"),
    )(z, z, z, ct, cq, do, lse)


def _fox_bwd(z, ct, cq, rowdot, do, lse, q_cb, k_cb, v_cb, n_pairs, blk):
    t_dim = z.shape[0]
    blk = _pick(t_dim, blk)
    nb = t_dim // blk
    scale = HEAD ** -0.5

    def body(q_ref, k_ref, v_ref, ct_ref, cq_ref, rd_ref, do_ref, lse_ref, dq_ref, dk_ref, dv_ref, dc_ref):
        j = pl.program_id(1)

        @pl.when(j == 0)
        def _():
            dq_ref[...] = jnp.zeros_like(dq_ref)

        rowi = lax.broadcasted_iota(jnp.int32, (blk, blk), 0)
        coli = lax.broadcasted_iota(jnp.int32, (blk, blk), 1)
        krows = pl.ds(pl.multiple_of(j * blk, blk), blk)
        dc_rows = []
        for hh in range(2):
            hs = slice(hh * HEAD, (hh + 1) * HEAD)
            kh = k_ref[:, hs].astype(BF16)
            vh = v_ref[:, hs].astype(BF16)
            ck = ct_ref[hh:hh + 1, krows]

            def q_step(i, carry, masked):
                dk, dv, dc = carry
                rows = pl.ds(pl.multiple_of(i * blk, blk), blk)
                qh = q_ref[rows, hs].astype(BF16)
                dob = do_ref[rows, hs].astype(BF16)
                dcol = rd_ref[rows, hh:hh + 1]
                s = lax.dot_general(qh, kh, _DOT_DIMS["nt"], preferred_element_type=F32) * scale + (cq_ref[rows, hh:hh + 1] - ck)
                pr = jnp.exp(s - lse_ref[rows, hh:hh + 1])
                if masked:
                    pr = jnp.where(rowi >= coli, pr, 0.0)
                dv = dv + lax.dot_general(pr.astype(BF16), dob, _DOT_DIMS["tn"], preferred_element_type=F32)
                dp = lax.dot_general(dob, vh, _DOT_DIMS["nt"], preferred_element_type=F32)
                ds = pr * (dp - dcol)
                dsb = ds.astype(BF16)
                dq_ref[rows, hs] += jnp.dot(dsb, kh, preferred_element_type=F32) * scale
                dk = dk + lax.dot_general(dsb, qh, _DOT_DIMS["tn"], preferred_element_type=F32) * scale
                dc = dc - _colsum(ds)
                return dk, dv, dc

            init = (jnp.zeros((blk, HEAD), F32), jnp.zeros((blk, HEAD), F32), jnp.zeros((1, blk), F32))
            carry = q_step(j, init, True)
            dk, dv, dc = lax.fori_loop(j + 1, nb, functools.partial(q_step, masked=False), carry)
            dk_ref[:, hs] = dk
            dv_ref[:, hs] = dv
            dc_rows.append(dc)
        dc_ref[0:1, :] = dc_rows[0]
        dc_ref[1:2, :] = dc_rows[1]
        dc_ref[2:SUBLANES, :] = jnp.zeros((SUBLANES - 2, blk), F32)

    full = lambda: pl.BlockSpec((t_dim, LANES), lambda p, j: (0, p))
    blkspec = pl.BlockSpec((blk, LANES), lambda p, j: (j, p))
    return pl.pallas_call(
        body, name="fox_attn_bwd", grid=(n_pairs, nb),
        in_specs=[pl.BlockSpec((t_dim, LANES), lambda p, j: (0, q_cb + p)),
                  pl.BlockSpec((blk, LANES), lambda p, j: (j, k_cb + p)),
                  pl.BlockSpec((blk, LANES), lambda p, j: (j, v_cb + p)),
                  pl.BlockSpec((None, SUBLANES, t_dim), lambda p, j: (p, 0, 0)),
                  pl.BlockSpec((None, t_dim, 2), lambda p, j: (p, 0, 0)),
                  pl.BlockSpec((None, t_dim, 2), lambda p, j: (p, 0, 0)), full(),
                  pl.BlockSpec((None, t_dim, 2), lambda p, j: (p, 0, 0))],
        out_specs=[full(), blkspec, blkspec, pl.BlockSpec((None, SUBLANES, blk), lambda p, j: (p, 0, j))],
        out_shape=[jax.ShapeDtypeStruct((t_dim, n_pairs * LANES), F32)] * 3
                  + [jax.ShapeDtypeStruct((n_pairs, SUBLANES, t_dim), F32)],
        compiler_params=_params("parallel", "arbitrary"),
    )(z, z, z, ct, cq, rowdot, do, lse)


HBM_SPEC = pl.BlockSpec(memory_space=pltpu.HBM)


def _all_gather(shards, name):
    n = len(shards)

    def body(*refs):
        x_refs, out_refs = refs[:n], refs[n:2 * n]
        send_sems, recv_sems, local_sems = refs[2 * n:]
        x, y, c = lax.axis_index("x"), lax.axis_index("y"), lax.axis_index("c")
        me, sibling = (x, y, c), (x, y, 1 - c)
        chips = [(1 - x, y), (x, 1 - y), (1 - x, 1 - y)]

        def copy(a, k, block, to, from_input=False):
            px, py, pc = block
            slot = out_refs[a].at[4 * px + 2 * py + pc]
            return pltpu.make_async_remote_copy(
                src_ref=x_refs[a] if from_input else slot, dst_ref=slot,
                send_sem=send_sems.at[7 * a + k], recv_sem=recv_sems.at[7 * a + k], device_id=to, device_id_type=MESH)

        mine = [pltpu.make_async_copy(x_refs[a], out_refs[a].at[4 * x + 2 * y + c], local_sems.at[a]) for a in range(n)]
        for cp in mine:
            cp.start()
        first = []
        for a in range(n):
            first += [copy(a, 1 + j, me, (*chip, c), from_input=True) for j, chip in enumerate(chips)]
            first.append(copy(a, 0, me, sibling, from_input=True))
        for cp in first:
            cp.start()
        passed = []
        for a in range(n):
            for j, chip in enumerate(chips):
                copy(a, 1 + j, (*chip, c), me).wait_recv()
                passed.append(copy(a, 4 + j, (*chip, c), sibling))
                passed[-1].start()
        for a in range(n):
            copy(a, 0, sibling, me).wait_recv()
            for j, chip in enumerate(chips):
                copy(a, 4 + j, (*chip, 1 - c), me).wait_recv()
        for cp in first + passed:
            cp.wait_send()
        for cp in mine:
            cp.wait()

    return pl.pallas_call(
        body, name=name, out_shape=[jax.ShapeDtypeStruct((NDEV,) + s.shape, s.dtype) for s in shards],
        in_specs=[HBM_SPEC] * n, out_specs=[HBM_SPEC] * n,
        scratch_shapes=[pltpu.SemaphoreType.DMA((7 * n,)), pltpu.SemaphoreType.DMA((7 * n,)),
                        pltpu.SemaphoreType.DMA((n,))],
    )(*shards)


def _grad_exchange(parts, small, name):
    n = len(parts)

    def body(*refs):
        g_refs, s_ref = refs[:n], refs[n]
        recv_refs, sall_ref = refs[n + 1:2 * n + 1], refs[2 * n + 1]
        send_sems, recv_sems, local_sems = refs[2 * n + 2:]
        x, y, c = lax.axis_index("x"), lax.axis_index("y"), lax.axis_index("c")
        me = 4 * x + 2 * y + c
        own = [pltpu.make_async_copy(g_refs[a].at[me], recv_refs[a].at[me], local_sems.at[a]) for a in range(n)]
        own.append(pltpu.make_async_copy(s_ref, sall_ref.at[me], local_sems.at[n]))
        for cp in own:
            cp.start()
        sends, recvs = [], []
        for a in range(n + 1):
            for k in range(1, NDEV):
                px = 1 - x if k & 4 else x
                py = 1 - y if k & 2 else y
                pc = 1 - c if k & 1 else c
                peer = 4 * px + 2 * py + pc
                sems = dict(send_sem=send_sems.at[7 * a + k - 1], recv_sem=recv_sems.at[7 * a + k - 1],
                            device_id=(px, py, pc), device_id_type=MESH)
                if a < n:
                    src, to_me, from_peer = g_refs[a].at[peer], recv_refs[a].at[me], recv_refs[a].at[peer]
                else:
                    src, to_me, from_peer = s_ref, sall_ref.at[me], sall_ref.at[peer]
                sends.append(pltpu.make_async_remote_copy(src_ref=src, dst_ref=to_me, **sems))
                recvs.append(pltpu.make_async_remote_copy(src_ref=src, dst_ref=from_peer, **sems))
        for cp in sends:
            cp.start()
        for cp in recvs:
            cp.wait_recv()
        for cp in sends:
            cp.wait_send()
        for cp in own:
            cp.wait()

    return pl.pallas_call(
        body, name=name,
        out_shape=[jax.ShapeDtypeStruct(a.shape, a.dtype) for a in parts]
                  + [jax.ShapeDtypeStruct((NDEV,) + small.shape, small.dtype)],
        in_specs=[HBM_SPEC] * (n + 1), out_specs=[HBM_SPEC] * (n + 1),
        scratch_shapes=[pltpu.SemaphoreType.DMA((7 * (n + 1),)), pltpu.SemaphoreType.DMA((7 * (n + 1),)),
                        pltpu.SemaphoreType.DMA((n + 1,))],
    )(*parts, small)


def _adamw(partials, w, m, v, name):
    rows, width = w.shape
    br = _pick(rows, 128, 2 * SUBLANES)

    def body(p_ref, w_ref, m_ref, v_ref, g_out, d_out, m_out, v_out):
        g = p_ref[0].astype(F32)
        for d in range(1, NDEV):
            g = g + p_ref[d].astype(F32)
        mn = ADAM_B1 * m_ref[...] + (1.0 - ADAM_B1) * g
        vn = ADAM_B2 * v_ref[...] + (1.0 - ADAM_B2) * jnp.square(g)
        m_hat = mn / (1.0 - ADAM_B1 ** ADAM_STEP)
        v_hat = vn / (1.0 - ADAM_B2 ** ADAM_STEP)
        g_out[...] = g
        d_out[...] = -ADAM_LR * (m_hat / (jnp.sqrt(v_hat) + ADAM_EPS) + ADAM_WD * w_ref[...])
        m_out[...] = mn
        v_out[...] = vn

    blk = pl.BlockSpec((br, width), lambda i: (i, 0))
    return pl.pallas_call(
        body, name=name, grid=(rows // br,),
        in_specs=[pl.BlockSpec((NDEV, br, width), lambda i: (0, i, 0)), blk, blk, blk],
        out_specs=[blk] * 4, out_shape=[jax.ShapeDtypeStruct((rows, width), F32)] * 4,
        compiler_params=_params("parallel"),
    )(partials, w, m, v)


def _pack_rows(flat):
    n = flat.shape[0]
    padded = _round_up(n, PACK_ALIGN)
    return jnp.pad(flat, (0, padded - n)).reshape(padded // PACK_W, PACK_W)


def _split_shards(full, axis):
    rows, cols = full.shape
    if axis == 0:
        return full.reshape(NDEV, rows // NDEV, cols)
    width = cols // NDEV
    return jnp.stack([full[:, d * width:(d + 1) * width] for d in range(NDEV)])


def _join_shards(blocks, axis):
    if axis == 0:
        return blocks.reshape(-1, blocks.shape[2])
    return jnp.concatenate([blocks[d] for d in range(NDEV)], axis=1)


SHARDED = (("w_in", 1), ("rw_w_lora_up", 1), ("rw_a_lora_up", 1), ("w_up_rwkv", 1), ("w_up_fox", 1),
           ("w_out", 0), ("ple_proj", 1), ("ple_gate_w", 0))
REPLICATED = ("norm_g", "rw_shift_mu", "rw_w0", "rw_a0", "rw_k_k", "rw_k_a", "rw_r_k", "rw_ln_g", "rw_ln_b",
              "fox_b_f", "ple_norm_g", "final_norm_g")
WEIGHTS = ("norm_g", "w_in", "rw_shift_mu", "rw_w0", "rw_w_lora_up", "rw_a0", "rw_a_lora_up", "rw_k_k", "rw_k_a",
           "rw_r_k", "rw_ln_g", "rw_ln_b", "fox_b_f", "w_up_rwkv", "w_up_fox", "w_out", "ple_proj", "ple_gate_w",
           "ple_norm_g", "final_norm_g")


def _local_step(x, p, tgt, wz, wl, wa, wur, wuf, wo, pp, pg, rep, dims):
    t_dim, d_model, c_rw, lora, c_fox, h_fox, sec = dims
    bt = _pick(t_dim, 128, SUBLANES)
    n_pairs = c_fox // LANES
    row = lambda a: a.reshape(1, -1)
    norm_g, mu, w0, a0 = row(rep["norm_g"]), row(rep["rw_shift_mu"]), row(rep["rw_w0"]), row(rep["rw_a0"])
    k_k, k_a, r_k = row(rep["rw_k_k"]), row(rep["rw_k_a"]), row(rep["rw_r_k"])
    ln_g, ln_b = row(rep["rw_ln_g"]), row(rep["rw_ln_b"])
    g2, g3 = row(rep["ple_norm_g"]), row(rep["final_norm_g"])
    b_f = jnp.pad(row(rep["fox_b_f"]), ((0, 0), (0, LANES - h_fox)))
    e_head = _head_ones(F32)
    c4 = 4 * c_rw
    inv_d = 1.0 / d_model
    decay_k = math.exp(-0.5)

    def norm_in(x_ref, g_ref, h_ref):
        xv = x_ref[...]
        rms = lax.rsqrt(jnp.mean(xv * xv, axis=-1, keepdims=True) + NORM_EPS)
        h_ref[...] = (xv * rms * g_ref[...]).astype(BF16)

    (h,) = _rowcall(norm_in, "norm_in", t_dim, bt, [x], [norm_g], [(d_model, BF16)])
    z = _matmul(h, wz, "nn", "proj_in", bn=1408)

    def rw_values(zs, w0_ref, wl_ref, a0_ref, wa_ref, kk_ref, ka_ref, e_ref):
        k = zs[:, c_rw:2 * c_rw]
        tw = jnp.tanh(zs[:, c4:c4 + lora])
        al = zs[:, c4 + lora:c4 + 2 * lora]
        sw = _sigmoid(w0_ref[...] + jnp.dot(tw.astype(BF16), wl_ref[...], preferred_element_type=F32))
        decay = jnp.exp(-decay_k * sw)
        a = _sigmoid(a0_ref[...] + jnp.dot(al.astype(BF16), wa_ref[...], preferred_element_type=F32))
        kk0 = k * kk_ref[...]
        nrm = jnp.sqrt(_headsum(kk0 * kk0, e_ref[...]))
        inv = 1.0 / jnp.maximum(nrm, 1e-12)
        k2 = k * (1.0 + (a - 1.0) * ka_ref[...])
        return k, tw, al, sw, decay, a, kk0, nrm, inv, k2

    def rw_prep(z_ref, mu_ref, w0_ref, wl_ref, a0_ref, wa_ref, kk_ref, ka_ref, e_ref,
                r_o, w_o, k_o, v_o, kk_o, a_o, g_o, carry):
        _first_step_zero(carry)
        zv = z_ref[...]
        zs = zv + (_shift_down(zv, carry) - zv) * mu_ref[...]
        k, tw, al, sw, decay, a, kk0, nrm, inv, k2 = rw_values(zs, w0_ref, wl_ref, a0_ref, wa_ref, kk_ref, ka_ref, e_ref)
        r_o[...] = zs[:, 0:c_rw]
        w_o[...] = decay
        k_o[...] = k2
        v_o[...] = zs[:, 2 * c_rw:3 * c_rw]
        kk_o[...] = kk0 * inv
        a_o[...] = a
        g_o[...] = zs[:, 3 * c_rw:c4]

    rw_consts = [mu, w0, wl, a0, wa, k_k, k_a, e_head]
    r_s, w_s, k_s, v_s, kk_s, a_s, g_s = _rowcall(
        rw_prep, "rwkv_prep", t_dim, bt, [(z, 0, sec)], rw_consts, [(c_rw, F32)] * 7,
        scratch=[pltpu.VMEM((1, sec), F32)])
    pairs_fwd = max(n for n in (1, 2, 4) if c_rw % (n * LANES) == 0)
    pairs_bwd = min(pairs_fwd, 2)
    y_s, s_all, s_fin = _scan_fwd(r_s, w_s, k_s, v_s, kk_s, a_s, 64, pairs_fwd)

    def rw_post_values(y, r, k2, v, g, lng_ref, lnb_ref, rk_ref, e):
        mean = _headsum(y, e) * (1.0 / HEAD)
        d = y - mean
        rstd = lax.rsqrt(_headsum(d * d, e) * (1.0 / HEAD) + GN_EPS)
        yh = d * rstd
        rk = _headsum(r * k2 * rk_ref[...], e)
        yo = yh * lng_ref[...] + lnb_ref[...] + rk * v
        sg = _sigmoid(g)
        return rstd, yh, rk, yo, sg

    def rw_post(y_ref, r_ref, k_ref, v_ref, g_ref, lng_ref, lnb_ref, rk_ref, e_ref, out_ref):
        g = g_ref[...]
        _, _, _, yo, sg = rw_post_values(y_ref[...], r_ref[...], k_ref[...], v_ref[...], g, lng_ref, lnb_ref, rk_ref, e_ref[...])
        out_ref[...] = (yo * g * sg).astype(BF16)

    (y_rw,) = _rowcall(rw_post, "rwkv_post", t_dim, bt, [y_s, r_s, k_s, v_s, g_s], [ln_g, ln_b, r_k, e_head], [(c_rw, BF16)])

    fl_cb = (sec + 4 * c_fox) // LANES
    hp = _round_up(h_fox, SUBLANES)
    bt_c = _pick(t_dim, 256)
    tri = (jnp.arange(bt_c)[:, None] >= jnp.arange(bt_c)[None, :]).astype(F32)

    rows8 = n_pairs * SUBLANES
    pair_rows = (jnp.arange(rows8)[:, None] // SUBLANES * 2 + jnp.arange(rows8)[:, None] % SUBLANES
                 == jnp.arange(LANES)[None, :]) & (jnp.arange(rows8)[:, None] % SUBLANES < 2)
    pair_rows = pair_rows.astype(F32)

    def fox_decay(fl_ref, bf_ref, tri_ref, sel_ref, ct_ref, cq_ref, carry):
        _first_step_zero(carry)
        lf = _log_sigmoid(fl_ref[...] + bf_ref[...])
        c = jnp.dot(tri_ref[...], lf, precision=HI, preferred_element_type=F32) + carry[...]
        carry[...] = c[bt_c - 1:bt_c, :]
        ct = jnp.dot(sel_ref[...], jnp.transpose(c), precision=HI, preferred_element_type=F32)
        ct_ref[...] = ct.reshape(n_pairs, SUBLANES, bt_c)
        for pair in range(n_pairs):
            cq_ref[pair] = c[:, 2 * pair:2 * pair + 2]

    ct, cq = pl.pallas_call(
        fox_decay, name="fox_decay", grid=(t_dim // bt_c,),
        in_specs=[pl.BlockSpec((bt_c, LANES), lambda i: (i, fl_cb)), pl.BlockSpec((1, LANES), lambda i: (0, 0)),
                  pl.BlockSpec((bt_c, bt_c), lambda i: (0, 0)), pl.BlockSpec((rows8, LANES), lambda i: (0, 0))],
        out_specs=[pl.BlockSpec((n_pairs, SUBLANES, bt_c), lambda i: (0, 0, i)),
                   pl.BlockSpec((n_pairs, bt_c, 2), lambda i: (0, i, 0))],
        out_shape=[jax.ShapeDtypeStruct((n_pairs, SUBLANES, t_dim), F32), jax.ShapeDtypeStruct((n_pairs, t_dim, 2), F32)],
        scratch_shapes=[pltpu.VMEM((1, LANES), F32)], compiler_params=_params("arbitrary"),
    )(z, b_f, tri, pair_rows)
    q_cb = sec // LANES
    k_cb, v_cb = q_cb + n_pairs, q_cb + 2 * n_pairs
    o_fox, lse = _fox_fwd(z, ct, cq, q_cb, k_cb, v_cb, n_pairs, 256)

    def fox_post(o_ref, z_ref, out_ref):
        g = z_ref[:, 3 * c_fox:4 * c_fox]
        out_ref[...] = (o_ref[...] * g * _sigmoid(g)).astype(BF16)

    (y_fox,) = _rowcall(fox_post, "fox_post", t_dim, bt, [o_fox, (z, 1, sec)], [], [(c_fox, BF16)])

    u_rw = _matmul(y_rw, wur, "nn", "up_rwkv")
    u_fox = _matmul(y_fox, wuf, "nn", "up_fox")

    def merge(ur_ref, uf_ref, z_ref, out_ref):
        s1 = _sigmoid(z_ref[:, 0:d_model])
        s2 = _sigmoid(z_ref[:, d_model:2 * d_model])
        out_ref[...] = (s1 * ur_ref[...] + s2 * uf_ref[...]).astype(BF16)

    (merged,) = _rowcall(merge, "merge", t_dim, bt, [u_rw, u_fox, (z, 2, sec)], [], [(d_model, BF16)])
    mo = _matmul(merged, wo, "nn", "proj_out")

    def resid_norm(x_ref, mo_ref, g_ref, x1_ref, n2_ref):
        x1 = x_ref[...] + mo_ref[...]
        rms = lax.rsqrt(jnp.mean(x1 * x1, axis=-1, keepdims=True) + NORM_EPS)
        x1_ref[...] = x1
        n2_ref[...] = (x1 * rms * g_ref[...]).astype(BF16)

    x1, n2 = _rowcall(resid_norm, "resid_norm", t_dim, bt, [x, mo], [g2], [(d_model, F32), (d_model, BF16)])
    ple = _matmul(p, pp, "nn", "ple_proj")
    gl = _matmul(n2, pg, "nn", "ple_gate")

    def head(x1_ref, ple_ref, gl_ref, tgt_ref, g_ref, dx2_ref, dple_ref, dgl_ref, loss_ref, dg3_ref):
        _first_step_zero(loss_ref, dg3_ref)
        sg = _sigmoid(gl_ref[...])
        pl_v = ple_ref[...]
        x2 = x1_ref[...] + pl_v * sg
        rms = lax.rsqrt(jnp.mean(x2 * x2, axis=-1, keepdims=True) + NORM_EPS)
        xn = x2 * rms
        diff = xn * g_ref[...] - tgt_ref[...]
        loss_ref[...] += 0.5 * jnp.sum(jnp.mean(diff * diff, axis=-1, keepdims=True), axis=0, keepdims=True)
        dyf = diff * inv_d
        dg3_ref[...] += _colsum(dyf * xn)
        gy = dyf * g_ref[...]
        dx2 = rms * (gy - xn * jnp.mean(xn * gy, axis=-1, keepdims=True))
        dx2_ref[...] = dx2
        dple_ref[...] = (dx2 * sg).astype(BF16)
        dgl_ref[...] = (dx2 * pl_v * sg * (1.0 - sg)).astype(BF16)

    dx2, dple, dgl, loss, d_g3 = _rowcall(
        head, "head", t_dim, bt, [x1, ple, gl, tgt], [g3], [(d_model, F32), (d_model, BF16), (d_model, BF16)],
        acc_outs=[(1, 1), (1, d_model)])

    d_pp = _matmul(p, dple, "tn", "d_ple_proj", out_dtype=BF16)
    d_pg = _matmul(n2, dgl, "tn", "d_ple_gate", out_dtype=BF16)
    dn2 = _matmul(dgl, pg, "nt", "d_n2")

    def resid_norm_bwd(dx2_ref, dn2_ref, x1_ref, g_ref, dx1_ref, dx1b_ref, dg2_ref):
        _first_step_zero(dg2_ref)
        x1 = x1_ref[...]
        rms = lax.rsqrt(jnp.mean(x1 * x1, axis=-1, keepdims=True) + NORM_EPS)
        xn = x1 * rms
        dn = dn2_ref[...]
        dg2_ref[...] += _colsum(dn * xn)
        gy = dn * g_ref[...]
        dx1 = dx2_ref[...] + rms * (gy - xn * jnp.mean(xn * gy, axis=-1, keepdims=True))
        dx1_ref[...] = dx1
        dx1b_ref[...] = dx1.astype(BF16)

    dx1, dx1b, d_g2 = _rowcall(resid_norm_bwd, "resid_norm_bwd", t_dim, bt, [dx2, dn2, x1], [g2],
                               [(d_model, F32), (d_model, BF16)], acc_outs=[(1, d_model)])
    d_wo = _matmul(merged, dx1b, "tn", "d_w_out", out_dtype=BF16)
    dmerged = _matmul(dx1b, wo, "nt", "d_merged")

    def merge_bwd(dm_ref, ur_ref, uf_ref, z_ref, dur_ref, duf_ref, dzg_ref):
        dm = dm_ref[...]
        s1 = _sigmoid(z_ref[:, 0:d_model])
        s2 = _sigmoid(z_ref[:, d_model:2 * d_model])
        dur_ref[...] = (dm * s1).astype(BF16)
        duf_ref[...] = (dm * s2).astype(BF16)
        dzg_ref[:, 0:d_model] = (dm * ur_ref[...] * s1 * (1.0 - s1)).astype(BF16)
        dzg_ref[:, d_model:2 * d_model] = (dm * uf_ref[...] * s2 * (1.0 - s2)).astype(BF16)
        if sec > 2 * d_model:
            dzg_ref[:, 2 * d_model:sec] = jnp.zeros((dm.shape[0], sec - 2 * d_model), BF16)

    du_rw, du_fox, dz_gate = _rowcall(merge_bwd, "merge_bwd", t_dim, bt, [dmerged, u_rw, u_fox, (z, 2, sec)], [],
                                      [(d_model, BF16), (d_model, BF16), (sec, BF16)])
    d_wur = _matmul(y_rw, du_rw, "tn", "d_w_up_rwkv", out_dtype=BF16)
    d_wuf = _matmul(y_fox, du_fox, "tn", "d_w_up_fox", out_dtype=BF16)
    dy_rw = _matmul(du_rw, wur, "nt", "d_y_rwkv")
    dy_fox = _matmul(du_fox, wuf, "nt", "d_y_fox")

    def fox_post_bwd(dy_ref, o_ref, z_ref, do_ref, dg_ref):
        g = z_ref[:, 3 * c_fox:4 * c_fox]
        sg = _sigmoid(g)
        dy = dy_ref[...]
        do_ref[...] = dy * g * sg
        dg_ref[...] = (dy * o_ref[...] * sg * (1.0 + g * (1.0 - sg))).astype(BF16)

    do_fox, dg_fox = _rowcall(fox_post_bwd, "fox_post_bwd", t_dim, bt, [dy_fox, o_fox, (z, 1, sec)], [],
                              [(c_fox, F32), (c_fox, BF16)])
    rowdot = _fox_rowdot(z, ct, cq, do_fox, lse, q_cb, k_cb, v_cb, n_pairs, 256)
    dq_f, dk_f, dv_f, dc_t = _fox_bwd(z, ct, cq, rowdot, do_fox, lse, q_cb, k_cb, v_cb, n_pairs, 256)
    sel = (jnp.arange(hp)[:, None] // 2 * SUBLANES + jnp.arange(hp)[:, None] % 2 == jnp.arange(rows8)[None, :]).astype(F32)
    tri_rev = (jnp.arange(bt_c)[:, None] >= jnp.arange(bt_c)[None, :]).astype(F32)
    bf_col = b_f.reshape(LANES, 1)[0:hp]
    nbc = t_dim // bt_c

    def fox_decay_bwd(dc_ref, fl_ref, sel_ref, tri_ref, bf_ref, dfl_ref, dbf_ref, carry):
        _first_step_zero(carry, dbf_ref)
        dc = jnp.dot(sel_ref[...], dc_ref[...].reshape(rows8, bt_c), precision=HI, preferred_element_type=F32)
        dlf = jnp.dot(dc, tri_ref[...], precision=HI, preferred_element_type=F32) + carry[...]
        carry[...] = dlf[:, 0:1]
        flt = jnp.transpose(fl_ref[...])[0:hp, :]
        dfl = dlf * _sigmoid(-(flt + bf_ref[...]))
        head_row = lax.broadcasted_iota(jnp.int32, (hp, bt_c), 0)
        dfl = jnp.where(head_row < h_fox, dfl, 0.0)
        dbf_ref[...] += jnp.sum(dfl, axis=1, keepdims=True)
        full = jnp.concatenate([dfl, jnp.zeros((LANES - hp, bt_c), F32)], axis=0) if hp < LANES else dfl
        dfl_ref[...] = jnp.transpose(full).astype(BF16)

    dz_fl, d_bf = pl.pallas_call(
        fox_decay_bwd, name="fox_decay_bwd", grid=(nbc,),
        in_specs=[pl.BlockSpec((n_pairs, SUBLANES, bt_c), lambda i: (0, 0, nbc - 1 - i)),
                  pl.BlockSpec((bt_c, LANES), lambda i: (nbc - 1 - i, fl_cb)),
                  pl.BlockSpec(sel.shape, lambda i: (0, 0)), pl.BlockSpec((bt_c, bt_c), lambda i: (0, 0)),
                  pl.BlockSpec((hp, 1), lambda i: (0, 0))],
        out_specs=[pl.BlockSpec((bt_c, LANES), lambda i: (nbc - 1 - i, 0)), pl.BlockSpec((hp, 1), lambda i: (0, 0))],
        out_shape=[jax.ShapeDtypeStruct((t_dim, LANES), BF16), jax.ShapeDtypeStruct((hp, 1), F32)],
        scratch_shapes=[pltpu.VMEM((hp, 1), F32)], compiler_params=_params("arbitrary"),
    )(dc_t, z, sel, tri_rev, bf_col)

    def rw_post_bwd(dy_ref, y_ref, r_ref, k_ref, v_ref, g_ref, lng_ref, lnb_ref, rk_ref, e_ref,
                    dg_ref, dys_ref, dr_ref, dk_ref, dv_ref, dlng_ref, dlnb_ref, drk_ref):
        _first_step_zero(dlng_ref, dlnb_ref, drk_ref)
        e = e_ref[...]
        dy, r, k2, v, g = dy_ref[...], r_ref[...], k_ref[...], v_ref[...], g_ref[...]
        rstd, yh, rk, yo, sg = rw_post_values(y_ref[...], r, k2, v, g, lng_ref, lnb_ref, rk_ref, e)
        dg_ref[...] = dy * yo * sg * (1.0 + g * (1.0 - sg))
        dyo = dy * g * sg
        dlnb_ref[...] += _colsum(dyo)
        dlng_ref[...] += _colsum(dyo * yh)
        dyh = dyo * lng_ref[...]
        dys_ref[...] = rstd * (dyh - _headsum(dyh, e) * (1.0 / HEAD) - yh * _headsum(dyh * yh, e) * (1.0 / HEAD))
        drk = _headsum(dyo * v, e)
        dv_ref[...] = dyo * rk
        dr_ref[...] = drk * k2 * rk_ref[...]
        dk_ref[...] = drk * r * rk_ref[...]
        drk_ref[...] += _colsum(drk * r * k2)

    dg_rw, dy_s, dr_b, dk_b, dv_b, d_lng, d_lnb, d_rk = _rowcall(
        rw_post_bwd, "rwkv_post_bwd", t_dim, bt, [dy_rw, y_s, r_s, k_s, v_s, g_s], [ln_g, ln_b, r_k, e_head],
        [(c_rw, F32)] * 5, acc_outs=[(1, c_rw)] * 3)
    dr_c, dw_c, dk_c, dv_c, dkk_c, da_c = _scan_bwd(r_s, w_s, k_s, v_s, kk_s, a_s, dy_s, s_all, s_fin, 128, pairs_bwd)

    def rw_prep_bwd(z_ref, dr1, dr2, dw_ref, dk1, dk2_ref, dv1, dv2, dkk_ref, da_ref, dg_ref,
                    mu_ref, w0_ref, wl_ref, a0_ref, wa_ref, kk_ref, ka_ref, e_ref,
                    dzs_ref, tw_ref, al_ref, dwr_ref, dar_ref, dmu_ref, dw0_ref, da0_ref, dkk_acc, dka_acc, carry):
        _first_step_zero(carry, dmu_ref, dw0_ref, da0_ref, dkk_acc, dka_acc)
        e = e_ref[...]
        zv = z_ref[...]
        zp = _shift_down(zv, carry)
        zs = zv + (zp - zv) * mu_ref[...]
        k, tw, al, sw, decay, a, kk0, nrm, inv, k2 = rw_values(zs, w0_ref, wl_ref, a0_ref, wa_ref, kk_ref, ka_ref, e_ref)
        dk2 = dk1[...] + dk2_ref[...]
        da = da_ref[...] + dk2 * k * ka_ref[...]
        dk = dk2 * (1.0 + (a - 1.0) * ka_ref[...])
        dka_acc[...] += _colsum(dk2 * k * (a - 1.0))
        kk = kk0 * inv
        dkk = dkk_ref[...]
        dkk0 = inv * jnp.where(nrm > 1e-12, dkk - kk * _headsum(dkk * kk, e), dkk)
        dk = dk + dkk0 * kk_ref[...]
        dkk_acc[...] += _colsum(dkk0 * k)
        da_raw = da * a * (1.0 - a)
        da0_ref[...] += _colsum(da_raw)
        dw_raw = dw_ref[...] * decay * (-decay_k) * sw * (1.0 - sw)
        dw0_ref[...] += _colsum(dw_raw)
        dar_b, dwr_b = da_raw.astype(BF16), dw_raw.astype(BF16)
        dal = lax.dot_general(dar_b, wa_ref[...], _DOT_DIMS["nt"], preferred_element_type=F32)
        dtw = lax.dot_general(dwr_b, wl_ref[...], _DOT_DIMS["nt"], preferred_element_type=F32)
        dzs_ref[:, 0:c_rw] = dr1[...] + dr2[...]
        dzs_ref[:, c_rw:2 * c_rw] = dk
        dzs_ref[:, 2 * c_rw:3 * c_rw] = dv1[...] + dv2[...]
        dzs_ref[:, 3 * c_rw:c4] = dg_ref[...]
        dzs_ref[:, c4:c4 + lora] = dtw * (1.0 - tw * tw)
        dzs_ref[:, c4 + lora:c4 + 2 * lora] = dal
        if sec > c4 + 2 * lora:
            dzs_ref[:, c4 + 2 * lora:sec] = jnp.zeros((zv.shape[0], sec - c4 - 2 * lora), F32)
        tw_ref[...] = tw.astype(BF16)
        al_ref[...] = al.astype(BF16)
        dwr_ref[...] = dwr_b
        dar_ref[...] = dar_b
        dmu_ref[...] += _colsum(dzs_ref[...] * (zp - zv))

    dzs, tw_b, al_b, dwr_b, dar_b, d_mu, d_w0, d_a0, d_kk, d_ka = _rowcall(
        rw_prep_bwd, "rwkv_prep_bwd", t_dim, bt,
        [(z, 0, sec), dr_c, dr_b, dw_c, dk_c, dk_b, dv_c, dv_b, dkk_c, da_c, dg_rw], rw_consts,
        [(sec, F32), (lora, BF16), (lora, BF16), (c_rw, BF16), (c_rw, BF16)],
        acc_outs=[(1, sec), (1, c_rw), (1, c_rw), (1, c_rw), (1, c_rw)], scratch=[pltpu.VMEM((1, sec), F32)])
    d_wl = _matmul(tw_b, dwr_b, "tn", "d_w_lora", out_dtype=BF16)
    d_wa = _matmul(al_b, dar_b, "tn", "d_a_lora", out_dtype=BF16)

    def shift_bwd(dzs_ref, mu_ref, dz_ref, carry):
        _first_step_zero(carry)
        d = dzs_ref[...]
        nbt = d.shape[0]
        nxt = pltpu.roll(d, nbt - 1, 0)
        rowi = lax.broadcasted_iota(jnp.int32, d.shape, 0)
        nxt = jnp.where(rowi == nbt - 1, carry[...], nxt)
        carry[...] = d[0:1, :]
        m = mu_ref[...]
        dz_ref[...] = (d * (1.0 - m) + nxt * m).astype(BF16)

    (dz_rw,) = _rowcall(shift_bwd, "shift_bwd", t_dim, bt, [dzs], [mu], [(sec, BF16)],
                        scratch=[pltpu.VMEM((1, sec), F32)], reverse=True)

    fox_parts = [dq_f.astype(BF16), dk_f.astype(BF16), dv_f.astype(BF16), dg_fox, dz_fl]
    if sec > 4 * c_fox + LANES:
        fox_parts.append(jnp.zeros((t_dim, sec - 4 * c_fox - LANES), BF16))
    dz = jnp.concatenate([dz_rw] + fox_parts + [dz_gate], axis=1)
    d_wz = _matmul(h, dz, "tn", "d_w_in", out_dtype=BF16, bn=1408)
    dh = _matmul(dz, wz, "nt", "d_h", bk=1408)

    def norm_in_bwd(dh_ref, x_ref, dx1_ref, g_ref, dx_ref, dg1_ref):
        _first_step_zero(dg1_ref)
        xv = x_ref[...]
        rms = lax.rsqrt(jnp.mean(xv * xv, axis=-1, keepdims=True) + NORM_EPS)
        xn = xv * rms
        d = dh_ref[...]
        dg1_ref[...] += _colsum(d * xn)
        gy = d * g_ref[...]
        dx_ref[...] = dx1_ref[...] + rms * (gy - xn * jnp.mean(xn * gy, axis=-1, keepdims=True))

    dx, d_g1 = _rowcall(norm_in_bwd, "norm_in_bwd", t_dim, bt, [dh, x, dx1], [norm_g], [(d_model, F32)],
                        acc_outs=[(1, d_model)])

    full_grads = {"w_in": d_wz, "rw_w_lora_up": d_wl, "rw_a_lora_up": d_wa, "w_up_rwkv": d_wur, "w_up_fox": d_wuf,
                  "w_out": d_wo, "ple_proj": d_pp, "ple_gate_w": d_pg}
    rep_grads = {"norm_g": d_g1, "rw_shift_mu": d_mu[:, 0:c4 + 2 * lora], "rw_w0": d_w0, "rw_a0": d_a0, "rw_k_k": d_kk,
                 "rw_k_a": d_ka, "rw_r_k": d_rk, "rw_ln_g": d_lng, "rw_ln_b": d_lnb, "fox_b_f": d_bf[0:h_fox, 0],
                 "ple_norm_g": d_g2, "final_norm_g": d_g3}
    return loss[0, 0], dx, full_grads, rep_grads


def kernel(x, p, norm_g, w_in, rw_shift_mu, rw_w0, rw_w_lora_up, rw_a0, rw_a_lora_up, rw_k_k, rw_k_a, rw_r_k, rw_ln_g, rw_ln_b, fox_b_f, w_up_rwkv, w_up_fox, w_out, ple_proj, ple_gate_w, ple_norm_g, final_norm_g, loss_target, m_norm_g, m_w_in, m_rw_shift_mu, m_rw_w0, m_rw_w_lora_up, m_rw_a0, m_rw_a_lora_up, m_rw_k_k, m_rw_k_a, m_rw_r_k, m_rw_ln_g, m_rw_ln_b, m_fox_b_f, m_w_up_rwkv, m_w_up_fox, m_w_out, m_ple_proj, m_ple_gate_w, m_ple_norm_g, m_final_norm_g, v_norm_g, v_w_in, v_rw_shift_mu, v_rw_w0, v_rw_w_lora_up, v_rw_a0, v_rw_a_lora_up, v_rw_k_k, v_rw_k_a, v_rw_r_k, v_rw_ln_g, v_rw_ln_b, v_fox_b_f, v_w_up_rwkv, v_w_up_fox, v_w_out, v_ple_proj, v_ple_gate_w, v_ple_norm_g, v_final_norm_g):
    args = locals()
    w = {n: args[n] for n in WEIGHTS}
    mom = {n: args["m_" + n] for n in WEIGHTS}
    var = {n: args["v_" + n] for n in WEIGHTS}

    t_dim, d_model = x.shape[1], x.shape[2]
    c_rw, lora = rw_w0.shape[1], rw_w_lora_up.shape[1]
    h_fox = fox_b_f.shape[1]
    c_fox = h_fox * HEAD
    rw_cols, fox_cols, gate_cols = 4 * c_rw + 2 * lora, 4 * c_fox + h_fox, 2 * d_model
    sec = max(rw_cols, 4 * c_fox + LANES, _round_up(gate_cols, LANES))
    assert c_rw % LANES == 0 and c_fox % LANES == 0 and rw_cols % LANES == 0 and h_fox <= LANES and d_model % LANES == 0
    assert rw_a_lora_up.shape[1] == lora and w_in.shape[2] * NDEV == rw_cols + fox_cols + gate_cols

    gathered = _all_gather([w[n][0].astype(BF16) for n, _ in SHARDED], "gather_weights")
    full = {n: _join_shards(g, ax) for (n, ax), g in zip(SHARDED, gathered)}

    def to_sections(wi):
        pad = lambda a, width: jnp.pad(a, ((0, 0), (0, width - a.shape[1])))
        return jnp.concatenate([pad(wi[:, :rw_cols], sec), pad(wi[:, rw_cols:rw_cols + fox_cols], sec),
                                pad(wi[:, rw_cols + fox_cols:], sec)], axis=1)

    def from_sections(g):
        return jnp.concatenate([g[:, :rw_cols], g[:, sec:sec + fox_cols], g[:, 2 * sec:2 * sec + gate_cols]], axis=1)

    rep = {n: w[n] for n in REPLICATED}
    dims = (t_dim, d_model, c_rw, lora, c_fox, h_fox, sec)
    loss_local, grad_x, full_grads, rep_grads = _local_step(
        x[0], p[0, 0], loss_target[0], to_sections(full["w_in"]), full["rw_w_lora_up"], full["rw_a_lora_up"],
        full["w_up_rwkv"], full["w_up_fox"], full["w_out"], full["ple_proj"], full["ple_gate_w"], rep, dims)
    full_grads["w_in"] = from_sections(full_grads["w_in"])

    rep_sizes = [w[n].size for n in REPLICATED]
    rep_offs = [sum(rep_sizes[:i]) for i in range(len(rep_sizes))]
    pack_rep = lambda tree: _pack_rows(jnp.concatenate([tree[n].astype(F32).reshape(-1) for n in REPLICATED]))
    *recv, small_all = _grad_exchange([_split_shards(full_grads[n], ax) for n, ax in SHARDED], pack_rep(rep_grads),
                                      "exchange_grads")
    kinds = ("grad", "delta", "new_m", "new_v")
    outs = {}
    for (n, _), partials in zip(SHARDED, recv):
        for kind, buf in zip(kinds, _adamw(partials, w[n][0], mom[n][0], var[n][0], "adamw_" + n)):
            outs[kind, n] = buf[None]
    for kind, buf in zip(kinds, _adamw(small_all, pack_rep(w), pack_rep(mom), pack_rep(var), "adamw_replicated")):
        flat = buf.reshape(-1)
        for n, o, s in zip(REPLICATED, rep_offs, rep_sizes):
            outs[kind, n] = flat[o:o + s].reshape(w[n].shape)
    loss = lax.psum(loss_local, MESH_AXES)
    return (loss, grad_x[None], *[outs[kind, n] for kind in ("grad", "delta", "new_m", "new_v") for n in WEIGHTS])
```

```python
import functools
import math

import jax
import jax.numpy as jnp
from jax import lax
from jax.experimental import pallas as pl
from jax.experimental.pallas import tpu as pltpu

F32, BF16 = jnp.float32, jnp.bfloat16
HI = lax.Precision.HIGHEST
LANES = 128
SUBLANES = 8
HEAD = 64
NORM_EPS = 1e-6
GN_EPS = 64e-5
VMEM_LIMIT = 56 * 1024 * 1024
NDEV = 8
PACK_W = 1024
PACK_ALIGN = 16 * PACK_W
MESH_AXES = ("x", "y", "c")
MESH = pl.DeviceIdType.MESH

ADAM_LR, ADAM_B1, ADAM_B2, ADAM_EPS, ADAM_WD, ADAM_STEP = 0.001, 0.9, 0.999, 1e-08, 0.01, 10


def _round_up(n, m):
    return (n + m - 1) // m * m


def _pick(dim, pref, align=LANES):
    if dim <= pref:
        return dim
    best = None
    for cand in range(align, pref + 1, align):
        if dim % cand == 0:
            best = cand
    return dim if best is None else best


def _params(*sem):
    return pltpu.CompilerParams(dimension_semantics=sem, vmem_limit_bytes=VMEM_LIMIT)


def _sigmoid(v):
    return jax.nn.sigmoid(v)


def _log_sigmoid(v):
    return jnp.minimum(v, 0.0) - jnp.log(1.0 + jnp.exp(-jnp.abs(v)))


_DOT_DIMS = {"nn": (((1,), (0,)), ((), ())), "nt": (((1,), (1,)), ((), ())), "tn": (((0,), (0,)), ((), ()))}


def _matmul(a, b, mode, name, out_dtype=F32, bm=512, bn=1024, bk=512):
    if mode == "tn":
        k_dim, m_dim = a.shape
    else:
        m_dim, k_dim = a.shape
    n_dim = b.shape[0] if mode == "nt" else b.shape[1]
    bm, bn, bk = _pick(m_dim, bm), _pick(n_dim, bn), _pick(k_dim, bk)
    nk = k_dim // bk

    def body(a_ref, b_ref, o_ref, acc_ref):
        k = pl.program_id(2)

        @pl.when(k == 0)
        def _():
            acc_ref[...] = jnp.zeros_like(acc_ref)

        acc_ref[...] += lax.dot_general(a_ref[...].astype(BF16), b_ref[...].astype(BF16), _DOT_DIMS[mode],
                                        preferred_element_type=F32)

        @pl.when(k == nk - 1)
        def _():
            o_ref[...] = acc_ref[...].astype(o_ref.dtype)

    if mode == "tn":
        a_spec = pl.BlockSpec((bk, bm), lambda i, j, k: (k, i))
    else:
        a_spec = pl.BlockSpec((bm, bk), lambda i, j, k: (i, k))
    if mode == "nt":
        b_spec = pl.BlockSpec((bn, bk), lambda i, j, k: (j, k))
    else:
        b_spec = pl.BlockSpec((bk, bn), lambda i, j, k: (k, j))
    return pl.pallas_call(
        body, name=name, grid=(m_dim // bm, n_dim // bn, nk),
        in_specs=[a_spec, b_spec], out_specs=pl.BlockSpec((bm, bn), lambda i, j, k: (i, j)),
        out_shape=jax.ShapeDtypeStruct((m_dim, n_dim), out_dtype),
        scratch_shapes=[pltpu.VMEM((bm, bn), F32)],
        compiler_params=_params("parallel", "parallel", "arbitrary"),
    )(a, b)


def _rowcall(body, name, t_dim, bt, row_ins, const_ins, row_outs, acc_outs=(), scratch=(), reverse=False):
    nt = t_dim // bt

    def rmap(i):
        return nt - 1 - i if reverse else i

    in_specs, args = [], []
    for item in row_ins:
        arr, cb, w = item if isinstance(item, tuple) else (item, 0, item.shape[1])
        in_specs.append(pl.BlockSpec((bt, w), lambda i, cb=cb: (rmap(i), cb)))
        args.append(arr)
    for arr in const_ins:
        in_specs.append(pl.BlockSpec(arr.shape, lambda i, nd=arr.ndim: (0,) * nd))
        args.append(arr)
    out_specs = [pl.BlockSpec((bt, w), lambda i: (rmap(i), 0)) for w, _ in row_outs]
    out_shape = [jax.ShapeDtypeStruct((t_dim, w), dt) for w, dt in row_outs]
    for shp in acc_outs:
        out_specs.append(pl.BlockSpec(shp, lambda i, nd=len(shp): (0,) * nd))
        out_shape.append(jax.ShapeDtypeStruct(shp, F32))
    return pl.pallas_call(
        body, name=name, grid=(nt,), in_specs=in_specs, out_specs=out_specs, out_shape=out_shape,
        scratch_shapes=list(scratch), compiler_params=_params("arbitrary"),
    )(*args)


def _first_step_zero(*refs):
    @pl.when(pl.program_id(0) == 0)
    def _():
        for r in refs:
            r[...] = jnp.zeros_like(r)


def _colsum(v):
    return jnp.sum(v, axis=0, keepdims=True)


def _headsum(v, e):
    parts = [jnp.dot(v[:, p * LANES:(p + 1) * LANES], e, precision=HI, preferred_element_type=F32)
             for p in range(v.shape[1] // LANES)]
    return parts[0] if len(parts) == 1 else jnp.concatenate(parts, axis=1)


def _shift_down(v, carry_ref):
    bt = v.shape[0]
    prev = pltpu.roll(v, 1, 0)
    row = lax.broadcasted_iota(jnp.int32, v.shape, 0)
    prev = jnp.where(row == 0, carry_ref[...], prev)
    carry_ref[...] = v[bt - 1:bt, :]
    return prev


def _pair_consts():
    lane = lax.broadcasted_iota(jnp.int32, (1, LANES), 1)
    m0 = (lane < HEAD).astype(F32)
    m1 = 1.0 - m0
    sub = lax.broadcasted_iota(jnp.int32, (HEAD, LANES), 0)
    lane2 = lax.broadcasted_iota(jnp.int32, (HEAD, LANES), 1)
    i0 = (lane2 == sub).astype(F32)
    i1 = (lane2 == sub + HEAD).astype(F32)
    return m0, m1, i0, i1


def _head_masks():
    lane = lax.broadcasted_iota(jnp.int32, (1, LANES), 1)
    first = (lane < HEAD).astype(F32)
    return first, 1.0 - first


def _head_ones(dtype=BF16):
    lane = jnp.arange(LANES)
    return (lane[:, None] // HEAD == lane[None, :] // HEAD).astype(dtype)


def _lanesum(v):
    return jnp.sum(v, axis=1, keepdims=True)


def _split_bf16(v):
    hi = v.astype(BF16).astype(F32)
    rest = v - hi
    mid = rest.astype(BF16).astype(F32)
    return hi, mid, (rest - mid).astype(BF16).astype(F32)


def _col_pair(parts, j, i01, e_bf16):
    lhs = jnp.concatenate([(part[j:j + 1] * i01).astype(BF16) for part in parts], axis=0)
    out = jnp.dot(lhs, e_bf16, preferred_element_type=F32)
    return (out[0:HEAD] + out[HEAD:2 * HEAD]) + out[2 * HEAD:3 * HEAD]


def _col_tiles(out_ref, src_ref, rows, lanes, i01, e_bf16):
    for sl in lanes:
        parts = _split_bf16(src_ref[rows, sl])
        for j in range(SUBLANES):
            out_ref[j, :, sl] = _col_pair(parts, j, i01, e_bf16)


def _row_pair(c0, c1, i0, i1):
    return _colsum(c0 * i0 + c1 * i1)


def _scan_fwd(r, w, k, v, kk, a, tc, npb):
    t_dim, c_dim = r.shape
    wb = LANES * npb
    tc = _pick(t_dim, tc, SUBLANES)

    def body(r_ref, w_ref, k_ref, v_ref, kk_ref, a_ref, e_ref, y_ref, sall_ref, sfin_ref, s_ref):
        @pl.when(pl.program_id(1) == 0)
        def _():
            s_ref[...] = jnp.zeros_like(s_ref)

        m0, m1, i0, i1 = _pair_consts()
        i01 = i0 + i1
        e = e_ref[...]
        sub8 = lax.broadcasted_iota(jnp.int32, (SUBLANES, LANES), 0)
        lanes = [slice(q * LANES, (q + 1) * LANES) for q in range(npb)]
        ng = tc // SUBLANES

        def lanesums(tiles):
            sums = _lanesum(jnp.concatenate(tiles, axis=0))
            return [sums[i * HEAD:(i + 1) * HEAD] for i in range(len(tiles))]

        def halves(s, row):
            return [s * (row * m0), s * (row * m1)]

        def group(gi, carry):
            base = pl.multiple_of(gi * SUBLANES, SUBLANES)
            rows = pl.ds(base, SUBLANES)
            s = list(carry)
            r8 = [r_ref[rows, sl] for sl in lanes]
            w8 = [w_ref[rows, sl] for sl in lanes]
            k8 = [k_ref[rows, sl] for sl in lanes]
            kk8 = [kk_ref[rows, sl] for sl in lanes]
            b8 = [kk8[q] * a_ref[rows, lanes[q]] for q in range(npb)]
            v8 = [_split_bf16(v_ref[rows, sl]) for sl in lanes]
            y8 = [jnp.zeros((SUBLANES, LANES), F32)] * npb
            for j in range(SUBLANES + 1):
                one, before = slice(j, j + 1), slice(j - 1, j)
                tiles = []
                for q in range(npb):
                    if j < SUBLANES:
                        tiles += halves(s[q], kk8[q][one])
                    if j > 0:
                        tiles += halves(s[q], r8[q][before])
                cols = lanesums(tiles)
                per = len(tiles) // npb
                for q in range(npb):
                    mine = cols[q * per:(q + 1) * per]
                    if j > 0:
                        y8[q] = jnp.where(sub8 == j - 1, _row_pair(mine[-2], mine[-1], i0, i1), y8[q])
                    if j < SUBLANES:
                        sall_ref[base + j, :, lanes[q]] = s[q]
                        sb = mine[0] * m0 + mine[1] * m1
                        s[q] = s[q] * w8[q][one] - sb * b8[q][one] + _col_pair(v8[q], j, i01, e) * k8[q][one]
            for q in range(npb):
                y_ref[rows, lanes[q]] = y8[q]
            return tuple(s)

        init = tuple(s_ref[:, q * LANES:(q + 1) * LANES] for q in range(npb))
        fin = lax.fori_loop(0, ng, group, init)
        for q in range(npb):
            s_ref[:, q * LANES:(q + 1) * LANES] = fin[q]
            sfin_ref[:, q * LANES:(q + 1) * LANES] = fin[q]

    row = pl.BlockSpec((tc, wb), lambda p, c: (c, p))
    return pl.pallas_call(
        body, name="rwkv_scan_fwd", grid=(c_dim // wb, t_dim // tc),
        in_specs=[row] * 6 + [pl.BlockSpec((LANES, LANES), lambda p, c: (0, 0))],
        out_specs=[row, pl.BlockSpec((tc, HEAD, wb), lambda p, c: (c, 0, p)), pl.BlockSpec((HEAD, wb), lambda p, c: (0, p))],
        out_shape=[jax.ShapeDtypeStruct((t_dim, c_dim), F32), jax.ShapeDtypeStruct((t_dim, HEAD, c_dim), F32),
                   jax.ShapeDtypeStruct((HEAD, c_dim), F32)],
        scratch_shapes=[pltpu.VMEM((HEAD, wb), F32)],
        compiler_params=_params("parallel", "arbitrary"),
    )(r, w, k, v, kk, a, _head_ones())


def _scan_bwd(r, w, k, v, kk, a, dy, sall, sfin, tc, npb):
    t_dim, c_dim = r.shape
    wb = LANES * npb
    tc = _pick(t_dim, tc, SUBLANES)
    nc = t_dim // tc

    def body(r_ref, w_ref, k_ref, v_ref, kk_ref, a_ref, dy_ref, sall_ref, sfin_ref, e_ref,
             dr_ref, dw_ref, dk_ref, dv_ref, dkk_ref, da_ref, ds_ref, sn_ref):
        @pl.when(pl.program_id(1) == 0)
        def _():
            ds_ref[...] = jnp.zeros_like(ds_ref)
            sn_ref[...] = sfin_ref[...]

        m0, m1, i0, i1 = _pair_consts()
        i01 = i0 + i1
        e = e_ref[...]
        sub8 = lax.broadcasted_iota(jnp.int32, (SUBLANES, LANES), 0)
        lanes = [slice(q * LANES, (q + 1) * LANES) for q in range(npb)]
        ng = tc // SUBLANES

        def halves(s, row):
            return [s * (row * m0), s * (row * m1)]

        def group(gi, carry):
            base = pl.multiple_of((ng - 1 - gi) * SUBLANES, SUBLANES)
            rows = pl.ds(base, SUBLANES)
            ds = list(carry)
            r8 = [r_ref[rows, sl] for sl in lanes]
            w8 = [w_ref[rows, sl] for sl in lanes]
            k8 = [k_ref[rows, sl] for sl in lanes]
            kk8 = [kk_ref[rows, sl] for sl in lanes]
            a8 = [a_ref[rows, sl] for sl in lanes]
            v8 = [_split_bf16(v_ref[rows, sl]) for sl in lanes]
            dy8 = [_split_bf16(dy_ref[rows, sl]) for sl in lanes]
            zero8 = jnp.zeros((SUBLANES, LANES), F32)
            dr8, dw8, dk8, dv8, dkk8, da8 = ([zero8] * npb for _ in range(6))
            for j in reversed(range(SUBLANES)):
                one = slice(j, j + 1)
                here = sub8 == j
                d, s_prev, tiles = [], [], []
                for q in range(npb):
                    dyb = _col_pair(dy8[q], j, i01, e)
                    s_prev.append(sall_ref[base + j, :, lanes[q]])
                    dr8[q] = jnp.where(here, _colsum(sn_ref[:, lanes[q]] * dyb), dr8[q])
                    sn_ref[:, lanes[q]] = s_prev[q]
                    d.append(ds[q] + dyb * r8[q][one])
                for q in range(npb):
                    tiles += halves(d[q], kk8[q][one] * a8[q][one])
                for q in range(npb):
                    tiles += halves(d[q], k8[q][one]) + halves(s_prev[q], kk8[q][one])
                sums = _lanesum(jnp.concatenate(tiles, axis=0))
                cols = [sums[i * HEAD:(i + 1) * HEAD] for i in range(len(tiles))]
                for q in range(npb):
                    kkr, ar = kk8[q][one], a8[q][one]
                    dsb = -(cols[2 * q] * m0 + cols[2 * q + 1] * m1)
                    ds[q] = d[q] * w8[q][one] + dsb * kkr
                    rest = cols[2 * npb + 4 * q:2 * npb + 4 * q + 4]
                    dv8[q] = jnp.where(here, _row_pair(rest[0], rest[1], i0, i1), dv8[q])
                    sb = rest[2] * m0 + rest[3] * m1
                    db = -_colsum(d[q] * sb)
                    dk8[q] = jnp.where(here, _colsum(d[q] * _col_pair(v8[q], j, i01, e)), dk8[q])
                    dw8[q] = jnp.where(here, _colsum(d[q] * s_prev[q]), dw8[q])
                    dkk8[q] = jnp.where(here, _colsum(s_prev[q] * dsb) + db * ar, dkk8[q])
                    da8[q] = jnp.where(here, db * kkr, da8[q])
            for q in range(npb):
                sl = lanes[q]
                dr_ref[rows, sl], dw_ref[rows, sl], dk_ref[rows, sl] = dr8[q], dw8[q], dk8[q]
                dv_ref[rows, sl], dkk_ref[rows, sl], da_ref[rows, sl] = dv8[q], dkk8[q], da8[q]
            return tuple(ds)

        init = tuple(ds_ref[:, q * LANES:(q + 1) * LANES] for q in range(npb))
        fin = lax.fori_loop(0, ng, group, init)
        for q in range(npb):
            ds_ref[:, q * LANES:(q + 1) * LANES] = fin[q]

    row = pl.BlockSpec((tc, wb), lambda p, c: (nc - 1 - c, p))
    return pl.pallas_call(
        body, name="rwkv_scan_bwd", grid=(c_dim // wb, nc),
        in_specs=[row] * 7 + [pl.BlockSpec((tc, HEAD, wb), lambda p, c: (nc - 1 - c, 0, p)),
                              pl.BlockSpec((HEAD, wb), lambda p, c: (0, p)),
                              pl.BlockSpec((LANES, LANES), lambda p, c: (0, 0))],
        out_specs=[row] * 6, out_shape=[jax.ShapeDtypeStruct((t_dim, c_dim), F32)] * 6,
        scratch_shapes=[pltpu.VMEM((HEAD, wb), F32)] * 2,
        compiler_params=_params("parallel", "arbitrary"),
    )(r, w, k, v, kk, a, dy, sall, sfin, _head_ones())


def _fox_fwd(z, ct, cq, q_cb, k_cb, v_cb, n_pairs, blk):
    t_dim = z.shape[0]
    blk = _pick(t_dim, blk)
    nq = t_dim // blk
    scale = HEAD ** -0.5

    def body(q_ref, k_ref, v_ref, ct_ref, cq_ref, o_ref, lse_ref):
        i = pl.program_id(1)
        rowi = lax.broadcasted_iota(jnp.int32, (blk, blk), 0)
        coli = lax.broadcasted_iota(jnp.int32, (blk, blk), 1)
        masks = _head_masks()
        qv = q_ref[...]
        qs = [(qv * mk).astype(BF16) for mk in masks]
        cqs = [cq_ref[:, hh:hh + 1] for hh in range(2)]

        def kv_step(j, carry, masked):
            rows = pl.ds(pl.multiple_of(j * blk, blk), blk)
            kb = k_ref[rows, :].astype(BF16)
            vv = v_ref[rows, :]
            stats, acc = list(carry[:4]), carry[4]
            rescale, add = 0.0, 0.0
            for hh in range(2):
                m, l = stats[2 * hh], stats[2 * hh + 1]
                s = (lax.dot_general(qs[hh], kb, _DOT_DIMS["nt"], preferred_element_type=F32) * scale
                     + (cqs[hh] - ct_ref[hh:hh + 1, rows]))
                if masked:
                    s = jnp.where(rowi >= coli, s, -jnp.inf)
                m_new = jnp.maximum(m, jnp.max(s, axis=1, keepdims=True))
                alpha = jnp.exp(m - m_new)
                pr = jnp.exp(s - m_new)
                stats[2 * hh], stats[2 * hh + 1] = m_new, l * alpha + jnp.sum(pr, axis=1, keepdims=True)
                rescale = rescale + alpha * masks[hh]
                add = add + jnp.dot(pr.astype(BF16), (vv * masks[hh]).astype(BF16), preferred_element_type=F32)
            return (*stats, acc * rescale + add)

        neg, zero = jnp.full((blk, 1), -jnp.inf, F32), jnp.zeros((blk, 1), F32)
        carry = lax.fori_loop(0, i, functools.partial(kv_step, masked=False),
                              (neg, zero, neg, zero, jnp.zeros((blk, LANES), F32)))
        m0, l0, m1, l1, acc = kv_step(i, carry, True)
        o_ref[...] = acc * (masks[0] / l0 + masks[1] / l1)
        lse_ref[:, 0:1] = m0 + jnp.log(l0)
        lse_ref[:, 1:2] = m1 + jnp.log(l1)

    full = lambda cb: pl.BlockSpec((t_dim, LANES), lambda p, i, cb=cb: (0, cb + p))
    return pl.pallas_call(
        body, name="fox_attn_fwd", grid=(n_pairs, nq),
        in_specs=[pl.BlockSpec((blk, LANES), lambda p, i: (i, q_cb + p)), full(k_cb), full(v_cb),
                  pl.BlockSpec((None, SUBLANES, t_dim), lambda p, i: (p, 0, 0)),
                  pl.BlockSpec((None, blk, 2), lambda p, i: (p, i, 0))],
        out_specs=[pl.BlockSpec((blk, LANES), lambda p, i: (i, p)), pl.BlockSpec((None, blk, 2), lambda p, i: (p, i, 0))],
        out_shape=[jax.ShapeDtypeStruct((t_dim, n_pairs * LANES), F32), jax.ShapeDtypeStruct((n_pairs, t_dim, 2), F32)],
        compiler_params=_params("parallel", "arbitrary"),
    )(z, z, z, ct, cq)


def _fox_rowdot(z, ct, cq, do, lse, q_cb, k_cb, v_cb, n_pairs, blk):
    t_dim = z.shape[0]
    blk = _pick(t_dim, blk)
    scale = HEAD ** -0.5

    def body(q_ref, k_ref, v_ref, ct_ref, cq_ref, do_ref, lse_ref, out_ref):
        i = pl.program_id(1)
        rowi = lax.broadcasted_iota(jnp.int32, (blk, blk), 0)
        coli = lax.broadcasted_iota(jnp.int32, (blk, blk), 1)
        masks = _head_masks()
        qv, dov = q_ref[...], do_ref[...]
        qs = [(qv * mk).astype(BF16) for mk in masks]
        dos = [(dov * mk).astype(BF16) for mk in masks]
        cqs = [cq_ref[:, hh:hh + 1] for hh in range(2)]
        lses = [lse_ref[:, hh:hh + 1] for hh in range(2)]

        def kv_step(j, carry, masked):
            rows = pl.ds(pl.multiple_of(j * blk, blk), blk)
            kb = k_ref[rows, :].astype(BF16)
            vb = v_ref[rows, :].astype(BF16)
            out = []
            for hh in range(2):
                num, den = carry[2 * hh], carry[2 * hh + 1]
                s = (lax.dot_general(qs[hh], kb, _DOT_DIMS["nt"], preferred_element_type=F32) * scale
                     + (cqs[hh] - ct_ref[hh:hh + 1, rows]))
                pr = jnp.exp(s - lses[hh])
                if masked:
                    pr = jnp.where(rowi >= coli, pr, 0.0)
                dp = lax.dot_general(dos[hh], vb, _DOT_DIMS["nt"], preferred_element_type=F32)
                out += [num + jnp.sum(pr * dp, axis=1, keepdims=True), den + jnp.sum(pr, axis=1, keepdims=True)]
            return tuple(out)

        zero = jnp.zeros((blk, 1), F32)
        carry = lax.fori_loop(0, i, functools.partial(kv_step, masked=False), (zero, zero, zero, zero))
        num0, den0, num1, den1 = kv_step(i, carry, True)
        out_ref[:, 0:1] = num0 / den0
        out_ref[:, 1:2] = num1 / den1

    full = lambda cb: pl.BlockSpec((t_dim, LANES), lambda p, i, cb=cb: (0, cb + p))
    return pl.pallas_call(
        body, name="fox_attn_rowdot", grid=(n_pairs, t_dim // blk),
        in_specs=[pl.BlockSpec((blk, LANES), lambda p, i: (i, q_cb + p)), full(k_cb), full(v_cb),
                  pl.BlockSpec((None, SUBLANES, t_dim), lambda p, i: (p, 0, 0)),
                  pl.BlockSpec((None, blk, 2), lambda p, i: (p, i, 0)),
                  pl.BlockSpec((blk, LANES), lambda p, i: (i, p)),
                  pl.BlockSpec((None, blk, 2), lambda p, i: (p, i, 0))],
        out_specs=pl.BlockSpec((None, blk, 2), lambda p, i: (p, i, 0)),
        out_shape=jax.ShapeDtypeStruct((n_pairs, t_dim, 2), F32),
        compiler_params=_params("parallel", "arbitrary"),
    )(z, z, z, ct, cq, do, lse)


def _fox_bwd(z, ct, cq, rowdot, do, lse, q_cb, k_cb, v_cb, n_pairs, blk):
    t_dim = z.shape[0]
    blk = _pick(t_dim, blk)
    nb = t_dim // blk
    scale = HEAD ** -0.5

    def body(q_ref, k_ref, v_ref, ct_ref, cq_ref, rd_ref, do_ref, lse_ref, dq_ref, dk_ref, dv_ref, dc_ref):
        j = pl.program_id(1)

        @pl.when(j == 0)
        def _():
            dq_ref[...] = jnp.zeros_like(dq_ref)

        rowi = lax.broadcasted_iota(jnp.int32, (blk, blk), 0)
        coli = lax.broadcasted_iota(jnp.int32, (blk, blk), 1)
        krows = pl.ds(pl.multiple_of(j * blk, blk), blk)
        masks = _head_masks()
        kv, vb = k_ref[...], v_ref[...].astype(BF16)
        kb = kv.astype(BF16)
        ks = [(kv * mk).astype(BF16) for mk in masks]
        cks = [ct_ref[hh:hh + 1, krows] for hh in range(2)]

        def q_step(i, carry, masked):
            dk, dv, dcs = carry[0], carry[1], list(carry[2:])
            rows = pl.ds(pl.multiple_of(i * blk, blk), blk)
            qv, dov = q_ref[rows, :], do_ref[rows, :]
            dq = 0.0
            for hh in range(2):
                qh = (qv * masks[hh]).astype(BF16)
                doh = (dov * masks[hh]).astype(BF16)
                s = (lax.dot_general(qh, kb, _DOT_DIMS["nt"], preferred_element_type=F32) * scale
                     + (cq_ref[rows, hh:hh + 1] - cks[hh]))
                pr = jnp.exp(s - lse_ref[rows, hh:hh + 1])
                if masked:
                    pr = jnp.where(rowi >= coli, pr, 0.0)
                dv = dv + lax.dot_general(pr.astype(BF16), doh, _DOT_DIMS["tn"], preferred_element_type=F32)
                dp = lax.dot_general(doh, vb, _DOT_DIMS["nt"], preferred_element_type=F32)
                ds = pr * (dp - rd_ref[rows, hh:hh + 1])
                dsb = ds.astype(BF16)
                dq = dq + jnp.dot(dsb, ks[hh], preferred_element_type=F32)
                dk = dk + lax.dot_general(dsb, qh, _DOT_DIMS["tn"], preferred_element_type=F32)
                dcs[hh] = dcs[hh] - _colsum(ds)
            dq_ref[rows, :] += dq * scale
            return (dk, dv, *dcs)

        zero_row = jnp.zeros((1, blk), F32)
        init = (jnp.zeros((blk, LANES), F32), jnp.zeros((blk, LANES), F32), zero_row, zero_row)
        carry = q_step(j, init, True)
        dk, dv, dc0, dc1 = lax.fori_loop(j + 1, nb, functools.partial(q_step, masked=False), carry)
        dk_ref[...] = dk * scale
        dv_ref[...] = dv
        dc_ref[0:1, :] = dc0
        dc_ref[1:2, :] = dc1
        dc_ref[2:SUBLANES, :] = jnp.zeros((SUBLANES - 2, blk), F32)

    full = lambda: pl.BlockSpec((t_dim, LANES), lambda p, j: (0, p))
    blkspec = pl.BlockSpec((blk, LANES), lambda p, j: (j, p))
    return pl.pallas_call(
        body, name="fox_attn_bwd", grid=(n_pairs, nb),
        in_specs=[pl.BlockSpec((t_dim, LANES), lambda p, j: (0, q_cb + p)),
                  pl.BlockSpec((blk, LANES), lambda p, j: (j, k_cb + p)),
                  pl.BlockSpec((blk, LANES), lambda p, j: (j, v_cb + p)),
                  pl.BlockSpec((None, SUBLANES, t_dim), lambda p, j: (p, 0, 0)),
                  pl.BlockSpec((None, t_dim, 2), lambda p, j: (p, 0, 0)),
                  pl.BlockSpec((None, t_dim, 2), lambda p, j: (p, 0, 0)), full(),
                  pl.BlockSpec((None, t_dim, 2), lambda p, j: (p, 0, 0))],
        out_specs=[full(), blkspec, blkspec, pl.BlockSpec((None, SUBLANES, blk), lambda p, j: (p, 0, j))],
        out_shape=[jax.ShapeDtypeStruct((t_dim, n_pairs * LANES), F32)] * 3
                  + [jax.ShapeDtypeStruct((n_pairs, SUBLANES, t_dim), F32)],
        compiler_params=_params("parallel", "arbitrary"),
    )(z, z, z, ct, cq, rowdot, do, lse)


HBM_SPEC = pl.BlockSpec(memory_space=pltpu.HBM)


def _all_gather(shards, name):
    n = len(shards)

    def body(*refs):
        x_refs, out_refs = refs[:n], refs[n:2 * n]
        send_sems, recv_sems, local_sems = refs[2 * n:]
        x, y, c = lax.axis_index("x"), lax.axis_index("y"), lax.axis_index("c")
        me, sibling = (x, y, c), (x, y, 1 - c)
        chips = [(1 - x, y), (x, 1 - y), (1 - x, 1 - y)]

        def copy(a, k, block, to, from_input=False):
            px, py, pc = block
            slot = out_refs[a].at[4 * px + 2 * py + pc]
            return pltpu.make_async_remote_copy(
                src_ref=x_refs[a] if from_input else slot, dst_ref=slot,
                send_sem=send_sems.at[7 * a + k], recv_sem=recv_sems.at[7 * a + k], device_id=to, device_id_type=MESH)

        mine = [pltpu.make_async_copy(x_refs[a], out_refs[a].at[4 * x + 2 * y + c], local_sems.at[a]) for a in range(n)]
        for cp in mine:
            cp.start()
        first = []
        for a in range(n):
            first += [copy(a, 1 + j, me, (*chip, c), from_input=True) for j, chip in enumerate(chips)]
            first.append(copy(a, 0, me, sibling, from_input=True))
        for cp in first:
            cp.start()
        passed = []
        for a in range(n):
            for j, chip in enumerate(chips):
                copy(a, 1 + j, (*chip, c), me).wait_recv()
                passed.append(copy(a, 4 + j, (*chip, c), sibling))
                passed[-1].start()
        for a in range(n):
            copy(a, 0, sibling, me).wait_recv()
            for j, chip in enumerate(chips):
                copy(a, 4 + j, (*chip, 1 - c), me).wait_recv()
        for cp in first + passed:
            cp.wait_send()
        for cp in mine:
            cp.wait()

    return pl.pallas_call(
        body, name=name, out_shape=[jax.ShapeDtypeStruct((NDEV,) + s.shape, s.dtype) for s in shards],
        in_specs=[HBM_SPEC] * n, out_specs=[HBM_SPEC] * n,
        scratch_shapes=[pltpu.SemaphoreType.DMA((7 * n,)), pltpu.SemaphoreType.DMA((7 * n,)),
                        pltpu.SemaphoreType.DMA((n,))],
    )(*shards)


def _grad_exchange(parts, small, name):
    n = len(parts)

    def body(*refs):
        g_refs, s_ref = refs[:n], refs[n]
        recv_refs, sall_ref = refs[n + 1:2 * n + 1], refs[2 * n + 1]
        send_sems, recv_sems, local_sems = refs[2 * n + 2:]
        x, y, c = lax.axis_index("x"), lax.axis_index("y"), lax.axis_index("c")
        me = 4 * x + 2 * y + c
        own = [pltpu.make_async_copy(g_refs[a].at[me], recv_refs[a].at[me], local_sems.at[a]) for a in range(n)]
        own.append(pltpu.make_async_copy(s_ref, sall_ref.at[me], local_sems.at[n]))
        for cp in own:
            cp.start()
        sends, recvs = [], []
        for a in range(n + 1):
            for k in range(1, NDEV):
                px = 1 - x if k & 4 else x
                py = 1 - y if k & 2 else y
                pc = 1 - c if k & 1 else c
                peer = 4 * px + 2 * py + pc
                sems = dict(send_sem=send_sems.at[7 * a + k - 1], recv_sem=recv_sems.at[7 * a + k - 1],
                            device_id=(px, py, pc), device_id_type=MESH)
                if a < n:
                    src, to_me, from_peer = g_refs[a].at[peer], recv_refs[a].at[me], recv_refs[a].at[peer]
                else:
                    src, to_me, from_peer = s_ref, sall_ref.at[me], sall_ref.at[peer]
                sends.append(pltpu.make_async_remote_copy(src_ref=src, dst_ref=to_me, **sems))
                recvs.append(pltpu.make_async_remote_copy(src_ref=src, dst_ref=from_peer, **sems))
        for cp in sends:
            cp.start()
        for cp in recvs:
            cp.wait_recv()
        for cp in sends:
            cp.wait_send()
        for cp in own:
            cp.wait()

    return pl.pallas_call(
        body, name=name,
        out_shape=[jax.ShapeDtypeStruct(a.shape, a.dtype) for a in parts]
                  + [jax.ShapeDtypeStruct((NDEV,) + small.shape, small.dtype)],
        in_specs=[HBM_SPEC] * (n + 1), out_specs=[HBM_SPEC] * (n + 1),
        scratch_shapes=[pltpu.SemaphoreType.DMA((7 * (n + 1),)), pltpu.SemaphoreType.DMA((7 * (n + 1),)),
                        pltpu.SemaphoreType.DMA((n + 1,))],
    )(*parts, small)


def _adamw(partials, w, m, v, name):
    rows, width = w.shape
    br = _pick(rows, 128, 2 * SUBLANES)

    def body(p_ref, w_ref, m_ref, v_ref, g_out, d_out, m_out, v_out):
        g = p_ref[0].astype(F32)
        for d in range(1, NDEV):
            g = g + p_ref[d].astype(F32)
        mn = ADAM_B1 * m_ref[...] + (1.0 - ADAM_B1) * g
        vn = ADAM_B2 * v_ref[...] + (1.0 - ADAM_B2) * jnp.square(g)
        m_hat = mn / (1.0 - ADAM_B1 ** ADAM_STEP)
        v_hat = vn / (1.0 - ADAM_B2 ** ADAM_STEP)
        g_out[...] = g
        d_out[...] = -ADAM_LR * (m_hat / (jnp.sqrt(v_hat) + ADAM_EPS) + ADAM_WD * w_ref[...])
        m_out[...] = mn
        v_out[...] = vn

    blk = pl.BlockSpec((br, width), lambda i: (i, 0))
    return pl.pallas_call(
        body, name=name, grid=(rows // br,),
        in_specs=[pl.BlockSpec((NDEV, br, width), lambda i: (0, i, 0)), blk, blk, blk],
        out_specs=[blk] * 4, out_shape=[jax.ShapeDtypeStruct((rows, width), F32)] * 4,
        compiler_params=_params("parallel"),
    )(partials, w, m, v)


def _pack_rows(flat):
    n = flat.shape[0]
    padded = _round_up(n, PACK_ALIGN)
    return jnp.pad(flat, (0, padded - n)).reshape(padded // PACK_W, PACK_W)


def _split_shards(full, axis):
    rows, cols = full.shape
    if axis == 0:
        return full.reshape(NDEV, rows // NDEV, cols)
    width = cols // NDEV
    return jnp.stack([full[:, d * width:(d + 1) * width] for d in range(NDEV)])


def _join_shards(blocks, axis):
    if axis == 0:
        return blocks.reshape(-1, blocks.shape[2])
    return jnp.concatenate([blocks[d] for d in range(NDEV)], axis=1)


SHARDED = (("w_in", 1), ("rw_w_lora_up", 1), ("rw_a_lora_up", 1), ("w_up_rwkv", 1), ("w_up_fox", 1),
           ("w_out", 0), ("ple_proj", 1), ("ple_gate_w", 0))
REPLICATED = ("norm_g", "rw_shift_mu", "rw_w0", "rw_a0", "rw_k_k", "rw_k_a", "rw_r_k", "rw_ln_g", "rw_ln_b",
              "fox_b_f", "ple_norm_g", "final_norm_g")
WEIGHTS = ("norm_g", "w_in", "rw_shift_mu", "rw_w0", "rw_w_lora_up", "rw_a0", "rw_a_lora_up", "rw_k_k", "rw_k_a",
           "rw_r_k", "rw_ln_g", "rw_ln_b", "fox_b_f", "w_up_rwkv", "w_up_fox", "w_out", "ple_proj", "ple_gate_w",
           "ple_norm_g", "final_norm_g")


def _local_step(x, p, tgt, wz, wl, wa, wur, wuf, wo, pp, pg, rep, dims):
    t_dim, d_model, c_rw, lora, c_fox, h_fox, sec = dims
    bt = _pick(t_dim, 128, SUBLANES)
    n_pairs = c_fox // LANES
    row = lambda a: a.reshape(1, -1)
    norm_g, mu, w0, a0 = row(rep["norm_g"]), row(rep["rw_shift_mu"]), row(rep["rw_w0"]), row(rep["rw_a0"])
    k_k, k_a, r_k = row(rep["rw_k_k"]), row(rep["rw_k_a"]), row(rep["rw_r_k"])
    ln_g, ln_b = row(rep["rw_ln_g"]), row(rep["rw_ln_b"])
    g2, g3 = row(rep["ple_norm_g"]), row(rep["final_norm_g"])
    b_f = jnp.pad(row(rep["fox_b_f"]), ((0, 0), (0, LANES - h_fox)))
    e_head = _head_ones(F32)
    c4 = 4 * c_rw
    inv_d = 1.0 / d_model
    decay_k = math.exp(-0.5)

    def norm_in(x_ref, g_ref, h_ref):
        xv = x_ref[...]
        rms = lax.rsqrt(jnp.mean(xv * xv, axis=-1, keepdims=True) + NORM_EPS)
        h_ref[...] = (xv * rms * g_ref[...]).astype(BF16)

    (h,) = _rowcall(norm_in, "norm_in", t_dim, bt, [x], [norm_g], [(d_model, BF16)])
    z = _matmul(h, wz, "nn", "proj_in", bn=1408)

    def rw_values(zs, w0_ref, wl_ref, a0_ref, wa_ref, kk_ref, ka_ref, e_ref):
        k = zs[:, c_rw:2 * c_rw]
        tw = jnp.tanh(zs[:, c4:c4 + lora])
        al = zs[:, c4 + lora:c4 + 2 * lora]
        sw = _sigmoid(w0_ref[...] + jnp.dot(tw.astype(BF16), wl_ref[...], preferred_element_type=F32))
        decay = jnp.exp(-decay_k * sw)
        a = _sigmoid(a0_ref[...] + jnp.dot(al.astype(BF16), wa_ref[...], preferred_element_type=F32))
        kk0 = k * kk_ref[...]
        nrm = jnp.sqrt(_headsum(kk0 * kk0, e_ref[...]))
        inv = 1.0 / jnp.maximum(nrm, 1e-12)
        k2 = k * (1.0 + (a - 1.0) * ka_ref[...])
        return k, tw, al, sw, decay, a, kk0, nrm, inv, k2

    def rw_prep(z_ref, mu_ref, w0_ref, wl_ref, a0_ref, wa_ref, kk_ref, ka_ref, e_ref,
                r_o, w_o, k_o, v_o, kk_o, a_o, g_o, carry):
        _first_step_zero(carry)
        zv = z_ref[...]
        zs = zv + (_shift_down(zv, carry) - zv) * mu_ref[...]
        k, tw, al, sw, decay, a, kk0, nrm, inv, k2 = rw_values(zs, w0_ref, wl_ref, a0_ref, wa_ref, kk_ref, ka_ref, e_ref)
        r_o[...] = zs[:, 0:c_rw]
        w_o[...] = decay
        k_o[...] = k2
        v_o[...] = zs[:, 2 * c_rw:3 * c_rw]
        kk_o[...] = kk0 * inv
        a_o[...] = a
        g_o[...] = zs[:, 3 * c_rw:c4]

    rw_consts = [mu, w0, wl, a0, wa, k_k, k_a, e_head]
    r_s, w_s, k_s, v_s, kk_s, a_s, g_s = _rowcall(
        rw_prep, "rwkv_prep", t_dim, bt, [(z, 0, sec)], rw_consts, [(c_rw, F32)] * 7,
        scratch=[pltpu.VMEM((1, sec), F32)])
    pairs_fwd = max(n for n in (1, 2, 4) if c_rw % (n * LANES) == 0)
    pairs_bwd = min(pairs_fwd, 2)
    y_s, s_all, s_fin = _scan_fwd(r_s, w_s, k_s, v_s, kk_s, a_s, 64, pairs_fwd)

    def rw_post_values(y, r, k2, v, g, lng_ref, lnb_ref, rk_ref, e):
        mean = _headsum(y, e) * (1.0 / HEAD)
        d = y - mean
        rstd = lax.rsqrt(_headsum(d * d, e) * (1.0 / HEAD) + GN_EPS)
        yh = d * rstd
        rk = _headsum(r * k2 * rk_ref[...], e)
        yo = yh * lng_ref[...] + lnb_ref[...] + rk * v
        sg = _sigmoid(g)
        return rstd, yh, rk, yo, sg

    def rw_post(y_ref, r_ref, k_ref, v_ref, g_ref, lng_ref, lnb_ref, rk_ref, e_ref, out_ref):
        g = g_ref[...]
        _, _, _, yo, sg = rw_post_values(y_ref[...], r_ref[...], k_ref[...], v_ref[...], g, lng_ref, lnb_ref, rk_ref, e_ref[...])
        out_ref[...] = (yo * g * sg).astype(BF16)

    (y_rw,) = _rowcall(rw_post, "rwkv_post", t_dim, bt, [y_s, r_s, k_s, v_s, g_s], [ln_g, ln_b, r_k, e_head], [(c_rw, BF16)])

    fl_cb = (sec + 4 * c_fox) // LANES
    hp = _round_up(h_fox, SUBLANES)
    bt_c = _pick(t_dim, 256)
    tri = (jnp.arange(bt_c)[:, None] >= jnp.arange(bt_c)[None, :]).astype(F32)

    rows8 = n_pairs * SUBLANES
    pair_rows = (jnp.arange(rows8)[:, None] // SUBLANES * 2 + jnp.arange(rows8)[:, None] % SUBLANES
                 == jnp.arange(LANES)[None, :]) & (jnp.arange(rows8)[:, None] % SUBLANES < 2)
    pair_rows = pair_rows.astype(F32)

    def fox_decay(fl_ref, bf_ref, tri_ref, sel_ref, ct_ref, cq_ref, carry):
        _first_step_zero(carry)
        lf = _log_sigmoid(fl_ref[...] + bf_ref[...])
        c = jnp.dot(tri_ref[...], lf, precision=HI, preferred_element_type=F32) + carry[...]
        carry[...] = c[bt_c - 1:bt_c, :]
        ct = jnp.dot(sel_ref[...], jnp.transpose(c), precision=HI, preferred_element_type=F32)
        ct_ref[...] = ct.reshape(n_pairs, SUBLANES, bt_c)
        for pair in range(n_pairs):
            cq_ref[pair] = c[:, 2 * pair:2 * pair + 2]

    ct, cq = pl.pallas_call(
        fox_decay, name="fox_decay", grid=(t_dim // bt_c,),
        in_specs=[pl.BlockSpec((bt_c, LANES), lambda i: (i, fl_cb)), pl.BlockSpec((1, LANES), lambda i: (0, 0)),
                  pl.BlockSpec((bt_c, bt_c), lambda i: (0, 0)), pl.BlockSpec((rows8, LANES), lambda i: (0, 0))],
        out_specs=[pl.BlockSpec((n_pairs, SUBLANES, bt_c), lambda i: (0, 0, i)),
                   pl.BlockSpec((n_pairs, bt_c, 2), lambda i: (0, i, 0))],
        out_shape=[jax.ShapeDtypeStruct((n_pairs, SUBLANES, t_dim), F32), jax.ShapeDtypeStruct((n_pairs, t_dim, 2), F32)],
        scratch_shapes=[pltpu.VMEM((1, LANES), F32)], compiler_params=_params("arbitrary"),
    )(z, b_f, tri, pair_rows)
    q_cb = sec // LANES
    k_cb, v_cb = q_cb + n_pairs, q_cb + 2 * n_pairs
    o_fox, lse = _fox_fwd(z, ct, cq, q_cb, k_cb, v_cb, n_pairs, 256)

    def fox_post(o_ref, z_ref, out_ref):
        g = z_ref[:, 3 * c_fox:4 * c_fox]
        out_ref[...] = (o_ref[...] * g * _sigmoid(g)).astype(BF16)

    (y_fox,) = _rowcall(fox_post, "fox_post", t_dim, bt, [o_fox, (z, 1, sec)], [], [(c_fox, BF16)])

    u_rw = _matmul(y_rw, wur, "nn", "up_rwkv")
    u_fox = _matmul(y_fox, wuf, "nn", "up_fox")

    def merge(ur_ref, uf_ref, z_ref, out_ref):
        s1 = _sigmoid(z_ref[:, 0:d_model])
        s2 = _sigmoid(z_ref[:, d_model:2 * d_model])
        out_ref[...] = (s1 * ur_ref[...] + s2 * uf_ref[...]).astype(BF16)

    (merged,) = _rowcall(merge, "merge", t_dim, bt, [u_rw, u_fox, (z, 2, sec)], [], [(d_model, BF16)])
    mo = _matmul(merged, wo, "nn", "proj_out")

    def resid_norm(x_ref, mo_ref, g_ref, x1_ref, n2_ref):
        x1 = x_ref[...] + mo_ref[...]
        rms = lax.rsqrt(jnp.mean(x1 * x1, axis=-1, keepdims=True) + NORM_EPS)
        x1_ref[...] = x1
        n2_ref[...] = (x1 * rms * g_ref[...]).astype(BF16)

    x1, n2 = _rowcall(resid_norm, "resid_norm", t_dim, bt, [x, mo], [g2], [(d_model, F32), (d_model, BF16)])
    ple = _matmul(p, pp, "nn", "ple_proj")
    gl = _matmul(n2, pg, "nn", "ple_gate")

    def head(x1_ref, ple_ref, gl_ref, tgt_ref, g_ref, dx2_ref, dple_ref, dgl_ref, loss_ref, dg3_ref):
        _first_step_zero(loss_ref, dg3_ref)
        sg = _sigmoid(gl_ref[...])
        pl_v = ple_ref[...]
        x2 = x1_ref[...] + pl_v * sg
        rms = lax.rsqrt(jnp.mean(x2 * x2, axis=-1, keepdims=True) + NORM_EPS)
        xn = x2 * rms
        diff = xn * g_ref[...] - tgt_ref[...]
        loss_ref[...] += 0.5 * jnp.sum(jnp.mean(diff * diff, axis=-1, keepdims=True), axis=0, keepdims=True)
        dyf = diff * inv_d
        dg3_ref[...] += _colsum(dyf * xn)
        gy = dyf * g_ref[...]
        dx2 = rms * (gy - xn * jnp.mean(xn * gy, axis=-1, keepdims=True))
        dx2_ref[...] = dx2
        dple_ref[...] = (dx2 * sg).astype(BF16)
        dgl_ref[...] = (dx2 * pl_v * sg * (1.0 - sg)).astype(BF16)

    dx2, dple, dgl, loss, d_g3 = _rowcall(
        head, "head", t_dim, bt, [x1, ple, gl, tgt], [g3], [(d_model, F32), (d_model, BF16), (d_model, BF16)],
        acc_outs=[(1, 1), (1, d_model)])

    d_pp = _matmul(p, dple, "tn", "d_ple_proj", out_dtype=BF16)
    d_pg = _matmul(n2, dgl, "tn", "d_ple_gate", out_dtype=BF16)
    dn2 = _matmul(dgl, pg, "nt", "d_n2")

    def resid_norm_bwd(dx2_ref, dn2_ref, x1_ref, g_ref, dx1_ref, dx1b_ref, dg2_ref):
        _first_step_zero(dg2_ref)
        x1 = x1_ref[...]
        rms = lax.rsqrt(jnp.mean(x1 * x1, axis=-1, keepdims=True) + NORM_EPS)
        xn = x1 * rms
        dn = dn2_ref[...]
        dg2_ref[...] += _colsum(dn * xn)
        gy = dn * g_ref[...]
        dx1 = dx2_ref[...] + rms * (gy - xn * jnp.mean(xn * gy, axis=-1, keepdims=True))
        dx1_ref[...] = dx1
        dx1b_ref[...] = dx1.astype(BF16)

    dx1, dx1b, d_g2 = _rowcall(resid_norm_bwd, "resid_norm_bwd", t_dim, bt, [dx2, dn2, x1], [g2],
                               [(d_model, F32), (d_model, BF16)], acc_outs=[(1, d_model)])
    d_wo = _matmul(merged, dx1b, "tn", "d_w_out", out_dtype=BF16)
    dmerged = _matmul(dx1b, wo, "nt", "d_merged")

    def merge_bwd(dm_ref, ur_ref, uf_ref, z_ref, dur_ref, duf_ref, dzg_ref):
        dm = dm_ref[...]
        s1 = _sigmoid(z_ref[:, 0:d_model])
        s2 = _sigmoid(z_ref[:, d_model:2 * d_model])
        dur_ref[...] = (dm * s1).astype(BF16)
        duf_ref[...] = (dm * s2).astype(BF16)
        dzg_ref[:, 0:d_model] = (dm * ur_ref[...] * s1 * (1.0 - s1)).astype(BF16)
        dzg_ref[:, d_model:2 * d_model] = (dm * uf_ref[...] * s2 * (1.0 - s2)).astype(BF16)
        if sec > 2 * d_model:
            dzg_ref[:, 2 * d_model:sec] = jnp.zeros((dm.shape[0], sec - 2 * d_model), BF16)

    du_rw, du_fox, dz_gate = _rowcall(merge_bwd, "merge_bwd", t_dim, bt, [dmerged, u_rw, u_fox, (z, 2, sec)], [],
                                      [(d_model, BF16), (d_model, BF16), (sec, BF16)])
    d_wur = _matmul(y_rw, du_rw, "tn", "d_w_up_rwkv", out_dtype=BF16)
    d_wuf = _matmul(y_fox, du_fox, "tn", "d_w_up_fox", out_dtype=BF16)
    dy_rw = _matmul(du_rw, wur, "nt", "d_y_rwkv")
    dy_fox = _matmul(du_fox, wuf, "nt", "d_y_fox")

    def fox_post_bwd(dy_ref, o_ref, z_ref, do_ref, dg_ref):
        g = z_ref[:, 3 * c_fox:4 * c_fox]
        sg = _sigmoid(g)
        dy = dy_ref[...]
        do_ref[...] = dy * g * sg
        dg_ref[...] = (dy * o_ref[...] * sg * (1.0 + g * (1.0 - sg))).astype(BF16)

    do_fox, dg_fox = _rowcall(fox_post_bwd, "fox_post_bwd", t_dim, bt, [dy_fox, o_fox, (z, 1, sec)], [],
                              [(c_fox, F32), (c_fox, BF16)])
    rowdot = _fox_rowdot(z, ct, cq, do_fox, lse, q_cb, k_cb, v_cb, n_pairs, 256)
    dq_f, dk_f, dv_f, dc_t = _fox_bwd(z, ct, cq, rowdot, do_fox, lse, q_cb, k_cb, v_cb, n_pairs, 256)
    sel = (jnp.arange(hp)[:, None] // 2 * SUBLANES + jnp.arange(hp)[:, None] % 2 == jnp.arange(rows8)[None, :]).astype(F32)
    tri_rev = (jnp.arange(bt_c)[:, None] >= jnp.arange(bt_c)[None, :]).astype(F32)
    bf_col = b_f.reshape(LANES, 1)[0:hp]
    nbc = t_dim // bt_c

    def fox_decay_bwd(dc_ref, fl_ref, sel_ref, tri_ref, bf_ref, dfl_ref, dbf_ref, carry):
        _first_step_zero(carry, dbf_ref)
        dc = jnp.dot(sel_ref[...], dc_ref[...].reshape(rows8, bt_c), precision=HI, preferred_element_type=F32)
        dlf = jnp.dot(dc, tri_ref[...], precision=HI, preferred_element_type=F32) + carry[...]
        carry[...] = dlf[:, 0:1]
        flt = jnp.transpose(fl_ref[...])[0:hp, :]
        dfl = dlf * _sigmoid(-(flt + bf_ref[...]))
        head_row = lax.broadcasted_iota(jnp.int32, (hp, bt_c), 0)
        dfl = jnp.where(head_row < h_fox, dfl, 0.0)
        dbf_ref[...] += jnp.sum(dfl, axis=1, keepdims=True)
        full = jnp.concatenate([dfl, jnp.zeros((LANES - hp, bt_c), F32)], axis=0) if hp < LANES else dfl
        dfl_ref[...] = jnp.transpose(full).astype(BF16)

    dz_fl, d_bf = pl.pallas_call(
        fox_decay_bwd, name="fox_decay_bwd", grid=(nbc,),
        in_specs=[pl.BlockSpec((n_pairs, SUBLANES, bt_c), lambda i: (0, 0, nbc - 1 - i)),
                  pl.BlockSpec((bt_c, LANES), lambda i: (nbc - 1 - i, fl_cb)),
                  pl.BlockSpec(sel.shape, lambda i: (0, 0)), pl.BlockSpec((bt_c, bt_c), lambda i: (0, 0)),
                  pl.BlockSpec((hp, 1), lambda i: (0, 0))],
        out_specs=[pl.BlockSpec((bt_c, LANES), lambda i: (nbc - 1 - i, 0)), pl.BlockSpec((hp, 1), lambda i: (0, 0))],
        out_shape=[jax.ShapeDtypeStruct((t_dim, LANES), BF16), jax.ShapeDtypeStruct((hp, 1), F32)],
        scratch_shapes=[pltpu.VMEM((hp, 1), F32)], compiler_params=_params("arbitrary"),
    )(dc_t, z, sel, tri_rev, bf_col)

    def rw_post_bwd(dy_ref, y_ref, r_ref, k_ref, v_ref, g_ref, lng_ref, lnb_ref, rk_ref, e_ref,
                    dg_ref, dys_ref, dr_ref, dk_ref, dv_ref, dlng_ref, dlnb_ref, drk_ref):
        _first_step_zero(dlng_ref, dlnb_ref, drk_ref)
        e = e_ref[...]
        dy, r, k2, v, g = dy_ref[...], r_ref[...], k_ref[...], v_ref[...], g_ref[...]
        rstd, yh, rk, yo, sg = rw_post_values(y_ref[...], r, k2, v, g, lng_ref, lnb_ref, rk_ref, e)
        dg_ref[...] = dy * yo * sg * (1.0 + g * (1.0 - sg))
        dyo = dy * g * sg
        dlnb_ref[...] += _colsum(dyo)
        dlng_ref[...] += _colsum(dyo * yh)
        dyh = dyo * lng_ref[...]
        dys_ref[...] = rstd * (dyh - _headsum(dyh, e) * (1.0 / HEAD) - yh * _headsum(dyh * yh, e) * (1.0 / HEAD))
        drk = _headsum(dyo * v, e)
        dv_ref[...] = dyo * rk
        dr_ref[...] = drk * k2 * rk_ref[...]
        dk_ref[...] = drk * r * rk_ref[...]
        drk_ref[...] += _colsum(drk * r * k2)

    dg_rw, dy_s, dr_b, dk_b, dv_b, d_lng, d_lnb, d_rk = _rowcall(
        rw_post_bwd, "rwkv_post_bwd", t_dim, bt, [dy_rw, y_s, r_s, k_s, v_s, g_s], [ln_g, ln_b, r_k, e_head],
        [(c_rw, F32)] * 5, acc_outs=[(1, c_rw)] * 3)
    dr_c, dw_c, dk_c, dv_c, dkk_c, da_c = _scan_bwd(r_s, w_s, k_s, v_s, kk_s, a_s, dy_s, s_all, s_fin, 128, pairs_bwd)

    def rw_prep_bwd(z_ref, dr1, dr2, dw_ref, dk1, dk2_ref, dv1, dv2, dkk_ref, da_ref, dg_ref,
                    mu_ref, w0_ref, wl_ref, a0_ref, wa_ref, kk_ref, ka_ref, e_ref,
                    dzs_ref, tw_ref, al_ref, dwr_ref, dar_ref, dmu_ref, dw0_ref, da0_ref, dkk_acc, dka_acc, carry):
        _first_step_zero(carry, dmu_ref, dw0_ref, da0_ref, dkk_acc, dka_acc)
        e = e_ref[...]
        zv = z_ref[...]
        zp = _shift_down(zv, carry)
        zs = zv + (zp - zv) * mu_ref[...]
        k, tw, al, sw, decay, a, kk0, nrm, inv, k2 = rw_values(zs, w0_ref, wl_ref, a0_ref, wa_ref, kk_ref, ka_ref, e_ref)
        dk2 = dk1[...] + dk2_ref[...]
        da = da_ref[...] + dk2 * k * ka_ref[...]
        dk = dk2 * (1.0 + (a - 1.0) * ka_ref[...])
        dka_acc[...] += _colsum(dk2 * k * (a - 1.0))
        kk = kk0 * inv
        dkk = dkk_ref[...]
        dkk0 = inv * jnp.where(nrm > 1e-12, dkk - kk * _headsum(dkk * kk, e), dkk)
        dk = dk + dkk0 * kk_ref[...]
        dkk_acc[...] += _colsum(dkk0 * k)
        da_raw = da * a * (1.0 - a)
        da0_ref[...] += _colsum(da_raw)
        dw_raw = dw_ref[...] * decay * (-decay_k) * sw * (1.0 - sw)
        dw0_ref[...] += _colsum(dw_raw)
        dar_b, dwr_b = da_raw.astype(BF16), dw_raw.astype(BF16)
        dal = lax.dot_general(dar_b, wa_ref[...], _DOT_DIMS["nt"], preferred_element_type=F32)
        dtw = lax.dot_general(dwr_b, wl_ref[...], _DOT_DIMS["nt"], preferred_element_type=F32)
        dzs_ref[:, 0:c_rw] = dr1[...] + dr2[...]
        dzs_ref[:, c_rw:2 * c_rw] = dk
        dzs_ref[:, 2 * c_rw:3 * c_rw] = dv1[...] + dv2[...]
        dzs_ref[:, 3 * c_rw:c4] = dg_ref[...]
        dzs_ref[:, c4:c4 + lora] = dtw * (1.0 - tw * tw)
        dzs_ref[:, c4 + lora:c4 + 2 * lora] = dal
        if sec > c4 + 2 * lora:
            dzs_ref[:, c4 + 2 * lora:sec] = jnp.zeros((zv.shape[0], sec - c4 - 2 * lora), F32)
        tw_ref[...] = tw.astype(BF16)
        al_ref[...] = al.astype(BF16)
        dwr_ref[...] = dwr_b
        dar_ref[...] = dar_b
        dmu_ref[...] += _colsum(dzs_ref[...] * (zp - zv))

    dzs, tw_b, al_b, dwr_b, dar_b, d_mu, d_w0, d_a0, d_kk, d_ka = _rowcall(
        rw_prep_bwd, "rwkv_prep_bwd", t_dim, bt,
        [(z, 0, sec), dr_c, dr_b, dw_c, dk_c, dk_b, dv_c, dv_b, dkk_c, da_c, dg_rw], rw_consts,
        [(sec, F32), (lora, BF16), (lora, BF16), (c_rw, BF16), (c_rw, BF16)],
        acc_outs=[(1, sec), (1, c_rw), (1, c_rw), (1, c_rw), (1, c_rw)], scratch=[pltpu.VMEM((1, sec), F32)])
    d_wl = _matmul(tw_b, dwr_b, "tn", "d_w_lora", out_dtype=BF16)
    d_wa = _matmul(al_b, dar_b, "tn", "d_a_lora", out_dtype=BF16)

    def shift_bwd(dzs_ref, mu_ref, dz_ref, carry):
        _first_step_zero(carry)
        d = dzs_ref[...]
        nbt = d.shape[0]
        nxt = pltpu.roll(d, nbt - 1, 0)
        rowi = lax.broadcasted_iota(jnp.int32, d.shape, 0)
        nxt = jnp.where(rowi == nbt - 1, carry[...], nxt)
        carry[...] = d[0:1, :]
        m = mu_ref[...]
        dz_ref[...] = (d * (1.0 - m) + nxt * m).astype(BF16)

    (dz_rw,) = _rowcall(shift_bwd, "shift_bwd", t_dim, bt, [dzs], [mu], [(sec, BF16)],
                        scratch=[pltpu.VMEM((1, sec), F32)], reverse=True)

    fox_parts = [dq_f.astype(BF16), dk_f.astype(BF16), dv_f.astype(BF16), dg_fox, dz_fl]
    if sec > 4 * c_fox + LANES:
        fox_parts.append(jnp.zeros((t_dim, sec - 4 * c_fox - LANES), BF16))
    dz = jnp.concatenate([dz_rw] + fox_parts + [dz_gate], axis=1)
    d_wz = _matmul(h, dz, "tn", "d_w_in", out_dtype=BF16, bn=1408)
    dh = _matmul(dz, wz, "nt", "d_h", bk=1408)

    def norm_in_bwd(dh_ref, x_ref, dx1_ref, g_ref, dx_ref, dg1_ref):
        _first_step_zero(dg1_ref)
        xv = x_ref[...]
        rms = lax.rsqrt(jnp.mean(xv * xv, axis=-1, keepdims=True) + NORM_EPS)
        xn = xv * rms
        d = dh_ref[...]
        dg1_ref[...] += _colsum(d * xn)
        gy = d * g_ref[...]
        dx_ref[...] = dx1_ref[...] + rms * (gy - xn * jnp.mean(xn * gy, axis=-1, keepdims=True))

    dx, d_g1 = _rowcall(norm_in_bwd, "norm_in_bwd", t_dim, bt, [dh, x, dx1], [norm_g], [(d_model, F32)],
                        acc_outs=[(1, d_model)])

    full_grads = {"w_in": d_wz, "rw_w_lora_up": d_wl, "rw_a_lora_up": d_wa, "w_up_rwkv": d_wur, "w_up_fox": d_wuf,
                  "w_out": d_wo, "ple_proj": d_pp, "ple_gate_w": d_pg}
    rep_grads = {"norm_g": d_g1, "rw_shift_mu": d_mu[:, 0:c4 + 2 * lora], "rw_w0": d_w0, "rw_a0": d_a0, "rw_k_k": d_kk,
                 "rw_k_a": d_ka, "rw_r_k": d_rk, "rw_ln_g": d_lng, "rw_ln_b": d_lnb, "fox_b_f": d_bf[0:h_fox, 0],
                 "ple_norm_g": d_g2, "final_norm_g": d_g3}
    return loss[0, 0], dx, full_grads, rep_grads


def kernel(x, p, norm_g, w_in, rw_shift_mu, rw_w0, rw_w_lora_up, rw_a0, rw_a_lora_up, rw_k_k, rw_k_a, rw_r_k, rw_ln_g, rw_ln_b, fox_b_f, w_up_rwkv, w_up_fox, w_out, ple_proj, ple_gate_w, ple_norm_g, final_norm_g, loss_target, m_norm_g, m_w_in, m_rw_shift_mu, m_rw_w0, m_rw_w_lora_up, m_rw_a0, m_rw_a_lora_up, m_rw_k_k, m_rw_k_a, m_rw_r_k, m_rw_ln_g, m_rw_ln_b, m_fox_b_f, m_w_up_rwkv, m_w_up_fox, m_w_out, m_ple_proj, m_ple_gate_w, m_ple_norm_g, m_final_norm_g, v_norm_g, v_w_in, v_rw_shift_mu, v_rw_w0, v_rw_w_lora_up, v_rw_a0, v_rw_a_lora_up, v_rw_k_k, v_rw_k_a, v_rw_r_k, v_rw_ln_g, v_rw_ln_b, v_fox_b_f, v_w_up_rwkv, v_w_up_fox, v_w_out, v_ple_proj, v_ple_gate_w, v_ple_norm_g, v_final_norm_g):
    args = locals()
    w = {n: args[n] for n in WEIGHTS}
    mom = {n: args["m_" + n] for n in WEIGHTS}
    var = {n: args["v_" + n] for n in WEIGHTS}

    t_dim, d_model = x.shape[1], x.shape[2]
    c_rw, lora = rw_w0.shape[1], rw_w_lora_up.shape[1]
    h_fox = fox_b_f.shape[1]
    c_fox = h_fox * HEAD
    rw_cols, fox_cols, gate_cols = 4 * c_rw + 2 * lora, 4 * c_fox + h_fox, 2 * d_model
    sec = max(rw_cols, 4 * c_fox + LANES, _round_up(gate_cols, LANES))
    assert c_rw % LANES == 0 and c_fox % LANES == 0 and rw_cols % LANES == 0 and h_fox <= LANES and d_model % LANES == 0
    assert rw_a_lora_up.shape[1] == lora and w_in.shape[2] * NDEV == rw_cols + fox_cols + gate_cols

    gathered = _all_gather([w[n][0].astype(BF16) for n, _ in SHARDED], "gather_weights")
    full = {n: _join_shards(g, ax) for (n, ax), g in zip(SHARDED, gathered)}

    def to_sections(wi):
        pad = lambda a, width: jnp.pad(a, ((0, 0), (0, width - a.shape[1])))
        return jnp.concatenate([pad(wi[:, :rw_cols], sec), pad(wi[:, rw_cols:rw_cols + fox_cols], sec),
                                pad(wi[:, rw_cols + fox_cols:], sec)], axis=1)

    def from_sections(g):
        return jnp.concatenate([g[:, :rw_cols], g[:, sec:sec + fox_cols], g[:, 2 * sec:2 * sec + gate_cols]], axis=1)

    rep = {n: w[n] for n in REPLICATED}
    dims = (t_dim, d_model, c_rw, lora, c_fox, h_fox, sec)
    loss_local, grad_x, full_grads, rep_grads = _local_step(
        x[0], p[0, 0], loss_target[0], to_sections(full["w_in"]), full["rw_w_lora_up"], full["rw_a_lora_up"],
        full["w_up_rwkv"], full["w_up_fox"], full["w_out"], full["ple_proj"], full["ple_gate_w"], rep, dims)
    full_grads["w_in"] = from_sections(full_grads["w_in"])

    rep_sizes = [w[n].size for n in REPLICATED]
    rep_offs = [sum(rep_sizes[:i]) for i in range(len(rep_sizes))]
    pack_rep = lambda tree: _pack_rows(jnp.concatenate([tree[n].astype(F32).reshape(-1) for n in REPLICATED]))
    *recv, small_all = _grad_exchange([_split_shards(full_grads[n], ax) for n, ax in SHARDED], pack_rep(rep_grads),
                                      "exchange_grads")
    kinds = ("grad", "delta", "new_m", "new_v")
    outs = {}
    for (n, _), partials in zip(SHARDED, recv):
        for kind, buf in zip(kinds, _adamw(partials, w[n][0], mom[n][0], var[n][0], "adamw_" + n)):
            outs[kind, n] = buf[None]
    for kind, buf in zip(kinds, _adamw(small_all, pack_rep(w), pack_rep(mom), pack_rep(var), "adamw_replicated")):
        flat = buf.reshape(-1)
        for n, o, s in zip(REPLICATED, rep_offs, rep_sizes):
            outs[kind, n] = flat[o:o + s].reshape(w[n].shape)
    loss = lax.psum(loss_local, MESH_AXES)
    return (loss, grad_x[None], *[outs[kind, n] for kind in ("grad", "delta", "new_m", "new_v") for n in WEIGHTS])
```

```python
import functools
import math

import jax
import jax.numpy as jnp
from jax import lax
from jax.experimental import pallas as pl
from jax.experimental.pallas import tpu as pltpu

F32, BF16 = jnp.float32, jnp.bfloat16
HI = lax.Precision.HIGHEST
LANES = 128
SUBLANES = 8
HEAD = 64
NORM_EPS = 1e-6
GN_EPS = 64e-5
VMEM_LIMIT = 56 * 1024 * 1024
NDEV = 8
PACK_W = 1024
PACK_ALIGN = 16 * PACK_W
MESH_AXES = ("x", "y", "c")
MESH = pl.DeviceIdType.MESH

ADAM_LR, ADAM_B1, ADAM_B2, ADAM_EPS, ADAM_WD, ADAM_STEP = 0.001, 0.9, 0.999, 1e-08, 0.01, 10


def _round_up(n, m):
    return (n + m - 1) // m * m


def _pick(dim, pref, align=LANES):
    if dim <= pref:
        return dim
    best = None
    for cand in range(align, pref + 1, align):
        if dim % cand == 0:
            best = cand
    return dim if best is None else best


def _params(*sem):
    return pltpu.CompilerParams(dimension_semantics=sem, vmem_limit_bytes=VMEM_LIMIT)


def _sigmoid(v):
    return jax.nn.sigmoid(v)


def _log_sigmoid(v):
    return jnp.minimum(v, 0.0) - jnp.log(1.0 + jnp.exp(-jnp.abs(v)))


_DOT_DIMS = {"nn": (((1,), (0,)), ((), ())), "nt": (((1,), (1,)), ((), ())), "tn": (((0,), (0,)), ((), ()))}


def _matmul(a, b, mode, name, out_dtype=F32, bm=512, bn=1024, bk=2048):
    if mode == "tn":
        k_dim, m_dim = a.shape
    else:
        m_dim, k_dim = a.shape
    n_dim = b.shape[0] if mode == "nt" else b.shape[1]
    bm, bn, bk = _pick(m_dim, bm), _pick(n_dim, bn), _pick(k_dim, bk)
    nk = k_dim // bk

    def body(a_ref, b_ref, o_ref, *acc):
        prod = lax.dot_general(a_ref[...].astype(BF16), b_ref[...].astype(BF16), _DOT_DIMS[mode],
                               preferred_element_type=F32)
        if nk == 1:
            o_ref[...] = prod.astype(o_ref.dtype)
            return
        acc_ref, k = acc[0], pl.program_id(2)

        @pl.when(k == 0)
        def _():
            acc_ref[...] = prod

        @pl.when(k > 0)
        def _():
            acc_ref[...] += prod

        @pl.when(k == nk - 1)
        def _():
            o_ref[...] = acc_ref[...].astype(o_ref.dtype)

    if mode == "tn":
        a_spec = pl.BlockSpec((bk, bm), lambda i, j, k: (k, i))
    else:
        a_spec = pl.BlockSpec((bm, bk), lambda i, j, k: (i, k))
    if mode == "nt":
        b_spec = pl.BlockSpec((bn, bk), lambda i, j, k: (j, k))
    else:
        b_spec = pl.BlockSpec((bk, bn), lambda i, j, k: (k, j))
    return pl.pallas_call(
        body, name=name, grid=(m_dim // bm, n_dim // bn, nk),
        in_specs=[a_spec, b_spec], out_specs=pl.BlockSpec((bm, bn), lambda i, j, k: (i, j)),
        out_shape=jax.ShapeDtypeStruct((m_dim, n_dim), out_dtype),
        scratch_shapes=[pltpu.VMEM((bm, bn), F32)] if nk > 1 else [],
        compiler_params=_params("parallel", "parallel", "arbitrary"),
    )(a, b)


def _rowcall(body, name, t_dim, bt, row_ins, const_ins, row_outs, acc_outs=(), scratch=(), reverse=False):
    nt = t_dim // bt

    def rmap(i):
        return nt - 1 - i if reverse else i

    in_specs, args = [], []
    for item in row_ins:
        arr, cb, w = item if isinstance(item, tuple) else (item, 0, item.shape[1])
        in_specs.append(pl.BlockSpec((bt, w), lambda i, cb=cb: (rmap(i), cb)))
        args.append(arr)
    for arr in const_ins:
        in_specs.append(pl.BlockSpec(arr.shape, lambda i, nd=arr.ndim: (0,) * nd))
        args.append(arr)
    out_specs = [pl.BlockSpec((bt, w), lambda i: (rmap(i), 0)) for w, _ in row_outs]
    out_shape = [jax.ShapeDtypeStruct((t_dim, w), dt) for w, dt in row_outs]
    for shp in acc_outs:
        out_specs.append(pl.BlockSpec(shp, lambda i, nd=len(shp): (0,) * nd))
        out_shape.append(jax.ShapeDtypeStruct(shp, F32))
    return pl.pallas_call(
        body, name=name, grid=(nt,), in_specs=in_specs, out_specs=out_specs, out_shape=out_shape,
        scratch_shapes=list(scratch), compiler_params=_params("arbitrary"),
    )(*args)


def _first_step_zero(*refs):
    @pl.when(pl.program_id(0) == 0)
    def _():
        for r in refs:
            r[...] = jnp.zeros_like(r)


def _colsum(v):
    return jnp.sum(v, axis=0, keepdims=True)


def _headsum(v, e):
    parts = [jnp.dot(v[:, p * LANES:(p + 1) * LANES], e, precision=HI, preferred_element_type=F32)
             for p in range(v.shape[1] // LANES)]
    return parts[0] if len(parts) == 1 else jnp.concatenate(parts, axis=1)


def _shift_down(v, carry_ref):
    bt = v.shape[0]
    prev = pltpu.roll(v, 1, 0)
    row = lax.broadcasted_iota(jnp.int32, v.shape, 0)
    prev = jnp.where(row == 0, carry_ref[...], prev)
    carry_ref[...] = v[bt - 1:bt, :]
    return prev


def _pair_consts():
    lane = lax.broadcasted_iota(jnp.int32, (1, LANES), 1)
    m0 = (lane < HEAD).astype(F32)
    m1 = 1.0 - m0
    sub = lax.broadcasted_iota(jnp.int32, (HEAD, LANES), 0)
    lane2 = lax.broadcasted_iota(jnp.int32, (HEAD, LANES), 1)
    i0 = (lane2 == sub).astype(F32)
    i1 = (lane2 == sub + HEAD).astype(F32)
    return m0, m1, i0, i1


def _head_masks():
    lane = lax.broadcasted_iota(jnp.int32, (1, LANES), 1)
    first = (lane < HEAD).astype(F32)
    return first, 1.0 - first


def _head_ones(dtype=BF16):
    lane = jnp.arange(LANES)
    return (lane[:, None] // HEAD == lane[None, :] // HEAD).astype(dtype)


def _lanesum(v):
    return jnp.sum(v, axis=1, keepdims=True)


def _split_bf16(v):
    hi = v.astype(BF16).astype(F32)
    rest = v - hi
    mid = rest.astype(BF16).astype(F32)
    return hi, mid, (rest - mid).astype(BF16).astype(F32)


def _col_pair(parts, j, i01, e_bf16):
    lhs = jnp.concatenate([(part[j:j + 1] * i01).astype(BF16) for part in parts], axis=0)
    out = jnp.dot(lhs, e_bf16, preferred_element_type=F32)
    return (out[0:HEAD] + out[HEAD:2 * HEAD]) + out[2 * HEAD:3 * HEAD]


def _col_tiles(out_ref, src_ref, rows, lanes, i01, e_bf16):
    for sl in lanes:
        parts = _split_bf16(src_ref[rows, sl])
        for j in range(SUBLANES):
            out_ref[j, :, sl] = _col_pair(parts, j, i01, e_bf16)


def _row_pair(c0, c1, i0, i1):
    return _colsum(c0 * i0 + c1 * i1)


def _scan_fwd(r, w, k, v, kk, a, tc, npb):
    t_dim, c_dim = r.shape
    wb = LANES * npb
    tc = _pick(t_dim, tc, SUBLANES)

    def body(r_ref, w_ref, k_ref, v_ref, kk_ref, a_ref, e_ref, y_ref, sall_ref, sfin_ref, s_ref):
        @pl.when(pl.program_id(1) == 0)
        def _():
            s_ref[...] = jnp.zeros_like(s_ref)

        m0, m1, i0, i1 = _pair_consts()
        i01 = i0 + i1
        e = e_ref[...]
        sub8 = lax.broadcasted_iota(jnp.int32, (SUBLANES, LANES), 0)
        lanes = [slice(q * LANES, (q + 1) * LANES) for q in range(npb)]
        ng = tc // SUBLANES

        def lanesums(tiles):
            sums = _lanesum(jnp.concatenate(tiles, axis=0))
            return [sums[i * HEAD:(i + 1) * HEAD] for i in range(len(tiles))]

        def halves(s, row):
            return [s * (row * m0), s * (row * m1)]

        def group(gi, carry):
            base = pl.multiple_of(gi * SUBLANES, SUBLANES)
            rows = pl.ds(base, SUBLANES)
            s = list(carry)
            r8 = [r_ref[rows, sl] for sl in lanes]
            w8 = [w_ref[rows, sl] for sl in lanes]
            k8 = [k_ref[rows, sl] for sl in lanes]
            kk8 = [kk_ref[rows, sl] for sl in lanes]
            b8 = [kk8[q] * a_ref[rows, lanes[q]] for q in range(npb)]
            v8 = [_split_bf16(v_ref[rows, sl]) for sl in lanes]
            y8 = [jnp.zeros((SUBLANES, LANES), F32)] * npb
            for j in range(SUBLANES + 1):
                one, before = slice(j, j + 1), slice(j - 1, j)
                tiles = []
                for q in range(npb):
                    if j < SUBLANES:
                        tiles += halves(s[q], kk8[q][one])
                    if j > 0:
                        tiles += halves(s[q], r8[q][before])
                cols = lanesums(tiles)
                per = len(tiles) // npb
                for q in range(npb):
                    mine = cols[q * per:(q + 1) * per]
                    if j > 0:
                        y8[q] = jnp.where(sub8 == j - 1, _row_pair(mine[-2], mine[-1], i0, i1), y8[q])
                    if j < SUBLANES:
                        sall_ref[base + j, :, lanes[q]] = s[q]
                        sb = mine[0] * m0 + mine[1] * m1
                        s[q] = s[q] * w8[q][one] - sb * b8[q][one] + _col_pair(v8[q], j, i01, e) * k8[q][one]
            for q in range(npb):
                y_ref[rows, lanes[q]] = y8[q]
            return tuple(s)

        init = tuple(s_ref[:, q * LANES:(q + 1) * LANES] for q in range(npb))
        fin = lax.fori_loop(0, ng, group, init)
        for q in range(npb):
            s_ref[:, q * LANES:(q + 1) * LANES] = fin[q]
            sfin_ref[:, q * LANES:(q + 1) * LANES] = fin[q]

    row = pl.BlockSpec((tc, wb), lambda p, c: (c, p))
    return pl.pallas_call(
        body, name="rwkv_scan_fwd", grid=(c_dim // wb, t_dim // tc),
        in_specs=[row] * 6 + [pl.BlockSpec((LANES, LANES), lambda p, c: (0, 0))],
        out_specs=[row, pl.BlockSpec((tc, HEAD, wb), lambda p, c: (c, 0, p)), pl.BlockSpec((HEAD, wb), lambda p, c: (0, p))],
        out_shape=[jax.ShapeDtypeStruct((t_dim, c_dim), F32), jax.ShapeDtypeStruct((t_dim, HEAD, c_dim), F32),
                   jax.ShapeDtypeStruct((HEAD, c_dim), F32)],
        scratch_shapes=[pltpu.VMEM((HEAD, wb), F32)],
        compiler_params=_params("parallel", "arbitrary"),
    )(r, w, k, v, kk, a, _head_ones())


def _scan_bwd(r, w, k, v, kk, a, dy, sall, sfin, tc, npb):
    t_dim, c_dim = r.shape
    wb = LANES * npb
    tc = _pick(t_dim, tc, SUBLANES)
    nc = t_dim // tc

    def body(r_ref, w_ref, k_ref, v_ref, kk_ref, a_ref, dy_ref, sall_ref, sfin_ref, e_ref,
             dr_ref, dw_ref, dk_ref, dv_ref, dkk_ref, da_ref, ds_ref, sn_ref):
        @pl.when(pl.program_id(1) == 0)
        def _():
            ds_ref[...] = jnp.zeros_like(ds_ref)
            sn_ref[...] = sfin_ref[...]

        m0, m1, i0, i1 = _pair_consts()
        i01 = i0 + i1
        e = e_ref[...]
        sub8 = lax.broadcasted_iota(jnp.int32, (SUBLANES, LANES), 0)
        lanes = [slice(q * LANES, (q + 1) * LANES) for q in range(npb)]
        ng = tc // SUBLANES

        def halves(s, row):
            return [s * (row * m0), s * (row * m1)]

        def group(gi, carry):
            base = pl.multiple_of((ng - 1 - gi) * SUBLANES, SUBLANES)
            rows = pl.ds(base, SUBLANES)
            ds = list(carry)
            r8 = [r_ref[rows, sl] for sl in lanes]
            w8 = [w_ref[rows, sl] for sl in lanes]
            k8 = [k_ref[rows, sl] for sl in lanes]
            kk8 = [kk_ref[rows, sl] for sl in lanes]
            a8 = [a_ref[rows, sl] for sl in lanes]
            v8 = [_split_bf16(v_ref[rows, sl]) for sl in lanes]
            dy8 = [_split_bf16(dy_ref[rows, sl]) for sl in lanes]
            zero8 = jnp.zeros((SUBLANES, LANES), F32)
            dr8, dw8, dk8, dv8, dkk8, da8 = ([zero8] * npb for _ in range(6))
            for j in reversed(range(SUBLANES)):
                one = slice(j, j + 1)
                here = sub8 == j
                d, s_prev, tiles = [], [], []
                for q in range(npb):
                    dyb = _col_pair(dy8[q], j, i01, e)
                    s_prev.append(sall_ref[base + j, :, lanes[q]])
                    dr8[q] = jnp.where(here, _colsum(sn_ref[:, lanes[q]] * dyb), dr8[q])
                    sn_ref[:, lanes[q]] = s_prev[q]
                    d.append(ds[q] + dyb * r8[q][one])
                for q in range(npb):
                    tiles += halves(d[q], kk8[q][one] * a8[q][one])
                for q in range(npb):
                    tiles += halves(d[q], k8[q][one]) + halves(s_prev[q], kk8[q][one])
                sums = _lanesum(jnp.concatenate(tiles, axis=0))
                cols = [sums[i * HEAD:(i + 1) * HEAD] for i in range(len(tiles))]
                for q in range(npb):
                    kkr, ar = kk8[q][one], a8[q][one]
                    dsb = -(cols[2 * q] * m0 + cols[2 * q + 1] * m1)
                    ds[q] = d[q] * w8[q][one] + dsb * kkr
                    rest = cols[2 * npb + 4 * q:2 * npb + 4 * q + 4]
                    dv8[q] = jnp.where(here, _row_pair(rest[0], rest[1], i0, i1), dv8[q])
                    sb = rest[2] * m0 + rest[3] * m1
                    db = -_colsum(d[q] * sb)
                    dk8[q] = jnp.where(here, _colsum(d[q] * _col_pair(v8[q], j, i01, e)), dk8[q])
                    dw8[q] = jnp.where(here, _colsum(d[q] * s_prev[q]), dw8[q])
                    dkk8[q] = jnp.where(here, _colsum(s_prev[q] * dsb) + db * ar, dkk8[q])
                    da8[q] = jnp.where(here, db * kkr, da8[q])
            for q in range(npb):
                sl = lanes[q]
                dr_ref[rows, sl], dw_ref[rows, sl], dk_ref[rows, sl] = dr8[q], dw8[q], dk8[q]
                dv_ref[rows, sl], dkk_ref[rows, sl], da_ref[rows, sl] = dv8[q], dkk8[q], da8[q]
            return tuple(ds)

        init = tuple(ds_ref[:, q * LANES:(q + 1) * LANES] for q in range(npb))
        fin = lax.fori_loop(0, ng, group, init)
        for q in range(npb):
            ds_ref[:, q * LANES:(q + 1) * LANES] = fin[q]

    row = pl.BlockSpec((tc, wb), lambda p, c: (nc - 1 - c, p))
    return pl.pallas_call(
        body, name="rwkv_scan_bwd", grid=(c_dim // wb, nc),
        in_specs=[row] * 7 + [pl.BlockSpec((tc, HEAD, wb), lambda p, c: (nc - 1 - c, 0, p)),
                              pl.BlockSpec((HEAD, wb), lambda p, c: (0, p)),
                              pl.BlockSpec((LANES, LANES), lambda p, c: (0, 0))],
        out_specs=[row] * 6, out_shape=[jax.ShapeDtypeStruct((t_dim, c_dim), F32)] * 6,
        scratch_shapes=[pltpu.VMEM((HEAD, wb), F32)] * 2,
        compiler_params=_params("parallel", "arbitrary"),
    )(r, w, k, v, kk, a, dy, sall, sfin, _head_ones())


def _fox_fwd(z, ct, cq, q_cb, k_cb, v_cb, n_pairs, blk):
    t_dim = z.shape[0]
    blk = _pick(t_dim, blk)
    nq = t_dim // blk
    scale = HEAD ** -0.5

    def body(q_ref, k_ref, v_ref, ct_ref, cq_ref, o_ref, lse_ref):
        i = pl.program_id(1)
        rowi = lax.broadcasted_iota(jnp.int32, (blk, blk), 0)
        coli = lax.broadcasted_iota(jnp.int32, (blk, blk), 1)
        masks = _head_masks()
        qv = q_ref[...]
        qs = [(qv * mk).astype(BF16) for mk in masks]
        cqs = [cq_ref[:, hh:hh + 1] for hh in range(2)]

        def kv_step(j, carry, masked):
            rows = pl.ds(pl.multiple_of(j * blk, blk), blk)
            kb = k_ref[rows, :].astype(BF16)
            vv = v_ref[rows, :]
            stats, acc = list(carry[:4]), carry[4]
            rescale, add = 0.0, 0.0
            for hh in range(2):
                m, l = stats[2 * hh], stats[2 * hh + 1]
                s = (lax.dot_general(qs[hh], kb, _DOT_DIMS["nt"], preferred_element_type=F32) * scale
                     + (cqs[hh] - ct_ref[hh:hh + 1, rows]))
                if masked:
                    s = jnp.where(rowi >= coli, s, -jnp.inf)
                m_new = jnp.maximum(m, jnp.max(s, axis=1, keepdims=True))
                alpha = jnp.exp(m - m_new)
                pr = jnp.exp(s - m_new)
                stats[2 * hh], stats[2 * hh + 1] = m_new, l * alpha + jnp.sum(pr, axis=1, keepdims=True)
                rescale = rescale + alpha * masks[hh]
                add = add + jnp.dot(pr.astype(BF16), (vv * masks[hh]).astype(BF16), preferred_element_type=F32)
            return (*stats, acc * rescale + add)

        neg, zero = jnp.full((blk, 1), -jnp.inf, F32), jnp.zeros((blk, 1), F32)
        carry = lax.fori_loop(0, i, functools.partial(kv_step, masked=False),
                              (neg, zero, neg, zero, jnp.zeros((blk, LANES), F32)))
        m0, l0, m1, l1, acc = kv_step(i, carry, True)
        o_ref[...] = acc * (masks[0] / l0 + masks[1] / l1)
        lse_ref[:, 0:1] = m0 + jnp.log(l0)
        lse_ref[:, 1:2] = m1 + jnp.log(l1)

    full = lambda cb: pl.BlockSpec((t_dim, LANES), lambda p, i, cb=cb: (0, cb + p))
    return pl.pallas_call(
        body, name="fox_attn_fwd", grid=(n_pairs, nq),
        in_specs=[pl.BlockSpec((blk, LANES), lambda p, i: (i, q_cb + p)), full(k_cb), full(v_cb),
                  pl.BlockSpec((None, SUBLANES, t_dim), lambda p, i: (p, 0, 0)),
                  pl.BlockSpec((None, blk, 2), lambda p, i: (p, i, 0))],
        out_specs=[pl.BlockSpec((blk, LANES), lambda p, i: (i, p)), pl.BlockSpec((None, blk, 2), lambda p, i: (p, i, 0))],
        out_shape=[jax.ShapeDtypeStruct((t_dim, n_pairs * LANES), F32), jax.ShapeDtypeStruct((n_pairs, t_dim, 2), F32)],
        compiler_params=_params("parallel", "arbitrary"),
    )(z, z, z, ct, cq)


def _fox_rowdot(z, ct, cq, do, lse, q_cb, k_cb, v_cb, n_pairs, blk):
    t_dim = z.shape[0]
    blk = _pick(t_dim, blk)
    scale = HEAD ** -0.5

    def body(q_ref, k_ref, v_ref, ct_ref, cq_ref, do_ref, lse_ref, out_ref):
        i = pl.program_id(1)
        rowi = lax.broadcasted_iota(jnp.int32, (blk, blk), 0)
        coli = lax.broadcasted_iota(jnp.int32, (blk, blk), 1)
        masks = _head_masks()
        qv, dov = q_ref[...], do_ref[...]
        qs = [(qv * mk).astype(BF16) for mk in masks]
        dos = [(dov * mk).astype(BF16) for mk in masks]
        cqs = [cq_ref[:, hh:hh + 1] for hh in range(2)]
        lses = [lse_ref[:, hh:hh + 1] for hh in range(2)]

        def kv_step(j, carry, masked):
            rows = pl.ds(pl.multiple_of(j * blk, blk), blk)
            kb = k_ref[rows, :].astype(BF16)
            vb = v_ref[rows, :].astype(BF16)
            out = []
            for hh in range(2):
                num, den = carry[2 * hh], carry[2 * hh + 1]
                s = (lax.dot_general(qs[hh], kb, _DOT_DIMS["nt"], preferred_element_type=F32) * scale
                     + (cqs[hh] - ct_ref[hh:hh + 1, rows]))
                pr = jnp.exp(s - lses[hh])
                if masked:
                    pr = jnp.where(rowi >= coli, pr, 0.0)
                dp = lax.dot_general(dos[hh], vb, _DOT_DIMS["nt"], preferred_element_type=F32)
                out += [num + jnp.sum(pr * dp, axis=1, keepdims=True), den + jnp.sum(pr, axis=1, keepdims=True)]
            return tuple(out)

        zero = jnp.zeros((blk, 1), F32)
        carry = lax.fori_loop(0, i, functools.partial(kv_step, masked=False), (zero, zero, zero, zero))
        num0, den0, num1, den1 = kv_step(i, carry, True)
        out_ref[:, 0:1] = num0 / den0
        out_ref[:, 1:2] = num1 / den1

    full = lambda cb: pl.BlockSpec((t_dim, LANES), lambda p, i, cb=cb: (0, cb + p))
    return pl.pallas_call(
        body, name="fox_attn_rowdot", grid=(n_pairs, t_dim // blk),
        in_specs=[pl.BlockSpec((blk, LANES), lambda p, i: (i, q_cb + p)), full(k_cb), full(v_cb),
                  pl.BlockSpec((None, SUBLANES, t_dim), lambda p, i: (p, 0, 0)),
                  pl.BlockSpec((None, blk, 2), lambda p, i: (p, i, 0)),
                  pl.BlockSpec((blk, LANES), lambda p, i: (i, p)),
                  pl.BlockSpec((None, blk, 2), lambda p, i: (p, i, 0))],
        out_specs=pl.BlockSpec((None, blk, 2), lambda p, i: (p, i, 0)),
        out_shape=jax.ShapeDtypeStruct((n_pairs, t_dim, 2), F32),
        compiler_params=_params("parallel", "arbitrary"),
    )(z, z, z, ct, cq, do, lse)


def _fox_bwd(z, ct, cq, rowdot, do, lse, q_cb, k_cb, v_cb, n_pairs, blk):
    t_dim = z.shape[0]
    blk = _pick(t_dim, blk)
    nb = t_dim // blk
    scale = HEAD ** -0.5

    def body(q_ref, k_ref, v_ref, ct_ref, cq_ref, rd_ref, do_ref, lse_ref, dq_ref, dk_ref, dv_ref, dc_ref):
        j = pl.program_id(1)

        @pl.when(j == 0)
        def _():
            dq_ref[...] = jnp.zeros_like(dq_ref)

        rowi = lax.broadcasted_iota(jnp.int32, (blk, blk), 0)
        coli = lax.broadcasted_iota(jnp.int32, (blk, blk), 1)
        krows = pl.ds(pl.multiple_of(j * blk, blk), blk)
        masks = _head_masks()
        kv, vb = k_ref[...], v_ref[...].astype(BF16)
        kb = kv.astype(BF16)
        ks = [(kv * mk).astype(BF16) for mk in masks]
        cks = [ct_ref[hh:hh + 1, krows] for hh in range(2)]

        def q_step(i, carry, masked):
            dk, dv, dcs = carry[0], carry[1], list(carry[2:])
            rows = pl.ds(pl.multiple_of(i * blk, blk), blk)
            qv, dov = q_ref[rows, :], do_ref[rows, :]
            dq = 0.0
            for hh in range(2):
                qh = (qv * masks[hh]).astype(BF16)
                doh = (dov * masks[hh]).astype(BF16)
                s = (lax.dot_general(qh, kb, _DOT_DIMS["nt"], preferred_element_type=F32) * scale
                     + (cq_ref[rows, hh:hh + 1] - cks[hh]))
                pr = jnp.exp(s - lse_ref[rows, hh:hh + 1])
                if masked:
                    pr = jnp.where(rowi >= coli, pr, 0.0)
                dv = dv + lax.dot_general(pr.astype(BF16), doh, _DOT_DIMS["tn"], preferred_element_type=F32)
                dp = lax.dot_general(doh, vb, _DOT_DIMS["nt"], preferred_element_type=F32)
                ds = pr * (dp - rd_ref[rows, hh:hh + 1])
                dsb = ds.astype(BF16)
                dq = dq + jnp.dot(dsb, ks[hh], preferred_element_type=F32)
                dk = dk + lax.dot_general(dsb, qh, _DOT_DIMS["tn"], preferred_element_type=F32)
                dcs[hh] = dcs[hh] - _colsum(ds)
            dq_ref[rows, :] += dq * scale
            return (dk, dv, *dcs)

        zero_row = jnp.zeros((1, blk), F32)
        init = (jnp.zeros((blk, LANES), F32), jnp.zeros((blk, LANES), F32), zero_row, zero_row)
        carry = q_step(j, init, True)
        dk, dv, dc0, dc1 = lax.fori_loop(j + 1, nb, functools.partial(q_step, masked=False), carry)
        dk_ref[...] = dk * scale
        dv_ref[...] = dv
        dc_ref[0:1, :] = dc0
        dc_ref[1:2, :] = dc1
        dc_ref[2:SUBLANES, :] = jnp.zeros((SUBLANES - 2, blk), F32)

    full = lambda: pl.BlockSpec((t_dim, LANES), lambda p, j: (0, p))
    blkspec = pl.BlockSpec((blk, LANES), lambda p, j: (j, p))
    return pl.pallas_call(
        body, name="fox_attn_bwd", grid=(n_pairs, nb),
        in_specs=[pl.BlockSpec((t_dim, LANES), lambda p, j: (0, q_cb + p)),
                  pl.BlockSpec((blk, LANES), lambda p, j: (j, k_cb + p)),
                  pl.BlockSpec((blk, LANES), lambda p, j: (j, v_cb + p)),
                  pl.BlockSpec((None, SUBLANES, t_dim), lambda p, j: (p, 0, 0)),
                  pl.BlockSpec((None, t_dim, 2), lambda p, j: (p, 0, 0)),
                  pl.BlockSpec((None, t_dim, 2), lambda p, j: (p, 0, 0)), full(),
                  pl.BlockSpec((None, t_dim, 2), lambda p, j: (p, 0, 0))],
        out_specs=[full(), blkspec, blkspec, pl.BlockSpec((None, SUBLANES, blk), lambda p, j: (p, 0, j))],
        out_shape=[jax.ShapeDtypeStruct((t_dim, n_pairs * LANES), F32)] * 3
                  + [jax.ShapeDtypeStruct((n_pairs, SUBLANES, t_dim), F32)],
        compiler_params=_params("parallel", "arbitrary"),
    )(z, z, z, ct, cq, rowdot, do, lse)


HBM_SPEC = pl.BlockSpec(memory_space=pltpu.HBM)


def _all_gather(shards, name):
    n = len(shards)

    def body(*refs):
        x_refs, out_refs = refs[:n], refs[n:2 * n]
        send_sems, recv_sems, local_sems = refs[2 * n:]
        x, y, c = lax.axis_index("x"), lax.axis_index("y"), lax.axis_index("c")
        me, sibling = (x, y, c), (x, y, 1 - c)
        chips = [(1 - x, y), (x, 1 - y), (1 - x, 1 - y)]

        def copy(a, k, block, to, from_input=False):
            px, py, pc = block
            slot = out_refs[a].at[4 * px + 2 * py + pc]
            return pltpu.make_async_remote_copy(
                src_ref=x_refs[a] if from_input else slot, dst_ref=slot,
                send_sem=send_sems.at[7 * a + k], recv_sem=recv_sems.at[7 * a + k], device_id=to, device_id_type=MESH)

        mine = [pltpu.make_async_copy(x_refs[a], out_refs[a].at[4 * x + 2 * y + c], local_sems.at[a]) for a in range(n)]
        for cp in mine:
            cp.start()
        first = []
        for a in range(n):
            first += [copy(a, 1 + j, me, (*chip, c), from_input=True) for j, chip in enumerate(chips)]
            first.append(copy(a, 0, me, sibling, from_input=True))
        for cp in first:
            cp.start()
        passed = []
        for a in range(n):
            for j, chip in enumerate(chips):
                copy(a, 1 + j, (*chip, c), me).wait_recv()
                passed.append(copy(a, 4 + j, (*chip, c), sibling))
                passed[-1].start()
        for a in range(n):
            copy(a, 0, sibling, me).wait_recv()
            for j, chip in enumerate(chips):
                copy(a, 4 + j, (*chip, 1 - c), me).wait_recv()
        for cp in first + passed:
            cp.wait_send()
        for cp in mine:
            cp.wait()

    return pl.pallas_call(
        body, name=name, out_shape=[jax.ShapeDtypeStruct((NDEV,) + s.shape, s.dtype) for s in shards],
        in_specs=[HBM_SPEC] * n, out_specs=[HBM_SPEC] * n,
        scratch_shapes=[pltpu.SemaphoreType.DMA((7 * n,)), pltpu.SemaphoreType.DMA((7 * n,)),
                        pltpu.SemaphoreType.DMA((n,))],
    )(*shards)


def _grad_exchange(parts, small, name):
    n = len(parts)

    def body(*refs):
        g_refs, s_ref = refs[:n], refs[n]
        recv_refs, sall_ref = refs[n + 1:2 * n + 1], refs[2 * n + 1]
        send_sems, recv_sems, local_sems = refs[2 * n + 2:]
        x, y, c = lax.axis_index("x"), lax.axis_index("y"), lax.axis_index("c")
        me = 4 * x + 2 * y + c
        own = [pltpu.make_async_copy(g_refs[a].at[me], recv_refs[a].at[me], local_sems.at[a]) for a in range(n)]
        own.append(pltpu.make_async_copy(s_ref, sall_ref.at[me], local_sems.at[n]))
        for cp in own:
            cp.start()
        sends, recvs = [], []
        for a in range(n + 1):
            for k in range(1, NDEV):
                px = 1 - x if k & 4 else x
                py = 1 - y if k & 2 else y
                pc = 1 - c if k & 1 else c
                peer = 4 * px + 2 * py + pc
                sems = dict(send_sem=send_sems.at[7 * a + k - 1], recv_sem=recv_sems.at[7 * a + k - 1],
                            device_id=(px, py, pc), device_id_type=MESH)
                if a < n:
                    src, to_me, from_peer = g_refs[a].at[peer], recv_refs[a].at[me], recv_refs[a].at[peer]
                else:
                    src, to_me, from_peer = s_ref, sall_ref.at[me], sall_ref.at[peer]
                sends.append(pltpu.make_async_remote_copy(src_ref=src, dst_ref=to_me, **sems))
                recvs.append(pltpu.make_async_remote_copy(src_ref=src, dst_ref=from_peer, **sems))
        for cp in sends:
            cp.start()
        for cp in recvs:
            cp.wait_recv()
        for cp in sends:
            cp.wait_send()
        for cp in own:
            cp.wait()

    return pl.pallas_call(
        body, name=name,
        out_shape=[jax.ShapeDtypeStruct(a.shape, a.dtype) for a in parts]
                  + [jax.ShapeDtypeStruct((NDEV,) + small.shape, small.dtype)],
        in_specs=[HBM_SPEC] * (n + 1), out_specs=[HBM_SPEC] * (n + 1),
        scratch_shapes=[pltpu.SemaphoreType.DMA((7 * (n + 1),)), pltpu.SemaphoreType.DMA((7 * (n + 1),)),
                        pltpu.SemaphoreType.DMA((n + 1,))],
    )(*parts, small)


def _adamw(partials, w, m, v, name):
    rows, width = w.shape
    br = _pick(rows, 128, 2 * SUBLANES)

    def body(p_ref, w_ref, m_ref, v_ref, g_out, d_out, m_out, v_out):
        g = p_ref[0].astype(F32)
        for d in range(1, NDEV):
            g = g + p_ref[d].astype(F32)
        mn = ADAM_B1 * m_ref[...] + (1.0 - ADAM_B1) * g
        vn = ADAM_B2 * v_ref[...] + (1.0 - ADAM_B2) * jnp.square(g)
        m_hat = mn / (1.0 - ADAM_B1 ** ADAM_STEP)
        v_hat = vn / (1.0 - ADAM_B2 ** ADAM_STEP)
        g_out[...] = g
        d_out[...] = -ADAM_LR * (m_hat / (jnp.sqrt(v_hat) + ADAM_EPS) + ADAM_WD * w_ref[...])
        m_out[...] = mn
        v_out[...] = vn

    blk = pl.BlockSpec((br, width), lambda i: (i, 0))
    return pl.pallas_call(
        body, name=name, grid=(rows // br,),
        in_specs=[pl.BlockSpec((NDEV, br, width), lambda i: (0, i, 0)), blk, blk, blk],
        out_specs=[pl.BlockSpec((None, br, width), lambda i: (0, i, 0))] * 4,
        out_shape=[jax.ShapeDtypeStruct((1, rows, width), F32)] * 4,
        compiler_params=_params("parallel"),
    )(partials, w, m, v)


def _pack_rows(flat):
    n = flat.shape[0]
    padded = _round_up(n, PACK_ALIGN)
    return jnp.pad(flat, (0, padded - n)).reshape(padded // PACK_W, PACK_W)


def _split_shards(full, axis):
    rows, cols = full.shape
    if axis == 0:
        return full.reshape(NDEV, rows // NDEV, cols)
    width = cols // NDEV
    return jnp.stack([full[:, d * width:(d + 1) * width] for d in range(NDEV)])


def _join_shards(blocks, axis):
    if axis == 0:
        return blocks.reshape(-1, blocks.shape[2])
    return jnp.concatenate([blocks[d] for d in range(NDEV)], axis=1)


SHARDED = (("w_in", 1), ("rw_w_lora_up", 1), ("rw_a_lora_up", 1), ("w_up_rwkv", 1), ("w_up_fox", 1),
           ("w_out", 0), ("ple_proj", 1), ("ple_gate_w", 0))
REPLICATED = ("norm_g", "rw_shift_mu", "rw_w0", "rw_a0", "rw_k_k", "rw_k_a", "rw_r_k", "rw_ln_g", "rw_ln_b",
              "fox_b_f", "ple_norm_g", "final_norm_g")
WEIGHTS = ("norm_g", "w_in", "rw_shift_mu", "rw_w0", "rw_w_lora_up", "rw_a0", "rw_a_lora_up", "rw_k_k", "rw_k_a",
           "rw_r_k", "rw_ln_g", "rw_ln_b", "fox_b_f", "w_up_rwkv", "w_up_fox", "w_out", "ple_proj", "ple_gate_w",
           "ple_norm_g", "final_norm_g")


def _local_step(x, p, tgt, wz, wl, wa, wur, wuf, wo, pp, pg, rep, dims):
    t_dim, d_model, c_rw, lora, c_fox, h_fox, sec = dims
    bt = _pick(t_dim, 128, SUBLANES)
    n_pairs = c_fox // LANES
    row = lambda a: a.reshape(1, -1)
    norm_g, mu, w0, a0 = row(rep["norm_g"]), row(rep["rw_shift_mu"]), row(rep["rw_w0"]), row(rep["rw_a0"])
    k_k, k_a, r_k = row(rep["rw_k_k"]), row(rep["rw_k_a"]), row(rep["rw_r_k"])
    ln_g, ln_b = row(rep["rw_ln_g"]), row(rep["rw_ln_b"])
    g2, g3 = row(rep["ple_norm_g"]), row(rep["final_norm_g"])
    b_f = jnp.pad(row(rep["fox_b_f"]), ((0, 0), (0, LANES - h_fox)))
    e_head = _head_ones(F32)
    c4 = 4 * c_rw
    inv_d = 1.0 / d_model
    decay_k = math.exp(-0.5)

    def norm_in(x_ref, g_ref, h_ref):
        xv = x_ref[...]
        rms = lax.rsqrt(jnp.mean(xv * xv, axis=-1, keepdims=True) + NORM_EPS)
        h_ref[...] = (xv * rms * g_ref[...]).astype(BF16)

    (h,) = _rowcall(norm_in, "norm_in", t_dim, bt, [x], [norm_g], [(d_model, BF16)])
    z = _matmul(h, wz, "nn", "proj_in", bn=1408)

    def rw_values(zs, w0_ref, wl_ref, a0_ref, wa_ref, kk_ref, ka_ref, e_ref):
        k = zs[:, c_rw:2 * c_rw]
        tw = jnp.tanh(zs[:, c4:c4 + lora])
        al = zs[:, c4 + lora:c4 + 2 * lora]
        sw = _sigmoid(w0_ref[...] + jnp.dot(tw.astype(BF16), wl_ref[...], preferred_element_type=F32))
        decay = jnp.exp(-decay_k * sw)
        a = _sigmoid(a0_ref[...] + jnp.dot(al.astype(BF16), wa_ref[...], preferred_element_type=F32))
        kk0 = k * kk_ref[...]
        nrm = jnp.sqrt(_headsum(kk0 * kk0, e_ref[...]))
        inv = 1.0 / jnp.maximum(nrm, 1e-12)
        k2 = k * (1.0 + (a - 1.0) * ka_ref[...])
        return k, tw, al, sw, decay, a, kk0, nrm, inv, k2

    def rw_prep(z_ref, mu_ref, w0_ref, wl_ref, a0_ref, wa_ref, kk_ref, ka_ref, e_ref,
                r_o, w_o, k_o, v_o, kk_o, a_o, g_o, carry):
        _first_step_zero(carry)
        zv = z_ref[...]
        zs = zv + (_shift_down(zv, carry) - zv) * mu_ref[...]
        k, tw, al, sw, decay, a, kk0, nrm, inv, k2 = rw_values(zs, w0_ref, wl_ref, a0_ref, wa_ref, kk_ref, ka_ref, e_ref)
        r_o[...] = zs[:, 0:c_rw]
        w_o[...] = decay
        k_o[...] = k2
        v_o[...] = zs[:, 2 * c_rw:3 * c_rw]
        kk_o[...] = kk0 * inv
        a_o[...] = a
        g_o[...] = zs[:, 3 * c_rw:c4]

    rw_consts = [mu, w0, wl, a0, wa, k_k, k_a, e_head]
    r_s, w_s, k_s, v_s, kk_s, a_s, g_s = _rowcall(
        rw_prep, "rwkv_prep", t_dim, bt, [(z, 0, sec)], rw_consts, [(c_rw, F32)] * 7,
        scratch=[pltpu.VMEM((1, sec), F32)])
    pairs_fwd = max(n for n in (1, 2, 4) if c_rw % (n * LANES) == 0)
    pairs_bwd = pairs_fwd
    y_s, s_all, s_fin = _scan_fwd(r_s, w_s, k_s, v_s, kk_s, a_s, 64, pairs_fwd)

    def rw_post_values(y, r, k2, v, g, lng_ref, lnb_ref, rk_ref, e):
        mean = _headsum(y, e) * (1.0 / HEAD)
        d = y - mean
        rstd = lax.rsqrt(_headsum(d * d, e) * (1.0 / HEAD) + GN_EPS)
        yh = d * rstd
        rk = _headsum(r * k2 * rk_ref[...], e)
        yo = yh * lng_ref[...] + lnb_ref[...] + rk * v
        sg = _sigmoid(g)
        return rstd, yh, rk, yo, sg

    def rw_post(y_ref, r_ref, k_ref, v_ref, g_ref, lng_ref, lnb_ref, rk_ref, e_ref, out_ref):
        g = g_ref[...]
        _, _, _, yo, sg = rw_post_values(y_ref[...], r_ref[...], k_ref[...], v_ref[...], g, lng_ref, lnb_ref, rk_ref, e_ref[...])
        out_ref[...] = (yo * g * sg).astype(BF16)

    (y_rw,) = _rowcall(rw_post, "rwkv_post", t_dim, bt, [y_s, r_s, k_s, v_s, g_s], [ln_g, ln_b, r_k, e_head], [(c_rw, BF16)])

    fl_cb = (sec + 4 * c_fox) // LANES
    hp = _round_up(h_fox, SUBLANES)
    bt_c = _pick(t_dim, 256)
    tri = (jnp.arange(bt_c)[:, None] >= jnp.arange(bt_c)[None, :]).astype(F32)

    rows8 = n_pairs * SUBLANES
    pair_rows = (jnp.arange(rows8)[:, None] // SUBLANES * 2 + jnp.arange(rows8)[:, None] % SUBLANES
                 == jnp.arange(LANES)[None, :]) & (jnp.arange(rows8)[:, None] % SUBLANES < 2)
    pair_rows = pair_rows.astype(F32)

    def fox_decay(fl_ref, bf_ref, tri_ref, sel_ref, ct_ref, cq_ref, carry):
        _first_step_zero(carry)
        lf = _log_sigmoid(fl_ref[...] + bf_ref[...])
        c = jnp.dot(tri_ref[...], lf, precision=HI, preferred_element_type=F32) + carry[...]
        carry[...] = c[bt_c - 1:bt_c, :]
        ct = jnp.dot(sel_ref[...], jnp.transpose(c), precision=HI, preferred_element_type=F32)
        ct_ref[...] = ct.reshape(n_pairs, SUBLANES, bt_c)
        for pair in range(n_pairs):
            cq_ref[pair] = c[:, 2 * pair:2 * pair + 2]

    ct, cq = pl.pallas_call(
        fox_decay, name="fox_decay", grid=(t_dim // bt_c,),
        in_specs=[pl.BlockSpec((bt_c, LANES), lambda i: (i, fl_cb)), pl.BlockSpec((1, LANES), lambda i: (0, 0)),
                  pl.BlockSpec((bt_c, bt_c), lambda i: (0, 0)), pl.BlockSpec((rows8, LANES), lambda i: (0, 0))],
        out_specs=[pl.BlockSpec((n_pairs, SUBLANES, bt_c), lambda i: (0, 0, i)),
                   pl.BlockSpec((n_pairs, bt_c, 2), lambda i: (0, i, 0))],
        out_shape=[jax.ShapeDtypeStruct((n_pairs, SUBLANES, t_dim), F32), jax.ShapeDtypeStruct((n_pairs, t_dim, 2), F32)],
        scratch_shapes=[pltpu.VMEM((1, LANES), F32)], compiler_params=_params("arbitrary"),
    )(z, b_f, tri, pair_rows)
    q_cb = sec // LANES
    k_cb, v_cb = q_cb + n_pairs, q_cb + 2 * n_pairs
    o_fox, lse = _fox_fwd(z, ct, cq, q_cb, k_cb, v_cb, n_pairs, 256)

    def fox_post(o_ref, z_ref, out_ref):
        g = z_ref[:, 3 * c_fox:4 * c_fox]
        out_ref[...] = (o_ref[...] * g * _sigmoid(g)).astype(BF16)

    (y_fox,) = _rowcall(fox_post, "fox_post", t_dim, bt, [o_fox, (z, 1, sec)], [], [(c_fox, BF16)])

    u_rw = _matmul(y_rw, wur, "nn", "up_rwkv")
    u_fox = _matmul(y_fox, wuf, "nn", "up_fox")

    def merge(ur_ref, uf_ref, z_ref, out_ref):
        s1 = _sigmoid(z_ref[:, 0:d_model])
        s2 = _sigmoid(z_ref[:, d_model:2 * d_model])
        out_ref[...] = (s1 * ur_ref[...] + s2 * uf_ref[...]).astype(BF16)

    (merged,) = _rowcall(merge, "merge", t_dim, bt, [u_rw, u_fox, (z, 2, sec)], [], [(d_model, BF16)])
    mo = _matmul(merged, wo, "nn", "proj_out")

    def resid_norm(x_ref, mo_ref, g_ref, x1_ref, n2_ref):
        x1 = x_ref[...] + mo_ref[...]
        rms = lax.rsqrt(jnp.mean(x1 * x1, axis=-1, keepdims=True) + NORM_EPS)
        x1_ref[...] = x1
        n2_ref[...] = (x1 * rms * g_ref[...]).astype(BF16)

    x1, n2 = _rowcall(resid_norm, "resid_norm", t_dim, bt, [x, mo], [g2], [(d_model, F32), (d_model, BF16)])
    ple = _matmul(p, pp, "nn", "ple_proj")
    gl = _matmul(n2, pg, "nn", "ple_gate")

    def head(x1_ref, ple_ref, gl_ref, tgt_ref, g_ref, dx2_ref, dple_ref, dgl_ref, loss_ref, dg3_ref):
        _first_step_zero(loss_ref, dg3_ref)
        sg = _sigmoid(gl_ref[...])
        pl_v = ple_ref[...]
        x2 = x1_ref[...] + pl_v * sg
        rms = lax.rsqrt(jnp.mean(x2 * x2, axis=-1, keepdims=True) + NORM_EPS)
        xn = x2 * rms
        diff = xn * g_ref[...] - tgt_ref[...]
        loss_ref[...] += 0.5 * jnp.sum(jnp.mean(diff * diff, axis=-1, keepdims=True), axis=0, keepdims=True)
        dyf = diff * inv_d
        dg3_ref[...] += _colsum(dyf * xn)
        gy = dyf * g_ref[...]
        dx2 = rms * (gy - xn * jnp.mean(xn * gy, axis=-1, keepdims=True))
        dx2_ref[...] = dx2
        dple_ref[...] = (dx2 * sg).astype(BF16)
        dgl_ref[...] = (dx2 * pl_v * sg * (1.0 - sg)).astype(BF16)

    dx2, dple, dgl, loss, d_g3 = _rowcall(
        head, "head", t_dim, bt, [x1, ple, gl, tgt], [g3], [(d_model, F32), (d_model, BF16), (d_model, BF16)],
        acc_outs=[(1, 1), (1, d_model)])

    d_pp = _matmul(p, dple, "tn", "d_ple_proj", out_dtype=BF16)
    d_pg = _matmul(n2, dgl, "tn", "d_ple_gate", out_dtype=BF16)
    dn2 = _matmul(dgl, pg, "nt", "d_n2")

    def resid_norm_bwd(dx2_ref, dn2_ref, x1_ref, g_ref, dx1_ref, dx1b_ref, dg2_ref):
        _first_step_zero(dg2_ref)
        x1 = x1_ref[...]
        rms = lax.rsqrt(jnp.mean(x1 * x1, axis=-1, keepdims=True) + NORM_EPS)
        xn = x1 * rms
        dn = dn2_ref[...]
        dg2_ref[...] += _colsum(dn * xn)
        gy = dn * g_ref[...]
        dx1 = dx2_ref[...] + rms * (gy - xn * jnp.mean(xn * gy, axis=-1, keepdims=True))
        dx1_ref[...] = dx1
        dx1b_ref[...] = dx1.astype(BF16)

    dx1, dx1b, d_g2 = _rowcall(resid_norm_bwd, "resid_norm_bwd", t_dim, bt, [dx2, dn2, x1], [g2],
                               [(d_model, F32), (d_model, BF16)], acc_outs=[(1, d_model)])
    d_wo = _matmul(merged, dx1b, "tn", "d_w_out", out_dtype=BF16)
    dmerged = _matmul(dx1b, wo, "nt", "d_merged")

    def merge_bwd(dm_ref, ur_ref, uf_ref, z_ref, dur_ref, duf_ref, dzg_ref):
        dm = dm_ref[...]
        s1 = _sigmoid(z_ref[:, 0:d_model])
        s2 = _sigmoid(z_ref[:, d_model:2 * d_model])
        dur_ref[...] = (dm * s1).astype(BF16)
        duf_ref[...] = (dm * s2).astype(BF16)
        dzg_ref[:, 0:d_model] = (dm * ur_ref[...] * s1 * (1.0 - s1)).astype(BF16)
        dzg_ref[:, d_model:2 * d_model] = (dm * uf_ref[...] * s2 * (1.0 - s2)).astype(BF16)
        if sec > 2 * d_model:
            dzg_ref[:, 2 * d_model:sec] = jnp.zeros((dm.shape[0], sec - 2 * d_model), BF16)

    du_rw, du_fox, dz_gate = _rowcall(merge_bwd, "merge_bwd", t_dim, bt, [dmerged, u_rw, u_fox, (z, 2, sec)], [],
                                      [(d_model, BF16), (d_model, BF16), (sec, BF16)])
    d_wur = _matmul(y_rw, du_rw, "tn", "d_w_up_rwkv", out_dtype=BF16)
    d_wuf = _matmul(y_fox, du_fox, "tn", "d_w_up_fox", out_dtype=BF16)
    dy_rw = _matmul(du_rw, wur, "nt", "d_y_rwkv")
    dy_fox = _matmul(du_fox, wuf, "nt", "d_y_fox")

    def fox_post_bwd(dy_ref, o_ref, z_ref, do_ref, dg_ref):
        g = z_ref[:, 3 * c_fox:4 * c_fox]
        sg = _sigmoid(g)
        dy = dy_ref[...]
        do_ref[...] = dy * g * sg
        dg_ref[...] = (dy * o_ref[...] * sg * (1.0 + g * (1.0 - sg))).astype(BF16)

    do_fox, dg_fox = _rowcall(fox_post_bwd, "fox_post_bwd", t_dim, bt, [dy_fox, o_fox, (z, 1, sec)], [],
                              [(c_fox, F32), (c_fox, BF16)])
    rowdot = _fox_rowdot(z, ct, cq, do_fox, lse, q_cb, k_cb, v_cb, n_pairs, 256)
    dq_f, dk_f, dv_f, dc_t = _fox_bwd(z, ct, cq, rowdot, do_fox, lse, q_cb, k_cb, v_cb, n_pairs, 256)
    sel = (jnp.arange(hp)[:, None] // 2 * SUBLANES + jnp.arange(hp)[:, None] % 2 == jnp.arange(rows8)[None, :]).astype(F32)
    tri_rev = (jnp.arange(bt_c)[:, None] >= jnp.arange(bt_c)[None, :]).astype(F32)
    bf_col = b_f.reshape(LANES, 1)[0:hp]
    nbc = t_dim // bt_c

    def fox_decay_bwd(dc_ref, fl_ref, sel_ref, tri_ref, bf_ref, dfl_ref, dbf_ref, carry):
        _first_step_zero(carry, dbf_ref)
        dc = jnp.dot(sel_ref[...], dc_ref[...].reshape(rows8, bt_c), precision=HI, preferred_element_type=F32)
        dlf = jnp.dot(dc, tri_ref[...], precision=HI, preferred_element_type=F32) + carry[...]
        carry[...] = dlf[:, 0:1]
        flt = jnp.transpose(fl_ref[...])[0:hp, :]
        dfl = dlf * _sigmoid(-(flt + bf_ref[...]))
        head_row = lax.broadcasted_iota(jnp.int32, (hp, bt_c), 0)
        dfl = jnp.where(head_row < h_fox, dfl, 0.0)
        dbf_ref[...] += jnp.sum(dfl, axis=1, keepdims=True)
        full = jnp.concatenate([dfl, jnp.zeros((LANES - hp, bt_c), F32)], axis=0) if hp < LANES else dfl
        dfl_ref[...] = jnp.transpose(full).astype(BF16)

    dz_fl, d_bf = pl.pallas_call(
        fox_decay_bwd, name="fox_decay_bwd", grid=(nbc,),
        in_specs=[pl.BlockSpec((n_pairs, SUBLANES, bt_c), lambda i: (0, 0, nbc - 1 - i)),
                  pl.BlockSpec((bt_c, LANES), lambda i: (nbc - 1 - i, fl_cb)),
                  pl.BlockSpec(sel.shape, lambda i: (0, 0)), pl.BlockSpec((bt_c, bt_c), lambda i: (0, 0)),
                  pl.BlockSpec((hp, 1), lambda i: (0, 0))],
        out_specs=[pl.BlockSpec((bt_c, LANES), lambda i: (nbc - 1 - i, 0)), pl.BlockSpec((hp, 1), lambda i: (0, 0))],
        out_shape=[jax.ShapeDtypeStruct((t_dim, LANES), BF16), jax.ShapeDtypeStruct((hp, 1), F32)],
        scratch_shapes=[pltpu.VMEM((hp, 1), F32)], compiler_params=_params("arbitrary"),
    )(dc_t, z, sel, tri_rev, bf_col)

    def rw_post_bwd(dy_ref, y_ref, r_ref, k_ref, v_ref, g_ref, lng_ref, lnb_ref, rk_ref, e_ref,
                    dg_ref, dys_ref, dr_ref, dk_ref, dv_ref, dlng_ref, dlnb_ref, drk_ref):
        _first_step_zero(dlng_ref, dlnb_ref, drk_ref)
        e = e_ref[...]
        dy, r, k2, v, g = dy_ref[...], r_ref[...], k_ref[...], v_ref[...], g_ref[...]
        rstd, yh, rk, yo, sg = rw_post_values(y_ref[...], r, k2, v, g, lng_ref, lnb_ref, rk_ref, e)
        dg_ref[...] = dy * yo * sg * (1.0 + g * (1.0 - sg))
        dyo = dy * g * sg
        dlnb_ref[...] += _colsum(dyo)
        dlng_ref[...] += _colsum(dyo * yh)
        dyh = dyo * lng_ref[...]
        dys_ref[...] = rstd * (dyh - _headsum(dyh, e) * (1.0 / HEAD) - yh * _headsum(dyh * yh, e) * (1.0 / HEAD))
        drk = _headsum(dyo * v, e)
        dv_ref[...] = dyo * rk
        dr_ref[...] = drk * k2 * rk_ref[...]
        dk_ref[...] = drk * r * rk_ref[...]
        drk_ref[...] += _colsum(drk * r * k2)

    dg_rw, dy_s, dr_b, dk_b, dv_b, d_lng, d_lnb, d_rk = _rowcall(
        rw_post_bwd, "rwkv_post_bwd", t_dim, bt, [dy_rw, y_s, r_s, k_s, v_s, g_s], [ln_g, ln_b, r_k, e_head],
        [(c_rw, F32)] * 5, acc_outs=[(1, c_rw)] * 3)
    dr_c, dw_c, dk_c, dv_c, dkk_c, da_c = _scan_bwd(r_s, w_s, k_s, v_s, kk_s, a_s, dy_s, s_all, s_fin, 64, pairs_bwd)

    def rw_prep_bwd(z_ref, dr1, dr2, dw_ref, dk1, dk2_ref, dv1, dv2, dkk_ref, da_ref, dg_ref,
                    mu_ref, w0_ref, wl_ref, a0_ref, wa_ref, kk_ref, ka_ref, e_ref,
                    dzs_ref, tw_ref, al_ref, dwr_ref, dar_ref, dmu_ref, dw0_ref, da0_ref, dkk_acc, dka_acc, carry):
        _first_step_zero(carry, dmu_ref, dw0_ref, da0_ref, dkk_acc, dka_acc)
        e = e_ref[...]
        zv = z_ref[...]
        zp = _shift_down(zv, carry)
        zs = zv + (zp - zv) * mu_ref[...]
        k, tw, al, sw, decay, a, kk0, nrm, inv, k2 = rw_values(zs, w0_ref, wl_ref, a0_ref, wa_ref, kk_ref, ka_ref, e_ref)
        dk2 = dk1[...] + dk2_ref[...]
        da = da_ref[...] + dk2 * k * ka_ref[...]
        dk = dk2 * (1.0 + (a - 1.0) * ka_ref[...])
        dka_acc[...] += _colsum(dk2 * k * (a - 1.0))
        kk = kk0 * inv
        dkk = dkk_ref[...]
        dkk0 = inv * jnp.where(nrm > 1e-12, dkk - kk * _headsum(dkk * kk, e), dkk)
        dk = dk + dkk0 * kk_ref[...]
        dkk_acc[...] += _colsum(dkk0 * k)
        da_raw = da * a * (1.0 - a)
        da0_ref[...] += _colsum(da_raw)
        dw_raw = dw_ref[...] * decay * (-decay_k) * sw * (1.0 - sw)
        dw0_ref[...] += _colsum(dw_raw)
        dar_b, dwr_b = da_raw.astype(BF16), dw_raw.astype(BF16)
        dal = lax.dot_general(dar_b, wa_ref[...], _DOT_DIMS["nt"], preferred_element_type=F32)
        dtw = lax.dot_general(dwr_b, wl_ref[...], _DOT_DIMS["nt"], preferred_element_type=F32)
        dzs_ref[:, 0:c_rw] = dr1[...] + dr2[...]
        dzs_ref[:, c_rw:2 * c_rw] = dk
        dzs_ref[:, 2 * c_rw:3 * c_rw] = dv1[...] + dv2[...]
        dzs_ref[:, 3 * c_rw:c4] = dg_ref[...]
        dzs_ref[:, c4:c4 + lora] = dtw * (1.0 - tw * tw)
        dzs_ref[:, c4 + lora:c4 + 2 * lora] = dal
        if sec > c4 + 2 * lora:
            dzs_ref[:, c4 + 2 * lora:sec] = jnp.zeros((zv.shape[0], sec - c4 - 2 * lora), F32)
        tw_ref[...] = tw.astype(BF16)
        al_ref[...] = al.astype(BF16)
        dwr_ref[...] = dwr_b
        dar_ref[...] = dar_b
        dmu_ref[...] += _colsum(dzs_ref[...] * (zp - zv))

    dzs, tw_b, al_b, dwr_b, dar_b, d_mu, d_w0, d_a0, d_kk, d_ka = _rowcall(
        rw_prep_bwd, "rwkv_prep_bwd", t_dim, bt,
        [(z, 0, sec), dr_c, dr_b, dw_c, dk_c, dk_b, dv_c, dv_b, dkk_c, da_c, dg_rw], rw_consts,
        [(sec, F32), (lora, BF16), (lora, BF16), (c_rw, BF16), (c_rw, BF16)],
        acc_outs=[(1, sec), (1, c_rw), (1, c_rw), (1, c_rw), (1, c_rw)], scratch=[pltpu.VMEM((1, sec), F32)])
    d_wl = _matmul(tw_b, dwr_b, "tn", "d_w_lora", out_dtype=BF16)
    d_wa = _matmul(al_b, dar_b, "tn", "d_a_lora", out_dtype=BF16)

    def shift_bwd(dzs_ref, mu_ref, dz_ref, carry):
        _first_step_zero(carry)
        d = dzs_ref[...]
        nbt = d.shape[0]
        nxt = pltpu.roll(d, nbt - 1, 0)
        rowi = lax.broadcasted_iota(jnp.int32, d.shape, 0)
        nxt = jnp.where(rowi == nbt - 1, carry[...], nxt)
        carry[...] = d[0:1, :]
        m = mu_ref[...]
        dz_ref[...] = (d * (1.0 - m) + nxt * m).astype(BF16)

    (dz_rw,) = _rowcall(shift_bwd, "shift_bwd", t_dim, bt, [dzs], [mu], [(sec, BF16)],
                        scratch=[pltpu.VMEM((1, sec), F32)], reverse=True)

    fox_parts = [dq_f.astype(BF16), dk_f.astype(BF16), dv_f.astype(BF16), dg_fox, dz_fl]
    if sec > 4 * c_fox + LANES:
        fox_parts.append(jnp.zeros((t_dim, sec - 4 * c_fox - LANES), BF16))
    dz = jnp.concatenate([dz_rw] + fox_parts + [dz_gate], axis=1)
    d_wz = _matmul(h, dz, "tn", "d_w_in", out_dtype=BF16, bn=1408)
    dh = _matmul(dz, wz, "nt", "d_h", bk=sec)

    def norm_in_bwd(dh_ref, x_ref, dx1_ref, g_ref, dx_ref, dg1_ref):
        _first_step_zero(dg1_ref)
        xv = x_ref[...]
        rms = lax.rsqrt(jnp.mean(xv * xv, axis=-1, keepdims=True) + NORM_EPS)
        xn = xv * rms
        d = dh_ref[...]
        dg1_ref[...] += _colsum(d * xn)
        gy = d * g_ref[...]
        dx_ref[...] = dx1_ref[...] + rms * (gy - xn * jnp.mean(xn * gy, axis=-1, keepdims=True))

    dx, d_g1 = _rowcall(norm_in_bwd, "norm_in_bwd", t_dim, bt, [dh, x, dx1], [norm_g], [(d_model, F32)],
                        acc_outs=[(1, d_model)])

    full_grads = {"w_in": d_wz, "rw_w_lora_up": d_wl, "rw_a_lora_up": d_wa, "w_up_rwkv": d_wur, "w_up_fox": d_wuf,
                  "w_out": d_wo, "ple_proj": d_pp, "ple_gate_w": d_pg}
    rep_grads = {"norm_g": d_g1, "rw_shift_mu": d_mu[:, 0:c4 + 2 * lora], "rw_w0": d_w0, "rw_a0": d_a0, "rw_k_k": d_kk,
                 "rw_k_a": d_ka, "rw_r_k": d_rk, "rw_ln_g": d_lng, "rw_ln_b": d_lnb, "fox_b_f": d_bf[0:h_fox, 0],
                 "ple_norm_g": d_g2, "final_norm_g": d_g3}
    return loss[0, 0], dx, full_grads, rep_grads


def kernel(x, p, norm_g, w_in, rw_shift_mu, rw_w0, rw_w_lora_up, rw_a0, rw_a_lora_up, rw_k_k, rw_k_a, rw_r_k, rw_ln_g, rw_ln_b, fox_b_f, w_up_rwkv, w_up_fox, w_out, ple_proj, ple_gate_w, ple_norm_g, final_norm_g, loss_target, m_norm_g, m_w_in, m_rw_shift_mu, m_rw_w0, m_rw_w_lora_up, m_rw_a0, m_rw_a_lora_up, m_rw_k_k, m_rw_k_a, m_rw_r_k, m_rw_ln_g, m_rw_ln_b, m_fox_b_f, m_w_up_rwkv, m_w_up_fox, m_w_out, m_ple_proj, m_ple_gate_w, m_ple_norm_g, m_final_norm_g, v_norm_g, v_w_in, v_rw_shift_mu, v_rw_w0, v_rw_w_lora_up, v_rw_a0, v_rw_a_lora_up, v_rw_k_k, v_rw_k_a, v_rw_r_k, v_rw_ln_g, v_rw_ln_b, v_fox_b_f, v_w_up_rwkv, v_w_up_fox, v_w_out, v_ple_proj, v_ple_gate_w, v_ple_norm_g, v_final_norm_g):
    args = locals()
    w = {n: args[n] for n in WEIGHTS}
    mom = {n: args["m_" + n] for n in WEIGHTS}
    var = {n: args["v_" + n] for n in WEIGHTS}

    t_dim, d_model = x.shape[1], x.shape[2]
    c_rw, lora = rw_w0.shape[1], rw_w_lora_up.shape[1]
    h_fox = fox_b_f.shape[1]
    c_fox = h_fox * HEAD
    rw_cols, fox_cols, gate_cols = 4 * c_rw + 2 * lora, 4 * c_fox + h_fox, 2 * d_model
    sec = max(rw_cols, 4 * c_fox + LANES, _round_up(gate_cols, LANES))
    assert c_rw % LANES == 0 and c_fox % LANES == 0 and rw_cols % LANES == 0 and h_fox <= LANES and d_model % LANES == 0
    assert rw_a_lora_up.shape[1] == lora and w_in.shape[2] * NDEV == rw_cols + fox_cols + gate_cols

    gathered = _all_gather([w[n][0].astype(BF16) for n, _ in SHARDED], "gather_weights")
    full = {n: _join_shards(g, ax) for (n, ax), g in zip(SHARDED, gathered)}

    def to_sections(wi):
        pad = lambda a, width: jnp.pad(a, ((0, 0), (0, width - a.shape[1])))
        return jnp.concatenate([pad(wi[:, :rw_cols], sec), pad(wi[:, rw_cols:rw_cols + fox_cols], sec),
                                pad(wi[:, rw_cols + fox_cols:], sec)], axis=1)

    def from_sections(g):
        return jnp.concatenate([g[:, :rw_cols], g[:, sec:sec + fox_cols], g[:, 2 * sec:2 * sec + gate_cols]], axis=1)

    rep = {n: w[n] for n in REPLICATED}
    dims = (t_dim, d_model, c_rw, lora, c_fox, h_fox, sec)
    loss_local, grad_x, full_grads, rep_grads = _local_step(
        x[0], p[0, 0], loss_target[0], to_sections(full["w_in"]), full["rw_w_lora_up"], full["rw_a_lora_up"],
        full["w_up_rwkv"], full["w_up_fox"], full["w_out"], full["ple_proj"], full["ple_gate_w"], rep, dims)
    full_grads["w_in"] = from_sections(full_grads["w_in"])

    rep_sizes = [w[n].size for n in REPLICATED]
    rep_offs = [sum(rep_sizes[:i]) for i in range(len(rep_sizes))]
    pack_rep = lambda tree: _pack_rows(jnp.concatenate([tree[n].astype(F32).reshape(-1) for n in REPLICATED]))
    *recv, small_all = _grad_exchange([_split_shards(full_grads[n], ax) for n, ax in SHARDED], pack_rep(rep_grads),
                                      "exchange_grads")
    kinds = ("grad", "delta", "new_m", "new_v")
    outs = {}
    for (n, _), partials in zip(SHARDED, recv):
        for kind, buf in zip(kinds, _adamw(partials, w[n][0], mom[n][0], var[n][0], "adamw_" + n)):
            outs[kind, n] = buf
    for kind, buf in zip(kinds, _adamw(small_all, pack_rep(w), pack_rep(mom), pack_rep(var), "adamw_replicated")):
        flat = buf.reshape(-1)
        for n, o, s in zip(REPLICATED, rep_offs, rep_sizes):
            outs[kind, n] = flat[o:o + s].reshape(w[n].shape)
    loss = lax.psum(loss_local, MESH_AXES)
    return (loss, grad_x[None], *[outs[kind, n] for kind in ("grad", "delta", "new_m", "new_v") for n in WEIGHTS])
```

```python
import functools
import math

import jax
import jax.numpy as jnp
from jax import lax
from jax.experimental import pallas as pl
from jax.experimental.pallas import tpu as pltpu

F32, BF16 = jnp.float32, jnp.bfloat16
HI = lax.Precision.HIGHEST
LANES = 128
SUBLANES = 8
HEAD = 64
NORM_EPS = 1e-6
GN_EPS = 64e-5
VMEM_LIMIT = 56 * 1024 * 1024
NDEV = 8
PACK_W = 1024
PACK_ALIGN = 16 * PACK_W
MESH_AXES = ("x", "y", "c")
MESH = pl.DeviceIdType.MESH

ADAM_LR, ADAM_B1, ADAM_B2, ADAM_EPS, ADAM_WD, ADAM_STEP = 0.001, 0.9, 0.999, 1e-08, 0.01, 10


def _round_up(n, m):
    return (n + m - 1) // m * m


def _pick(dim, pref, align=LANES):
    if dim <= pref:
        return dim
    best = None
    for cand in range(align, pref + 1, align):
        if dim % cand == 0:
            best = cand
    return dim if best is None else best


def _params(*sem):
    return pltpu.CompilerParams(dimension_semantics=sem, vmem_limit_bytes=VMEM_LIMIT)


def _sigmoid(v):
    return jax.nn.sigmoid(v)


def _log_sigmoid(v):
    return jnp.minimum(v, 0.0) - jnp.log(1.0 + jnp.exp(-jnp.abs(v)))


_DOT_DIMS = {"nn": (((1,), (0,)), ((), ())), "nt": (((1,), (1,)), ((), ())), "tn": (((0,), (0,)), ((), ()))}


def _matmul(a, b, mode, name, out_dtype=F32, bm=512, bn=1024, bk=2048):
    if mode == "tn":
        k_dim, m_dim = a.shape
    else:
        m_dim, k_dim = a.shape
    n_dim = b.shape[0] if mode == "nt" else b.shape[1]
    bm, bn, bk = _pick(m_dim, bm), _pick(n_dim, bn), _pick(k_dim, bk)
    nk = k_dim // bk

    def body(a_ref, b_ref, o_ref, *acc):
        prod = lax.dot_general(a_ref[...].astype(BF16), b_ref[...].astype(BF16), _DOT_DIMS[mode],
                               preferred_element_type=F32)
        if nk == 1:
            o_ref[...] = prod.astype(o_ref.dtype)
            return
        acc_ref, k = acc[0], pl.program_id(2)

        @pl.when(k == 0)
        def _():
            acc_ref[...] = prod

        @pl.when(k > 0)
        def _():
            acc_ref[...] += prod

        @pl.when(k == nk - 1)
        def _():
            o_ref[...] = acc_ref[...].astype(o_ref.dtype)

    if mode == "tn":
        a_spec = pl.BlockSpec((bk, bm), lambda i, j, k: (k, i))
    else:
        a_spec = pl.BlockSpec((bm, bk), lambda i, j, k: (i, k))
    if mode == "nt":
        b_spec = pl.BlockSpec((bn, bk), lambda i, j, k: (j, k))
    else:
        b_spec = pl.BlockSpec((bk, bn), lambda i, j, k: (k, j))
    return pl.pallas_call(
        body, name=name, grid=(m_dim // bm, n_dim // bn, nk),
        in_specs=[a_spec, b_spec], out_specs=pl.BlockSpec((bm, bn), lambda i, j, k: (i, j)),
        out_shape=jax.ShapeDtypeStruct((m_dim, n_dim), out_dtype),
        scratch_shapes=[pltpu.VMEM((bm, bn), F32)] if nk > 1 else [],
        compiler_params=_params("parallel", "parallel", "arbitrary"),
    )(a, b)


def _rowcall(body, name, t_dim, bt, row_ins, const_ins, row_outs, acc_outs=(), scratch=(), reverse=False):
    nt = t_dim // bt

    def rmap(i):
        return nt - 1 - i if reverse else i

    in_specs, args = [], []
    for item in row_ins:
        arr, cb, w = item if isinstance(item, tuple) else (item, 0, item.shape[1])
        in_specs.append(pl.BlockSpec((bt, w), lambda i, cb=cb: (rmap(i), cb)))
        args.append(arr)
    for arr in const_ins:
        in_specs.append(pl.BlockSpec(arr.shape, lambda i, nd=arr.ndim: (0,) * nd))
        args.append(arr)
    out_specs = [pl.BlockSpec((bt, w), lambda i: (rmap(i), 0)) for w, _ in row_outs]
    out_shape = [jax.ShapeDtypeStruct((t_dim, w), dt) for w, dt in row_outs]
    for shp in acc_outs:
        out_specs.append(pl.BlockSpec(shp, lambda i, nd=len(shp): (0,) * nd))
        out_shape.append(jax.ShapeDtypeStruct(shp, F32))
    return pl.pallas_call(
        body, name=name, grid=(nt,), in_specs=in_specs, out_specs=out_specs, out_shape=out_shape,
        scratch_shapes=list(scratch), compiler_params=_params("arbitrary"),
    )(*args)


def _first_step_zero(*refs):
    @pl.when(pl.program_id(0) == 0)
    def _():
        for r in refs:
            r[...] = jnp.zeros_like(r)


def _colsum(v):
    return jnp.sum(v, axis=0, keepdims=True)


def _headsum(v, e):
    parts = [jnp.dot(v[:, p * LANES:(p + 1) * LANES], e, precision=HI, preferred_element_type=F32)
             for p in range(v.shape[1] // LANES)]
    return parts[0] if len(parts) == 1 else jnp.concatenate(parts, axis=1)


def _shift_down(v, carry_ref):
    bt = v.shape[0]
    prev = pltpu.roll(v, 1, 0)
    row = lax.broadcasted_iota(jnp.int32, v.shape, 0)
    prev = jnp.where(row == 0, carry_ref[...], prev)
    carry_ref[...] = v[bt - 1:bt, :]
    return prev


def _pair_consts():
    lane = lax.broadcasted_iota(jnp.int32, (1, LANES), 1)
    m0 = (lane < HEAD).astype(F32)
    m1 = 1.0 - m0
    sub = lax.broadcasted_iota(jnp.int32, (HEAD, LANES), 0)
    lane2 = lax.broadcasted_iota(jnp.int32, (HEAD, LANES), 1)
    i0 = (lane2 == sub).astype(F32)
    i1 = (lane2 == sub + HEAD).astype(F32)
    return m0, m1, i0, i1


def _head_masks():
    lane = lax.broadcasted_iota(jnp.int32, (1, LANES), 1)
    first = (lane < HEAD).astype(F32)
    return first, 1.0 - first


def _head_ones(dtype=BF16):
    lane = jnp.arange(LANES)
    return (lane[:, None] // HEAD == lane[None, :] // HEAD).astype(dtype)


def _lanesum(v):
    return jnp.sum(v, axis=1, keepdims=True)


def _split_bf16(v):
    hi = v.astype(BF16).astype(F32)
    rest = v - hi
    mid = rest.astype(BF16).astype(F32)
    return hi, mid, (rest - mid).astype(BF16).astype(F32)


def _col_pair(parts, j, i01, e_bf16):
    lhs = jnp.concatenate([(part[j:j + 1] * i01).astype(BF16) for part in parts], axis=0)
    out = jnp.dot(lhs, e_bf16, preferred_element_type=F32)
    return (out[0:HEAD] + out[HEAD:2 * HEAD]) + out[2 * HEAD:3 * HEAD]


def _col_tiles(out_ref, src_ref, rows, lanes, i01, e_bf16):
    for sl in lanes:
        parts = _split_bf16(src_ref[rows, sl])
        for j in range(SUBLANES):
            out_ref[j, :, sl] = _col_pair(parts, j, i01, e_bf16)


def _row_pair(c0, c1, i0, i1):
    return _colsum(c0 * i0 + c1 * i1)


def _scan_fwd(r, w, k, v, kk, a, tc, npb):
    t_dim, c_dim = r.shape
    wb = LANES * npb
    tc = _pick(t_dim, tc, SUBLANES)

    def body(r_ref, w_ref, k_ref, v_ref, kk_ref, a_ref, e_ref, y_ref, sall_ref, sfin_ref, s_ref):
        @pl.when(pl.program_id(1) == 0)
        def _():
            s_ref[...] = jnp.zeros_like(s_ref)

        m0, m1, i0, i1 = _pair_consts()
        i01 = i0 + i1
        e = e_ref[...]
        sub8 = lax.broadcasted_iota(jnp.int32, (SUBLANES, LANES), 0)
        lanes = [slice(q * LANES, (q + 1) * LANES) for q in range(npb)]
        ng = tc // SUBLANES

        def lanesums(tiles):
            sums = _lanesum(jnp.concatenate(tiles, axis=0))
            return [sums[i * HEAD:(i + 1) * HEAD] for i in range(len(tiles))]

        def halves(s, row):
            return [s * (row * m0), s * (row * m1)]

        def group(gi, carry):
            base = pl.multiple_of(gi * SUBLANES, SUBLANES)
            rows = pl.ds(base, SUBLANES)
            s = list(carry)
            r8 = [r_ref[rows, sl] for sl in lanes]
            w8 = [w_ref[rows, sl] for sl in lanes]
            k8 = [k_ref[rows, sl] for sl in lanes]
            kk8 = [kk_ref[rows, sl] for sl in lanes]
            b8 = [kk8[q] * a_ref[rows, lanes[q]] for q in range(npb)]
            v8 = [_split_bf16(v_ref[rows, sl]) for sl in lanes]
            y8 = [jnp.zeros((SUBLANES, LANES), F32)] * npb
            for j in range(SUBLANES + 1):
                one, before = slice(j, j + 1), slice(j - 1, j)
                tiles = []
                for q in range(npb):
                    if j < SUBLANES:
                        tiles += halves(s[q], kk8[q][one])
                    if j > 0:
                        tiles += halves(s[q], r8[q][before])
                cols = lanesums(tiles)
                per = len(tiles) // npb
                for q in range(npb):
                    mine = cols[q * per:(q + 1) * per]
                    if j > 0:
                        y8[q] = jnp.where(sub8 == j - 1, _row_pair(mine[-2], mine[-1], i0, i1), y8[q])
                    if j < SUBLANES:
                        sall_ref[base + j, :, lanes[q]] = s[q]
                        sb = mine[0] * m0 + mine[1] * m1
                        s[q] = s[q] * w8[q][one] - sb * b8[q][one] + _col_pair(v8[q], j, i01, e) * k8[q][one]
            for q in range(npb):
                y_ref[rows, lanes[q]] = y8[q]
            return tuple(s)

        init = tuple(s_ref[:, q * LANES:(q + 1) * LANES] for q in range(npb))
        fin = lax.fori_loop(0, ng, group, init)
        for q in range(npb):
            s_ref[:, q * LANES:(q + 1) * LANES] = fin[q]
            sfin_ref[:, q * LANES:(q + 1) * LANES] = fin[q]

    row = pl.BlockSpec((tc, wb), lambda p, c: (c, p))
    return pl.pallas_call(
        body, name="rwkv_scan_fwd", grid=(c_dim // wb, t_dim // tc),
        in_specs=[row] * 6 + [pl.BlockSpec((LANES, LANES), lambda p, c: (0, 0))],
        out_specs=[row, pl.BlockSpec((tc, HEAD, wb), lambda p, c: (c, 0, p)), pl.BlockSpec((HEAD, wb), lambda p, c: (0, p))],
        out_shape=[jax.ShapeDtypeStruct((t_dim, c_dim), F32), jax.ShapeDtypeStruct((t_dim, HEAD, c_dim), F32),
                   jax.ShapeDtypeStruct((HEAD, c_dim), F32)],
        scratch_shapes=[pltpu.VMEM((HEAD, wb), F32)],
        compiler_params=_params("parallel", "arbitrary"),
    )(r, w, k, v, kk, a, _head_ones())


def _scan_bwd(r, w, k, v, kk, a, dy, sall, sfin, tc, npb):
    t_dim, c_dim = r.shape
    wb = LANES * npb
    tc = _pick(t_dim, tc, SUBLANES)
    nc = t_dim // tc

    def body(r_ref, w_ref, k_ref, v_ref, kk_ref, a_ref, dy_ref, sall_ref, sfin_ref, e_ref,
             dr_ref, dw_ref, dk_ref, dv_ref, dkk_ref, da_ref, ds_ref, sn_ref):
        @pl.when(pl.program_id(1) == 0)
        def _():
            ds_ref[...] = jnp.zeros_like(ds_ref)
            sn_ref[...] = sfin_ref[...]

        m0, m1, i0, i1 = _pair_consts()
        i01 = i0 + i1
        e = e_ref[...]
        sub8 = lax.broadcasted_iota(jnp.int32, (SUBLANES, LANES), 0)
        lanes = [slice(q * LANES, (q + 1) * LANES) for q in range(npb)]
        ng = tc // SUBLANES

        def halves(s, row):
            return [s * (row * m0), s * (row * m1)]

        def group(gi, carry):
            base = pl.multiple_of((ng - 1 - gi) * SUBLANES, SUBLANES)
            rows = pl.ds(base, SUBLANES)
            ds = list(carry)
            r8 = [r_ref[rows, sl] for sl in lanes]
            w8 = [w_ref[rows, sl] for sl in lanes]
            k8 = [k_ref[rows, sl] for sl in lanes]
            kk8 = [kk_ref[rows, sl] for sl in lanes]
            a8 = [a_ref[rows, sl] for sl in lanes]
            v8 = [_split_bf16(v_ref[rows, sl]) for sl in lanes]
            dy8 = [_split_bf16(dy_ref[rows, sl]) for sl in lanes]
            zero8 = jnp.zeros((SUBLANES, LANES), F32)
            dr8, dw8, dk8, dv8, dkk8, da8 = ([zero8] * npb for _ in range(6))
            for j in reversed(range(SUBLANES)):
                one = slice(j, j + 1)
                here = sub8 == j
                d, s_prev, tiles = [], [], []
                for q in range(npb):
                    dyb = _col_pair(dy8[q], j, i01, e)
                    s_prev.append(sall_ref[base + j, :, lanes[q]])
                    dr8[q] = jnp.where(here, _colsum(sn_ref[:, lanes[q]] * dyb), dr8[q])
                    sn_ref[:, lanes[q]] = s_prev[q]
                    d.append(ds[q] + dyb * r8[q][one])
                for q in range(npb):
                    tiles += halves(d[q], kk8[q][one] * a8[q][one])
                for q in range(npb):
                    tiles += halves(d[q], k8[q][one]) + halves(s_prev[q], kk8[q][one])
                sums = _lanesum(jnp.concatenate(tiles, axis=0))
                cols = [sums[i * HEAD:(i + 1) * HEAD] for i in range(len(tiles))]
                for q in range(npb):
                    kkr, ar = kk8[q][one], a8[q][one]
                    dsb = -(cols[2 * q] * m0 + cols[2 * q + 1] * m1)
                    ds[q] = d[q] * w8[q][one] + dsb * kkr
                    rest = cols[2 * npb + 4 * q:2 * npb + 4 * q + 4]
                    dv8[q] = jnp.where(here, _row_pair(rest[0], rest[1], i0, i1), dv8[q])
                    sb = rest[2] * m0 + rest[3] * m1
                    db = -_colsum(d[q] * sb)
                    dk8[q] = jnp.where(here, _colsum(d[q] * _col_pair(v8[q], j, i01, e)), dk8[q])
                    dw8[q] = jnp.where(here, _colsum(d[q] * s_prev[q]), dw8[q])
                    dkk8[q] = jnp.where(here, _colsum(s_prev[q] * dsb) + db * ar, dkk8[q])
                    da8[q] = jnp.where(here, db * kkr, da8[q])
            for q in range(npb):
                sl = lanes[q]
                dr_ref[rows, sl], dw_ref[rows, sl], dk_ref[rows, sl] = dr8[q], dw8[q], dk8[q]
                dv_ref[rows, sl], dkk_ref[rows, sl], da_ref[rows, sl] = dv8[q], dkk8[q], da8[q]
            return tuple(ds)

        init = tuple(ds_ref[:, q * LANES:(q + 1) * LANES] for q in range(npb))
        fin = lax.fori_loop(0, ng, group, init)
        for q in range(npb):
            ds_ref[:, q * LANES:(q + 1) * LANES] = fin[q]

    row = pl.BlockSpec((tc, wb), lambda p, c: (nc - 1 - c, p))
    return pl.pallas_call(
        body, name="rwkv_scan_bwd", grid=(c_dim // wb, nc),
        in_specs=[row] * 7 + [pl.BlockSpec((tc, HEAD, wb), lambda p, c: (nc - 1 - c, 0, p)),
                              pl.BlockSpec((HEAD, wb), lambda p, c: (0, p)),
                              pl.BlockSpec((LANES, LANES), lambda p, c: (0, 0))],
        out_specs=[row] * 6, out_shape=[jax.ShapeDtypeStruct((t_dim, c_dim), F32)] * 6,
        scratch_shapes=[pltpu.VMEM((HEAD, wb), F32)] * 2,
        compiler_params=_params("parallel", "arbitrary"),
    )(r, w, k, v, kk, a, dy, sall, sfin, _head_ones())


def _fox_fwd(z, ct, cq, q_cb, k_cb, v_cb, n_pairs, blk):
    t_dim = z.shape[0]
    blk = _pick(t_dim, blk)
    nq = t_dim // blk
    scale = HEAD ** -0.5

    def body(q_ref, k_ref, v_ref, ct_ref, cq_ref, o_ref, lse_ref):
        i = pl.program_id(1)
        rowi = lax.broadcasted_iota(jnp.int32, (blk, blk), 0)
        coli = lax.broadcasted_iota(jnp.int32, (blk, blk), 1)
        masks = _head_masks()
        qv = q_ref[...]
        qs = [(qv * mk).astype(BF16) for mk in masks]
        cqs = [cq_ref[:, hh:hh + 1] for hh in range(2)]

        def kv_step(j, carry, masked):
            rows = pl.ds(pl.multiple_of(j * blk, blk), blk)
            kb = k_ref[rows, :].astype(BF16)
            vv = v_ref[rows, :]
            stats, acc = list(carry[:4]), carry[4]
            rescale, add = 0.0, 0.0
            for hh in range(2):
                m, l = stats[2 * hh], stats[2 * hh + 1]
                s = (lax.dot_general(qs[hh], kb, _DOT_DIMS["nt"], preferred_element_type=F32) * scale
                     + (cqs[hh] - ct_ref[hh:hh + 1, rows]))
                if masked:
                    s = jnp.where(rowi >= coli, s, -jnp.inf)
                m_new = jnp.maximum(m, jnp.max(s, axis=1, keepdims=True))
                alpha = jnp.exp(m - m_new)
                pr = jnp.exp(s - m_new)
                stats[2 * hh], stats[2 * hh + 1] = m_new, l * alpha + jnp.sum(pr, axis=1, keepdims=True)
                rescale = rescale + alpha * masks[hh]
                hi = pr.astype(BF16)
                both = jnp.concatenate([hi, (pr - hi.astype(F32)).astype(BF16)], axis=1)
                vh = (vv * masks[hh]).astype(BF16)
                add = add + jnp.dot(both, jnp.concatenate([vh, vh], axis=0), preferred_element_type=F32)
            return (*stats, acc * rescale + add)

        neg, zero = jnp.full((blk, 1), -jnp.inf, F32), jnp.zeros((blk, 1), F32)
        carry = lax.fori_loop(0, i, functools.partial(kv_step, masked=False),
                              (neg, zero, neg, zero, jnp.zeros((blk, LANES), F32)))
        m0, l0, m1, l1, acc = kv_step(i, carry, True)
        o_ref[...] = acc * (masks[0] / l0 + masks[1] / l1)
        lse_ref[:, 0:1] = m0 + jnp.log(l0)
        lse_ref[:, 1:2] = m1 + jnp.log(l1)

    full = lambda cb: pl.BlockSpec((t_dim, LANES), lambda p, i, cb=cb: (0, cb + p))
    return pl.pallas_call(
        body, name="fox_attn_fwd", grid=(n_pairs, nq),
        in_specs=[pl.BlockSpec((blk, LANES), lambda p, i: (i, q_cb + p)), full(k_cb), full(v_cb),
                  pl.BlockSpec((None, SUBLANES, t_dim), lambda p, i: (p, 0, 0)),
                  pl.BlockSpec((None, blk, 2), lambda p, i: (p, i, 0))],
        out_specs=[pl.BlockSpec((blk, LANES), lambda p, i: (i, p)), pl.BlockSpec((None, blk, 2), lambda p, i: (p, i, 0))],
        out_shape=[jax.ShapeDtypeStruct((t_dim, n_pairs * LANES), F32), jax.ShapeDtypeStruct((n_pairs, t_dim, 2), F32)],
        compiler_params=_params("parallel", "arbitrary"),
    )(z, z, z, ct, cq)


def _fox_rowdot(z, ct, cq, do, lse, q_cb, k_cb, v_cb, n_pairs, blk):
    t_dim = z.shape[0]
    blk = _pick(t_dim, blk)
    scale = HEAD ** -0.5

    def body(q_ref, k_ref, v_ref, ct_ref, cq_ref, do_ref, lse_ref, out_ref):
        i = pl.program_id(1)
        rowi = lax.broadcasted_iota(jnp.int32, (blk, blk), 0)
        coli = lax.broadcasted_iota(jnp.int32, (blk, blk), 1)
        masks = _head_masks()
        qv, dov = q_ref[...], do_ref[...]
        qs = [(qv * mk).astype(BF16) for mk in masks]
        dos = [(dov * mk).astype(BF16) for mk in masks]
        cqs = [cq_ref[:, hh:hh + 1] for hh in range(2)]
        lses = [lse_ref[:, hh:hh + 1] for hh in range(2)]

        def kv_step(j, carry, masked):
            rows = pl.ds(pl.multiple_of(j * blk, blk), blk)
            kb = k_ref[rows, :].astype(BF16)
            vb = v_ref[rows, :].astype(BF16)
            out = []
            for hh in range(2):
                num, den = carry[2 * hh], carry[2 * hh + 1]
                s = (lax.dot_general(qs[hh], kb, _DOT_DIMS["nt"], preferred_element_type=F32) * scale
                     + (cqs[hh] - ct_ref[hh:hh + 1, rows]))
                pr = jnp.exp(s - lses[hh])
                if masked:
                    pr = jnp.where(rowi >= coli, pr, 0.0)
                dp = lax.dot_general(dos[hh], vb, _DOT_DIMS["nt"], preferred_element_type=F32)
                out += [num + jnp.sum(pr * dp, axis=1, keepdims=True), den + jnp.sum(pr, axis=1, keepdims=True)]
            return tuple(out)

        zero = jnp.zeros((blk, 1), F32)
        carry = lax.fori_loop(0, i, functools.partial(kv_step, masked=False), (zero, zero, zero, zero))
        num0, den0, num1, den1 = kv_step(i, carry, True)
        out_ref[:, 0:1] = num0 / den0
        out_ref[:, 1:2] = num1 / den1

    full = lambda cb: pl.BlockSpec((t_dim, LANES), lambda p, i, cb=cb: (0, cb + p))
    return pl.pallas_call(
        body, name="fox_attn_rowdot", grid=(n_pairs, t_dim // blk),
        in_specs=[pl.BlockSpec((blk, LANES), lambda p, i: (i, q_cb + p)), full(k_cb), full(v_cb),
                  pl.BlockSpec((None, SUBLANES, t_dim), lambda p, i: (p, 0, 0)),
                  pl.BlockSpec((None, blk, 2), lambda p, i: (p, i, 0)),
                  pl.BlockSpec((blk, LANES), lambda p, i: (i, p)),
                  pl.BlockSpec((None, blk, 2), lambda p, i: (p, i, 0))],
        out_specs=pl.BlockSpec((None, blk, 2), lambda p, i: (p, i, 0)),
        out_shape=jax.ShapeDtypeStruct((n_pairs, t_dim, 2), F32),
        compiler_params=_params("parallel", "arbitrary"),
    )(z, z, z, ct, cq, do, lse)


def _fox_bwd(z, ct, cq, rowdot, do, lse, q_cb, k_cb, v_cb, n_pairs, blk):
    t_dim = z.shape[0]
    blk = _pick(t_dim, blk)
    nb = t_dim // blk
    scale = HEAD ** -0.5

    def body(q_ref, k_ref, v_ref, ct_ref, cq_ref, rd_ref, do_ref, lse_ref, dq_ref, dk_ref, dv_ref, dc_ref):
        j = pl.program_id(1)

        @pl.when(j == 0)
        def _():
            dq_ref[...] = jnp.zeros_like(dq_ref)

        rowi = lax.broadcasted_iota(jnp.int32, (blk, blk), 0)
        coli = lax.broadcasted_iota(jnp.int32, (blk, blk), 1)
        krows = pl.ds(pl.multiple_of(j * blk, blk), blk)
        masks = _head_masks()
        kv, vb = k_ref[...], v_ref[...].astype(BF16)
        kb = kv.astype(BF16)
        ks = [(kv * mk).astype(BF16) for mk in masks]
        cks = [ct_ref[hh:hh + 1, krows] for hh in range(2)]

        def q_step(i, carry, masked):
            dk, dv, dcs = carry[0], carry[1], list(carry[2:])
            rows = pl.ds(pl.multiple_of(i * blk, blk), blk)
            qv, dov = q_ref[rows, :], do_ref[rows, :]
            dq = 0.0
            for hh in range(2):
                qh = (qv * masks[hh]).astype(BF16)
                doh = (dov * masks[hh]).astype(BF16)
                s = (lax.dot_general(qh, kb, _DOT_DIMS["nt"], preferred_element_type=F32) * scale
                     + (cq_ref[rows, hh:hh + 1] - cks[hh]))
                pr = jnp.exp(s - lse_ref[rows, hh:hh + 1])
                if masked:
                    pr = jnp.where(rowi >= coli, pr, 0.0)
                dv = dv + lax.dot_general(pr.astype(BF16), doh, _DOT_DIMS["tn"], preferred_element_type=F32)
                dp = lax.dot_general(doh, vb, _DOT_DIMS["nt"], preferred_element_type=F32)
                ds = pr * (dp - rd_ref[rows, hh * HEAD:hh * HEAD + 1])
                dsb = ds.astype(BF16)
                dq = dq + jnp.dot(dsb, ks[hh], preferred_element_type=F32)
                dk = dk + lax.dot_general(dsb, qh, _DOT_DIMS["tn"], preferred_element_type=F32)
                dcs[hh] = dcs[hh] - _colsum(ds)
            dq_ref[rows, :] += dq * scale
            return (dk, dv, *dcs)

        zero_row = jnp.zeros((1, blk), F32)
        init = (jnp.zeros((blk, LANES), F32), jnp.zeros((blk, LANES), F32), zero_row, zero_row)
        carry = q_step(j, init, True)
        dk, dv, dc0, dc1 = lax.fori_loop(j + 1, nb, functools.partial(q_step, masked=False), carry)
        dk_ref[...] = dk * scale
        dv_ref[...] = dv
        dc_ref[0:1, :] = dc0
        dc_ref[1:2, :] = dc1
        dc_ref[2:SUBLANES, :] = jnp.zeros((SUBLANES - 2, blk), F32)

    full = lambda: pl.BlockSpec((t_dim, LANES), lambda p, j: (0, p))
    blkspec = pl.BlockSpec((blk, LANES), lambda p, j: (j, p))
    return pl.pallas_call(
        body, name="fox_attn_bwd", grid=(n_pairs, nb),
        in_specs=[pl.BlockSpec((t_dim, LANES), lambda p, j: (0, q_cb + p)),
                  pl.BlockSpec((blk, LANES), lambda p, j: (j, k_cb + p)),
                  pl.BlockSpec((blk, LANES), lambda p, j: (j, v_cb + p)),
                  pl.BlockSpec((None, SUBLANES, t_dim), lambda p, j: (p, 0, 0)),
                  pl.BlockSpec((None, t_dim, 2), lambda p, j: (p, 0, 0)), full(), full(),
                  pl.BlockSpec((None, t_dim, 2), lambda p, j: (p, 0, 0))],
        out_specs=[full(), blkspec, blkspec, pl.BlockSpec((None, SUBLANES, blk), lambda p, j: (p, 0, j))],
        out_shape=[jax.ShapeDtypeStruct((t_dim, n_pairs * LANES), F32)] * 3
                  + [jax.ShapeDtypeStruct((n_pairs, SUBLANES, t_dim), F32)],
        compiler_params=_params("parallel", "arbitrary"),
    )(z, z, z, ct, cq, rowdot, do, lse)


HBM_SPEC = pl.BlockSpec(memory_space=pltpu.HBM)


def _all_gather(shards, name):
    n = len(shards)

    def body(*refs):
        x_refs, out_refs = refs[:n], refs[n:2 * n]
        send_sems, recv_sems, local_sems = refs[2 * n:]
        x, y, c = lax.axis_index("x"), lax.axis_index("y"), lax.axis_index("c")
        me, sibling = (x, y, c), (x, y, 1 - c)
        chips = [(1 - x, y), (x, 1 - y), (1 - x, 1 - y)]

        def copy(a, k, block, to, from_input=False):
            px, py, pc = block
            slot = out_refs[a].at[4 * px + 2 * py + pc]
            return pltpu.make_async_remote_copy(
                src_ref=x_refs[a] if from_input else slot, dst_ref=slot,
                send_sem=send_sems.at[7 * a + k], recv_sem=recv_sems.at[7 * a + k], device_id=to, device_id_type=MESH)

        mine = [pltpu.make_async_copy(x_refs[a], out_refs[a].at[4 * x + 2 * y + c], local_sems.at[a]) for a in range(n)]
        for cp in mine:
            cp.start()
        first = []
        for a in range(n):
            first += [copy(a, 1 + j, me, (*chip, c), from_input=True) for j, chip in enumerate(chips)]
            first.append(copy(a, 0, me, sibling, from_input=True))
        for cp in first:
            cp.start()
        passed = []
        for a in range(n):
            for j, chip in enumerate(chips):
                copy(a, 1 + j, (*chip, c), me).wait_recv()
                passed.append(copy(a, 4 + j, (*chip, c), sibling))
                passed[-1].start()
        for a in range(n):
            copy(a, 0, sibling, me).wait_recv()
            for j, chip in enumerate(chips):
                copy(a, 4 + j, (*chip, 1 - c), me).wait_recv()
        for cp in first + passed:
            cp.wait_send()
        for cp in mine:
            cp.wait()

    return pl.pallas_call(
        body, name=name, out_shape=[jax.ShapeDtypeStruct((NDEV,) + s.shape, s.dtype) for s in shards],
        in_specs=[HBM_SPEC] * n, out_specs=[HBM_SPEC] * n,
        scratch_shapes=[pltpu.SemaphoreType.DMA((7 * n,)), pltpu.SemaphoreType.DMA((7 * n,)),
                        pltpu.SemaphoreType.DMA((n,))],
    )(*shards)


def _grad_exchange(parts, small, name):
    n = len(parts)

    def body(*refs):
        g_refs, s_ref = refs[:n], refs[n]
        recv_refs, sall_ref = refs[n + 1:2 * n + 1], refs[2 * n + 1]
        send_sems, recv_sems, local_sems = refs[2 * n + 2:]
        x, y, c = lax.axis_index("x"), lax.axis_index("y"), lax.axis_index("c")
        me = 4 * x + 2 * y + c
        own = [pltpu.make_async_copy(g_refs[a].at[me], recv_refs[a].at[me], local_sems.at[a]) for a in range(n)]
        own.append(pltpu.make_async_copy(s_ref, sall_ref.at[me], local_sems.at[n]))
        for cp in own:
            cp.start()
        sends, recvs = [], []
        for a in range(n + 1):
            for k in range(1, NDEV):
                px = 1 - x if k & 4 else x
                py = 1 - y if k & 2 else y
                pc = 1 - c if k & 1 else c
                peer = 4 * px + 2 * py + pc
                sems = dict(send_sem=send_sems.at[7 * a + k - 1], recv_sem=recv_sems.at[7 * a + k - 1],
                            device_id=(px, py, pc), device_id_type=MESH)
                if a < n:
                    src, to_me, from_peer = g_refs[a].at[peer], recv_refs[a].at[me], recv_refs[a].at[peer]
                else:
                    src, to_me, from_peer = s_ref, sall_ref.at[me], sall_ref.at[peer]
                sends.append(pltpu.make_async_remote_copy(src_ref=src, dst_ref=to_me, **sems))
                recvs.append(pltpu.make_async_remote_copy(src_ref=src, dst_ref=from_peer, **sems))
        for cp in sends:
            cp.start()
        for cp in recvs:
            cp.wait_recv()
        for cp in sends:
            cp.wait_send()
        for cp in own:
            cp.wait()

    return pl.pallas_call(
        body, name=name,
        out_shape=[jax.ShapeDtypeStruct(a.shape, a.dtype) for a in parts]
                  + [jax.ShapeDtypeStruct((NDEV,) + small.shape, small.dtype)],
        in_specs=[HBM_SPEC] * (n + 1), out_specs=[HBM_SPEC] * (n + 1),
        scratch_shapes=[pltpu.SemaphoreType.DMA((7 * (n + 1),)), pltpu.SemaphoreType.DMA((7 * (n + 1),)),
                        pltpu.SemaphoreType.DMA((n + 1,))],
    )(*parts, small)


def _adamw(partials, w, m, v, name):
    rows, width = w.shape
    br = _pick(rows, 128, 2 * SUBLANES)

    def body(p_ref, w_ref, m_ref, v_ref, g_out, d_out, m_out, v_out):
        g = p_ref[0].astype(F32)
        for d in range(1, NDEV):
            g = g + p_ref[d].astype(F32)
        mn = ADAM_B1 * m_ref[...] + (1.0 - ADAM_B1) * g
        vn = ADAM_B2 * v_ref[...] + (1.0 - ADAM_B2) * jnp.square(g)
        m_hat = mn / (1.0 - ADAM_B1 ** ADAM_STEP)
        v_hat = vn / (1.0 - ADAM_B2 ** ADAM_STEP)
        g_out[...] = g
        d_out[...] = -ADAM_LR * (m_hat / (jnp.sqrt(v_hat) + ADAM_EPS) + ADAM_WD * w_ref[...])
        m_out[...] = mn
        v_out[...] = vn

    blk = pl.BlockSpec((br, width), lambda i: (i, 0))
    return pl.pallas_call(
        body, name=name, grid=(rows // br,),
        in_specs=[pl.BlockSpec((NDEV, br, width), lambda i: (0, i, 0)), blk, blk, blk],
        out_specs=[pl.BlockSpec((None, br, width), lambda i: (0, i, 0))] * 4,
        out_shape=[jax.ShapeDtypeStruct((1, rows, width), F32)] * 4,
        compiler_params=_params("parallel"),
    )(partials, w, m, v)


def _pack_rows(flat):
    n = flat.shape[0]
    padded = _round_up(n, PACK_ALIGN)
    return jnp.pad(flat, (0, padded - n)).reshape(padded // PACK_W, PACK_W)


def _split_shards(full, axis):
    rows, cols = full.shape
    if axis == 0:
        return full.reshape(NDEV, rows // NDEV, cols)
    width = cols // NDEV
    return jnp.stack([full[:, d * width:(d + 1) * width] for d in range(NDEV)])


def _join_shards(blocks, axis):
    if axis == 0:
        return blocks.reshape(-1, blocks.shape[2])
    return jnp.concatenate([blocks[d] for d in range(NDEV)], axis=1)


SHARDED = (("w_in", 1), ("rw_w_lora_up", 1), ("rw_a_lora_up", 1), ("w_up_rwkv", 1), ("w_up_fox", 1),
           ("w_out", 0), ("ple_proj", 1), ("ple_gate_w", 0))
REPLICATED = ("norm_g", "rw_shift_mu", "rw_w0", "rw_a0", "rw_k_k", "rw_k_a", "rw_r_k", "rw_ln_g", "rw_ln_b",
              "fox_b_f", "ple_norm_g", "final_norm_g")
WEIGHTS = ("norm_g", "w_in", "rw_shift_mu", "rw_w0", "rw_w_lora_up", "rw_a0", "rw_a_lora_up", "rw_k_k", "rw_k_a",
           "rw_r_k", "rw_ln_g", "rw_ln_b", "fox_b_f", "w_up_rwkv", "w_up_fox", "w_out", "ple_proj", "ple_gate_w",
           "ple_norm_g", "final_norm_g")


def _local_step(x, p, tgt, wz, wl, wa, wur, wuf, wo, pp, pg, rep, dims):
    t_dim, d_model, c_rw, lora, c_fox, h_fox, sec = dims
    bt = _pick(t_dim, 256, 2 * SUBLANES)
    bt_many = _pick(t_dim, 128, 2 * SUBLANES)
    n_pairs = c_fox // LANES
    row = lambda a: a.reshape(1, -1)
    norm_g, mu, w0, a0 = row(rep["norm_g"]), row(rep["rw_shift_mu"]), row(rep["rw_w0"]), row(rep["rw_a0"])
    k_k, k_a, r_k = row(rep["rw_k_k"]), row(rep["rw_k_a"]), row(rep["rw_r_k"])
    ln_g, ln_b = row(rep["rw_ln_g"]), row(rep["rw_ln_b"])
    g2, g3 = row(rep["ple_norm_g"]), row(rep["final_norm_g"])
    b_f = jnp.pad(row(rep["fox_b_f"]), ((0, 0), (0, LANES - h_fox)))
    e_head = _head_ones(F32)
    c4 = 4 * c_rw
    inv_d = 1.0 / d_model
    decay_k = math.exp(-0.5)

    def norm_in(x_ref, g_ref, h_ref):
        xv = x_ref[...]
        rms = lax.rsqrt(jnp.mean(xv * xv, axis=-1, keepdims=True) + NORM_EPS)
        h_ref[...] = (xv * rms * g_ref[...]).astype(BF16)

    (h,) = _rowcall(norm_in, "norm_in", t_dim, bt, [x], [norm_g], [(d_model, BF16)])
    z = _matmul(h, wz, "nn", "proj_in", bn=1408)

    def rw_values(zs, w0_ref, wl_ref, a0_ref, wa_ref, kk_ref, ka_ref, e_ref):
        k = zs[:, c_rw:2 * c_rw]
        tw = jnp.tanh(zs[:, c4:c4 + lora])
        al = zs[:, c4 + lora:c4 + 2 * lora]
        sw = _sigmoid(w0_ref[...] + jnp.dot(tw.astype(BF16), wl_ref[...], preferred_element_type=F32))
        decay = jnp.exp(-decay_k * sw)
        a = _sigmoid(a0_ref[...] + jnp.dot(al.astype(BF16), wa_ref[...], preferred_element_type=F32))
        kk0 = k * kk_ref[...]
        nrm = jnp.sqrt(_headsum(kk0 * kk0, e_ref[...]))
        inv = 1.0 / jnp.maximum(nrm, 1e-12)
        k2 = k * (1.0 + (a - 1.0) * ka_ref[...])
        return k, tw, al, sw, decay, a, kk0, nrm, inv, k2

    def rw_prep(z_ref, mu_ref, w0_ref, wl_ref, a0_ref, wa_ref, kk_ref, ka_ref, e_ref,
                r_o, w_o, k_o, v_o, kk_o, a_o, g_o, carry):
        _first_step_zero(carry)
        zv = z_ref[...]
        zs = zv + (_shift_down(zv, carry) - zv) * mu_ref[...]
        k, tw, al, sw, decay, a, kk0, nrm, inv, k2 = rw_values(zs, w0_ref, wl_ref, a0_ref, wa_ref, kk_ref, ka_ref, e_ref)
        r_o[...] = zs[:, 0:c_rw]
        w_o[...] = decay
        k_o[...] = k2
        v_o[...] = zs[:, 2 * c_rw:3 * c_rw]
        kk_o[...] = kk0 * inv
        a_o[...] = a
        g_o[...] = zs[:, 3 * c_rw:c4]

    rw_consts = [mu, w0, wl, a0, wa, k_k, k_a, e_head]
    r_s, w_s, k_s, v_s, kk_s, a_s, g_s = _rowcall(
        rw_prep, "rwkv_prep", t_dim, bt, [(z, 0, sec)], rw_consts, [(c_rw, F32)] * 7,
        scratch=[pltpu.VMEM((1, sec), F32)])
    pairs_fwd = max(n for n in (1, 2, 4) if c_rw % (n * LANES) == 0)
    pairs_bwd = pairs_fwd
    y_s, s_all, s_fin = _scan_fwd(r_s, w_s, k_s, v_s, kk_s, a_s, 64, pairs_fwd)

    def rw_post_values(y, r, k2, v, g, lng_ref, lnb_ref, rk_ref, e):
        mean = _headsum(y, e) * (1.0 / HEAD)
        d = y - mean
        rstd = lax.rsqrt(_headsum(d * d, e) * (1.0 / HEAD) + GN_EPS)
        yh = d * rstd
        rk = _headsum(r * k2 * rk_ref[...], e)
        yo = yh * lng_ref[...] + lnb_ref[...] + rk * v
        sg = _sigmoid(g)
        return rstd, yh, rk, yo, sg

    def rw_post(y_ref, r_ref, k_ref, v_ref, g_ref, lng_ref, lnb_ref, rk_ref, e_ref, out_ref):
        g = g_ref[...]
        _, _, _, yo, sg = rw_post_values(y_ref[...], r_ref[...], k_ref[...], v_ref[...], g, lng_ref, lnb_ref, rk_ref, e_ref[...])
        out_ref[...] = (yo * g * sg).astype(BF16)

    (y_rw,) = _rowcall(rw_post, "rwkv_post", t_dim, bt, [y_s, r_s, k_s, v_s, g_s], [ln_g, ln_b, r_k, e_head], [(c_rw, BF16)])

    fl_cb = (sec + 4 * c_fox) // LANES
    hp = _round_up(h_fox, SUBLANES)
    bt_c = _pick(t_dim, 256)
    tri = (jnp.arange(bt_c)[:, None] >= jnp.arange(bt_c)[None, :]).astype(F32)

    rows8 = n_pairs * SUBLANES
    pair_rows = (jnp.arange(rows8)[:, None] // SUBLANES * 2 + jnp.arange(rows8)[:, None] % SUBLANES
                 == jnp.arange(LANES)[None, :]) & (jnp.arange(rows8)[:, None] % SUBLANES < 2)
    pair_rows = pair_rows.astype(F32)

    def fox_decay(fl_ref, bf_ref, tri_ref, sel_ref, ct_ref, cq_ref, carry):
        _first_step_zero(carry)
        lf = _log_sigmoid(fl_ref[...] + bf_ref[...])
        c = jnp.dot(tri_ref[...], lf, precision=HI, preferred_element_type=F32) + carry[...]
        carry[...] = c[bt_c - 1:bt_c, :]
        ct = jnp.dot(sel_ref[...], jnp.transpose(c), precision=HI, preferred_element_type=F32)
        ct_ref[...] = ct.reshape(n_pairs, SUBLANES, bt_c)
        for pair in range(n_pairs):
            cq_ref[pair] = c[:, 2 * pair:2 * pair + 2]

    ct, cq = pl.pallas_call(
        fox_decay, name="fox_decay", grid=(t_dim // bt_c,),
        in_specs=[pl.BlockSpec((bt_c, LANES), lambda i: (i, fl_cb)), pl.BlockSpec((1, LANES), lambda i: (0, 0)),
                  pl.BlockSpec((bt_c, bt_c), lambda i: (0, 0)), pl.BlockSpec((rows8, LANES), lambda i: (0, 0))],
        out_specs=[pl.BlockSpec((n_pairs, SUBLANES, bt_c), lambda i: (0, 0, i)),
                   pl.BlockSpec((n_pairs, bt_c, 2), lambda i: (0, i, 0))],
        out_shape=[jax.ShapeDtypeStruct((n_pairs, SUBLANES, t_dim), F32), jax.ShapeDtypeStruct((n_pairs, t_dim, 2), F32)],
        scratch_shapes=[pltpu.VMEM((1, LANES), F32)], compiler_params=_params("arbitrary"),
    )(z, b_f, tri, pair_rows)
    q_cb = sec // LANES
    k_cb, v_cb = q_cb + n_pairs, q_cb + 2 * n_pairs
    o_fox, lse = _fox_fwd(z, ct, cq, q_cb, k_cb, v_cb, n_pairs, 256)

    def fox_post(o_ref, z_ref, out_ref):
        g = z_ref[:, 3 * c_fox:4 * c_fox]
        out_ref[...] = (o_ref[...] * g * _sigmoid(g)).astype(BF16)

    (y_fox,) = _rowcall(fox_post, "fox_post", t_dim, bt, [o_fox, (z, 1, sec)], [], [(c_fox, BF16)])

    u_rw = _matmul(y_rw, wur, "nn", "up_rwkv")
    u_fox = _matmul(y_fox, wuf, "nn", "up_fox")

    def merge(ur_ref, uf_ref, z_ref, out_ref):
        s1 = _sigmoid(z_ref[:, 0:d_model])
        s2 = _sigmoid(z_ref[:, d_model:2 * d_model])
        out_ref[...] = (s1 * ur_ref[...] + s2 * uf_ref[...]).astype(BF16)

    (merged,) = _rowcall(merge, "merge", t_dim, bt, [u_rw, u_fox, (z, 2, sec)], [], [(d_model, BF16)])
    mo = _matmul(merged, wo, "nn", "proj_out")

    def resid_norm(x_ref, mo_ref, g_ref, x1_ref, n2_ref):
        x1 = x_ref[...] + mo_ref[...]
        rms = lax.rsqrt(jnp.mean(x1 * x1, axis=-1, keepdims=True) + NORM_EPS)
        x1_ref[...] = x1
        n2_ref[...] = (x1 * rms * g_ref[...]).astype(BF16)

    x1, n2 = _rowcall(resid_norm, "resid_norm", t_dim, bt, [x, mo], [g2], [(d_model, F32), (d_model, BF16)])
    ple = _matmul(p, pp, "nn", "ple_proj")
    gl = _matmul(n2, pg, "nn", "ple_gate")

    def head(x1_ref, ple_ref, gl_ref, tgt_ref, g_ref, dx2_ref, dple_ref, dgl_ref, loss_ref, dg3_ref):
        _first_step_zero(loss_ref, dg3_ref)
        sg = _sigmoid(gl_ref[...])
        pl_v = ple_ref[...]
        x2 = x1_ref[...] + pl_v * sg
        rms = lax.rsqrt(jnp.mean(x2 * x2, axis=-1, keepdims=True) + NORM_EPS)
        xn = x2 * rms
        diff = xn * g_ref[...] - tgt_ref[...]
        loss_ref[...] += 0.5 * jnp.sum(jnp.mean(diff * diff, axis=-1, keepdims=True), axis=0, keepdims=True)
        dyf = diff * inv_d
        dg3_ref[...] += _colsum(dyf * xn)
        gy = dyf * g_ref[...]
        dx2 = rms * (gy - xn * jnp.mean(xn * gy, axis=-1, keepdims=True))
        dx2_ref[...] = dx2
        dple_ref[...] = (dx2 * sg).astype(BF16)
        dgl_ref[...] = (dx2 * pl_v * sg * (1.0 - sg)).astype(BF16)

    dx2, dple, dgl, loss, d_g3 = _rowcall(
        head, "head", t_dim, bt, [x1, ple, gl, tgt], [g3], [(d_model, F32), (d_model, BF16), (d_model, BF16)],
        acc_outs=[(1, 1), (1, d_model)])

    d_pp = _matmul(p, dple, "tn", "d_ple_proj", out_dtype=BF16)
    d_pg = _matmul(n2, dgl, "tn", "d_ple_gate", out_dtype=BF16)
    dn2 = _matmul(dgl, pg, "nt", "d_n2")

    def resid_norm_bwd(dx2_ref, dn2_ref, x1_ref, g_ref, dx1_ref, dx1b_ref, dg2_ref):
        _first_step_zero(dg2_ref)
        x1 = x1_ref[...]
        rms = lax.rsqrt(jnp.mean(x1 * x1, axis=-1, keepdims=True) + NORM_EPS)
        xn = x1 * rms
        dn = dn2_ref[...]
        dg2_ref[...] += _colsum(dn * xn)
        gy = dn * g_ref[...]
        dx1 = dx2_ref[...] + rms * (gy - xn * jnp.mean(xn * gy, axis=-1, keepdims=True))
        dx1_ref[...] = dx1
        dx1b_ref[...] = dx1.astype(BF16)

    dx1, dx1b, d_g2 = _rowcall(resid_norm_bwd, "resid_norm_bwd", t_dim, bt, [dx2, dn2, x1], [g2],
                               [(d_model, F32), (d_model, BF16)], acc_outs=[(1, d_model)])
    d_wo = _matmul(merged, dx1b, "tn", "d_w_out", out_dtype=BF16)
    dmerged = _matmul(dx1b, wo, "nt", "d_merged")

    def merge_bwd(dm_ref, ur_ref, uf_ref, z_ref, dur_ref, duf_ref, dzg_ref):
        dm = dm_ref[...]
        s1 = _sigmoid(z_ref[:, 0:d_model])
        s2 = _sigmoid(z_ref[:, d_model:2 * d_model])
        dur_ref[...] = (dm * s1).astype(BF16)
        duf_ref[...] = (dm * s2).astype(BF16)
        dzg_ref[:, 0:d_model] = (dm * ur_ref[...] * s1 * (1.0 - s1)).astype(BF16)
        dzg_ref[:, d_model:2 * d_model] = (dm * uf_ref[...] * s2 * (1.0 - s2)).astype(BF16)
        if sec > 2 * d_model:
            dzg_ref[:, 2 * d_model:sec] = jnp.zeros((dm.shape[0], sec - 2 * d_model), BF16)

    du_rw, du_fox, dz_gate = _rowcall(merge_bwd, "merge_bwd", t_dim, bt, [dmerged, u_rw, u_fox, (z, 2, sec)], [],
                                      [(d_model, BF16), (d_model, BF16), (sec, BF16)])
    d_wur = _matmul(y_rw, du_rw, "tn", "d_w_up_rwkv", out_dtype=BF16)
    d_wuf = _matmul(y_fox, du_fox, "tn", "d_w_up_fox", out_dtype=BF16)
    dy_rw = _matmul(du_rw, wur, "nt", "d_y_rwkv")
    dy_fox = _matmul(du_fox, wuf, "nt", "d_y_fox")

    def fox_post_bwd(dy_ref, o_ref, z_ref, e_ref, do_ref, dg_ref, rd_ref):
        g = z_ref[:, 3 * c_fox:4 * c_fox]
        sg = _sigmoid(g)
        dy, o = dy_ref[...], o_ref[...]
        do = dy * g * sg
        do_ref[...] = do
        dg_ref[...] = (dy * o * sg * (1.0 + g * (1.0 - sg))).astype(BF16)
        rd_ref[...] = _headsum(do.astype(BF16).astype(F32) * o, e_ref[...])

    do_fox, dg_fox, rowdot = _rowcall(fox_post_bwd, "fox_post_bwd", t_dim, bt, [dy_fox, o_fox, (z, 1, sec)], [e_head],
                                      [(c_fox, F32), (c_fox, BF16), (c_fox, F32)])
    dq_f, dk_f, dv_f, dc_t = _fox_bwd(z, ct, cq, rowdot, do_fox, lse, q_cb, k_cb, v_cb, n_pairs, 256)
    sel = (jnp.arange(hp)[:, None] // 2 * SUBLANES + jnp.arange(hp)[:, None] % 2 == jnp.arange(rows8)[None, :]).astype(F32)
    tri_rev = (jnp.arange(bt_c)[:, None] >= jnp.arange(bt_c)[None, :]).astype(F32)
    bf_col = b_f.reshape(LANES, 1)[0:hp]
    nbc = t_dim // bt_c

    def fox_decay_bwd(dc_ref, fl_ref, sel_ref, tri_ref, bf_ref, dfl_ref, dbf_ref, carry):
        _first_step_zero(carry, dbf_ref)
        dc = jnp.dot(sel_ref[...], dc_ref[...].reshape(rows8, bt_c), precision=HI, preferred_element_type=F32)
        dlf = jnp.dot(dc, tri_ref[...], precision=HI, preferred_element_type=F32) + carry[...]
        carry[...] = dlf[:, 0:1]
        flt = jnp.transpose(fl_ref[...])[0:hp, :]
        dfl = dlf * _sigmoid(-(flt + bf_ref[...]))
        head_row = lax.broadcasted_iota(jnp.int32, (hp, bt_c), 0)
        dfl = jnp.where(head_row < h_fox, dfl, 0.0)
        dbf_ref[...] += jnp.sum(dfl, axis=1, keepdims=True)
        full = jnp.concatenate([dfl, jnp.zeros((LANES - hp, bt_c), F32)], axis=0) if hp < LANES else dfl
        dfl_ref[...] = jnp.transpose(full).astype(BF16)

    dz_fl, d_bf = pl.pallas_call(
        fox_decay_bwd, name="fox_decay_bwd", grid=(nbc,),
        in_specs=[pl.BlockSpec((n_pairs, SUBLANES, bt_c), lambda i: (0, 0, nbc - 1 - i)),
                  pl.BlockSpec((bt_c, LANES), lambda i: (nbc - 1 - i, fl_cb)),
                  pl.BlockSpec(sel.shape, lambda i: (0, 0)), pl.BlockSpec((bt_c, bt_c), lambda i: (0, 0)),
                  pl.BlockSpec((hp, 1), lambda i: (0, 0))],
        out_specs=[pl.BlockSpec((bt_c, LANES), lambda i: (nbc - 1 - i, 0)), pl.BlockSpec((hp, 1), lambda i: (0, 0))],
        out_shape=[jax.ShapeDtypeStruct((t_dim, LANES), BF16), jax.ShapeDtypeStruct((hp, 1), F32)],
        scratch_shapes=[pltpu.VMEM((hp, 1), F32)], compiler_params=_params("arbitrary"),
    )(dc_t, z, sel, tri_rev, bf_col)

    def rw_post_bwd(dy_ref, y_ref, r_ref, k_ref, v_ref, g_ref, lng_ref, lnb_ref, rk_ref, e_ref,
                    dg_ref, dys_ref, dr_ref, dk_ref, dv_ref, dlng_ref, dlnb_ref, drk_ref):
        _first_step_zero(dlng_ref, dlnb_ref, drk_ref)
        e = e_ref[...]
        dy, r, k2, v, g = dy_ref[...], r_ref[...], k_ref[...], v_ref[...], g_ref[...]
        rstd, yh, rk, yo, sg = rw_post_values(y_ref[...], r, k2, v, g, lng_ref, lnb_ref, rk_ref, e)
        dg_ref[...] = dy * yo * sg * (1.0 + g * (1.0 - sg))
        dyo = dy * g * sg
        dlnb_ref[...] += _colsum(dyo)
        dlng_ref[...] += _colsum(dyo * yh)
        dyh = dyo * lng_ref[...]
        dys_ref[...] = rstd * (dyh - _headsum(dyh, e) * (1.0 / HEAD) - yh * _headsum(dyh * yh, e) * (1.0 / HEAD))
        drk = _headsum(dyo * v, e)
        dv_ref[...] = dyo * rk
        dr_ref[...] = drk * k2 * rk_ref[...]
        dk_ref[...] = drk * r * rk_ref[...]
        drk_ref[...] += _colsum(drk * r * k2)

    dg_rw, dy_s, dr_b, dk_b, dv_b, d_lng, d_lnb, d_rk = _rowcall(
        rw_post_bwd, "rwkv_post_bwd", t_dim, bt, [dy_rw, y_s, r_s, k_s, v_s, g_s], [ln_g, ln_b, r_k, e_head],
        [(c_rw, F32)] * 5, acc_outs=[(1, c_rw)] * 3)
    dr_c, dw_c, dk_c, dv_c, dkk_c, da_c = _scan_bwd(r_s, w_s, k_s, v_s, kk_s, a_s, dy_s, s_all, s_fin, 64, pairs_bwd)

    def rw_prep_bwd(z_ref, dr1, dr2, dw_ref, dk1, dk2_ref, dv1, dv2, dkk_ref, da_ref, dg_ref,
                    mu_ref, w0_ref, wl_ref, a0_ref, wa_ref, kk_ref, ka_ref, e_ref,
                    dzs_ref, tw_ref, al_ref, dwr_ref, dar_ref, dmu_ref, dw0_ref, da0_ref, dkk_acc, dka_acc, carry):
        _first_step_zero(carry, dmu_ref, dw0_ref, da0_ref, dkk_acc, dka_acc)
        e = e_ref[...]
        zv = z_ref[...]
        zp = _shift_down(zv, carry)
        zs = zv + (zp - zv) * mu_ref[...]
        k, tw, al, sw, decay, a, kk0, nrm, inv, k2 = rw_values(zs, w0_ref, wl_ref, a0_ref, wa_ref, kk_ref, ka_ref, e_ref)
        dk2 = dk1[...] + dk2_ref[...]
        da = da_ref[...] + dk2 * k * ka_ref[...]
        dk = dk2 * (1.0 + (a - 1.0) * ka_ref[...])
        dka_acc[...] += _colsum(dk2 * k * (a - 1.0))
        kk = kk0 * inv
        dkk = dkk_ref[...]
        dkk0 = inv * jnp.where(nrm > 1e-12, dkk - kk * _headsum(dkk * kk, e), dkk)
        dk = dk + dkk0 * kk_ref[...]
        dkk_acc[...] += _colsum(dkk0 * k)
        da_raw = da * a * (1.0 - a)
        da0_ref[...] += _colsum(da_raw)
        dw_raw = dw_ref[...] * decay * (-decay_k) * sw * (1.0 - sw)
        dw0_ref[...] += _colsum(dw_raw)
        dar_b, dwr_b = da_raw.astype(BF16), dw_raw.astype(BF16)
        dal = lax.dot_general(dar_b, wa_ref[...], _DOT_DIMS["nt"], preferred_element_type=F32)
        dtw = lax.dot_general(dwr_b, wl_ref[...], _DOT_DIMS["nt"], preferred_element_type=F32)
        dzs_ref[:, 0:c_rw] = dr1[...] + dr2[...]
        dzs_ref[:, c_rw:2 * c_rw] = dk
        dzs_ref[:, 2 * c_rw:3 * c_rw] = dv1[...] + dv2[...]
        dzs_ref[:, 3 * c_rw:c4] = dg_ref[...]
        dzs_ref[:, c4:c4 + lora] = dtw * (1.0 - tw * tw)
        dzs_ref[:, c4 + lora:c4 + 2 * lora] = dal
        if sec > c4 + 2 * lora:
            dzs_ref[:, c4 + 2 * lora:sec] = jnp.zeros((zv.shape[0], sec - c4 - 2 * lora), F32)
        tw_ref[...] = tw.astype(BF16)
        al_ref[...] = al.astype(BF16)
        dwr_ref[...] = dwr_b
        dar_ref[...] = dar_b
        dmu_ref[...] += _colsum(dzs_ref[...] * (zp - zv))

    dzs, tw_b, al_b, dwr_b, dar_b, d_mu, d_w0, d_a0, d_kk, d_ka = _rowcall(
        rw_prep_bwd, "rwkv_prep_bwd", t_dim, bt_many,
        [(z, 0, sec), dr_c, dr_b, dw_c, dk_c, dk_b, dv_c, dv_b, dkk_c, da_c, dg_rw], rw_consts,
        [(sec, F32), (lora, BF16), (lora, BF16), (c_rw, BF16), (c_rw, BF16)],
        acc_outs=[(1, sec), (1, c_rw), (1, c_rw), (1, c_rw), (1, c_rw)], scratch=[pltpu.VMEM((1, sec), F32)])
    d_wl = _matmul(tw_b, dwr_b, "tn", "d_w_lora", out_dtype=BF16)
    d_wa = _matmul(al_b, dar_b, "tn", "d_a_lora", out_dtype=BF16)

    def shift_bwd(dzs_ref, mu_ref, dz_ref, carry):
        _first_step_zero(carry)
        d = dzs_ref[...]
        nbt = d.shape[0]
        nxt = pltpu.roll(d, nbt - 1, 0)
        rowi = lax.broadcasted_iota(jnp.int32, d.shape, 0)
        nxt = jnp.where(rowi == nbt - 1, carry[...], nxt)
        carry[...] = d[0:1, :]
        m = mu_ref[...]
        dz_ref[...] = (d * (1.0 - m) + nxt * m).astype(BF16)

    (dz_rw,) = _rowcall(shift_bwd, "shift_bwd", t_dim, bt, [dzs], [mu], [(sec, BF16)],
                        scratch=[pltpu.VMEM((1, sec), F32)], reverse=True)

    fox_parts = [dq_f.astype(BF16), dk_f.astype(BF16), dv_f.astype(BF16), dg_fox, dz_fl]
    if sec > 4 * c_fox + LANES:
        fox_parts.append(jnp.zeros((t_dim, sec - 4 * c_fox - LANES), BF16))
    dz = jnp.concatenate([dz_rw] + fox_parts + [dz_gate], axis=1)
    d_wz = _matmul(h, dz, "tn", "d_w_in", out_dtype=BF16, bn=1408)
    dh = _matmul(dz, wz, "nt", "d_h", bk=sec)

    def norm_in_bwd(dh_ref, x_ref, dx1_ref, g_ref, dx_ref, dg1_ref):
        _first_step_zero(dg1_ref)
        xv = x_ref[...]
        rms = lax.rsqrt(jnp.mean(xv * xv, axis=-1, keepdims=True) + NORM_EPS)
        xn = xv * rms
        d = dh_ref[...]
        dg1_ref[...] += _colsum(d * xn)
        gy = d * g_ref[...]
        dx_ref[...] = dx1_ref[...] + rms * (gy - xn * jnp.mean(xn * gy, axis=-1, keepdims=True))

    dx, d_g1 = _rowcall(norm_in_bwd, "norm_in_bwd", t_dim, bt, [dh, x, dx1], [norm_g], [(d_model, F32)],
                        acc_outs=[(1, d_model)])

    full_grads = {"w_in": d_wz, "rw_w_lora_up": d_wl, "rw_a_lora_up": d_wa, "w_up_rwkv": d_wur, "w_up_fox": d_wuf,
                  "w_out": d_wo, "ple_proj": d_pp, "ple_gate_w": d_pg}
    rep_grads = {"norm_g": d_g1, "rw_shift_mu": d_mu[:, 0:c4 + 2 * lora], "rw_w0": d_w0, "rw_a0": d_a0, "rw_k_k": d_kk,
                 "rw_k_a": d_ka, "rw_r_k": d_rk, "rw_ln_g": d_lng, "rw_ln_b": d_lnb, "fox_b_f": d_bf[0:h_fox, 0],
                 "ple_norm_g": d_g2, "final_norm_g": d_g3}
    return loss[0, 0], dx, full_grads, rep_grads


def kernel(x, p, norm_g, w_in, rw_shift_mu, rw_w0, rw_w_lora_up, rw_a0, rw_a_lora_up, rw_k_k, rw_k_a, rw_r_k, rw_ln_g, rw_ln_b, fox_b_f, w_up_rwkv, w_up_fox, w_out, ple_proj, ple_gate_w, ple_norm_g, final_norm_g, loss_target, m_norm_g, m_w_in, m_rw_shift_mu, m_rw_w0, m_rw_w_lora_up, m_rw_a0, m_rw_a_lora_up, m_rw_k_k, m_rw_k_a, m_rw_r_k, m_rw_ln_g, m_rw_ln_b, m_fox_b_f, m_w_up_rwkv, m_w_up_fox, m_w_out, m_ple_proj, m_ple_gate_w, m_ple_norm_g, m_final_norm_g, v_norm_g, v_w_in, v_rw_shift_mu, v_rw_w0, v_rw_w_lora_up, v_rw_a0, v_rw_a_lora_up, v_rw_k_k, v_rw_k_a, v_rw_r_k, v_rw_ln_g, v_rw_ln_b, v_fox_b_f, v_w_up_rwkv, v_w_up_fox, v_w_out, v_ple_proj, v_ple_gate_w, v_ple_norm_g, v_final_norm_g):
    args = locals()
    w = {n: args[n] for n in WEIGHTS}
    mom = {n: args["m_" + n] for n in WEIGHTS}
    var = {n: args["v_" + n] for n in WEIGHTS}

    t_dim, d_model = x.shape[1], x.shape[2]
    c_rw, lora = rw_w0.shape[1], rw_w_lora_up.shape[1]
    h_fox = fox_b_f.shape[1]
    c_fox = h_fox * HEAD
    rw_cols, fox_cols, gate_cols = 4 * c_rw + 2 * lora, 4 * c_fox + h_fox, 2 * d_model
    sec = max(rw_cols, 4 * c_fox + LANES, _round_up(gate_cols, LANES))
    assert c_rw % LANES == 0 and c_fox % LANES == 0 and rw_cols % LANES == 0 and h_fox <= LANES and d_model % LANES == 0
    assert rw_a_lora_up.shape[1] == lora and w_in.shape[2] * NDEV == rw_cols + fox_cols + gate_cols

    gathered = _all_gather([w[n][0].astype(BF16) for n, _ in SHARDED], "gather_weights")
    full = {n: _join_shards(g, ax) for (n, ax), g in zip(SHARDED, gathered)}

    def to_sections(wi):
        pad = lambda a, width: jnp.pad(a, ((0, 0), (0, width - a.shape[1])))
        return jnp.concatenate([pad(wi[:, :rw_cols], sec), pad(wi[:, rw_cols:rw_cols + fox_cols], sec),
                                pad(wi[:, rw_cols + fox_cols:], sec)], axis=1)

    def from_sections(g):
        return jnp.concatenate([g[:, :rw_cols], g[:, sec:sec + fox_cols], g[:, 2 * sec:2 * sec + gate_cols]], axis=1)

    rep = {n: w[n] for n in REPLICATED}
    dims = (t_dim, d_model, c_rw, lora, c_fox, h_fox, sec)
    loss_local, grad_x, full_grads, rep_grads = _local_step(
        x[0], p[0, 0], loss_target[0], to_sections(full["w_in"]), full["rw_w_lora_up"], full["rw_a_lora_up"],
        full["w_up_rwkv"], full["w_up_fox"], full["w_out"], full["ple_proj"], full["ple_gate_w"], rep, dims)
    full_grads["w_in"] = from_sections(full_grads["w_in"])

    rep_sizes = [w[n].size for n in REPLICATED]
    rep_offs = [sum(rep_sizes[:i]) for i in range(len(rep_sizes))]
    pack_rep = lambda tree: _pack_rows(jnp.concatenate([tree[n].astype(F32).reshape(-1) for n in REPLICATED]))
    *recv, small_all = _grad_exchange([_split_shards(full_grads[n], ax) for n, ax in SHARDED], pack_rep(rep_grads),
                                      "exchange_grads")
    kinds = ("grad", "delta", "new_m", "new_v")
    outs = {}
    for (n, _), partials in zip(SHARDED, recv):
        for kind, buf in zip(kinds, _adamw(partials, w[n][0], mom[n][0], var[n][0], "adamw_" + n)):
            outs[kind, n] = buf
    for kind, buf in zip(kinds, _adamw(small_all, pack_rep(w), pack_rep(mom), pack_rep(var), "adamw_replicated")):
        flat = buf.reshape(-1)
        for n, o, s in zip(REPLICATED, rep_offs, rep_sizes):
            outs[kind, n] = flat[o:o + s].reshape(w[n].shape)
    loss = lax.psum(loss_local, MESH_AXES)
    return (loss, grad_x[None], *[outs[kind, n] for kind in ("grad", "delta", "new_m", "new_v") for n in WEIGHTS])
```

```python
import functools
import math

import jax
import jax.numpy as jnp
from jax import lax
from jax.experimental import pallas as pl
from jax.experimental.pallas import tpu as pltpu

F32, BF16 = jnp.float32, jnp.bfloat16
HI = lax.Precision.HIGHEST
LANES = 128
SUBLANES = 8
HEAD = 64
NORM_EPS = 1e-6
GN_EPS = 64e-5
VMEM_LIMIT = 56 * 1024 * 1024
NDEV = 8
PACK_W = 1024
PACK_ALIGN = 16 * PACK_W
MESH_AXES = ("x", "y", "c")
MESH = pl.DeviceIdType.MESH

ADAM_LR, ADAM_B1, ADAM_B2, ADAM_EPS, ADAM_WD, ADAM_STEP = 0.001, 0.9, 0.999, 1e-08, 0.01, 10


def _round_up(n, m):
    return (n + m - 1) // m * m


def _pick(dim, pref, align=LANES):
    if dim <= pref:
        return dim
    best = None
    for cand in range(align, pref + 1, align):
        if dim % cand == 0:
            best = cand
    return dim if best is None else best


def _params(*sem):
    return pltpu.CompilerParams(dimension_semantics=sem, vmem_limit_bytes=VMEM_LIMIT)


def _sigmoid(v):
    return jax.nn.sigmoid(v)


def _log_sigmoid(v):
    return jnp.minimum(v, 0.0) - jnp.log(1.0 + jnp.exp(-jnp.abs(v)))


_DOT_DIMS = {"nn": (((1,), (0,)), ((), ())), "nt": (((1,), (1,)), ((), ())), "tn": (((0,), (0,)), ((), ()))}


def _matmul(a, b, mode, name, out_dtype=F32, bm=512, bn=1024, bk=2048, exchange=None):
    if mode == "tn":
        k_dim, m_dim = a.shape
    else:
        m_dim, k_dim = a.shape
    n_dim = b.shape[0] if mode == "nt" else b.shape[1]
    bm, bn, bk = _pick(m_dim, bm), _pick(n_dim, bn), _pick(k_dim, bk)
    nk = k_dim // bk

    nx = 0 if exchange is None else exchange.n
    grid = (m_dim // bm, n_dim // bn, nk)

    def body(*refs):
        a_ref, b_ref, o_ref = refs[0], refs[1], refs[2 + nx]
        step = [pl.program_id(d) for d in range(3)]
        if nx:
            own_scratch = 1 if nk > 1 else 0
            ex_refs = (refs[2:2 + nx], refs[3 + nx:3 + 2 * nx], refs[3 + 2 * nx + own_scratch:])

            @pl.when((step[0] == 0) & (step[1] == 0) & (step[2] == 0))
            def _():
                exchange.start(*ex_refs)

        prod = lax.dot_general(a_ref[...].astype(BF16), b_ref[...].astype(BF16), _DOT_DIMS[mode],
                               preferred_element_type=F32)
        if nk == 1:
            o_ref[...] = prod.astype(o_ref.dtype)
        else:
            acc_ref, k = refs[3 + 2 * nx], step[2]

            @pl.when(k == 0)
            def _():
                acc_ref[...] = prod

            @pl.when(k > 0)
            def _():
                acc_ref[...] += prod

            @pl.when(k == nk - 1)
            def _():
                o_ref[...] = acc_ref[...].astype(o_ref.dtype)
        if nx:
            @pl.when((step[0] == grid[0] - 1) & (step[1] == grid[1] - 1) & (step[2] == grid[2] - 1))
            def _():
                exchange.finish(*ex_refs)

    if mode == "tn":
        a_spec = pl.BlockSpec((bk, bm), lambda i, j, k: (k, i))
    else:
        a_spec = pl.BlockSpec((bm, bk), lambda i, j, k: (i, k))
    if mode == "nt":
        b_spec = pl.BlockSpec((bn, bk), lambda i, j, k: (j, k))
    else:
        b_spec = pl.BlockSpec((bk, bn), lambda i, j, k: (k, j))
    out_spec = pl.BlockSpec((bm, bn), lambda i, j, k: (i, j))
    out_shape = jax.ShapeDtypeStruct((m_dim, n_dim), out_dtype)
    acc_scratch = [pltpu.VMEM((bm, bn), F32)] if nk > 1 else []
    if exchange is None:
        return pl.pallas_call(
            body, name=name, grid=grid, in_specs=[a_spec, b_spec], out_specs=out_spec, out_shape=out_shape,
            scratch_shapes=acc_scratch, compiler_params=_params("parallel", "parallel", "arbitrary"),
        )(a, b)
    outs = pl.pallas_call(
        body, name=name, grid=grid, in_specs=[a_spec, b_spec] + [HBM_SPEC] * nx,
        out_specs=[out_spec] + [HBM_SPEC] * nx, out_shape=[out_shape] + exchange.out_shape,
        scratch_shapes=acc_scratch + exchange.scratch, compiler_params=_params("arbitrary", "arbitrary", "arbitrary"),
    )(a, b, *exchange.arrays)
    return outs[0], outs[1:]


def _rowcall(body, name, t_dim, bt, row_ins, const_ins, row_outs, acc_outs=(), scratch=(), reverse=False):
    nt = t_dim // bt

    def rmap(i):
        return nt - 1 - i if reverse else i

    in_specs, args = [], []
    for item in row_ins:
        arr, cb, w = item if isinstance(item, tuple) else (item, 0, item.shape[1])
        in_specs.append(pl.BlockSpec((bt, w), lambda i, cb=cb: (rmap(i), cb)))
        args.append(arr)
    for arr in const_ins:
        in_specs.append(pl.BlockSpec(arr.shape, lambda i, nd=arr.ndim: (0,) * nd))
        args.append(arr)
    out_specs = [pl.BlockSpec((bt, w), lambda i: (rmap(i), 0)) for w, _ in row_outs]
    out_shape = [jax.ShapeDtypeStruct((t_dim, w), dt) for w, dt in row_outs]
    for shp in acc_outs:
        out_specs.append(pl.BlockSpec(shp, lambda i, nd=len(shp): (0,) * nd))
        out_shape.append(jax.ShapeDtypeStruct(shp, F32))
    return pl.pallas_call(
        body, name=name, grid=(nt,), in_specs=in_specs, out_specs=out_specs, out_shape=out_shape,
        scratch_shapes=list(scratch), compiler_params=_params("arbitrary"),
    )(*args)


def _first_step_zero(*refs):
    @pl.when(pl.program_id(0) == 0)
    def _():
        for r in refs:
            r[...] = jnp.zeros_like(r)


def _colsum(v):
    return jnp.sum(v, axis=0, keepdims=True)


def _headsum(v, e):
    parts = [jnp.dot(v[:, p * LANES:(p + 1) * LANES], e, precision=HI, preferred_element_type=F32)
             for p in range(v.shape[1] // LANES)]
    return parts[0] if len(parts) == 1 else jnp.concatenate(parts, axis=1)


def _shift_down(v, carry_ref):
    bt = v.shape[0]
    prev = pltpu.roll(v, 1, 0)
    row = lax.broadcasted_iota(jnp.int32, v.shape, 0)
    prev = jnp.where(row == 0, carry_ref[...], prev)
    carry_ref[...] = v[bt - 1:bt, :]
    return prev


def _pair_consts():
    lane = lax.broadcasted_iota(jnp.int32, (1, LANES), 1)
    m0 = (lane < HEAD).astype(F32)
    m1 = 1.0 - m0
    sub = lax.broadcasted_iota(jnp.int32, (HEAD, LANES), 0)
    lane2 = lax.broadcasted_iota(jnp.int32, (HEAD, LANES), 1)
    i0 = (lane2 == sub).astype(F32)
    i1 = (lane2 == sub + HEAD).astype(F32)
    return m0, m1, i0, i1


def _head_masks():
    lane = lax.broadcasted_iota(jnp.int32, (1, LANES), 1)
    first = (lane < HEAD).astype(F32)
    return first, 1.0 - first


def _head_ones(dtype=BF16):
    lane = jnp.arange(LANES)
    return (lane[:, None] // HEAD == lane[None, :] // HEAD).astype(dtype)


def _lanesum(v):
    return jnp.sum(v, axis=1, keepdims=True)


def _split_bf16(v):
    hi = v.astype(BF16).astype(F32)
    rest = v - hi
    mid = rest.astype(BF16).astype(F32)
    return hi, mid, (rest - mid).astype(BF16).astype(F32)


def _col_pair(parts, j, i01, e_bf16):
    lhs = jnp.concatenate([(part[j:j + 1] * i01).astype(BF16) for part in parts], axis=0)
    out = jnp.dot(lhs, e_bf16, preferred_element_type=F32)
    return (out[0:HEAD] + out[HEAD:2 * HEAD]) + out[2 * HEAD:3 * HEAD]


def _col_tiles(out_ref, src_ref, rows, lanes, i01, e_bf16):
    for sl in lanes:
        parts = _split_bf16(src_ref[rows, sl])
        for j in range(SUBLANES):
            out_ref[j, :, sl] = _col_pair(parts, j, i01, e_bf16)


def _row_pair(c0, c1, i0, i1):
    return _colsum(c0 * i0 + c1 * i1)


def _scan_fwd(r, w, k, v, kk, a, tc, npb):
    t_dim, c_dim = r.shape
    wb = LANES * npb
    tc = _pick(t_dim, tc, SUBLANES)

    def body(r_ref, w_ref, k_ref, v_ref, kk_ref, a_ref, e_ref, y_ref, sall_ref, sfin_ref, s_ref):
        @pl.when(pl.program_id(1) == 0)
        def _():
            s_ref[...] = jnp.zeros_like(s_ref)

        m0, m1, i0, i1 = _pair_consts()
        i01 = i0 + i1
        e = e_ref[...]
        sub8 = lax.broadcasted_iota(jnp.int32, (SUBLANES, LANES), 0)
        lanes = [slice(q * LANES, (q + 1) * LANES) for q in range(npb)]
        ng = tc // SUBLANES

        def lanesums(tiles):
            sums = _lanesum(jnp.concatenate(tiles, axis=0))
            return [sums[i * HEAD:(i + 1) * HEAD] for i in range(len(tiles))]

        def halves(s, row):
            return [s * (row * m0), s * (row * m1)]

        def group(gi, carry):
            base = pl.multiple_of(gi * SUBLANES, SUBLANES)
            rows = pl.ds(base, SUBLANES)
            s = list(carry)
            r8 = [r_ref[rows, sl] for sl in lanes]
            w8 = [w_ref[rows, sl] for sl in lanes]
            k8 = [k_ref[rows, sl] for sl in lanes]
            kk8 = [kk_ref[rows, sl] for sl in lanes]
            b8 = [kk8[q] * a_ref[rows, lanes[q]] for q in range(npb)]
            v8 = [_split_bf16(v_ref[rows, sl]) for sl in lanes]
            y8 = [jnp.zeros((SUBLANES, LANES), F32)] * npb
            for j in range(SUBLANES + 1):
                one, before = slice(j, j + 1), slice(j - 1, j)
                tiles = []
                for q in range(npb):
                    if j < SUBLANES:
                        tiles += halves(s[q], kk8[q][one])
                    if j > 0:
                        tiles += halves(s[q], r8[q][before])
                cols = lanesums(tiles)
                per = len(tiles) // npb
                for q in range(npb):
                    mine = cols[q * per:(q + 1) * per]
                    if j > 0:
                        y8[q] = jnp.where(sub8 == j - 1, _row_pair(mine[-2], mine[-1], i0, i1), y8[q])
                    if j < SUBLANES:
                        sall_ref[base + j, :, lanes[q]] = s[q]
                        sb = mine[0] * m0 + mine[1] * m1
                        s[q] = s[q] * w8[q][one] - sb * b8[q][one] + _col_pair(v8[q], j, i01, e) * k8[q][one]
            for q in range(npb):
                y_ref[rows, lanes[q]] = y8[q]
            return tuple(s)

        init = tuple(s_ref[:, q * LANES:(q + 1) * LANES] for q in range(npb))
        fin = lax.fori_loop(0, ng, group, init)
        for q in range(npb):
            s_ref[:, q * LANES:(q + 1) * LANES] = fin[q]
            sfin_ref[:, q * LANES:(q + 1) * LANES] = fin[q]

    row = pl.BlockSpec((tc, wb), lambda p, c: (c, p))
    return pl.pallas_call(
        body, name="rwkv_scan_fwd", grid=(c_dim // wb, t_dim // tc),
        in_specs=[row] * 6 + [pl.BlockSpec((LANES, LANES), lambda p, c: (0, 0))],
        out_specs=[row, pl.BlockSpec((tc, HEAD, wb), lambda p, c: (c, 0, p)), pl.BlockSpec((HEAD, wb), lambda p, c: (0, p))],
        out_shape=[jax.ShapeDtypeStruct((t_dim, c_dim), F32), jax.ShapeDtypeStruct((t_dim, HEAD, c_dim), F32),
                   jax.ShapeDtypeStruct((HEAD, c_dim), F32)],
        scratch_shapes=[pltpu.VMEM((HEAD, wb), F32)],
        compiler_params=_params("parallel", "arbitrary"),
    )(r, w, k, v, kk, a, _head_ones())


def _scan_bwd(r, w, k, v, kk, a, dy, sall, sfin, tc, npb, exchange):
    t_dim, c_dim = r.shape
    wb = LANES * npb
    tc = _pick(t_dim, tc, SUBLANES)
    nc = t_dim // tc
    nx = exchange.n
    n_blocks = c_dim // wb

    def body(*refs):
        r_ref, w_ref, k_ref, v_ref, kk_ref, a_ref, dy_ref, sall_ref, sfin_ref, e_ref = refs[:10]
        dr_ref, dw_ref, dk_ref, dv_ref, dkk_ref, da_ref = refs[10 + nx:16 + nx]
        ds_ref, sn_ref = refs[16 + 2 * nx:18 + 2 * nx]
        here_p, here_c = pl.program_id(0), pl.program_id(1)
        ex_refs = (refs[10:10 + nx], refs[16 + nx:16 + 2 * nx], refs[18 + 2 * nx:])

        @pl.when((here_p == 0) & (here_c == 0))
        def _():
            exchange.start(*ex_refs)

        @pl.when(pl.program_id(1) == 0)
        def _():
            ds_ref[...] = jnp.zeros_like(ds_ref)
            sn_ref[...] = sfin_ref[...]

        m0, m1, i0, i1 = _pair_consts()
        i01 = i0 + i1
        e = e_ref[...]
        sub8 = lax.broadcasted_iota(jnp.int32, (SUBLANES, LANES), 0)
        lanes = [slice(q * LANES, (q + 1) * LANES) for q in range(npb)]
        ng = tc // SUBLANES

        def halves(s, row):
            return [s * (row * m0), s * (row * m1)]

        def group(gi, carry):
            base = pl.multiple_of((ng - 1 - gi) * SUBLANES, SUBLANES)
            rows = pl.ds(base, SUBLANES)
            ds = list(carry)
            r8 = [r_ref[rows, sl] for sl in lanes]
            w8 = [w_ref[rows, sl] for sl in lanes]
            k8 = [k_ref[rows, sl] for sl in lanes]
            kk8 = [kk_ref[rows, sl] for sl in lanes]
            a8 = [a_ref[rows, sl] for sl in lanes]
            v8 = [_split_bf16(v_ref[rows, sl]) for sl in lanes]
            dy8 = [_split_bf16(dy_ref[rows, sl]) for sl in lanes]
            zero8 = jnp.zeros((SUBLANES, LANES), F32)
            dr8, dw8, dk8, dv8, dkk8, da8 = ([zero8] * npb for _ in range(6))
            for j in reversed(range(SUBLANES)):
                one = slice(j, j + 1)
                here = sub8 == j
                d, s_prev, tiles = [], [], []
                for q in range(npb):
                    dyb = _col_pair(dy8[q], j, i01, e)
                    s_prev.append(sall_ref[base + j, :, lanes[q]])
                    dr8[q] = jnp.where(here, _colsum(sn_ref[:, lanes[q]] * dyb), dr8[q])
                    sn_ref[:, lanes[q]] = s_prev[q]
                    d.append(ds[q] + dyb * r8[q][one])
                for q in range(npb):
                    tiles += halves(d[q], kk8[q][one] * a8[q][one])
                for q in range(npb):
                    tiles += halves(d[q], k8[q][one]) + halves(s_prev[q], kk8[q][one])
                sums = _lanesum(jnp.concatenate(tiles, axis=0))
                cols = [sums[i * HEAD:(i + 1) * HEAD] for i in range(len(tiles))]
                for q in range(npb):
                    kkr, ar = kk8[q][one], a8[q][one]
                    dsb = -(cols[2 * q] * m0 + cols[2 * q + 1] * m1)
                    ds[q] = d[q] * w8[q][one] + dsb * kkr
                    rest = cols[2 * npb + 4 * q:2 * npb + 4 * q + 4]
                    dv8[q] = jnp.where(here, _row_pair(rest[0], rest[1], i0, i1), dv8[q])
                    sb = rest[2] * m0 + rest[3] * m1
                    db = -_colsum(d[q] * sb)
                    dk8[q] = jnp.where(here, _colsum(d[q] * _col_pair(v8[q], j, i01, e)), dk8[q])
                    dw8[q] = jnp.where(here, _colsum(d[q] * s_prev[q]), dw8[q])
                    dkk8[q] = jnp.where(here, _colsum(s_prev[q] * dsb) + db * ar, dkk8[q])
                    da8[q] = jnp.where(here, db * kkr, da8[q])
            for q in range(npb):
                sl = lanes[q]
                dr_ref[rows, sl], dw_ref[rows, sl], dk_ref[rows, sl] = dr8[q], dw8[q], dk8[q]
                dv_ref[rows, sl], dkk_ref[rows, sl], da_ref[rows, sl] = dv8[q], dkk8[q], da8[q]
            return tuple(ds)

        init = tuple(ds_ref[:, q * LANES:(q + 1) * LANES] for q in range(npb))
        fin = lax.fori_loop(0, ng, group, init)
        for q in range(npb):
            ds_ref[:, q * LANES:(q + 1) * LANES] = fin[q]

        @pl.when((here_p == n_blocks - 1) & (here_c == nc - 1))
        def _():
            exchange.finish(*ex_refs)

    row = pl.BlockSpec((tc, wb), lambda p, c: (nc - 1 - c, p))
    outs = pl.pallas_call(
        body, name="rwkv_scan_bwd", grid=(n_blocks, nc),
        in_specs=[row] * 7 + [pl.BlockSpec((tc, HEAD, wb), lambda p, c: (nc - 1 - c, 0, p)),
                              pl.BlockSpec((HEAD, wb), lambda p, c: (0, p)),
                              pl.BlockSpec((LANES, LANES), lambda p, c: (0, 0))] + [HBM_SPEC] * nx,
        out_specs=[row] * 6 + [HBM_SPEC] * nx,
        out_shape=[jax.ShapeDtypeStruct((t_dim, c_dim), F32)] * 6 + exchange.out_shape,
        scratch_shapes=[pltpu.VMEM((HEAD, wb), F32)] * 2 + exchange.scratch,
        compiler_params=_params("arbitrary", "arbitrary"),
    )(r, w, k, v, kk, a, dy, sall, sfin, _head_ones(), *exchange.arrays)
    return outs[:6], outs[6:]


def _fox_fwd(z, ct, cq, q_cb, k_cb, v_cb, n_pairs, blk):
    t_dim = z.shape[0]
    blk = _pick(t_dim, blk)
    nq = t_dim // blk
    scale = HEAD ** -0.5

    def body(q_ref, k_ref, v_ref, ct_ref, cq_ref, o_ref, lse_ref):
        i = pl.program_id(1)
        rowi = lax.broadcasted_iota(jnp.int32, (blk, blk), 0)
        coli = lax.broadcasted_iota(jnp.int32, (blk, blk), 1)
        masks = _head_masks()
        qv = q_ref[...]
        qs = [(qv * mk).astype(BF16) for mk in masks]
        cqs = [cq_ref[:, hh:hh + 1] for hh in range(2)]

        def kv_step(j, carry, masked):
            rows = pl.ds(pl.multiple_of(j * blk, blk), blk)
            kb = k_ref[rows, :].astype(BF16)
            vv = v_ref[rows, :]
            stats, acc = list(carry[:4]), carry[4]
            rescale, add = 0.0, 0.0
            for hh in range(2):
                m, l = stats[2 * hh], stats[2 * hh + 1]
                s = (lax.dot_general(qs[hh], kb, _DOT_DIMS["nt"], preferred_element_type=F32) * scale
                     + (cqs[hh] - ct_ref[hh:hh + 1, rows]))
                if masked:
                    s = jnp.where(rowi >= coli, s, -jnp.inf)
                m_new = jnp.maximum(m, jnp.max(s, axis=1, keepdims=True))
                alpha = jnp.exp(m - m_new)
                pr = jnp.exp(s - m_new)
                stats[2 * hh], stats[2 * hh + 1] = m_new, l * alpha + jnp.sum(pr, axis=1, keepdims=True)
                rescale = rescale + alpha * masks[hh]
                hi = pr.astype(BF16)
                both = jnp.concatenate([hi, (pr - hi.astype(F32)).astype(BF16)], axis=1)
                vh = (vv * masks[hh]).astype(BF16)
                add = add + jnp.dot(both, jnp.concatenate([vh, vh], axis=0), preferred_element_type=F32)
            return (*stats, acc * rescale + add)

        neg, zero = jnp.full((blk, 1), -jnp.inf, F32), jnp.zeros((blk, 1), F32)
        carry = lax.fori_loop(0, i, functools.partial(kv_step, masked=False),
                              (neg, zero, neg, zero, jnp.zeros((blk, LANES), F32)))
        m0, l0, m1, l1, acc = kv_step(i, carry, True)
        o_ref[...] = acc * (masks[0] / l0 + masks[1] / l1)
        lse_ref[:, 0:1] = m0 + jnp.log(l0)
        lse_ref[:, 1:2] = m1 + jnp.log(l1)

    full = lambda cb: pl.BlockSpec((t_dim, LANES), lambda p, i, cb=cb: (0, cb + p))
    return pl.pallas_call(
        body, name="fox_attn_fwd", grid=(n_pairs, nq),
        in_specs=[pl.BlockSpec((blk, LANES), lambda p, i: (i, q_cb + p)), full(k_cb), full(v_cb),
                  pl.BlockSpec((None, SUBLANES, t_dim), lambda p, i: (p, 0, 0)),
                  pl.BlockSpec((None, blk, 2), lambda p, i: (p, i, 0))],
        out_specs=[pl.BlockSpec((blk, LANES), lambda p, i: (i, p)), pl.BlockSpec((None, blk, 2), lambda p, i: (p, i, 0))],
        out_shape=[jax.ShapeDtypeStruct((t_dim, n_pairs * LANES), F32), jax.ShapeDtypeStruct((n_pairs, t_dim, 2), F32)],
        compiler_params=_params("parallel", "arbitrary"),
    )(z, z, z, ct, cq)


def _fox_rowdot(z, ct, cq, do, lse, q_cb, k_cb, v_cb, n_pairs, blk):
    t_dim = z.shape[0]
    blk = _pick(t_dim, blk)
    scale = HEAD ** -0.5

    def body(q_ref, k_ref, v_ref, ct_ref, cq_ref, do_ref, lse_ref, out_ref):
        i = pl.program_id(1)
        rowi = lax.broadcasted_iota(jnp.int32, (blk, blk), 0)
        coli = lax.broadcasted_iota(jnp.int32, (blk, blk), 1)
        masks = _head_masks()
        qv, dov = q_ref[...], do_ref[...]
        qs = [(qv * mk).astype(BF16) for mk in masks]
        dos = [(dov * mk).astype(BF16) for mk in masks]
        cqs = [cq_ref[:, hh:hh + 1] for hh in range(2)]
        lses = [lse_ref[:, hh:hh + 1] for hh in range(2)]

        def kv_step(j, carry, masked):
            rows = pl.ds(pl.multiple_of(j * blk, blk), blk)
            kb = k_ref[rows, :].astype(BF16)
            vb = v_ref[rows, :].astype(BF16)
            out = []
            for hh in range(2):
                num, den = carry[2 * hh], carry[2 * hh + 1]
                s = (lax.dot_general(qs[hh], kb, _DOT_DIMS["nt"], preferred_element_type=F32) * scale
                     + (cqs[hh] - ct_ref[hh:hh + 1, rows]))
                pr = jnp.exp(s - lses[hh])
                if masked:
                    pr = jnp.where(rowi >= coli, pr, 0.0)
                dp = lax.dot_general(dos[hh], vb, _DOT_DIMS["nt"], preferred_element_type=F32)
                out += [num + jnp.sum(pr * dp, axis=1, keepdims=True), den + jnp.sum(pr, axis=1, keepdims=True)]
            return tuple(out)

        zero = jnp.zeros((blk, 1), F32)
        carry = lax.fori_loop(0, i, functools.partial(kv_step, masked=False), (zero, zero, zero, zero))
        num0, den0, num1, den1 = kv_step(i, carry, True)
        out_ref[:, 0:1] = num0 / den0
        out_ref[:, 1:2] = num1 / den1

    full = lambda cb: pl.BlockSpec((t_dim, LANES), lambda p, i, cb=cb: (0, cb + p))
    return pl.pallas_call(
        body, name="fox_attn_rowdot", grid=(n_pairs, t_dim // blk),
        in_specs=[pl.BlockSpec((blk, LANES), lambda p, i: (i, q_cb + p)), full(k_cb), full(v_cb),
                  pl.BlockSpec((None, SUBLANES, t_dim), lambda p, i: (p, 0, 0)),
                  pl.BlockSpec((None, blk, 2), lambda p, i: (p, i, 0)),
                  pl.BlockSpec((blk, LANES), lambda p, i: (i, p)),
                  pl.BlockSpec((None, blk, 2), lambda p, i: (p, i, 0))],
        out_specs=pl.BlockSpec((None, blk, 2), lambda p, i: (p, i, 0)),
        out_shape=jax.ShapeDtypeStruct((n_pairs, t_dim, 2), F32),
        compiler_params=_params("parallel", "arbitrary"),
    )(z, z, z, ct, cq, do, lse)


def _fox_bwd(z, ct, cq, rowdot, do, lse, q_cb, k_cb, v_cb, n_pairs, blk):
    t_dim = z.shape[0]
    blk = _pick(t_dim, blk)
    nb = t_dim // blk
    scale = HEAD ** -0.5

    def body(q_ref, k_ref, v_ref, ct_ref, cq_ref, rd_ref, do_ref, lse_ref, dq_ref, dk_ref, dv_ref, dc_ref):
        j = pl.program_id(1)

        @pl.when(j == 0)
        def _():
            dq_ref[...] = jnp.zeros_like(dq_ref)

        rowi = lax.broadcasted_iota(jnp.int32, (blk, blk), 0)
        coli = lax.broadcasted_iota(jnp.int32, (blk, blk), 1)
        krows = pl.ds(pl.multiple_of(j * blk, blk), blk)
        masks = _head_masks()
        kv, vb = k_ref[...], v_ref[...].astype(BF16)
        kb = kv.astype(BF16)
        ks = [(kv * mk).astype(BF16) for mk in masks]
        cks = [ct_ref[hh:hh + 1, krows] for hh in range(2)]

        def q_step(i, carry, masked):
            dk, dv, dcs = carry[0], carry[1], list(carry[2:])
            rows = pl.ds(pl.multiple_of(i * blk, blk), blk)
            qv, dov = q_ref[rows, :], do_ref[rows, :]
            dq = 0.0
            for hh in range(2):
                qh = (qv * masks[hh]).astype(BF16)
                doh = (dov * masks[hh]).astype(BF16)
                s = (lax.dot_general(qh, kb, _DOT_DIMS["nt"], preferred_element_type=F32) * scale
                     + (cq_ref[rows, hh:hh + 1] - cks[hh]))
                pr = jnp.exp(s - lse_ref[rows, hh:hh + 1])
                if masked:
                    pr = jnp.where(rowi >= coli, pr, 0.0)
                dv = dv + lax.dot_general(pr.astype(BF16), doh, _DOT_DIMS["tn"], preferred_element_type=F32)
                dp = lax.dot_general(doh, vb, _DOT_DIMS["nt"], preferred_element_type=F32)
                ds = pr * (dp - rd_ref[rows, hh * HEAD:hh * HEAD + 1])
                dsb = ds.astype(BF16)
                dq = dq + jnp.dot(dsb, ks[hh], preferred_element_type=F32)
                dk = dk + lax.dot_general(dsb, qh, _DOT_DIMS["tn"], preferred_element_type=F32)
                dcs[hh] = dcs[hh] - _colsum(ds)
            dq_ref[rows, :] += dq * scale
            return (dk, dv, *dcs)

        zero_row = jnp.zeros((1, blk), F32)
        init = (jnp.zeros((blk, LANES), F32), jnp.zeros((blk, LANES), F32), zero_row, zero_row)
        carry = q_step(j, init, True)
        dk, dv, dc0, dc1 = lax.fori_loop(j + 1, nb, functools.partial(q_step, masked=False), carry)
        dk_ref[...] = dk * scale
        dv_ref[...] = dv
        dc_ref[0:1, :] = dc0
        dc_ref[1:2, :] = dc1
        dc_ref[2:SUBLANES, :] = jnp.zeros((SUBLANES - 2, blk), F32)

    full = lambda: pl.BlockSpec((t_dim, LANES), lambda p, j: (0, p))
    blkspec = pl.BlockSpec((blk, LANES), lambda p, j: (j, p))
    return pl.pallas_call(
        body, name="fox_attn_bwd", grid=(n_pairs, nb),
        in_specs=[pl.BlockSpec((t_dim, LANES), lambda p, j: (0, q_cb + p)),
                  pl.BlockSpec((blk, LANES), lambda p, j: (j, k_cb + p)),
                  pl.BlockSpec((blk, LANES), lambda p, j: (j, v_cb + p)),
                  pl.BlockSpec((None, SUBLANES, t_dim), lambda p, j: (p, 0, 0)),
                  pl.BlockSpec((None, t_dim, 2), lambda p, j: (p, 0, 0)), full(), full(),
                  pl.BlockSpec((None, t_dim, 2), lambda p, j: (p, 0, 0))],
        out_specs=[full(), blkspec, blkspec, pl.BlockSpec((None, SUBLANES, blk), lambda p, j: (p, 0, j))],
        out_shape=[jax.ShapeDtypeStruct((t_dim, n_pairs * LANES), F32)] * 3
                  + [jax.ShapeDtypeStruct((n_pairs, SUBLANES, t_dim), F32)],
        compiler_params=_params("parallel", "arbitrary"),
    )(z, z, z, ct, cq, rowdot, do, lse)


HBM_SPEC = pl.BlockSpec(memory_space=pltpu.HBM)


def _all_gather(shards, name):
    n = len(shards)

    def body(*refs):
        x_refs, out_refs = refs[:n], refs[n:2 * n]
        send_sems, recv_sems, local_sems = refs[2 * n:]
        x, y, c = lax.axis_index("x"), lax.axis_index("y"), lax.axis_index("c")
        me, sibling = (x, y, c), (x, y, 1 - c)
        chips = [(1 - x, y), (x, 1 - y), (1 - x, 1 - y)]

        def copy(a, k, block, to, from_input=False):
            px, py, pc = block
            slot = out_refs[a].at[4 * px + 2 * py + pc]
            return pltpu.make_async_remote_copy(
                src_ref=x_refs[a] if from_input else slot, dst_ref=slot,
                send_sem=send_sems.at[7 * a + k], recv_sem=recv_sems.at[7 * a + k], device_id=to, device_id_type=MESH)

        mine = [pltpu.make_async_copy(x_refs[a], out_refs[a].at[4 * x + 2 * y + c], local_sems.at[a]) for a in range(n)]
        for cp in mine:
            cp.start()
        first = []
        for a in range(n):
            first += [copy(a, 1 + j, me, (*chip, c), from_input=True) for j, chip in enumerate(chips)]
            first.append(copy(a, 0, me, sibling, from_input=True))
        for cp in first:
            cp.start()
        passed = []
        for a in range(n):
            for j, chip in enumerate(chips):
                copy(a, 1 + j, (*chip, c), me).wait_recv()
                passed.append(copy(a, 4 + j, (*chip, c), sibling))
                passed[-1].start()
        for a in range(n):
            copy(a, 0, sibling, me).wait_recv()
            for j, chip in enumerate(chips):
                copy(a, 4 + j, (*chip, 1 - c), me).wait_recv()
        for cp in first + passed:
            cp.wait_send()
        for cp in mine:
            cp.wait()

    return pl.pallas_call(
        body, name=name, out_shape=[jax.ShapeDtypeStruct((NDEV,) + s.shape, s.dtype) for s in shards],
        in_specs=[HBM_SPEC] * n, out_specs=[HBM_SPEC] * n,
        scratch_shapes=[pltpu.SemaphoreType.DMA((7 * n,)), pltpu.SemaphoreType.DMA((7 * n,)),
                        pltpu.SemaphoreType.DMA((n,))],
    )(*shards)


class _Exchange:
    def __init__(self, arrays):
        self.arrays = list(arrays)
        self.n = len(self.arrays)
        self.per_dest = [a.ndim == 3 for a in self.arrays]
        self.out_shape = [jax.ShapeDtypeStruct(a.shape if pd else (NDEV,) + a.shape, a.dtype)
                          for a, pd in zip(self.arrays, self.per_dest)]
        self.scratch = [pltpu.SemaphoreType.DMA((7 * self.n,)), pltpu.SemaphoreType.DMA((7 * self.n,)),
                        pltpu.SemaphoreType.DMA((self.n,))]

    def _copies(self, in_refs, out_refs, sems):
        send_sems, recv_sems, local_sems = sems
        x, y, c = lax.axis_index("x"), lax.axis_index("y"), lax.axis_index("c")
        me = 4 * x + 2 * y + c
        own, sends, recvs = [], [], []
        for a in range(self.n):
            mine = in_refs[a].at[me] if self.per_dest[a] else in_refs[a]
            own.append(pltpu.make_async_copy(mine, out_refs[a].at[me], local_sems.at[a]))
            for k in range(1, NDEV):
                px = 1 - x if k & 4 else x
                py = 1 - y if k & 2 else y
                pc = 1 - c if k & 1 else c
                peer = 4 * px + 2 * py + pc
                sem = dict(send_sem=send_sems.at[7 * a + k - 1], recv_sem=recv_sems.at[7 * a + k - 1],
                           device_id=(px, py, pc), device_id_type=MESH)
                src = in_refs[a].at[peer] if self.per_dest[a] else in_refs[a]
                sends.append(pltpu.make_async_remote_copy(src_ref=src, dst_ref=out_refs[a].at[me], **sem))
                recvs.append(pltpu.make_async_remote_copy(src_ref=src, dst_ref=out_refs[a].at[peer], **sem))
        return own, sends, recvs

    def start(self, in_refs, out_refs, sems):
        own, sends, _ = self._copies(in_refs, out_refs, sems)
        for cp in own + sends:
            cp.start()

    def finish(self, in_refs, out_refs, sems):
        own, sends, recvs = self._copies(in_refs, out_refs, sems)
        for cp in recvs:
            cp.wait_recv()
        for cp in sends:
            cp.wait_send()
        for cp in own:
            cp.wait()


def _grad_exchange(arrays, name):
    ex = _Exchange(arrays)

    def body(*refs):
        in_refs, out_refs, sems = refs[:ex.n], refs[ex.n:2 * ex.n], refs[2 * ex.n:]
        ex.start(in_refs, out_refs, sems)
        ex.finish(in_refs, out_refs, sems)

    return pl.pallas_call(body, name=name, out_shape=ex.out_shape, in_specs=[HBM_SPEC] * ex.n,
                          out_specs=[HBM_SPEC] * ex.n, scratch_shapes=ex.scratch)(*ex.arrays)


def _adamw(partials, w, m, v, name):
    rows, width = w.shape
    br = _pick(rows, 128, 2 * SUBLANES)

    def body(p_ref, w_ref, m_ref, v_ref, g_out, d_out, m_out, v_out):
        g = p_ref[0].astype(F32)
        for d in range(1, NDEV):
            g = g + p_ref[d].astype(F32)
        mn = ADAM_B1 * m_ref[...] + (1.0 - ADAM_B1) * g
        vn = ADAM_B2 * v_ref[...] + (1.0 - ADAM_B2) * jnp.square(g)
        m_hat = mn / (1.0 - ADAM_B1 ** ADAM_STEP)
        v_hat = vn / (1.0 - ADAM_B2 ** ADAM_STEP)
        g_out[...] = g
        d_out[...] = -ADAM_LR * (m_hat / (jnp.sqrt(v_hat) + ADAM_EPS) + ADAM_WD * w_ref[...])
        m_out[...] = mn
        v_out[...] = vn

    blk = pl.BlockSpec((br, width), lambda i: (i, 0))
    return pl.pallas_call(
        body, name=name, grid=(rows // br,),
        in_specs=[pl.BlockSpec((NDEV, br, width), lambda i: (0, i, 0)), blk, blk, blk],
        out_specs=[pl.BlockSpec((None, br, width), lambda i: (0, i, 0))] * 4,
        out_shape=[jax.ShapeDtypeStruct((1, rows, width), F32)] * 4,
        compiler_params=_params("parallel"),
    )(partials, w, m, v)


def _pack_rows(flat):
    n = flat.shape[0]
    padded = _round_up(n, PACK_ALIGN)
    return jnp.pad(flat, (0, padded - n)).reshape(padded // PACK_W, PACK_W)


def _split_shards(full, axis):
    rows, cols = full.shape
    if axis == 0:
        return full.reshape(NDEV, rows // NDEV, cols)
    width = cols // NDEV
    return jnp.stack([full[:, d * width:(d + 1) * width] for d in range(NDEV)])


def _join_shards(blocks, axis):
    if axis == 0:
        return blocks.reshape(-1, blocks.shape[2])
    return jnp.concatenate([blocks[d] for d in range(NDEV)], axis=1)


SHARDED = (("w_in", 1), ("rw_w_lora_up", 1), ("rw_a_lora_up", 1), ("w_up_rwkv", 1), ("w_up_fox", 1),
           ("w_out", 0), ("ple_proj", 1), ("ple_gate_w", 0))
REPLICATED = ("norm_g", "rw_shift_mu", "rw_w0", "rw_a0", "rw_k_k", "rw_k_a", "rw_r_k", "rw_ln_g", "rw_ln_b",
              "fox_b_f", "ple_norm_g", "final_norm_g")
WEIGHTS = ("norm_g", "w_in", "rw_shift_mu", "rw_w0", "rw_w_lora_up", "rw_a0", "rw_a_lora_up", "rw_k_k", "rw_k_a",
           "rw_r_k", "rw_ln_g", "rw_ln_b", "fox_b_f", "w_up_rwkv", "w_up_fox", "w_out", "ple_proj", "ple_gate_w",
           "ple_norm_g", "final_norm_g")


def _local_step(x, p, tgt, wz, wl, wa, wur, wuf, wo, pp, pg, rep, dims):
    t_dim, d_model, c_rw, lora, c_fox, h_fox, sec = dims
    bt = _pick(t_dim, 256, 2 * SUBLANES)
    bt_many = _pick(t_dim, 128, 2 * SUBLANES)
    n_pairs = c_fox // LANES
    row = lambda a: a.reshape(1, -1)
    norm_g, mu, w0, a0 = row(rep["norm_g"]), row(rep["rw_shift_mu"]), row(rep["rw_w0"]), row(rep["rw_a0"])
    k_k, k_a, r_k = row(rep["rw_k_k"]), row(rep["rw_k_a"]), row(rep["rw_r_k"])
    ln_g, ln_b = row(rep["rw_ln_g"]), row(rep["rw_ln_b"])
    g2, g3 = row(rep["ple_norm_g"]), row(rep["final_norm_g"])
    b_f = jnp.pad(row(rep["fox_b_f"]), ((0, 0), (0, LANES - h_fox)))
    e_head = _head_ones(F32)
    c4 = 4 * c_rw
    inv_d = 1.0 / d_model
    decay_k = math.exp(-0.5)

    def norm_in(x_ref, g_ref, h_ref):
        xv = x_ref[...]
        rms = lax.rsqrt(jnp.mean(xv * xv, axis=-1, keepdims=True) + NORM_EPS)
        h_ref[...] = (xv * rms * g_ref[...]).astype(BF16)

    (h,) = _rowcall(norm_in, "norm_in", t_dim, bt, [x], [norm_g], [(d_model, BF16)])
    z = _matmul(h, wz, "nn", "proj_in", bn=1408)

    def rw_values(zs, w0_ref, wl_ref, a0_ref, wa_ref, kk_ref, ka_ref, e_ref):
        k = zs[:, c_rw:2 * c_rw]
        tw = jnp.tanh(zs[:, c4:c4 + lora])
        al = zs[:, c4 + lora:c4 + 2 * lora]
        sw = _sigmoid(w0_ref[...] + jnp.dot(tw.astype(BF16), wl_ref[...], preferred_element_type=F32))
        decay = jnp.exp(-decay_k * sw)
        a = _sigmoid(a0_ref[...] + jnp.dot(al.astype(BF16), wa_ref[...], preferred_element_type=F32))
        kk0 = k * kk_ref[...]
        nrm = jnp.sqrt(_headsum(kk0 * kk0, e_ref[...]))
        inv = 1.0 / jnp.maximum(nrm, 1e-12)
        k2 = k * (1.0 + (a - 1.0) * ka_ref[...])
        return k, tw, al, sw, decay, a, kk0, nrm, inv, k2

    def rw_prep(z_ref, mu_ref, w0_ref, wl_ref, a0_ref, wa_ref, kk_ref, ka_ref, e_ref,
                r_o, w_o, k_o, v_o, kk_o, a_o, g_o, carry):
        _first_step_zero(carry)
        zv = z_ref[...]
        zs = zv + (_shift_down(zv, carry) - zv) * mu_ref[...]
        k, tw, al, sw, decay, a, kk0, nrm, inv, k2 = rw_values(zs, w0_ref, wl_ref, a0_ref, wa_ref, kk_ref, ka_ref, e_ref)
        r_o[...] = zs[:, 0:c_rw]
        w_o[...] = decay
        k_o[...] = k2
        v_o[...] = zs[:, 2 * c_rw:3 * c_rw]
        kk_o[...] = kk0 * inv
        a_o[...] = a
        g_o[...] = zs[:, 3 * c_rw:c4]

    rw_consts = [mu, w0, wl, a0, wa, k_k, k_a, e_head]
    r_s, w_s, k_s, v_s, kk_s, a_s, g_s = _rowcall(
        rw_prep, "rwkv_prep", t_dim, bt, [(z, 0, sec)], rw_consts, [(c_rw, F32)] * 7,
        scratch=[pltpu.VMEM((1, sec), F32)])
    pairs_fwd = max(n for n in (1, 2, 4) if c_rw % (n * LANES) == 0)
    pairs_bwd = pairs_fwd
    y_s, s_all, s_fin = _scan_fwd(r_s, w_s, k_s, v_s, kk_s, a_s, 64, pairs_fwd)

    def rw_post_values(y, r, k2, v, g, lng_ref, lnb_ref, rk_ref, e):
        mean = _headsum(y, e) * (1.0 / HEAD)
        d = y - mean
        rstd = lax.rsqrt(_headsum(d * d, e) * (1.0 / HEAD) + GN_EPS)
        yh = d * rstd
        rk = _headsum(r * k2 * rk_ref[...], e)
        yo = yh * lng_ref[...] + lnb_ref[...] + rk * v
        sg = _sigmoid(g)
        return rstd, yh, rk, yo, sg

    def rw_post(y_ref, r_ref, k_ref, v_ref, g_ref, lng_ref, lnb_ref, rk_ref, e_ref, out_ref):
        g = g_ref[...]
        _, _, _, yo, sg = rw_post_values(y_ref[...], r_ref[...], k_ref[...], v_ref[...], g, lng_ref, lnb_ref, rk_ref, e_ref[...])
        out_ref[...] = (yo * g * sg).astype(BF16)

    (y_rw,) = _rowcall(rw_post, "rwkv_post", t_dim, bt, [y_s, r_s, k_s, v_s, g_s], [ln_g, ln_b, r_k, e_head], [(c_rw, BF16)])

    fl_cb = (sec + 4 * c_fox) // LANES
    hp = _round_up(h_fox, SUBLANES)
    bt_c = _pick(t_dim, 256)
    tri = (jnp.arange(bt_c)[:, None] >= jnp.arange(bt_c)[None, :]).astype(F32)

    rows8 = n_pairs * SUBLANES
    pair_rows = (jnp.arange(rows8)[:, None] // SUBLANES * 2 + jnp.arange(rows8)[:, None] % SUBLANES
                 == jnp.arange(LANES)[None, :]) & (jnp.arange(rows8)[:, None] % SUBLANES < 2)
    pair_rows = pair_rows.astype(F32)

    def fox_decay(fl_ref, bf_ref, tri_ref, sel_ref, ct_ref, cq_ref, carry):
        _first_step_zero(carry)
        lf = _log_sigmoid(fl_ref[...] + bf_ref[...])
        c = jnp.dot(tri_ref[...], lf, precision=HI, preferred_element_type=F32) + carry[...]
        carry[...] = c[bt_c - 1:bt_c, :]
        ct = jnp.dot(sel_ref[...], jnp.transpose(c), precision=HI, preferred_element_type=F32)
        ct_ref[...] = ct.reshape(n_pairs, SUBLANES, bt_c)
        for pair in range(n_pairs):
            cq_ref[pair] = c[:, 2 * pair:2 * pair + 2]

    ct, cq = pl.pallas_call(
        fox_decay, name="fox_decay", grid=(t_dim // bt_c,),
        in_specs=[pl.BlockSpec((bt_c, LANES), lambda i: (i, fl_cb)), pl.BlockSpec((1, LANES), lambda i: (0, 0)),
                  pl.BlockSpec((bt_c, bt_c), lambda i: (0, 0)), pl.BlockSpec((rows8, LANES), lambda i: (0, 0))],
        out_specs=[pl.BlockSpec((n_pairs, SUBLANES, bt_c), lambda i: (0, 0, i)),
                   pl.BlockSpec((n_pairs, bt_c, 2), lambda i: (0, i, 0))],
        out_shape=[jax.ShapeDtypeStruct((n_pairs, SUBLANES, t_dim), F32), jax.ShapeDtypeStruct((n_pairs, t_dim, 2), F32)],
        scratch_shapes=[pltpu.VMEM((1, LANES), F32)], compiler_params=_params("arbitrary"),
    )(z, b_f, tri, pair_rows)
    q_cb = sec // LANES
    k_cb, v_cb = q_cb + n_pairs, q_cb + 2 * n_pairs
    o_fox, lse = _fox_fwd(z, ct, cq, q_cb, k_cb, v_cb, n_pairs, 256)

    def fox_post(o_ref, z_ref, out_ref):
        g = z_ref[:, 3 * c_fox:4 * c_fox]
        out_ref[...] = (o_ref[...] * g * _sigmoid(g)).astype(BF16)

    (y_fox,) = _rowcall(fox_post, "fox_post", t_dim, bt, [o_fox, (z, 1, sec)], [], [(c_fox, BF16)])

    u_rw = _matmul(y_rw, wur, "nn", "up_rwkv")
    u_fox = _matmul(y_fox, wuf, "nn", "up_fox")

    def merge(ur_ref, uf_ref, z_ref, out_ref):
        s1 = _sigmoid(z_ref[:, 0:d_model])
        s2 = _sigmoid(z_ref[:, d_model:2 * d_model])
        out_ref[...] = (s1 * ur_ref[...] + s2 * uf_ref[...]).astype(BF16)

    (merged,) = _rowcall(merge, "merge", t_dim, bt, [u_rw, u_fox, (z, 2, sec)], [], [(d_model, BF16)])
    mo = _matmul(merged, wo, "nn", "proj_out")

    def resid_norm(x_ref, mo_ref, g_ref, x1_ref, n2_ref):
        x1 = x_ref[...] + mo_ref[...]
        rms = lax.rsqrt(jnp.mean(x1 * x1, axis=-1, keepdims=True) + NORM_EPS)
        x1_ref[...] = x1
        n2_ref[...] = (x1 * rms * g_ref[...]).astype(BF16)

    x1, n2 = _rowcall(resid_norm, "resid_norm", t_dim, bt, [x, mo], [g2], [(d_model, F32), (d_model, BF16)])
    ple = _matmul(p, pp, "nn", "ple_proj")
    gl = _matmul(n2, pg, "nn", "ple_gate")

    def head(x1_ref, ple_ref, gl_ref, tgt_ref, g_ref, dx2_ref, dple_ref, dgl_ref, loss_ref, dg3_ref):
        _first_step_zero(loss_ref, dg3_ref)
        sg = _sigmoid(gl_ref[...])
        pl_v = ple_ref[...]
        x2 = x1_ref[...] + pl_v * sg
        rms = lax.rsqrt(jnp.mean(x2 * x2, axis=-1, keepdims=True) + NORM_EPS)
        xn = x2 * rms
        diff = xn * g_ref[...] - tgt_ref[...]
        loss_ref[...] += 0.5 * jnp.sum(jnp.mean(diff * diff, axis=-1, keepdims=True), axis=0, keepdims=True)
        dyf = diff * inv_d
        dg3_ref[...] += _colsum(dyf * xn)
        gy = dyf * g_ref[...]
        dx2 = rms * (gy - xn * jnp.mean(xn * gy, axis=-1, keepdims=True))
        dx2_ref[...] = dx2
        dple_ref[...] = (dx2 * sg).astype(BF16)
        dgl_ref[...] = (dx2 * pl_v * sg * (1.0 - sg)).astype(BF16)

    dx2, dple, dgl, loss, d_g3 = _rowcall(
        head, "head", t_dim, bt, [x1, ple, gl, tgt], [g3], [(d_model, F32), (d_model, BF16), (d_model, BF16)],
        acc_outs=[(1, 1), (1, d_model)])

    d_pp = _matmul(p, dple, "tn", "d_ple_proj", out_dtype=BF16)
    d_pg = _matmul(n2, dgl, "tn", "d_ple_gate", out_dtype=BF16)
    dn2 = _matmul(dgl, pg, "nt", "d_n2")

    def resid_norm_bwd(dx2_ref, dn2_ref, x1_ref, g_ref, dx1_ref, dx1b_ref, dg2_ref):
        _first_step_zero(dg2_ref)
        x1 = x1_ref[...]
        rms = lax.rsqrt(jnp.mean(x1 * x1, axis=-1, keepdims=True) + NORM_EPS)
        xn = x1 * rms
        dn = dn2_ref[...]
        dg2_ref[...] += _colsum(dn * xn)
        gy = dn * g_ref[...]
        dx1 = dx2_ref[...] + rms * (gy - xn * jnp.mean(xn * gy, axis=-1, keepdims=True))
        dx1_ref[...] = dx1
        dx1b_ref[...] = dx1.astype(BF16)

    dx1, dx1b, d_g2 = _rowcall(resid_norm_bwd, "resid_norm_bwd", t_dim, bt, [dx2, dn2, x1], [g2],
                               [(d_model, F32), (d_model, BF16)], acc_outs=[(1, d_model)])
    d_wo = _matmul(merged, dx1b, "tn", "d_w_out", out_dtype=BF16)
    dmerged = _matmul(dx1b, wo, "nt", "d_merged")

    def merge_bwd(dm_ref, ur_ref, uf_ref, z_ref, dur_ref, duf_ref, dzg_ref):
        dm = dm_ref[...]
        s1 = _sigmoid(z_ref[:, 0:d_model])
        s2 = _sigmoid(z_ref[:, d_model:2 * d_model])
        dur_ref[...] = (dm * s1).astype(BF16)
        duf_ref[...] = (dm * s2).astype(BF16)
        dzg_ref[:, 0:d_model] = (dm * ur_ref[...] * s1 * (1.0 - s1)).astype(BF16)
        dzg_ref[:, d_model:2 * d_model] = (dm * uf_ref[...] * s2 * (1.0 - s2)).astype(BF16)
        if sec > 2 * d_model:
            dzg_ref[:, 2 * d_model:sec] = jnp.zeros((dm.shape[0], sec - 2 * d_model), BF16)

    du_rw, du_fox, dz_gate = _rowcall(merge_bwd, "merge_bwd", t_dim, bt, [dmerged, u_rw, u_fox, (z, 2, sec)], [],
                                      [(d_model, BF16), (d_model, BF16), (sec, BF16)])
    d_wur = _matmul(y_rw, du_rw, "tn", "d_w_up_rwkv", out_dtype=BF16)
    d_wuf = _matmul(y_fox, du_fox, "tn", "d_w_up_fox", out_dtype=BF16)
    dy_rw = _matmul(du_rw, wur, "nt", "d_y_rwkv")
    dy_fox = _matmul(du_fox, wuf, "nt", "d_y_fox")

    def fox_post_bwd(dy_ref, o_ref, z_ref, e_ref, do_ref, dg_ref, rd_ref):
        g = z_ref[:, 3 * c_fox:4 * c_fox]
        sg = _sigmoid(g)
        dy, o = dy_ref[...], o_ref[...]
        do = dy * g * sg
        do_ref[...] = do
        dg_ref[...] = (dy * o * sg * (1.0 + g * (1.0 - sg))).astype(BF16)
        rd_ref[...] = _headsum(do.astype(BF16).astype(F32) * o, e_ref[...])

    do_fox, dg_fox, rowdot = _rowcall(fox_post_bwd, "fox_post_bwd", t_dim, bt, [dy_fox, o_fox, (z, 1, sec)], [e_head],
                                      [(c_fox, F32), (c_fox, BF16), (c_fox, F32)])
    dq_f, dk_f, dv_f, dc_t = _fox_bwd(z, ct, cq, rowdot, do_fox, lse, q_cb, k_cb, v_cb, n_pairs, 256)
    sel = (jnp.arange(hp)[:, None] // 2 * SUBLANES + jnp.arange(hp)[:, None] % 2 == jnp.arange(rows8)[None, :]).astype(F32)
    tri_rev = (jnp.arange(bt_c)[:, None] >= jnp.arange(bt_c)[None, :]).astype(F32)
    bf_col = b_f.reshape(LANES, 1)[0:hp]
    nbc = t_dim // bt_c

    def fox_decay_bwd(dc_ref, fl_ref, sel_ref, tri_ref, bf_ref, dfl_ref, dbf_ref, carry):
        _first_step_zero(carry, dbf_ref)
        dc = jnp.dot(sel_ref[...], dc_ref[...].reshape(rows8, bt_c), precision=HI, preferred_element_type=F32)
        dlf = jnp.dot(dc, tri_ref[...], precision=HI, preferred_element_type=F32) + carry[...]
        carry[...] = dlf[:, 0:1]
        flt = jnp.transpose(fl_ref[...])[0:hp, :]
        dfl = dlf * _sigmoid(-(flt + bf_ref[...]))
        head_row = lax.broadcasted_iota(jnp.int32, (hp, bt_c), 0)
        dfl = jnp.where(head_row < h_fox, dfl, 0.0)
        dbf_ref[...] += jnp.sum(dfl, axis=1, keepdims=True)
        full = jnp.concatenate([dfl, jnp.zeros((LANES - hp, bt_c), F32)], axis=0) if hp < LANES else dfl
        dfl_ref[...] = jnp.transpose(full).astype(BF16)

    dz_fl, d_bf = pl.pallas_call(
        fox_decay_bwd, name="fox_decay_bwd", grid=(nbc,),
        in_specs=[pl.BlockSpec((n_pairs, SUBLANES, bt_c), lambda i: (0, 0, nbc - 1 - i)),
                  pl.BlockSpec((bt_c, LANES), lambda i: (nbc - 1 - i, fl_cb)),
                  pl.BlockSpec(sel.shape, lambda i: (0, 0)), pl.BlockSpec((bt_c, bt_c), lambda i: (0, 0)),
                  pl.BlockSpec((hp, 1), lambda i: (0, 0))],
        out_specs=[pl.BlockSpec((bt_c, LANES), lambda i: (nbc - 1 - i, 0)), pl.BlockSpec((hp, 1), lambda i: (0, 0))],
        out_shape=[jax.ShapeDtypeStruct((t_dim, LANES), BF16), jax.ShapeDtypeStruct((hp, 1), F32)],
        scratch_shapes=[pltpu.VMEM((hp, 1), F32)], compiler_params=_params("arbitrary"),
    )(dc_t, z, sel, tri_rev, bf_col)

    def rw_post_bwd(dy_ref, y_ref, r_ref, k_ref, v_ref, g_ref, lng_ref, lnb_ref, rk_ref, e_ref,
                    dg_ref, dys_ref, dr_ref, dk_ref, dv_ref, dlng_ref, dlnb_ref, drk_ref):
        _first_step_zero(dlng_ref, dlnb_ref, drk_ref)
        e = e_ref[...]
        dy, r, k2, v, g = dy_ref[...], r_ref[...], k_ref[...], v_ref[...], g_ref[...]
        rstd, yh, rk, yo, sg = rw_post_values(y_ref[...], r, k2, v, g, lng_ref, lnb_ref, rk_ref, e)
        dg_ref[...] = dy * yo * sg * (1.0 + g * (1.0 - sg))
        dyo = dy * g * sg
        dlnb_ref[...] += _colsum(dyo)
        dlng_ref[...] += _colsum(dyo * yh)
        dyh = dyo * lng_ref[...]
        dys_ref[...] = rstd * (dyh - _headsum(dyh, e) * (1.0 / HEAD) - yh * _headsum(dyh * yh, e) * (1.0 / HEAD))
        drk = _headsum(dyo * v, e)
        dv_ref[...] = dyo * rk
        dr_ref[...] = drk * k2 * rk_ref[...]
        dk_ref[...] = drk * r * rk_ref[...]
        drk_ref[...] += _colsum(drk * r * k2)

    dg_rw, dy_s, dr_b, dk_b, dv_b, d_lng, d_lnb, d_rk = _rowcall(
        rw_post_bwd, "rwkv_post_bwd", t_dim, bt, [dy_rw, y_s, r_s, k_s, v_s, g_s], [ln_g, ln_b, r_k, e_head],
        [(c_rw, F32)] * 5, acc_outs=[(1, c_rw)] * 3)
    axis = dict(SHARDED)
    early = {"w_up_rwkv": d_wur, "w_up_fox": d_wuf, "w_out": d_wo, "ple_proj": d_pp, "ple_gate_w": d_pg}
    (dr_c, dw_c, dk_c, dv_c, dkk_c, da_c), early_recv = _scan_bwd(
        r_s, w_s, k_s, v_s, kk_s, a_s, dy_s, s_all, s_fin, 64, pairs_bwd,
        _Exchange([_split_shards(g, axis[n]) for n, g in early.items()]))
    recv = dict(zip(early, early_recv))

    def rw_prep_bwd(z_ref, dr1, dr2, dw_ref, dk1, dk2_ref, dv1, dv2, dkk_ref, da_ref, dg_ref,
                    mu_ref, w0_ref, wl_ref, a0_ref, wa_ref, kk_ref, ka_ref, e_ref,
                    dzs_ref, tw_ref, al_ref, dwr_ref, dar_ref, dmu_ref, dw0_ref, da0_ref, dkk_acc, dka_acc, carry):
        _first_step_zero(carry, dmu_ref, dw0_ref, da0_ref, dkk_acc, dka_acc)
        e = e_ref[...]
        zv = z_ref[...]
        zp = _shift_down(zv, carry)
        zs = zv + (zp - zv) * mu_ref[...]
        k, tw, al, sw, decay, a, kk0, nrm, inv, k2 = rw_values(zs, w0_ref, wl_ref, a0_ref, wa_ref, kk_ref, ka_ref, e_ref)
        dk2 = dk1[...] + dk2_ref[...]
        da = da_ref[...] + dk2 * k * ka_ref[...]
        dk = dk2 * (1.0 + (a - 1.0) * ka_ref[...])
        dka_acc[...] += _colsum(dk2 * k * (a - 1.0))
        kk = kk0 * inv
        dkk = dkk_ref[...]
        dkk0 = inv * jnp.where(nrm > 1e-12, dkk - kk * _headsum(dkk * kk, e), dkk)
        dk = dk + dkk0 * kk_ref[...]
        dkk_acc[...] += _colsum(dkk0 * k)
        da_raw = da * a * (1.0 - a)
        da0_ref[...] += _colsum(da_raw)
        dw_raw = dw_ref[...] * decay * (-decay_k) * sw * (1.0 - sw)
        dw0_ref[...] += _colsum(dw_raw)
        dar_b, dwr_b = da_raw.astype(BF16), dw_raw.astype(BF16)
        dal = lax.dot_general(dar_b, wa_ref[...], _DOT_DIMS["nt"], preferred_element_type=F32)
        dtw = lax.dot_general(dwr_b, wl_ref[...], _DOT_DIMS["nt"], preferred_element_type=F32)
        dzs_ref[:, 0:c_rw] = dr1[...] + dr2[...]
        dzs_ref[:, c_rw:2 * c_rw] = dk
        dzs_ref[:, 2 * c_rw:3 * c_rw] = dv1[...] + dv2[...]
        dzs_ref[:, 3 * c_rw:c4] = dg_ref[...]
        dzs_ref[:, c4:c4 + lora] = dtw * (1.0 - tw * tw)
        dzs_ref[:, c4 + lora:c4 + 2 * lora] = dal
        if sec > c4 + 2 * lora:
            dzs_ref[:, c4 + 2 * lora:sec] = jnp.zeros((zv.shape[0], sec - c4 - 2 * lora), F32)
        tw_ref[...] = tw.astype(BF16)
        al_ref[...] = al.astype(BF16)
        dwr_ref[...] = dwr_b
        dar_ref[...] = dar_b
        dmu_ref[...] += _colsum(dzs_ref[...] * (zp - zv))

    dzs, tw_b, al_b, dwr_b, dar_b, d_mu, d_w0, d_a0, d_kk, d_ka = _rowcall(
        rw_prep_bwd, "rwkv_prep_bwd", t_dim, bt_many,
        [(z, 0, sec), dr_c, dr_b, dw_c, dk_c, dk_b, dv_c, dv_b, dkk_c, da_c, dg_rw], rw_consts,
        [(sec, F32), (lora, BF16), (lora, BF16), (c_rw, BF16), (c_rw, BF16)],
        acc_outs=[(1, sec), (1, c_rw), (1, c_rw), (1, c_rw), (1, c_rw)], scratch=[pltpu.VMEM((1, sec), F32)])
    d_wl = _matmul(tw_b, dwr_b, "tn", "d_w_lora", out_dtype=BF16)
    d_wa = _matmul(al_b, dar_b, "tn", "d_a_lora", out_dtype=BF16)

    def shift_bwd(dzs_ref, mu_ref, dz_ref, carry):
        _first_step_zero(carry)
        d = dzs_ref[...]
        nbt = d.shape[0]
        nxt = pltpu.roll(d, nbt - 1, 0)
        rowi = lax.broadcasted_iota(jnp.int32, d.shape, 0)
        nxt = jnp.where(rowi == nbt - 1, carry[...], nxt)
        carry[...] = d[0:1, :]
        m = mu_ref[...]
        dz_ref[...] = (d * (1.0 - m) + nxt * m).astype(BF16)

    (dz_rw,) = _rowcall(shift_bwd, "shift_bwd", t_dim, bt, [dzs], [mu], [(sec, BF16)],
                        scratch=[pltpu.VMEM((1, sec), F32)], reverse=True)

    fox_parts = [dq_f.astype(BF16), dk_f.astype(BF16), dv_f.astype(BF16), dg_fox, dz_fl]
    if sec > 4 * c_fox + LANES:
        fox_parts.append(jnp.zeros((t_dim, sec - 4 * c_fox - LANES), BF16))
    dz = jnp.concatenate([dz_rw] + fox_parts + [dz_gate], axis=1)
    d_wz = _matmul(h, dz, "tn", "d_w_in", out_dtype=BF16, bn=1408)
    rw_cols, fox_cols = c4 + 2 * lora, 4 * c_fox + h_fox
    d_wi = jnp.concatenate([d_wz[:, :rw_cols], d_wz[:, sec:sec + fox_cols], d_wz[:, 2 * sec:2 * sec + 2 * d_model]], axis=1)
    late = {"w_in": d_wi, "rw_w_lora_up": d_wl, "rw_a_lora_up": d_wa}
    dh, late_recv = _matmul(dz, wz, "nt", "d_h", bk=sec,
                            exchange=_Exchange([_split_shards(g, axis[n]) for n, g in late.items()]))
    recv.update(zip(late, late_recv))

    def norm_in_bwd(dh_ref, x_ref, dx1_ref, g_ref, dx_ref, dg1_ref):
        _first_step_zero(dg1_ref)
        xv = x_ref[...]
        rms = lax.rsqrt(jnp.mean(xv * xv, axis=-1, keepdims=True) + NORM_EPS)
        xn = xv * rms
        d = dh_ref[...]
        dg1_ref[...] += _colsum(d * xn)
        gy = d * g_ref[...]
        dx_ref[...] = dx1_ref[...] + rms * (gy - xn * jnp.mean(xn * gy, axis=-1, keepdims=True))

    dx, d_g1 = _rowcall(norm_in_bwd, "norm_in_bwd", t_dim, bt, [dh, x, dx1], [norm_g], [(d_model, F32)],
                        acc_outs=[(1, d_model)])

    rep_grads = {"norm_g": d_g1, "rw_shift_mu": d_mu[:, 0:c4 + 2 * lora], "rw_w0": d_w0, "rw_a0": d_a0, "rw_k_k": d_kk,
                 "rw_k_a": d_ka, "rw_r_k": d_rk, "rw_ln_g": d_lng, "rw_ln_b": d_lnb, "fox_b_f": d_bf[0:h_fox, 0],
                 "ple_norm_g": d_g2, "final_norm_g": d_g3}
    return loss[0, 0], dx, recv, rep_grads


def kernel(x, p, norm_g, w_in, rw_shift_mu, rw_w0, rw_w_lora_up, rw_a0, rw_a_lora_up, rw_k_k, rw_k_a, rw_r_k, rw_ln_g, rw_ln_b, fox_b_f, w_up_rwkv, w_up_fox, w_out, ple_proj, ple_gate_w, ple_norm_g, final_norm_g, loss_target, m_norm_g, m_w_in, m_rw_shift_mu, m_rw_w0, m_rw_w_lora_up, m_rw_a0, m_rw_a_lora_up, m_rw_k_k, m_rw_k_a, m_rw_r_k, m_rw_ln_g, m_rw_ln_b, m_fox_b_f, m_w_up_rwkv, m_w_up_fox, m_w_out, m_ple_proj, m_ple_gate_w, m_ple_norm_g, m_final_norm_g, v_norm_g, v_w_in, v_rw_shift_mu, v_rw_w0, v_rw_w_lora_up, v_rw_a0, v_rw_a_lora_up, v_rw_k_k, v_rw_k_a, v_rw_r_k, v_rw_ln_g, v_rw_ln_b, v_fox_b_f, v_w_up_rwkv, v_w_up_fox, v_w_out, v_ple_proj, v_ple_gate_w, v_ple_norm_g, v_final_norm_g):
    args = locals()
    w = {n: args[n] for n in WEIGHTS}
    mom = {n: args["m_" + n] for n in WEIGHTS}
    var = {n: args["v_" + n] for n in WEIGHTS}

    t_dim, d_model = x.shape[1], x.shape[2]
    c_rw, lora = rw_w0.shape[1], rw_w_lora_up.shape[1]
    h_fox = fox_b_f.shape[1]
    c_fox = h_fox * HEAD
    rw_cols, fox_cols, gate_cols = 4 * c_rw + 2 * lora, 4 * c_fox + h_fox, 2 * d_model
    sec = max(rw_cols, 4 * c_fox + LANES, _round_up(gate_cols, LANES))
    assert c_rw % LANES == 0 and c_fox % LANES == 0 and rw_cols % LANES == 0 and h_fox <= LANES and d_model % LANES == 0
    assert rw_a_lora_up.shape[1] == lora and w_in.shape[2] * NDEV == rw_cols + fox_cols + gate_cols

    gathered = _all_gather([w[n][0].astype(BF16) for n, _ in SHARDED], "gather_weights")
    full = {n: _join_shards(g, ax) for (n, ax), g in zip(SHARDED, gathered)}

    def to_sections(wi):
        pad = lambda a, width: jnp.pad(a, ((0, 0), (0, width - a.shape[1])))
        return jnp.concatenate([pad(wi[:, :rw_cols], sec), pad(wi[:, rw_cols:rw_cols + fox_cols], sec),
                                pad(wi[:, rw_cols + fox_cols:], sec)], axis=1)

    rep = {n: w[n] for n in REPLICATED}
    dims = (t_dim, d_model, c_rw, lora, c_fox, h_fox, sec)
    loss_local, grad_x, recv, rep_grads = _local_step(
        x[0], p[0, 0], loss_target[0], to_sections(full["w_in"]), full["rw_w_lora_up"], full["rw_a_lora_up"],
        full["w_up_rwkv"], full["w_up_fox"], full["w_out"], full["ple_proj"], full["ple_gate_w"], rep, dims)

    rep_sizes = [w[n].size for n in REPLICATED]
    rep_offs = [sum(rep_sizes[:i]) for i in range(len(rep_sizes))]
    pack_rep = lambda tree: _pack_rows(jnp.concatenate([tree[n].astype(F32).reshape(-1) for n in REPLICATED]))
    (small_all,) = _grad_exchange([pack_rep(rep_grads)], "exchange_replicated")
    kinds = ("grad", "delta", "new_m", "new_v")
    outs = {}
    for n, _ in SHARDED:
        for kind, buf in zip(kinds, _adamw(recv[n], w[n][0], mom[n][0], var[n][0], "adamw_" + n)):
            outs[kind, n] = buf
    for kind, buf in zip(kinds, _adamw(small_all, pack_rep(w), pack_rep(mom), pack_rep(var), "adamw_replicated")):
        flat = buf.reshape(-1)
        for n, o, s in zip(REPLICATED, rep_offs, rep_sizes):
            outs[kind, n] = flat[o:o + s].reshape(w[n].shape)
    loss = lax.psum(loss_local, MESH_AXES)
    return (loss, grad_x[None], *[outs[kind, n] for kind in ("grad", "delta", "new_m", "new_v") for n in WEIGHTS])
```

```python
import functools
import math

import jax
import jax.numpy as jnp
from jax import lax
from jax.experimental import pallas as pl
from jax.experimental.pallas import tpu as pltpu

F32, BF16 = jnp.float32, jnp.bfloat16
HI = lax.Precision.HIGHEST
LANES = 128
SUBLANES = 8
HEAD = 64
NORM_EPS = 1e-6
GN_EPS = 64e-5
VMEM_LIMIT = 56 * 1024 * 1024
NDEV = 8
PACK_W = 1024
PACK_ALIGN = 16 * PACK_W
MESH_AXES = ("x", "y", "c")
MESH = pl.DeviceIdType.MESH

ADAM_LR, ADAM_B1, ADAM_B2, ADAM_EPS, ADAM_WD, ADAM_STEP = 0.001, 0.9, 0.999, 1e-08, 0.01, 10


def _round_up(n, m):
    return (n + m - 1) // m * m


def _pick(dim, pref, align=LANES):
    if dim <= pref:
        return dim
    best = None
    for cand in range(align, pref + 1, align):
        if dim % cand == 0:
            best = cand
    return dim if best is None else best


def _params(*sem):
    return pltpu.CompilerParams(dimension_semantics=sem, vmem_limit_bytes=VMEM_LIMIT)


def _sigmoid(v):
    return jax.nn.sigmoid(v)


def _log_sigmoid(v):
    return jnp.minimum(v, 0.0) - jnp.log(1.0 + jnp.exp(-jnp.abs(v)))


_DOT_DIMS = {"nn": (((1,), (0,)), ((), ())), "nt": (((1,), (1,)), ((), ())), "tn": (((0,), (0,)), ((), ()))}


def _matmul(a, b, mode, name, out_dtype=F32, bm=512, bn=1024, bk=2048, exchange=None):
    if mode == "tn":
        k_dim, m_dim = a.shape
    else:
        m_dim, k_dim = a.shape
    n_dim = b.shape[0] if mode == "nt" else b.shape[1]
    bm, bn, bk = _pick(m_dim, bm), _pick(n_dim, bn), _pick(k_dim, bk)
    nk = k_dim // bk

    nx = 0 if exchange is None else exchange.n
    grid = (m_dim // bm, n_dim // bn, nk)

    def body(*refs):
        a_ref, b_ref, o_ref = refs[0], refs[1], refs[2 + nx]
        step = [pl.program_id(d) for d in range(3)]
        if nx:
            own_scratch = 1 if nk > 1 else 0
            ex_refs = (refs[2:2 + nx], refs[3 + nx:3 + 2 * nx], refs[3 + 2 * nx + own_scratch:])

            @pl.when((step[0] == 0) & (step[1] == 0) & (step[2] == 0))
            def _():
                exchange.start(*ex_refs)

        prod = lax.dot_general(a_ref[...].astype(BF16), b_ref[...].astype(BF16), _DOT_DIMS[mode],
                               preferred_element_type=F32)
        if nk == 1:
            o_ref[...] = prod.astype(o_ref.dtype)
        else:
            acc_ref, k = refs[3 + 2 * nx], step[2]

            @pl.when(k == 0)
            def _():
                acc_ref[...] = prod

            @pl.when(k > 0)
            def _():
                acc_ref[...] += prod

            @pl.when(k == nk - 1)
            def _():
                o_ref[...] = acc_ref[...].astype(o_ref.dtype)
        if nx:
            @pl.when((step[0] == grid[0] - 1) & (step[1] == grid[1] - 1) & (step[2] == grid[2] - 1))
            def _():
                exchange.finish(*ex_refs)

    if mode == "tn":
        a_spec = pl.BlockSpec((bk, bm), lambda i, j, k: (k, i))
    else:
        a_spec = pl.BlockSpec((bm, bk), lambda i, j, k: (i, k))
    if mode == "nt":
        b_spec = pl.BlockSpec((bn, bk), lambda i, j, k: (j, k))
    else:
        b_spec = pl.BlockSpec((bk, bn), lambda i, j, k: (k, j))
    out_spec = pl.BlockSpec((bm, bn), lambda i, j, k: (i, j))
    out_shape = jax.ShapeDtypeStruct((m_dim, n_dim), out_dtype)
    acc_scratch = [pltpu.VMEM((bm, bn), F32)] if nk > 1 else []
    if exchange is None:
        return pl.pallas_call(
            body, name=name, grid=grid, in_specs=[a_spec, b_spec], out_specs=out_spec, out_shape=out_shape,
            scratch_shapes=acc_scratch, compiler_params=_params("parallel", "parallel", "arbitrary"),
        )(a, b)
    outs = pl.pallas_call(
        body, name=name, grid=grid, in_specs=[a_spec, b_spec] + [HBM_SPEC] * nx,
        out_specs=[out_spec] + [HBM_SPEC] * nx, out_shape=[out_shape] + exchange.out_shape,
        scratch_shapes=acc_scratch + exchange.scratch, compiler_params=_params("arbitrary", "arbitrary", "arbitrary"),
    )(a, b, *exchange.arrays)
    return outs[0], outs[1:]


def _rowcall(body, name, t_dim, bt, row_ins, const_ins, row_outs, acc_outs=(), scratch=(), reverse=False):
    nt = t_dim // bt

    def rmap(i):
        return nt - 1 - i if reverse else i

    in_specs, args = [], []
    for item in row_ins:
        arr, cb, w = item if isinstance(item, tuple) else (item, 0, item.shape[1])
        in_specs.append(pl.BlockSpec((bt, w), lambda i, cb=cb: (rmap(i), cb)))
        args.append(arr)
    for arr in const_ins:
        in_specs.append(pl.BlockSpec(arr.shape, lambda i, nd=arr.ndim: (0,) * nd))
        args.append(arr)
    out_specs = [pl.BlockSpec((bt, w), lambda i: (rmap(i), 0)) for w, _ in row_outs]
    out_shape = [jax.ShapeDtypeStruct((t_dim, w), dt) for w, dt in row_outs]
    for shp in acc_outs:
        out_specs.append(pl.BlockSpec(shp, lambda i, nd=len(shp): (0,) * nd))
        out_shape.append(jax.ShapeDtypeStruct(shp, F32))
    return pl.pallas_call(
        body, name=name, grid=(nt,), in_specs=in_specs, out_specs=out_specs, out_shape=out_shape,
        scratch_shapes=list(scratch), compiler_params=_params("arbitrary"),
    )(*args)


def _first_step_zero(*refs):
    @pl.when(pl.program_id(0) == 0)
    def _():
        for r in refs:
            r[...] = jnp.zeros_like(r)


def _colsum(v):
    return jnp.sum(v, axis=0, keepdims=True)


def _headsum(v, e):
    parts = [jnp.dot(v[:, p * LANES:(p + 1) * LANES], e, precision=HI, preferred_element_type=F32)
             for p in range(v.shape[1] // LANES)]
    return parts[0] if len(parts) == 1 else jnp.concatenate(parts, axis=1)


def _shift_down(v, carry_ref):
    bt = v.shape[0]
    prev = pltpu.roll(v, 1, 0)
    row = lax.broadcasted_iota(jnp.int32, v.shape, 0)
    prev = jnp.where(row == 0, carry_ref[...], prev)
    carry_ref[...] = v[bt - 1:bt, :]
    return prev


def _pair_consts():
    lane = lax.broadcasted_iota(jnp.int32, (1, LANES), 1)
    m0 = (lane < HEAD).astype(F32)
    m1 = 1.0 - m0
    sub = lax.broadcasted_iota(jnp.int32, (HEAD, LANES), 0)
    lane2 = lax.broadcasted_iota(jnp.int32, (HEAD, LANES), 1)
    i0 = (lane2 == sub).astype(F32)
    i1 = (lane2 == sub + HEAD).astype(F32)
    return m0, m1, i0, i1


def _head_masks():
    lane = lax.broadcasted_iota(jnp.int32, (1, LANES), 1)
    first = (lane < HEAD).astype(F32)
    return first, 1.0 - first


def _head_ones(dtype=BF16):
    lane = jnp.arange(LANES)
    return (lane[:, None] // HEAD == lane[None, :] // HEAD).astype(dtype)


def _lanesum(v):
    return jnp.sum(v, axis=1, keepdims=True)


def _split_bf16(v):
    hi = v.astype(BF16).astype(F32)
    rest = v - hi
    mid = rest.astype(BF16).astype(F32)
    return hi, mid, (rest - mid).astype(BF16).astype(F32)


def _col_pair(parts, j, i01, e_bf16):
    lhs = jnp.concatenate([(part[j:j + 1] * i01).astype(BF16) for part in parts], axis=0)
    out = jnp.dot(lhs, e_bf16, preferred_element_type=F32)
    return (out[0:HEAD] + out[HEAD:2 * HEAD]) + out[2 * HEAD:3 * HEAD]


def _col_tiles(out_ref, src_ref, rows, lanes, i01, e_bf16):
    for sl in lanes:
        parts = _split_bf16(src_ref[rows, sl])
        for j in range(SUBLANES):
            out_ref[j, :, sl] = _col_pair(parts, j, i01, e_bf16)


def _row_pair(c0, c1, i0, i1):
    return _colsum(c0 * i0 + c1 * i1)


def _scan_fwd(r, w, k, v, kk, a, tc, npb):
    t_dim, c_dim = r.shape
    wb = LANES * npb
    tc = _pick(t_dim, tc, SUBLANES)

    def body(r_ref, w_ref, k_ref, v_ref, kk_ref, a_ref, e_ref, y_ref, sall_ref, sfin_ref, s_ref):
        @pl.when(pl.program_id(1) == 0)
        def _():
            s_ref[...] = jnp.zeros_like(s_ref)

        m0, m1, i0, i1 = _pair_consts()
        i01 = i0 + i1
        e = e_ref[...]
        sub8 = lax.broadcasted_iota(jnp.int32, (SUBLANES, LANES), 0)
        lanes = [slice(q * LANES, (q + 1) * LANES) for q in range(npb)]
        ng = tc // SUBLANES

        def lanesums(tiles):
            sums = _lanesum(jnp.concatenate(tiles, axis=0))
            return [sums[i * HEAD:(i + 1) * HEAD] for i in range(len(tiles))]

        def halves(s, row):
            return [s * (row * m0), s * (row * m1)]

        def group(gi, carry):
            base = pl.multiple_of(gi * SUBLANES, SUBLANES)
            rows = pl.ds(base, SUBLANES)
            s = list(carry)
            r8 = [r_ref[rows, sl] for sl in lanes]
            w8 = [w_ref[rows, sl] for sl in lanes]
            k8 = [k_ref[rows, sl] for sl in lanes]
            kk8 = [kk_ref[rows, sl] for sl in lanes]
            b8 = [kk8[q] * a_ref[rows, lanes[q]] for q in range(npb)]
            v8 = [_split_bf16(v_ref[rows, sl]) for sl in lanes]
            y8 = [jnp.zeros((SUBLANES, LANES), F32)] * npb
            for j in range(SUBLANES + 1):
                one, before = slice(j, j + 1), slice(j - 1, j)
                tiles = []
                for q in range(npb):
                    if j < SUBLANES:
                        tiles += halves(s[q], kk8[q][one])
                    if j > 0:
                        tiles += halves(s[q], r8[q][before])
                cols = lanesums(tiles)
                per = len(tiles) // npb
                for q in range(npb):
                    mine = cols[q * per:(q + 1) * per]
                    if j > 0:
                        y8[q] = jnp.where(sub8 == j - 1, _row_pair(mine[-2], mine[-1], i0, i1), y8[q])
                    if j < SUBLANES:
                        sall_ref[base + j, :, lanes[q]] = s[q]
                        sb = mine[0] * m0 + mine[1] * m1
                        s[q] = s[q] * w8[q][one] - sb * b8[q][one] + _col_pair(v8[q], j, i01, e) * k8[q][one]
            for q in range(npb):
                y_ref[rows, lanes[q]] = y8[q]
            return tuple(s)

        init = tuple(s_ref[:, q * LANES:(q + 1) * LANES] for q in range(npb))
        fin = lax.fori_loop(0, ng, group, init)
        for q in range(npb):
            s_ref[:, q * LANES:(q + 1) * LANES] = fin[q]
            sfin_ref[:, q * LANES:(q + 1) * LANES] = fin[q]

    row = pl.BlockSpec((tc, wb), lambda p, c: (c, p))
    return pl.pallas_call(
        body, name="rwkv_scan_fwd", grid=(c_dim // wb, t_dim // tc),
        in_specs=[row] * 6 + [pl.BlockSpec((LANES, LANES), lambda p, c: (0, 0))],
        out_specs=[row, pl.BlockSpec((tc, HEAD, wb), lambda p, c: (c, 0, p)), pl.BlockSpec((HEAD, wb), lambda p, c: (0, p))],
        out_shape=[jax.ShapeDtypeStruct((t_dim, c_dim), F32), jax.ShapeDtypeStruct((t_dim, HEAD, c_dim), F32),
                   jax.ShapeDtypeStruct((HEAD, c_dim), F32)],
        scratch_shapes=[pltpu.VMEM((HEAD, wb), F32)],
        compiler_params=_params("parallel", "arbitrary"),
    )(r, w, k, v, kk, a, _head_ones())


def _scan_bwd(r, w, k, v, kk, a, dy, sall, sfin, tc, npb, exchange):
    t_dim, c_dim = r.shape
    wb = LANES * npb
    tc = _pick(t_dim, tc, SUBLANES)
    nc = t_dim // tc
    nx = exchange.n
    n_blocks = c_dim // wb

    def body(*refs):
        r_ref, w_ref, k_ref, v_ref, kk_ref, a_ref, dy_ref, sall_ref, sfin_ref, e_ref = refs[:10]
        dr_ref, dw_ref, dk_ref, dv_ref, dkk_ref, da_ref = refs[10 + nx:16 + nx]
        ds_ref, sn_ref = refs[16 + 2 * nx:18 + 2 * nx]
        here_p, here_c = pl.program_id(0), pl.program_id(1)
        ex_refs = (refs[10:10 + nx], refs[16 + nx:16 + 2 * nx], refs[18 + 2 * nx:])

        @pl.when((here_p == 0) & (here_c == 0))
        def _():
            exchange.start(*ex_refs)

        @pl.when(pl.program_id(1) == 0)
        def _():
            ds_ref[...] = jnp.zeros_like(ds_ref)
            sn_ref[...] = sfin_ref[...]

        m0, m1, i0, i1 = _pair_consts()
        i01 = i0 + i1
        e = e_ref[...]
        sub8 = lax.broadcasted_iota(jnp.int32, (SUBLANES, LANES), 0)
        lanes = [slice(q * LANES, (q + 1) * LANES) for q in range(npb)]
        ng = tc // SUBLANES

        def halves(s, row):
            return [s * (row * m0), s * (row * m1)]

        def group(gi, carry):
            base = pl.multiple_of((ng - 1 - gi) * SUBLANES, SUBLANES)
            rows = pl.ds(base, SUBLANES)
            ds = list(carry)
            r8 = [r_ref[rows, sl] for sl in lanes]
            w8 = [w_ref[rows, sl] for sl in lanes]
            k8 = [k_ref[rows, sl] for sl in lanes]
            kk8 = [kk_ref[rows, sl] for sl in lanes]
            a8 = [a_ref[rows, sl] for sl in lanes]
            v8 = [_split_bf16(v_ref[rows, sl]) for sl in lanes]
            dy8 = [_split_bf16(dy_ref[rows, sl]) for sl in lanes]
            zero8 = jnp.zeros((SUBLANES, LANES), F32)
            dr8, dw8, dk8, dv8, dkk8, da8 = ([zero8] * npb for _ in range(6))
            for j in reversed(range(SUBLANES)):
                one = slice(j, j + 1)
                here = sub8 == j
                d, s_prev, tiles = [], [], []
                for q in range(npb):
                    dyb = _col_pair(dy8[q], j, i01, e)
                    s_prev.append(sall_ref[base + j, :, lanes[q]])
                    dr8[q] = jnp.where(here, _colsum(sn_ref[:, lanes[q]] * dyb), dr8[q])
                    sn_ref[:, lanes[q]] = s_prev[q]
                    d.append(ds[q] + dyb * r8[q][one])
                for q in range(npb):
                    tiles += halves(d[q], kk8[q][one] * a8[q][one])
                for q in range(npb):
                    tiles += halves(d[q], k8[q][one]) + halves(s_prev[q], kk8[q][one])
                sums = _lanesum(jnp.concatenate(tiles, axis=0))
                cols = [sums[i * HEAD:(i + 1) * HEAD] for i in range(len(tiles))]
                for q in range(npb):
                    kkr, ar = kk8[q][one], a8[q][one]
                    dsb = -(cols[2 * q] * m0 + cols[2 * q + 1] * m1)
                    ds[q] = d[q] * w8[q][one] + dsb * kkr
                    rest = cols[2 * npb + 4 * q:2 * npb + 4 * q + 4]
                    dv8[q] = jnp.where(here, _row_pair(rest[0], rest[1], i0, i1), dv8[q])
                    sb = rest[2] * m0 + rest[3] * m1
                    db = -_colsum(d[q] * sb)
                    dk8[q] = jnp.where(here, _colsum(d[q] * _col_pair(v8[q], j, i01, e)), dk8[q])
                    dw8[q] = jnp.where(here, _colsum(d[q] * s_prev[q]), dw8[q])
                    dkk8[q] = jnp.where(here, _colsum(s_prev[q] * dsb) + db * ar, dkk8[q])
                    da8[q] = jnp.where(here, db * kkr, da8[q])
            for q in range(npb):
                sl = lanes[q]
                dr_ref[rows, sl], dw_ref[rows, sl], dk_ref[rows, sl] = dr8[q], dw8[q], dk8[q]
                dv_ref[rows, sl], dkk_ref[rows, sl], da_ref[rows, sl] = dv8[q], dkk8[q], da8[q]
            return tuple(ds)

        init = tuple(ds_ref[:, q * LANES:(q + 1) * LANES] for q in range(npb))
        fin = lax.fori_loop(0, ng, group, init)
        for q in range(npb):
            ds_ref[:, q * LANES:(q + 1) * LANES] = fin[q]

        @pl.when((here_p == n_blocks - 1) & (here_c == nc - 1))
        def _():
            exchange.finish(*ex_refs)

    row = pl.BlockSpec((tc, wb), lambda p, c: (nc - 1 - c, p))
    outs = pl.pallas_call(
        body, name="rwkv_scan_bwd", grid=(n_blocks, nc),
        in_specs=[row] * 7 + [pl.BlockSpec((tc, HEAD, wb), lambda p, c: (nc - 1 - c, 0, p)),
                              pl.BlockSpec((HEAD, wb), lambda p, c: (0, p)),
                              pl.BlockSpec((LANES, LANES), lambda p, c: (0, 0))] + [HBM_SPEC] * nx,
        out_specs=[row] * 6 + [HBM_SPEC] * nx,
        out_shape=[jax.ShapeDtypeStruct((t_dim, c_dim), F32)] * 6 + exchange.out_shape,
        scratch_shapes=[pltpu.VMEM((HEAD, wb), F32)] * 2 + exchange.scratch,
        compiler_params=_params("arbitrary", "arbitrary"),
    )(r, w, k, v, kk, a, dy, sall, sfin, _head_ones(), *exchange.arrays)
    return outs[:6], outs[6:]


def _fox_fwd(z, ct, cq, q_cb, k_cb, v_cb, n_pairs, blk):
    t_dim = z.shape[0]
    blk = _pick(t_dim, blk)
    nq = t_dim // blk
    scale = HEAD ** -0.5

    def body(q_ref, k_ref, v_ref, ct_ref, cq_ref, o_ref, lse_ref):
        i = pl.program_id(1)
        rowi = lax.broadcasted_iota(jnp.int32, (blk, blk), 0)
        coli = lax.broadcasted_iota(jnp.int32, (blk, blk), 1)
        masks = _head_masks()
        qv = q_ref[...]
        qs = [(qv * mk).astype(BF16) for mk in masks]
        cqs = [cq_ref[:, hh:hh + 1] for hh in range(2)]

        def kv_step(j, carry, masked):
            rows = pl.ds(pl.multiple_of(j * blk, blk), blk)
            kb = k_ref[rows, :].astype(BF16)
            vv = v_ref[rows, :]
            stats, acc = list(carry[:4]), carry[4]
            rescale, add = 0.0, 0.0
            for hh in range(2):
                m, l = stats[2 * hh], stats[2 * hh + 1]
                s = (lax.dot_general(qs[hh], kb, _DOT_DIMS["nt"], preferred_element_type=F32) * scale
                     + (cqs[hh] - ct_ref[hh:hh + 1, rows]))
                if masked:
                    s = jnp.where(rowi >= coli, s, -jnp.inf)
                m_new = jnp.maximum(m, jnp.max(s, axis=1, keepdims=True))
                alpha = jnp.exp(m - m_new)
                pr = jnp.exp(s - m_new)
                stats[2 * hh], stats[2 * hh + 1] = m_new, l * alpha + jnp.sum(pr, axis=1, keepdims=True)
                rescale = rescale + alpha * masks[hh]
                hi = pr.astype(BF16)
                both = jnp.concatenate([hi, (pr - hi.astype(F32)).astype(BF16)], axis=1)
                vh = (vv * masks[hh]).astype(BF16)
                add = add + jnp.dot(both, jnp.concatenate([vh, vh], axis=0), preferred_element_type=F32)
            return (*stats, acc * rescale + add)

        neg, zero = jnp.full((blk, 1), -jnp.inf, F32), jnp.zeros((blk, 1), F32)
        carry = lax.fori_loop(0, i, functools.partial(kv_step, masked=False),
                              (neg, zero, neg, zero, jnp.zeros((blk, LANES), F32)))
        m0, l0, m1, l1, acc = kv_step(i, carry, True)
        o_ref[...] = acc * (masks[0] / l0 + masks[1] / l1)
        lse_ref[:, 0:1] = m0 + jnp.log(l0)
        lse_ref[:, 1:2] = m1 + jnp.log(l1)

    full = lambda cb: pl.BlockSpec((t_dim, LANES), lambda p, i, cb=cb: (0, cb + p))
    return pl.pallas_call(
        body, name="fox_attn_fwd", grid=(n_pairs, nq),
        in_specs=[pl.BlockSpec((blk, LANES), lambda p, i: (i, q_cb + p)), full(k_cb), full(v_cb),
                  pl.BlockSpec((None, SUBLANES, t_dim), lambda p, i: (p, 0, 0)),
                  pl.BlockSpec((None, blk, 2), lambda p, i: (p, i, 0))],
        out_specs=[pl.BlockSpec((blk, LANES), lambda p, i: (i, p)), pl.BlockSpec((None, blk, 2), lambda p, i: (p, i, 0))],
        out_shape=[jax.ShapeDtypeStruct((t_dim, n_pairs * LANES), F32), jax.ShapeDtypeStruct((n_pairs, t_dim, 2), F32)],
        compiler_params=_params("parallel", "arbitrary"),
    )(z, z, z, ct, cq)


def _fox_rowdot(z, ct, cq, do, lse, q_cb, k_cb, v_cb, n_pairs, blk):
    t_dim = z.shape[0]
    blk = _pick(t_dim, blk)
    scale = HEAD ** -0.5

    def body(q_ref, k_ref, v_ref, ct_ref, cq_ref, do_ref, lse_ref, out_ref):
        i = pl.program_id(1)
        rowi = lax.broadcasted_iota(jnp.int32, (blk, blk), 0)
        coli = lax.broadcasted_iota(jnp.int32, (blk, blk), 1)
        masks = _head_masks()
        qv, dov = q_ref[...], do_ref[...]
        qs = [(qv * mk).astype(BF16) for mk in masks]
        dos = [(dov * mk).astype(BF16) for mk in masks]
        cqs = [cq_ref[:, hh:hh + 1] for hh in range(2)]
        lses = [lse_ref[:, hh:hh + 1] for hh in range(2)]

        def kv_step(j, carry, masked):
            rows = pl.ds(pl.multiple_of(j * blk, blk), blk)
            kb = k_ref[rows, :].astype(BF16)
            vb = v_ref[rows, :].astype(BF16)
            out = []
            for hh in range(2):
                num, den = carry[2 * hh], carry[2 * hh + 1]
                s = (lax.dot_general(qs[hh], kb, _DOT_DIMS["nt"], preferred_element_type=F32) * scale
                     + (cqs[hh] - ct_ref[hh:hh + 1, rows]))
                pr = jnp.exp(s - lses[hh])
                if masked:
                    pr = jnp.where(rowi >= coli, pr, 0.0)
                dp = lax.dot_general(dos[hh], vb, _DOT_DIMS["nt"], preferred_element_type=F32)
                out += [num + jnp.sum(pr * dp, axis=1, keepdims=True), den + jnp.sum(pr, axis=1, keepdims=True)]
            return tuple(out)

        zero = jnp.zeros((blk, 1), F32)
        carry = lax.fori_loop(0, i, functools.partial(kv_step, masked=False), (zero, zero, zero, zero))
        num0, den0, num1, den1 = kv_step(i, carry, True)
        out_ref[:, 0:1] = num0 / den0
        out_ref[:, 1:2] = num1 / den1

    full = lambda cb: pl.BlockSpec((t_dim, LANES), lambda p, i, cb=cb: (0, cb + p))
    return pl.pallas_call(
        body, name="fox_attn_rowdot", grid=(n_pairs, t_dim // blk),
        in_specs=[pl.BlockSpec((blk, LANES), lambda p, i: (i, q_cb + p)), full(k_cb), full(v_cb),
                  pl.BlockSpec((None, SUBLANES, t_dim), lambda p, i: (p, 0, 0)),
                  pl.BlockSpec((None, blk, 2), lambda p, i: (p, i, 0)),
                  pl.BlockSpec((blk, LANES), lambda p, i: (i, p)),
                  pl.BlockSpec((None, blk, 2), lambda p, i: (p, i, 0))],
        out_specs=pl.BlockSpec((None, blk, 2), lambda p, i: (p, i, 0)),
        out_shape=jax.ShapeDtypeStruct((n_pairs, t_dim, 2), F32),
        compiler_params=_params("parallel", "arbitrary"),
    )(z, z, z, ct, cq, do, lse)


def _fox_bwd(z, ct, cq, rowdot, do, lse, q_cb, k_cb, v_cb, n_pairs, blk):
    t_dim = z.shape[0]
    blk = _pick(t_dim, blk)
    nb = t_dim // blk
    scale = HEAD ** -0.5

    def body(q_ref, k_ref, v_ref, ct_ref, cq_ref, rd_ref, do_ref, lse_ref, dq_ref, dk_ref, dv_ref, dc_ref):
        j = pl.program_id(1)

        @pl.when(j == 0)
        def _():
            dq_ref[...] = jnp.zeros_like(dq_ref)

        rowi = lax.broadcasted_iota(jnp.int32, (blk, blk), 0)
        coli = lax.broadcasted_iota(jnp.int32, (blk, blk), 1)
        krows = pl.ds(pl.multiple_of(j * blk, blk), blk)
        masks = _head_masks()
        kv, vb = k_ref[...], v_ref[...].astype(BF16)
        kb = kv.astype(BF16)
        ks = [(kv * mk).astype(BF16) for mk in masks]
        cks = [ct_ref[hh:hh + 1, krows] for hh in range(2)]

        def q_step(i, carry, masked):
            dk, dv, dcs = carry[0], carry[1], list(carry[2:])
            rows = pl.ds(pl.multiple_of(i * blk, blk), blk)
            qv, dov = q_ref[rows, :], do_ref[rows, :]
            dq = 0.0
            for hh in range(2):
                qh = (qv * masks[hh]).astype(BF16)
                doh = (dov * masks[hh]).astype(BF16)
                s = (lax.dot_general(qh, kb, _DOT_DIMS["nt"], preferred_element_type=F32) * scale
                     + (cq_ref[rows, hh:hh + 1] - cks[hh]))
                pr = jnp.exp(s - lse_ref[rows, hh:hh + 1])
                if masked:
                    pr = jnp.where(rowi >= coli, pr, 0.0)
                dv = dv + lax.dot_general(pr.astype(BF16), doh, _DOT_DIMS["tn"], preferred_element_type=F32)
                dp = lax.dot_general(doh, vb, _DOT_DIMS["nt"], preferred_element_type=F32)
                ds = pr * (dp - rd_ref[rows, hh * HEAD:hh * HEAD + 1])
                dsb = ds.astype(BF16)
                dq = dq + jnp.dot(dsb, ks[hh], preferred_element_type=F32)
                dk = dk + lax.dot_general(dsb, qh, _DOT_DIMS["tn"], preferred_element_type=F32)
                dcs[hh] = dcs[hh] - _colsum(ds)
            dq_ref[rows, :] += dq * scale
            return (dk, dv, *dcs)

        zero_row = jnp.zeros((1, blk), F32)
        init = (jnp.zeros((blk, LANES), F32), jnp.zeros((blk, LANES), F32), zero_row, zero_row)
        carry = q_step(j, init, True)
        dk, dv, dc0, dc1 = lax.fori_loop(j + 1, nb, functools.partial(q_step, masked=False), carry)
        dk_ref[...] = dk * scale
        dv_ref[...] = dv
        dc_ref[0:1, :] = dc0
        dc_ref[1:2, :] = dc1
        dc_ref[2:SUBLANES, :] = jnp.zeros((SUBLANES - 2, blk), F32)

    full = lambda: pl.BlockSpec((t_dim, LANES), lambda p, j: (0, p))
    blkspec = pl.BlockSpec((blk, LANES), lambda p, j: (j, p))
    return pl.pallas_call(
        body, name="fox_attn_bwd", grid=(n_pairs, nb),
        in_specs=[pl.BlockSpec((t_dim, LANES), lambda p, j: (0, q_cb + p)),
                  pl.BlockSpec((blk, LANES), lambda p, j: (j, k_cb + p)),
                  pl.BlockSpec((blk, LANES), lambda p, j: (j, v_cb + p)),
                  pl.BlockSpec((None, SUBLANES, t_dim), lambda p, j: (p, 0, 0)),
                  pl.BlockSpec((None, t_dim, 2), lambda p, j: (p, 0, 0)), full(), full(),
                  pl.BlockSpec((None, t_dim, 2), lambda p, j: (p, 0, 0))],
        out_specs=[full(), blkspec, blkspec, pl.BlockSpec((None, SUBLANES, blk), lambda p, j: (p, 0, j))],
        out_shape=[jax.ShapeDtypeStruct((t_dim, n_pairs * LANES), F32)] * 3
                  + [jax.ShapeDtypeStruct((n_pairs, SUBLANES, t_dim), F32)],
        compiler_params=_params("parallel", "arbitrary"),
    )(z, z, z, ct, cq, rowdot, do, lse)


HBM_SPEC = pl.BlockSpec(memory_space=pltpu.HBM)


def _all_gather(shards, name):
    n = len(shards)

    def body(*refs):
        x_refs, out_refs = refs[:n], refs[n:2 * n]
        send_sems, recv_sems, local_sems = refs[2 * n:]
        x, y, c = lax.axis_index("x"), lax.axis_index("y"), lax.axis_index("c")
        me, sibling = (x, y, c), (x, y, 1 - c)
        chips = [(1 - x, y), (x, 1 - y), (1 - x, 1 - y)]

        def copy(a, k, block, to, from_input=False):
            px, py, pc = block
            slot = out_refs[a].at[4 * px + 2 * py + pc]
            return pltpu.make_async_remote_copy(
                src_ref=x_refs[a] if from_input else slot, dst_ref=slot,
                send_sem=send_sems.at[7 * a + k], recv_sem=recv_sems.at[7 * a + k], device_id=to, device_id_type=MESH)

        mine = [pltpu.make_async_copy(x_refs[a], out_refs[a].at[4 * x + 2 * y + c], local_sems.at[a]) for a in range(n)]
        for cp in mine:
            cp.start()
        first = []
        for a in range(n):
            first += [copy(a, 1 + j, me, (*chip, c), from_input=True) for j, chip in enumerate(chips)]
            first.append(copy(a, 0, me, sibling, from_input=True))
        for cp in first:
            cp.start()
        passed = []
        for a in range(n):
            for j, chip in enumerate(chips):
                copy(a, 1 + j, (*chip, c), me).wait_recv()
                passed.append(copy(a, 4 + j, (*chip, c), sibling))
                passed[-1].start()
        for a in range(n):
            copy(a, 0, sibling, me).wait_recv()
            for j, chip in enumerate(chips):
                copy(a, 4 + j, (*chip, 1 - c), me).wait_recv()
        for cp in first + passed:
            cp.wait_send()
        for cp in mine:
            cp.wait()

    return pl.pallas_call(
        body, name=name, out_shape=[jax.ShapeDtypeStruct((NDEV,) + s.shape, s.dtype) for s in shards],
        in_specs=[HBM_SPEC] * n, out_specs=[HBM_SPEC] * n,
        scratch_shapes=[pltpu.SemaphoreType.DMA((7 * n,)), pltpu.SemaphoreType.DMA((7 * n,)),
                        pltpu.SemaphoreType.DMA((n,))],
    )(*shards)


class _Exchange:
    def __init__(self, arrays, first_dest=None):
        self.arrays = list(arrays)
        self.n = len(self.arrays)
        self.per_dest = [a.ndim == 3 for a in self.arrays]
        self.first = [0] * self.n if first_dest is None else list(first_dest)
        self.out_shape = [jax.ShapeDtypeStruct((NDEV,) + a.shape[pd:], a.dtype)
                          for a, pd in zip(self.arrays, self.per_dest)]
        self.scratch = [pltpu.SemaphoreType.DMA((7 * self.n,)), pltpu.SemaphoreType.DMA((7 * self.n,)),
                        pltpu.SemaphoreType.DMA((self.n,))]

    def _copies(self, in_refs, out_refs, sems):
        send_sems, recv_sems, local_sems = sems
        x, y, c = lax.axis_index("x"), lax.axis_index("y"), lax.axis_index("c")
        me = 4 * x + 2 * y + c
        own, sends, recvs = [], [], []
        for a in range(self.n):
            lo = self.first[a]
            hi = lo + self.arrays[a].shape[0] if self.per_dest[a] else NDEV
            whole = lo == 0 and hi == NDEV

            def takes(dev, lo=lo, hi=hi, whole=whole):
                return None if whole else (dev >= lo) & (dev < hi)

            mine = in_refs[a].at[jnp.clip(me - lo, 0, hi - lo - 1)] if self.per_dest[a] else in_refs[a]
            own.append((pltpu.make_async_copy(mine, out_refs[a].at[me], local_sems.at[a]), takes(me)))
            for k in range(1, NDEV):
                px = 1 - x if k & 4 else x
                py = 1 - y if k & 2 else y
                pc = 1 - c if k & 1 else c
                peer = 4 * px + 2 * py + pc
                sem = dict(send_sem=send_sems.at[7 * a + k - 1], recv_sem=recv_sems.at[7 * a + k - 1],
                           device_id=(px, py, pc), device_id_type=MESH)
                src = in_refs[a].at[jnp.clip(peer - lo, 0, hi - lo - 1)] if self.per_dest[a] else in_refs[a]
                sends.append((pltpu.make_async_remote_copy(src_ref=src, dst_ref=out_refs[a].at[me], **sem), takes(peer)))
                recvs.append((pltpu.make_async_remote_copy(src_ref=src, dst_ref=out_refs[a].at[peer], **sem), takes(me)))
        return own, sends, recvs

    @staticmethod
    def _each(copies, action):
        for cp, cond in copies:
            if cond is None:
                action(cp)
            else:
                pl.when(cond)(functools.partial(action, cp))

    def start(self, in_refs, out_refs, sems):
        own, sends, _ = self._copies(in_refs, out_refs, sems)
        self._each(own + sends, lambda cp: cp.start())

    def finish(self, in_refs, out_refs, sems):
        own, sends, recvs = self._copies(in_refs, out_refs, sems)
        self._each(recvs, lambda cp: cp.wait_recv())
        self._each(sends, lambda cp: cp.wait_send())
        self._each(own, lambda cp: cp.wait())


def _grad_exchange(arrays, name):
    ex = _Exchange(arrays)

    def body(*refs):
        in_refs, out_refs, sems = refs[:ex.n], refs[ex.n:2 * ex.n], refs[2 * ex.n:]
        ex.start(in_refs, out_refs, sems)
        ex.finish(in_refs, out_refs, sems)

    return pl.pallas_call(body, name=name, out_shape=ex.out_shape, in_specs=[HBM_SPEC] * ex.n,
                          out_specs=[HBM_SPEC] * ex.n, scratch_shapes=ex.scratch)(*ex.arrays)


def _adamw(partials, w, m, v, name):
    rows, width = w.shape
    br = _pick(rows, 128, 2 * SUBLANES)

    def body(p_ref, w_ref, m_ref, v_ref, g_out, d_out, m_out, v_out):
        g = p_ref[0].astype(F32)
        for d in range(1, NDEV):
            g = g + p_ref[d].astype(F32)
        mn = ADAM_B1 * m_ref[...] + (1.0 - ADAM_B1) * g
        vn = ADAM_B2 * v_ref[...] + (1.0 - ADAM_B2) * jnp.square(g)
        m_hat = mn / (1.0 - ADAM_B1 ** ADAM_STEP)
        v_hat = vn / (1.0 - ADAM_B2 ** ADAM_STEP)
        g_out[...] = g
        d_out[...] = -ADAM_LR * (m_hat / (jnp.sqrt(v_hat) + ADAM_EPS) + ADAM_WD * w_ref[...])
        m_out[...] = mn
        v_out[...] = vn

    blk = pl.BlockSpec((br, width), lambda i: (i, 0))
    return pl.pallas_call(
        body, name=name, grid=(rows // br,),
        in_specs=[pl.BlockSpec((NDEV, br, width), lambda i: (0, i, 0)), blk, blk, blk],
        out_specs=[pl.BlockSpec((None, br, width), lambda i: (0, i, 0))] * 4,
        out_shape=[jax.ShapeDtypeStruct((1, rows, width), F32)] * 4,
        compiler_params=_params("parallel"),
    )(partials, w, m, v)


def _pack_rows(flat):
    n = flat.shape[0]
    padded = _round_up(n, PACK_ALIGN)
    return jnp.pad(flat, (0, padded - n)).reshape(padded // PACK_W, PACK_W)


def _split_shards(full, axis):
    rows, cols = full.shape
    if axis == 0:
        return full.reshape(NDEV, rows // NDEV, cols)
    width = cols // NDEV
    return jnp.stack([full[:, d * width:(d + 1) * width] for d in range(NDEV)])


def _join_shards(blocks, axis):
    if axis == 0:
        return blocks.reshape(-1, blocks.shape[2])
    return jnp.concatenate([blocks[d] for d in range(NDEV)], axis=1)


SHARDED = (("w_in", 1), ("rw_w_lora_up", 1), ("rw_a_lora_up", 1), ("w_up_rwkv", 1), ("w_up_fox", 1),
           ("w_out", 0), ("ple_proj", 1), ("ple_gate_w", 0))
REPLICATED = ("norm_g", "rw_shift_mu", "rw_w0", "rw_a0", "rw_k_k", "rw_k_a", "rw_r_k", "rw_ln_g", "rw_ln_b",
              "fox_b_f", "ple_norm_g", "final_norm_g")
WEIGHTS = ("norm_g", "w_in", "rw_shift_mu", "rw_w0", "rw_w_lora_up", "rw_a0", "rw_a_lora_up", "rw_k_k", "rw_k_a",
           "rw_r_k", "rw_ln_g", "rw_ln_b", "fox_b_f", "w_up_rwkv", "w_up_fox", "w_out", "ple_proj", "ple_gate_w",
           "ple_norm_g", "final_norm_g")


def _local_step(x, p, tgt, wz, wl, wa, wur, wuf, wo, pp, pg, rep, dims):
    t_dim, d_model, c_rw, lora, c_fox, h_fox, sec = dims
    bt = _pick(t_dim, 256, 2 * SUBLANES)
    bt_many = _pick(t_dim, 128, 2 * SUBLANES)
    n_pairs = c_fox // LANES
    row = lambda a: a.reshape(1, -1)
    norm_g, mu, w0, a0 = row(rep["norm_g"]), row(rep["rw_shift_mu"]), row(rep["rw_w0"]), row(rep["rw_a0"])
    k_k, k_a, r_k = row(rep["rw_k_k"]), row(rep["rw_k_a"]), row(rep["rw_r_k"])
    ln_g, ln_b = row(rep["rw_ln_g"]), row(rep["rw_ln_b"])
    g2, g3 = row(rep["ple_norm_g"]), row(rep["final_norm_g"])
    b_f = jnp.pad(row(rep["fox_b_f"]), ((0, 0), (0, LANES - h_fox)))
    e_head = _head_ones(F32)
    c4 = 4 * c_rw
    inv_d = 1.0 / d_model
    decay_k = math.exp(-0.5)

    def norm_in(x_ref, g_ref, h_ref):
        xv = x_ref[...]
        rms = lax.rsqrt(jnp.mean(xv * xv, axis=-1, keepdims=True) + NORM_EPS)
        h_ref[...] = (xv * rms * g_ref[...]).astype(BF16)

    (h,) = _rowcall(norm_in, "norm_in", t_dim, bt, [x], [norm_g], [(d_model, BF16)])
    z = _matmul(h, wz, "nn", "proj_in", bn=1408)

    def rw_values(zs, w0_ref, wl_ref, a0_ref, wa_ref, kk_ref, ka_ref, e_ref):
        k = zs[:, c_rw:2 * c_rw]
        tw = jnp.tanh(zs[:, c4:c4 + lora])
        al = zs[:, c4 + lora:c4 + 2 * lora]
        sw = _sigmoid(w0_ref[...] + jnp.dot(tw.astype(BF16), wl_ref[...], preferred_element_type=F32))
        decay = jnp.exp(-decay_k * sw)
        a = _sigmoid(a0_ref[...] + jnp.dot(al.astype(BF16), wa_ref[...], preferred_element_type=F32))
        kk0 = k * kk_ref[...]
        nrm = jnp.sqrt(_headsum(kk0 * kk0, e_ref[...]))
        inv = 1.0 / jnp.maximum(nrm, 1e-12)
        k2 = k * (1.0 + (a - 1.0) * ka_ref[...])
        return k, tw, al, sw, decay, a, kk0, nrm, inv, k2

    def rw_prep(z_ref, mu_ref, w0_ref, wl_ref, a0_ref, wa_ref, kk_ref, ka_ref, e_ref,
                r_o, w_o, k_o, v_o, kk_o, a_o, g_o, carry):
        _first_step_zero(carry)
        zv = z_ref[...]
        zs = zv + (_shift_down(zv, carry) - zv) * mu_ref[...]
        k, tw, al, sw, decay, a, kk0, nrm, inv, k2 = rw_values(zs, w0_ref, wl_ref, a0_ref, wa_ref, kk_ref, ka_ref, e_ref)
        r_o[...] = zs[:, 0:c_rw]
        w_o[...] = decay
        k_o[...] = k2
        v_o[...] = zs[:, 2 * c_rw:3 * c_rw]
        kk_o[...] = kk0 * inv
        a_o[...] = a
        g_o[...] = zs[:, 3 * c_rw:c4]

    rw_consts = [mu, w0, wl, a0, wa, k_k, k_a, e_head]
    r_s, w_s, k_s, v_s, kk_s, a_s, g_s = _rowcall(
        rw_prep, "rwkv_prep", t_dim, bt, [(z, 0, sec)], rw_consts, [(c_rw, F32)] * 7,
        scratch=[pltpu.VMEM((1, sec), F32)])
    pairs_fwd = max(n for n in (1, 2, 4) if c_rw % (n * LANES) == 0)
    pairs_bwd = pairs_fwd
    y_s, s_all, s_fin = _scan_fwd(r_s, w_s, k_s, v_s, kk_s, a_s, 64, pairs_fwd)

    def rw_post_values(y, r, k2, v, g, lng_ref, lnb_ref, rk_ref, e):
        mean = _headsum(y, e) * (1.0 / HEAD)
        d = y - mean
        rstd = lax.rsqrt(_headsum(d * d, e) * (1.0 / HEAD) + GN_EPS)
        yh = d * rstd
        rk = _headsum(r * k2 * rk_ref[...], e)
        yo = yh * lng_ref[...] + lnb_ref[...] + rk * v
        sg = _sigmoid(g)
        return rstd, yh, rk, yo, sg

    def rw_post(y_ref, r_ref, k_ref, v_ref, g_ref, lng_ref, lnb_ref, rk_ref, e_ref, out_ref):
        g = g_ref[...]
        _, _, _, yo, sg = rw_post_values(y_ref[...], r_ref[...], k_ref[...], v_ref[...], g, lng_ref, lnb_ref, rk_ref, e_ref[...])
        out_ref[...] = (yo * g * sg).astype(BF16)

    (y_rw,) = _rowcall(rw_post, "rwkv_post", t_dim, bt, [y_s, r_s, k_s, v_s, g_s], [ln_g, ln_b, r_k, e_head], [(c_rw, BF16)])

    fl_cb = (sec + 4 * c_fox) // LANES
    hp = _round_up(h_fox, SUBLANES)
    bt_c = _pick(t_dim, 256)
    tri = (jnp.arange(bt_c)[:, None] >= jnp.arange(bt_c)[None, :]).astype(F32)

    rows8 = n_pairs * SUBLANES
    pair_rows = (jnp.arange(rows8)[:, None] // SUBLANES * 2 + jnp.arange(rows8)[:, None] % SUBLANES
                 == jnp.arange(LANES)[None, :]) & (jnp.arange(rows8)[:, None] % SUBLANES < 2)
    pair_rows = pair_rows.astype(F32)

    def fox_decay(fl_ref, bf_ref, tri_ref, sel_ref, ct_ref, cq_ref, carry):
        _first_step_zero(carry)
        lf = _log_sigmoid(fl_ref[...] + bf_ref[...])
        c = jnp.dot(tri_ref[...], lf, precision=HI, preferred_element_type=F32) + carry[...]
        carry[...] = c[bt_c - 1:bt_c, :]
        ct = jnp.dot(sel_ref[...], jnp.transpose(c), precision=HI, preferred_element_type=F32)
        ct_ref[...] = ct.reshape(n_pairs, SUBLANES, bt_c)
        for pair in range(n_pairs):
            cq_ref[pair] = c[:, 2 * pair:2 * pair + 2]

    ct, cq = pl.pallas_call(
        fox_decay, name="fox_decay", grid=(t_dim // bt_c,),
        in_specs=[pl.BlockSpec((bt_c, LANES), lambda i: (i, fl_cb)), pl.BlockSpec((1, LANES), lambda i: (0, 0)),
                  pl.BlockSpec((bt_c, bt_c), lambda i: (0, 0)), pl.BlockSpec((rows8, LANES), lambda i: (0, 0))],
        out_specs=[pl.BlockSpec((n_pairs, SUBLANES, bt_c), lambda i: (0, 0, i)),
                   pl.BlockSpec((n_pairs, bt_c, 2), lambda i: (0, i, 0))],
        out_shape=[jax.ShapeDtypeStruct((n_pairs, SUBLANES, t_dim), F32), jax.ShapeDtypeStruct((n_pairs, t_dim, 2), F32)],
        scratch_shapes=[pltpu.VMEM((1, LANES), F32)], compiler_params=_params("arbitrary"),
    )(z, b_f, tri, pair_rows)
    q_cb = sec // LANES
    k_cb, v_cb = q_cb + n_pairs, q_cb + 2 * n_pairs
    o_fox, lse = _fox_fwd(z, ct, cq, q_cb, k_cb, v_cb, n_pairs, 256)

    def fox_post(o_ref, z_ref, out_ref):
        g = z_ref[:, 3 * c_fox:4 * c_fox]
        out_ref[...] = (o_ref[...] * g * _sigmoid(g)).astype(BF16)

    (y_fox,) = _rowcall(fox_post, "fox_post", t_dim, bt, [o_fox, (z, 1, sec)], [], [(c_fox, BF16)])

    u_rw = _matmul(y_rw, wur, "nn", "up_rwkv")
    u_fox = _matmul(y_fox, wuf, "nn", "up_fox")

    def merge(ur_ref, uf_ref, z_ref, out_ref):
        s1 = _sigmoid(z_ref[:, 0:d_model])
        s2 = _sigmoid(z_ref[:, d_model:2 * d_model])
        out_ref[...] = (s1 * ur_ref[...] + s2 * uf_ref[...]).astype(BF16)

    (merged,) = _rowcall(merge, "merge", t_dim, bt, [u_rw, u_fox, (z, 2, sec)], [], [(d_model, BF16)])
    mo = _matmul(merged, wo, "nn", "proj_out")

    def resid_norm(x_ref, mo_ref, g_ref, x1_ref, n2_ref):
        x1 = x_ref[...] + mo_ref[...]
        rms = lax.rsqrt(jnp.mean(x1 * x1, axis=-1, keepdims=True) + NORM_EPS)
        x1_ref[...] = x1
        n2_ref[...] = (x1 * rms * g_ref[...]).astype(BF16)

    x1, n2 = _rowcall(resid_norm, "resid_norm", t_dim, bt, [x, mo], [g2], [(d_model, F32), (d_model, BF16)])
    ple = _matmul(p, pp, "nn", "ple_proj")
    gl = _matmul(n2, pg, "nn", "ple_gate")

    def head(x1_ref, ple_ref, gl_ref, tgt_ref, g_ref, dx2_ref, dple_ref, dgl_ref, loss_ref, dg3_ref):
        _first_step_zero(loss_ref, dg3_ref)
        sg = _sigmoid(gl_ref[...])
        pl_v = ple_ref[...]
        x2 = x1_ref[...] + pl_v * sg
        rms = lax.rsqrt(jnp.mean(x2 * x2, axis=-1, keepdims=True) + NORM_EPS)
        xn = x2 * rms
        diff = xn * g_ref[...] - tgt_ref[...]
        loss_ref[...] += 0.5 * jnp.sum(jnp.mean(diff * diff, axis=-1, keepdims=True), axis=0, keepdims=True)
        dyf = diff * inv_d
        dg3_ref[...] += _colsum(dyf * xn)
        gy = dyf * g_ref[...]
        dx2 = rms * (gy - xn * jnp.mean(xn * gy, axis=-1, keepdims=True))
        dx2_ref[...] = dx2
        dple_ref[...] = (dx2 * sg).astype(BF16)
        dgl_ref[...] = (dx2 * pl_v * sg * (1.0 - sg)).astype(BF16)

    dx2, dple, dgl, loss, d_g3 = _rowcall(
        head, "head", t_dim, bt, [x1, ple, gl, tgt], [g3], [(d_model, F32), (d_model, BF16), (d_model, BF16)],
        acc_outs=[(1, 1), (1, d_model)])

    d_pp = _matmul(p, dple, "tn", "d_ple_proj", out_dtype=BF16)
    d_pg = _matmul(n2, dgl, "tn", "d_ple_gate", out_dtype=BF16)
    dn2 = _matmul(dgl, pg, "nt", "d_n2")

    def resid_norm_bwd(dx2_ref, dn2_ref, x1_ref, g_ref, dx1_ref, dx1b_ref, dg2_ref):
        _first_step_zero(dg2_ref)
        x1 = x1_ref[...]
        rms = lax.rsqrt(jnp.mean(x1 * x1, axis=-1, keepdims=True) + NORM_EPS)
        xn = x1 * rms
        dn = dn2_ref[...]
        dg2_ref[...] += _colsum(dn * xn)
        gy = dn * g_ref[...]
        dx1 = dx2_ref[...] + rms * (gy - xn * jnp.mean(xn * gy, axis=-1, keepdims=True))
        dx1_ref[...] = dx1
        dx1b_ref[...] = dx1.astype(BF16)

    dx1, dx1b, d_g2 = _rowcall(resid_norm_bwd, "resid_norm_bwd", t_dim, bt, [dx2, dn2, x1], [g2],
                               [(d_model, F32), (d_model, BF16)], acc_outs=[(1, d_model)])
    d_wo = _matmul(merged, dx1b, "tn", "d_w_out", out_dtype=BF16)
    dmerged = _matmul(dx1b, wo, "nt", "d_merged")

    def merge_bwd(dm_ref, ur_ref, uf_ref, z_ref, dur_ref, duf_ref, dzg_ref):
        dm = dm_ref[...]
        s1 = _sigmoid(z_ref[:, 0:d_model])
        s2 = _sigmoid(z_ref[:, d_model:2 * d_model])
        dur_ref[...] = (dm * s1).astype(BF16)
        duf_ref[...] = (dm * s2).astype(BF16)
        dzg_ref[:, 0:d_model] = (dm * ur_ref[...] * s1 * (1.0 - s1)).astype(BF16)
        dzg_ref[:, d_model:2 * d_model] = (dm * uf_ref[...] * s2 * (1.0 - s2)).astype(BF16)
        if sec > 2 * d_model:
            dzg_ref[:, 2 * d_model:sec] = jnp.zeros((dm.shape[0], sec - 2 * d_model), BF16)

    du_rw, du_fox, dz_gate = _rowcall(merge_bwd, "merge_bwd", t_dim, bt, [dmerged, u_rw, u_fox, (z, 2, sec)], [],
                                      [(d_model, BF16), (d_model, BF16), (sec, BF16)])
    d_wur = _matmul(y_rw, du_rw, "tn", "d_w_up_rwkv", out_dtype=BF16)
    d_wuf = _matmul(y_fox, du_fox, "tn", "d_w_up_fox", out_dtype=BF16)
    dy_rw = _matmul(du_rw, wur, "nt", "d_y_rwkv")
    dy_fox = _matmul(du_fox, wuf, "nt", "d_y_fox")

    def fox_post_bwd(dy_ref, o_ref, z_ref, e_ref, do_ref, dg_ref, rd_ref):
        g = z_ref[:, 3 * c_fox:4 * c_fox]
        sg = _sigmoid(g)
        dy, o = dy_ref[...], o_ref[...]
        do = dy * g * sg
        do_ref[...] = do
        dg_ref[...] = (dy * o * sg * (1.0 + g * (1.0 - sg))).astype(BF16)
        rd_ref[...] = _headsum(do.astype(BF16).astype(F32) * o, e_ref[...])

    do_fox, dg_fox, rowdot = _rowcall(fox_post_bwd, "fox_post_bwd", t_dim, bt, [dy_fox, o_fox, (z, 1, sec)], [e_head],
                                      [(c_fox, F32), (c_fox, BF16), (c_fox, F32)])
    dq_f, dk_f, dv_f, dc_t = _fox_bwd(z, ct, cq, rowdot, do_fox, lse, q_cb, k_cb, v_cb, n_pairs, 256)
    sel = (jnp.arange(hp)[:, None] // 2 * SUBLANES + jnp.arange(hp)[:, None] % 2 == jnp.arange(rows8)[None, :]).astype(F32)
    tri_rev = (jnp.arange(bt_c)[:, None] >= jnp.arange(bt_c)[None, :]).astype(F32)
    bf_col = b_f.reshape(LANES, 1)[0:hp]
    nbc = t_dim // bt_c

    def fox_decay_bwd(dc_ref, fl_ref, sel_ref, tri_ref, bf_ref, dfl_ref, dbf_ref, carry):
        _first_step_zero(carry, dbf_ref)
        dc = jnp.dot(sel_ref[...], dc_ref[...].reshape(rows8, bt_c), precision=HI, preferred_element_type=F32)
        dlf = jnp.dot(dc, tri_ref[...], precision=HI, preferred_element_type=F32) + carry[...]
        carry[...] = dlf[:, 0:1]
        flt = jnp.transpose(fl_ref[...])[0:hp, :]
        dfl = dlf * _sigmoid(-(flt + bf_ref[...]))
        head_row = lax.broadcasted_iota(jnp.int32, (hp, bt_c), 0)
        dfl = jnp.where(head_row < h_fox, dfl, 0.0)
        dbf_ref[...] += jnp.sum(dfl, axis=1, keepdims=True)
        full = jnp.concatenate([dfl, jnp.zeros((LANES - hp, bt_c), F32)], axis=0) if hp < LANES else dfl
        dfl_ref[...] = jnp.transpose(full).astype(BF16)

    dz_fl, d_bf = pl.pallas_call(
        fox_decay_bwd, name="fox_decay_bwd", grid=(nbc,),
        in_specs=[pl.BlockSpec((n_pairs, SUBLANES, bt_c), lambda i: (0, 0, nbc - 1 - i)),
                  pl.BlockSpec((bt_c, LANES), lambda i: (nbc - 1 - i, fl_cb)),
                  pl.BlockSpec(sel.shape, lambda i: (0, 0)), pl.BlockSpec((bt_c, bt_c), lambda i: (0, 0)),
                  pl.BlockSpec((hp, 1), lambda i: (0, 0))],
        out_specs=[pl.BlockSpec((bt_c, LANES), lambda i: (nbc - 1 - i, 0)), pl.BlockSpec((hp, 1), lambda i: (0, 0))],
        out_shape=[jax.ShapeDtypeStruct((t_dim, LANES), BF16), jax.ShapeDtypeStruct((hp, 1), F32)],
        scratch_shapes=[pltpu.VMEM((hp, 1), F32)], compiler_params=_params("arbitrary"),
    )(dc_t, z, sel, tri_rev, bf_col)

    def rw_post_bwd(dy_ref, y_ref, r_ref, k_ref, v_ref, g_ref, lng_ref, lnb_ref, rk_ref, e_ref,
                    dg_ref, dys_ref, dr_ref, dk_ref, dv_ref, dlng_ref, dlnb_ref, drk_ref):
        _first_step_zero(dlng_ref, dlnb_ref, drk_ref)
        e = e_ref[...]
        dy, r, k2, v, g = dy_ref[...], r_ref[...], k_ref[...], v_ref[...], g_ref[...]
        rstd, yh, rk, yo, sg = rw_post_values(y_ref[...], r, k2, v, g, lng_ref, lnb_ref, rk_ref, e)
        dg_ref[...] = dy * yo * sg * (1.0 + g * (1.0 - sg))
        dyo = dy * g * sg
        dlnb_ref[...] += _colsum(dyo)
        dlng_ref[...] += _colsum(dyo * yh)
        dyh = dyo * lng_ref[...]
        dys_ref[...] = rstd * (dyh - _headsum(dyh, e) * (1.0 / HEAD) - yh * _headsum(dyh * yh, e) * (1.0 / HEAD))
        drk = _headsum(dyo * v, e)
        dv_ref[...] = dyo * rk
        dr_ref[...] = drk * k2 * rk_ref[...]
        dk_ref[...] = drk * r * rk_ref[...]
        drk_ref[...] += _colsum(drk * r * k2)

    dg_rw, dy_s, dr_b, dk_b, dv_b, d_lng, d_lnb, d_rk = _rowcall(
        rw_post_bwd, "rwkv_post_bwd", t_dim, bt, [dy_rw, y_s, r_s, k_s, v_s, g_s], [ln_g, ln_b, r_k, e_head],
        [(c_rw, F32)] * 5, acc_outs=[(1, c_rw)] * 3)
    axis = dict(SHARDED)
    rw_cols, fox_cols, gate_cols = c4 + 2 * lora, 4 * c_fox + h_fox, 2 * d_model
    shard_w = (rw_cols + fox_cols + gate_cols) // NDEV
    first_early = -(-rw_cols // shard_w)
    fox_parts = [dq_f.astype(BF16), dk_f.astype(BF16), dv_f.astype(BF16), dg_fox, dz_fl]
    if sec > 4 * c_fox + LANES:
        fox_parts.append(jnp.zeros((t_dim, sec - 4 * c_fox - LANES), BF16))
    dz_fg = jnp.concatenate(fox_parts + [dz_gate], axis=1)
    d_fg = _matmul(h, dz_fg, "tn", "d_w_in_fox_gate", out_dtype=BF16, bn=1408)
    d_fg = jnp.concatenate([d_fg[:, :fox_cols], d_fg[:, sec:sec + gate_cols]], axis=1)
    skip = first_early * shard_w - rw_cols
    d_wi_early = jnp.stack([d_fg[:, skip + i * shard_w:skip + (i + 1) * shard_w] for i in range(NDEV - first_early)])
    early = {"w_up_rwkv": d_wur, "w_up_fox": d_wuf, "w_out": d_wo, "ple_proj": d_pp, "ple_gate_w": d_pg}
    (dr_c, dw_c, dk_c, dv_c, dkk_c, da_c), early_recv = _scan_bwd(
        r_s, w_s, k_s, v_s, kk_s, a_s, dy_s, s_all, s_fin, 64, pairs_bwd,
        _Exchange([_split_shards(g, axis[n]) for n, g in early.items()] + [d_wi_early],
                  [0] * len(early) + [first_early]))
    recv = dict(zip(early, early_recv))
    recv_wi_early = early_recv[-1]

    def rw_prep_bwd(z_ref, dr1, dr2, dw_ref, dk1, dk2_ref, dv1, dv2, dkk_ref, da_ref, dg_ref,
                    mu_ref, w0_ref, wl_ref, a0_ref, wa_ref, kk_ref, ka_ref, e_ref,
                    dzs_ref, tw_ref, al_ref, dwr_ref, dar_ref, dmu_ref, dw0_ref, da0_ref, dkk_acc, dka_acc, carry):
        _first_step_zero(carry, dmu_ref, dw0_ref, da0_ref, dkk_acc, dka_acc)
        e = e_ref[...]
        zv = z_ref[...]
        zp = _shift_down(zv, carry)
        zs = zv + (zp - zv) * mu_ref[...]
        k, tw, al, sw, decay, a, kk0, nrm, inv, k2 = rw_values(zs, w0_ref, wl_ref, a0_ref, wa_ref, kk_ref, ka_ref, e_ref)
        dk2 = dk1[...] + dk2_ref[...]
        da = da_ref[...] + dk2 * k * ka_ref[...]
        dk = dk2 * (1.0 + (a - 1.0) * ka_ref[...])
        dka_acc[...] += _colsum(dk2 * k * (a - 1.0))
        kk = kk0 * inv
        dkk = dkk_ref[...]
        dkk0 = inv * jnp.where(nrm > 1e-12, dkk - kk * _headsum(dkk * kk, e), dkk)
        dk = dk + dkk0 * kk_ref[...]
        dkk_acc[...] += _colsum(dkk0 * k)
        da_raw = da * a * (1.0 - a)
        da0_ref[...] += _colsum(da_raw)
        dw_raw = dw_ref[...] * decay * (-decay_k) * sw * (1.0 - sw)
        dw0_ref[...] += _colsum(dw_raw)
        dar_b, dwr_b = da_raw.astype(BF16), dw_raw.astype(BF16)
        dal = lax.dot_general(dar_b, wa_ref[...], _DOT_DIMS["nt"], preferred_element_type=F32)
        dtw = lax.dot_general(dwr_b, wl_ref[...], _DOT_DIMS["nt"], preferred_element_type=F32)
        dzs_ref[:, 0:c_rw] = dr1[...] + dr2[...]
        dzs_ref[:, c_rw:2 * c_rw] = dk
        dzs_ref[:, 2 * c_rw:3 * c_rw] = dv1[...] + dv2[...]
        dzs_ref[:, 3 * c_rw:c4] = dg_ref[...]
        dzs_ref[:, c4:c4 + lora] = dtw * (1.0 - tw * tw)
        dzs_ref[:, c4 + lora:c4 + 2 * lora] = dal
        if sec > c4 + 2 * lora:
            dzs_ref[:, c4 + 2 * lora:sec] = jnp.zeros((zv.shape[0], sec - c4 - 2 * lora), F32)
        tw_ref[...] = tw.astype(BF16)
        al_ref[...] = al.astype(BF16)
        dwr_ref[...] = dwr_b
        dar_ref[...] = dar_b
        dmu_ref[...] += _colsum(dzs_ref[...] * (zp - zv))

    dzs, tw_b, al_b, dwr_b, dar_b, d_mu, d_w0, d_a0, d_kk, d_ka = _rowcall(
        rw_prep_bwd, "rwkv_prep_bwd", t_dim, bt_many,
        [(z, 0, sec), dr_c, dr_b, dw_c, dk_c, dk_b, dv_c, dv_b, dkk_c, da_c, dg_rw], rw_consts,
        [(sec, F32), (lora, BF16), (lora, BF16), (c_rw, BF16), (c_rw, BF16)],
        acc_outs=[(1, sec), (1, c_rw), (1, c_rw), (1, c_rw), (1, c_rw)], scratch=[pltpu.VMEM((1, sec), F32)])
    d_wl = _matmul(tw_b, dwr_b, "tn", "d_w_lora", out_dtype=BF16)
    d_wa = _matmul(al_b, dar_b, "tn", "d_a_lora", out_dtype=BF16)

    def shift_bwd(dzs_ref, mu_ref, dz_ref, carry):
        _first_step_zero(carry)
        d = dzs_ref[...]
        nbt = d.shape[0]
        nxt = pltpu.roll(d, nbt - 1, 0)
        rowi = lax.broadcasted_iota(jnp.int32, d.shape, 0)
        nxt = jnp.where(rowi == nbt - 1, carry[...], nxt)
        carry[...] = d[0:1, :]
        m = mu_ref[...]
        dz_ref[...] = (d * (1.0 - m) + nxt * m).astype(BF16)

    (dz_rw,) = _rowcall(shift_bwd, "shift_bwd", t_dim, bt, [dzs], [mu], [(sec, BF16)],
                        scratch=[pltpu.VMEM((1, sec), F32)], reverse=True)

    dz = jnp.concatenate([dz_rw, dz_fg], axis=1)
    d_rw = _matmul(h, dz_rw, "tn", "d_w_in_rwkv", out_dtype=BF16, bn=1408)
    d_late = jnp.concatenate([d_rw[:, :rw_cols], d_fg[:, :skip]], axis=1)
    d_wi_late = jnp.stack([d_late[:, i * shard_w:(i + 1) * shard_w] for i in range(first_early)])
    late = {"rw_w_lora_up": d_wl, "rw_a_lora_up": d_wa}
    dh, late_recv = _matmul(dz, wz, "nt", "d_h", bk=sec,
                            exchange=_Exchange([_split_shards(g, axis[n]) for n, g in late.items()] + [d_wi_late]))
    recv.update(zip(late, late_recv))
    me = 4 * lax.axis_index("x") + 2 * lax.axis_index("y") + lax.axis_index("c")
    recv["w_in"] = jnp.where(me >= first_early, recv_wi_early, late_recv[-1])

    def norm_in_bwd(dh_ref, x_ref, dx1_ref, g_ref, dx_ref, dg1_ref):
        _first_step_zero(dg1_ref)
        xv = x_ref[...]
        rms = lax.rsqrt(jnp.mean(xv * xv, axis=-1, keepdims=True) + NORM_EPS)
        xn = xv * rms
        d = dh_ref[...]
        dg1_ref[...] += _colsum(d * xn)
        gy = d * g_ref[...]
        dx_ref[...] = dx1_ref[...] + rms * (gy - xn * jnp.mean(xn * gy, axis=-1, keepdims=True))

    dx, d_g1 = _rowcall(norm_in_bwd, "norm_in_bwd", t_dim, bt, [dh, x, dx1], [norm_g], [(d_model, F32)],
                        acc_outs=[(1, d_model)])

    rep_grads = {"norm_g": d_g1, "rw_shift_mu": d_mu[:, 0:c4 + 2 * lora], "rw_w0": d_w0, "rw_a0": d_a0, "rw_k_k": d_kk,
                 "rw_k_a": d_ka, "rw_r_k": d_rk, "rw_ln_g": d_lng, "rw_ln_b": d_lnb, "fox_b_f": d_bf[0:h_fox, 0],
                 "ple_norm_g": d_g2, "final_norm_g": d_g3}
    return loss[0, 0], dx, recv, rep_grads


def kernel(x, p, norm_g, w_in, rw_shift_mu, rw_w0, rw_w_lora_up, rw_a0, rw_a_lora_up, rw_k_k, rw_k_a, rw_r_k, rw_ln_g, rw_ln_b, fox_b_f, w_up_rwkv, w_up_fox, w_out, ple_proj, ple_gate_w, ple_norm_g, final_norm_g, loss_target, m_norm_g, m_w_in, m_rw_shift_mu, m_rw_w0, m_rw_w_lora_up, m_rw_a0, m_rw_a_lora_up, m_rw_k_k, m_rw_k_a, m_rw_r_k, m_rw_ln_g, m_rw_ln_b, m_fox_b_f, m_w_up_rwkv, m_w_up_fox, m_w_out, m_ple_proj, m_ple_gate_w, m_ple_norm_g, m_final_norm_g, v_norm_g, v_w_in, v_rw_shift_mu, v_rw_w0, v_rw_w_lora_up, v_rw_a0, v_rw_a_lora_up, v_rw_k_k, v_rw_k_a, v_rw_r_k, v_rw_ln_g, v_rw_ln_b, v_fox_b_f, v_w_up_rwkv, v_w_up_fox, v_w_out, v_ple_proj, v_ple_gate_w, v_ple_norm_g, v_final_norm_g):
    args = locals()
    w = {n: args[n] for n in WEIGHTS}
    mom = {n: args["m_" + n] for n in WEIGHTS}
    var = {n: args["v_" + n] for n in WEIGHTS}

    t_dim, d_model = x.shape[1], x.shape[2]
    c_rw, lora = rw_w0.shape[1], rw_w_lora_up.shape[1]
    h_fox = fox_b_f.shape[1]
    c_fox = h_fox * HEAD
    rw_cols, fox_cols, gate_cols = 4 * c_rw + 2 * lora, 4 * c_fox + h_fox, 2 * d_model
    sec = max(rw_cols, 4 * c_fox + LANES, _round_up(gate_cols, LANES))
    assert c_rw % LANES == 0 and c_fox % LANES == 0 and rw_cols % LANES == 0 and h_fox <= LANES and d_model % LANES == 0
    assert rw_a_lora_up.shape[1] == lora and w_in.shape[2] * NDEV == rw_cols + fox_cols + gate_cols

    gathered = _all_gather([w[n][0].astype(BF16) for n, _ in SHARDED], "gather_weights")
    full = {n: _join_shards(g, ax) for (n, ax), g in zip(SHARDED, gathered)}

    def to_sections(wi):
        pad = lambda a, width: jnp.pad(a, ((0, 0), (0, width - a.shape[1])))
        return jnp.concatenate([pad(wi[:, :rw_cols], sec), pad(wi[:, rw_cols:rw_cols + fox_cols], sec),
                                pad(wi[:, rw_cols + fox_cols:], sec)], axis=1)

    rep = {n: w[n] for n in REPLICATED}
    dims = (t_dim, d_model, c_rw, lora, c_fox, h_fox, sec)
    loss_local, grad_x, recv, rep_grads = _local_step(
        x[0], p[0, 0], loss_target[0], to_sections(full["w_in"]), full["rw_w_lora_up"], full["rw_a_lora_up"],
        full["w_up_rwkv"], full["w_up_fox"], full["w_out"], full["ple_proj"], full["ple_gate_w"], rep, dims)

    rep_sizes = [w[n].size for n in REPLICATED]
    rep_offs = [sum(rep_sizes[:i]) for i in range(len(rep_sizes))]
    pack_rep = lambda tree: _pack_rows(jnp.concatenate([tree[n].astype(F32).reshape(-1) for n in REPLICATED]))
    (small_all,) = _grad_exchange([pack_rep(rep_grads)], "exchange_replicated")
    kinds = ("grad", "delta", "new_m", "new_v")
    outs = {}
    for n, _ in SHARDED:
        for kind, buf in zip(kinds, _adamw(recv[n], w[n][0], mom[n][0], var[n][0], "adamw_" + n)):
            outs[kind, n] = buf
    for kind, buf in zip(kinds, _adamw(small_all, pack_rep(w), pack_rep(mom), pack_rep(var), "adamw_replicated")):
        flat = buf.reshape(-1)
        for n, o, s in zip(REPLICATED, rep_offs, rep_sizes):
            outs[kind, n] = flat[o:o + s].reshape(w[n].shape)
    loss = lax.psum(loss_local, MESH_AXES)
    return (loss, grad_x[None], *[outs[kind, n] for kind in ("grad", "delta", "new_m", "new_v") for n in WEIGHTS])
```

```python
import functools
import math

import jax
import jax.numpy as jnp
from jax import lax
from jax.experimental import pallas as pl
from jax.experimental.pallas import tpu as pltpu

F32, BF16 = jnp.float32, jnp.bfloat16
HI = lax.Precision.HIGHEST
LANES = 128
SUBLANES = 8
HEAD = 64
NORM_EPS = 1e-6
GN_EPS = 64e-5
VMEM_LIMIT = 56 * 1024 * 1024
NDEV = 8
PACK_W = 1024
PACK_ALIGN = 16 * PACK_W
MESH_AXES = ("x", "y", "c")
MESH = pl.DeviceIdType.MESH

ADAM_LR, ADAM_B1, ADAM_B2, ADAM_EPS, ADAM_WD, ADAM_STEP = 0.001, 0.9, 0.999, 1e-08, 0.01, 10


def _round_up(n, m):
    return (n + m - 1) // m * m


def _pick(dim, pref, align=LANES):
    if dim <= pref:
        return dim
    best = None
    for cand in range(align, pref + 1, align):
        if dim % cand == 0:
            best = cand
    return dim if best is None else best


def _params(*sem):
    return pltpu.CompilerParams(dimension_semantics=sem, vmem_limit_bytes=VMEM_LIMIT)


def _sigmoid(v):
    return jax.nn.sigmoid(v)


def _log_sigmoid(v):
    return jnp.minimum(v, 0.0) - jnp.log(1.0 + jnp.exp(-jnp.abs(v)))


_DOT_DIMS = {"nn": (((1,), (0,)), ((), ())), "nt": (((1,), (1,)), ((), ())), "tn": (((0,), (0,)), ((), ()))}


def _matmul(a, b, mode, name, out_dtype=F32, bm=512, bn=1024, bk=2048, exchange=None):
    if mode == "tn":
        k_dim, m_dim = a.shape
    else:
        m_dim, k_dim = a.shape
    n_dim = b.shape[0] if mode == "nt" else b.shape[1]
    bm, bn, bk = _pick(m_dim, bm), _pick(n_dim, bn), _pick(k_dim, bk)
    nk = k_dim // bk

    nx = 0 if exchange is None else exchange.n
    grid = (m_dim // bm, n_dim // bn, nk)

    def body(*refs):
        a_ref, b_ref, o_ref = refs[0], refs[1], refs[2 + nx]
        step = [pl.program_id(d) for d in range(3)]
        if nx:
            own_scratch = 1 if nk > 1 else 0
            ex_refs = (refs[2:2 + nx], refs[3 + nx:3 + 2 * nx], refs[3 + 2 * nx + own_scratch:])

            @pl.when((step[0] == 0) & (step[1] == 0) & (step[2] == 0))
            def _():
                exchange.start(*ex_refs)

        prod = lax.dot_general(a_ref[...].astype(BF16), b_ref[...].astype(BF16), _DOT_DIMS[mode],
                               preferred_element_type=F32)
        if nk == 1:
            o_ref[...] = prod.astype(o_ref.dtype)
        else:
            acc_ref, k = refs[3 + 2 * nx], step[2]

            @pl.when(k == 0)
            def _():
                acc_ref[...] = prod

            @pl.when(k > 0)
            def _():
                acc_ref[...] += prod

            @pl.when(k == nk - 1)
            def _():
                o_ref[...] = acc_ref[...].astype(o_ref.dtype)
        if nx:
            @pl.when((step[0] == grid[0] - 1) & (step[1] == grid[1] - 1) & (step[2] == grid[2] - 1))
            def _():
                exchange.finish(*ex_refs)

    if mode == "tn":
        a_spec = pl.BlockSpec((bk, bm), lambda i, j, k: (k, i))
    else:
        a_spec = pl.BlockSpec((bm, bk), lambda i, j, k: (i, k))
    if mode == "nt":
        b_spec = pl.BlockSpec((bn, bk), lambda i, j, k: (j, k))
    else:
        b_spec = pl.BlockSpec((bk, bn), lambda i, j, k: (k, j))
    out_spec = pl.BlockSpec((bm, bn), lambda i, j, k: (i, j))
    out_shape = jax.ShapeDtypeStruct((m_dim, n_dim), out_dtype)
    acc_scratch = [pltpu.VMEM((bm, bn), F32)] if nk > 1 else []
    if exchange is None:
        return pl.pallas_call(
            body, name=name, grid=grid, in_specs=[a_spec, b_spec], out_specs=out_spec, out_shape=out_shape,
            scratch_shapes=acc_scratch, compiler_params=_params("parallel", "parallel", "arbitrary"),
        )(a, b)
    outs = pl.pallas_call(
        body, name=name, grid=grid, in_specs=[a_spec, b_spec] + [HBM_SPEC] * nx,
        out_specs=[out_spec] + [HBM_SPEC] * nx, out_shape=[out_shape] + exchange.out_shape,
        scratch_shapes=acc_scratch + exchange.scratch, compiler_params=_params("arbitrary", "arbitrary", "arbitrary"),
    )(a, b, *exchange.arrays)
    return outs[0], outs[1:]


def _rowcall(body, name, t_dim, bt, row_ins, const_ins, row_outs, acc_outs=(), scratch=(), reverse=False):
    nt = t_dim // bt

    def rmap(i):
        return nt - 1 - i if reverse else i

    in_specs, args = [], []
    for item in row_ins:
        arr, cb, w = item if isinstance(item, tuple) else (item, 0, item.shape[1])
        in_specs.append(pl.BlockSpec((bt, w), lambda i, cb=cb: (rmap(i), cb)))
        args.append(arr)
    for arr in const_ins:
        in_specs.append(pl.BlockSpec(arr.shape, lambda i, nd=arr.ndim: (0,) * nd))
        args.append(arr)
    out_specs = [pl.BlockSpec((bt, w), lambda i: (rmap(i), 0)) for w, _ in row_outs]
    out_shape = [jax.ShapeDtypeStruct((t_dim, w), dt) for w, dt in row_outs]
    for shp in acc_outs:
        out_specs.append(pl.BlockSpec(shp, lambda i, nd=len(shp): (0,) * nd))
        out_shape.append(jax.ShapeDtypeStruct(shp, F32))
    return pl.pallas_call(
        body, name=name, grid=(nt,), in_specs=in_specs, out_specs=out_specs, out_shape=out_shape,
        scratch_shapes=list(scratch), compiler_params=_params("arbitrary"),
    )(*args)


def _first_step_zero(*refs):
    @pl.when(pl.program_id(0) == 0)
    def _():
        for r in refs:
            r[...] = jnp.zeros_like(r)


def _colsum(v):
    return jnp.sum(v, axis=0, keepdims=True)


def _headsum(v, e):
    parts = [jnp.dot(v[:, p * LANES:(p + 1) * LANES], e, precision=HI, preferred_element_type=F32)
             for p in range(v.shape[1] // LANES)]
    return parts[0] if len(parts) == 1 else jnp.concatenate(parts, axis=1)


def _shift_down(v, carry_ref):
    bt = v.shape[0]
    prev = pltpu.roll(v, 1, 0)
    row = lax.broadcasted_iota(jnp.int32, v.shape, 0)
    prev = jnp.where(row == 0, carry_ref[...], prev)
    carry_ref[...] = v[bt - 1:bt, :]
    return prev


def _pair_consts():
    lane = lax.broadcasted_iota(jnp.int32, (1, LANES), 1)
    m0 = (lane < HEAD).astype(F32)
    m1 = 1.0 - m0
    sub = lax.broadcasted_iota(jnp.int32, (HEAD, LANES), 0)
    lane2 = lax.broadcasted_iota(jnp.int32, (HEAD, LANES), 1)
    i0 = (lane2 == sub).astype(F32)
    i1 = (lane2 == sub + HEAD).astype(F32)
    return m0, m1, i0, i1


def _head_masks():
    lane = lax.broadcasted_iota(jnp.int32, (1, LANES), 1)
    first = (lane < HEAD).astype(F32)
    return first, 1.0 - first


def _head_ones(dtype=BF16):
    lane = jnp.arange(LANES)
    return (lane[:, None] // HEAD == lane[None, :] // HEAD).astype(dtype)


def _lanesum(v):
    return jnp.sum(v, axis=1, keepdims=True)


def _split_bf16(v):
    hi = v.astype(BF16).astype(F32)
    rest = v - hi
    mid = rest.astype(BF16).astype(F32)
    return hi, mid, (rest - mid).astype(BF16).astype(F32)


def _col_pair(parts, j, i01, e_bf16):
    lhs = jnp.concatenate([(part[j:j + 1] * i01).astype(BF16) for part in parts], axis=0)
    out = jnp.dot(lhs, e_bf16, preferred_element_type=F32)
    return (out[0:HEAD] + out[HEAD:2 * HEAD]) + out[2 * HEAD:3 * HEAD]


def _col_tiles(out_ref, src_ref, rows, lanes, i01, e_bf16):
    for sl in lanes:
        parts = _split_bf16(src_ref[rows, sl])
        for j in range(SUBLANES):
            out_ref[j, :, sl] = _col_pair(parts, j, i01, e_bf16)


def _row_pair(c0, c1, i0, i1):
    return _colsum(c0 * i0 + c1 * i1)


def _scan_fwd(r, w, k, v, kk, a, tc, npb):
    t_dim, c_dim = r.shape
    wb = LANES * npb
    tc = _pick(t_dim, tc, SUBLANES)

    def body(r_ref, w_ref, k_ref, v_ref, kk_ref, a_ref, e_ref, y_ref, sall_ref, sfin_ref, s_ref):
        @pl.when(pl.program_id(1) == 0)
        def _():
            s_ref[...] = jnp.zeros_like(s_ref)

        m0, m1, i0, i1 = _pair_consts()
        i01 = i0 + i1
        e = e_ref[...]
        sub8 = lax.broadcasted_iota(jnp.int32, (SUBLANES, LANES), 0)
        lanes = [slice(q * LANES, (q + 1) * LANES) for q in range(npb)]
        ng = tc // SUBLANES

        def lanesums(tiles):
            sums = _lanesum(jnp.concatenate(tiles, axis=0))
            return [sums[i * HEAD:(i + 1) * HEAD] for i in range(len(tiles))]

        def halves(s, row):
            return [s * (row * m0), s * (row * m1)]

        def group(gi, carry):
            base = pl.multiple_of(gi * SUBLANES, SUBLANES)
            rows = pl.ds(base, SUBLANES)
            s = list(carry)
            r8 = [r_ref[rows, sl] for sl in lanes]
            w8 = [w_ref[rows, sl] for sl in lanes]
            k8 = [k_ref[rows, sl] for sl in lanes]
            kk8 = [kk_ref[rows, sl] for sl in lanes]
            b8 = [kk8[q] * a_ref[rows, lanes[q]] for q in range(npb)]
            v8 = [_split_bf16(v_ref[rows, sl]) for sl in lanes]
            y8 = [jnp.zeros((SUBLANES, LANES), F32)] * npb
            for j in range(SUBLANES + 1):
                one, before = slice(j, j + 1), slice(j - 1, j)
                tiles = []
                for q in range(npb):
                    if j < SUBLANES:
                        tiles += halves(s[q], kk8[q][one])
                    if j > 0:
                        tiles += halves(s[q], r8[q][before])
                cols = lanesums(tiles)
                per = len(tiles) // npb
                for q in range(npb):
                    mine = cols[q * per:(q + 1) * per]
                    if j > 0:
                        y8[q] = jnp.where(sub8 == j - 1, _row_pair(mine[-2], mine[-1], i0, i1), y8[q])
                    if j < SUBLANES:
                        sall_ref[base + j, :, lanes[q]] = s[q]
                        sb = mine[0] * m0 + mine[1] * m1
                        s[q] = s[q] * w8[q][one] - sb * b8[q][one] + _col_pair(v8[q], j, i01, e) * k8[q][one]
            for q in range(npb):
                y_ref[rows, lanes[q]] = y8[q]
            return tuple(s)

        init = tuple(s_ref[:, q * LANES:(q + 1) * LANES] for q in range(npb))
        fin = lax.fori_loop(0, ng, group, init)
        for q in range(npb):
            s_ref[:, q * LANES:(q + 1) * LANES] = fin[q]
            sfin_ref[:, q * LANES:(q + 1) * LANES] = fin[q]

    row = pl.BlockSpec((tc, wb), lambda p, c: (c, p))
    return pl.pallas_call(
        body, name="rwkv_scan_fwd", grid=(c_dim // wb, t_dim // tc),
        in_specs=[row] * 6 + [pl.BlockSpec((LANES, LANES), lambda p, c: (0, 0))],
        out_specs=[row, pl.BlockSpec((tc, HEAD, wb), lambda p, c: (c, 0, p)), pl.BlockSpec((HEAD, wb), lambda p, c: (0, p))],
        out_shape=[jax.ShapeDtypeStruct((t_dim, c_dim), F32), jax.ShapeDtypeStruct((t_dim, HEAD, c_dim), F32),
                   jax.ShapeDtypeStruct((HEAD, c_dim), F32)],
        scratch_shapes=[pltpu.VMEM((HEAD, wb), F32)],
        compiler_params=_params("parallel", "arbitrary"),
    )(r, w, k, v, kk, a, _head_ones())


def _scan_bwd(r, w, k, v, kk, a, dy, sall, sfin, tc, npb, exchange):
    t_dim, c_dim = r.shape
    wb = LANES * npb
    tc = _pick(t_dim, tc, SUBLANES)
    nc = t_dim // tc
    nx = exchange.n
    n_blocks = c_dim // wb

    def body(*refs):
        r_ref, w_ref, k_ref, v_ref, kk_ref, a_ref, dy_ref, sall_ref, sfin_ref, e_ref = refs[:10]
        dr_ref, dw_ref, dk_ref, dv_ref, dkk_ref, da_ref = refs[10 + nx:16 + nx]
        ds_ref, sn_ref = refs[16 + 2 * nx:18 + 2 * nx]
        here_p, here_c = pl.program_id(0), pl.program_id(1)
        ex_refs = (refs[10:10 + nx], refs[16 + nx:16 + 2 * nx], refs[18 + 2 * nx:])

        @pl.when((here_p == 0) & (here_c == 0))
        def _():
            exchange.start(*ex_refs)

        @pl.when(pl.program_id(1) == 0)
        def _():
            ds_ref[...] = jnp.zeros_like(ds_ref)
            sn_ref[...] = sfin_ref[...]

        m0, m1, i0, i1 = _pair_consts()
        i01 = i0 + i1
        e = e_ref[...]
        sub8 = lax.broadcasted_iota(jnp.int32, (SUBLANES, LANES), 0)
        lanes = [slice(q * LANES, (q + 1) * LANES) for q in range(npb)]
        ng = tc // SUBLANES

        def halves(s, row):
            return [s * (row * m0), s * (row * m1)]

        def group(gi, carry):
            base = pl.multiple_of((ng - 1 - gi) * SUBLANES, SUBLANES)
            rows = pl.ds(base, SUBLANES)
            ds = list(carry)
            r8 = [r_ref[rows, sl] for sl in lanes]
            w8 = [w_ref[rows, sl] for sl in lanes]
            k8 = [k_ref[rows, sl] for sl in lanes]
            kk8 = [kk_ref[rows, sl] for sl in lanes]
            a8 = [a_ref[rows, sl] for sl in lanes]
            v8 = [_split_bf16(v_ref[rows, sl]) for sl in lanes]
            dy8 = [_split_bf16(dy_ref[rows, sl]) for sl in lanes]
            zero8 = jnp.zeros((SUBLANES, LANES), F32)
            dr8, dw8, dk8, dv8, dkk8, da8 = ([zero8] * npb for _ in range(6))
            for j in reversed(range(SUBLANES)):
                one = slice(j, j + 1)
                here = sub8 == j
                d, s_prev, tiles = [], [], []
                for q in range(npb):
                    dyb = _col_pair(dy8[q], j, i01, e)
                    s_prev.append(sall_ref[base + j, :, lanes[q]])
                    dr8[q] = jnp.where(here, _colsum(sn_ref[:, lanes[q]] * dyb), dr8[q])
                    sn_ref[:, lanes[q]] = s_prev[q]
                    d.append(ds[q] + dyb * r8[q][one])
                for q in range(npb):
                    tiles += halves(d[q], kk8[q][one] * a8[q][one])
                for q in range(npb):
                    tiles += halves(d[q], k8[q][one]) + halves(s_prev[q], kk8[q][one])
                sums = _lanesum(jnp.concatenate(tiles, axis=0))
                cols = [sums[i * HEAD:(i + 1) * HEAD] for i in range(len(tiles))]
                for q in range(npb):
                    kkr, ar = kk8[q][one], a8[q][one]
                    dsb = -(cols[2 * q] * m0 + cols[2 * q + 1] * m1)
                    ds[q] = d[q] * w8[q][one] + dsb * kkr
                    rest = cols[2 * npb + 4 * q:2 * npb + 4 * q + 4]
                    dv8[q] = jnp.where(here, _row_pair(rest[0], rest[1], i0, i1), dv8[q])
                    sb = rest[2] * m0 + rest[3] * m1
                    db = -_colsum(d[q] * sb)
                    dk8[q] = jnp.where(here, _colsum(d[q] * _col_pair(v8[q], j, i01, e)), dk8[q])
                    dw8[q] = jnp.where(here, _colsum(d[q] * s_prev[q]), dw8[q])
                    dkk8[q] = jnp.where(here, _colsum(s_prev[q] * dsb) + db * ar, dkk8[q])
                    da8[q] = jnp.where(here, db * kkr, da8[q])
            for q in range(npb):
                sl = lanes[q]
                dr_ref[rows, sl], dw_ref[rows, sl], dk_ref[rows, sl] = dr8[q], dw8[q], dk8[q]
                dv_ref[rows, sl], dkk_ref[rows, sl], da_ref[rows, sl] = dv8[q], dkk8[q], da8[q]
            return tuple(ds)

        init = tuple(ds_ref[:, q * LANES:(q + 1) * LANES] for q in range(npb))
        fin = lax.fori_loop(0, ng, group, init)
        for q in range(npb):
            ds_ref[:, q * LANES:(q + 1) * LANES] = fin[q]

        @pl.when((here_p == n_blocks - 1) & (here_c == nc - 1))
        def _():
            exchange.finish(*ex_refs)

    row = pl.BlockSpec((tc, wb), lambda p, c: (nc - 1 - c, p))
    outs = pl.pallas_call(
        body, name="rwkv_scan_bwd", grid=(n_blocks, nc),
        in_specs=[row] * 7 + [pl.BlockSpec((tc, HEAD, wb), lambda p, c: (nc - 1 - c, 0, p)),
                              pl.BlockSpec((HEAD, wb), lambda p, c: (0, p)),
                              pl.BlockSpec((LANES, LANES), lambda p, c: (0, 0))] + [HBM_SPEC] * nx,
        out_specs=[row] * 6 + [HBM_SPEC] * nx,
        out_shape=[jax.ShapeDtypeStruct((t_dim, c_dim), F32)] * 6 + exchange.out_shape,
        scratch_shapes=[pltpu.VMEM((HEAD, wb), F32)] * 2 + exchange.scratch,
        compiler_params=_params("arbitrary", "arbitrary"),
    )(r, w, k, v, kk, a, dy, sall, sfin, _head_ones(), *exchange.arrays)
    return outs[:6], outs[6:]


def _fox_fwd(z, ct, cq, q_cb, k_cb, v_cb, n_pairs, blk):
    t_dim = z.shape[0]
    blk = _pick(t_dim, blk)
    nq = t_dim // blk
    scale = HEAD ** -0.5

    def body(q_ref, k_ref, v_ref, ct_ref, cq_ref, o_ref, lse_ref):
        i = pl.program_id(1)
        rowi = lax.broadcasted_iota(jnp.int32, (blk, blk), 0)
        coli = lax.broadcasted_iota(jnp.int32, (blk, blk), 1)
        masks = _head_masks()
        qv = q_ref[...]
        qs = [(qv * mk).astype(BF16) for mk in masks]
        cqs = [cq_ref[:, hh:hh + 1] for hh in range(2)]

        def kv_step(j, carry, masked):
            rows = pl.ds(pl.multiple_of(j * blk, blk), blk)
            kb = k_ref[rows, :].astype(BF16)
            vv = v_ref[rows, :]
            stats, acc = list(carry[:4]), carry[4]
            rescale, add = 0.0, 0.0
            for hh in range(2):
                m, l = stats[2 * hh], stats[2 * hh + 1]
                s = (lax.dot_general(qs[hh], kb, _DOT_DIMS["nt"], preferred_element_type=F32) * scale
                     + (cqs[hh] - ct_ref[hh:hh + 1, rows]))
                if masked:
                    s = jnp.where(rowi >= coli, s, -jnp.inf)
                m_new = jnp.maximum(m, jnp.max(s, axis=1, keepdims=True))
                alpha = jnp.exp(m - m_new)
                pr = jnp.exp(s - m_new)
                stats[2 * hh], stats[2 * hh + 1] = m_new, l * alpha + jnp.sum(pr, axis=1, keepdims=True)
                rescale = rescale + alpha * masks[hh]
                hi = pr.astype(BF16)
                both = jnp.concatenate([hi, (pr - hi.astype(F32)).astype(BF16)], axis=1)
                vh = (vv * masks[hh]).astype(BF16)
                add = add + jnp.dot(both, jnp.concatenate([vh, vh], axis=0), preferred_element_type=F32)
            return (*stats, acc * rescale + add)

        neg, zero = jnp.full((blk, 1), -jnp.inf, F32), jnp.zeros((blk, 1), F32)
        carry = lax.fori_loop(0, i, functools.partial(kv_step, masked=False),
                              (neg, zero, neg, zero, jnp.zeros((blk, LANES), F32)))
        m0, l0, m1, l1, acc = kv_step(i, carry, True)
        o_ref[...] = acc * (masks[0] / l0 + masks[1] / l1)
        lse_ref[:, 0:1] = m0 + jnp.log(l0)
        lse_ref[:, 1:2] = m1 + jnp.log(l1)

    full = lambda cb: pl.BlockSpec((t_dim, LANES), lambda p, i, cb=cb: (0, cb + p))
    return pl.pallas_call(
        body, name="fox_attn_fwd", grid=(n_pairs, nq),
        in_specs=[pl.BlockSpec((blk, LANES), lambda p, i: (i, q_cb + p)), full(k_cb), full(v_cb),
                  pl.BlockSpec((None, SUBLANES, t_dim), lambda p, i: (p, 0, 0)),
                  pl.BlockSpec((None, blk, 2), lambda p, i: (p, i, 0))],
        out_specs=[pl.BlockSpec((blk, LANES), lambda p, i: (i, p)), pl.BlockSpec((None, blk, 2), lambda p, i: (p, i, 0))],
        out_shape=[jax.ShapeDtypeStruct((t_dim, n_pairs * LANES), F32), jax.ShapeDtypeStruct((n_pairs, t_dim, 2), F32)],
        compiler_params=_params("parallel", "arbitrary"),
    )(z, z, z, ct, cq)


def _fox_rowdot(z, ct, cq, do, lse, q_cb, k_cb, v_cb, n_pairs, blk):
    t_dim = z.shape[0]
    blk = _pick(t_dim, blk)
    scale = HEAD ** -0.5

    def body(q_ref, k_ref, v_ref, ct_ref, cq_ref, do_ref, lse_ref, out_ref):
        i = pl.program_id(1)
        rowi = lax.broadcasted_iota(jnp.int32, (blk, blk), 0)
        coli = lax.broadcasted_iota(jnp.int32, (blk, blk), 1)
        masks = _head_masks()
        qv, dov = q_ref[...], do_ref[...]
        qs = [(qv * mk).astype(BF16) for mk in masks]
        dos = [(dov * mk).astype(BF16) for mk in masks]
        cqs = [cq_ref[:, hh:hh + 1] for hh in range(2)]
        lses = [lse_ref[:, hh:hh + 1] for hh in range(2)]

        def kv_step(j, carry, masked):
            rows = pl.ds(pl.multiple_of(j * blk, blk), blk)
            kb = k_ref[rows, :].astype(BF16)
            vb = v_ref[rows, :].astype(BF16)
            out = []
            for hh in range(2):
                num, den = carry[2 * hh], carry[2 * hh + 1]
                s = (lax.dot_general(qs[hh], kb, _DOT_DIMS["nt"], preferred_element_type=F32) * scale
                     + (cqs[hh] - ct_ref[hh:hh + 1, rows]))
                pr = jnp.exp(s - lses[hh])
                if masked:
                    pr = jnp.where(rowi >= coli, pr, 0.0)
                dp = lax.dot_general(dos[hh], vb, _DOT_DIMS["nt"], preferred_element_type=F32)
                out += [num + jnp.sum(pr * dp, axis=1, keepdims=True), den + jnp.sum(pr, axis=1, keepdims=True)]
            return tuple(out)

        zero = jnp.zeros((blk, 1), F32)
        carry = lax.fori_loop(0, i, functools.partial(kv_step, masked=False), (zero, zero, zero, zero))
        num0, den0, num1, den1 = kv_step(i, carry, True)
        out_ref[:, 0:1] = num0 / den0
        out_ref[:, 1:2] = num1 / den1

    full = lambda cb: pl.BlockSpec((t_dim, LANES), lambda p, i, cb=cb: (0, cb + p))
    return pl.pallas_call(
        body, name="fox_attn_rowdot", grid=(n_pairs, t_dim // blk),
        in_specs=[pl.BlockSpec((blk, LANES), lambda p, i: (i, q_cb + p)), full(k_cb), full(v_cb),
                  pl.BlockSpec((None, SUBLANES, t_dim), lambda p, i: (p, 0, 0)),
                  pl.BlockSpec((None, blk, 2), lambda p, i: (p, i, 0)),
                  pl.BlockSpec((blk, LANES), lambda p, i: (i, p)),
                  pl.BlockSpec((None, blk, 2), lambda p, i: (p, i, 0))],
        out_specs=pl.BlockSpec((None, blk, 2), lambda p, i: (p, i, 0)),
        out_shape=jax.ShapeDtypeStruct((n_pairs, t_dim, 2), F32),
        compiler_params=_params("parallel", "arbitrary"),
    )(z, z, z, ct, cq, do, lse)


def _fox_bwd(z, ct, cq, rowdot, do, lse, q_cb, k_cb, v_cb, n_pairs, blk):
    t_dim = z.shape[0]
    blk = _pick(t_dim, blk)
    nb = t_dim // blk
    scale = HEAD ** -0.5

    def body(q_ref, k_ref, v_ref, ct_ref, cq_ref, rd_ref, do_ref, lse_ref, dq_ref, dk_ref, dv_ref, dc_ref):
        j = pl.program_id(1)

        @pl.when(j == 0)
        def _():
            dq_ref[...] = jnp.zeros_like(dq_ref)

        rowi = lax.broadcasted_iota(jnp.int32, (blk, blk), 0)
        coli = lax.broadcasted_iota(jnp.int32, (blk, blk), 1)
        krows = pl.ds(pl.multiple_of(j * blk, blk), blk)
        masks = _head_masks()
        kv, vb = k_ref[...], v_ref[...].astype(BF16)
        kb = kv.astype(BF16)
        ks = [(kv * mk).astype(BF16) for mk in masks]
        cks = [ct_ref[hh:hh + 1, krows] for hh in range(2)]

        def q_step(i, carry, masked):
            dk, dv, dcs = carry[0], carry[1], list(carry[2:])
            rows = pl.ds(pl.multiple_of(i * blk, blk), blk)
            qv, dov = q_ref[rows, :], do_ref[rows, :]
            dq = 0.0
            for hh in range(2):
                qh = (qv * masks[hh]).astype(BF16)
                doh = (dov * masks[hh]).astype(BF16)
                s = (lax.dot_general(qh, kb, _DOT_DIMS["nt"], preferred_element_type=F32) * scale
                     + (cq_ref[rows, hh:hh + 1] - cks[hh]))
                pr = jnp.exp(s - lse_ref[rows, hh:hh + 1])
                if masked:
                    pr = jnp.where(rowi >= coli, pr, 0.0)
                dv = dv + lax.dot_general(pr.astype(BF16), doh, _DOT_DIMS["tn"], preferred_element_type=F32)
                dp = lax.dot_general(doh, vb, _DOT_DIMS["nt"], preferred_element_type=F32)
                ds = pr * (dp - rd_ref[rows, hh * HEAD:hh * HEAD + 1])
                dsb = ds.astype(BF16)
                dq = dq + jnp.dot(dsb, ks[hh], preferred_element_type=F32)
                dk = dk + lax.dot_general(dsb, qh, _DOT_DIMS["tn"], preferred_element_type=F32)
                dcs[hh] = dcs[hh] - _colsum(ds)
            dq_ref[rows, :] += dq * scale
            return (dk, dv, *dcs)

        zero_row = jnp.zeros((1, blk), F32)
        init = (jnp.zeros((blk, LANES), F32), jnp.zeros((blk, LANES), F32), zero_row, zero_row)
        carry = q_step(j, init, True)
        dk, dv, dc0, dc1 = lax.fori_loop(j + 1, nb, functools.partial(q_step, masked=False), carry)
        dk_ref[...] = dk * scale
        dv_ref[...] = dv
        dc_ref[0:1, :] = dc0
        dc_ref[1:2, :] = dc1
        dc_ref[2:SUBLANES, :] = jnp.zeros((SUBLANES - 2, blk), F32)

    full = lambda: pl.BlockSpec((t_dim, LANES), lambda p, j: (0, p))
    blkspec = pl.BlockSpec((blk, LANES), lambda p, j: (j, p))
    return pl.pallas_call(
        body, name="fox_attn_bwd", grid=(n_pairs, nb),
        in_specs=[pl.BlockSpec((t_dim, LANES), lambda p, j: (0, q_cb + p)),
                  pl.BlockSpec((blk, LANES), lambda p, j: (j, k_cb + p)),
                  pl.BlockSpec((blk, LANES), lambda p, j: (j, v_cb + p)),
                  pl.BlockSpec((None, SUBLANES, t_dim), lambda p, j: (p, 0, 0)),
                  pl.BlockSpec((None, t_dim, 2), lambda p, j: (p, 0, 0)), full(), full(),
                  pl.BlockSpec((None, t_dim, 2), lambda p, j: (p, 0, 0))],
        out_specs=[full(), blkspec, blkspec, pl.BlockSpec((None, SUBLANES, blk), lambda p, j: (p, 0, j))],
        out_shape=[jax.ShapeDtypeStruct((t_dim, n_pairs * LANES), F32)] * 3
                  + [jax.ShapeDtypeStruct((n_pairs, SUBLANES, t_dim), F32)],
        compiler_params=_params("parallel", "arbitrary"),
    )(z, z, z, ct, cq, rowdot, do, lse)


HBM_SPEC = pl.BlockSpec(memory_space=pltpu.HBM)


def _all_gather(shards, name):
    n = len(shards)

    def body(*refs):
        x_refs, out_refs = refs[:n], refs[n:2 * n]
        send_sems, recv_sems, local_sems = refs[2 * n:]
        x, y, c = lax.axis_index("x"), lax.axis_index("y"), lax.axis_index("c")
        me, sibling = (x, y, c), (x, y, 1 - c)
        chips = [(1 - x, y), (x, 1 - y), (1 - x, 1 - y)]

        def copy(a, k, block, to, from_input=False):
            px, py, pc = block
            slot = out_refs[a].at[4 * px + 2 * py + pc]
            return pltpu.make_async_remote_copy(
                src_ref=x_refs[a] if from_input else slot, dst_ref=slot,
                send_sem=send_sems.at[7 * a + k], recv_sem=recv_sems.at[7 * a + k], device_id=to, device_id_type=MESH)

        mine = [pltpu.make_async_copy(x_refs[a], out_refs[a].at[4 * x + 2 * y + c], local_sems.at[a]) for a in range(n)]
        for cp in mine:
            cp.start()
        first = []
        for a in range(n):
            first += [copy(a, 1 + j, me, (*chip, c), from_input=True) for j, chip in enumerate(chips)]
            first.append(copy(a, 0, me, sibling, from_input=True))
        for cp in first:
            cp.start()
        passed = []
        for a in range(n):
            for j, chip in enumerate(chips):
                copy(a, 1 + j, (*chip, c), me).wait_recv()
                passed.append(copy(a, 4 + j, (*chip, c), sibling))
                passed[-1].start()
        for a in range(n):
            copy(a, 0, sibling, me).wait_recv()
            for j, chip in enumerate(chips):
                copy(a, 4 + j, (*chip, 1 - c), me).wait_recv()
        for cp in first + passed:
            cp.wait_send()
        for cp in mine:
            cp.wait()

    return pl.pallas_call(
        body, name=name, out_shape=[jax.ShapeDtypeStruct((NDEV,) + s.shape, s.dtype) for s in shards],
        in_specs=[HBM_SPEC] * n, out_specs=[HBM_SPEC] * n,
        scratch_shapes=[pltpu.SemaphoreType.DMA((7 * n,)), pltpu.SemaphoreType.DMA((7 * n,)),
                        pltpu.SemaphoreType.DMA((n,))],
    )(*shards)


class _Exchange:
    def __init__(self, arrays):
        self.arrays = list(arrays)
        self.n = len(self.arrays)
        self.per_dest = [a.ndim == 3 for a in self.arrays]
        self.out_shape = [jax.ShapeDtypeStruct(a.shape if pd else (NDEV,) + a.shape, a.dtype)
                          for a, pd in zip(self.arrays, self.per_dest)]
        self.scratch = [pltpu.SemaphoreType.DMA((7 * self.n,)), pltpu.SemaphoreType.DMA((7 * self.n,)),
                        pltpu.SemaphoreType.DMA((self.n,))]

    def _copies(self, in_refs, out_refs, sems):
        send_sems, recv_sems, local_sems = sems
        x, y, c = lax.axis_index("x"), lax.axis_index("y"), lax.axis_index("c")
        me = 4 * x + 2 * y + c
        own, sends, recvs = [], [], []
        for a in range(self.n):
            mine = in_refs[a].at[me] if self.per_dest[a] else in_refs[a]
            own.append(pltpu.make_async_copy(mine, out_refs[a].at[me], local_sems.at[a]))
            for k in range(1, NDEV):
                px = 1 - x if k & 4 else x
                py = 1 - y if k & 2 else y
                pc = 1 - c if k & 1 else c
                peer = 4 * px + 2 * py + pc
                sem = dict(send_sem=send_sems.at[7 * a + k - 1], recv_sem=recv_sems.at[7 * a + k - 1],
                           device_id=(px, py, pc), device_id_type=MESH)
                src = in_refs[a].at[peer] if self.per_dest[a] else in_refs[a]
                sends.append(pltpu.make_async_remote_copy(src_ref=src, dst_ref=out_refs[a].at[me], **sem))
                recvs.append(pltpu.make_async_remote_copy(src_ref=src, dst_ref=out_refs[a].at[peer], **sem))
        return own, sends, recvs

    def start(self, in_refs, out_refs, sems):
        own, sends, _ = self._copies(in_refs, out_refs, sems)
        for cp in own + sends:
            cp.start()

    def finish(self, in_refs, out_refs, sems):
        own, sends, recvs = self._copies(in_refs, out_refs, sems)
        for cp in recvs:
            cp.wait_recv()
        for cp in sends:
            cp.wait_send()
        for cp in own:
            cp.wait()


def _grad_exchange(arrays, name):
    ex = _Exchange(arrays)

    def body(*refs):
        in_refs, out_refs, sems = refs[:ex.n], refs[ex.n:2 * ex.n], refs[2 * ex.n:]
        ex.start(in_refs, out_refs, sems)
        ex.finish(in_refs, out_refs, sems)

    return pl.pallas_call(body, name=name, out_shape=ex.out_shape, in_specs=[HBM_SPEC] * ex.n,
                          out_specs=[HBM_SPEC] * ex.n, scratch_shapes=ex.scratch)(*ex.arrays)


def _adamw(partials, w, m, v, name):
    rows, width = w.shape
    br = _pick(rows, 128, 2 * SUBLANES)

    def body(p_ref, w_ref, m_ref, v_ref, g_out, d_out, m_out, v_out):
        g = p_ref[0].astype(F32)
        for d in range(1, NDEV):
            g = g + p_ref[d].astype(F32)
        mn = ADAM_B1 * m_ref[...] + (1.0 - ADAM_B1) * g
        vn = ADAM_B2 * v_ref[...] + (1.0 - ADAM_B2) * jnp.square(g)
        m_hat = mn / (1.0 - ADAM_B1 ** ADAM_STEP)
        v_hat = vn / (1.0 - ADAM_B2 ** ADAM_STEP)
        g_out[...] = g
        d_out[...] = -ADAM_LR * (m_hat / (jnp.sqrt(v_hat) + ADAM_EPS) + ADAM_WD * w_ref[...])
        m_out[...] = mn
        v_out[...] = vn

    blk = pl.BlockSpec((br, width), lambda i: (i, 0))
    return pl.pallas_call(
        body, name=name, grid=(rows // br,),
        in_specs=[pl.BlockSpec((NDEV, br, width), lambda i: (0, i, 0)), blk, blk, blk],
        out_specs=[pl.BlockSpec((None, br, width), lambda i: (0, i, 0))] * 4,
        out_shape=[jax.ShapeDtypeStruct((1, rows, width), F32)] * 4,
        compiler_params=_params("parallel"),
    )(partials, w, m, v)


def _pack_rows(flat):
    n = flat.shape[0]
    padded = _round_up(n, PACK_ALIGN)
    return jnp.pad(flat, (0, padded - n)).reshape(padded // PACK_W, PACK_W)


def _split_shards(full, axis):
    rows, cols = full.shape
    if axis == 0:
        return full.reshape(NDEV, rows // NDEV, cols)
    width = cols // NDEV
    return jnp.stack([full[:, d * width:(d + 1) * width] for d in range(NDEV)])


def _join_shards(blocks, axis):
    if axis == 0:
        return blocks.reshape(-1, blocks.shape[2])
    return jnp.concatenate([blocks[d] for d in range(NDEV)], axis=1)


SHARDED = (("w_in", 1), ("rw_w_lora_up", 1), ("rw_a_lora_up", 1), ("w_up_rwkv", 1), ("w_up_fox", 1),
           ("w_out", 0), ("ple_proj", 1), ("ple_gate_w", 0))
REPLICATED = ("norm_g", "rw_shift_mu", "rw_w0", "rw_a0", "rw_k_k", "rw_k_a", "rw_r_k", "rw_ln_g", "rw_ln_b",
              "fox_b_f", "ple_norm_g", "final_norm_g")
WEIGHTS = ("norm_g", "w_in", "rw_shift_mu", "rw_w0", "rw_w_lora_up", "rw_a0", "rw_a_lora_up", "rw_k_k", "rw_k_a",
           "rw_r_k", "rw_ln_g", "rw_ln_b", "fox_b_f", "w_up_rwkv", "w_up_fox", "w_out", "ple_proj", "ple_gate_w",
           "ple_norm_g", "final_norm_g")


def _local_step(x, p, tgt, wz, wl, wa, wur, wuf, wo, pp, pg, rep, dims):
    t_dim, d_model, c_rw, lora, c_fox, h_fox, sec = dims
    bt = _pick(t_dim, 256, 2 * SUBLANES)
    bt_many = _pick(t_dim, 128, 2 * SUBLANES)
    n_pairs = c_fox // LANES
    row = lambda a: a.reshape(1, -1)
    norm_g, mu, w0, a0 = row(rep["norm_g"]), row(rep["rw_shift_mu"]), row(rep["rw_w0"]), row(rep["rw_a0"])
    k_k, k_a, r_k = row(rep["rw_k_k"]), row(rep["rw_k_a"]), row(rep["rw_r_k"])
    ln_g, ln_b = row(rep["rw_ln_g"]), row(rep["rw_ln_b"])
    g2, g3 = row(rep["ple_norm_g"]), row(rep["final_norm_g"])
    b_f = jnp.pad(row(rep["fox_b_f"]), ((0, 0), (0, LANES - h_fox)))
    e_head = _head_ones(F32)
    c4 = 4 * c_rw
    inv_d = 1.0 / d_model
    decay_k = math.exp(-0.5)

    def norm_in(x_ref, g_ref, h_ref):
        xv = x_ref[...]
        rms = lax.rsqrt(jnp.mean(xv * xv, axis=-1, keepdims=True) + NORM_EPS)
        h_ref[...] = (xv * rms * g_ref[...]).astype(BF16)

    (h,) = _rowcall(norm_in, "norm_in", t_dim, bt, [x], [norm_g], [(d_model, BF16)])
    z = _matmul(h, wz, "nn", "proj_in", bn=1408)

    def rw_values(zs, w0_ref, wl_ref, a0_ref, wa_ref, kk_ref, ka_ref, e_ref):
        k = zs[:, c_rw:2 * c_rw]
        tw = jnp.tanh(zs[:, c4:c4 + lora])
        al = zs[:, c4 + lora:c4 + 2 * lora]
        sw = _sigmoid(w0_ref[...] + jnp.dot(tw.astype(BF16), wl_ref[...], preferred_element_type=F32))
        decay = jnp.exp(-decay_k * sw)
        a = _sigmoid(a0_ref[...] + jnp.dot(al.astype(BF16), wa_ref[...], preferred_element_type=F32))
        kk0 = k * kk_ref[...]
        nrm = jnp.sqrt(_headsum(kk0 * kk0, e_ref[...]))
        inv = 1.0 / jnp.maximum(nrm, 1e-12)
        k2 = k * (1.0 + (a - 1.0) * ka_ref[...])
        return k, tw, al, sw, decay, a, kk0, nrm, inv, k2

    def rw_prep(z_ref, mu_ref, w0_ref, wl_ref, a0_ref, wa_ref, kk_ref, ka_ref, e_ref,
                r_o, w_o, k_o, v_o, kk_o, a_o, g_o, carry):
        _first_step_zero(carry)
        zv = z_ref[...]
        zs = zv + (_shift_down(zv, carry) - zv) * mu_ref[...]
        k, tw, al, sw, decay, a, kk0, nrm, inv, k2 = rw_values(zs, w0_ref, wl_ref, a0_ref, wa_ref, kk_ref, ka_ref, e_ref)
        r_o[...] = zs[:, 0:c_rw]
        w_o[...] = decay
        k_o[...] = k2
        v_o[...] = zs[:, 2 * c_rw:3 * c_rw]
        kk_o[...] = kk0 * inv
        a_o[...] = a
        g_o[...] = zs[:, 3 * c_rw:c4]

    rw_consts = [mu, w0, wl, a0, wa, k_k, k_a, e_head]
    r_s, w_s, k_s, v_s, kk_s, a_s, g_s = _rowcall(
        rw_prep, "rwkv_prep", t_dim, bt, [(z, 0, sec)], rw_consts, [(c_rw, F32)] * 7,
        scratch=[pltpu.VMEM((1, sec), F32)])
    pairs_fwd = max(n for n in (1, 2, 4) if c_rw % (n * LANES) == 0)
    pairs_bwd = pairs_fwd
    y_s, s_all, s_fin = _scan_fwd(r_s, w_s, k_s, v_s, kk_s, a_s, 64, pairs_fwd)

    def rw_post_values(y, r, k2, v, g, lng_ref, lnb_ref, rk_ref, e):
        mean = _headsum(y, e) * (1.0 / HEAD)
        d = y - mean
        rstd = lax.rsqrt(_headsum(d * d, e) * (1.0 / HEAD) + GN_EPS)
        yh = d * rstd
        rk = _headsum(r * k2 * rk_ref[...], e)
        yo = yh * lng_ref[...] + lnb_ref[...] + rk * v
        sg = _sigmoid(g)
        return rstd, yh, rk, yo, sg

    def rw_post(y_ref, r_ref, k_ref, v_ref, g_ref, lng_ref, lnb_ref, rk_ref, e_ref, out_ref):
        g = g_ref[...]
        _, _, _, yo, sg = rw_post_values(y_ref[...], r_ref[...], k_ref[...], v_ref[...], g, lng_ref, lnb_ref, rk_ref, e_ref[...])
        out_ref[...] = (yo * g * sg).astype(BF16)

    (y_rw,) = _rowcall(rw_post, "rwkv_post", t_dim, bt, [y_s, r_s, k_s, v_s, g_s], [ln_g, ln_b, r_k, e_head], [(c_rw, BF16)])

    fl_cb = (sec + 4 * c_fox) // LANES
    hp = _round_up(h_fox, SUBLANES)
    bt_c = _pick(t_dim, 256)
    tri = (jnp.arange(bt_c)[:, None] >= jnp.arange(bt_c)[None, :]).astype(F32)

    rows8 = n_pairs * SUBLANES
    pair_rows = (jnp.arange(rows8)[:, None] // SUBLANES * 2 + jnp.arange(rows8)[:, None] % SUBLANES
                 == jnp.arange(LANES)[None, :]) & (jnp.arange(rows8)[:, None] % SUBLANES < 2)
    pair_rows = pair_rows.astype(F32)

    def fox_decay(fl_ref, bf_ref, tri_ref, sel_ref, ct_ref, cq_ref, carry):
        _first_step_zero(carry)
        lf = _log_sigmoid(fl_ref[...] + bf_ref[...])
        c = jnp.dot(tri_ref[...], lf, precision=HI, preferred_element_type=F32) + carry[...]
        carry[...] = c[bt_c - 1:bt_c, :]
        ct = jnp.dot(sel_ref[...], jnp.transpose(c), precision=HI, preferred_element_type=F32)
        ct_ref[...] = ct.reshape(n_pairs, SUBLANES, bt_c)
        for pair in range(n_pairs):
            cq_ref[pair] = c[:, 2 * pair:2 * pair + 2]

    ct, cq = pl.pallas_call(
        fox_decay, name="fox_decay", grid=(t_dim // bt_c,),
        in_specs=[pl.BlockSpec((bt_c, LANES), lambda i: (i, fl_cb)), pl.BlockSpec((1, LANES), lambda i: (0, 0)),
                  pl.BlockSpec((bt_c, bt_c), lambda i: (0, 0)), pl.BlockSpec((rows8, LANES), lambda i: (0, 0))],
        out_specs=[pl.BlockSpec((n_pairs, SUBLANES, bt_c), lambda i: (0, 0, i)),
                   pl.BlockSpec((n_pairs, bt_c, 2), lambda i: (0, i, 0))],
        out_shape=[jax.ShapeDtypeStruct((n_pairs, SUBLANES, t_dim), F32), jax.ShapeDtypeStruct((n_pairs, t_dim, 2), F32)],
        scratch_shapes=[pltpu.VMEM((1, LANES), F32)], compiler_params=_params("arbitrary"),
    )(z, b_f, tri, pair_rows)
    q_cb = sec // LANES
    k_cb, v_cb = q_cb + n_pairs, q_cb + 2 * n_pairs
    o_fox, lse = _fox_fwd(z, ct, cq, q_cb, k_cb, v_cb, n_pairs, 1024)

    def fox_post(o_ref, z_ref, out_ref):
        g = z_ref[:, 3 * c_fox:4 * c_fox]
        out_ref[...] = (o_ref[...] * g * _sigmoid(g)).astype(BF16)

    (y_fox,) = _rowcall(fox_post, "fox_post", t_dim, bt, [o_fox, (z, 1, sec)], [], [(c_fox, BF16)])

    u_rw = _matmul(y_rw, wur, "nn", "up_rwkv")
    u_fox = _matmul(y_fox, wuf, "nn", "up_fox")

    def merge(ur_ref, uf_ref, z_ref, out_ref):
        s1 = _sigmoid(z_ref[:, 0:d_model])
        s2 = _sigmoid(z_ref[:, d_model:2 * d_model])
        out_ref[...] = (s1 * ur_ref[...] + s2 * uf_ref[...]).astype(BF16)

    (merged,) = _rowcall(merge, "merge", t_dim, bt, [u_rw, u_fox, (z, 2, sec)], [], [(d_model, BF16)])
    mo = _matmul(merged, wo, "nn", "proj_out")

    def resid_norm(x_ref, mo_ref, g_ref, x1_ref, n2_ref):
        x1 = x_ref[...] + mo_ref[...]
        rms = lax.rsqrt(jnp.mean(x1 * x1, axis=-1, keepdims=True) + NORM_EPS)
        x1_ref[...] = x1
        n2_ref[...] = (x1 * rms * g_ref[...]).astype(BF16)

    x1, n2 = _rowcall(resid_norm, "resid_norm", t_dim, bt, [x, mo], [g2], [(d_model, F32), (d_model, BF16)])
    ple = _matmul(p, pp, "nn", "ple_proj")
    gl = _matmul(n2, pg, "nn", "ple_gate")

    def head(x1_ref, ple_ref, gl_ref, tgt_ref, g_ref, dx2_ref, dple_ref, dgl_ref, loss_ref, dg3_ref):
        _first_step_zero(loss_ref, dg3_ref)
        sg = _sigmoid(gl_ref[...])
        pl_v = ple_ref[...]
        x2 = x1_ref[...] + pl_v * sg
        rms = lax.rsqrt(jnp.mean(x2 * x2, axis=-1, keepdims=True) + NORM_EPS)
        xn = x2 * rms
        diff = xn * g_ref[...] - tgt_ref[...]
        loss_ref[...] += 0.5 * jnp.sum(jnp.mean(diff * diff, axis=-1, keepdims=True), axis=0, keepdims=True)
        dyf = diff * inv_d
        dg3_ref[...] += _colsum(dyf * xn)
        gy = dyf * g_ref[...]
        dx2 = rms * (gy - xn * jnp.mean(xn * gy, axis=-1, keepdims=True))
        dx2_ref[...] = dx2
        dple_ref[...] = (dx2 * sg).astype(BF16)
        dgl_ref[...] = (dx2 * pl_v * sg * (1.0 - sg)).astype(BF16)

    dx2, dple, dgl, loss, d_g3 = _rowcall(
        head, "head", t_dim, bt, [x1, ple, gl, tgt], [g3], [(d_model, F32), (d_model, BF16), (d_model, BF16)],
        acc_outs=[(1, 1), (1, d_model)])

    d_pp = _matmul(p, dple, "tn", "d_ple_proj", out_dtype=BF16)
    d_pg = _matmul(n2, dgl, "tn", "d_ple_gate", out_dtype=BF16)
    dn2 = _matmul(dgl, pg, "nt", "d_n2")

    def resid_norm_bwd(dx2_ref, dn2_ref, x1_ref, g_ref, dx1_ref, dx1b_ref, dg2_ref):
        _first_step_zero(dg2_ref)
        x1 = x1_ref[...]
        rms = lax.rsqrt(jnp.mean(x1 * x1, axis=-1, keepdims=True) + NORM_EPS)
        xn = x1 * rms
        dn = dn2_ref[...]
        dg2_ref[...] += _colsum(dn * xn)
        gy = dn * g_ref[...]
        dx1 = dx2_ref[...] + rms * (gy - xn * jnp.mean(xn * gy, axis=-1, keepdims=True))
        dx1_ref[...] = dx1
        dx1b_ref[...] = dx1.astype(BF16)

    dx1, dx1b, d_g2 = _rowcall(resid_norm_bwd, "resid_norm_bwd", t_dim, bt, [dx2, dn2, x1], [g2],
                               [(d_model, F32), (d_model, BF16)], acc_outs=[(1, d_model)])
    d_wo = _matmul(merged, dx1b, "tn", "d_w_out", out_dtype=BF16)
    dmerged = _matmul(dx1b, wo, "nt", "d_merged")

    def merge_bwd(dm_ref, ur_ref, uf_ref, z_ref, dur_ref, duf_ref, dzg_ref):
        dm = dm_ref[...]
        s1 = _sigmoid(z_ref[:, 0:d_model])
        s2 = _sigmoid(z_ref[:, d_model:2 * d_model])
        dur_ref[...] = (dm * s1).astype(BF16)
        duf_ref[...] = (dm * s2).astype(BF16)
        dzg_ref[:, 0:d_model] = (dm * ur_ref[...] * s1 * (1.0 - s1)).astype(BF16)
        dzg_ref[:, d_model:2 * d_model] = (dm * uf_ref[...] * s2 * (1.0 - s2)).astype(BF16)
        if sec > 2 * d_model:
            dzg_ref[:, 2 * d_model:sec] = jnp.zeros((dm.shape[0], sec - 2 * d_model), BF16)

    du_rw, du_fox, dz_gate = _rowcall(merge_bwd, "merge_bwd", t_dim, bt, [dmerged, u_rw, u_fox, (z, 2, sec)], [],
                                      [(d_model, BF16), (d_model, BF16), (sec, BF16)])
    d_wur = _matmul(y_rw, du_rw, "tn", "d_w_up_rwkv", out_dtype=BF16)
    d_wuf = _matmul(y_fox, du_fox, "tn", "d_w_up_fox", out_dtype=BF16)
    dy_rw = _matmul(du_rw, wur, "nt", "d_y_rwkv")
    dy_fox = _matmul(du_fox, wuf, "nt", "d_y_fox")

    def fox_post_bwd(dy_ref, o_ref, z_ref, e_ref, do_ref, dg_ref, rd_ref):
        g = z_ref[:, 3 * c_fox:4 * c_fox]
        sg = _sigmoid(g)
        dy, o = dy_ref[...], o_ref[...]
        do = dy * g * sg
        do_ref[...] = do
        dg_ref[...] = (dy * o * sg * (1.0 + g * (1.0 - sg))).astype(BF16)
        rd_ref[...] = _headsum(do.astype(BF16).astype(F32) * o, e_ref[...])

    do_fox, dg_fox, rowdot = _rowcall(fox_post_bwd, "fox_post_bwd", t_dim, bt, [dy_fox, o_fox, (z, 1, sec)], [e_head],
                                      [(c_fox, F32), (c_fox, BF16), (c_fox, F32)])
    dq_f, dk_f, dv_f, dc_t = _fox_bwd(z, ct, cq, rowdot, do_fox, lse, q_cb, k_cb, v_cb, n_pairs, 512)
    sel = (jnp.arange(hp)[:, None] // 2 * SUBLANES + jnp.arange(hp)[:, None] % 2 == jnp.arange(rows8)[None, :]).astype(F32)
    tri_rev = (jnp.arange(bt_c)[:, None] >= jnp.arange(bt_c)[None, :]).astype(F32)
    bf_col = b_f.reshape(LANES, 1)[0:hp]
    nbc = t_dim // bt_c

    def fox_decay_bwd(dc_ref, fl_ref, sel_ref, tri_ref, bf_ref, dfl_ref, dbf_ref, carry):
        _first_step_zero(carry, dbf_ref)
        dc = jnp.dot(sel_ref[...], dc_ref[...].reshape(rows8, bt_c), precision=HI, preferred_element_type=F32)
        dlf = jnp.dot(dc, tri_ref[...], precision=HI, preferred_element_type=F32) + carry[...]
        carry[...] = dlf[:, 0:1]
        flt = jnp.transpose(fl_ref[...])[0:hp, :]
        dfl = dlf * _sigmoid(-(flt + bf_ref[...]))
        head_row = lax.broadcasted_iota(jnp.int32, (hp, bt_c), 0)
        dfl = jnp.where(head_row < h_fox, dfl, 0.0)
        dbf_ref[...] += jnp.sum(dfl, axis=1, keepdims=True)
        full = jnp.concatenate([dfl, jnp.zeros((LANES - hp, bt_c), F32)], axis=0) if hp < LANES else dfl
        dfl_ref[...] = jnp.transpose(full).astype(BF16)

    dz_fl, d_bf = pl.pallas_call(
        fox_decay_bwd, name="fox_decay_bwd", grid=(nbc,),
        in_specs=[pl.BlockSpec((n_pairs, SUBLANES, bt_c), lambda i: (0, 0, nbc - 1 - i)),
                  pl.BlockSpec((bt_c, LANES), lambda i: (nbc - 1 - i, fl_cb)),
                  pl.BlockSpec(sel.shape, lambda i: (0, 0)), pl.BlockSpec((bt_c, bt_c), lambda i: (0, 0)),
                  pl.BlockSpec((hp, 1), lambda i: (0, 0))],
        out_specs=[pl.BlockSpec((bt_c, LANES), lambda i: (nbc - 1 - i, 0)), pl.BlockSpec((hp, 1), lambda i: (0, 0))],
        out_shape=[jax.ShapeDtypeStruct((t_dim, LANES), BF16), jax.ShapeDtypeStruct((hp, 1), F32)],
        scratch_shapes=[pltpu.VMEM((hp, 1), F32)], compiler_params=_params("arbitrary"),
    )(dc_t, z, sel, tri_rev, bf_col)

    def rw_post_bwd(dy_ref, y_ref, r_ref, k_ref, v_ref, g_ref, lng_ref, lnb_ref, rk_ref, e_ref,
                    dg_ref, dys_ref, dr_ref, dk_ref, dv_ref, dlng_ref, dlnb_ref, drk_ref):
        _first_step_zero(dlng_ref, dlnb_ref, drk_ref)
        e = e_ref[...]
        dy, r, k2, v, g = dy_ref[...], r_ref[...], k_ref[...], v_ref[...], g_ref[...]
        rstd, yh, rk, yo, sg = rw_post_values(y_ref[...], r, k2, v, g, lng_ref, lnb_ref, rk_ref, e)
        dg_ref[...] = dy * yo * sg * (1.0 + g * (1.0 - sg))
        dyo = dy * g * sg
        dlnb_ref[...] += _colsum(dyo)
        dlng_ref[...] += _colsum(dyo * yh)
        dyh = dyo * lng_ref[...]
        dys_ref[...] = rstd * (dyh - _headsum(dyh, e) * (1.0 / HEAD) - yh * _headsum(dyh * yh, e) * (1.0 / HEAD))
        drk = _headsum(dyo * v, e)
        dv_ref[...] = dyo * rk
        dr_ref[...] = drk * k2 * rk_ref[...]
        dk_ref[...] = drk * r * rk_ref[...]
        drk_ref[...] += _colsum(drk * r * k2)

    dg_rw, dy_s, dr_b, dk_b, dv_b, d_lng, d_lnb, d_rk = _rowcall(
        rw_post_bwd, "rwkv_post_bwd", t_dim, bt, [dy_rw, y_s, r_s, k_s, v_s, g_s], [ln_g, ln_b, r_k, e_head],
        [(c_rw, F32)] * 5, acc_outs=[(1, c_rw)] * 3)
    axis = dict(SHARDED)
    early = {"w_up_rwkv": d_wur, "w_up_fox": d_wuf, "w_out": d_wo, "ple_proj": d_pp, "ple_gate_w": d_pg}
    (dr_c, dw_c, dk_c, dv_c, dkk_c, da_c), early_recv = _scan_bwd(
        r_s, w_s, k_s, v_s, kk_s, a_s, dy_s, s_all, s_fin, 64, pairs_bwd,
        _Exchange([_split_shards(g, axis[n]) for n, g in early.items()]))
    recv = dict(zip(early, early_recv))

    def rw_prep_bwd(z_ref, dr1, dr2, dw_ref, dk1, dk2_ref, dv1, dv2, dkk_ref, da_ref, dg_ref,
                    mu_ref, w0_ref, wl_ref, a0_ref, wa_ref, kk_ref, ka_ref, e_ref,
                    dzs_ref, tw_ref, al_ref, dwr_ref, dar_ref, dmu_ref, dw0_ref, da0_ref, dkk_acc, dka_acc, carry):
        _first_step_zero(carry, dmu_ref, dw0_ref, da0_ref, dkk_acc, dka_acc)
        e = e_ref[...]
        zv = z_ref[...]
        zp = _shift_down(zv, carry)
        zs = zv + (zp - zv) * mu_ref[...]
        k, tw, al, sw, decay, a, kk0, nrm, inv, k2 = rw_values(zs, w0_ref, wl_ref, a0_ref, wa_ref, kk_ref, ka_ref, e_ref)
        dk2 = dk1[...] + dk2_ref[...]
        da = da_ref[...] + dk2 * k * ka_ref[...]
        dk = dk2 * (1.0 + (a - 1.0) * ka_ref[...])
        dka_acc[...] += _colsum(dk2 * k * (a - 1.0))
        kk = kk0 * inv
        dkk = dkk_ref[...]
        dkk0 = inv * jnp.where(nrm > 1e-12, dkk - kk * _headsum(dkk * kk, e), dkk)
        dk = dk + dkk0 * kk_ref[...]
        dkk_acc[...] += _colsum(dkk0 * k)
        da_raw = da * a * (1.0 - a)
        da0_ref[...] += _colsum(da_raw)
        dw_raw = dw_ref[...] * decay * (-decay_k) * sw * (1.0 - sw)
        dw0_ref[...] += _colsum(dw_raw)
        dar_b, dwr_b = da_raw.astype(BF16), dw_raw.astype(BF16)
        dal = lax.dot_general(dar_b, wa_ref[...], _DOT_DIMS["nt"], preferred_element_type=F32)
        dtw = lax.dot_general(dwr_b, wl_ref[...], _DOT_DIMS["nt"], preferred_element_type=F32)
        dzs_ref[:, 0:c_rw] = dr1[...] + dr2[...]
        dzs_ref[:, c_rw:2 * c_rw] = dk
        dzs_ref[:, 2 * c_rw:3 * c_rw] = dv1[...] + dv2[...]
        dzs_ref[:, 3 * c_rw:c4] = dg_ref[...]
        dzs_ref[:, c4:c4 + lora] = dtw * (1.0 - tw * tw)
        dzs_ref[:, c4 + lora:c4 + 2 * lora] = dal
        if sec > c4 + 2 * lora:
            dzs_ref[:, c4 + 2 * lora:sec] = jnp.zeros((zv.shape[0], sec - c4 - 2 * lora), F32)
        tw_ref[...] = tw.astype(BF16)
        al_ref[...] = al.astype(BF16)
        dwr_ref[...] = dwr_b
        dar_ref[...] = dar_b
        dmu_ref[...] += _colsum(dzs_ref[...] * (zp - zv))

    dzs, tw_b, al_b, dwr_b, dar_b, d_mu, d_w0, d_a0, d_kk, d_ka = _rowcall(
        rw_prep_bwd, "rwkv_prep_bwd", t_dim, bt_many,
        [(z, 0, sec), dr_c, dr_b, dw_c, dk_c, dk_b, dv_c, dv_b, dkk_c, da_c, dg_rw], rw_consts,
        [(sec, F32), (lora, BF16), (lora, BF16), (c_rw, BF16), (c_rw, BF16)],
        acc_outs=[(1, sec), (1, c_rw), (1, c_rw), (1, c_rw), (1, c_rw)], scratch=[pltpu.VMEM((1, sec), F32)])
    d_wl = _matmul(tw_b, dwr_b, "tn", "d_w_lora", out_dtype=BF16)
    d_wa = _matmul(al_b, dar_b, "tn", "d_a_lora", out_dtype=BF16)

    def shift_bwd(dzs_ref, mu_ref, dz_ref, carry):
        _first_step_zero(carry)
        d = dzs_ref[...]
        nbt = d.shape[0]
        nxt = pltpu.roll(d, nbt - 1, 0)
        rowi = lax.broadcasted_iota(jnp.int32, d.shape, 0)
        nxt = jnp.where(rowi == nbt - 1, carry[...], nxt)
        carry[...] = d[0:1, :]
        m = mu_ref[...]
        dz_ref[...] = (d * (1.0 - m) + nxt * m).astype(BF16)

    (dz_rw,) = _rowcall(shift_bwd, "shift_bwd", t_dim, bt, [dzs], [mu], [(sec, BF16)],
                        scratch=[pltpu.VMEM((1, sec), F32)], reverse=True)

    fox_parts = [dq_f.astype(BF16), dk_f.astype(BF16), dv_f.astype(BF16), dg_fox, dz_fl]
    if sec > 4 * c_fox + LANES:
        fox_parts.append(jnp.zeros((t_dim, sec - 4 * c_fox - LANES), BF16))
    dz = jnp.concatenate([dz_rw] + fox_parts + [dz_gate], axis=1)
    d_wz = _matmul(h, dz, "tn", "d_w_in", out_dtype=BF16, bn=1408)
    rw_cols, fox_cols = c4 + 2 * lora, 4 * c_fox + h_fox
    d_wi = jnp.concatenate([d_wz[:, :rw_cols], d_wz[:, sec:sec + fox_cols], d_wz[:, 2 * sec:2 * sec + 2 * d_model]], axis=1)
    late = {"w_in": d_wi, "rw_w_lora_up": d_wl, "rw_a_lora_up": d_wa}
    dh, late_recv = _matmul(dz, wz, "nt", "d_h", bk=sec,
                            exchange=_Exchange([_split_shards(g, axis[n]) for n, g in late.items()]))
    recv.update(zip(late, late_recv))

    def norm_in_bwd(dh_ref, x_ref, dx1_ref, g_ref, dx_ref, dg1_ref):
        _first_step_zero(dg1_ref)
        xv = x_ref[...]
        rms = lax.rsqrt(jnp.mean(xv * xv, axis=-1, keepdims=True) + NORM_EPS)
        xn = xv * rms
        d = dh_ref[...]
        dg1_ref[...] += _colsum(d * xn)
        gy = d * g_ref[...]
        dx_ref[...] = dx1_ref[...] + rms * (gy - xn * jnp.mean(xn * gy, axis=-1, keepdims=True))

    dx, d_g1 = _rowcall(norm_in_bwd, "norm_in_bwd", t_dim, bt, [dh, x, dx1], [norm_g], [(d_model, F32)],
                        acc_outs=[(1, d_model)])

    rep_grads = {"norm_g": d_g1, "rw_shift_mu": d_mu[:, 0:c4 + 2 * lora], "rw_w0": d_w0, "rw_a0": d_a0, "rw_k_k": d_kk,
                 "rw_k_a": d_ka, "rw_r_k": d_rk, "rw_ln_g": d_lng, "rw_ln_b": d_lnb, "fox_b_f": d_bf[0:h_fox, 0],
                 "ple_norm_g": d_g2, "final_norm_g": d_g3}
    return loss[0, 0], dx, recv, rep_grads


def kernel(x, p, norm_g, w_in, rw_shift_mu, rw_w0, rw_w_lora_up, rw_a0, rw_a_lora_up, rw_k_k, rw_k_a, rw_r_k, rw_ln_g, rw_ln_b, fox_b_f, w_up_rwkv, w_up_fox, w_out, ple_proj, ple_gate_w, ple_norm_g, final_norm_g, loss_target, m_norm_g, m_w_in, m_rw_shift_mu, m_rw_w0, m_rw_w_lora_up, m_rw_a0, m_rw_a_lora_up, m_rw_k_k, m_rw_k_a, m_rw_r_k, m_rw_ln_g, m_rw_ln_b, m_fox_b_f, m_w_up_rwkv, m_w_up_fox, m_w_out, m_ple_proj, m_ple_gate_w, m_ple_norm_g, m_final_norm_g, v_norm_g, v_w_in, v_rw_shift_mu, v_rw_w0, v_rw_w_lora_up, v_rw_a0, v_rw_a_lora_up, v_rw_k_k, v_rw_k_a, v_rw_r_k, v_rw_ln_g, v_rw_ln_b, v_fox_b_f, v_w_up_rwkv, v_w_up_fox, v_w_out, v_ple_proj, v_ple_gate_w, v_ple_norm_g, v_final_norm_g):
    args = locals()
    w = {n: args[n] for n in WEIGHTS}
    mom = {n: args["m_" + n] for n in WEIGHTS}
    var = {n: args["v_" + n] for n in WEIGHTS}

    t_dim, d_model = x.shape[1], x.shape[2]
    c_rw, lora = rw_w0.shape[1], rw_w_lora_up.shape[1]
    h_fox = fox_b_f.shape[1]
    c_fox = h_fox * HEAD
    rw_cols, fox_cols, gate_cols = 4 * c_rw + 2 * lora, 4 * c_fox + h_fox, 2 * d_model
    sec = max(rw_cols, 4 * c_fox + LANES, _round_up(gate_cols, LANES))
    assert c_rw % LANES == 0 and c_fox % LANES == 0 and rw_cols % LANES == 0 and h_fox <= LANES and d_model % LANES == 0
    assert rw_a_lora_up.shape[1] == lora and w_in.shape[2] * NDEV == rw_cols + fox_cols + gate_cols

    gathered = _all_gather([w[n][0].astype(BF16) for n, _ in SHARDED], "gather_weights")
    full = {n: _join_shards(g, ax) for (n, ax), g in zip(SHARDED, gathered)}

    def to_sections(wi):
        pad = lambda a, width: jnp.pad(a, ((0, 0), (0, width - a.shape[1])))
        return jnp.concatenate([pad(wi[:, :rw_cols], sec), pad(wi[:, rw_cols:rw_cols + fox_cols], sec),
                                pad(wi[:, rw_cols + fox_cols:], sec)], axis=1)

    rep = {n: w[n] for n in REPLICATED}
    dims = (t_dim, d_model, c_rw, lora, c_fox, h_fox, sec)
    loss_local, grad_x, recv, rep_grads = _local_step(
        x[0], p[0, 0], loss_target[0], to_sections(full["w_in"]), full["rw_w_lora_up"], full["rw_a_lora_up"],
        full["w_up_rwkv"], full["w_up_fox"], full["w_out"], full["ple_proj"], full["ple_gate_w"], rep, dims)

    rep_sizes = [w[n].size for n in REPLICATED]
    rep_offs = [sum(rep_sizes[:i]) for i in range(len(rep_sizes))]
    pack_rep = lambda tree: _pack_rows(jnp.concatenate([tree[n].astype(F32).reshape(-1) for n in REPLICATED]))
    (small_all,) = _grad_exchange([pack_rep(rep_grads)], "exchange_replicated")
    kinds = ("grad", "delta", "new_m", "new_v")
    outs = {}
    for n, _ in SHARDED:
        for kind, buf in zip(kinds, _adamw(recv[n], w[n][0], mom[n][0], var[n][0], "adamw_" + n)):
            outs[kind, n] = buf
    for kind, buf in zip(kinds, _adamw(small_all, pack_rep(w), pack_rep(mom), pack_rep(var), "adamw_replicated")):
        flat = buf.reshape(-1)
        for n, o, s in zip(REPLICATED, rep_offs, rep_sizes):
            outs[kind, n] = flat[o:o + s].reshape(w[n].shape)
    loss = lax.psum(loss_local, MESH_AXES)
    return (loss, grad_x[None], *[outs[kind, n] for kind in ("grad", "delta", "new_m", "new_v") for n in WEIGHTS])
```

```python
import functools
import math

import jax
import jax.numpy as jnp
from jax import lax
from jax.experimental import pallas as pl
from jax.experimental.pallas import tpu as pltpu

F32, BF16 = jnp.float32, jnp.bfloat16
HI = lax.Precision.HIGHEST
LANES = 128
SUBLANES = 8
HEAD = 64
NORM_EPS = 1e-6
GN_EPS = 64e-5
VMEM_LIMIT = 56 * 1024 * 1024
NDEV = 8
PACK_W = 1024
PACK_ALIGN = 16 * PACK_W
MESH_AXES = ("x", "y", "c")
MESH = pl.DeviceIdType.MESH

ADAM_LR, ADAM_B1, ADAM_B2, ADAM_EPS, ADAM_WD, ADAM_STEP = 0.001, 0.9, 0.999, 1e-08, 0.01, 10


def _round_up(n, m):
    return (n + m - 1) // m * m


def _pick(dim, pref, align=LANES):
    if dim <= pref:
        return dim
    best = None
    for cand in range(align, pref + 1, align):
        if dim % cand == 0:
            best = cand
    return dim if best is None else best


def _params(*sem):
    return pltpu.CompilerParams(dimension_semantics=sem, vmem_limit_bytes=VMEM_LIMIT)


def _sigmoid(v):
    return jax.nn.sigmoid(v)


def _log_sigmoid(v):
    return jnp.minimum(v, 0.0) - jnp.log(1.0 + jnp.exp(-jnp.abs(v)))


_DOT_DIMS = {"nn": (((1,), (0,)), ((), ())), "nt": (((1,), (1,)), ((), ())), "tn": (((0,), (0,)), ((), ()))}


def _matmul(a, b, mode, name, out_dtype=F32, bm=512, bn=1024, bk=2048, exchange=None):
    if mode == "tn":
        k_dim, m_dim = a.shape
    else:
        m_dim, k_dim = a.shape
    n_dim = b.shape[0] if mode == "nt" else b.shape[1]
    bm, bn, bk = _pick(m_dim, bm), _pick(n_dim, bn), _pick(k_dim, bk)
    nk = k_dim // bk

    nx = 0 if exchange is None else exchange.n
    grid = (m_dim // bm, n_dim // bn, nk)

    def body(*refs):
        a_ref, b_ref, o_ref = refs[0], refs[1], refs[2 + nx]
        step = [pl.program_id(d) for d in range(3)]
        if nx:
            own_scratch = 1 if nk > 1 else 0
            ex_refs = (refs[2:2 + nx], refs[3 + nx:3 + 2 * nx], refs[3 + 2 * nx + own_scratch:])
            linear = (step[0] * grid[1] + step[1]) * grid[2] + step[2]

            @pl.when(linear == 0)
            def _():
                exchange.start(*ex_refs)

            if hasattr(exchange, "forward"):
                @pl.when(linear == (grid[0] * grid[1] * grid[2]) // 2)
                def _():
                    exchange.forward(*ex_refs)

        prod = lax.dot_general(a_ref[...].astype(BF16), b_ref[...].astype(BF16), _DOT_DIMS[mode],
                               preferred_element_type=F32)
        if nk == 1:
            o_ref[...] = prod.astype(o_ref.dtype)
        else:
            acc_ref, k = refs[3 + 2 * nx], step[2]

            @pl.when(k == 0)
            def _():
                acc_ref[...] = prod

            @pl.when(k > 0)
            def _():
                acc_ref[...] += prod

            @pl.when(k == nk - 1)
            def _():
                o_ref[...] = acc_ref[...].astype(o_ref.dtype)
        if nx:
            @pl.when(linear == grid[0] * grid[1] * grid[2] - 1)
            def _():
                exchange.finish(*ex_refs)

    if mode == "tn":
        a_spec = pl.BlockSpec((bk, bm), lambda i, j, k: (k, i))
    else:
        a_spec = pl.BlockSpec((bm, bk), lambda i, j, k: (i, k))
    if mode == "nt":
        b_spec = pl.BlockSpec((bn, bk), lambda i, j, k: (j, k))
    else:
        b_spec = pl.BlockSpec((bk, bn), lambda i, j, k: (k, j))
    out_spec = pl.BlockSpec((bm, bn), lambda i, j, k: (i, j))
    out_shape = jax.ShapeDtypeStruct((m_dim, n_dim), out_dtype)
    acc_scratch = [pltpu.VMEM((bm, bn), F32)] if nk > 1 else []
    if exchange is None:
        return pl.pallas_call(
            body, name=name, grid=grid, in_specs=[a_spec, b_spec], out_specs=out_spec, out_shape=out_shape,
            scratch_shapes=acc_scratch, compiler_params=_params("parallel", "parallel", "arbitrary"),
        )(a, b)
    outs = pl.pallas_call(
        body, name=name, grid=grid, in_specs=[a_spec, b_spec] + [HBM_SPEC] * nx,
        out_specs=[out_spec] + [HBM_SPEC] * nx, out_shape=[out_shape] + exchange.out_shape,
        scratch_shapes=acc_scratch + exchange.scratch, compiler_params=_params("arbitrary", "arbitrary", "arbitrary"),
    )(a, b, *exchange.arrays)
    return outs[0], outs[1:]


def _rowcall(body, name, t_dim, bt, row_ins, const_ins, row_outs, acc_outs=(), scratch=(), reverse=False):
    nt = t_dim // bt

    def rmap(i):
        return nt - 1 - i if reverse else i

    in_specs, args = [], []
    for item in row_ins:
        arr, cb, w = item if isinstance(item, tuple) else (item, 0, item.shape[1])
        in_specs.append(pl.BlockSpec((bt, w), lambda i, cb=cb: (rmap(i), cb)))
        args.append(arr)
    for arr in const_ins:
        in_specs.append(pl.BlockSpec(arr.shape, lambda i, nd=arr.ndim: (0,) * nd))
        args.append(arr)
    out_specs = [pl.BlockSpec((bt, w), lambda i: (rmap(i), 0)) for w, _ in row_outs]
    out_shape = [jax.ShapeDtypeStruct((t_dim, w), dt) for w, dt in row_outs]
    for shp in acc_outs:
        out_specs.append(pl.BlockSpec(shp, lambda i, nd=len(shp): (0,) * nd))
        out_shape.append(jax.ShapeDtypeStruct(shp, F32))
    return pl.pallas_call(
        body, name=name, grid=(nt,), in_specs=in_specs, out_specs=out_specs, out_shape=out_shape,
        scratch_shapes=list(scratch), compiler_params=_params("arbitrary"),
    )(*args)


def _first_step_zero(*refs):
    @pl.when(pl.program_id(0) == 0)
    def _():
        for r in refs:
            r[...] = jnp.zeros_like(r)


def _colsum(v):
    return jnp.sum(v, axis=0, keepdims=True)


def _headsum(v, e):
    parts = [jnp.dot(v[:, p * LANES:(p + 1) * LANES], e, precision=HI, preferred_element_type=F32)
             for p in range(v.shape[1] // LANES)]
    return parts[0] if len(parts) == 1 else jnp.concatenate(parts, axis=1)


def _shift_down(v, carry_ref):
    bt = v.shape[0]
    prev = pltpu.roll(v, 1, 0)
    row = lax.broadcasted_iota(jnp.int32, v.shape, 0)
    prev = jnp.where(row == 0, carry_ref[...], prev)
    carry_ref[...] = v[bt - 1:bt, :]
    return prev


def _pair_consts():
    lane = lax.broadcasted_iota(jnp.int32, (1, LANES), 1)
    m0 = (lane < HEAD).astype(F32)
    m1 = 1.0 - m0
    sub = lax.broadcasted_iota(jnp.int32, (HEAD, LANES), 0)
    lane2 = lax.broadcasted_iota(jnp.int32, (HEAD, LANES), 1)
    i0 = (lane2 == sub).astype(F32)
    i1 = (lane2 == sub + HEAD).astype(F32)
    return m0, m1, i0, i1


def _head_masks():
    lane = lax.broadcasted_iota(jnp.int32, (1, LANES), 1)
    first = (lane < HEAD).astype(F32)
    return first, 1.0 - first


def _head_ones(dtype=BF16):
    lane = jnp.arange(LANES)
    return (lane[:, None] // HEAD == lane[None, :] // HEAD).astype(dtype)


def _lanesum(v):
    return jnp.sum(v, axis=1, keepdims=True)


def _split_bf16(v):
    hi = v.astype(BF16).astype(F32)
    rest = v - hi
    mid = rest.astype(BF16).astype(F32)
    return hi, mid, (rest - mid).astype(BF16).astype(F32)


def _col_pair(parts, j, i01, e_bf16):
    lhs = jnp.concatenate([(part[j:j + 1] * i01).astype(BF16) for part in parts], axis=0)
    out = jnp.dot(lhs, e_bf16, preferred_element_type=F32)
    return (out[0:HEAD] + out[HEAD:2 * HEAD]) + out[2 * HEAD:3 * HEAD]


def _col_tiles(out_ref, src_ref, rows, lanes, i01, e_bf16):
    for sl in lanes:
        parts = _split_bf16(src_ref[rows, sl])
        for j in range(SUBLANES):
            out_ref[j, :, sl] = _col_pair(parts, j, i01, e_bf16)


def _row_pair(c0, c1, i0, i1):
    return _colsum(c0 * i0 + c1 * i1)


def _scan_fwd(r, w, k, v, kk, a, tc, npb):
    t_dim, c_dim = r.shape
    wb = LANES * npb
    tc = _pick(t_dim, tc, SUBLANES)

    def body(r_ref, w_ref, k_ref, v_ref, kk_ref, a_ref, e_ref, y_ref, sall_ref, sfin_ref, s_ref):
        @pl.when(pl.program_id(1) == 0)
        def _():
            s_ref[...] = jnp.zeros_like(s_ref)

        m0, m1, i0, i1 = _pair_consts()
        i01 = i0 + i1
        e = e_ref[...]
        sub8 = lax.broadcasted_iota(jnp.int32, (SUBLANES, LANES), 0)
        lanes = [slice(q * LANES, (q + 1) * LANES) for q in range(npb)]
        ng = tc // SUBLANES

        def lanesums(tiles):
            sums = _lanesum(jnp.concatenate(tiles, axis=0))
            return [sums[i * HEAD:(i + 1) * HEAD] for i in range(len(tiles))]

        def halves(s, row):
            return [s * (row * m0), s * (row * m1)]

        def group(gi, carry):
            base = pl.multiple_of(gi * SUBLANES, SUBLANES)
            rows = pl.ds(base, SUBLANES)
            s = list(carry)
            r8 = [r_ref[rows, sl] for sl in lanes]
            w8 = [w_ref[rows, sl] for sl in lanes]
            k8 = [k_ref[rows, sl] for sl in lanes]
            kk8 = [kk_ref[rows, sl] for sl in lanes]
            b8 = [kk8[q] * a_ref[rows, lanes[q]] for q in range(npb)]
            v8 = [_split_bf16(v_ref[rows, sl]) for sl in lanes]
            y8 = [jnp.zeros((SUBLANES, LANES), F32)] * npb
            for j in range(SUBLANES + 1):
                one, before = slice(j, j + 1), slice(j - 1, j)
                tiles = []
                for q in range(npb):
                    if j < SUBLANES:
                        tiles += halves(s[q], kk8[q][one])
                    if j > 0:
                        tiles += halves(s[q], r8[q][before])
                cols = lanesums(tiles)
                per = len(tiles) // npb
                for q in range(npb):
                    mine = cols[q * per:(q + 1) * per]
                    if j > 0:
                        y8[q] = jnp.where(sub8 == j - 1, _row_pair(mine[-2], mine[-1], i0, i1), y8[q])
                    if j < SUBLANES:
                        sall_ref[base + j, :, lanes[q]] = s[q]
                        sb = mine[0] * m0 + mine[1] * m1
                        s[q] = s[q] * w8[q][one] - sb * b8[q][one] + _col_pair(v8[q], j, i01, e) * k8[q][one]
            for q in range(npb):
                y_ref[rows, lanes[q]] = y8[q]
            return tuple(s)

        init = tuple(s_ref[:, q * LANES:(q + 1) * LANES] for q in range(npb))
        fin = lax.fori_loop(0, ng, group, init)
        for q in range(npb):
            s_ref[:, q * LANES:(q + 1) * LANES] = fin[q]
            sfin_ref[:, q * LANES:(q + 1) * LANES] = fin[q]

    row = pl.BlockSpec((tc, wb), lambda p, c: (c, p))
    return pl.pallas_call(
        body, name="rwkv_scan_fwd", grid=(c_dim // wb, t_dim // tc),
        in_specs=[row] * 6 + [pl.BlockSpec((LANES, LANES), lambda p, c: (0, 0))],
        out_specs=[row, pl.BlockSpec((tc, HEAD, wb), lambda p, c: (c, 0, p)), pl.BlockSpec((HEAD, wb), lambda p, c: (0, p))],
        out_shape=[jax.ShapeDtypeStruct((t_dim, c_dim), F32), jax.ShapeDtypeStruct((t_dim, HEAD, c_dim), F32),
                   jax.ShapeDtypeStruct((HEAD, c_dim), F32)],
        scratch_shapes=[pltpu.VMEM((HEAD, wb), F32)],
        compiler_params=_params("parallel", "arbitrary"),
    )(r, w, k, v, kk, a, _head_ones())


def _scan_bwd(r, w, k, v, kk, a, dy, sall, sfin, tc, npb, exchange):
    t_dim, c_dim = r.shape
    wb = LANES * npb
    tc = _pick(t_dim, tc, SUBLANES)
    nc = t_dim // tc
    nx = exchange.n
    n_blocks = c_dim // wb

    def body(*refs):
        r_ref, w_ref, k_ref, v_ref, kk_ref, a_ref, dy_ref, sall_ref, sfin_ref, e_ref = refs[:10]
        dr_ref, dw_ref, dk_ref, dv_ref, dkk_ref, da_ref = refs[10 + nx:16 + nx]
        ds_ref, sn_ref = refs[16 + 2 * nx:18 + 2 * nx]
        here_p, here_c = pl.program_id(0), pl.program_id(1)
        ex_refs = (refs[10:10 + nx], refs[16 + nx:16 + 2 * nx], refs[18 + 2 * nx:])

        @pl.when((here_p == 0) & (here_c == 0))
        def _():
            exchange.start(*ex_refs)

        @pl.when(pl.program_id(1) == 0)
        def _():
            ds_ref[...] = jnp.zeros_like(ds_ref)
            sn_ref[...] = sfin_ref[...]

        m0, m1, i0, i1 = _pair_consts()
        i01 = i0 + i1
        e = e_ref[...]
        sub8 = lax.broadcasted_iota(jnp.int32, (SUBLANES, LANES), 0)
        lanes = [slice(q * LANES, (q + 1) * LANES) for q in range(npb)]
        ng = tc // SUBLANES

        def halves(s, row):
            return [s * (row * m0), s * (row * m1)]

        def group(gi, carry):
            base = pl.multiple_of((ng - 1 - gi) * SUBLANES, SUBLANES)
            rows = pl.ds(base, SUBLANES)
            ds = list(carry)
            r8 = [r_ref[rows, sl] for sl in lanes]
            w8 = [w_ref[rows, sl] for sl in lanes]
            k8 = [k_ref[rows, sl] for sl in lanes]
            kk8 = [kk_ref[rows, sl] for sl in lanes]
            a8 = [a_ref[rows, sl] for sl in lanes]
            v8 = [_split_bf16(v_ref[rows, sl]) for sl in lanes]
            dy8 = [_split_bf16(dy_ref[rows, sl]) for sl in lanes]
            zero8 = jnp.zeros((SUBLANES, LANES), F32)
            dr8, dw8, dk8, dv8, dkk8, da8 = ([zero8] * npb for _ in range(6))
            for j in reversed(range(SUBLANES)):
                one = slice(j, j + 1)
                here = sub8 == j
                d, s_prev, tiles = [], [], []
                for q in range(npb):
                    dyb = _col_pair(dy8[q], j, i01, e)
                    s_prev.append(sall_ref[base + j, :, lanes[q]])
                    dr8[q] = jnp.where(here, _colsum(sn_ref[:, lanes[q]] * dyb), dr8[q])
                    sn_ref[:, lanes[q]] = s_prev[q]
                    d.append(ds[q] + dyb * r8[q][one])
                for q in range(npb):
                    tiles += halves(d[q], kk8[q][one] * a8[q][one])
                for q in range(npb):
                    tiles += halves(d[q], k8[q][one]) + halves(s_prev[q], kk8[q][one])
                sums = _lanesum(jnp.concatenate(tiles, axis=0))
                cols = [sums[i * HEAD:(i + 1) * HEAD] for i in range(len(tiles))]
                for q in range(npb):
                    kkr, ar = kk8[q][one], a8[q][one]
                    dsb = -(cols[2 * q] * m0 + cols[2 * q + 1] * m1)
                    ds[q] = d[q] * w8[q][one] + dsb * kkr
                    rest = cols[2 * npb + 4 * q:2 * npb + 4 * q + 4]
                    dv8[q] = jnp.where(here, _row_pair(rest[0], rest[1], i0, i1), dv8[q])
                    sb = rest[2] * m0 + rest[3] * m1
                    db = -_colsum(d[q] * sb)
                    dk8[q] = jnp.where(here, _colsum(d[q] * _col_pair(v8[q], j, i01, e)), dk8[q])
                    dw8[q] = jnp.where(here, _colsum(d[q] * s_prev[q]), dw8[q])
                    dkk8[q] = jnp.where(here, _colsum(s_prev[q] * dsb) + db * ar, dkk8[q])
                    da8[q] = jnp.where(here, db * kkr, da8[q])
            for q in range(npb):
                sl = lanes[q]
                dr_ref[rows, sl], dw_ref[rows, sl], dk_ref[rows, sl] = dr8[q], dw8[q], dk8[q]
                dv_ref[rows, sl], dkk_ref[rows, sl], da_ref[rows, sl] = dv8[q], dkk8[q], da8[q]
            return tuple(ds)

        init = tuple(ds_ref[:, q * LANES:(q + 1) * LANES] for q in range(npb))
        fin = lax.fori_loop(0, ng, group, init)
        for q in range(npb):
            ds_ref[:, q * LANES:(q + 1) * LANES] = fin[q]

        @pl.when((here_p == n_blocks - 1) & (here_c == nc - 1))
        def _():
            exchange.finish(*ex_refs)

    row = pl.BlockSpec((tc, wb), lambda p, c: (nc - 1 - c, p))
    outs = pl.pallas_call(
        body, name="rwkv_scan_bwd", grid=(n_blocks, nc),
        in_specs=[row] * 7 + [pl.BlockSpec((tc, HEAD, wb), lambda p, c: (nc - 1 - c, 0, p)),
                              pl.BlockSpec((HEAD, wb), lambda p, c: (0, p)),
                              pl.BlockSpec((LANES, LANES), lambda p, c: (0, 0))] + [HBM_SPEC] * nx,
        out_specs=[row] * 6 + [HBM_SPEC] * nx,
        out_shape=[jax.ShapeDtypeStruct((t_dim, c_dim), F32)] * 6 + exchange.out_shape,
        scratch_shapes=[pltpu.VMEM((HEAD, wb), F32)] * 2 + exchange.scratch,
        compiler_params=_params("arbitrary", "arbitrary"),
    )(r, w, k, v, kk, a, dy, sall, sfin, _head_ones(), *exchange.arrays)
    return outs[:6], outs[6:]


def _fox_fwd(z, ct, cq, q_cb, k_cb, v_cb, n_pairs, blk):
    t_dim = z.shape[0]
    blk = _pick(t_dim, blk)
    nq = t_dim // blk
    scale = HEAD ** -0.5

    def body(q_ref, k_ref, v_ref, ct_ref, cq_ref, o_ref, lse_ref):
        i = pl.program_id(1)
        rowi = lax.broadcasted_iota(jnp.int32, (blk, blk), 0)
        coli = lax.broadcasted_iota(jnp.int32, (blk, blk), 1)
        masks = _head_masks()
        qv = q_ref[...]
        qs = [(qv * mk).astype(BF16) for mk in masks]
        cqs = [cq_ref[:, hh:hh + 1] for hh in range(2)]

        def kv_step(j, carry, masked):
            rows = pl.ds(pl.multiple_of(j * blk, blk), blk)
            kb = k_ref[rows, :].astype(BF16)
            vv = v_ref[rows, :]
            stats, acc = list(carry[:4]), carry[4]
            rescale, add = 0.0, 0.0
            for hh in range(2):
                m, l = stats[2 * hh], stats[2 * hh + 1]
                s = (lax.dot_general(qs[hh], kb, _DOT_DIMS["nt"], preferred_element_type=F32) * scale
                     + (cqs[hh] - ct_ref[hh:hh + 1, rows]))
                if masked:
                    s = jnp.where(rowi >= coli, s, -jnp.inf)
                m_new = jnp.maximum(m, jnp.max(s, axis=1, keepdims=True))
                alpha = jnp.exp(m - m_new)
                pr = jnp.exp(s - m_new)
                stats[2 * hh], stats[2 * hh + 1] = m_new, l * alpha + jnp.sum(pr, axis=1, keepdims=True)
                rescale = rescale + alpha * masks[hh]
                hi = pr.astype(BF16)
                both = jnp.concatenate([hi, (pr - hi.astype(F32)).astype(BF16)], axis=1)
                vh = (vv * masks[hh]).astype(BF16)
                add = add + jnp.dot(both, jnp.concatenate([vh, vh], axis=0), preferred_element_type=F32)
            return (*stats, acc * rescale + add)

        neg, zero = jnp.full((blk, 1), -jnp.inf, F32), jnp.zeros((blk, 1), F32)
        carry = lax.fori_loop(0, i, functools.partial(kv_step, masked=False),
                              (neg, zero, neg, zero, jnp.zeros((blk, LANES), F32)))
        m0, l0, m1, l1, acc = kv_step(i, carry, True)
        o_ref[...] = acc * (masks[0] / l0 + masks[1] / l1)
        lse_ref[:, 0:1] = m0 + jnp.log(l0)
        lse_ref[:, 1:2] = m1 + jnp.log(l1)

    full = lambda cb: pl.BlockSpec((t_dim, LANES), lambda p, i, cb=cb: (0, cb + p))
    return pl.pallas_call(
        body, name="fox_attn_fwd", grid=(n_pairs, nq),
        in_specs=[pl.BlockSpec((blk, LANES), lambda p, i: (i, q_cb + p)), full(k_cb), full(v_cb),
                  pl.BlockSpec((None, SUBLANES, t_dim), lambda p, i: (p, 0, 0)),
                  pl.BlockSpec((None, blk, 2), lambda p, i: (p, i, 0))],
        out_specs=[pl.BlockSpec((blk, LANES), lambda p, i: (i, p)), pl.BlockSpec((None, blk, 2), lambda p, i: (p, i, 0))],
        out_shape=[jax.ShapeDtypeStruct((t_dim, n_pairs * LANES), F32), jax.ShapeDtypeStruct((n_pairs, t_dim, 2), F32)],
        compiler_params=_params("parallel", "arbitrary"),
    )(z, z, z, ct, cq)


def _fox_rowdot(z, ct, cq, do, lse, q_cb, k_cb, v_cb, n_pairs, blk):
    t_dim = z.shape[0]
    blk = _pick(t_dim, blk)
    scale = HEAD ** -0.5

    def body(q_ref, k_ref, v_ref, ct_ref, cq_ref, do_ref, lse_ref, out_ref):
        i = pl.program_id(1)
        rowi = lax.broadcasted_iota(jnp.int32, (blk, blk), 0)
        coli = lax.broadcasted_iota(jnp.int32, (blk, blk), 1)
        masks = _head_masks()
        qv, dov = q_ref[...], do_ref[...]
        qs = [(qv * mk).astype(BF16) for mk in masks]
        dos = [(dov * mk).astype(BF16) for mk in masks]
        cqs = [cq_ref[:, hh:hh + 1] for hh in range(2)]
        lses = [lse_ref[:, hh:hh + 1] for hh in range(2)]

        def kv_step(j, carry, masked):
            rows = pl.ds(pl.multiple_of(j * blk, blk), blk)
            kb = k_ref[rows, :].astype(BF16)
            vb = v_ref[rows, :].astype(BF16)
            out = []
            for hh in range(2):
                num, den = carry[2 * hh], carry[2 * hh + 1]
                s = (lax.dot_general(qs[hh], kb, _DOT_DIMS["nt"], preferred_element_type=F32) * scale
                     + (cqs[hh] - ct_ref[hh:hh + 1, rows]))
                pr = jnp.exp(s - lses[hh])
                if masked:
                    pr = jnp.where(rowi >= coli, pr, 0.0)
                dp = lax.dot_general(dos[hh], vb, _DOT_DIMS["nt"], preferred_element_type=F32)
                out += [num + jnp.sum(pr * dp, axis=1, keepdims=True), den + jnp.sum(pr, axis=1, keepdims=True)]
            return tuple(out)

        zero = jnp.zeros((blk, 1), F32)
        carry = lax.fori_loop(0, i, functools.partial(kv_step, masked=False), (zero, zero, zero, zero))
        num0, den0, num1, den1 = kv_step(i, carry, True)
        out_ref[:, 0:1] = num0 / den0
        out_ref[:, 1:2] = num1 / den1

    full = lambda cb: pl.BlockSpec((t_dim, LANES), lambda p, i, cb=cb: (0, cb + p))
    return pl.pallas_call(
        body, name="fox_attn_rowdot", grid=(n_pairs, t_dim // blk),
        in_specs=[pl.BlockSpec((blk, LANES), lambda p, i: (i, q_cb + p)), full(k_cb), full(v_cb),
                  pl.BlockSpec((None, SUBLANES, t_dim), lambda p, i: (p, 0, 0)),
                  pl.BlockSpec((None, blk, 2), lambda p, i: (p, i, 0)),
                  pl.BlockSpec((blk, LANES), lambda p, i: (i, p)),
                  pl.BlockSpec((None, blk, 2), lambda p, i: (p, i, 0))],
        out_specs=pl.BlockSpec((None, blk, 2), lambda p, i: (p, i, 0)),
        out_shape=jax.ShapeDtypeStruct((n_pairs, t_dim, 2), F32),
        compiler_params=_params("parallel", "arbitrary"),
    )(z, z, z, ct, cq, do, lse)


def _fox_bwd(z, ct, cq, rowdot, do, lse, q_cb, k_cb, v_cb, n_pairs, blk):
    t_dim = z.shape[0]
    blk = _pick(t_dim, blk)
    nb = t_dim // blk
    scale = HEAD ** -0.5

    def body(q_ref, k_ref, v_ref, ct_ref, cq_ref, rd_ref, do_ref, lse_ref, dq_ref, dk_ref, dv_ref, dc_ref):
        j = pl.program_id(1)

        @pl.when(j == 0)
        def _():
            dq_ref[...] = jnp.zeros_like(dq_ref)

        rowi = lax.broadcasted_iota(jnp.int32, (blk, blk), 0)
        coli = lax.broadcasted_iota(jnp.int32, (blk, blk), 1)
        krows = pl.ds(pl.multiple_of(j * blk, blk), blk)
        masks = _head_masks()
        kv, vb = k_ref[...], v_ref[...].astype(BF16)
        kb = kv.astype(BF16)
        ks = [(kv * mk).astype(BF16) for mk in masks]
        cks = [ct_ref[hh:hh + 1, krows] for hh in range(2)]

        def q_step(i, carry, masked):
            dk, dv, dcs = carry[0], carry[1], list(carry[2:])
            rows = pl.ds(pl.multiple_of(i * blk, blk), blk)
            qv, dov = q_ref[rows, :], do_ref[rows, :]
            dq = 0.0
            for hh in range(2):
                qh = (qv * masks[hh]).astype(BF16)
                doh = (dov * masks[hh]).astype(BF16)
                s = (lax.dot_general(qh, kb, _DOT_DIMS["nt"], preferred_element_type=F32) * scale
                     + (cq_ref[rows, hh:hh + 1] - cks[hh]))
                pr = jnp.exp(s - lse_ref[rows, hh:hh + 1])
                if masked:
                    pr = jnp.where(rowi >= coli, pr, 0.0)
                dv = dv + lax.dot_general(pr.astype(BF16), doh, _DOT_DIMS["tn"], preferred_element_type=F32)
                dp = lax.dot_general(doh, vb, _DOT_DIMS["nt"], preferred_element_type=F32)
                ds = pr * (dp - rd_ref[rows, hh * HEAD:hh * HEAD + 1])
                dsb = ds.astype(BF16)
                dq = dq + jnp.dot(dsb, ks[hh], preferred_element_type=F32)
                dk = dk + lax.dot_general(dsb, qh, _DOT_DIMS["tn"], preferred_element_type=F32)
                dcs[hh] = dcs[hh] - _colsum(ds)
            dq_ref[rows, :] += dq * scale
            return (dk, dv, *dcs)

        zero_row = jnp.zeros((1, blk), F32)
        init = (jnp.zeros((blk, LANES), F32), jnp.zeros((blk, LANES), F32), zero_row, zero_row)
        carry = q_step(j, init, True)
        dk, dv, dc0, dc1 = lax.fori_loop(j + 1, nb, functools.partial(q_step, masked=False), carry)
        dk_ref[...] = dk * scale
        dv_ref[...] = dv
        dc_ref[0:1, :] = dc0
        dc_ref[1:2, :] = dc1
        dc_ref[2:SUBLANES, :] = jnp.zeros((SUBLANES - 2, blk), F32)

    full = lambda: pl.BlockSpec((t_dim, LANES), lambda p, j: (0, p))
    blkspec = pl.BlockSpec((blk, LANES), lambda p, j: (j, p))
    return pl.pallas_call(
        body, name="fox_attn_bwd", grid=(n_pairs, nb),
        in_specs=[pl.BlockSpec((t_dim, LANES), lambda p, j: (0, q_cb + p)),
                  pl.BlockSpec((blk, LANES), lambda p, j: (j, k_cb + p)),
                  pl.BlockSpec((blk, LANES), lambda p, j: (j, v_cb + p)),
                  pl.BlockSpec((None, SUBLANES, t_dim), lambda p, j: (p, 0, 0)),
                  pl.BlockSpec((None, t_dim, 2), lambda p, j: (p, 0, 0)), full(), full(),
                  pl.BlockSpec((None, t_dim, 2), lambda p, j: (p, 0, 0))],
        out_specs=[full(), blkspec, blkspec, pl.BlockSpec((None, SUBLANES, blk), lambda p, j: (p, 0, j))],
        out_shape=[jax.ShapeDtypeStruct((t_dim, n_pairs * LANES), F32)] * 3
                  + [jax.ShapeDtypeStruct((n_pairs, SUBLANES, t_dim), F32)],
        compiler_params=_params("parallel", "arbitrary"),
    )(z, z, z, ct, cq, rowdot, do, lse)


HBM_SPEC = pl.BlockSpec(memory_space=pltpu.HBM)


class _Gather:
    def __init__(self, shards):
        self.arrays = list(shards)
        self.n = len(self.arrays)
        self.out_shape = [jax.ShapeDtypeStruct((NDEV,) + s.shape, s.dtype) for s in self.arrays]
        self.scratch = [pltpu.SemaphoreType.DMA((7 * self.n,)), pltpu.SemaphoreType.DMA((7 * self.n,)),
                        pltpu.SemaphoreType.DMA((self.n,))]

    def _plan(self, x_refs, out_refs, sems):
        send_sems, recv_sems, local_sems = sems
        x, y, c = lax.axis_index("x"), lax.axis_index("y"), lax.axis_index("c")
        me, sibling = (x, y, c), (x, y, 1 - c)
        chips = [(1 - x, y), (x, 1 - y), (1 - x, 1 - y)]

        def copy(a, k, block, to, from_input=False):
            px, py, pc = block
            slot = out_refs[a].at[4 * px + 2 * py + pc]
            return pltpu.make_async_remote_copy(
                src_ref=x_refs[a] if from_input else slot, dst_ref=slot,
                send_sem=send_sems.at[7 * a + k], recv_sem=recv_sems.at[7 * a + k], device_id=to, device_id_type=MESH)

        mine = [pltpu.make_async_copy(x_refs[a], out_refs[a].at[4 * x + 2 * y + c], local_sems.at[a])
                for a in range(self.n)]
        first = []
        for a in range(self.n):
            first += [copy(a, 1 + j, me, (*chip, c), from_input=True) for j, chip in enumerate(chips)]
            first.append(copy(a, 0, me, sibling, from_input=True))
        over_ici = [copy(a, 1 + j, (*chip, c), me) for a in range(self.n) for j, chip in enumerate(chips)]
        passed = [copy(a, 4 + j, (*chip, c), sibling) for a in range(self.n) for j, chip in enumerate(chips)]
        from_sibling = []
        for a in range(self.n):
            from_sibling.append(copy(a, 0, sibling, me))
            from_sibling += [copy(a, 4 + j, (*chip, 1 - c), me) for j, chip in enumerate(chips)]
        return mine, first, over_ici, passed, from_sibling

    def start(self, *refs):
        mine, first, _, _, _ = self._plan(*refs)
        for cp in mine + first:
            cp.start()

    def forward(self, *refs):
        _, _, over_ici, passed, _ = self._plan(*refs)
        for arrived, onward in zip(over_ici, passed):
            arrived.wait_recv()
            onward.start()

    def finish(self, *refs):
        mine, first, _, passed, from_sibling = self._plan(*refs)
        for cp in from_sibling:
            cp.wait_recv()
        for cp in first + passed:
            cp.wait_send()
        for cp in mine:
            cp.wait()


def _all_gather(shards, name):
    ga = _Gather(shards)

    def body(*refs):
        parts = (refs[:ga.n], refs[ga.n:2 * ga.n], refs[2 * ga.n:])
        ga.start(*parts)
        ga.forward(*parts)
        ga.finish(*parts)

    return pl.pallas_call(body, name=name, out_shape=ga.out_shape, in_specs=[HBM_SPEC] * ga.n,
                          out_specs=[HBM_SPEC] * ga.n, scratch_shapes=ga.scratch)(*ga.arrays)


class _Exchange:
    def __init__(self, arrays):
        self.arrays = list(arrays)
        self.n = len(self.arrays)
        self.per_dest = [a.ndim == 3 for a in self.arrays]
        self.out_shape = [jax.ShapeDtypeStruct(a.shape if pd else (NDEV,) + a.shape, a.dtype)
                          for a, pd in zip(self.arrays, self.per_dest)]
        self.scratch = [pltpu.SemaphoreType.DMA((7 * self.n,)), pltpu.SemaphoreType.DMA((7 * self.n,)),
                        pltpu.SemaphoreType.DMA((self.n,))]

    def _copies(self, in_refs, out_refs, sems):
        send_sems, recv_sems, local_sems = sems
        x, y, c = lax.axis_index("x"), lax.axis_index("y"), lax.axis_index("c")
        me = 4 * x + 2 * y + c
        own, sends, recvs = [], [], []
        for a in range(self.n):
            mine = in_refs[a].at[me] if self.per_dest[a] else in_refs[a]
            own.append(pltpu.make_async_copy(mine, out_refs[a].at[me], local_sems.at[a]))
            for k in range(1, NDEV):
                px = 1 - x if k & 4 else x
                py = 1 - y if k & 2 else y
                pc = 1 - c if k & 1 else c
                peer = 4 * px + 2 * py + pc
                sem = dict(send_sem=send_sems.at[7 * a + k - 1], recv_sem=recv_sems.at[7 * a + k - 1],
                           device_id=(px, py, pc), device_id_type=MESH)
                src = in_refs[a].at[peer] if self.per_dest[a] else in_refs[a]
                sends.append(pltpu.make_async_remote_copy(src_ref=src, dst_ref=out_refs[a].at[me], **sem))
                recvs.append(pltpu.make_async_remote_copy(src_ref=src, dst_ref=out_refs[a].at[peer], **sem))
        return own, sends, recvs

    def start(self, in_refs, out_refs, sems):
        own, sends, _ = self._copies(in_refs, out_refs, sems)
        for cp in own + sends:
            cp.start()

    def finish(self, in_refs, out_refs, sems):
        own, sends, recvs = self._copies(in_refs, out_refs, sems)
        for cp in recvs:
            cp.wait_recv()
        for cp in sends:
            cp.wait_send()
        for cp in own:
            cp.wait()


def _grad_exchange(arrays, name):
    ex = _Exchange(arrays)

    def body(*refs):
        in_refs, out_refs, sems = refs[:ex.n], refs[ex.n:2 * ex.n], refs[2 * ex.n:]
        ex.start(in_refs, out_refs, sems)
        ex.finish(in_refs, out_refs, sems)

    return pl.pallas_call(body, name=name, out_shape=ex.out_shape, in_specs=[HBM_SPEC] * ex.n,
                          out_specs=[HBM_SPEC] * ex.n, scratch_shapes=ex.scratch)(*ex.arrays)


def _adamw(partials, w, m, v, name):
    rows, width = w.shape
    br = _pick(rows, 128, 2 * SUBLANES)

    def body(p_ref, w_ref, m_ref, v_ref, g_out, d_out, m_out, v_out):
        g = p_ref[0].astype(F32)
        for d in range(1, NDEV):
            g = g + p_ref[d].astype(F32)
        mn = ADAM_B1 * m_ref[...] + (1.0 - ADAM_B1) * g
        vn = ADAM_B2 * v_ref[...] + (1.0 - ADAM_B2) * jnp.square(g)
        m_hat = mn / (1.0 - ADAM_B1 ** ADAM_STEP)
        v_hat = vn / (1.0 - ADAM_B2 ** ADAM_STEP)
        g_out[...] = g
        d_out[...] = -ADAM_LR * (m_hat / (jnp.sqrt(v_hat) + ADAM_EPS) + ADAM_WD * w_ref[...])
        m_out[...] = mn
        v_out[...] = vn

    blk = pl.BlockSpec((br, width), lambda i: (i, 0))
    return pl.pallas_call(
        body, name=name, grid=(rows // br,),
        in_specs=[pl.BlockSpec((NDEV, br, width), lambda i: (0, i, 0)), blk, blk, blk],
        out_specs=[pl.BlockSpec((None, br, width), lambda i: (0, i, 0))] * 4,
        out_shape=[jax.ShapeDtypeStruct((1, rows, width), F32)] * 4,
        compiler_params=_params("parallel"),
    )(partials, w, m, v)


def _pack_rows(flat):
    n = flat.shape[0]
    padded = _round_up(n, PACK_ALIGN)
    return jnp.pad(flat, (0, padded - n)).reshape(padded // PACK_W, PACK_W)


def _split_shards(full, axis):
    rows, cols = full.shape
    if axis == 0:
        return full.reshape(NDEV, rows // NDEV, cols)
    width = cols // NDEV
    return jnp.stack([full[:, d * width:(d + 1) * width] for d in range(NDEV)])


def _join_shards(blocks, axis):
    if axis == 0:
        return blocks.reshape(-1, blocks.shape[2])
    return jnp.concatenate([blocks[d] for d in range(NDEV)], axis=1)


SHARDED = (("w_in", 1), ("rw_w_lora_up", 1), ("rw_a_lora_up", 1), ("w_up_rwkv", 1), ("w_up_fox", 1),
           ("w_out", 0), ("ple_proj", 1), ("ple_gate_w", 0))
REPLICATED = ("norm_g", "rw_shift_mu", "rw_w0", "rw_a0", "rw_k_k", "rw_k_a", "rw_r_k", "rw_ln_g", "rw_ln_b",
              "fox_b_f", "ple_norm_g", "final_norm_g")
WEIGHTS = ("norm_g", "w_in", "rw_shift_mu", "rw_w0", "rw_w_lora_up", "rw_a0", "rw_a_lora_up", "rw_k_k", "rw_k_a",
           "rw_r_k", "rw_ln_g", "rw_ln_b", "fox_b_f", "w_up_rwkv", "w_up_fox", "w_out", "ple_proj", "ple_gate_w",
           "ple_norm_g", "final_norm_g")


def _local_step(x, p, tgt, wz, other_shards, rep, dims):
    t_dim, d_model, c_rw, lora, c_fox, h_fox, sec = dims
    bt = _pick(t_dim, 256, 2 * SUBLANES)
    bt_many = _pick(t_dim, 128, 2 * SUBLANES)
    n_pairs = c_fox // LANES
    row = lambda a: a.reshape(1, -1)
    norm_g, mu, w0, a0 = row(rep["norm_g"]), row(rep["rw_shift_mu"]), row(rep["rw_w0"]), row(rep["rw_a0"])
    k_k, k_a, r_k = row(rep["rw_k_k"]), row(rep["rw_k_a"]), row(rep["rw_r_k"])
    ln_g, ln_b = row(rep["rw_ln_g"]), row(rep["rw_ln_b"])
    g2, g3 = row(rep["ple_norm_g"]), row(rep["final_norm_g"])
    b_f = jnp.pad(row(rep["fox_b_f"]), ((0, 0), (0, LANES - h_fox)))
    e_head = _head_ones(F32)
    c4 = 4 * c_rw
    inv_d = 1.0 / d_model
    decay_k = math.exp(-0.5)

    def norm_in(x_ref, g_ref, h_ref):
        xv = x_ref[...]
        rms = lax.rsqrt(jnp.mean(xv * xv, axis=-1, keepdims=True) + NORM_EPS)
        h_ref[...] = (xv * rms * g_ref[...]).astype(BF16)

    (h,) = _rowcall(norm_in, "norm_in", t_dim, bt, [x], [norm_g], [(d_model, BF16)])
    z, gathered = _matmul(h, wz, "nn", "proj_in", bn=1408, exchange=_Gather(other_shards))
    wl, wa, wur, wuf, wo, pp, pg = [_join_shards(g, ax) for (_, ax), g in zip(SHARDED[1:], gathered)]

    def rw_values(zs, w0_ref, wl_ref, a0_ref, wa_ref, kk_ref, ka_ref, e_ref):
        k = zs[:, c_rw:2 * c_rw]
        tw = jnp.tanh(zs[:, c4:c4 + lora])
        al = zs[:, c4 + lora:c4 + 2 * lora]
        sw = _sigmoid(w0_ref[...] + jnp.dot(tw.astype(BF16), wl_ref[...], preferred_element_type=F32))
        decay = jnp.exp(-decay_k * sw)
        a = _sigmoid(a0_ref[...] + jnp.dot(al.astype(BF16), wa_ref[...], preferred_element_type=F32))
        kk0 = k * kk_ref[...]
        nrm = jnp.sqrt(_headsum(kk0 * kk0, e_ref[...]))
        inv = 1.0 / jnp.maximum(nrm, 1e-12)
        k2 = k * (1.0 + (a - 1.0) * ka_ref[...])
        return k, tw, al, sw, decay, a, kk0, nrm, inv, k2

    def rw_prep(z_ref, mu_ref, w0_ref, wl_ref, a0_ref, wa_ref, kk_ref, ka_ref, e_ref,
                r_o, w_o, k_o, v_o, kk_o, a_o, g_o, carry):
        _first_step_zero(carry)
        zv = z_ref[...]
        zs = zv + (_shift_down(zv, carry) - zv) * mu_ref[...]
        k, tw, al, sw, decay, a, kk0, nrm, inv, k2 = rw_values(zs, w0_ref, wl_ref, a0_ref, wa_ref, kk_ref, ka_ref, e_ref)
        r_o[...] = zs[:, 0:c_rw]
        w_o[...] = decay
        k_o[...] = k2
        v_o[...] = zs[:, 2 * c_rw:3 * c_rw]
        kk_o[...] = kk0 * inv
        a_o[...] = a
        g_o[...] = zs[:, 3 * c_rw:c4]

    rw_consts = [mu, w0, wl, a0, wa, k_k, k_a, e_head]
    r_s, w_s, k_s, v_s, kk_s, a_s, g_s = _rowcall(
        rw_prep, "rwkv_prep", t_dim, bt, [(z, 0, sec)], rw_consts, [(c_rw, F32)] * 7,
        scratch=[pltpu.VMEM((1, sec), F32)])
    pairs_fwd = max(n for n in (1, 2, 4) if c_rw % (n * LANES) == 0)
    pairs_bwd = pairs_fwd
    y_s, s_all, s_fin = _scan_fwd(r_s, w_s, k_s, v_s, kk_s, a_s, 64, pairs_fwd)

    def rw_post_values(y, r, k2, v, g, lng_ref, lnb_ref, rk_ref, e):
        mean = _headsum(y, e) * (1.0 / HEAD)
        d = y - mean
        rstd = lax.rsqrt(_headsum(d * d, e) * (1.0 / HEAD) + GN_EPS)
        yh = d * rstd
        rk = _headsum(r * k2 * rk_ref[...], e)
        yo = yh * lng_ref[...] + lnb_ref[...] + rk * v
        sg = _sigmoid(g)
        return rstd, yh, rk, yo, sg

    def rw_post(y_ref, r_ref, k_ref, v_ref, g_ref, lng_ref, lnb_ref, rk_ref, e_ref, out_ref):
        g = g_ref[...]
        _, _, _, yo, sg = rw_post_values(y_ref[...], r_ref[...], k_ref[...], v_ref[...], g, lng_ref, lnb_ref, rk_ref, e_ref[...])
        out_ref[...] = (yo * g * sg).astype(BF16)

    (y_rw,) = _rowcall(rw_post, "rwkv_post", t_dim, bt, [y_s, r_s, k_s, v_s, g_s], [ln_g, ln_b, r_k, e_head], [(c_rw, BF16)])

    fl_cb = (sec + 4 * c_fox) // LANES
    hp = _round_up(h_fox, SUBLANES)
    bt_c = _pick(t_dim, 256)
    tri = (jnp.arange(bt_c)[:, None] >= jnp.arange(bt_c)[None, :]).astype(F32)

    rows8 = n_pairs * SUBLANES
    pair_rows = (jnp.arange(rows8)[:, None] // SUBLANES * 2 + jnp.arange(rows8)[:, None] % SUBLANES
                 == jnp.arange(LANES)[None, :]) & (jnp.arange(rows8)[:, None] % SUBLANES < 2)
    pair_rows = pair_rows.astype(F32)

    def fox_decay(fl_ref, bf_ref, tri_ref, sel_ref, ct_ref, cq_ref, carry):
        _first_step_zero(carry)
        lf = _log_sigmoid(fl_ref[...] + bf_ref[...])
        c = jnp.dot(tri_ref[...], lf, precision=HI, preferred_element_type=F32) + carry[...]
        carry[...] = c[bt_c - 1:bt_c, :]
        ct = jnp.dot(sel_ref[...], jnp.transpose(c), precision=HI, preferred_element_type=F32)
        ct_ref[...] = ct.reshape(n_pairs, SUBLANES, bt_c)
        for pair in range(n_pairs):
            cq_ref[pair] = c[:, 2 * pair:2 * pair + 2]

    ct, cq = pl.pallas_call(
        fox_decay, name="fox_decay", grid=(t_dim // bt_c,),
        in_specs=[pl.BlockSpec((bt_c, LANES), lambda i: (i, fl_cb)), pl.BlockSpec((1, LANES), lambda i: (0, 0)),
                  pl.BlockSpec((bt_c, bt_c), lambda i: (0, 0)), pl.BlockSpec((rows8, LANES), lambda i: (0, 0))],
        out_specs=[pl.BlockSpec((n_pairs, SUBLANES, bt_c), lambda i: (0, 0, i)),
                   pl.BlockSpec((n_pairs, bt_c, 2), lambda i: (0, i, 0))],
        out_shape=[jax.ShapeDtypeStruct((n_pairs, SUBLANES, t_dim), F32), jax.ShapeDtypeStruct((n_pairs, t_dim, 2), F32)],
        scratch_shapes=[pltpu.VMEM((1, LANES), F32)], compiler_params=_params("arbitrary"),
    )(z, b_f, tri, pair_rows)
    q_cb = sec // LANES
    k_cb, v_cb = q_cb + n_pairs, q_cb + 2 * n_pairs
    o_fox, lse = _fox_fwd(z, ct, cq, q_cb, k_cb, v_cb, n_pairs, 1024)

    def fox_post(o_ref, z_ref, out_ref):
        g = z_ref[:, 3 * c_fox:4 * c_fox]
        out_ref[...] = (o_ref[...] * g * _sigmoid(g)).astype(BF16)

    (y_fox,) = _rowcall(fox_post, "fox_post", t_dim, bt, [o_fox, (z, 1, sec)], [], [(c_fox, BF16)])

    u_rw = _matmul(y_rw, wur, "nn", "up_rwkv")
    u_fox = _matmul(y_fox, wuf, "nn", "up_fox")

    def merge(ur_ref, uf_ref, z_ref, out_ref):
        s1 = _sigmoid(z_ref[:, 0:d_model])
        s2 = _sigmoid(z_ref[:, d_model:2 * d_model])
        out_ref[...] = (s1 * ur_ref[...] + s2 * uf_ref[...]).astype(BF16)

    (merged,) = _rowcall(merge, "merge", t_dim, bt, [u_rw, u_fox, (z, 2, sec)], [], [(d_model, BF16)])
    mo = _matmul(merged, wo, "nn", "proj_out")

    def resid_norm(x_ref, mo_ref, g_ref, x1_ref, n2_ref):
        x1 = x_ref[...] + mo_ref[...]
        rms = lax.rsqrt(jnp.mean(x1 * x1, axis=-1, keepdims=True) + NORM_EPS)
        x1_ref[...] = x1
        n2_ref[...] = (x1 * rms * g_ref[...]).astype(BF16)

    x1, n2 = _rowcall(resid_norm, "resid_norm", t_dim, bt, [x, mo], [g2], [(d_model, F32), (d_model, BF16)])
    ple = _matmul(p, pp, "nn", "ple_proj")
    gl = _matmul(n2, pg, "nn", "ple_gate")

    def head(x1_ref, ple_ref, gl_ref, tgt_ref, g_ref, dx2_ref, dple_ref, dgl_ref, loss_ref, dg3_ref):
        _first_step_zero(loss_ref, dg3_ref)
        sg = _sigmoid(gl_ref[...])
        pl_v = ple_ref[...]
        x2 = x1_ref[...] + pl_v * sg
        rms = lax.rsqrt(jnp.mean(x2 * x2, axis=-1, keepdims=True) + NORM_EPS)
        xn = x2 * rms
        diff = xn * g_ref[...] - tgt_ref[...]
        loss_ref[...] += 0.5 * jnp.sum(jnp.mean(diff * diff, axis=-1, keepdims=True), axis=0, keepdims=True)
        dyf = diff * inv_d
        dg3_ref[...] += _colsum(dyf * xn)
        gy = dyf * g_ref[...]
        dx2 = rms * (gy - xn * jnp.mean(xn * gy, axis=-1, keepdims=True))
        dx2_ref[...] = dx2
        dple_ref[...] = (dx2 * sg).astype(BF16)
        dgl_ref[...] = (dx2 * pl_v * sg * (1.0 - sg)).astype(BF16)

    dx2, dple, dgl, loss, d_g3 = _rowcall(
        head, "head", t_dim, bt, [x1, ple, gl, tgt], [g3], [(d_model, F32), (d_model, BF16), (d_model, BF16)],
        acc_outs=[(1, 1), (1, d_model)])

    d_pp = _matmul(p, dple, "tn", "d_ple_proj", out_dtype=BF16)
    d_pg = _matmul(n2, dgl, "tn", "d_ple_gate", out_dtype=BF16)
    dn2 = _matmul(dgl, pg, "nt", "d_n2")

    def resid_norm_bwd(dx2_ref, dn2_ref, x1_ref, g_ref, dx1_ref, dx1b_ref, dg2_ref):
        _first_step_zero(dg2_ref)
        x1 = x1_ref[...]
        rms = lax.rsqrt(jnp.mean(x1 * x1, axis=-1, keepdims=True) + NORM_EPS)
        xn = x1 * rms
        dn = dn2_ref[...]
        dg2_ref[...] += _colsum(dn * xn)
        gy = dn * g_ref[...]
        dx1 = dx2_ref[...] + rms * (gy - xn * jnp.mean(xn * gy, axis=-1, keepdims=True))
        dx1_ref[...] = dx1
        dx1b_ref[...] = dx1.astype(BF16)

    dx1, dx1b, d_g2 = _rowcall(resid_norm_bwd, "resid_norm_bwd", t_dim, bt, [dx2, dn2, x1], [g2],
                               [(d_model, F32), (d_model, BF16)], acc_outs=[(1, d_model)])
    d_wo = _matmul(merged, dx1b, "tn", "d_w_out", out_dtype=BF16)
    dmerged = _matmul(dx1b, wo, "nt", "d_merged")

    def merge_bwd(dm_ref, ur_ref, uf_ref, z_ref, dur_ref, duf_ref, dzg_ref):
        dm = dm_ref[...]
        s1 = _sigmoid(z_ref[:, 0:d_model])
        s2 = _sigmoid(z_ref[:, d_model:2 * d_model])
        dur_ref[...] = (dm * s1).astype(BF16)
        duf_ref[...] = (dm * s2).astype(BF16)
        dzg_ref[:, 0:d_model] = (dm * ur_ref[...] * s1 * (1.0 - s1)).astype(BF16)
        dzg_ref[:, d_model:2 * d_model] = (dm * uf_ref[...] * s2 * (1.0 - s2)).astype(BF16)
        if sec > 2 * d_model:
            dzg_ref[:, 2 * d_model:sec] = jnp.zeros((dm.shape[0], sec - 2 * d_model), BF16)

    du_rw, du_fox, dz_gate = _rowcall(merge_bwd, "merge_bwd", t_dim, bt, [dmerged, u_rw, u_fox, (z, 2, sec)], [],
                                      [(d_model, BF16), (d_model, BF16), (sec, BF16)])
    d_wur = _matmul(y_rw, du_rw, "tn", "d_w_up_rwkv", out_dtype=BF16)
    d_wuf = _matmul(y_fox, du_fox, "tn", "d_w_up_fox", out_dtype=BF16)
    dy_rw = _matmul(du_rw, wur, "nt", "d_y_rwkv")
    dy_fox = _matmul(du_fox, wuf, "nt", "d_y_fox")

    def fox_post_bwd(dy_ref, o_ref, z_ref, e_ref, do_ref, dg_ref, rd_ref):
        g = z_ref[:, 3 * c_fox:4 * c_fox]
        sg = _sigmoid(g)
        dy, o = dy_ref[...], o_ref[...]
        do = dy * g * sg
        do_ref[...] = do
        dg_ref[...] = (dy * o * sg * (1.0 + g * (1.0 - sg))).astype(BF16)
        rd_ref[...] = _headsum(do.astype(BF16).astype(F32) * o, e_ref[...])

    do_fox, dg_fox, rowdot = _rowcall(fox_post_bwd, "fox_post_bwd", t_dim, bt, [dy_fox, o_fox, (z, 1, sec)], [e_head],
                                      [(c_fox, F32), (c_fox, BF16), (c_fox, F32)])
    dq_f, dk_f, dv_f, dc_t = _fox_bwd(z, ct, cq, rowdot, do_fox, lse, q_cb, k_cb, v_cb, n_pairs, 512)
    sel = (jnp.arange(hp)[:, None] // 2 * SUBLANES + jnp.arange(hp)[:, None] % 2 == jnp.arange(rows8)[None, :]).astype(F32)
    tri_rev = (jnp.arange(bt_c)[:, None] >= jnp.arange(bt_c)[None, :]).astype(F32)
    bf_col = b_f.reshape(LANES, 1)[0:hp]
    nbc = t_dim // bt_c

    def fox_decay_bwd(dc_ref, fl_ref, sel_ref, tri_ref, bf_ref, dfl_ref, dbf_ref, carry):
        _first_step_zero(carry, dbf_ref)
        dc = jnp.dot(sel_ref[...], dc_ref[...].reshape(rows8, bt_c), precision=HI, preferred_element_type=F32)
        dlf = jnp.dot(dc, tri_ref[...], precision=HI, preferred_element_type=F32) + carry[...]
        carry[...] = dlf[:, 0:1]
        flt = jnp.transpose(fl_ref[...])[0:hp, :]
        dfl = dlf * _sigmoid(-(flt + bf_ref[...]))
        head_row = lax.broadcasted_iota(jnp.int32, (hp, bt_c), 0)
        dfl = jnp.where(head_row < h_fox, dfl, 0.0)
        dbf_ref[...] += jnp.sum(dfl, axis=1, keepdims=True)
        full = jnp.concatenate([dfl, jnp.zeros((LANES - hp, bt_c), F32)], axis=0) if hp < LANES else dfl
        dfl_ref[...] = jnp.transpose(full).astype(BF16)

    dz_fl, d_bf = pl.pallas_call(
        fox_decay_bwd, name="fox_decay_bwd", grid=(nbc,),
        in_specs=[pl.BlockSpec((n_pairs, SUBLANES, bt_c), lambda i: (0, 0, nbc - 1 - i)),
                  pl.BlockSpec((bt_c, LANES), lambda i: (nbc - 1 - i, fl_cb)),
                  pl.BlockSpec(sel.shape, lambda i: (0, 0)), pl.BlockSpec((bt_c, bt_c), lambda i: (0, 0)),
                  pl.BlockSpec((hp, 1), lambda i: (0, 0))],
        out_specs=[pl.BlockSpec((bt_c, LANES), lambda i: (nbc - 1 - i, 0)), pl.BlockSpec((hp, 1), lambda i: (0, 0))],
        out_shape=[jax.ShapeDtypeStruct((t_dim, LANES), BF16), jax.ShapeDtypeStruct((hp, 1), F32)],
        scratch_shapes=[pltpu.VMEM((hp, 1), F32)], compiler_params=_params("arbitrary"),
    )(dc_t, z, sel, tri_rev, bf_col)

    def rw_post_bwd(dy_ref, y_ref, r_ref, k_ref, v_ref, g_ref, lng_ref, lnb_ref, rk_ref, e_ref,
                    dg_ref, dys_ref, dr_ref, dk_ref, dv_ref, dlng_ref, dlnb_ref, drk_ref):
        _first_step_zero(dlng_ref, dlnb_ref, drk_ref)
        e = e_ref[...]
        dy, r, k2, v, g = dy_ref[...], r_ref[...], k_ref[...], v_ref[...], g_ref[...]
        rstd, yh, rk, yo, sg = rw_post_values(y_ref[...], r, k2, v, g, lng_ref, lnb_ref, rk_ref, e)
        dg_ref[...] = dy * yo * sg * (1.0 + g * (1.0 - sg))
        dyo = dy * g * sg
        dlnb_ref[...] += _colsum(dyo)
        dlng_ref[...] += _colsum(dyo * yh)
        dyh = dyo * lng_ref[...]
        dys_ref[...] = rstd * (dyh - _headsum(dyh, e) * (1.0 / HEAD) - yh * _headsum(dyh * yh, e) * (1.0 / HEAD))
        drk = _headsum(dyo * v, e)
        dv_ref[...] = dyo * rk
        dr_ref[...] = drk * k2 * rk_ref[...]
        dk_ref[...] = drk * r * rk_ref[...]
        drk_ref[...] += _colsum(drk * r * k2)

    dg_rw, dy_s, dr_b, dk_b, dv_b, d_lng, d_lnb, d_rk = _rowcall(
        rw_post_bwd, "rwkv_post_bwd", t_dim, bt, [dy_rw, y_s, r_s, k_s, v_s, g_s], [ln_g, ln_b, r_k, e_head],
        [(c_rw, F32)] * 5, acc_outs=[(1, c_rw)] * 3)
    axis = dict(SHARDED)
    early = {"w_up_rwkv": d_wur, "w_up_fox": d_wuf, "w_out": d_wo, "ple_proj": d_pp, "ple_gate_w": d_pg}
    (dr_c, dw_c, dk_c, dv_c, dkk_c, da_c), early_recv = _scan_bwd(
        r_s, w_s, k_s, v_s, kk_s, a_s, dy_s, s_all, s_fin, 64, pairs_bwd,
        _Exchange([_split_shards(g, axis[n]) for n, g in early.items()]))
    recv = dict(zip(early, early_recv))

    def rw_prep_bwd(z_ref, dr1, dr2, dw_ref, dk1, dk2_ref, dv1, dv2, dkk_ref, da_ref, dg_ref,
                    mu_ref, w0_ref, wl_ref, a0_ref, wa_ref, kk_ref, ka_ref, e_ref,
                    dzs_ref, tw_ref, al_ref, dwr_ref, dar_ref, dmu_ref, dw0_ref, da0_ref, dkk_acc, dka_acc, carry):
        _first_step_zero(carry, dmu_ref, dw0_ref, da0_ref, dkk_acc, dka_acc)
        e = e_ref[...]
        zv = z_ref[...]
        zp = _shift_down(zv, carry)
        zs = zv + (zp - zv) * mu_ref[...]
        k, tw, al, sw, decay, a, kk0, nrm, inv, k2 = rw_values(zs, w0_ref, wl_ref, a0_ref, wa_ref, kk_ref, ka_ref, e_ref)
        dk2 = dk1[...] + dk2_ref[...]
        da = da_ref[...] + dk2 * k * ka_ref[...]
        dk = dk2 * (1.0 + (a - 1.0) * ka_ref[...])
        dka_acc[...] += _colsum(dk2 * k * (a - 1.0))
        kk = kk0 * inv
        dkk = dkk_ref[...]
        dkk0 = inv * jnp.where(nrm > 1e-12, dkk - kk * _headsum(dkk * kk, e), dkk)
        dk = dk + dkk0 * kk_ref[...]
        dkk_acc[...] += _colsum(dkk0 * k)
        da_raw = da * a * (1.0 - a)
        da0_ref[...] += _colsum(da_raw)
        dw_raw = dw_ref[...] * decay * (-decay_k) * sw * (1.0 - sw)
        dw0_ref[...] += _colsum(dw_raw)
        dar_b, dwr_b = da_raw.astype(BF16), dw_raw.astype(BF16)
        dal = lax.dot_general(dar_b, wa_ref[...], _DOT_DIMS["nt"], preferred_element_type=F32)
        dtw = lax.dot_general(dwr_b, wl_ref[...], _DOT_DIMS["nt"], preferred_element_type=F32)
        dzs_ref[:, 0:c_rw] = dr1[...] + dr2[...]
        dzs_ref[:, c_rw:2 * c_rw] = dk
        dzs_ref[:, 2 * c_rw:3 * c_rw] = dv1[...] + dv2[...]
        dzs_ref[:, 3 * c_rw:c4] = dg_ref[...]
        dzs_ref[:, c4:c4 + lora] = dtw * (1.0 - tw * tw)
        dzs_ref[:, c4 + lora:c4 + 2 * lora] = dal
        if sec > c4 + 2 * lora:
            dzs_ref[:, c4 + 2 * lora:sec] = jnp.zeros((zv.shape[0], sec - c4 - 2 * lora), F32)
        tw_ref[...] = tw.astype(BF16)
        al_ref[...] = al.astype(BF16)
        dwr_ref[...] = dwr_b
        dar_ref[...] = dar_b
        dmu_ref[...] += _colsum(dzs_ref[...] * (zp - zv))

    dzs, tw_b, al_b, dwr_b, dar_b, d_mu, d_w0, d_a0, d_kk, d_ka = _rowcall(
        rw_prep_bwd, "rwkv_prep_bwd", t_dim, bt_many,
        [(z, 0, sec), dr_c, dr_b, dw_c, dk_c, dk_b, dv_c, dv_b, dkk_c, da_c, dg_rw], rw_consts,
        [(sec, F32), (lora, BF16), (lora, BF16), (c_rw, BF16), (c_rw, BF16)],
        acc_outs=[(1, sec), (1, c_rw), (1, c_rw), (1, c_rw), (1, c_rw)], scratch=[pltpu.VMEM((1, sec), F32)])
    d_wl = _matmul(tw_b, dwr_b, "tn", "d_w_lora", out_dtype=BF16)
    d_wa = _matmul(al_b, dar_b, "tn", "d_a_lora", out_dtype=BF16)

    def shift_bwd(dzs_ref, mu_ref, dz_ref, carry):
        _first_step_zero(carry)
        d = dzs_ref[...]
        nbt = d.shape[0]
        nxt = pltpu.roll(d, nbt - 1, 0)
        rowi = lax.broadcasted_iota(jnp.int32, d.shape, 0)
        nxt = jnp.where(rowi == nbt - 1, carry[...], nxt)
        carry[...] = d[0:1, :]
        m = mu_ref[...]
        dz_ref[...] = (d * (1.0 - m) + nxt * m).astype(BF16)

    (dz_rw,) = _rowcall(shift_bwd, "shift_bwd", t_dim, bt, [dzs], [mu], [(sec, BF16)],
                        scratch=[pltpu.VMEM((1, sec), F32)], reverse=True)

    fox_parts = [dq_f.astype(BF16), dk_f.astype(BF16), dv_f.astype(BF16), dg_fox, dz_fl]
    if sec > 4 * c_fox + LANES:
        fox_parts.append(jnp.zeros((t_dim, sec - 4 * c_fox - LANES), BF16))
    dz = jnp.concatenate([dz_rw] + fox_parts + [dz_gate], axis=1)
    d_wz = _matmul(h, dz, "tn", "d_w_in", out_dtype=BF16, bn=1408)
    rw_cols, fox_cols = c4 + 2 * lora, 4 * c_fox + h_fox
    d_wi = jnp.concatenate([d_wz[:, :rw_cols], d_wz[:, sec:sec + fox_cols], d_wz[:, 2 * sec:2 * sec + 2 * d_model]], axis=1)
    late = {"w_in": d_wi, "rw_w_lora_up": d_wl, "rw_a_lora_up": d_wa}
    dh, late_recv = _matmul(dz, wz, "nt", "d_h", bk=sec,
                            exchange=_Exchange([_split_shards(g, axis[n]) for n, g in late.items()]))
    recv.update(zip(late, late_recv))

    def norm_in_bwd(dh_ref, x_ref, dx1_ref, g_ref, dx_ref, dg1_ref):
        _first_step_zero(dg1_ref)
        xv = x_ref[...]
        rms = lax.rsqrt(jnp.mean(xv * xv, axis=-1, keepdims=True) + NORM_EPS)
        xn = xv * rms
        d = dh_ref[...]
        dg1_ref[...] += _colsum(d * xn)
        gy = d * g_ref[...]
        dx_ref[...] = dx1_ref[...] + rms * (gy - xn * jnp.mean(xn * gy, axis=-1, keepdims=True))

    dx, d_g1 = _rowcall(norm_in_bwd, "norm_in_bwd", t_dim, bt, [dh, x, dx1], [norm_g], [(d_model, F32)],
                        acc_outs=[(1, d_model)])

    rep_grads = {"norm_g": d_g1, "rw_shift_mu": d_mu[:, 0:c4 + 2 * lora], "rw_w0": d_w0, "rw_a0": d_a0, "rw_k_k": d_kk,
                 "rw_k_a": d_ka, "rw_r_k": d_rk, "rw_ln_g": d_lng, "rw_ln_b": d_lnb, "fox_b_f": d_bf[0:h_fox, 0],
                 "ple_norm_g": d_g2, "final_norm_g": d_g3}
    return loss[0, 0], dx, recv, rep_grads


def kernel(x, p, norm_g, w_in, rw_shift_mu, rw_w0, rw_w_lora_up, rw_a0, rw_a_lora_up, rw_k_k, rw_k_a, rw_r_k, rw_ln_g, rw_ln_b, fox_b_f, w_up_rwkv, w_up_fox, w_out, ple_proj, ple_gate_w, ple_norm_g, final_norm_g, loss_target, m_norm_g, m_w_in, m_rw_shift_mu, m_rw_w0, m_rw_w_lora_up, m_rw_a0, m_rw_a_lora_up, m_rw_k_k, m_rw_k_a, m_rw_r_k, m_rw_ln_g, m_rw_ln_b, m_fox_b_f, m_w_up_rwkv, m_w_up_fox, m_w_out, m_ple_proj, m_ple_gate_w, m_ple_norm_g, m_final_norm_g, v_norm_g, v_w_in, v_rw_shift_mu, v_rw_w0, v_rw_w_lora_up, v_rw_a0, v_rw_a_lora_up, v_rw_k_k, v_rw_k_a, v_rw_r_k, v_rw_ln_g, v_rw_ln_b, v_fox_b_f, v_w_up_rwkv, v_w_up_fox, v_w_out, v_ple_proj, v_ple_gate_w, v_ple_norm_g, v_final_norm_g):
    args = locals()
    w = {n: args[n] for n in WEIGHTS}
    mom = {n: args["m_" + n] for n in WEIGHTS}
    var = {n: args["v_" + n] for n in WEIGHTS}

    t_dim, d_model = x.shape[1], x.shape[2]
    c_rw, lora = rw_w0.shape[1], rw_w_lora_up.shape[1]
    h_fox = fox_b_f.shape[1]
    c_fox = h_fox * HEAD
    rw_cols, fox_cols, gate_cols = 4 * c_rw + 2 * lora, 4 * c_fox + h_fox, 2 * d_model
    sec = max(rw_cols, 4 * c_fox + LANES, _round_up(gate_cols, LANES))
    assert c_rw % LANES == 0 and c_fox % LANES == 0 and rw_cols % LANES == 0 and h_fox <= LANES and d_model % LANES == 0
    assert rw_a_lora_up.shape[1] == lora and w_in.shape[2] * NDEV == rw_cols + fox_cols + gate_cols

    assert SHARDED[0] == ("w_in", 1)
    (w_in_all,) = _all_gather([w_in[0].astype(BF16)], "gather_w_in")
    other_shards = [w[n][0].astype(BF16) for n, _ in SHARDED[1:]]

    def to_sections(wi):
        pad = lambda a, width: jnp.pad(a, ((0, 0), (0, width - a.shape[1])))
        return jnp.concatenate([pad(wi[:, :rw_cols], sec), pad(wi[:, rw_cols:rw_cols + fox_cols], sec),
                                pad(wi[:, rw_cols + fox_cols:], sec)], axis=1)

    rep = {n: w[n] for n in REPLICATED}
    dims = (t_dim, d_model, c_rw, lora, c_fox, h_fox, sec)
    loss_local, grad_x, recv, rep_grads = _local_step(
        x[0], p[0, 0], loss_target[0], to_sections(_join_shards(w_in_all, 1)), other_shards, rep, dims)

    rep_sizes = [w[n].size for n in REPLICATED]
    rep_offs = [sum(rep_sizes[:i]) for i in range(len(rep_sizes))]
    pack_rep = lambda tree: _pack_rows(jnp.concatenate([tree[n].astype(F32).reshape(-1) for n in REPLICATED]))
    (small_all,) = _grad_exchange([pack_rep(rep_grads)], "exchange_replicated")
    kinds = ("grad", "delta", "new_m", "new_v")
    outs = {}
    for n, _ in SHARDED:
        for kind, buf in zip(kinds, _adamw(recv[n], w[n][0], mom[n][0], var[n][0], "adamw_" + n)):
            outs[kind, n] = buf
    for kind, buf in zip(kinds, _adamw(small_all, pack_rep(w), pack_rep(mom), pack_rep(var), "adamw_replicated")):
        flat = buf.reshape(-1)
        for n, o, s in zip(REPLICATED, rep_offs, rep_sizes):
            outs[kind, n] = flat[o:o + s].reshape(w[n].shape)
    loss = lax.psum(loss_local, MESH_AXES)
    return (loss, grad_x[None], *[outs[kind, n] for kind in ("grad", "delta", "new_m", "new_v") for n in WEIGHTS])
```

```python
import functools
import math

import jax
import jax.numpy as jnp
from jax import lax
from jax.experimental import pallas as pl
from jax.experimental.pallas import tpu as pltpu

F32, BF16 = jnp.float32, jnp.bfloat16
HI = lax.Precision.HIGHEST
LANES = 128
SUBLANES = 8
HEAD = 64
NORM_EPS = 1e-6
GN_EPS = 64e-5
VMEM_LIMIT = 56 * 1024 * 1024
NDEV = 8
PACK_W = 1024
PACK_ALIGN = 16 * PACK_W
MESH_AXES = ("x", "y", "c")
MESH = pl.DeviceIdType.MESH

ADAM_LR, ADAM_B1, ADAM_B2, ADAM_EPS, ADAM_WD, ADAM_STEP = 0.001, 0.9, 0.999, 1e-08, 0.01, 10


def _round_up(n, m):
    return (n + m - 1) // m * m


def _pick(dim, pref, align=LANES):
    if dim <= pref:
        return dim
    best = None
    for cand in range(align, pref + 1, align):
        if dim % cand == 0:
            best = cand
    return dim if best is None else best


def _params(*sem):
    return pltpu.CompilerParams(dimension_semantics=sem, vmem_limit_bytes=VMEM_LIMIT)


def _sigmoid(v):
    return jax.nn.sigmoid(v)


def _log_sigmoid(v):
    return jnp.minimum(v, 0.0) - jnp.log(1.0 + jnp.exp(-jnp.abs(v)))


_DOT_DIMS = {"nn": (((1,), (0,)), ((), ())), "nt": (((1,), (1,)), ((), ())), "tn": (((0,), (0,)), ((), ()))}


def _matmul(a, b, mode, name, out_dtype=F32, bm=512, bn=1024, bk=2048, exchange=None):
    if mode == "tn":
        k_dim, m_dim = a.shape
    else:
        m_dim, k_dim = a.shape
    n_dim = b.shape[0] if mode == "nt" else b.shape[1]
    bm, bn, bk = _pick(m_dim, bm), _pick(n_dim, bn), _pick(k_dim, bk)
    nk = k_dim // bk

    nx = 0 if exchange is None else exchange.n
    grid = (m_dim // bm, n_dim // bn, nk)

    def body(*refs):
        a_ref, b_ref, o_ref = refs[0], refs[1], refs[2 + nx]
        step = [pl.program_id(d) for d in range(3)]
        if nx:
            own_scratch = 1 if nk > 1 else 0
            ex_refs = (refs[2:2 + nx], refs[3 + nx:3 + 2 * nx], refs[3 + 2 * nx + own_scratch:])
            linear = (step[0] * grid[1] + step[1]) * grid[2] + step[2]

            @pl.when(linear == 0)
            def _():
                exchange.start(*ex_refs)

            if hasattr(exchange, "forward"):
                @pl.when(linear == (grid[0] * grid[1] * grid[2]) // 2)
                def _():
                    exchange.forward(*ex_refs)

        prod = lax.dot_general(a_ref[...].astype(BF16), b_ref[...].astype(BF16), _DOT_DIMS[mode],
                               preferred_element_type=F32)
        if nk == 1:
            o_ref[...] = prod.astype(o_ref.dtype)
        else:
            acc_ref, k = refs[3 + 2 * nx], step[2]

            @pl.when(k == 0)
            def _():
                acc_ref[...] = prod

            @pl.when(k > 0)
            def _():
                acc_ref[...] += prod

            @pl.when(k == nk - 1)
            def _():
                o_ref[...] = acc_ref[...].astype(o_ref.dtype)
        if nx:
            @pl.when(linear == grid[0] * grid[1] * grid[2] - 1)
            def _():
                exchange.finish(*ex_refs)

    if mode == "tn":
        a_spec = pl.BlockSpec((bk, bm), lambda i, j, k: (k, i))
    else:
        a_spec = pl.BlockSpec((bm, bk), lambda i, j, k: (i, k))
    if mode == "nt":
        b_spec = pl.BlockSpec((bn, bk), lambda i, j, k: (j, k))
    else:
        b_spec = pl.BlockSpec((bk, bn), lambda i, j, k: (k, j))
    out_spec = pl.BlockSpec((bm, bn), lambda i, j, k: (i, j))
    out_shape = jax.ShapeDtypeStruct((m_dim, n_dim), out_dtype)
    acc_scratch = [pltpu.VMEM((bm, bn), F32)] if nk > 1 else []
    if exchange is None:
        return pl.pallas_call(
            body, name=name, grid=grid, in_specs=[a_spec, b_spec], out_specs=out_spec, out_shape=out_shape,
            scratch_shapes=acc_scratch, compiler_params=_params("parallel", "parallel", "arbitrary"),
        )(a, b)
    outs = pl.pallas_call(
        body, name=name, grid=grid, in_specs=[a_spec, b_spec] + [HBM_SPEC] * nx,
        out_specs=[out_spec] + [HBM_SPEC] * nx, out_shape=[out_shape] + exchange.out_shape,
        scratch_shapes=acc_scratch + exchange.scratch, compiler_params=_params("arbitrary", "arbitrary", "arbitrary"),
    )(a, b, *exchange.arrays)
    return outs[0], outs[1:]


def _rowcall(body, name, t_dim, bt, row_ins, const_ins, row_outs, acc_outs=(), scratch=(), reverse=False):
    nt = t_dim // bt

    def rmap(i):
        return nt - 1 - i if reverse else i

    in_specs, args = [], []
    for item in row_ins:
        arr, cb, w = item if isinstance(item, tuple) else (item, 0, item.shape[1])
        in_specs.append(pl.BlockSpec((bt, w), lambda i, cb=cb: (rmap(i), cb)))
        args.append(arr)
    for arr in const_ins:
        in_specs.append(pl.BlockSpec(arr.shape, lambda i, nd=arr.ndim: (0,) * nd))
        args.append(arr)
    out_specs = [pl.BlockSpec((bt, w), lambda i: (rmap(i), 0)) for w, _ in row_outs]
    out_shape = [jax.ShapeDtypeStruct((t_dim, w), dt) for w, dt in row_outs]
    for shp in acc_outs:
        out_specs.append(pl.BlockSpec(shp, lambda i, nd=len(shp): (0,) * nd))
        out_shape.append(jax.ShapeDtypeStruct(shp, F32))
    return pl.pallas_call(
        body, name=name, grid=(nt,), in_specs=in_specs, out_specs=out_specs, out_shape=out_shape,
        scratch_shapes=list(scratch), compiler_params=_params("arbitrary"),
    )(*args)


def _first_step_zero(*refs):
    @pl.when(pl.program_id(0) == 0)
    def _():
        for r in refs:
            r[...] = jnp.zeros_like(r)


def _colsum(v):
    return jnp.sum(v, axis=0, keepdims=True)


def _headsum(v, e):
    parts = [jnp.dot(v[:, p * LANES:(p + 1) * LANES], e, precision=HI, preferred_element_type=F32)
             for p in range(v.shape[1] // LANES)]
    return parts[0] if len(parts) == 1 else jnp.concatenate(parts, axis=1)


def _shift_down(v, carry_ref):
    bt = v.shape[0]
    prev = pltpu.roll(v, 1, 0)
    row = lax.broadcasted_iota(jnp.int32, v.shape, 0)
    prev = jnp.where(row == 0, carry_ref[...], prev)
    carry_ref[...] = v[bt - 1:bt, :]
    return prev


def _pair_consts():
    lane = lax.broadcasted_iota(jnp.int32, (1, LANES), 1)
    m0 = (lane < HEAD).astype(F32)
    m1 = 1.0 - m0
    sub = lax.broadcasted_iota(jnp.int32, (HEAD, LANES), 0)
    lane2 = lax.broadcasted_iota(jnp.int32, (HEAD, LANES), 1)
    i0 = (lane2 == sub).astype(F32)
    i1 = (lane2 == sub + HEAD).astype(F32)
    return m0, m1, i0, i1


def _head_masks():
    lane = lax.broadcasted_iota(jnp.int32, (1, LANES), 1)
    first = (lane < HEAD).astype(F32)
    return first, 1.0 - first


def _head_ones(dtype=BF16):
    lane = jnp.arange(LANES)
    return (lane[:, None] // HEAD == lane[None, :] // HEAD).astype(dtype)


def _lanesum(v):
    return jnp.sum(v, axis=1, keepdims=True)


def _split_bf16(v):
    hi = v.astype(BF16).astype(F32)
    rest = v - hi
    mid = rest.astype(BF16).astype(F32)
    return hi, mid, (rest - mid).astype(BF16).astype(F32)


def _col_pair(parts, j, i01, e_bf16):
    lhs = jnp.concatenate([(part[j:j + 1] * i01).astype(BF16) for part in parts], axis=0)
    out = jnp.dot(lhs, e_bf16, preferred_element_type=F32)
    return (out[0:HEAD] + out[HEAD:2 * HEAD]) + out[2 * HEAD:3 * HEAD]


def _row_pair(c0, c1, i0, i1):
    return _colsum(c0 * i0 + c1 * i1)


def _scan_fwd(r, w, k, v, kk, a, tc, npb):
    t_dim, c_dim = r.shape
    wb = LANES * npb
    tc = _pick(t_dim, tc, SUBLANES)

    def body(r_ref, w_ref, k_ref, v_ref, kk_ref, a_ref, e_ref, y_ref, sall_ref, sfin_ref, s_ref):
        @pl.when(pl.program_id(1) == 0)
        def _():
            s_ref[...] = jnp.zeros_like(s_ref)

        m0, m1, i0, i1 = _pair_consts()
        i01 = i0 + i1
        e = e_ref[...]
        sub8 = lax.broadcasted_iota(jnp.int32, (SUBLANES, LANES), 0)
        lanes = [slice(q * LANES, (q + 1) * LANES) for q in range(npb)]
        ng = tc // SUBLANES

        def lanesums(tiles):
            sums = _lanesum(jnp.concatenate(tiles, axis=0))
            return [sums[i * HEAD:(i + 1) * HEAD] for i in range(len(tiles))]

        def halves(s, row):
            return [s * (row * m0), s * (row * m1)]

        def group(gi, carry):
            base = pl.multiple_of(gi * SUBLANES, SUBLANES)
            rows = pl.ds(base, SUBLANES)
            s = list(carry)
            r8 = [r_ref[rows, sl] for sl in lanes]
            w8 = [w_ref[rows, sl] for sl in lanes]
            k8 = [k_ref[rows, sl] for sl in lanes]
            kk8 = [kk_ref[rows, sl] for sl in lanes]
            b8 = [kk8[q] * a_ref[rows, lanes[q]] for q in range(npb)]
            v8 = [_split_bf16(v_ref[rows, sl]) for sl in lanes]
            y8 = [jnp.zeros((SUBLANES, LANES), F32)] * npb
            for j in range(SUBLANES + 1):
                one, before = slice(j, j + 1), slice(j - 1, j)
                tiles = []
                for q in range(npb):
                    if j < SUBLANES:
                        tiles += halves(s[q], kk8[q][one])
                    if j > 0:
                        tiles += halves(s[q], r8[q][before])
                cols = lanesums(tiles)
                per = len(tiles) // npb
                for q in range(npb):
                    mine = cols[q * per:(q + 1) * per]
                    if j > 0:
                        y8[q] = jnp.where(sub8 == j - 1, _row_pair(mine[-2], mine[-1], i0, i1), y8[q])
                    if j < SUBLANES:
                        sall_ref[base + j, :, lanes[q]] = s[q]
                        sb = mine[0] * m0 + mine[1] * m1
                        s[q] = s[q] * w8[q][one] - sb * b8[q][one] + _col_pair(v8[q], j, i01, e) * k8[q][one]
            for q in range(npb):
                y_ref[rows, lanes[q]] = y8[q]
            return tuple(s)

        init = tuple(s_ref[:, q * LANES:(q + 1) * LANES] for q in range(npb))
        fin = lax.fori_loop(0, ng, group, init)
        for q in range(npb):
            s_ref[:, q * LANES:(q + 1) * LANES] = fin[q]
            sfin_ref[:, q * LANES:(q + 1) * LANES] = fin[q]

    row = pl.BlockSpec((tc, wb), lambda p, c: (c, p))
    return pl.pallas_call(
        body, name="rwkv_scan_fwd", grid=(c_dim // wb, t_dim // tc),
        in_specs=[row] * 6 + [pl.BlockSpec((LANES, LANES), lambda p, c: (0, 0))],
        out_specs=[row, pl.BlockSpec((tc, HEAD, wb), lambda p, c: (c, 0, p)), pl.BlockSpec((HEAD, wb), lambda p, c: (0, p))],
        out_shape=[jax.ShapeDtypeStruct((t_dim, c_dim), F32), jax.ShapeDtypeStruct((t_dim, HEAD, c_dim), F32),
                   jax.ShapeDtypeStruct((HEAD, c_dim), F32)],
        scratch_shapes=[pltpu.VMEM((HEAD, wb), F32)],
        compiler_params=_params("parallel", "arbitrary"),
    )(r, w, k, v, kk, a, _head_ones())


def _scan_bwd(r, w, k, v, kk, a, dy, sall, sfin, tc, npb, exchange):
    t_dim, c_dim = r.shape
    wb = LANES * npb
    tc = _pick(t_dim, tc, SUBLANES)
    nc = t_dim // tc
    nx = exchange.n
    n_blocks = c_dim // wb

    def body(*refs):
        r_ref, w_ref, k_ref, v_ref, kk_ref, a_ref, dy_ref, sall_ref, sfin_ref, e_ref = refs[:10]
        dr_ref, dw_ref, dk_ref, dv_ref, dkk_ref, da_ref = refs[10 + nx:16 + nx]
        ds_ref, sn_ref = refs[16 + 2 * nx:18 + 2 * nx]
        here_p, here_c = pl.program_id(0), pl.program_id(1)
        ex_refs = (refs[10:10 + nx], refs[16 + nx:16 + 2 * nx], refs[18 + 2 * nx:])

        @pl.when((here_p == 0) & (here_c == 0))
        def _():
            exchange.start(*ex_refs)

        @pl.when(pl.program_id(1) == 0)
        def _():
            ds_ref[...] = jnp.zeros_like(ds_ref)
            sn_ref[...] = sfin_ref[...]

        m0, m1, i0, i1 = _pair_consts()
        i01 = i0 + i1
        e = e_ref[...]
        sub8 = lax.broadcasted_iota(jnp.int32, (SUBLANES, LANES), 0)
        lanes = [slice(q * LANES, (q + 1) * LANES) for q in range(npb)]
        ng = tc // SUBLANES

        def halves(s, row):
            return [s * (row * m0), s * (row * m1)]

        def group(gi, carry):
            base = pl.multiple_of((ng - 1 - gi) * SUBLANES, SUBLANES)
            rows = pl.ds(base, SUBLANES)
            ds = list(carry)
            r8 = [r_ref[rows, sl] for sl in lanes]
            w8 = [w_ref[rows, sl] for sl in lanes]
            k8 = [k_ref[rows, sl] for sl in lanes]
            kk8 = [kk_ref[rows, sl] for sl in lanes]
            a8 = [a_ref[rows, sl] for sl in lanes]
            v8 = [_split_bf16(v_ref[rows, sl]) for sl in lanes]
            dy8 = [_split_bf16(dy_ref[rows, sl]) for sl in lanes]
            zero8 = jnp.zeros((SUBLANES, LANES), F32)
            dr8, dw8, dk8, dv8, dkk8, da8 = ([zero8] * npb for _ in range(6))
            for j in reversed(range(SUBLANES)):
                one = slice(j, j + 1)
                here = sub8 == j
                d, s_prev, tiles = [], [], []
                for q in range(npb):
                    dyb = _col_pair(dy8[q], j, i01, e)
                    s_prev.append(sall_ref[base + j, :, lanes[q]])
                    dr8[q] = jnp.where(here, _colsum(sn_ref[:, lanes[q]] * dyb), dr8[q])
                    sn_ref[:, lanes[q]] = s_prev[q]
                    d.append(ds[q] + dyb * r8[q][one])
                for q in range(npb):
                    tiles += halves(d[q], kk8[q][one] * a8[q][one])
                for q in range(npb):
                    tiles += halves(d[q], k8[q][one]) + halves(s_prev[q], kk8[q][one])
                sums = _lanesum(jnp.concatenate(tiles, axis=0))
                cols = [sums[i * HEAD:(i + 1) * HEAD] for i in range(len(tiles))]
                for q in range(npb):
                    kkr, ar = kk8[q][one], a8[q][one]
                    dsb = -(cols[2 * q] * m0 + cols[2 * q + 1] * m1)
                    ds[q] = d[q] * w8[q][one] + dsb * kkr
                    rest = cols[2 * npb + 4 * q:2 * npb + 4 * q + 4]
                    dv8[q] = jnp.where(here, _row_pair(rest[0], rest[1], i0, i1), dv8[q])
                    sb = rest[2] * m0 + rest[3] * m1
                    db = -_colsum(d[q] * sb)
                    dk8[q] = jnp.where(here, _colsum(d[q] * _col_pair(v8[q], j, i01, e)), dk8[q])
                    dw8[q] = jnp.where(here, _colsum(d[q] * s_prev[q]), dw8[q])
                    dkk8[q] = jnp.where(here, _colsum(s_prev[q] * dsb) + db * ar, dkk8[q])
                    da8[q] = jnp.where(here, db * kkr, da8[q])
            for q in range(npb):
                sl = lanes[q]
                dr_ref[rows, sl], dw_ref[rows, sl], dk_ref[rows, sl] = dr8[q], dw8[q], dk8[q]
                dv_ref[rows, sl], dkk_ref[rows, sl], da_ref[rows, sl] = dv8[q], dkk8[q], da8[q]
            return tuple(ds)

        init = tuple(ds_ref[:, q * LANES:(q + 1) * LANES] for q in range(npb))
        fin = lax.fori_loop(0, ng, group, init)
        for q in range(npb):
            ds_ref[:, q * LANES:(q + 1) * LANES] = fin[q]

        @pl.when((here_p == n_blocks - 1) & (here_c == nc - 1))
        def _():
            exchange.finish(*ex_refs)

    row = pl.BlockSpec((tc, wb), lambda p, c: (nc - 1 - c, p))
    outs = pl.pallas_call(
        body, name="rwkv_scan_bwd", grid=(n_blocks, nc),
        in_specs=[row] * 7 + [pl.BlockSpec((tc, HEAD, wb), lambda p, c: (nc - 1 - c, 0, p)),
                              pl.BlockSpec((HEAD, wb), lambda p, c: (0, p)),
                              pl.BlockSpec((LANES, LANES), lambda p, c: (0, 0))] + [HBM_SPEC] * nx,
        out_specs=[row] * 6 + [HBM_SPEC] * nx,
        out_shape=[jax.ShapeDtypeStruct((t_dim, c_dim), F32)] * 6 + exchange.out_shape,
        scratch_shapes=[pltpu.VMEM((HEAD, wb), F32)] * 2 + exchange.scratch,
        compiler_params=_params("arbitrary", "arbitrary"),
    )(r, w, k, v, kk, a, dy, sall, sfin, _head_ones(), *exchange.arrays)
    return outs[:6], outs[6:]


def _fox_fwd(z, ct, cq, q_cb, k_cb, v_cb, n_pairs, blk):
    t_dim = z.shape[0]
    blk = _pick(t_dim, blk)
    nq = t_dim // blk
    scale = HEAD ** -0.5

    def body(q_ref, k_ref, v_ref, ct_ref, cq_ref, o_ref, lse_ref):
        i = pl.program_id(1)
        rowi = lax.broadcasted_iota(jnp.int32, (blk, blk), 0)
        coli = lax.broadcasted_iota(jnp.int32, (blk, blk), 1)
        masks = _head_masks()
        qv = q_ref[...]
        qs = [(qv * mk).astype(BF16) for mk in masks]
        cqs = [cq_ref[:, hh:hh + 1] for hh in range(2)]

        def kv_step(j, carry, masked):
            rows = pl.ds(pl.multiple_of(j * blk, blk), blk)
            kb = k_ref[rows, :].astype(BF16)
            vv = v_ref[rows, :]
            stats, acc = list(carry[:4]), carry[4]
            rescale, add = 0.0, 0.0
            for hh in range(2):
                m, l = stats[2 * hh], stats[2 * hh + 1]
                s = (lax.dot_general(qs[hh], kb, _DOT_DIMS["nt"], preferred_element_type=F32) * scale
                     + (cqs[hh] - ct_ref[hh:hh + 1, rows]))
                if masked:
                    s = jnp.where(rowi >= coli, s, -jnp.inf)
                m_new = jnp.maximum(m, jnp.max(s, axis=1, keepdims=True))
                alpha = jnp.exp(m - m_new)
                pr = jnp.exp(s - m_new)
                stats[2 * hh], stats[2 * hh + 1] = m_new, l * alpha + jnp.sum(pr, axis=1, keepdims=True)
                rescale = rescale + alpha * masks[hh]
                hi = pr.astype(BF16)
                both = jnp.concatenate([hi, (pr - hi.astype(F32)).astype(BF16)], axis=1)
                vh = (vv * masks[hh]).astype(BF16)
                add = add + jnp.dot(both, jnp.concatenate([vh, vh], axis=0), preferred_element_type=F32)
            return (*stats, acc * rescale + add)

        neg, zero = jnp.full((blk, 1), -jnp.inf, F32), jnp.zeros((blk, 1), F32)
        carry = lax.fori_loop(0, i, functools.partial(kv_step, masked=False),
                              (neg, zero, neg, zero, jnp.zeros((blk, LANES), F32)))
        m0, l0, m1, l1, acc = kv_step(i, carry, True)
        o_ref[...] = acc * (masks[0] / l0 + masks[1] / l1)
        lse_ref[:, 0:1] = m0 + jnp.log(l0)
        lse_ref[:, 1:2] = m1 + jnp.log(l1)

    full = lambda cb: pl.BlockSpec((t_dim, LANES), lambda p, i, cb=cb: (0, cb + p))
    return pl.pallas_call(
        body, name="fox_attn_fwd", grid=(n_pairs, nq),
        in_specs=[pl.BlockSpec((blk, LANES), lambda p, i: (i, q_cb + p)), full(k_cb), full(v_cb),
                  pl.BlockSpec((None, SUBLANES, t_dim), lambda p, i: (p, 0, 0)),
                  pl.BlockSpec((None, blk, 2), lambda p, i: (p, i, 0))],
        out_specs=[pl.BlockSpec((blk, LANES), lambda p, i: (i, p)), pl.BlockSpec((None, blk, 2), lambda p, i: (p, i, 0))],
        out_shape=[jax.ShapeDtypeStruct((t_dim, n_pairs * LANES), F32), jax.ShapeDtypeStruct((n_pairs, t_dim, 2), F32)],
        compiler_params=_params("parallel", "arbitrary"),
    )(z, z, z, ct, cq)


def _fox_bwd(z, ct, cq, rowdot, do, lse, q_cb, k_cb, v_cb, n_pairs, blk):
    t_dim = z.shape[0]
    blk = _pick(t_dim, blk)
    nb = t_dim // blk
    scale = HEAD ** -0.5

    def body(q_ref, k_ref, v_ref, ct_ref, cq_ref, rd_ref, do_ref, lse_ref, dq_ref, dk_ref, dv_ref, dc_ref):
        j = pl.program_id(1)

        @pl.when(j == 0)
        def _():
            dq_ref[...] = jnp.zeros_like(dq_ref)

        rowi = lax.broadcasted_iota(jnp.int32, (blk, blk), 0)
        coli = lax.broadcasted_iota(jnp.int32, (blk, blk), 1)
        krows = pl.ds(pl.multiple_of(j * blk, blk), blk)
        masks = _head_masks()
        kv, vb = k_ref[...], v_ref[...].astype(BF16)
        kb = kv.astype(BF16)
        ks = [(kv * mk).astype(BF16) for mk in masks]
        cks = [ct_ref[hh:hh + 1, krows] for hh in range(2)]

        def q_step(i, carry, masked):
            dk, dv, dcs = carry[0], carry[1], list(carry[2:])
            rows = pl.ds(pl.multiple_of(i * blk, blk), blk)
            qv, dov = q_ref[rows, :], do_ref[rows, :]
            dq = 0.0
            for hh in range(2):
                qh = (qv * masks[hh]).astype(BF16)
                doh = (dov * masks[hh]).astype(BF16)
                s = (lax.dot_general(qh, kb, _DOT_DIMS["nt"], preferred_element_type=F32) * scale
                     + (cq_ref[rows, hh:hh + 1] - cks[hh]))
                pr = jnp.exp(s - lse_ref[rows, hh:hh + 1])
                if masked:
                    pr = jnp.where(rowi >= coli, pr, 0.0)
                dv = dv + lax.dot_general(pr.astype(BF16), doh, _DOT_DIMS["tn"], preferred_element_type=F32)
                dp = lax.dot_general(doh, vb, _DOT_DIMS["nt"], preferred_element_type=F32)
                ds = pr * (dp - rd_ref[rows, hh * HEAD:hh * HEAD + 1])
                dsb = ds.astype(BF16)
                dq = dq + jnp.dot(dsb, ks[hh], preferred_element_type=F32)
                dk = dk + lax.dot_general(dsb, qh, _DOT_DIMS["tn"], preferred_element_type=F32)
                dcs[hh] = dcs[hh] - _colsum(ds)
            dq_ref[rows, :] += dq * scale
            return (dk, dv, *dcs)

        zero_row = jnp.zeros((1, blk), F32)
        init = (jnp.zeros((blk, LANES), F32), jnp.zeros((blk, LANES), F32), zero_row, zero_row)
        carry = q_step(j, init, True)
        dk, dv, dc0, dc1 = lax.fori_loop(j + 1, nb, functools.partial(q_step, masked=False), carry)
        dk_ref[...] = dk * scale
        dv_ref[...] = dv
        dc_ref[0:1, :] = dc0
        dc_ref[1:2, :] = dc1
        dc_ref[2:SUBLANES, :] = jnp.zeros((SUBLANES - 2, blk), F32)

    full = lambda: pl.BlockSpec((t_dim, LANES), lambda p, j: (0, p))
    blkspec = pl.BlockSpec((blk, LANES), lambda p, j: (j, p))
    return pl.pallas_call(
        body, name="fox_attn_bwd", grid=(n_pairs, nb),
        in_specs=[pl.BlockSpec((t_dim, LANES), lambda p, j: (0, q_cb + p)),
                  pl.BlockSpec((blk, LANES), lambda p, j: (j, k_cb + p)),
                  pl.BlockSpec((blk, LANES), lambda p, j: (j, v_cb + p)),
                  pl.BlockSpec((None, SUBLANES, t_dim), lambda p, j: (p, 0, 0)),
                  pl.BlockSpec((None, t_dim, 2), lambda p, j: (p, 0, 0)), full(), full(),
                  pl.BlockSpec((None, t_dim, 2), lambda p, j: (p, 0, 0))],
        out_specs=[full(), blkspec, blkspec, pl.BlockSpec((None, SUBLANES, blk), lambda p, j: (p, 0, j))],
        out_shape=[jax.ShapeDtypeStruct((t_dim, n_pairs * LANES), F32)] * 3
                  + [jax.ShapeDtypeStruct((n_pairs, SUBLANES, t_dim), F32)],
        compiler_params=_params("parallel", "arbitrary"),
    )(z, z, z, ct, cq, rowdot, do, lse)


HBM_SPEC = pl.BlockSpec(memory_space=pltpu.HBM)


class _Gather:
    def __init__(self, shards):
        self.arrays = list(shards)
        self.n = len(self.arrays)
        self.out_shape = [jax.ShapeDtypeStruct((NDEV,) + s.shape, s.dtype) for s in self.arrays]
        self.scratch = [pltpu.SemaphoreType.DMA((7 * self.n,)), pltpu.SemaphoreType.DMA((7 * self.n,)),
                        pltpu.SemaphoreType.DMA((self.n,))]

    def _plan(self, x_refs, out_refs, sems):
        send_sems, recv_sems, local_sems = sems
        x, y, c = lax.axis_index("x"), lax.axis_index("y"), lax.axis_index("c")
        me, sibling = (x, y, c), (x, y, 1 - c)
        chips = [(1 - x, y), (x, 1 - y), (1 - x, 1 - y)]

        def copy(a, k, block, to, from_input=False):
            px, py, pc = block
            slot = out_refs[a].at[4 * px + 2 * py + pc]
            return pltpu.make_async_remote_copy(
                src_ref=x_refs[a] if from_input else slot, dst_ref=slot,
                send_sem=send_sems.at[7 * a + k], recv_sem=recv_sems.at[7 * a + k], device_id=to, device_id_type=MESH)

        mine = [pltpu.make_async_copy(x_refs[a], out_refs[a].at[4 * x + 2 * y + c], local_sems.at[a])
                for a in range(self.n)]
        first = []
        for a in range(self.n):
            first += [copy(a, 1 + j, me, (*chip, c), from_input=True) for j, chip in enumerate(chips)]
            first.append(copy(a, 0, me, sibling, from_input=True))
        over_ici = [copy(a, 1 + j, (*chip, c), me) for a in range(self.n) for j, chip in enumerate(chips)]
        passed = [copy(a, 4 + j, (*chip, c), sibling) for a in range(self.n) for j, chip in enumerate(chips)]
        from_sibling = []
        for a in range(self.n):
            from_sibling.append(copy(a, 0, sibling, me))
            from_sibling += [copy(a, 4 + j, (*chip, 1 - c), me) for j, chip in enumerate(chips)]
        return mine, first, over_ici, passed, from_sibling

    def start(self, *refs):
        mine, first, _, _, _ = self._plan(*refs)
        for cp in mine + first:
            cp.start()

    def forward(self, *refs):
        _, _, over_ici, passed, _ = self._plan(*refs)
        for arrived, onward in zip(over_ici, passed):
            arrived.wait_recv()
            onward.start()

    def finish(self, *refs):
        mine, first, _, passed, from_sibling = self._plan(*refs)
        for cp in from_sibling:
            cp.wait_recv()
        for cp in first + passed:
            cp.wait_send()
        for cp in mine:
            cp.wait()


def _all_gather(shards, name):
    ga = _Gather(shards)

    def body(*refs):
        parts = (refs[:ga.n], refs[ga.n:2 * ga.n], refs[2 * ga.n:])
        ga.start(*parts)
        ga.forward(*parts)
        ga.finish(*parts)

    return pl.pallas_call(body, name=name, out_shape=ga.out_shape, in_specs=[HBM_SPEC] * ga.n,
                          out_specs=[HBM_SPEC] * ga.n, scratch_shapes=ga.scratch)(*ga.arrays)


class _Exchange:
    def __init__(self, arrays):
        self.arrays = list(arrays)
        self.n = len(self.arrays)
        self.per_dest = [a.ndim == 3 for a in self.arrays]
        self.out_shape = [jax.ShapeDtypeStruct(a.shape if pd else (NDEV,) + a.shape, a.dtype)
                          for a, pd in zip(self.arrays, self.per_dest)]
        self.scratch = [pltpu.SemaphoreType.DMA((7 * self.n,)), pltpu.SemaphoreType.DMA((7 * self.n,)),
                        pltpu.SemaphoreType.DMA((self.n,))]

    def _copies(self, in_refs, out_refs, sems):
        send_sems, recv_sems, local_sems = sems
        x, y, c = lax.axis_index("x"), lax.axis_index("y"), lax.axis_index("c")
        me = 4 * x + 2 * y + c
        own, sends, recvs = [], [], []
        for a in range(self.n):
            mine = in_refs[a].at[me] if self.per_dest[a] else in_refs[a]
            own.append(pltpu.make_async_copy(mine, out_refs[a].at[me], local_sems.at[a]))
            for k in range(1, NDEV):
                px = 1 - x if k & 4 else x
                py = 1 - y if k & 2 else y
                pc = 1 - c if k & 1 else c
                peer = 4 * px + 2 * py + pc
                sem = dict(send_sem=send_sems.at[7 * a + k - 1], recv_sem=recv_sems.at[7 * a + k - 1],
                           device_id=(px, py, pc), device_id_type=MESH)
                src = in_refs[a].at[peer] if self.per_dest[a] else in_refs[a]
                sends.append(pltpu.make_async_remote_copy(src_ref=src, dst_ref=out_refs[a].at[me], **sem))
                recvs.append(pltpu.make_async_remote_copy(src_ref=src, dst_ref=out_refs[a].at[peer], **sem))
        return own, sends, recvs

    def start(self, in_refs, out_refs, sems):
        own, sends, _ = self._copies(in_refs, out_refs, sems)
        for cp in own + sends:
            cp.start()

    def finish(self, in_refs, out_refs, sems):
        own, sends, recvs = self._copies(in_refs, out_refs, sems)
        for cp in recvs:
            cp.wait_recv()
        for cp in sends:
            cp.wait_send()
        for cp in own:
            cp.wait()


def _grad_exchange(arrays, name):
    ex = _Exchange(arrays)

    def body(*refs):
        in_refs, out_refs, sems = refs[:ex.n], refs[ex.n:2 * ex.n], refs[2 * ex.n:]
        ex.start(in_refs, out_refs, sems)
        ex.finish(in_refs, out_refs, sems)

    return pl.pallas_call(body, name=name, out_shape=ex.out_shape, in_specs=[HBM_SPEC] * ex.n,
                          out_specs=[HBM_SPEC] * ex.n, scratch_shapes=ex.scratch)(*ex.arrays)


def _adamw_body(p_ref, w_ref, m_ref, v_ref, g_out, d_out, m_out, v_out):
    g = p_ref[0].astype(F32)
    for d in range(1, NDEV):
        g = g + p_ref[d].astype(F32)
    mn = ADAM_B1 * m_ref[...] + (1.0 - ADAM_B1) * g
    vn = ADAM_B2 * v_ref[...] + (1.0 - ADAM_B2) * jnp.square(g)
    m_hat = mn / (1.0 - ADAM_B1 ** ADAM_STEP)
    v_hat = vn / (1.0 - ADAM_B2 ** ADAM_STEP)
    g_out[...] = g
    d_out[...] = -ADAM_LR * (m_hat / (jnp.sqrt(v_hat) + ADAM_EPS) + ADAM_WD * w_ref[...])
    m_out[...] = mn
    v_out[...] = vn


def _adamw(partials, w, m, v, name):
    rows, width = w.shape
    br = _pick(rows, 128, 2 * SUBLANES)
    body = functools.partial(_adamw_body)
    blk = pl.BlockSpec((br, width), lambda i: (i, 0))
    return pl.pallas_call(
        body, name=name, grid=(rows // br,),
        in_specs=[pl.BlockSpec((NDEV, br, width), lambda i: (0, i, 0)), blk, blk, blk],
        out_specs=[pl.BlockSpec((None, br, width), lambda i: (0, i, 0))] * 4,
        out_shape=[jax.ShapeDtypeStruct((1, rows, width), F32)] * 4,
        compiler_params=_params("parallel"),
    )(partials, w, m, v)


def _pack_rows(flat):
    n = flat.shape[0]
    padded = _round_up(n, PACK_ALIGN)
    return jnp.pad(flat, (0, padded - n)).reshape(padded // PACK_W, PACK_W)


def _split_shards(full, axis):
    rows, cols = full.shape
    if axis == 0:
        return full.reshape(NDEV, rows // NDEV, cols)
    width = cols // NDEV
    return jnp.stack([full[:, d * width:(d + 1) * width] for d in range(NDEV)])


def _join_shards(blocks, axis):
    if axis == 0:
        return blocks.reshape(-1, blocks.shape[2])
    return jnp.concatenate([blocks[d] for d in range(NDEV)], axis=1)


SHARDED = (("w_in", 1), ("rw_w_lora_up", 1), ("rw_a_lora_up", 1), ("w_up_rwkv", 1), ("w_up_fox", 1),
           ("w_out", 0), ("ple_proj", 1), ("ple_gate_w", 0))
REPLICATED = ("norm_g", "rw_shift_mu", "rw_w0", "rw_a0", "rw_k_k", "rw_k_a", "rw_r_k", "rw_ln_g", "rw_ln_b",
              "fox_b_f", "ple_norm_g", "final_norm_g")
WEIGHTS = ("norm_g", "w_in", "rw_shift_mu", "rw_w0", "rw_w_lora_up", "rw_a0", "rw_a_lora_up", "rw_k_k", "rw_k_a",
           "rw_r_k", "rw_ln_g", "rw_ln_b", "fox_b_f", "w_up_rwkv", "w_up_fox", "w_out", "ple_proj", "ple_gate_w",
           "ple_norm_g", "final_norm_g")


def _local_step(x, p, tgt, wz, other_shards, rep, dims):
    t_dim, d_model, c_rw, lora, c_fox, h_fox, sec = dims
    bt = _pick(t_dim, 256, 2 * SUBLANES)
    bt_many = _pick(t_dim, 128, 2 * SUBLANES)
    n_pairs = c_fox // LANES
    row = lambda a: a.reshape(1, -1)
    norm_g, mu, w0, a0 = row(rep["norm_g"]), row(rep["rw_shift_mu"]), row(rep["rw_w0"]), row(rep["rw_a0"])
    k_k, k_a, r_k = row(rep["rw_k_k"]), row(rep["rw_k_a"]), row(rep["rw_r_k"])
    ln_g, ln_b = row(rep["rw_ln_g"]), row(rep["rw_ln_b"])
    g2, g3 = row(rep["ple_norm_g"]), row(rep["final_norm_g"])
    b_f = jnp.pad(row(rep["fox_b_f"]), ((0, 0), (0, LANES - h_fox)))
    e_head = _head_ones(F32)
    c4 = 4 * c_rw
    inv_d = 1.0 / d_model
    decay_k = math.exp(-0.5)

    def norm_in(x_ref, g_ref, h_ref):
        xv = x_ref[...]
        rms = lax.rsqrt(jnp.mean(xv * xv, axis=-1, keepdims=True) + NORM_EPS)
        h_ref[...] = (xv * rms * g_ref[...]).astype(BF16)

    (h,) = _rowcall(norm_in, "norm_in", t_dim, bt, [x], [norm_g], [(d_model, BF16)])
    z, gathered = _matmul(h, wz, "nn", "proj_in", bn=1408, exchange=_Gather(other_shards))
    wl, wa, wur, wuf, wo, pp, pg = [_join_shards(g, ax) for (_, ax), g in zip(SHARDED[1:], gathered)]

    def rw_values(zs, w0_ref, wl_ref, a0_ref, wa_ref, kk_ref, ka_ref, e_ref):
        k = zs[:, c_rw:2 * c_rw]
        tw = jnp.tanh(zs[:, c4:c4 + lora])
        al = zs[:, c4 + lora:c4 + 2 * lora]
        sw = _sigmoid(w0_ref[...] + jnp.dot(tw.astype(BF16), wl_ref[...], preferred_element_type=F32))
        decay = jnp.exp(-decay_k * sw)
        a = _sigmoid(a0_ref[...] + jnp.dot(al.astype(BF16), wa_ref[...], preferred_element_type=F32))
        kk0 = k * kk_ref[...]
        nrm = jnp.sqrt(_headsum(kk0 * kk0, e_ref[...]))
        inv = 1.0 / jnp.maximum(nrm, 1e-12)
        k2 = k * (1.0 + (a - 1.0) * ka_ref[...])
        return k, tw, al, sw, decay, a, kk0, nrm, inv, k2

    def rw_prep(z_ref, mu_ref, w0_ref, wl_ref, a0_ref, wa_ref, kk_ref, ka_ref, e_ref,
                r_o, w_o, k_o, v_o, kk_o, a_o, g_o, carry):
        _first_step_zero(carry)
        zv = z_ref[...]
        zs = zv + (_shift_down(zv, carry) - zv) * mu_ref[...]
        k, tw, al, sw, decay, a, kk0, nrm, inv, k2 = rw_values(zs, w0_ref, wl_ref, a0_ref, wa_ref, kk_ref, ka_ref, e_ref)
        r_o[...] = zs[:, 0:c_rw]
        w_o[...] = decay
        k_o[...] = k2
        v_o[...] = zs[:, 2 * c_rw:3 * c_rw]
        kk_o[...] = kk0 * inv
        a_o[...] = a
        g_o[...] = zs[:, 3 * c_rw:c4]

    rw_consts = [mu, w0, wl, a0, wa, k_k, k_a, e_head]
    r_s, w_s, k_s, v_s, kk_s, a_s, g_s = _rowcall(
        rw_prep, "rwkv_prep", t_dim, bt, [(z, 0, sec)], rw_consts, [(c_rw, F32)] * 7,
        scratch=[pltpu.VMEM((1, sec), F32)])
    pairs_fwd = max(n for n in (1, 2, 4) if c_rw % (n * LANES) == 0)
    pairs_bwd = pairs_fwd
    y_s, s_all, s_fin = _scan_fwd(r_s, w_s, k_s, v_s, kk_s, a_s, 128, pairs_fwd)

    def rw_post_values(y, r, k2, v, g, lng_ref, lnb_ref, rk_ref, e):
        mean = _headsum(y, e) * (1.0 / HEAD)
        d = y - mean
        rstd = lax.rsqrt(_headsum(d * d, e) * (1.0 / HEAD) + GN_EPS)
        yh = d * rstd
        rk = _headsum(r * k2 * rk_ref[...], e)
        yo = yh * lng_ref[...] + lnb_ref[...] + rk * v
        sg = _sigmoid(g)
        return rstd, yh, rk, yo, sg

    def rw_post(y_ref, r_ref, k_ref, v_ref, g_ref, lng_ref, lnb_ref, rk_ref, e_ref, out_ref):
        g = g_ref[...]
        _, _, _, yo, sg = rw_post_values(y_ref[...], r_ref[...], k_ref[...], v_ref[...], g, lng_ref, lnb_ref, rk_ref, e_ref[...])
        out_ref[...] = (yo * g * sg).astype(BF16)

    (y_rw,) = _rowcall(rw_post, "rwkv_post", t_dim, bt, [y_s, r_s, k_s, v_s, g_s], [ln_g, ln_b, r_k, e_head], [(c_rw, BF16)])

    fl_cb = (sec + 4 * c_fox) // LANES
    hp = _round_up(h_fox, SUBLANES)
    bt_c = _pick(t_dim, 256)
    tri = (jnp.arange(bt_c)[:, None] >= jnp.arange(bt_c)[None, :]).astype(F32)

    rows8 = n_pairs * SUBLANES
    pair_rows = (jnp.arange(rows8)[:, None] // SUBLANES * 2 + jnp.arange(rows8)[:, None] % SUBLANES
                 == jnp.arange(LANES)[None, :]) & (jnp.arange(rows8)[:, None] % SUBLANES < 2)
    pair_rows = pair_rows.astype(F32)

    def fox_decay(fl_ref, bf_ref, tri_ref, sel_ref, ct_ref, cq_ref, carry):
        _first_step_zero(carry)
        lf = _log_sigmoid(fl_ref[...] + bf_ref[...])
        c = jnp.dot(tri_ref[...], lf, precision=HI, preferred_element_type=F32) + carry[...]
        carry[...] = c[bt_c - 1:bt_c, :]
        ct = jnp.dot(sel_ref[...], jnp.transpose(c), precision=HI, preferred_element_type=F32)
        ct_ref[...] = ct.reshape(n_pairs, SUBLANES, bt_c)
        for pair in range(n_pairs):
            cq_ref[pair] = c[:, 2 * pair:2 * pair + 2]

    ct, cq = pl.pallas_call(
        fox_decay, name="fox_decay", grid=(t_dim // bt_c,),
        in_specs=[pl.BlockSpec((bt_c, LANES), lambda i: (i, fl_cb)), pl.BlockSpec((1, LANES), lambda i: (0, 0)),
                  pl.BlockSpec((bt_c, bt_c), lambda i: (0, 0)), pl.BlockSpec((rows8, LANES), lambda i: (0, 0))],
        out_specs=[pl.BlockSpec((n_pairs, SUBLANES, bt_c), lambda i: (0, 0, i)),
                   pl.BlockSpec((n_pairs, bt_c, 2), lambda i: (0, i, 0))],
        out_shape=[jax.ShapeDtypeStruct((n_pairs, SUBLANES, t_dim), F32), jax.ShapeDtypeStruct((n_pairs, t_dim, 2), F32)],
        scratch_shapes=[pltpu.VMEM((1, LANES), F32)], compiler_params=_params("arbitrary"),
    )(z, b_f, tri, pair_rows)
    q_cb = sec // LANES
    k_cb, v_cb = q_cb + n_pairs, q_cb + 2 * n_pairs
    o_fox, lse = _fox_fwd(z, ct, cq, q_cb, k_cb, v_cb, n_pairs, 1024)

    def fox_post(o_ref, z_ref, out_ref):
        g = z_ref[:, 3 * c_fox:4 * c_fox]
        out_ref[...] = (o_ref[...] * g * _sigmoid(g)).astype(BF16)

    (y_fox,) = _rowcall(fox_post, "fox_post", t_dim, bt, [o_fox, (z, 1, sec)], [], [(c_fox, BF16)])

    u_rw = _matmul(y_rw, wur, "nn", "up_rwkv")
    u_fox = _matmul(y_fox, wuf, "nn", "up_fox")

    def merge(ur_ref, uf_ref, z_ref, out_ref):
        s1 = _sigmoid(z_ref[:, 0:d_model])
        s2 = _sigmoid(z_ref[:, d_model:2 * d_model])
        out_ref[...] = (s1 * ur_ref[...] + s2 * uf_ref[...]).astype(BF16)

    (merged,) = _rowcall(merge, "merge", t_dim, bt, [u_rw, u_fox, (z, 2, sec)], [], [(d_model, BF16)])
    mo = _matmul(merged, wo, "nn", "proj_out")

    def resid_norm(x_ref, mo_ref, g_ref, x1_ref, n2_ref):
        x1 = x_ref[...] + mo_ref[...]
        rms = lax.rsqrt(jnp.mean(x1 * x1, axis=-1, keepdims=True) + NORM_EPS)
        x1_ref[...] = x1
        n2_ref[...] = (x1 * rms * g_ref[...]).astype(BF16)

    x1, n2 = _rowcall(resid_norm, "resid_norm", t_dim, bt, [x, mo], [g2], [(d_model, F32), (d_model, BF16)])
    ple = _matmul(p, pp, "nn", "ple_proj")
    gl = _matmul(n2, pg, "nn", "ple_gate")

    def head(x1_ref, ple_ref, gl_ref, tgt_ref, g_ref, dx2_ref, dple_ref, dgl_ref, loss_ref, dg3_ref):
        _first_step_zero(loss_ref, dg3_ref)
        sg = _sigmoid(gl_ref[...])
        pl_v = ple_ref[...]
        x2 = x1_ref[...] + pl_v * sg
        rms = lax.rsqrt(jnp.mean(x2 * x2, axis=-1, keepdims=True) + NORM_EPS)
        xn = x2 * rms
        diff = xn * g_ref[...] - tgt_ref[...]
        loss_ref[...] += 0.5 * jnp.sum(jnp.mean(diff * diff, axis=-1, keepdims=True), axis=0, keepdims=True)
        dyf = diff * inv_d
        dg3_ref[...] += _colsum(dyf * xn)
        gy = dyf * g_ref[...]
        dx2 = rms * (gy - xn * jnp.mean(xn * gy, axis=-1, keepdims=True))
        dx2_ref[...] = dx2
        dple_ref[...] = (dx2 * sg).astype(BF16)
        dgl_ref[...] = (dx2 * pl_v * sg * (1.0 - sg)).astype(BF16)

    dx2, dple, dgl, loss, d_g3 = _rowcall(
        head, "head", t_dim, bt, [x1, ple, gl, tgt], [g3], [(d_model, F32), (d_model, BF16), (d_model, BF16)],
        acc_outs=[(1, 1), (1, d_model)])

    d_pp = _matmul(p, dple, "tn", "d_ple_proj", out_dtype=BF16)
    d_pg = _matmul(n2, dgl, "tn", "d_ple_gate", out_dtype=BF16)
    dn2 = _matmul(dgl, pg, "nt", "d_n2")

    def resid_norm_bwd(dx2_ref, dn2_ref, x1_ref, g_ref, dx1_ref, dx1b_ref, dg2_ref):
        _first_step_zero(dg2_ref)
        x1 = x1_ref[...]
        rms = lax.rsqrt(jnp.mean(x1 * x1, axis=-1, keepdims=True) + NORM_EPS)
        xn = x1 * rms
        dn = dn2_ref[...]
        dg2_ref[...] += _colsum(dn * xn)
        gy = dn * g_ref[...]
        dx1 = dx2_ref[...] + rms * (gy - xn * jnp.mean(xn * gy, axis=-1, keepdims=True))
        dx1_ref[...] = dx1
        dx1b_ref[...] = dx1.astype(BF16)

    dx1, dx1b, d_g2 = _rowcall(resid_norm_bwd, "resid_norm_bwd", t_dim, bt, [dx2, dn2, x1], [g2],
                               [(d_model, F32), (d_model, BF16)], acc_outs=[(1, d_model)])
    d_wo = _matmul(merged, dx1b, "tn", "d_w_out", out_dtype=BF16)
    dmerged = _matmul(dx1b, wo, "nt", "d_merged")

    def merge_bwd(dm_ref, ur_ref, uf_ref, z_ref, dur_ref, duf_ref, dzg_ref):
        dm = dm_ref[...]
        s1 = _sigmoid(z_ref[:, 0:d_model])
        s2 = _sigmoid(z_ref[:, d_model:2 * d_model])
        dur_ref[...] = (dm * s1).astype(BF16)
        duf_ref[...] = (dm * s2).astype(BF16)
        dzg_ref[:, 0:d_model] = (dm * ur_ref[...] * s1 * (1.0 - s1)).astype(BF16)
        dzg_ref[:, d_model:2 * d_model] = (dm * uf_ref[...] * s2 * (1.0 - s2)).astype(BF16)
        if sec > 2 * d_model:
            dzg_ref[:, 2 * d_model:sec] = jnp.zeros((dm.shape[0], sec - 2 * d_model), BF16)

    du_rw, du_fox, dz_gate = _rowcall(merge_bwd, "merge_bwd", t_dim, bt, [dmerged, u_rw, u_fox, (z, 2, sec)], [],
                                      [(d_model, BF16), (d_model, BF16), (sec, BF16)])
    d_wur = _matmul(y_rw, du_rw, "tn", "d_w_up_rwkv", out_dtype=BF16)
    d_wuf = _matmul(y_fox, du_fox, "tn", "d_w_up_fox", out_dtype=BF16)
    dy_rw = _matmul(du_rw, wur, "nt", "d_y_rwkv")
    dy_fox = _matmul(du_fox, wuf, "nt", "d_y_fox")

    def fox_post_bwd(dy_ref, o_ref, z_ref, e_ref, do_ref, dg_ref, rd_ref):
        g = z_ref[:, 3 * c_fox:4 * c_fox]
        sg = _sigmoid(g)
        dy, o = dy_ref[...], o_ref[...]
        do = dy * g * sg
        do_ref[...] = do
        dg_ref[...] = (dy * o * sg * (1.0 + g * (1.0 - sg))).astype(BF16)
        rd_ref[...] = _headsum(do.astype(BF16).astype(F32) * o, e_ref[...])

    do_fox, dg_fox, rowdot = _rowcall(fox_post_bwd, "fox_post_bwd", t_dim, bt, [dy_fox, o_fox, (z, 1, sec)], [e_head],
                                      [(c_fox, F32), (c_fox, BF16), (c_fox, F32)])
    dq_f, dk_f, dv_f, dc_t = _fox_bwd(z, ct, cq, rowdot, do_fox, lse, q_cb, k_cb, v_cb, n_pairs, 512)
    sel = (jnp.arange(hp)[:, None] // 2 * SUBLANES + jnp.arange(hp)[:, None] % 2 == jnp.arange(rows8)[None, :]).astype(F32)
    tri_rev = (jnp.arange(bt_c)[:, None] >= jnp.arange(bt_c)[None, :]).astype(F32)
    bf_col = b_f.reshape(LANES, 1)[0:hp]
    nbc = t_dim // bt_c

    def fox_decay_bwd(dc_ref, fl_ref, sel_ref, tri_ref, bf_ref, dfl_ref, dbf_ref, carry):
        _first_step_zero(carry, dbf_ref)
        dc = jnp.dot(sel_ref[...], dc_ref[...].reshape(rows8, bt_c), precision=HI, preferred_element_type=F32)
        dlf = jnp.dot(dc, tri_ref[...], precision=HI, preferred_element_type=F32) + carry[...]
        carry[...] = dlf[:, 0:1]
        flt = jnp.transpose(fl_ref[...])[0:hp, :]
        dfl = dlf * _sigmoid(-(flt + bf_ref[...]))
        head_row = lax.broadcasted_iota(jnp.int32, (hp, bt_c), 0)
        dfl = jnp.where(head_row < h_fox, dfl, 0.0)
        dbf_ref[...] += jnp.sum(dfl, axis=1, keepdims=True)
        full = jnp.concatenate([dfl, jnp.zeros((LANES - hp, bt_c), F32)], axis=0) if hp < LANES else dfl
        dfl_ref[...] = jnp.transpose(full).astype(BF16)

    dz_fl, d_bf = pl.pallas_call(
        fox_decay_bwd, name="fox_decay_bwd", grid=(nbc,),
        in_specs=[pl.BlockSpec((n_pairs, SUBLANES, bt_c), lambda i: (0, 0, nbc - 1 - i)),
                  pl.BlockSpec((bt_c, LANES), lambda i: (nbc - 1 - i, fl_cb)),
                  pl.BlockSpec(sel.shape, lambda i: (0, 0)), pl.BlockSpec((bt_c, bt_c), lambda i: (0, 0)),
                  pl.BlockSpec((hp, 1), lambda i: (0, 0))],
        out_specs=[pl.BlockSpec((bt_c, LANES), lambda i: (nbc - 1 - i, 0)), pl.BlockSpec((hp, 1), lambda i: (0, 0))],
        out_shape=[jax.ShapeDtypeStruct((t_dim, LANES), BF16), jax.ShapeDtypeStruct((hp, 1), F32)],
        scratch_shapes=[pltpu.VMEM((hp, 1), F32)], compiler_params=_params("arbitrary"),
    )(dc_t, z, sel, tri_rev, bf_col)

    def rw_post_bwd(dy_ref, y_ref, r_ref, k_ref, v_ref, g_ref, lng_ref, lnb_ref, rk_ref, e_ref,
                    dg_ref, dys_ref, dr_ref, dk_ref, dv_ref, dlng_ref, dlnb_ref, drk_ref):
        _first_step_zero(dlng_ref, dlnb_ref, drk_ref)
        e = e_ref[...]
        dy, r, k2, v, g = dy_ref[...], r_ref[...], k_ref[...], v_ref[...], g_ref[...]
        rstd, yh, rk, yo, sg = rw_post_values(y_ref[...], r, k2, v, g, lng_ref, lnb_ref, rk_ref, e)
        dg_ref[...] = dy * yo * sg * (1.0 + g * (1.0 - sg))
        dyo = dy * g * sg
        dlnb_ref[...] += _colsum(dyo)
        dlng_ref[...] += _colsum(dyo * yh)
        dyh = dyo * lng_ref[...]
        dys_ref[...] = rstd * (dyh - _headsum(dyh, e) * (1.0 / HEAD) - yh * _headsum(dyh * yh, e) * (1.0 / HEAD))
        drk = _headsum(dyo * v, e)
        dv_ref[...] = dyo * rk
        dr_ref[...] = drk * k2 * rk_ref[...]
        dk_ref[...] = drk * r * rk_ref[...]
        drk_ref[...] += _colsum(drk * r * k2)

    dg_rw, dy_s, dr_b, dk_b, dv_b, d_lng, d_lnb, d_rk = _rowcall(
        rw_post_bwd, "rwkv_post_bwd", t_dim, bt, [dy_rw, y_s, r_s, k_s, v_s, g_s], [ln_g, ln_b, r_k, e_head],
        [(c_rw, F32)] * 5, acc_outs=[(1, c_rw)] * 3)
    axis = dict(SHARDED)
    early = {"w_up_rwkv": d_wur, "w_up_fox": d_wuf, "w_out": d_wo, "ple_proj": d_pp, "ple_gate_w": d_pg}
    (dr_c, dw_c, dk_c, dv_c, dkk_c, da_c), early_recv = _scan_bwd(
        r_s, w_s, k_s, v_s, kk_s, a_s, dy_s, s_all, s_fin, 128, pairs_bwd,
        _Exchange([_split_shards(g, axis[n]) for n, g in early.items()]))
    recv = dict(zip(early, early_recv))

    def rw_prep_bwd(z_ref, dr1, dr2, dw_ref, dk1, dk2_ref, dv1, dv2, dkk_ref, da_ref, dg_ref,
                    mu_ref, w0_ref, wl_ref, a0_ref, wa_ref, kk_ref, ka_ref, e_ref,
                    dzs_ref, tw_ref, al_ref, dwr_ref, dar_ref, dmu_ref, dw0_ref, da0_ref, dkk_acc, dka_acc, carry):
        _first_step_zero(carry, dmu_ref, dw0_ref, da0_ref, dkk_acc, dka_acc)
        e = e_ref[...]
        zv = z_ref[...]
        zp = _shift_down(zv, carry)
        zs = zv + (zp - zv) * mu_ref[...]
        k, tw, al, sw, decay, a, kk0, nrm, inv, k2 = rw_values(zs, w0_ref, wl_ref, a0_ref, wa_ref, kk_ref, ka_ref, e_ref)
        dk2 = dk1[...] + dk2_ref[...]
        da = da_ref[...] + dk2 * k * ka_ref[...]
        dk = dk2 * (1.0 + (a - 1.0) * ka_ref[...])
        dka_acc[...] += _colsum(dk2 * k * (a - 1.0))
        kk = kk0 * inv
        dkk = dkk_ref[...]
        dkk0 = inv * jnp.where(nrm > 1e-12, dkk - kk * _headsum(dkk * kk, e), dkk)
        dk = dk + dkk0 * kk_ref[...]
        dkk_acc[...] += _colsum(dkk0 * k)
        da_raw = da * a * (1.0 - a)
        da0_ref[...] += _colsum(da_raw)
        dw_raw = dw_ref[...] * decay * (-decay_k) * sw * (1.0 - sw)
        dw0_ref[...] += _colsum(dw_raw)
        dar_b, dwr_b = da_raw.astype(BF16), dw_raw.astype(BF16)
        dal = lax.dot_general(dar_b, wa_ref[...], _DOT_DIMS["nt"], preferred_element_type=F32)
        dtw = lax.dot_general(dwr_b, wl_ref[...], _DOT_DIMS["nt"], preferred_element_type=F32)
        dzs_ref[:, 0:c_rw] = dr1[...] + dr2[...]
        dzs_ref[:, c_rw:2 * c_rw] = dk
        dzs_ref[:, 2 * c_rw:3 * c_rw] = dv1[...] + dv2[...]
        dzs_ref[:, 3 * c_rw:c4] = dg_ref[...]
        dzs_ref[:, c4:c4 + lora] = dtw * (1.0 - tw * tw)
        dzs_ref[:, c4 + lora:c4 + 2 * lora] = dal
        if sec > c4 + 2 * lora:
            dzs_ref[:, c4 + 2 * lora:sec] = jnp.zeros((zv.shape[0], sec - c4 - 2 * lora), F32)
        tw_ref[...] = tw.astype(BF16)
        al_ref[...] = al.astype(BF16)
        dwr_ref[...] = dwr_b
        dar_ref[...] = dar_b
        dmu_ref[...] += _colsum(dzs_ref[...] * (zp - zv))

    dzs, tw_b, al_b, dwr_b, dar_b, d_mu, d_w0, d_a0, d_kk, d_ka = _rowcall(
        rw_prep_bwd, "rwkv_prep_bwd", t_dim, bt_many,
        [(z, 0, sec), dr_c, dr_b, dw_c, dk_c, dk_b, dv_c, dv_b, dkk_c, da_c, dg_rw], rw_consts,
        [(sec, F32), (lora, BF16), (lora, BF16), (c_rw, BF16), (c_rw, BF16)],
        acc_outs=[(1, sec), (1, c_rw), (1, c_rw), (1, c_rw), (1, c_rw)], scratch=[pltpu.VMEM((1, sec), F32)])
    d_wl = _matmul(tw_b, dwr_b, "tn", "d_w_lora", out_dtype=BF16)
    d_wa = _matmul(al_b, dar_b, "tn", "d_a_lora", out_dtype=BF16)

    def shift_bwd(dzs_ref, mu_ref, dz_ref, carry):
        _first_step_zero(carry)
        d = dzs_ref[...]
        nbt = d.shape[0]
        nxt = pltpu.roll(d, nbt - 1, 0)
        rowi = lax.broadcasted_iota(jnp.int32, d.shape, 0)
        nxt = jnp.where(rowi == nbt - 1, carry[...], nxt)
        carry[...] = d[0:1, :]
        m = mu_ref[...]
        dz_ref[...] = (d * (1.0 - m) + nxt * m).astype(BF16)

    (dz_rw,) = _rowcall(shift_bwd, "shift_bwd", t_dim, bt, [dzs], [mu], [(sec, BF16)],
                        scratch=[pltpu.VMEM((1, sec), F32)], reverse=True)

    fox_parts = [dq_f.astype(BF16), dk_f.astype(BF16), dv_f.astype(BF16), dg_fox, dz_fl]
    if sec > 4 * c_fox + LANES:
        fox_parts.append(jnp.zeros((t_dim, sec - 4 * c_fox - LANES), BF16))
    dz = jnp.concatenate([dz_rw] + fox_parts + [dz_gate], axis=1)
    d_wz = _matmul(h, dz, "tn", "d_w_in", out_dtype=BF16, bn=1408)
    rw_cols, fox_cols = c4 + 2 * lora, 4 * c_fox + h_fox
    d_wi = jnp.concatenate([d_wz[:, :rw_cols], d_wz[:, sec:sec + fox_cols], d_wz[:, 2 * sec:2 * sec + 2 * d_model]], axis=1)
    late = {"w_in": d_wi, "rw_w_lora_up": d_wl, "rw_a_lora_up": d_wa}
    dh, late_recv = _matmul(dz, wz, "nt", "d_h", bk=sec,
                            exchange=_Exchange([_split_shards(g, axis[n]) for n, g in late.items()]))
    recv.update(zip(late, late_recv))

    def norm_in_bwd(dh_ref, x_ref, dx1_ref, g_ref, dx_ref, dg1_ref):
        _first_step_zero(dg1_ref)
        xv = x_ref[...]
        rms = lax.rsqrt(jnp.mean(xv * xv, axis=-1, keepdims=True) + NORM_EPS)
        xn = xv * rms
        d = dh_ref[...]
        dg1_ref[...] += _colsum(d * xn)
        gy = d * g_ref[...]
        dx_ref[...] = dx1_ref[...] + rms * (gy - xn * jnp.mean(xn * gy, axis=-1, keepdims=True))

    dx, d_g1 = _rowcall(norm_in_bwd, "norm_in_bwd", t_dim, bt, [dh, x, dx1], [norm_g], [(d_model, F32)],
                        acc_outs=[(1, d_model)])

    rep_grads = {"norm_g": d_g1, "rw_shift_mu": d_mu[:, 0:c4 + 2 * lora], "rw_w0": d_w0, "rw_a0": d_a0, "rw_k_k": d_kk,
                 "rw_k_a": d_ka, "rw_r_k": d_rk, "rw_ln_g": d_lng, "rw_ln_b": d_lnb, "fox_b_f": d_bf[0:h_fox, 0],
                 "ple_norm_g": d_g2, "final_norm_g": d_g3}
    return loss[0, 0], dx, recv, rep_grads


def kernel(x, p, norm_g, w_in, rw_shift_mu, rw_w0, rw_w_lora_up, rw_a0, rw_a_lora_up, rw_k_k, rw_k_a, rw_r_k, rw_ln_g, rw_ln_b, fox_b_f, w_up_rwkv, w_up_fox, w_out, ple_proj, ple_gate_w, ple_norm_g, final_norm_g, loss_target, m_norm_g, m_w_in, m_rw_shift_mu, m_rw_w0, m_rw_w_lora_up, m_rw_a0, m_rw_a_lora_up, m_rw_k_k, m_rw_k_a, m_rw_r_k, m_rw_ln_g, m_rw_ln_b, m_fox_b_f, m_w_up_rwkv, m_w_up_fox, m_w_out, m_ple_proj, m_ple_gate_w, m_ple_norm_g, m_final_norm_g, v_norm_g, v_w_in, v_rw_shift_mu, v_rw_w0, v_rw_w_lora_up, v_rw_a0, v_rw_a_lora_up, v_rw_k_k, v_rw_k_a, v_rw_r_k, v_rw_ln_g, v_rw_ln_b, v_fox_b_f, v_w_up_rwkv, v_w_up_fox, v_w_out, v_ple_proj, v_ple_gate_w, v_ple_norm_g, v_final_norm_g):
    args = locals()
    w = {n: args[n] for n in WEIGHTS}
    mom = {n: args["m_" + n] for n in WEIGHTS}
    var = {n: args["v_" + n] for n in WEIGHTS}

    t_dim, d_model = x.shape[1], x.shape[2]
    c_rw, lora = rw_w0.shape[1], rw_w_lora_up.shape[1]
    h_fox = fox_b_f.shape[1]
    c_fox = h_fox * HEAD
    rw_cols, fox_cols, gate_cols = 4 * c_rw + 2 * lora, 4 * c_fox + h_fox, 2 * d_model
    sec = max(rw_cols, 4 * c_fox + LANES, _round_up(gate_cols, LANES))
    assert c_rw % LANES == 0 and c_fox % LANES == 0 and rw_cols % LANES == 0 and h_fox <= LANES and d_model % LANES == 0
    assert rw_a_lora_up.shape[1] == lora and w_in.shape[2] * NDEV == rw_cols + fox_cols + gate_cols

    assert SHARDED[0] == ("w_in", 1)
    (w_in_all,) = _all_gather([w_in[0].astype(BF16)], "gather_w_in")
    other_shards = [w[n][0].astype(BF16) for n, _ in SHARDED[1:]]

    def to_sections(wi):
        pad = lambda a, width: jnp.pad(a, ((0, 0), (0, width - a.shape[1])))
        return jnp.concatenate([pad(wi[:, :rw_cols], sec), pad(wi[:, rw_cols:rw_cols + fox_cols], sec),
                                pad(wi[:, rw_cols + fox_cols:], sec)], axis=1)

    rep = {n: w[n] for n in REPLICATED}
    dims = (t_dim, d_model, c_rw, lora, c_fox, h_fox, sec)
    loss_local, grad_x, recv, rep_grads = _local_step(
        x[0], p[0, 0], loss_target[0], to_sections(_join_shards(w_in_all, 1)), other_shards, rep, dims)

    rep_sizes = [w[n].size for n in REPLICATED]
    rep_offs = [sum(rep_sizes[:i]) for i in range(len(rep_sizes))]
    pack_rep = lambda tree: _pack_rows(jnp.concatenate([tree[n].astype(F32).reshape(-1) for n in REPLICATED]))
    (small_all,) = _grad_exchange([pack_rep(rep_grads)], "exchange_replicated")
    kinds = ("grad", "delta", "new_m", "new_v")
    outs = {}
    for n, _ in SHARDED:
        for kind, buf in zip(kinds, _adamw(recv[n], w[n][0], mom[n][0], var[n][0], "adamw_" + n)):
            outs[kind, n] = buf
    for kind, buf in zip(kinds, _adamw(small_all, pack_rep(w), pack_rep(mom), pack_rep(var), "adamw_replicated")):
        flat = buf.reshape(-1)
        for n, o, s in zip(REPLICATED, rep_offs, rep_sizes):
            outs[kind, n] = flat[o:o + s].reshape(w[n].shape)
    loss = lax.psum(loss_local, MESH_AXES)
    return (loss, grad_x[None], *[outs[kind, n] for kind in ("grad", "delta", "new_m", "new_v") for n in WEIGHTS])
```

```python
import functools
import math

import jax
import jax.numpy as jnp
from jax import lax
from jax.experimental import pallas as pl
from jax.experimental.pallas import tpu as pltpu

F32, BF16 = jnp.float32, jnp.bfloat16
HI = lax.Precision.HIGHEST
LANES = 128
SUBLANES = 8
HEAD = 64
NORM_EPS = 1e-6
GN_EPS = 64e-5
VMEM_LIMIT = 56 * 1024 * 1024
NDEV = 8
PACK_W = 1024
PACK_ALIGN = 16 * PACK_W
MESH_AXES = ("x", "y", "c")
MESH = pl.DeviceIdType.MESH

ADAM_LR, ADAM_B1, ADAM_B2, ADAM_EPS, ADAM_WD, ADAM_STEP = 0.001, 0.9, 0.999, 1e-08, 0.01, 10


def _round_up(n, m):
    return (n + m - 1) // m * m


def _pick(dim, pref, align=LANES):
    if dim <= pref:
        return dim
    best = None
    for cand in range(align, pref + 1, align):
        if dim % cand == 0:
            best = cand
    return dim if best is None else best


def _params(*sem):
    return pltpu.CompilerParams(dimension_semantics=sem, vmem_limit_bytes=VMEM_LIMIT)


def _sigmoid(v):
    return jax.nn.sigmoid(v)


def _log_sigmoid(v):
    return jnp.minimum(v, 0.0) - jnp.log(1.0 + jnp.exp(-jnp.abs(v)))


_DOT_DIMS = {"nn": (((1,), (0,)), ((), ())), "nt": (((1,), (1,)), ((), ())), "tn": (((0,), (0,)), ((), ()))}


def _matmul(a, b, mode, name, out_dtype=F32, bm=512, bn=1024, bk=2048, exchange=None):
    if mode == "tn":
        k_dim, m_dim = a.shape
    else:
        m_dim, k_dim = a.shape
    n_dim = b.shape[0] if mode == "nt" else b.shape[1]
    bm, bn, bk = _pick(m_dim, bm), _pick(n_dim, bn), _pick(k_dim, bk)
    nk = k_dim // bk

    nx = 0 if exchange is None else exchange.n
    grid = (m_dim // bm, n_dim // bn, nk)

    def body(*refs):
        a_ref, b_ref, o_ref = refs[0], refs[1], refs[2 + nx]
        step = [pl.program_id(d) for d in range(3)]
        if nx:
            own_scratch = 1 if nk > 1 else 0
            ex_refs = (refs[2:2 + nx], refs[3 + nx:3 + 2 * nx], refs[3 + 2 * nx + own_scratch:])
            linear = (step[0] * grid[1] + step[1]) * grid[2] + step[2]

            @pl.when(linear == 0)
            def _():
                exchange.start(*ex_refs)

            if hasattr(exchange, "forward"):
                @pl.when(linear == (grid[0] * grid[1] * grid[2]) // 2)
                def _():
                    exchange.forward(*ex_refs)

        prod = lax.dot_general(a_ref[...].astype(BF16), b_ref[...].astype(BF16), _DOT_DIMS[mode],
                               preferred_element_type=F32)
        if nk == 1:
            o_ref[...] = prod.astype(o_ref.dtype)
        else:
            acc_ref, k = refs[3 + 2 * nx], step[2]

            @pl.when(k == 0)
            def _():
                acc_ref[...] = prod

            @pl.when(k > 0)
            def _():
                acc_ref[...] += prod

            @pl.when(k == nk - 1)
            def _():
                o_ref[...] = acc_ref[...].astype(o_ref.dtype)
        if nx:
            @pl.when(linear == grid[0] * grid[1] * grid[2] - 1)
            def _():
                exchange.finish(*ex_refs)

    if mode == "tn":
        a_spec = pl.BlockSpec((bk, bm), lambda i, j, k: (k, i))
    else:
        a_spec = pl.BlockSpec((bm, bk), lambda i, j, k: (i, k))
    if mode == "nt":
        b_spec = pl.BlockSpec((bn, bk), lambda i, j, k: (j, k))
    else:
        b_spec = pl.BlockSpec((bk, bn), lambda i, j, k: (k, j))
    out_spec = pl.BlockSpec((bm, bn), lambda i, j, k: (i, j))
    out_shape = jax.ShapeDtypeStruct((m_dim, n_dim), out_dtype)
    acc_scratch = [pltpu.VMEM((bm, bn), F32)] if nk > 1 else []
    if exchange is None:
        return pl.pallas_call(
            body, name=name, grid=grid, in_specs=[a_spec, b_spec], out_specs=out_spec, out_shape=out_shape,
            scratch_shapes=acc_scratch, compiler_params=_params("parallel", "parallel", "arbitrary"),
        )(a, b)
    outs = pl.pallas_call(
        body, name=name, grid=grid, in_specs=[a_spec, b_spec] + [HBM_SPEC] * nx,
        out_specs=[out_spec] + [HBM_SPEC] * nx, out_shape=[out_shape] + exchange.out_shape,
        scratch_shapes=acc_scratch + exchange.scratch, compiler_params=_params("arbitrary", "arbitrary", "arbitrary"),
    )(a, b, *exchange.arrays)
    return outs[0], outs[1:]


def _rowcall(body, name, t_dim, bt, row_ins, const_ins, row_outs, acc_outs=(), scratch=(), reverse=False):
    nt = t_dim // bt

    def rmap(i):
        return nt - 1 - i if reverse else i

    in_specs, args = [], []
    for item in row_ins:
        arr, cb, w = item if isinstance(item, tuple) else (item, 0, item.shape[1])
        in_specs.append(pl.BlockSpec((bt, w), lambda i, cb=cb: (rmap(i), cb)))
        args.append(arr)
    for arr in const_ins:
        in_specs.append(pl.BlockSpec(arr.shape, lambda i, nd=arr.ndim: (0,) * nd))
        args.append(arr)
    out_specs = [pl.BlockSpec((bt, w), lambda i: (rmap(i), 0)) for w, _ in row_outs]
    out_shape = [jax.ShapeDtypeStruct((t_dim, w), dt) for w, dt in row_outs]
    for shp in acc_outs:
        out_specs.append(pl.BlockSpec(shp, lambda i, nd=len(shp): (0,) * nd))
        out_shape.append(jax.ShapeDtypeStruct(shp, F32))
    return pl.pallas_call(
        body, name=name, grid=(nt,), in_specs=in_specs, out_specs=out_specs, out_shape=out_shape,
        scratch_shapes=list(scratch), compiler_params=_params("arbitrary"),
    )(*args)


def _first_step_zero(*refs):
    @pl.when(pl.program_id(0) == 0)
    def _():
        for r in refs:
            r[...] = jnp.zeros_like(r)


def _colsum(v):
    return jnp.sum(v, axis=0, keepdims=True)


def _headsum(v, e):
    parts = [jnp.dot(v[:, p * LANES:(p + 1) * LANES], e, precision=HI, preferred_element_type=F32)
             for p in range(v.shape[1] // LANES)]
    return parts[0] if len(parts) == 1 else jnp.concatenate(parts, axis=1)


def _shift_down(v, carry_ref):
    bt = v.shape[0]
    prev = pltpu.roll(v, 1, 0)
    row = lax.broadcasted_iota(jnp.int32, v.shape, 0)
    prev = jnp.where(row == 0, carry_ref[...], prev)
    carry_ref[...] = v[bt - 1:bt, :]
    return prev


def _pair_consts():
    lane = lax.broadcasted_iota(jnp.int32, (1, LANES), 1)
    m0 = (lane < HEAD).astype(F32)
    m1 = 1.0 - m0
    sub = lax.broadcasted_iota(jnp.int32, (HEAD, LANES), 0)
    lane2 = lax.broadcasted_iota(jnp.int32, (HEAD, LANES), 1)
    i0 = (lane2 == sub).astype(F32)
    i1 = (lane2 == sub + HEAD).astype(F32)
    return m0, m1, i0, i1


def _head_masks():
    lane = lax.broadcasted_iota(jnp.int32, (1, LANES), 1)
    first = (lane < HEAD).astype(F32)
    return first, 1.0 - first


def _head_ones(dtype=BF16):
    lane = jnp.arange(LANES)
    return (lane[:, None] // HEAD == lane[None, :] // HEAD).astype(dtype)


def _lanesum(v):
    return jnp.sum(v, axis=1, keepdims=True)


def _split_bf16(v):
    hi = v.astype(BF16).astype(F32)
    rest = v - hi
    mid = rest.astype(BF16).astype(F32)
    return hi, mid, (rest - mid).astype(BF16).astype(F32)


def _col_pair(parts, j, i01, e_bf16):
    lhs = jnp.concatenate([(part[j:j + 1] * i01).astype(BF16) for part in parts], axis=0)
    out = jnp.dot(lhs, e_bf16, preferred_element_type=F32)
    return (out[0:HEAD] + out[HEAD:2 * HEAD]) + out[2 * HEAD:3 * HEAD]


def _row_pair(c0, c1, i0, i1):
    return _colsum(c0 * i0 + c1 * i1)


def _scan_fwd(r, w, k, v, kk, a, tc, npb):
    t_dim, c_dim = r.shape
    wb = LANES * npb
    tc = _pick(t_dim, tc, SUBLANES)

    def body(r_ref, w_ref, k_ref, v_ref, kk_ref, a_ref, e_ref, y_ref, sall_ref, sfin_ref, s_ref):
        @pl.when(pl.program_id(1) == 0)
        def _():
            s_ref[...] = jnp.zeros_like(s_ref)

        m0, m1, i0, i1 = _pair_consts()
        i01 = i0 + i1
        e = e_ref[...]
        sub8 = lax.broadcasted_iota(jnp.int32, (SUBLANES, LANES), 0)
        lanes = [slice(q * LANES, (q + 1) * LANES) for q in range(npb)]
        ng = tc // SUBLANES

        def lanesums(tiles):
            sums = _lanesum(jnp.concatenate(tiles, axis=0))
            return [sums[i * HEAD:(i + 1) * HEAD] for i in range(len(tiles))]

        def halves(s, row):
            return [s * (row * m0), s * (row * m1)]

        def group(gi, carry):
            base = pl.multiple_of(gi * SUBLANES, SUBLANES)
            rows = pl.ds(base, SUBLANES)
            s = list(carry)
            r8 = [r_ref[rows, sl] for sl in lanes]
            w8 = [w_ref[rows, sl] for sl in lanes]
            k8 = [k_ref[rows, sl] for sl in lanes]
            kk8 = [kk_ref[rows, sl] for sl in lanes]
            b8 = [kk8[q] * a_ref[rows, lanes[q]] for q in range(npb)]
            v8 = [_split_bf16(v_ref[rows, sl]) for sl in lanes]
            y8 = [jnp.zeros((SUBLANES, LANES), F32)] * npb
            for j in range(SUBLANES + 1):
                one, before = slice(j, j + 1), slice(j - 1, j)
                tiles = []
                for q in range(npb):
                    if j < SUBLANES:
                        tiles += halves(s[q], kk8[q][one])
                    if j > 0:
                        tiles += halves(s[q], r8[q][before])
                cols = lanesums(tiles)
                per = len(tiles) // npb
                for q in range(npb):
                    mine = cols[q * per:(q + 1) * per]
                    if j > 0:
                        y8[q] = jnp.where(sub8 == j - 1, _row_pair(mine[-2], mine[-1], i0, i1), y8[q])
                    if j < SUBLANES:
                        sall_ref[base + j, :, lanes[q]] = s[q]
                        sb = mine[0] * m0 + mine[1] * m1
                        s[q] = s[q] * w8[q][one] - sb * b8[q][one] + _col_pair(v8[q], j, i01, e) * k8[q][one]
            for q in range(npb):
                y_ref[rows, lanes[q]] = y8[q]
            return tuple(s)

        init = tuple(s_ref[:, q * LANES:(q + 1) * LANES] for q in range(npb))
        fin = lax.fori_loop(0, ng, group, init)
        for q in range(npb):
            s_ref[:, q * LANES:(q + 1) * LANES] = fin[q]
            sfin_ref[:, q * LANES:(q + 1) * LANES] = fin[q]

    row = pl.BlockSpec((tc, wb), lambda p, c: (c, p))
    return pl.pallas_call(
        body, name="rwkv_scan_fwd", grid=(c_dim // wb, t_dim // tc),
        in_specs=[row] * 6 + [pl.BlockSpec((LANES, LANES), lambda p, c: (0, 0))],
        out_specs=[row, pl.BlockSpec((tc, HEAD, wb), lambda p, c: (c, 0, p)), pl.BlockSpec((HEAD, wb), lambda p, c: (0, p))],
        out_shape=[jax.ShapeDtypeStruct((t_dim, c_dim), F32), jax.ShapeDtypeStruct((t_dim, HEAD, c_dim), F32),
                   jax.ShapeDtypeStruct((HEAD, c_dim), F32)],
        scratch_shapes=[pltpu.VMEM((HEAD, wb), F32)],
        compiler_params=_params("parallel", "arbitrary"),
    )(r, w, k, v, kk, a, _head_ones())


def _scan_bwd(r, w, k, v, kk, a, dy, sall, sfin, tc, npb, exchange):
    t_dim, c_dim = r.shape
    wb = LANES * npb
    tc = _pick(t_dim, tc, SUBLANES)
    nc = t_dim // tc
    nx = exchange.n
    n_blocks = c_dim // wb

    def body(*refs):
        r_ref, w_ref, k_ref, v_ref, kk_ref, a_ref, dy_ref, sall_ref, sfin_ref, e_ref = refs[:10]
        dr_ref, dw_ref, dk_ref, dv_ref, dkk_ref, da_ref = refs[10 + nx:16 + nx]
        ds_ref, sn_ref = refs[16 + 2 * nx:18 + 2 * nx]
        here_p, here_c = pl.program_id(0), pl.program_id(1)
        ex_refs = (refs[10:10 + nx], refs[16 + nx:16 + 2 * nx], refs[18 + 2 * nx:])

        @pl.when((here_p == 0) & (here_c == 0))
        def _():
            exchange.start(*ex_refs)

        @pl.when(pl.program_id(1) == 0)
        def _():
            ds_ref[...] = jnp.zeros_like(ds_ref)
            sn_ref[...] = sfin_ref[...]

        m0, m1, i0, i1 = _pair_consts()
        i01 = i0 + i1
        e = e_ref[...]
        sub8 = lax.broadcasted_iota(jnp.int32, (SUBLANES, LANES), 0)
        lanes = [slice(q * LANES, (q + 1) * LANES) for q in range(npb)]
        ng = tc // SUBLANES

        def halves(s, row):
            return [s * (row * m0), s * (row * m1)]

        def group(gi, carry):
            base = pl.multiple_of((ng - 1 - gi) * SUBLANES, SUBLANES)
            rows = pl.ds(base, SUBLANES)
            ds = list(carry)
            r8 = [r_ref[rows, sl] for sl in lanes]
            w8 = [w_ref[rows, sl] for sl in lanes]
            k8 = [k_ref[rows, sl] for sl in lanes]
            kk8 = [kk_ref[rows, sl] for sl in lanes]
            a8 = [a_ref[rows, sl] for sl in lanes]
            v8 = [_split_bf16(v_ref[rows, sl]) for sl in lanes]
            dy8 = [_split_bf16(dy_ref[rows, sl]) for sl in lanes]
            zero8 = jnp.zeros((SUBLANES, LANES), F32)
            dr8, dw8, dk8, dv8, dkk8, da8 = ([zero8] * npb for _ in range(6))
            for j in reversed(range(SUBLANES)):
                one = slice(j, j + 1)
                here = sub8 == j
                d, s_prev, tiles = [], [], []
                for q in range(npb):
                    dyb = _col_pair(dy8[q], j, i01, e)
                    s_prev.append(sall_ref[base + j, :, lanes[q]])
                    dr8[q] = jnp.where(here, _colsum(sn_ref[:, lanes[q]] * dyb), dr8[q])
                    sn_ref[:, lanes[q]] = s_prev[q]
                    d.append(ds[q] + dyb * r8[q][one])
                for q in range(npb):
                    tiles += halves(d[q], kk8[q][one] * a8[q][one])
                for q in range(npb):
                    tiles += halves(d[q], k8[q][one]) + halves(s_prev[q], kk8[q][one])
                sums = _lanesum(jnp.concatenate(tiles, axis=0))
                cols = [sums[i * HEAD:(i + 1) * HEAD] for i in range(len(tiles))]
                for q in range(npb):
                    kkr, ar = kk8[q][one], a8[q][one]
                    dsb = -(cols[2 * q] * m0 + cols[2 * q + 1] * m1)
                    ds[q] = d[q] * w8[q][one] + dsb * kkr
                    rest = cols[2 * npb + 4 * q:2 * npb + 4 * q + 4]
                    dv8[q] = jnp.where(here, _row_pair(rest[0], rest[1], i0, i1), dv8[q])
                    sb = rest[2] * m0 + rest[3] * m1
                    db = -_colsum(d[q] * sb)
                    dk8[q] = jnp.where(here, _colsum(d[q] * _col_pair(v8[q], j, i01, e)), dk8[q])
                    dw8[q] = jnp.where(here, _colsum(d[q] * s_prev[q]), dw8[q])
                    dkk8[q] = jnp.where(here, _colsum(s_prev[q] * dsb) + db * ar, dkk8[q])
                    da8[q] = jnp.where(here, db * kkr, da8[q])
            for q in range(npb):
                sl = lanes[q]
                dr_ref[rows, sl], dw_ref[rows, sl], dk_ref[rows, sl] = dr8[q], dw8[q], dk8[q]
                dv_ref[rows, sl], dkk_ref[rows, sl], da_ref[rows, sl] = dv8[q], dkk8[q], da8[q]
            return tuple(ds)

        init = tuple(ds_ref[:, q * LANES:(q + 1) * LANES] for q in range(npb))
        fin = lax.fori_loop(0, ng, group, init)
        for q in range(npb):
            ds_ref[:, q * LANES:(q + 1) * LANES] = fin[q]

        @pl.when((here_p == n_blocks - 1) & (here_c == nc - 1))
        def _():
            exchange.finish(*ex_refs)

    row = pl.BlockSpec((tc, wb), lambda p, c: (nc - 1 - c, p))
    outs = pl.pallas_call(
        body, name="rwkv_scan_bwd", grid=(n_blocks, nc),
        in_specs=[row] * 7 + [pl.BlockSpec((tc, HEAD, wb), lambda p, c: (nc - 1 - c, 0, p)),
                              pl.BlockSpec((HEAD, wb), lambda p, c: (0, p)),
                              pl.BlockSpec((LANES, LANES), lambda p, c: (0, 0))] + [HBM_SPEC] * nx,
        out_specs=[row] * 6 + [HBM_SPEC] * nx,
        out_shape=[jax.ShapeDtypeStruct((t_dim, c_dim), F32)] * 6 + exchange.out_shape,
        scratch_shapes=[pltpu.VMEM((HEAD, wb), F32)] * 2 + exchange.scratch,
        compiler_params=_params("arbitrary", "arbitrary"),
    )(r, w, k, v, kk, a, dy, sall, sfin, _head_ones(), *exchange.arrays)
    return outs[:6], outs[6:]


def _fox_fwd(z, ct, cq, q_cb, k_cb, v_cb, n_pairs, blk):
    t_dim = z.shape[0]
    blk = _pick(t_dim, blk)
    nq = t_dim // blk
    scale = HEAD ** -0.5

    def body(q_ref, k_ref, v_ref, ct_ref, cq_ref, o_ref, lse_ref):
        i = pl.program_id(1)
        rowi = lax.broadcasted_iota(jnp.int32, (blk, blk), 0)
        coli = lax.broadcasted_iota(jnp.int32, (blk, blk), 1)
        masks = _head_masks()
        qv = q_ref[...]
        qs = [(qv * mk).astype(BF16) for mk in masks]
        cqs = [cq_ref[:, hh:hh + 1] for hh in range(2)]

        def kv_step(j, carry, masked):
            rows = pl.ds(pl.multiple_of(j * blk, blk), blk)
            kb = k_ref[rows, :].astype(BF16)
            vv = v_ref[rows, :]
            stats, acc = list(carry[:4]), carry[4]
            rescale, add = 0.0, 0.0
            for hh in range(2):
                m, l = stats[2 * hh], stats[2 * hh + 1]
                s = (lax.dot_general(qs[hh], kb, _DOT_DIMS["nt"], preferred_element_type=F32) * scale
                     + (cqs[hh] - ct_ref[hh:hh + 1, rows]))
                if masked:
                    s = jnp.where(rowi >= coli, s, -jnp.inf)
                m_new = jnp.maximum(m, jnp.max(s, axis=1, keepdims=True))
                alpha = jnp.exp(m - m_new)
                pr = jnp.exp(s - m_new)
                stats[2 * hh], stats[2 * hh + 1] = m_new, l * alpha + jnp.sum(pr, axis=1, keepdims=True)
                rescale = rescale + alpha * masks[hh]
                hi = pr.astype(BF16)
                both = jnp.concatenate([hi, (pr - hi.astype(F32)).astype(BF16)], axis=1)
                vh = (vv * masks[hh]).astype(BF16)
                add = add + jnp.dot(both, jnp.concatenate([vh, vh], axis=0), preferred_element_type=F32)
            return (*stats, acc * rescale + add)

        neg, zero = jnp.full((blk, 1), -jnp.inf, F32), jnp.zeros((blk, 1), F32)
        carry = lax.fori_loop(0, i, functools.partial(kv_step, masked=False),
                              (neg, zero, neg, zero, jnp.zeros((blk, LANES), F32)))
        m0, l0, m1, l1, acc = kv_step(i, carry, True)
        o_ref[...] = acc * (masks[0] / l0 + masks[1] / l1)
        lse_ref[:, 0:1] = m0 + jnp.log(l0)
        lse_ref[:, 1:2] = m1 + jnp.log(l1)

    full = lambda cb: pl.BlockSpec((t_dim, LANES), lambda p, i, cb=cb: (0, cb + p))
    return pl.pallas_call(
        body, name="fox_attn_fwd", grid=(n_pairs, nq),
        in_specs=[pl.BlockSpec((blk, LANES), lambda p, i: (i, q_cb + p)), full(k_cb), full(v_cb),
                  pl.BlockSpec((None, SUBLANES, t_dim), lambda p, i: (p, 0, 0)),
                  pl.BlockSpec((None, blk, 2), lambda p, i: (p, i, 0))],
        out_specs=[pl.BlockSpec((blk, LANES), lambda p, i: (i, p)), pl.BlockSpec((None, blk, 2), lambda p, i: (p, i, 0))],
        out_shape=[jax.ShapeDtypeStruct((t_dim, n_pairs * LANES), F32), jax.ShapeDtypeStruct((n_pairs, t_dim, 2), F32)],
        compiler_params=_params("parallel", "arbitrary"),
    )(z, z, z, ct, cq)


def _fox_bwd(z, ct, cq, rowdot, do, lse, q_cb, k_cb, v_cb, n_pairs, blk):
    t_dim = z.shape[0]
    blk = _pick(t_dim, blk)
    nb = t_dim // blk
    scale = HEAD ** -0.5

    def body(q_ref, k_ref, v_ref, ct_ref, cq_ref, rd_ref, do_ref, lse_ref, dq_ref, dk_ref, dv_ref, dc_ref):
        j = pl.program_id(1)

        @pl.when(j == 0)
        def _():
            dq_ref[...] = jnp.zeros_like(dq_ref)

        rowi = lax.broadcasted_iota(jnp.int32, (blk, blk), 0)
        coli = lax.broadcasted_iota(jnp.int32, (blk, blk), 1)
        krows = pl.ds(pl.multiple_of(j * blk, blk), blk)
        masks = _head_masks()
        kv, vb = k_ref[...], v_ref[...].astype(BF16)
        kb = kv.astype(BF16)
        ks = [(kv * mk).astype(BF16) for mk in masks]
        cks = [ct_ref[hh:hh + 1, krows] for hh in range(2)]

        def q_step(i, carry, masked):
            dk, dv, dcs = carry[0], carry[1], list(carry[2:])
            rows = pl.ds(pl.multiple_of(i * blk, blk), blk)
            qv, dov = q_ref[rows, :], do_ref[rows, :]
            dq = 0.0
            for hh in range(2):
                qh = (qv * masks[hh]).astype(BF16)
                doh = (dov * masks[hh]).astype(BF16)
                s = (lax.dot_general(qh, kb, _DOT_DIMS["nt"], preferred_element_type=F32) * scale
                     + (cq_ref[rows, hh:hh + 1] - cks[hh]))
                pr = jnp.exp(s - lse_ref[rows, hh:hh + 1])
                if masked:
                    pr = jnp.where(rowi >= coli, pr, 0.0)
                dv = dv + lax.dot_general(pr.astype(BF16), doh, _DOT_DIMS["tn"], preferred_element_type=F32)
                dp = lax.dot_general(doh, vb, _DOT_DIMS["nt"], preferred_element_type=F32)
                ds = pr * (dp - rd_ref[rows, hh * HEAD:hh * HEAD + 1])
                dsb = ds.astype(BF16)
                dq = dq + jnp.dot(dsb, ks[hh], preferred_element_type=F32)
                dk = dk + lax.dot_general(dsb, qh, _DOT_DIMS["tn"], preferred_element_type=F32)
                dcs[hh] = dcs[hh] - _colsum(ds)
            dq_ref[rows, :] += dq * scale
            return (dk, dv, *dcs)

        zero_row = jnp.zeros((1, blk), F32)
        init = (jnp.zeros((blk, LANES), F32), jnp.zeros((blk, LANES), F32), zero_row, zero_row)
        carry = q_step(j, init, True)
        dk, dv, dc0, dc1 = lax.fori_loop(j + 1, nb, functools.partial(q_step, masked=False), carry)
        dk_ref[...] = dk * scale
        dv_ref[...] = dv
        dc_ref[0:1, :] = dc0
        dc_ref[1:2, :] = dc1
        dc_ref[2:SUBLANES, :] = jnp.zeros((SUBLANES - 2, blk), F32)

    full = lambda: pl.BlockSpec((t_dim, LANES), lambda p, j: (0, p))
    blkspec = pl.BlockSpec((blk, LANES), lambda p, j: (j, p))
    return pl.pallas_call(
        body, name="fox_attn_bwd", grid=(n_pairs, nb),
        in_specs=[pl.BlockSpec((t_dim, LANES), lambda p, j: (0, q_cb + p)),
                  pl.BlockSpec((blk, LANES), lambda p, j: (j, k_cb + p)),
                  pl.BlockSpec((blk, LANES), lambda p, j: (j, v_cb + p)),
                  pl.BlockSpec((None, SUBLANES, t_dim), lambda p, j: (p, 0, 0)),
                  pl.BlockSpec((None, t_dim, 2), lambda p, j: (p, 0, 0)), full(), full(),
                  pl.BlockSpec((None, t_dim, 2), lambda p, j: (p, 0, 0))],
        out_specs=[full(), blkspec, blkspec, pl.BlockSpec((None, SUBLANES, blk), lambda p, j: (p, 0, j))],
        out_shape=[jax.ShapeDtypeStruct((t_dim, n_pairs * LANES), F32)] * 3
                  + [jax.ShapeDtypeStruct((n_pairs, SUBLANES, t_dim), F32)],
        compiler_params=_params("parallel", "arbitrary"),
    )(z, z, z, ct, cq, rowdot, do, lse)


HBM_SPEC = pl.BlockSpec(memory_space=pltpu.HBM)


class _Gather:
    def __init__(self, shards):
        self.arrays = list(shards)
        self.n = len(self.arrays)
        self.out_shape = [jax.ShapeDtypeStruct((NDEV,) + s.shape, s.dtype) for s in self.arrays]
        self.scratch = [pltpu.SemaphoreType.DMA((7 * self.n,)), pltpu.SemaphoreType.DMA((7 * self.n,)),
                        pltpu.SemaphoreType.DMA((self.n,))]

    def _plan(self, x_refs, out_refs, sems):
        send_sems, recv_sems, local_sems = sems
        x, y, c = lax.axis_index("x"), lax.axis_index("y"), lax.axis_index("c")
        me, sibling = (x, y, c), (x, y, 1 - c)
        chips = [(1 - x, y), (x, 1 - y), (1 - x, 1 - y)]

        def copy(a, k, block, to, from_input=False):
            px, py, pc = block
            slot = out_refs[a].at[4 * px + 2 * py + pc]
            return pltpu.make_async_remote_copy(
                src_ref=x_refs[a] if from_input else slot, dst_ref=slot,
                send_sem=send_sems.at[7 * a + k], recv_sem=recv_sems.at[7 * a + k], device_id=to, device_id_type=MESH)

        mine = [pltpu.make_async_copy(x_refs[a], out_refs[a].at[4 * x + 2 * y + c], local_sems.at[a])
                for a in range(self.n)]
        first = []
        for a in range(self.n):
            first += [copy(a, 1 + j, me, (*chip, c), from_input=True) for j, chip in enumerate(chips)]
            first.append(copy(a, 0, me, sibling, from_input=True))
        over_ici = [copy(a, 1 + j, (*chip, c), me) for a in range(self.n) for j, chip in enumerate(chips)]
        passed = [copy(a, 4 + j, (*chip, c), sibling) for a in range(self.n) for j, chip in enumerate(chips)]
        from_sibling = []
        for a in range(self.n):
            from_sibling.append(copy(a, 0, sibling, me))
            from_sibling += [copy(a, 4 + j, (*chip, 1 - c), me) for j, chip in enumerate(chips)]
        return mine, first, over_ici, passed, from_sibling

    def start(self, *refs):
        mine, first, _, _, _ = self._plan(*refs)
        for cp in mine + first:
            cp.start()

    def forward(self, *refs):
        _, _, over_ici, passed, _ = self._plan(*refs)
        for arrived, onward in zip(over_ici, passed):
            arrived.wait_recv()
            onward.start()

    def finish(self, *refs):
        mine, first, _, passed, from_sibling = self._plan(*refs)
        for cp in from_sibling:
            cp.wait_recv()
        for cp in first + passed:
            cp.wait_send()
        for cp in mine:
            cp.wait()


def _all_gather(shards, name):
    ga = _Gather(shards)

    def body(*refs):
        parts = (refs[:ga.n], refs[ga.n:2 * ga.n], refs[2 * ga.n:])
        ga.start(*parts)
        ga.forward(*parts)
        ga.finish(*parts)

    return pl.pallas_call(body, name=name, out_shape=ga.out_shape, in_specs=[HBM_SPEC] * ga.n,
                          out_specs=[HBM_SPEC] * ga.n, scratch_shapes=ga.scratch)(*ga.arrays)


class _Exchange:
    def __init__(self, arrays):
        self.arrays = list(arrays)
        self.n = len(self.arrays)
        self.per_dest = [a.ndim == 3 for a in self.arrays]
        self.out_shape = [jax.ShapeDtypeStruct(a.shape if pd else (NDEV,) + a.shape, a.dtype)
                          for a, pd in zip(self.arrays, self.per_dest)]
        self.scratch = [pltpu.SemaphoreType.DMA((7 * self.n,)), pltpu.SemaphoreType.DMA((7 * self.n,)),
                        pltpu.SemaphoreType.DMA((self.n,))]

    def _copies(self, in_refs, out_refs, sems):
        send_sems, recv_sems, local_sems = sems
        x, y, c = lax.axis_index("x"), lax.axis_index("y"), lax.axis_index("c")
        me = 4 * x + 2 * y + c
        own, sends, recvs = [], [], []
        for a in range(self.n):
            mine = in_refs[a].at[me] if self.per_dest[a] else in_refs[a]
            own.append(pltpu.make_async_copy(mine, out_refs[a].at[me], local_sems.at[a]))
            for k in range(1, NDEV):
                px = 1 - x if k & 4 else x
                py = 1 - y if k & 2 else y
                pc = 1 - c if k & 1 else c
                peer = 4 * px + 2 * py + pc
                sem = dict(send_sem=send_sems.at[7 * a + k - 1], recv_sem=recv_sems.at[7 * a + k - 1],
                           device_id=(px, py, pc), device_id_type=MESH)
                src = in_refs[a].at[peer] if self.per_dest[a] else in_refs[a]
                sends.append(pltpu.make_async_remote_copy(src_ref=src, dst_ref=out_refs[a].at[me], **sem))
                recvs.append(pltpu.make_async_remote_copy(src_ref=src, dst_ref=out_refs[a].at[peer], **sem))
        return own, sends, recvs

    def start(self, in_refs, out_refs, sems):
        own, sends, _ = self._copies(in_refs, out_refs, sems)
        for cp in own + sends:
            cp.start()

    def finish(self, in_refs, out_refs, sems):
        own, sends, recvs = self._copies(in_refs, out_refs, sems)
        for cp in recvs:
            cp.wait_recv()
        for cp in sends:
            cp.wait_send()
        for cp in own:
            cp.wait()


def _grad_exchange(arrays, name):
    ex = _Exchange(arrays)

    def body(*refs):
        in_refs, out_refs, sems = refs[:ex.n], refs[ex.n:2 * ex.n], refs[2 * ex.n:]
        ex.start(in_refs, out_refs, sems)
        ex.finish(in_refs, out_refs, sems)

    return pl.pallas_call(body, name=name, out_shape=ex.out_shape, in_specs=[HBM_SPEC] * ex.n,
                          out_specs=[HBM_SPEC] * ex.n, scratch_shapes=ex.scratch)(*ex.arrays)


NCHIP = NDEV // 2


def _sibling_swap(parts, name):
    def body(p_ref, out_ref, send_sems, recv_sems):
        x, y, c = lax.axis_index("x"), lax.axis_index("y"), lax.axis_index("c")
        copies = [pltpu.make_async_remote_copy(
            src_ref=p_ref.at[2 * i + 1 - c], dst_ref=out_ref.at[i], send_sem=send_sems.at[i], recv_sem=recv_sems.at[i],
            device_id=(x, y, 1 - c), device_id_type=MESH) for i in range(NCHIP)]
        for cp in copies:
            cp.start()
        for cp in copies:
            cp.wait_recv()
        for cp in copies:
            cp.wait_send()

    return pl.pallas_call(
        body, name=name, out_shape=jax.ShapeDtypeStruct((NCHIP,) + parts.shape[1:], parts.dtype),
        in_specs=[HBM_SPEC], out_specs=HBM_SPEC,
        scratch_shapes=[pltpu.SemaphoreType.DMA((NCHIP,)), pltpu.SemaphoreType.DMA((NCHIP,))],
    )(parts)


def _add_pairs(a, b, name):
    n, rows, width = a.shape
    br = _pick(rows, 256, 2 * SUBLANES)

    def body(a_ref, b_ref, o_ref):
        o_ref[...] = (a_ref[...].astype(F32) + b_ref[...].astype(F32)).astype(o_ref.dtype)

    blk = pl.BlockSpec((None, br, width), lambda i, j: (i, j, 0))
    return pl.pallas_call(body, name=name, grid=(n, rows // br), in_specs=[blk, blk], out_specs=blk,
                          out_shape=jax.ShapeDtypeStruct(a.shape, a.dtype),
                          compiler_params=_params("parallel", "parallel"))(a, b)


class _ChipExchange:
    FLIPS = ((0, 1), (1, 0), (1, 1))

    def __init__(self, arrays):
        self.arrays = list(arrays)
        self.n = len(self.arrays)
        self.out_shape = [jax.ShapeDtypeStruct(a.shape, a.dtype) for a in self.arrays]
        self.scratch = [pltpu.SemaphoreType.DMA((3 * self.n,)), pltpu.SemaphoreType.DMA((3 * self.n,)),
                        pltpu.SemaphoreType.DMA((self.n,))]

    def _copies(self, in_refs, out_refs, sems):
        send_sems, recv_sems, local_sems = sems
        x, y, c = lax.axis_index("x"), lax.axis_index("y"), lax.axis_index("c")
        here = 2 * x + y
        own, sends, recvs = [], [], []
        for a in range(self.n):
            own.append(pltpu.make_async_copy(in_refs[a].at[here], out_refs[a].at[here], local_sems.at[a]))
            for k, (fx, fy) in enumerate(self.FLIPS):
                px = 1 - x if fx else x
                py = 1 - y if fy else y
                there = 2 * px + py
                sem = dict(send_sem=send_sems.at[3 * a + k], recv_sem=recv_sems.at[3 * a + k],
                           device_id=(px, py, c), device_id_type=MESH)
                src = in_refs[a].at[there]
                sends.append(pltpu.make_async_remote_copy(src_ref=src, dst_ref=out_refs[a].at[here], **sem))
                recvs.append(pltpu.make_async_remote_copy(src_ref=src, dst_ref=out_refs[a].at[there], **sem))
        return own, sends, recvs

    start = _Exchange.start
    finish = _Exchange.finish


def _adamw_body(p_ref, w_ref, m_ref, v_ref, g_out, d_out, m_out, v_out):
    g = p_ref[0].astype(F32)
    for d in range(1, p_ref.shape[0]):
        g = g + p_ref[d].astype(F32)
    mn = ADAM_B1 * m_ref[...] + (1.0 - ADAM_B1) * g
    vn = ADAM_B2 * v_ref[...] + (1.0 - ADAM_B2) * jnp.square(g)
    m_hat = mn / (1.0 - ADAM_B1 ** ADAM_STEP)
    v_hat = vn / (1.0 - ADAM_B2 ** ADAM_STEP)
    g_out[...] = g
    d_out[...] = -ADAM_LR * (m_hat / (jnp.sqrt(v_hat) + ADAM_EPS) + ADAM_WD * w_ref[...])
    m_out[...] = mn
    v_out[...] = vn


def _adamw(partials, w, m, v, name):
    rows, width = w.shape
    br = _pick(rows, 128, 2 * SUBLANES)
    body = functools.partial(_adamw_body)
    blk = pl.BlockSpec((br, width), lambda i: (i, 0))
    return pl.pallas_call(
        body, name=name, grid=(rows // br,),
        in_specs=[pl.BlockSpec((partials.shape[0], br, width), lambda i: (0, i, 0)), blk, blk, blk],
        out_specs=[pl.BlockSpec((None, br, width), lambda i: (0, i, 0))] * 4,
        out_shape=[jax.ShapeDtypeStruct((1, rows, width), F32)] * 4,
        compiler_params=_params("parallel"),
    )(partials, w, m, v)


def _pack_rows(flat):
    n = flat.shape[0]
    padded = _round_up(n, PACK_ALIGN)
    return jnp.pad(flat, (0, padded - n)).reshape(padded // PACK_W, PACK_W)


def _split_shards(full, axis):
    rows, cols = full.shape
    if axis == 0:
        return full.reshape(NDEV, rows // NDEV, cols)
    width = cols // NDEV
    return jnp.stack([full[:, d * width:(d + 1) * width] for d in range(NDEV)])


def _join_shards(blocks, axis):
    if axis == 0:
        return blocks.reshape(-1, blocks.shape[2])
    return jnp.concatenate([blocks[d] for d in range(NDEV)], axis=1)


SHARDED = (("w_in", 1), ("rw_w_lora_up", 1), ("rw_a_lora_up", 1), ("w_up_rwkv", 1), ("w_up_fox", 1),
           ("w_out", 0), ("ple_proj", 1), ("ple_gate_w", 0))
REPLICATED = ("norm_g", "rw_shift_mu", "rw_w0", "rw_a0", "rw_k_k", "rw_k_a", "rw_r_k", "rw_ln_g", "rw_ln_b",
              "fox_b_f", "ple_norm_g", "final_norm_g")
WEIGHTS = ("norm_g", "w_in", "rw_shift_mu", "rw_w0", "rw_w_lora_up", "rw_a0", "rw_a_lora_up", "rw_k_k", "rw_k_a",
           "rw_r_k", "rw_ln_g", "rw_ln_b", "fox_b_f", "w_up_rwkv", "w_up_fox", "w_out", "ple_proj", "ple_gate_w",
           "ple_norm_g", "final_norm_g")


def _local_step(x, p, tgt, wz, other_shards, rep, dims):
    t_dim, d_model, c_rw, lora, c_fox, h_fox, sec = dims
    bt = _pick(t_dim, 256, 2 * SUBLANES)
    bt_many = _pick(t_dim, 128, 2 * SUBLANES)
    n_pairs = c_fox // LANES
    row = lambda a: a.reshape(1, -1)
    norm_g, mu, w0, a0 = row(rep["norm_g"]), row(rep["rw_shift_mu"]), row(rep["rw_w0"]), row(rep["rw_a0"])
    k_k, k_a, r_k = row(rep["rw_k_k"]), row(rep["rw_k_a"]), row(rep["rw_r_k"])
    ln_g, ln_b = row(rep["rw_ln_g"]), row(rep["rw_ln_b"])
    g2, g3 = row(rep["ple_norm_g"]), row(rep["final_norm_g"])
    b_f = jnp.pad(row(rep["fox_b_f"]), ((0, 0), (0, LANES - h_fox)))
    e_head = _head_ones(F32)
    c4 = 4 * c_rw
    inv_d = 1.0 / d_model
    decay_k = math.exp(-0.5)

    def norm_in(x_ref, g_ref, h_ref):
        xv = x_ref[...]
        rms = lax.rsqrt(jnp.mean(xv * xv, axis=-1, keepdims=True) + NORM_EPS)
        h_ref[...] = (xv * rms * g_ref[...]).astype(BF16)

    (h,) = _rowcall(norm_in, "norm_in", t_dim, bt, [x], [norm_g], [(d_model, BF16)])
    z, gathered = _matmul(h, wz, "nn", "proj_in", bn=1408, exchange=_Gather(other_shards))
    wl, wa, wur, wuf, wo, pp, pg = [_join_shards(g, ax) for (_, ax), g in zip(SHARDED[1:], gathered)]

    def rw_values(zs, w0_ref, wl_ref, a0_ref, wa_ref, kk_ref, ka_ref, e_ref):
        k = zs[:, c_rw:2 * c_rw]
        tw = jnp.tanh(zs[:, c4:c4 + lora])
        al = zs[:, c4 + lora:c4 + 2 * lora]
        sw = _sigmoid(w0_ref[...] + jnp.dot(tw.astype(BF16), wl_ref[...], preferred_element_type=F32))
        decay = jnp.exp(-decay_k * sw)
        a = _sigmoid(a0_ref[...] + jnp.dot(al.astype(BF16), wa_ref[...], preferred_element_type=F32))
        kk0 = k * kk_ref[...]
        nrm = jnp.sqrt(_headsum(kk0 * kk0, e_ref[...]))
        inv = 1.0 / jnp.maximum(nrm, 1e-12)
        k2 = k * (1.0 + (a - 1.0) * ka_ref[...])
        return k, tw, al, sw, decay, a, kk0, nrm, inv, k2

    def rw_prep(z_ref, mu_ref, w0_ref, wl_ref, a0_ref, wa_ref, kk_ref, ka_ref, e_ref,
                r_o, w_o, k_o, v_o, kk_o, a_o, g_o, carry):
        _first_step_zero(carry)
        zv = z_ref[...]
        zs = zv + (_shift_down(zv, carry) - zv) * mu_ref[...]
        k, tw, al, sw, decay, a, kk0, nrm, inv, k2 = rw_values(zs, w0_ref, wl_ref, a0_ref, wa_ref, kk_ref, ka_ref, e_ref)
        r_o[...] = zs[:, 0:c_rw]
        w_o[...] = decay
        k_o[...] = k2
        v_o[...] = zs[:, 2 * c_rw:3 * c_rw]
        kk_o[...] = kk0 * inv
        a_o[...] = a
        g_o[...] = zs[:, 3 * c_rw:c4]

    rw_consts = [mu, w0, wl, a0, wa, k_k, k_a, e_head]
    r_s, w_s, k_s, v_s, kk_s, a_s, g_s = _rowcall(
        rw_prep, "rwkv_prep", t_dim, bt, [(z, 0, sec)], rw_consts, [(c_rw, F32)] * 7,
        scratch=[pltpu.VMEM((1, sec), F32)])
    pairs_fwd = max(n for n in (1, 2, 4) if c_rw % (n * LANES) == 0)
    pairs_bwd = pairs_fwd
    y_s, s_all, s_fin = _scan_fwd(r_s, w_s, k_s, v_s, kk_s, a_s, 128, pairs_fwd)

    def rw_post_values(y, r, k2, v, g, lng_ref, lnb_ref, rk_ref, e):
        mean = _headsum(y, e) * (1.0 / HEAD)
        d = y - mean
        rstd = lax.rsqrt(_headsum(d * d, e) * (1.0 / HEAD) + GN_EPS)
        yh = d * rstd
        rk = _headsum(r * k2 * rk_ref[...], e)
        yo = yh * lng_ref[...] + lnb_ref[...] + rk * v
        sg = _sigmoid(g)
        return rstd, yh, rk, yo, sg

    def rw_post(y_ref, r_ref, k_ref, v_ref, g_ref, lng_ref, lnb_ref, rk_ref, e_ref, out_ref):
        g = g_ref[...]
        _, _, _, yo, sg = rw_post_values(y_ref[...], r_ref[...], k_ref[...], v_ref[...], g, lng_ref, lnb_ref, rk_ref, e_ref[...])
        out_ref[...] = (yo * g * sg).astype(BF16)

    (y_rw,) = _rowcall(rw_post, "rwkv_post", t_dim, bt, [y_s, r_s, k_s, v_s, g_s], [ln_g, ln_b, r_k, e_head], [(c_rw, BF16)])

    fl_cb = (sec + 4 * c_fox) // LANES
    hp = _round_up(h_fox, SUBLANES)
    bt_c = _pick(t_dim, 256)
    tri = (jnp.arange(bt_c)[:, None] >= jnp.arange(bt_c)[None, :]).astype(F32)

    rows8 = n_pairs * SUBLANES
    pair_rows = (jnp.arange(rows8)[:, None] // SUBLANES * 2 + jnp.arange(rows8)[:, None] % SUBLANES
                 == jnp.arange(LANES)[None, :]) & (jnp.arange(rows8)[:, None] % SUBLANES < 2)
    pair_rows = pair_rows.astype(F32)

    def fox_decay(fl_ref, bf_ref, tri_ref, sel_ref, ct_ref, cq_ref, carry):
        _first_step_zero(carry)
        lf = _log_sigmoid(fl_ref[...] + bf_ref[...])
        c = jnp.dot(tri_ref[...], lf, precision=HI, preferred_element_type=F32) + carry[...]
        carry[...] = c[bt_c - 1:bt_c, :]
        ct = jnp.dot(sel_ref[...], jnp.transpose(c), precision=HI, preferred_element_type=F32)
        ct_ref[...] = ct.reshape(n_pairs, SUBLANES, bt_c)
        for pair in range(n_pairs):
            cq_ref[pair] = c[:, 2 * pair:2 * pair + 2]

    ct, cq = pl.pallas_call(
        fox_decay, name="fox_decay", grid=(t_dim // bt_c,),
        in_specs=[pl.BlockSpec((bt_c, LANES), lambda i: (i, fl_cb)), pl.BlockSpec((1, LANES), lambda i: (0, 0)),
                  pl.BlockSpec((bt_c, bt_c), lambda i: (0, 0)), pl.BlockSpec((rows8, LANES), lambda i: (0, 0))],
        out_specs=[pl.BlockSpec((n_pairs, SUBLANES, bt_c), lambda i: (0, 0, i)),
                   pl.BlockSpec((n_pairs, bt_c, 2), lambda i: (0, i, 0))],
        out_shape=[jax.ShapeDtypeStruct((n_pairs, SUBLANES, t_dim), F32), jax.ShapeDtypeStruct((n_pairs, t_dim, 2), F32)],
        scratch_shapes=[pltpu.VMEM((1, LANES), F32)], compiler_params=_params("arbitrary"),
    )(z, b_f, tri, pair_rows)
    q_cb = sec // LANES
    k_cb, v_cb = q_cb + n_pairs, q_cb + 2 * n_pairs
    o_fox, lse = _fox_fwd(z, ct, cq, q_cb, k_cb, v_cb, n_pairs, 1024)

    def fox_post(o_ref, z_ref, out_ref):
        g = z_ref[:, 3 * c_fox:4 * c_fox]
        out_ref[...] = (o_ref[...] * g * _sigmoid(g)).astype(BF16)

    (y_fox,) = _rowcall(fox_post, "fox_post", t_dim, bt, [o_fox, (z, 1, sec)], [], [(c_fox, BF16)])

    u_rw = _matmul(y_rw, wur, "nn", "up_rwkv")
    u_fox = _matmul(y_fox, wuf, "nn", "up_fox")

    def merge(ur_ref, uf_ref, z_ref, out_ref):
        s1 = _sigmoid(z_ref[:, 0:d_model])
        s2 = _sigmoid(z_ref[:, d_model:2 * d_model])
        out_ref[...] = (s1 * ur_ref[...] + s2 * uf_ref[...]).astype(BF16)

    (merged,) = _rowcall(merge, "merge", t_dim, bt, [u_rw, u_fox, (z, 2, sec)], [], [(d_model, BF16)])
    mo = _matmul(merged, wo, "nn", "proj_out")

    def resid_norm(x_ref, mo_ref, g_ref, x1_ref, n2_ref):
        x1 = x_ref[...] + mo_ref[...]
        rms = lax.rsqrt(jnp.mean(x1 * x1, axis=-1, keepdims=True) + NORM_EPS)
        x1_ref[...] = x1
        n2_ref[...] = (x1 * rms * g_ref[...]).astype(BF16)

    x1, n2 = _rowcall(resid_norm, "resid_norm", t_dim, bt, [x, mo], [g2], [(d_model, F32), (d_model, BF16)])
    ple = _matmul(p, pp, "nn", "ple_proj")
    gl = _matmul(n2, pg, "nn", "ple_gate")

    def head(x1_ref, ple_ref, gl_ref, tgt_ref, g_ref, dx2_ref, dple_ref, dgl_ref, loss_ref, dg3_ref):
        _first_step_zero(loss_ref, dg3_ref)
        sg = _sigmoid(gl_ref[...])
        pl_v = ple_ref[...]
        x2 = x1_ref[...] + pl_v * sg
        rms = lax.rsqrt(jnp.mean(x2 * x2, axis=-1, keepdims=True) + NORM_EPS)
        xn = x2 * rms
        diff = xn * g_ref[...] - tgt_ref[...]
        loss_ref[...] += 0.5 * jnp.sum(jnp.mean(diff * diff, axis=-1, keepdims=True), axis=0, keepdims=True)
        dyf = diff * inv_d
        dg3_ref[...] += _colsum(dyf * xn)
        gy = dyf * g_ref[...]
        dx2 = rms * (gy - xn * jnp.mean(xn * gy, axis=-1, keepdims=True))
        dx2_ref[...] = dx2
        dple_ref[...] = (dx2 * sg).astype(BF16)
        dgl_ref[...] = (dx2 * pl_v * sg * (1.0 - sg)).astype(BF16)

    dx2, dple, dgl, loss, d_g3 = _rowcall(
        head, "head", t_dim, bt, [x1, ple, gl, tgt], [g3], [(d_model, F32), (d_model, BF16), (d_model, BF16)],
        acc_outs=[(1, 1), (1, d_model)])

    d_pp = _matmul(p, dple, "tn", "d_ple_proj", out_dtype=BF16)
    d_pg = _matmul(n2, dgl, "tn", "d_ple_gate", out_dtype=BF16)
    dn2 = _matmul(dgl, pg, "nt", "d_n2")

    def resid_norm_bwd(dx2_ref, dn2_ref, x1_ref, g_ref, dx1_ref, dx1b_ref, dg2_ref):
        _first_step_zero(dg2_ref)
        x1 = x1_ref[...]
        rms = lax.rsqrt(jnp.mean(x1 * x1, axis=-1, keepdims=True) + NORM_EPS)
        xn = x1 * rms
        dn = dn2_ref[...]
        dg2_ref[...] += _colsum(dn * xn)
        gy = dn * g_ref[...]
        dx1 = dx2_ref[...] + rms * (gy - xn * jnp.mean(xn * gy, axis=-1, keepdims=True))
        dx1_ref[...] = dx1
        dx1b_ref[...] = dx1.astype(BF16)

    dx1, dx1b, d_g2 = _rowcall(resid_norm_bwd, "resid_norm_bwd", t_dim, bt, [dx2, dn2, x1], [g2],
                               [(d_model, F32), (d_model, BF16)], acc_outs=[(1, d_model)])
    d_wo = _matmul(merged, dx1b, "tn", "d_w_out", out_dtype=BF16)
    dmerged = _matmul(dx1b, wo, "nt", "d_merged")

    def merge_bwd(dm_ref, ur_ref, uf_ref, z_ref, dur_ref, duf_ref, dzg_ref):
        dm = dm_ref[...]
        s1 = _sigmoid(z_ref[:, 0:d_model])
        s2 = _sigmoid(z_ref[:, d_model:2 * d_model])
        dur_ref[...] = (dm * s1).astype(BF16)
        duf_ref[...] = (dm * s2).astype(BF16)
        dzg_ref[:, 0:d_model] = (dm * ur_ref[...] * s1 * (1.0 - s1)).astype(BF16)
        dzg_ref[:, d_model:2 * d_model] = (dm * uf_ref[...] * s2 * (1.0 - s2)).astype(BF16)
        if sec > 2 * d_model:
            dzg_ref[:, 2 * d_model:sec] = jnp.zeros((dm.shape[0], sec - 2 * d_model), BF16)

    du_rw, du_fox, dz_gate = _rowcall(merge_bwd, "merge_bwd", t_dim, bt, [dmerged, u_rw, u_fox, (z, 2, sec)], [],
                                      [(d_model, BF16), (d_model, BF16), (sec, BF16)])
    d_wur = _matmul(y_rw, du_rw, "tn", "d_w_up_rwkv", out_dtype=BF16)
    d_wuf = _matmul(y_fox, du_fox, "tn", "d_w_up_fox", out_dtype=BF16)
    dy_rw = _matmul(du_rw, wur, "nt", "d_y_rwkv")
    dy_fox = _matmul(du_fox, wuf, "nt", "d_y_fox")

    def fox_post_bwd(dy_ref, o_ref, z_ref, e_ref, do_ref, dg_ref, rd_ref):
        g = z_ref[:, 3 * c_fox:4 * c_fox]
        sg = _sigmoid(g)
        dy, o = dy_ref[...], o_ref[...]
        do = dy * g * sg
        do_ref[...] = do
        dg_ref[...] = (dy * o * sg * (1.0 + g * (1.0 - sg))).astype(BF16)
        rd_ref[...] = _headsum(do.astype(BF16).astype(F32) * o, e_ref[...])

    do_fox, dg_fox, rowdot = _rowcall(fox_post_bwd, "fox_post_bwd", t_dim, bt, [dy_fox, o_fox, (z, 1, sec)], [e_head],
                                      [(c_fox, F32), (c_fox, BF16), (c_fox, F32)])
    dq_f, dk_f, dv_f, dc_t = _fox_bwd(z, ct, cq, rowdot, do_fox, lse, q_cb, k_cb, v_cb, n_pairs, 512)
    sel = (jnp.arange(hp)[:, None] // 2 * SUBLANES + jnp.arange(hp)[:, None] % 2 == jnp.arange(rows8)[None, :]).astype(F32)
    tri_rev = (jnp.arange(bt_c)[:, None] >= jnp.arange(bt_c)[None, :]).astype(F32)
    bf_col = b_f.reshape(LANES, 1)[0:hp]
    nbc = t_dim // bt_c

    def fox_decay_bwd(dc_ref, fl_ref, sel_ref, tri_ref, bf_ref, dfl_ref, dbf_ref, carry):
        _first_step_zero(carry, dbf_ref)
        dc = jnp.dot(sel_ref[...], dc_ref[...].reshape(rows8, bt_c), precision=HI, preferred_element_type=F32)
        dlf = jnp.dot(dc, tri_ref[...], precision=HI, preferred_element_type=F32) + carry[...]
        carry[...] = dlf[:, 0:1]
        flt = jnp.transpose(fl_ref[...])[0:hp, :]
        dfl = dlf * _sigmoid(-(flt + bf_ref[...]))
        head_row = lax.broadcasted_iota(jnp.int32, (hp, bt_c), 0)
        dfl = jnp.where(head_row < h_fox, dfl, 0.0)
        dbf_ref[...] += jnp.sum(dfl, axis=1, keepdims=True)
        full = jnp.concatenate([dfl, jnp.zeros((LANES - hp, bt_c), F32)], axis=0) if hp < LANES else dfl
        dfl_ref[...] = jnp.transpose(full).astype(BF16)

    dz_fl, d_bf = pl.pallas_call(
        fox_decay_bwd, name="fox_decay_bwd", grid=(nbc,),
        in_specs=[pl.BlockSpec((n_pairs, SUBLANES, bt_c), lambda i: (0, 0, nbc - 1 - i)),
                  pl.BlockSpec((bt_c, LANES), lambda i: (nbc - 1 - i, fl_cb)),
                  pl.BlockSpec(sel.shape, lambda i: (0, 0)), pl.BlockSpec((bt_c, bt_c), lambda i: (0, 0)),
                  pl.BlockSpec((hp, 1), lambda i: (0, 0))],
        out_specs=[pl.BlockSpec((bt_c, LANES), lambda i: (nbc - 1 - i, 0)), pl.BlockSpec((hp, 1), lambda i: (0, 0))],
        out_shape=[jax.ShapeDtypeStruct((t_dim, LANES), BF16), jax.ShapeDtypeStruct((hp, 1), F32)],
        scratch_shapes=[pltpu.VMEM((hp, 1), F32)], compiler_params=_params("arbitrary"),
    )(dc_t, z, sel, tri_rev, bf_col)

    def rw_post_bwd(dy_ref, y_ref, r_ref, k_ref, v_ref, g_ref, lng_ref, lnb_ref, rk_ref, e_ref,
                    dg_ref, dys_ref, dr_ref, dk_ref, dv_ref, dlng_ref, dlnb_ref, drk_ref):
        _first_step_zero(dlng_ref, dlnb_ref, drk_ref)
        e = e_ref[...]
        dy, r, k2, v, g = dy_ref[...], r_ref[...], k_ref[...], v_ref[...], g_ref[...]
        rstd, yh, rk, yo, sg = rw_post_values(y_ref[...], r, k2, v, g, lng_ref, lnb_ref, rk_ref, e)
        dg_ref[...] = dy * yo * sg * (1.0 + g * (1.0 - sg))
        dyo = dy * g * sg
        dlnb_ref[...] += _colsum(dyo)
        dlng_ref[...] += _colsum(dyo * yh)
        dyh = dyo * lng_ref[...]
        dys_ref[...] = rstd * (dyh - _headsum(dyh, e) * (1.0 / HEAD) - yh * _headsum(dyh * yh, e) * (1.0 / HEAD))
        drk = _headsum(dyo * v, e)
        dv_ref[...] = dyo * rk
        dr_ref[...] = drk * k2 * rk_ref[...]
        dk_ref[...] = drk * r * rk_ref[...]
        drk_ref[...] += _colsum(drk * r * k2)

    dg_rw, dy_s, dr_b, dk_b, dv_b, d_lng, d_lnb, d_rk = _rowcall(
        rw_post_bwd, "rwkv_post_bwd", t_dim, bt, [dy_rw, y_s, r_s, k_s, v_s, g_s], [ln_g, ln_b, r_k, e_head],
        [(c_rw, F32)] * 5, acc_outs=[(1, c_rw)] * 3)
    axis = dict(SHARDED)
    early = {"w_up_rwkv": d_wur, "w_up_fox": d_wuf, "w_out": d_wo, "ple_proj": d_pp, "ple_gate_w": d_pg}
    (dr_c, dw_c, dk_c, dv_c, dkk_c, da_c), early_recv = _scan_bwd(
        r_s, w_s, k_s, v_s, kk_s, a_s, dy_s, s_all, s_fin, 128, pairs_bwd,
        _Exchange([_split_shards(g, axis[n]) for n, g in early.items()]))
    recv = dict(zip(early, early_recv))

    def rw_prep_bwd(z_ref, dr1, dr2, dw_ref, dk1, dk2_ref, dv1, dv2, dkk_ref, da_ref, dg_ref,
                    mu_ref, w0_ref, wl_ref, a0_ref, wa_ref, kk_ref, ka_ref, e_ref,
                    dzs_ref, tw_ref, al_ref, dwr_ref, dar_ref, dmu_ref, dw0_ref, da0_ref, dkk_acc, dka_acc, carry):
        _first_step_zero(carry, dmu_ref, dw0_ref, da0_ref, dkk_acc, dka_acc)
        e = e_ref[...]
        zv = z_ref[...]
        zp = _shift_down(zv, carry)
        zs = zv + (zp - zv) * mu_ref[...]
        k, tw, al, sw, decay, a, kk0, nrm, inv, k2 = rw_values(zs, w0_ref, wl_ref, a0_ref, wa_ref, kk_ref, ka_ref, e_ref)
        dk2 = dk1[...] + dk2_ref[...]
        da = da_ref[...] + dk2 * k * ka_ref[...]
        dk = dk2 * (1.0 + (a - 1.0) * ka_ref[...])
        dka_acc[...] += _colsum(dk2 * k * (a - 1.0))
        kk = kk0 * inv
        dkk = dkk_ref[...]
        dkk0 = inv * jnp.where(nrm > 1e-12, dkk - kk * _headsum(dkk * kk, e), dkk)
        dk = dk + dkk0 * kk_ref[...]
        dkk_acc[...] += _colsum(dkk0 * k)
        da_raw = da * a * (1.0 - a)
        da0_ref[...] += _colsum(da_raw)
        dw_raw = dw_ref[...] * decay * (-decay_k) * sw * (1.0 - sw)
        dw0_ref[...] += _colsum(dw_raw)
        dar_b, dwr_b = da_raw.astype(BF16), dw_raw.astype(BF16)
        dal = lax.dot_general(dar_b, wa_ref[...], _DOT_DIMS["nt"], preferred_element_type=F32)
        dtw = lax.dot_general(dwr_b, wl_ref[...], _DOT_DIMS["nt"], preferred_element_type=F32)
        dzs_ref[:, 0:c_rw] = dr1[...] + dr2[...]
        dzs_ref[:, c_rw:2 * c_rw] = dk
        dzs_ref[:, 2 * c_rw:3 * c_rw] = dv1[...] + dv2[...]
        dzs_ref[:, 3 * c_rw:c4] = dg_ref[...]
        dzs_ref[:, c4:c4 + lora] = dtw * (1.0 - tw * tw)
        dzs_ref[:, c4 + lora:c4 + 2 * lora] = dal
        if sec > c4 + 2 * lora:
            dzs_ref[:, c4 + 2 * lora:sec] = jnp.zeros((zv.shape[0], sec - c4 - 2 * lora), F32)
        tw_ref[...] = tw.astype(BF16)
        al_ref[...] = al.astype(BF16)
        dwr_ref[...] = dwr_b
        dar_ref[...] = dar_b
        dmu_ref[...] += _colsum(dzs_ref[...] * (zp - zv))

    dzs, tw_b, al_b, dwr_b, dar_b, d_mu, d_w0, d_a0, d_kk, d_ka = _rowcall(
        rw_prep_bwd, "rwkv_prep_bwd", t_dim, bt_many,
        [(z, 0, sec), dr_c, dr_b, dw_c, dk_c, dk_b, dv_c, dv_b, dkk_c, da_c, dg_rw], rw_consts,
        [(sec, F32), (lora, BF16), (lora, BF16), (c_rw, BF16), (c_rw, BF16)],
        acc_outs=[(1, sec), (1, c_rw), (1, c_rw), (1, c_rw), (1, c_rw)], scratch=[pltpu.VMEM((1, sec), F32)])
    d_wl = _matmul(tw_b, dwr_b, "tn", "d_w_lora", out_dtype=BF16)
    d_wa = _matmul(al_b, dar_b, "tn", "d_a_lora", out_dtype=BF16)

    def shift_bwd(dzs_ref, mu_ref, dz_ref, carry):
        _first_step_zero(carry)
        d = dzs_ref[...]
        nbt = d.shape[0]
        nxt = pltpu.roll(d, nbt - 1, 0)
        rowi = lax.broadcasted_iota(jnp.int32, d.shape, 0)
        nxt = jnp.where(rowi == nbt - 1, carry[...], nxt)
        carry[...] = d[0:1, :]
        m = mu_ref[...]
        dz_ref[...] = (d * (1.0 - m) + nxt * m).astype(BF16)

    (dz_rw,) = _rowcall(shift_bwd, "shift_bwd", t_dim, bt, [dzs], [mu], [(sec, BF16)],
                        scratch=[pltpu.VMEM((1, sec), F32)], reverse=True)

    fox_parts = [dq_f.astype(BF16), dk_f.astype(BF16), dv_f.astype(BF16), dg_fox, dz_fl]
    if sec > 4 * c_fox + LANES:
        fox_parts.append(jnp.zeros((t_dim, sec - 4 * c_fox - LANES), BF16))
    dz = jnp.concatenate([dz_rw] + fox_parts + [dz_gate], axis=1)
    d_wz = _matmul(h, dz, "tn", "d_w_in", out_dtype=BF16, bn=1408)
    rw_cols, fox_cols = c4 + 2 * lora, 4 * c_fox + h_fox
    d_wi = jnp.concatenate([d_wz[:, :rw_cols], d_wz[:, sec:sec + fox_cols], d_wz[:, 2 * sec:2 * sec + 2 * d_model]], axis=1)
    parts = _split_shards(d_wi, 1)
    mine = lax.dynamic_index_in_dim(parts.reshape((NCHIP, 2) + parts.shape[1:]), lax.axis_index("c"), 1, keepdims=False)
    chip_parts = _add_pairs(mine, _sibling_swap(parts, "swap_w_in"), "add_w_in")
    dh, (recv["w_in"],) = _matmul(dz, wz, "nt", "d_h", bk=sec, exchange=_ChipExchange([chip_parts]))
    late_parts = [_split_shards(d_wl, 1), _split_shards(d_wa, 1)]

    def norm_in_bwd(dh_ref, x_ref, dx1_ref, g_ref, dx_ref, dg1_ref):
        _first_step_zero(dg1_ref)
        xv = x_ref[...]
        rms = lax.rsqrt(jnp.mean(xv * xv, axis=-1, keepdims=True) + NORM_EPS)
        xn = xv * rms
        d = dh_ref[...]
        dg1_ref[...] += _colsum(d * xn)
        gy = d * g_ref[...]
        dx_ref[...] = dx1_ref[...] + rms * (gy - xn * jnp.mean(xn * gy, axis=-1, keepdims=True))

    dx, d_g1 = _rowcall(norm_in_bwd, "norm_in_bwd", t_dim, bt, [dh, x, dx1], [norm_g], [(d_model, F32)],
                        acc_outs=[(1, d_model)])

    rep_grads = {"norm_g": d_g1, "rw_shift_mu": d_mu[:, 0:c4 + 2 * lora], "rw_w0": d_w0, "rw_a0": d_a0, "rw_k_k": d_kk,
                 "rw_k_a": d_ka, "rw_r_k": d_rk, "rw_ln_g": d_lng, "rw_ln_b": d_lnb, "fox_b_f": d_bf[0:h_fox, 0],
                 "ple_norm_g": d_g2, "final_norm_g": d_g3}
    return loss[0, 0], dx, recv, late_parts, rep_grads


def kernel(x, p, norm_g, w_in, rw_shift_mu, rw_w0, rw_w_lora_up, rw_a0, rw_a_lora_up, rw_k_k, rw_k_a, rw_r_k, rw_ln_g, rw_ln_b, fox_b_f, w_up_rwkv, w_up_fox, w_out, ple_proj, ple_gate_w, ple_norm_g, final_norm_g, loss_target, m_norm_g, m_w_in, m_rw_shift_mu, m_rw_w0, m_rw_w_lora_up, m_rw_a0, m_rw_a_lora_up, m_rw_k_k, m_rw_k_a, m_rw_r_k, m_rw_ln_g, m_rw_ln_b, m_fox_b_f, m_w_up_rwkv, m_w_up_fox, m_w_out, m_ple_proj, m_ple_gate_w, m_ple_norm_g, m_final_norm_g, v_norm_g, v_w_in, v_rw_shift_mu, v_rw_w0, v_rw_w_lora_up, v_rw_a0, v_rw_a_lora_up, v_rw_k_k, v_rw_k_a, v_rw_r_k, v_rw_ln_g, v_rw_ln_b, v_fox_b_f, v_w_up_rwkv, v_w_up_fox, v_w_out, v_ple_proj, v_ple_gate_w, v_ple_norm_g, v_final_norm_g):
    args = locals()
    w = {n: args[n] for n in WEIGHTS}
    mom = {n: args["m_" + n] for n in WEIGHTS}
    var = {n: args["v_" + n] for n in WEIGHTS}

    t_dim, d_model = x.shape[1], x.shape[2]
    c_rw, lora = rw_w0.shape[1], rw_w_lora_up.shape[1]
    h_fox = fox_b_f.shape[1]
    c_fox = h_fox * HEAD
    rw_cols, fox_cols, gate_cols = 4 * c_rw + 2 * lora, 4 * c_fox + h_fox, 2 * d_model
    sec = max(rw_cols, 4 * c_fox + LANES, _round_up(gate_cols, LANES))
    assert c_rw % LANES == 0 and c_fox % LANES == 0 and rw_cols % LANES == 0 and h_fox <= LANES and d_model % LANES == 0
    assert rw_a_lora_up.shape[1] == lora and w_in.shape[2] * NDEV == rw_cols + fox_cols + gate_cols

    assert SHARDED[0] == ("w_in", 1)
    (w_in_all,) = _all_gather([w_in[0].astype(BF16)], "gather_w_in")
    other_shards = [w[n][0].astype(BF16) for n, _ in SHARDED[1:]]

    def to_sections(wi):
        pad = lambda a, width: jnp.pad(a, ((0, 0), (0, width - a.shape[1])))
        return jnp.concatenate([pad(wi[:, :rw_cols], sec), pad(wi[:, rw_cols:rw_cols + fox_cols], sec),
                                pad(wi[:, rw_cols + fox_cols:], sec)], axis=1)

    rep = {n: w[n] for n in REPLICATED}
    dims = (t_dim, d_model, c_rw, lora, c_fox, h_fox, sec)
    loss_local, grad_x, recv, late_parts, rep_grads = _local_step(
        x[0], p[0, 0], loss_target[0], to_sections(_join_shards(w_in_all, 1)), other_shards, rep, dims)

    rep_sizes = [w[n].size for n in REPLICATED]
    rep_offs = [sum(rep_sizes[:i]) for i in range(len(rep_sizes))]
    pack_rep = lambda tree: _pack_rows(jnp.concatenate([tree[n].astype(F32).reshape(-1) for n in REPLICATED]))
    small_all, recv["rw_w_lora_up"], recv["rw_a_lora_up"] = _grad_exchange([pack_rep(rep_grads)] + late_parts,
                                                                           "exchange_small")
    kinds = ("grad", "delta", "new_m", "new_v")
    outs = {}
    for n, _ in SHARDED:
        for kind, buf in zip(kinds, _adamw(recv[n], w[n][0], mom[n][0], var[n][0], "adamw_" + n)):
            outs[kind, n] = buf
    for kind, buf in zip(kinds, _adamw(small_all, pack_rep(w), pack_rep(mom), pack_rep(var), "adamw_replicated")):
        flat = buf.reshape(-1)
        for n, o, s in zip(REPLICATED, rep_offs, rep_sizes):
            outs[kind, n] = flat[o:o + s].reshape(w[n].shape)
    loss = lax.psum(loss_local, MESH_AXES)
    return (loss, grad_x[None], *[outs[kind, n] for kind in ("grad", "delta", "new_m", "new_v") for n in WEIGHTS])
```

```python
import functools
import math

import jax
import jax.numpy as jnp
from jax import lax
from jax.experimental import pallas as pl
from jax.experimental.pallas import tpu as pltpu

F32, BF16 = jnp.float32, jnp.bfloat16
HI = lax.Precision.HIGHEST
LANES = 128
SUBLANES = 8
HEAD = 64
NORM_EPS = 1e-6
GN_EPS = 64e-5
VMEM_LIMIT = 56 * 1024 * 1024
NDEV = 8
PACK_W = 1024
PACK_ALIGN = 16 * PACK_W
MESH_AXES = ("x", "y", "c")
MESH = pl.DeviceIdType.MESH

ADAM_LR, ADAM_B1, ADAM_B2, ADAM_EPS, ADAM_WD, ADAM_STEP = 0.001, 0.9, 0.999, 1e-08, 0.01, 10


def _round_up(n, m):
    return (n + m - 1) // m * m


def _pick(dim, pref, align=LANES):
    if dim <= pref:
        return dim
    best = None
    for cand in range(align, pref + 1, align):
        if dim % cand == 0:
            best = cand
    return dim if best is None else best


def _params(*sem):
    return pltpu.CompilerParams(dimension_semantics=sem, vmem_limit_bytes=VMEM_LIMIT)


def _sigmoid(v):
    return jax.nn.sigmoid(v)


def _log_sigmoid(v):
    return jnp.minimum(v, 0.0) - jnp.log(1.0 + jnp.exp(-jnp.abs(v)))


_DOT_DIMS = {"nn": (((1,), (0,)), ((), ())), "nt": (((1,), (1,)), ((), ())), "tn": (((0,), (0,)), ((), ()))}


def _matmul(a, b, mode, name, out_dtype=F32, bm=512, bn=1024, bk=2048, exchange=None):
    if mode == "tn":
        k_dim, m_dim = a.shape
    else:
        m_dim, k_dim = a.shape
    n_dim = b.shape[0] if mode == "nt" else b.shape[1]
    bm, bn, bk = _pick(m_dim, bm), _pick(n_dim, bn), _pick(k_dim, bk)
    nk = k_dim // bk

    nx = 0 if exchange is None else exchange.n
    grid = (m_dim // bm, n_dim // bn, nk)

    def body(*refs):
        a_ref, b_ref, o_ref = refs[0], refs[1], refs[2 + nx]
        step = [pl.program_id(d) for d in range(3)]
        if nx:
            own_scratch = 1 if nk > 1 else 0
            ex_refs = (refs[2:2 + nx], refs[3 + nx:3 + 2 * nx], refs[3 + 2 * nx + own_scratch:])
            linear = (step[0] * grid[1] + step[1]) * grid[2] + step[2]

            @pl.when(linear == 0)
            def _():
                exchange.start(*ex_refs)

            if hasattr(exchange, "forward"):
                @pl.when(linear == (grid[0] * grid[1] * grid[2]) // 2)
                def _():
                    exchange.forward(*ex_refs)

        prod = lax.dot_general(a_ref[...].astype(BF16), b_ref[...].astype(BF16), _DOT_DIMS[mode],
                               preferred_element_type=F32)
        if nk == 1:
            o_ref[...] = prod.astype(o_ref.dtype)
        else:
            acc_ref, k = refs[3 + 2 * nx], step[2]

            @pl.when(k == 0)
            def _():
                acc_ref[...] = prod

            @pl.when(k > 0)
            def _():
                acc_ref[...] += prod

            @pl.when(k == nk - 1)
            def _():
                o_ref[...] = acc_ref[...].astype(o_ref.dtype)
        if nx:
            @pl.when(linear == grid[0] * grid[1] * grid[2] - 1)
            def _():
                exchange.finish(*ex_refs)

    if mode == "tn":
        a_spec = pl.BlockSpec((bk, bm), lambda i, j, k: (k, i))
    else:
        a_spec = pl.BlockSpec((bm, bk), lambda i, j, k: (i, k))
    if mode == "nt":
        b_spec = pl.BlockSpec((bn, bk), lambda i, j, k: (j, k))
    else:
        b_spec = pl.BlockSpec((bk, bn), lambda i, j, k: (k, j))
    out_spec = pl.BlockSpec((bm, bn), lambda i, j, k: (i, j))
    out_shape = jax.ShapeDtypeStruct((m_dim, n_dim), out_dtype)
    acc_scratch = [pltpu.VMEM((bm, bn), F32)] if nk > 1 else []
    if exchange is None:
        return pl.pallas_call(
            body, name=name, grid=grid, in_specs=[a_spec, b_spec], out_specs=out_spec, out_shape=out_shape,
            scratch_shapes=acc_scratch, compiler_params=_params("parallel", "parallel", "arbitrary"),
        )(a, b)
    outs = pl.pallas_call(
        body, name=name, grid=grid, in_specs=[a_spec, b_spec] + [HBM_SPEC] * nx,
        out_specs=[out_spec] + [HBM_SPEC] * nx, out_shape=[out_shape] + exchange.out_shape,
        scratch_shapes=acc_scratch + exchange.scratch, compiler_params=_params("arbitrary", "arbitrary", "arbitrary"),
    )(a, b, *exchange.arrays)
    return outs[0], outs[1:]


def _rowcall(body, name, t_dim, bt, row_ins, const_ins, row_outs, acc_outs=(), scratch=(), reverse=False):
    nt = t_dim // bt

    def rmap(i):
        return nt - 1 - i if reverse else i

    in_specs, args = [], []
    for item in row_ins:
        arr, cb, w = item if isinstance(item, tuple) else (item, 0, item.shape[1])
        in_specs.append(pl.BlockSpec((bt, w), lambda i, cb=cb: (rmap(i), cb)))
        args.append(arr)
    for arr in const_ins:
        in_specs.append(pl.BlockSpec(arr.shape, lambda i, nd=arr.ndim: (0,) * nd))
        args.append(arr)
    out_specs = [pl.BlockSpec((bt, w), lambda i: (rmap(i), 0)) for w, _ in row_outs]
    out_shape = [jax.ShapeDtypeStruct((t_dim, w), dt) for w, dt in row_outs]
    for shp in acc_outs:
        out_specs.append(pl.BlockSpec(shp, lambda i, nd=len(shp): (0,) * nd))
        out_shape.append(jax.ShapeDtypeStruct(shp, F32))
    return pl.pallas_call(
        body, name=name, grid=(nt,), in_specs=in_specs, out_specs=out_specs, out_shape=out_shape,
        scratch_shapes=list(scratch), compiler_params=_params("arbitrary"),
    )(*args)


def _first_step_zero(*refs):
    @pl.when(pl.program_id(0) == 0)
    def _():
        for r in refs:
            r[...] = jnp.zeros_like(r)


def _colsum(v):
    return jnp.sum(v, axis=0, keepdims=True)


def _headsum(v, e):
    parts = [jnp.dot(v[:, p * LANES:(p + 1) * LANES], e, precision=HI, preferred_element_type=F32)
             for p in range(v.shape[1] // LANES)]
    return parts[0] if len(parts) == 1 else jnp.concatenate(parts, axis=1)


def _shift_down(v, carry_ref):
    bt = v.shape[0]
    prev = pltpu.roll(v, 1, 0)
    row = lax.broadcasted_iota(jnp.int32, v.shape, 0)
    prev = jnp.where(row == 0, carry_ref[...], prev)
    carry_ref[...] = v[bt - 1:bt, :]
    return prev


def _pair_consts():
    lane = lax.broadcasted_iota(jnp.int32, (1, LANES), 1)
    m0 = (lane < HEAD).astype(F32)
    m1 = 1.0 - m0
    sub = lax.broadcasted_iota(jnp.int32, (HEAD, LANES), 0)
    lane2 = lax.broadcasted_iota(jnp.int32, (HEAD, LANES), 1)
    i0 = (lane2 == sub).astype(F32)
    i1 = (lane2 == sub + HEAD).astype(F32)
    return m0, m1, i0, i1


def _head_masks():
    lane = lax.broadcasted_iota(jnp.int32, (1, LANES), 1)
    first = (lane < HEAD).astype(F32)
    return first, 1.0 - first


def _head_ones(dtype=BF16):
    lane = jnp.arange(LANES)
    return (lane[:, None] // HEAD == lane[None, :] // HEAD).astype(dtype)


def _lanesum(v):
    return jnp.sum(v, axis=1, keepdims=True)


def _split_bf16(v):
    hi = v.astype(BF16).astype(F32)
    rest = v - hi
    mid = rest.astype(BF16).astype(F32)
    return hi, mid, (rest - mid).astype(BF16).astype(F32)


def _col_pair(parts, j, i01, e_bf16):
    lhs = jnp.concatenate([(part[j:j + 1] * i01).astype(BF16) for part in parts], axis=0)
    out = jnp.dot(lhs, e_bf16, preferred_element_type=F32)
    return (out[0:HEAD] + out[HEAD:2 * HEAD]) + out[2 * HEAD:3 * HEAD]


def _row_pair(c0, c1, i0, i1):
    return _colsum(c0 * i0 + c1 * i1)


def _scan_fwd(r, w, k, v, kk, a, tc, npb):
    t_dim, c_dim = r.shape
    wb = LANES * npb
    tc = _pick(t_dim, tc, SUBLANES)

    def body(r_ref, w_ref, k_ref, v_ref, kk_ref, a_ref, e_ref, y_ref, sall_ref, sfin_ref, s_ref):
        @pl.when(pl.program_id(1) == 0)
        def _():
            s_ref[...] = jnp.zeros_like(s_ref)

        m0, m1, i0, i1 = _pair_consts()
        i01 = i0 + i1
        e = e_ref[...]
        sub8 = lax.broadcasted_iota(jnp.int32, (SUBLANES, LANES), 0)
        lanes = [slice(q * LANES, (q + 1) * LANES) for q in range(npb)]
        ng = tc // SUBLANES

        def lanesums(tiles):
            sums = _lanesum(jnp.concatenate(tiles, axis=0))
            return [sums[i * HEAD:(i + 1) * HEAD] for i in range(len(tiles))]

        def halves(s, row):
            return [s * (row * m0), s * (row * m1)]

        def group(gi, carry):
            base = pl.multiple_of(gi * SUBLANES, SUBLANES)
            rows = pl.ds(base, SUBLANES)
            s = list(carry)
            r8 = [r_ref[rows, sl] for sl in lanes]
            w8 = [w_ref[rows, sl] for sl in lanes]
            k8 = [k_ref[rows, sl] for sl in lanes]
            kk8 = [kk_ref[rows, sl] for sl in lanes]
            b8 = [kk8[q] * a_ref[rows, lanes[q]] for q in range(npb)]
            v8 = [_split_bf16(v_ref[rows, sl]) for sl in lanes]
            y8 = [jnp.zeros((SUBLANES, LANES), F32)] * npb
            for j in range(SUBLANES + 1):
                one, before = slice(j, j + 1), slice(j - 1, j)
                tiles = []
                for q in range(npb):
                    if j < SUBLANES:
                        tiles += halves(s[q], kk8[q][one])
                    if j > 0:
                        tiles += halves(s[q], r8[q][before])
                cols = lanesums(tiles)
                per = len(tiles) // npb
                for q in range(npb):
                    mine = cols[q * per:(q + 1) * per]
                    if j > 0:
                        y8[q] = jnp.where(sub8 == j - 1, _row_pair(mine[-2], mine[-1], i0, i1), y8[q])
                    if j < SUBLANES:
                        sall_ref[base + j, :, lanes[q]] = s[q]
                        sb = mine[0] * m0 + mine[1] * m1
                        s[q] = s[q] * w8[q][one] - sb * b8[q][one] + _col_pair(v8[q], j, i01, e) * k8[q][one]
            for q in range(npb):
                y_ref[rows, lanes[q]] = y8[q]
            return tuple(s)

        init = tuple(s_ref[:, q * LANES:(q + 1) * LANES] for q in range(npb))
        fin = lax.fori_loop(0, ng, group, init)
        for q in range(npb):
            s_ref[:, q * LANES:(q + 1) * LANES] = fin[q]
            sfin_ref[:, q * LANES:(q + 1) * LANES] = fin[q]

    row = pl.BlockSpec((tc, wb), lambda p, c: (c, p))
    return pl.pallas_call(
        body, name="rwkv_scan_fwd", grid=(c_dim // wb, t_dim // tc),
        in_specs=[row] * 6 + [pl.BlockSpec((LANES, LANES), lambda p, c: (0, 0))],
        out_specs=[row, pl.BlockSpec((tc, HEAD, wb), lambda p, c: (c, 0, p)), pl.BlockSpec((HEAD, wb), lambda p, c: (0, p))],
        out_shape=[jax.ShapeDtypeStruct((t_dim, c_dim), F32), jax.ShapeDtypeStruct((t_dim, HEAD, c_dim), F32),
                   jax.ShapeDtypeStruct((HEAD, c_dim), F32)],
        scratch_shapes=[pltpu.VMEM((HEAD, wb), F32)],
        compiler_params=_params("parallel", "arbitrary"),
    )(r, w, k, v, kk, a, _head_ones())


def _scan_bwd(r, w, k, v, kk, a, dy, sall, sfin, tc, npb, exchange):
    t_dim, c_dim = r.shape
    wb = LANES * npb
    tc = _pick(t_dim, tc, SUBLANES)
    nc = t_dim // tc
    nx = exchange.n
    n_blocks = c_dim // wb

    def body(*refs):
        r_ref, w_ref, k_ref, v_ref, kk_ref, a_ref, dy_ref, sall_ref, sfin_ref, e_ref = refs[:10]
        dr_ref, dw_ref, dk_ref, dv_ref, dkk_ref, da_ref = refs[10 + nx:16 + nx]
        ds_ref, sn_ref = refs[16 + 2 * nx:18 + 2 * nx]
        here_p, here_c = pl.program_id(0), pl.program_id(1)
        ex_refs = (refs[10:10 + nx], refs[16 + nx:16 + 2 * nx], refs[18 + 2 * nx:])

        @pl.when((here_p == 0) & (here_c == 0))
        def _():
            exchange.start(*ex_refs)

        @pl.when(pl.program_id(1) == 0)
        def _():
            ds_ref[...] = jnp.zeros_like(ds_ref)
            sn_ref[...] = sfin_ref[...]

        m0, m1, i0, i1 = _pair_consts()
        i01 = i0 + i1
        e = e_ref[...]
        sub8 = lax.broadcasted_iota(jnp.int32, (SUBLANES, LANES), 0)
        lanes = [slice(q * LANES, (q + 1) * LANES) for q in range(npb)]
        ng = tc // SUBLANES

        def halves(s, row):
            return [s * (row * m0), s * (row * m1)]

        def group(gi, carry):
            base = pl.multiple_of((ng - 1 - gi) * SUBLANES, SUBLANES)
            rows = pl.ds(base, SUBLANES)
            ds = list(carry)
            r8 = [r_ref[rows, sl] for sl in lanes]
            w8 = [w_ref[rows, sl] for sl in lanes]
            k8 = [k_ref[rows, sl] for sl in lanes]
            kk8 = [kk_ref[rows, sl] for sl in lanes]
            a8 = [a_ref[rows, sl] for sl in lanes]
            v8 = [_split_bf16(v_ref[rows, sl]) for sl in lanes]
            dy8 = [_split_bf16(dy_ref[rows, sl]) for sl in lanes]
            zero8 = jnp.zeros((SUBLANES, LANES), F32)
            dr8, dw8, dk8, dv8, dkk8, da8 = ([zero8] * npb for _ in range(6))
            for j in reversed(range(SUBLANES)):
                one = slice(j, j + 1)
                here = sub8 == j
                d, s_prev, tiles = [], [], []
                for q in range(npb):
                    dyb = _col_pair(dy8[q], j, i01, e)
                    s_prev.append(sall_ref[base + j, :, lanes[q]])
                    dr8[q] = jnp.where(here, _colsum(sn_ref[:, lanes[q]] * dyb), dr8[q])
                    sn_ref[:, lanes[q]] = s_prev[q]
                    d.append(ds[q] + dyb * r8[q][one])
                for q in range(npb):
                    tiles += halves(d[q], kk8[q][one] * a8[q][one])
                for q in range(npb):
                    tiles += halves(d[q], k8[q][one]) + halves(s_prev[q], kk8[q][one])
                sums = _lanesum(jnp.concatenate(tiles, axis=0))
                cols = [sums[i * HEAD:(i + 1) * HEAD] for i in range(len(tiles))]
                for q in range(npb):
                    kkr, ar = kk8[q][one], a8[q][one]
                    dsb = -(cols[2 * q] * m0 + cols[2 * q + 1] * m1)
                    ds[q] = d[q] * w8[q][one] + dsb * kkr
                    rest = cols[2 * npb + 4 * q:2 * npb + 4 * q + 4]
                    dv8[q] = jnp.where(here, _row_pair(rest[0], rest[1], i0, i1), dv8[q])
                    sb = rest[2] * m0 + rest[3] * m1
                    db = -_colsum(d[q] * sb)
                    dk8[q] = jnp.where(here, _colsum(d[q] * _col_pair(v8[q], j, i01, e)), dk8[q])
                    dw8[q] = jnp.where(here, _colsum(d[q] * s_prev[q]), dw8[q])
                    dkk8[q] = jnp.where(here, _colsum(s_prev[q] * dsb) + db * ar, dkk8[q])
                    da8[q] = jnp.where(here, db * kkr, da8[q])
            for q in range(npb):
                sl = lanes[q]
                dr_ref[rows, sl], dw_ref[rows, sl], dk_ref[rows, sl] = dr8[q], dw8[q], dk8[q]
                dv_ref[rows, sl], dkk_ref[rows, sl], da_ref[rows, sl] = dv8[q], dkk8[q], da8[q]
            return tuple(ds)

        init = tuple(ds_ref[:, q * LANES:(q + 1) * LANES] for q in range(npb))
        fin = lax.fori_loop(0, ng, group, init)
        for q in range(npb):
            ds_ref[:, q * LANES:(q + 1) * LANES] = fin[q]

        @pl.when((here_p == n_blocks - 1) & (here_c == nc - 1))
        def _():
            exchange.finish(*ex_refs)

    row = pl.BlockSpec((tc, wb), lambda p, c: (nc - 1 - c, p))
    outs = pl.pallas_call(
        body, name="rwkv_scan_bwd", grid=(n_blocks, nc),
        in_specs=[row] * 7 + [pl.BlockSpec((tc, HEAD, wb), lambda p, c: (nc - 1 - c, 0, p)),
                              pl.BlockSpec((HEAD, wb), lambda p, c: (0, p)),
                              pl.BlockSpec((LANES, LANES), lambda p, c: (0, 0))] + [HBM_SPEC] * nx,
        out_specs=[row] * 6 + [HBM_SPEC] * nx,
        out_shape=[jax.ShapeDtypeStruct((t_dim, c_dim), F32)] * 6 + exchange.out_shape,
        scratch_shapes=[pltpu.VMEM((HEAD, wb), F32)] * 2 + exchange.scratch,
        compiler_params=_params("arbitrary", "arbitrary"),
    )(r, w, k, v, kk, a, dy, sall, sfin, _head_ones(), *exchange.arrays)
    return outs[:6], outs[6:]


def _fox_fwd(z, ct, cq, q_cb, k_cb, v_cb, n_pairs, blk):
    t_dim = z.shape[0]
    blk = _pick(t_dim, blk)
    nq = t_dim // blk
    scale = HEAD ** -0.5

    def body(q_ref, k_ref, v_ref, ct_ref, cq_ref, o_ref, lse_ref):
        i = pl.program_id(1)
        rowi = lax.broadcasted_iota(jnp.int32, (blk, blk), 0)
        coli = lax.broadcasted_iota(jnp.int32, (blk, blk), 1)
        masks = _head_masks()
        qv = q_ref[...]
        qs = [(qv * mk).astype(BF16) for mk in masks]
        cqs = [cq_ref[:, hh:hh + 1] for hh in range(2)]

        def kv_step(j, carry, masked):
            rows = pl.ds(pl.multiple_of(j * blk, blk), blk)
            kb = k_ref[rows, :].astype(BF16)
            vv = v_ref[rows, :]
            stats, acc = list(carry[:4]), carry[4]
            rescale, add = 0.0, 0.0
            for hh in range(2):
                m, l = stats[2 * hh], stats[2 * hh + 1]
                s = (lax.dot_general(qs[hh], kb, _DOT_DIMS["nt"], preferred_element_type=F32) * scale
                     + (cqs[hh] - ct_ref[hh:hh + 1, rows]))
                if masked:
                    s = jnp.where(rowi >= coli, s, -jnp.inf)
                m_new = jnp.maximum(m, jnp.max(s, axis=1, keepdims=True))
                alpha = jnp.exp(m - m_new)
                pr = jnp.exp(s - m_new)
                stats[2 * hh], stats[2 * hh + 1] = m_new, l * alpha + jnp.sum(pr, axis=1, keepdims=True)
                rescale = rescale + alpha * masks[hh]
                hi = pr.astype(BF16)
                both = jnp.concatenate([hi, (pr - hi.astype(F32)).astype(BF16)], axis=1)
                vh = (vv * masks[hh]).astype(BF16)
                add = add + jnp.dot(both, jnp.concatenate([vh, vh], axis=0), preferred_element_type=F32)
            return (*stats, acc * rescale + add)

        neg, zero = jnp.full((blk, 1), -jnp.inf, F32), jnp.zeros((blk, 1), F32)
        carry = lax.fori_loop(0, i, functools.partial(kv_step, masked=False),
                              (neg, zero, neg, zero, jnp.zeros((blk, LANES), F32)))
        m0, l0, m1, l1, acc = kv_step(i, carry, True)
        o_ref[...] = acc * (masks[0] / l0 + masks[1] / l1)
        lse_ref[:, 0:1] = m0 + jnp.log(l0)
        lse_ref[:, 1:2] = m1 + jnp.log(l1)

    full = lambda cb: pl.BlockSpec((t_dim, LANES), lambda p, i, cb=cb: (0, cb + p))
    return pl.pallas_call(
        body, name="fox_attn_fwd", grid=(n_pairs, nq),
        in_specs=[pl.BlockSpec((blk, LANES), lambda p, i: (i, q_cb + p)), full(k_cb), full(v_cb),
                  pl.BlockSpec((None, SUBLANES, t_dim), lambda p, i: (p, 0, 0)),
                  pl.BlockSpec((None, blk, 2), lambda p, i: (p, i, 0))],
        out_specs=[pl.BlockSpec((blk, LANES), lambda p, i: (i, p)), pl.BlockSpec((None, blk, 2), lambda p, i: (p, i, 0))],
        out_shape=[jax.ShapeDtypeStruct((t_dim, n_pairs * LANES), F32), jax.ShapeDtypeStruct((n_pairs, t_dim, 2), F32)],
        compiler_params=_params("parallel", "arbitrary"),
    )(z, z, z, ct, cq)


def _fox_bwd(z, ct, cq, rowdot, do, lse, q_cb, k_cb, v_cb, n_pairs, blk):
    t_dim = z.shape[0]
    blk = _pick(t_dim, blk)
    nb = t_dim // blk
    scale = HEAD ** -0.5

    def body(q_ref, k_ref, v_ref, ct_ref, cq_ref, rd_ref, do_ref, lse_ref, dq_ref, dk_ref, dv_ref, dc_ref):
        j = pl.program_id(1)

        @pl.when(j == 0)
        def _():
            dq_ref[...] = jnp.zeros_like(dq_ref)

        rowi = lax.broadcasted_iota(jnp.int32, (blk, blk), 0)
        coli = lax.broadcasted_iota(jnp.int32, (blk, blk), 1)
        krows = pl.ds(pl.multiple_of(j * blk, blk), blk)
        masks = _head_masks()
        kv, vb = k_ref[...], v_ref[...].astype(BF16)
        kb = kv.astype(BF16)
        ks = [(kv * mk).astype(BF16) for mk in masks]
        cks = [ct_ref[hh:hh + 1, krows] for hh in range(2)]

        def q_step(i, carry, masked):
            dk, dv, dcs = carry[0], carry[1], list(carry[2:])
            rows = pl.ds(pl.multiple_of(i * blk, blk), blk)
            qv, dov = q_ref[rows, :], do_ref[rows, :]
            dq = 0.0
            for hh in range(2):
                qh = (qv * masks[hh]).astype(BF16)
                doh = (dov * masks[hh]).astype(BF16)
                s = (lax.dot_general(qh, kb, _DOT_DIMS["nt"], preferred_element_type=F32) * scale
                     + (cq_ref[rows, hh:hh + 1] - cks[hh]))
                pr = jnp.exp(s - lse_ref[rows, hh:hh + 1])
                if masked:
                    pr = jnp.where(rowi >= coli, pr, 0.0)
                dv = dv + lax.dot_general(pr.astype(BF16), doh, _DOT_DIMS["tn"], preferred_element_type=F32)
                dp = lax.dot_general(doh, vb, _DOT_DIMS["nt"], preferred_element_type=F32)
                ds = pr * (dp - rd_ref[rows, hh * HEAD:hh * HEAD + 1])
                dsb = ds.astype(BF16)
                dq = dq + jnp.dot(dsb, ks[hh], preferred_element_type=F32)
                dk = dk + lax.dot_general(dsb, qh, _DOT_DIMS["tn"], preferred_element_type=F32)
                dcs[hh] = dcs[hh] - _colsum(ds)
            dq_ref[rows, :] += dq * scale
            return (dk, dv, *dcs)

        zero_row = jnp.zeros((1, blk), F32)
        init = (jnp.zeros((blk, LANES), F32), jnp.zeros((blk, LANES), F32), zero_row, zero_row)
        carry = q_step(j, init, True)
        dk, dv, dc0, dc1 = lax.fori_loop(j + 1, nb, functools.partial(q_step, masked=False), carry)
        dk_ref[...] = dk * scale
        dv_ref[...] = dv
        dc_ref[0:1, :] = dc0
        dc_ref[1:2, :] = dc1
        dc_ref[2:SUBLANES, :] = jnp.zeros((SUBLANES - 2, blk), F32)

    full = lambda: pl.BlockSpec((t_dim, LANES), lambda p, j: (0, p))
    blkspec = pl.BlockSpec((blk, LANES), lambda p, j: (j, p))
    return pl.pallas_call(
        body, name="fox_attn_bwd", grid=(n_pairs, nb),
        in_specs=[pl.BlockSpec((t_dim, LANES), lambda p, j: (0, q_cb + p)),
                  pl.BlockSpec((blk, LANES), lambda p, j: (j, k_cb + p)),
                  pl.BlockSpec((blk, LANES), lambda p, j: (j, v_cb + p)),
                  pl.BlockSpec((None, SUBLANES, t_dim), lambda p, j: (p, 0, 0)),
                  pl.BlockSpec((None, t_dim, 2), lambda p, j: (p, 0, 0)), full(), full(),
                  pl.BlockSpec((None, t_dim, 2), lambda p, j: (p, 0, 0))],
        out_specs=[full(), blkspec, blkspec, pl.BlockSpec((None, SUBLANES, blk), lambda p, j: (p, 0, j))],
        out_shape=[jax.ShapeDtypeStruct((t_dim, n_pairs * LANES), F32)] * 3
                  + [jax.ShapeDtypeStruct((n_pairs, SUBLANES, t_dim), F32)],
        compiler_params=_params("parallel", "arbitrary"),
    )(z, z, z, ct, cq, rowdot, do, lse)


HBM_SPEC = pl.BlockSpec(memory_space=pltpu.HBM)


class _Gather:
    def __init__(self, shards):
        self.arrays = list(shards)
        self.n = len(self.arrays)
        self.out_shape = [jax.ShapeDtypeStruct((NDEV,) + s.shape, s.dtype) for s in self.arrays]
        self.scratch = [pltpu.SemaphoreType.DMA((7 * self.n,)), pltpu.SemaphoreType.DMA((7 * self.n,)),
                        pltpu.SemaphoreType.DMA((self.n,))]

    def _plan(self, x_refs, out_refs, sems):
        send_sems, recv_sems, local_sems = sems
        x, y, c = lax.axis_index("x"), lax.axis_index("y"), lax.axis_index("c")
        me, sibling = (x, y, c), (x, y, 1 - c)
        chips = [(1 - x, y), (x, 1 - y), (1 - x, 1 - y)]

        def copy(a, k, block, to, from_input=False):
            px, py, pc = block
            slot = out_refs[a].at[4 * px + 2 * py + pc]
            return pltpu.make_async_remote_copy(
                src_ref=x_refs[a] if from_input else slot, dst_ref=slot,
                send_sem=send_sems.at[7 * a + k], recv_sem=recv_sems.at[7 * a + k], device_id=to, device_id_type=MESH)

        mine = [pltpu.make_async_copy(x_refs[a], out_refs[a].at[4 * x + 2 * y + c], local_sems.at[a])
                for a in range(self.n)]
        first = []
        for a in range(self.n):
            first += [copy(a, 1 + j, me, (*chip, c), from_input=True) for j, chip in enumerate(chips)]
            first.append(copy(a, 0, me, sibling, from_input=True))
        over_ici = [copy(a, 1 + j, (*chip, c), me) for a in range(self.n) for j, chip in enumerate(chips)]
        passed = [copy(a, 4 + j, (*chip, c), sibling) for a in range(self.n) for j, chip in enumerate(chips)]
        from_sibling = []
        for a in range(self.n):
            from_sibling.append(copy(a, 0, sibling, me))
            from_sibling += [copy(a, 4 + j, (*chip, 1 - c), me) for j, chip in enumerate(chips)]
        return mine, first, over_ici, passed, from_sibling

    def start(self, *refs):
        mine, first, _, _, _ = self._plan(*refs)
        for cp in mine + first:
            cp.start()

    def forward(self, *refs):
        _, _, over_ici, passed, _ = self._plan(*refs)
        for arrived, onward in zip(over_ici, passed):
            arrived.wait_recv()
            onward.start()

    def finish(self, *refs):
        mine, first, _, passed, from_sibling = self._plan(*refs)
        for cp in from_sibling:
            cp.wait_recv()
        for cp in first + passed:
            cp.wait_send()
        for cp in mine:
            cp.wait()


def _all_gather(shards, name):
    ga = _Gather(shards)

    def body(*refs):
        parts = (refs[:ga.n], refs[ga.n:2 * ga.n], refs[2 * ga.n:])
        ga.start(*parts)
        ga.forward(*parts)
        ga.finish(*parts)

    return pl.pallas_call(body, name=name, out_shape=ga.out_shape, in_specs=[HBM_SPEC] * ga.n,
                          out_specs=[HBM_SPEC] * ga.n, scratch_shapes=ga.scratch)(*ga.arrays)


class _Exchange:
    def __init__(self, arrays):
        self.arrays = list(arrays)
        self.n = len(self.arrays)
        self.per_dest = [a.ndim == 3 for a in self.arrays]
        self.out_shape = [jax.ShapeDtypeStruct(a.shape if pd else (NDEV,) + a.shape, a.dtype)
                          for a, pd in zip(self.arrays, self.per_dest)]
        self.scratch = [pltpu.SemaphoreType.DMA((7 * self.n,)), pltpu.SemaphoreType.DMA((7 * self.n,)),
                        pltpu.SemaphoreType.DMA((self.n,))]

    def _copies(self, in_refs, out_refs, sems):
        send_sems, recv_sems, local_sems = sems
        x, y, c = lax.axis_index("x"), lax.axis_index("y"), lax.axis_index("c")
        me = 4 * x + 2 * y + c
        own, sends, recvs = [], [], []
        for a in range(self.n):
            mine = in_refs[a].at[me] if self.per_dest[a] else in_refs[a]
            own.append(pltpu.make_async_copy(mine, out_refs[a].at[me], local_sems.at[a]))
            for k in range(1, NDEV):
                px = 1 - x if k & 4 else x
                py = 1 - y if k & 2 else y
                pc = 1 - c if k & 1 else c
                peer = 4 * px + 2 * py + pc
                sem = dict(send_sem=send_sems.at[7 * a + k - 1], recv_sem=recv_sems.at[7 * a + k - 1],
                           device_id=(px, py, pc), device_id_type=MESH)
                src = in_refs[a].at[peer] if self.per_dest[a] else in_refs[a]
                sends.append(pltpu.make_async_remote_copy(src_ref=src, dst_ref=out_refs[a].at[me], **sem))
                recvs.append(pltpu.make_async_remote_copy(src_ref=src, dst_ref=out_refs[a].at[peer], **sem))
        return own, sends, recvs

    def start(self, in_refs, out_refs, sems):
        own, sends, _ = self._copies(in_refs, out_refs, sems)
        for cp in own + sends:
            cp.start()

    def finish(self, in_refs, out_refs, sems):
        own, sends, recvs = self._copies(in_refs, out_refs, sems)
        for cp in recvs:
            cp.wait_recv()
        for cp in sends:
            cp.wait_send()
        for cp in own:
            cp.wait()


def _grad_exchange(arrays, name):
    ex = _Exchange(arrays)

    def body(*refs):
        in_refs, out_refs, sems = refs[:ex.n], refs[ex.n:2 * ex.n], refs[2 * ex.n:]
        ex.start(in_refs, out_refs, sems)
        ex.finish(in_refs, out_refs, sems)

    return pl.pallas_call(body, name=name, out_shape=ex.out_shape, in_specs=[HBM_SPEC] * ex.n,
                          out_specs=[HBM_SPEC] * ex.n, scratch_shapes=ex.scratch)(*ex.arrays)


NCHIP = NDEV // 2


def _sibling_swap(parts, name):
    def body(p_ref, out_ref, send_sems, recv_sems):
        x, y, c = lax.axis_index("x"), lax.axis_index("y"), lax.axis_index("c")
        copies = [pltpu.make_async_remote_copy(
            src_ref=p_ref.at[2 * i + 1 - c], dst_ref=out_ref.at[i], send_sem=send_sems.at[i], recv_sem=recv_sems.at[i],
            device_id=(x, y, 1 - c), device_id_type=MESH) for i in range(NCHIP)]
        for cp in copies:
            cp.start()
        for cp in copies:
            cp.wait_recv()
        for cp in copies:
            cp.wait_send()

    return pl.pallas_call(
        body, name=name, out_shape=jax.ShapeDtypeStruct((NCHIP,) + parts.shape[1:], parts.dtype),
        in_specs=[HBM_SPEC], out_specs=HBM_SPEC,
        scratch_shapes=[pltpu.SemaphoreType.DMA((NCHIP,)), pltpu.SemaphoreType.DMA((NCHIP,))],
    )(parts)


def _add_pairs(a, b, name):
    n, rows, width = a.shape
    br = _pick(rows, 256, 2 * SUBLANES)

    def body(a_ref, b_ref, o_ref):
        o_ref[...] = (a_ref[...].astype(F32) + b_ref[...].astype(F32)).astype(o_ref.dtype)

    blk = pl.BlockSpec((None, br, width), lambda i, j: (i, j, 0))
    return pl.pallas_call(body, name=name, grid=(n, rows // br), in_specs=[blk, blk], out_specs=blk,
                          out_shape=jax.ShapeDtypeStruct(a.shape, a.dtype),
                          compiler_params=_params("parallel", "parallel"))(a, b)


class _ChipExchange:
    FLIPS = ((0, 1), (1, 0), (1, 1))

    def __init__(self, arrays):
        self.arrays = list(arrays)
        self.n = len(self.arrays)
        self.out_shape = [jax.ShapeDtypeStruct(a.shape, a.dtype) for a in self.arrays]
        self.scratch = [pltpu.SemaphoreType.DMA((3 * self.n,)), pltpu.SemaphoreType.DMA((3 * self.n,)),
                        pltpu.SemaphoreType.DMA((self.n,))]

    def _copies(self, in_refs, out_refs, sems):
        send_sems, recv_sems, local_sems = sems
        x, y, c = lax.axis_index("x"), lax.axis_index("y"), lax.axis_index("c")
        here = 2 * x + y
        own, sends, recvs = [], [], []
        for a in range(self.n):
            own.append(pltpu.make_async_copy(in_refs[a].at[here], out_refs[a].at[here], local_sems.at[a]))
            for k, (fx, fy) in enumerate(self.FLIPS):
                px = 1 - x if fx else x
                py = 1 - y if fy else y
                there = 2 * px + py
                sem = dict(send_sem=send_sems.at[3 * a + k], recv_sem=recv_sems.at[3 * a + k],
                           device_id=(px, py, c), device_id_type=MESH)
                src = in_refs[a].at[there]
                sends.append(pltpu.make_async_remote_copy(src_ref=src, dst_ref=out_refs[a].at[here], **sem))
                recvs.append(pltpu.make_async_remote_copy(src_ref=src, dst_ref=out_refs[a].at[there], **sem))
        return own, sends, recvs

    start = _Exchange.start
    finish = _Exchange.finish


def _adamw_body(p_ref, w_ref, m_ref, v_ref, g_out, d_out, m_out, v_out):
    g = p_ref[0].astype(F32)
    for d in range(1, p_ref.shape[0]):
        g = g + p_ref[d].astype(F32)
    mn = ADAM_B1 * m_ref[...] + (1.0 - ADAM_B1) * g
    vn = ADAM_B2 * v_ref[...] + (1.0 - ADAM_B2) * jnp.square(g)
    m_hat = mn / (1.0 - ADAM_B1 ** ADAM_STEP)
    v_hat = vn / (1.0 - ADAM_B2 ** ADAM_STEP)
    g_out[...] = g
    d_out[...] = -ADAM_LR * (m_hat / (jnp.sqrt(v_hat) + ADAM_EPS) + ADAM_WD * w_ref[...])
    m_out[...] = mn
    v_out[...] = vn


def _adamw(partials, w, m, v, name):
    rows, width = w.shape
    n = partials.shape[0]
    body = functools.partial(_adamw_body)
    if rows % (2 * SUBLANES) == 0:
        br = _pick(rows, 128, 2 * SUBLANES)
        grid, shape, index, index3 = rows // br, (br, width), lambda i: (i, 0), lambda i: (0, i, 0)
    else:
        bc = _pick(width, 2 * LANES)
        grid, shape, index, index3 = width // bc, (rows, bc), lambda i: (0, i), lambda i: (0, 0, i)
    blk = pl.BlockSpec(shape, index)
    return pl.pallas_call(
        body, name=name, grid=(grid,),
        in_specs=[pl.BlockSpec((n,) + shape, index3), blk, blk, blk],
        out_specs=[pl.BlockSpec((None,) + shape, index3)] * 4,
        out_shape=[jax.ShapeDtypeStruct((1, rows, width), F32)] * 4,
        compiler_params=_params("parallel"),
    )(partials, w, m, v)


def _pack_rows(flat):
    n = flat.shape[0]
    padded = _round_up(n, PACK_ALIGN)
    return jnp.pad(flat, (0, padded - n)).reshape(padded // PACK_W, PACK_W)


def _split_shards(full, axis):
    rows, cols = full.shape
    if axis == 0:
        return full.reshape(NDEV, rows // NDEV, cols)
    width = cols // NDEV
    return jnp.stack([full[:, d * width:(d + 1) * width] for d in range(NDEV)])


def _join_shards(blocks, axis):
    if axis == 0:
        return blocks.reshape(-1, blocks.shape[2])
    return jnp.concatenate([blocks[d] for d in range(NDEV)], axis=1)


SHARDED = (("w_in", 1), ("rw_w_lora_up", 1), ("rw_a_lora_up", 1), ("w_up_rwkv", 1), ("w_up_fox", 1),
           ("w_out", 0), ("ple_proj", 1), ("ple_gate_w", 0))
REPLICATED = ("norm_g", "rw_shift_mu", "rw_w0", "rw_a0", "rw_k_k", "rw_k_a", "rw_r_k", "rw_ln_g", "rw_ln_b",
              "fox_b_f", "ple_norm_g", "final_norm_g")
WEIGHTS = ("norm_g", "w_in", "rw_shift_mu", "rw_w0", "rw_w_lora_up", "rw_a0", "rw_a_lora_up", "rw_k_k", "rw_k_a",
           "rw_r_k", "rw_ln_g", "rw_ln_b", "fox_b_f", "w_up_rwkv", "w_up_fox", "w_out", "ple_proj", "ple_gate_w",
           "ple_norm_g", "final_norm_g")


def _local_step(x, p, tgt, wz, other_shards, rep, dims):
    t_dim, d_model, c_rw, lora, c_fox, h_fox, sec = dims
    bt = _pick(t_dim, 256, 2 * SUBLANES)
    bt_many = _pick(t_dim, 128, 2 * SUBLANES)
    n_pairs = c_fox // LANES
    row = lambda a: a.reshape(1, -1)
    norm_g, mu, w0, a0 = row(rep["norm_g"]), row(rep["rw_shift_mu"]), row(rep["rw_w0"]), row(rep["rw_a0"])
    k_k, k_a, r_k = row(rep["rw_k_k"]), row(rep["rw_k_a"]), row(rep["rw_r_k"])
    ln_g, ln_b = row(rep["rw_ln_g"]), row(rep["rw_ln_b"])
    g2, g3 = row(rep["ple_norm_g"]), row(rep["final_norm_g"])
    b_f = jnp.pad(row(rep["fox_b_f"]), ((0, 0), (0, LANES - h_fox)))
    e_head = _head_ones(F32)
    c4 = 4 * c_rw
    inv_d = 1.0 / d_model
    decay_k = math.exp(-0.5)

    def norm_in(x_ref, g_ref, h_ref):
        xv = x_ref[...]
        rms = lax.rsqrt(jnp.mean(xv * xv, axis=-1, keepdims=True) + NORM_EPS)
        h_ref[...] = (xv * rms * g_ref[...]).astype(BF16)

    (h,) = _rowcall(norm_in, "norm_in", t_dim, bt, [x], [norm_g], [(d_model, BF16)])
    z, gathered = _matmul(h, wz, "nn", "proj_in", bn=1408, exchange=_Gather(other_shards))
    wl, wa, wur, wuf, wo, pp, pg = [_join_shards(g, ax) for (_, ax), g in zip(SHARDED[1:], gathered)]

    def rw_values(zs, w0_ref, wl_ref, a0_ref, wa_ref, kk_ref, ka_ref, e_ref):
        k = zs[:, c_rw:2 * c_rw]
        tw = jnp.tanh(zs[:, c4:c4 + lora])
        al = zs[:, c4 + lora:c4 + 2 * lora]
        sw = _sigmoid(w0_ref[...] + jnp.dot(tw.astype(BF16), wl_ref[...], preferred_element_type=F32))
        decay = jnp.exp(-decay_k * sw)
        a = _sigmoid(a0_ref[...] + jnp.dot(al.astype(BF16), wa_ref[...], preferred_element_type=F32))
        kk0 = k * kk_ref[...]
        nrm = jnp.sqrt(_headsum(kk0 * kk0, e_ref[...]))
        inv = 1.0 / jnp.maximum(nrm, 1e-12)
        k2 = k * (1.0 + (a - 1.0) * ka_ref[...])
        return k, tw, al, sw, decay, a, kk0, nrm, inv, k2

    def rw_prep(z_ref, mu_ref, w0_ref, wl_ref, a0_ref, wa_ref, kk_ref, ka_ref, e_ref,
                r_o, w_o, k_o, v_o, kk_o, a_o, g_o, carry):
        _first_step_zero(carry)
        zv = z_ref[...]
        zs = zv + (_shift_down(zv, carry) - zv) * mu_ref[...]
        k, tw, al, sw, decay, a, kk0, nrm, inv, k2 = rw_values(zs, w0_ref, wl_ref, a0_ref, wa_ref, kk_ref, ka_ref, e_ref)
        r_o[...] = zs[:, 0:c_rw]
        w_o[...] = decay
        k_o[...] = k2
        v_o[...] = zs[:, 2 * c_rw:3 * c_rw]
        kk_o[...] = kk0 * inv
        a_o[...] = a
        g_o[...] = zs[:, 3 * c_rw:c4]

    rw_consts = [mu, w0, wl, a0, wa, k_k, k_a, e_head]
    r_s, w_s, k_s, v_s, kk_s, a_s, g_s = _rowcall(
        rw_prep, "rwkv_prep", t_dim, bt, [(z, 0, sec)], rw_consts, [(c_rw, F32)] * 7,
        scratch=[pltpu.VMEM((1, sec), F32)])
    pairs_fwd = max(n for n in (1, 2, 4) if c_rw % (n * LANES) == 0)
    pairs_bwd = pairs_fwd
    y_s, s_all, s_fin = _scan_fwd(r_s, w_s, k_s, v_s, kk_s, a_s, 128, pairs_fwd)

    def rw_post_values(y, r, k2, v, g, lng_ref, lnb_ref, rk_ref, e):
        mean = _headsum(y, e) * (1.0 / HEAD)
        d = y - mean
        rstd = lax.rsqrt(_headsum(d * d, e) * (1.0 / HEAD) + GN_EPS)
        yh = d * rstd
        rk = _headsum(r * k2 * rk_ref[...], e)
        yo = yh * lng_ref[...] + lnb_ref[...] + rk * v
        sg = _sigmoid(g)
        return rstd, yh, rk, yo, sg

    def rw_post(y_ref, r_ref, k_ref, v_ref, g_ref, lng_ref, lnb_ref, rk_ref, e_ref, out_ref):
        g = g_ref[...]
        _, _, _, yo, sg = rw_post_values(y_ref[...], r_ref[...], k_ref[...], v_ref[...], g, lng_ref, lnb_ref, rk_ref, e_ref[...])
        out_ref[...] = (yo * g * sg).astype(BF16)

    (y_rw,) = _rowcall(rw_post, "rwkv_post", t_dim, bt, [y_s, r_s, k_s, v_s, g_s], [ln_g, ln_b, r_k, e_head], [(c_rw, BF16)])

    fl_cb = (sec + 4 * c_fox) // LANES
    hp = _round_up(h_fox, SUBLANES)
    bt_c = _pick(t_dim, 256)
    tri = (jnp.arange(bt_c)[:, None] >= jnp.arange(bt_c)[None, :]).astype(F32)

    rows8 = n_pairs * SUBLANES
    pair_rows = (jnp.arange(rows8)[:, None] // SUBLANES * 2 + jnp.arange(rows8)[:, None] % SUBLANES
                 == jnp.arange(LANES)[None, :]) & (jnp.arange(rows8)[:, None] % SUBLANES < 2)
    pair_rows = pair_rows.astype(F32)

    def fox_decay(fl_ref, bf_ref, tri_ref, sel_ref, ct_ref, cq_ref, carry):
        _first_step_zero(carry)
        lf = _log_sigmoid(fl_ref[...] + bf_ref[...])
        c = jnp.dot(tri_ref[...], lf, precision=HI, preferred_element_type=F32) + carry[...]
        carry[...] = c[bt_c - 1:bt_c, :]
        ct = jnp.dot(sel_ref[...], jnp.transpose(c), precision=HI, preferred_element_type=F32)
        ct_ref[...] = ct.reshape(n_pairs, SUBLANES, bt_c)
        for pair in range(n_pairs):
            cq_ref[pair] = c[:, 2 * pair:2 * pair + 2]

    ct, cq = pl.pallas_call(
        fox_decay, name="fox_decay", grid=(t_dim // bt_c,),
        in_specs=[pl.BlockSpec((bt_c, LANES), lambda i: (i, fl_cb)), pl.BlockSpec((1, LANES), lambda i: (0, 0)),
                  pl.BlockSpec((bt_c, bt_c), lambda i: (0, 0)), pl.BlockSpec((rows8, LANES), lambda i: (0, 0))],
        out_specs=[pl.BlockSpec((n_pairs, SUBLANES, bt_c), lambda i: (0, 0, i)),
                   pl.BlockSpec((n_pairs, bt_c, 2), lambda i: (0, i, 0))],
        out_shape=[jax.ShapeDtypeStruct((n_pairs, SUBLANES, t_dim), F32), jax.ShapeDtypeStruct((n_pairs, t_dim, 2), F32)],
        scratch_shapes=[pltpu.VMEM((1, LANES), F32)], compiler_params=_params("arbitrary"),
    )(z, b_f, tri, pair_rows)
    q_cb = sec // LANES
    k_cb, v_cb = q_cb + n_pairs, q_cb + 2 * n_pairs
    o_fox, lse = _fox_fwd(z, ct, cq, q_cb, k_cb, v_cb, n_pairs, 1024)

    def fox_post(o_ref, z_ref, out_ref):
        g = z_ref[:, 3 * c_fox:4 * c_fox]
        out_ref[...] = (o_ref[...] * g * _sigmoid(g)).astype(BF16)

    (y_fox,) = _rowcall(fox_post, "fox_post", t_dim, bt, [o_fox, (z, 1, sec)], [], [(c_fox, BF16)])

    u_rw = _matmul(y_rw, wur, "nn", "up_rwkv")
    u_fox = _matmul(y_fox, wuf, "nn", "up_fox")

    def merge(ur_ref, uf_ref, z_ref, out_ref):
        s1 = _sigmoid(z_ref[:, 0:d_model])
        s2 = _sigmoid(z_ref[:, d_model:2 * d_model])
        out_ref[...] = (s1 * ur_ref[...] + s2 * uf_ref[...]).astype(BF16)

    (merged,) = _rowcall(merge, "merge", t_dim, bt, [u_rw, u_fox, (z, 2, sec)], [], [(d_model, BF16)])
    mo = _matmul(merged, wo, "nn", "proj_out")

    def resid_norm(x_ref, mo_ref, g_ref, x1_ref, n2_ref):
        x1 = x_ref[...] + mo_ref[...]
        rms = lax.rsqrt(jnp.mean(x1 * x1, axis=-1, keepdims=True) + NORM_EPS)
        x1_ref[...] = x1
        n2_ref[...] = (x1 * rms * g_ref[...]).astype(BF16)

    x1, n2 = _rowcall(resid_norm, "resid_norm", t_dim, bt, [x, mo], [g2], [(d_model, F32), (d_model, BF16)])
    ple = _matmul(p, pp, "nn", "ple_proj")
    gl = _matmul(n2, pg, "nn", "ple_gate")

    def head(x1_ref, ple_ref, gl_ref, tgt_ref, g_ref, dx2_ref, dple_ref, dgl_ref, loss_ref, dg3_ref):
        _first_step_zero(loss_ref, dg3_ref)
        sg = _sigmoid(gl_ref[...])
        pl_v = ple_ref[...]
        x2 = x1_ref[...] + pl_v * sg
        rms = lax.rsqrt(jnp.mean(x2 * x2, axis=-1, keepdims=True) + NORM_EPS)
        xn = x2 * rms
        diff = xn * g_ref[...] - tgt_ref[...]
        loss_ref[...] += 0.5 * jnp.sum(jnp.mean(diff * diff, axis=-1, keepdims=True), axis=0, keepdims=True)
        dyf = diff * inv_d
        dg3_ref[...] += _colsum(dyf * xn)
        gy = dyf * g_ref[...]
        dx2 = rms * (gy - xn * jnp.mean(xn * gy, axis=-1, keepdims=True))
        dx2_ref[...] = dx2
        dple_ref[...] = (dx2 * sg).astype(BF16)
        dgl_ref[...] = (dx2 * pl_v * sg * (1.0 - sg)).astype(BF16)

    dx2, dple, dgl, loss, d_g3 = _rowcall(
        head, "head", t_dim, bt, [x1, ple, gl, tgt], [g3], [(d_model, F32), (d_model, BF16), (d_model, BF16)],
        acc_outs=[(1, 1), (1, d_model)])

    d_pp = _matmul(p, dple, "tn", "d_ple_proj", out_dtype=BF16)
    d_pg = _matmul(n2, dgl, "tn", "d_ple_gate", out_dtype=BF16)
    dn2 = _matmul(dgl, pg, "nt", "d_n2")

    def resid_norm_bwd(dx2_ref, dn2_ref, x1_ref, g_ref, dx1_ref, dx1b_ref, dg2_ref):
        _first_step_zero(dg2_ref)
        x1 = x1_ref[...]
        rms = lax.rsqrt(jnp.mean(x1 * x1, axis=-1, keepdims=True) + NORM_EPS)
        xn = x1 * rms
        dn = dn2_ref[...]
        dg2_ref[...] += _colsum(dn * xn)
        gy = dn * g_ref[...]
        dx1 = dx2_ref[...] + rms * (gy - xn * jnp.mean(xn * gy, axis=-1, keepdims=True))
        dx1_ref[...] = dx1
        dx1b_ref[...] = dx1.astype(BF16)

    dx1, dx1b, d_g2 = _rowcall(resid_norm_bwd, "resid_norm_bwd", t_dim, bt, [dx2, dn2, x1], [g2],
                               [(d_model, F32), (d_model, BF16)], acc_outs=[(1, d_model)])
    d_wo = _matmul(merged, dx1b, "tn", "d_w_out", out_dtype=BF16)
    dmerged = _matmul(dx1b, wo, "nt", "d_merged")

    def merge_bwd(dm_ref, ur_ref, uf_ref, z_ref, dur_ref, duf_ref, dzg_ref):
        dm = dm_ref[...]
        s1 = _sigmoid(z_ref[:, 0:d_model])
        s2 = _sigmoid(z_ref[:, d_model:2 * d_model])
        dur_ref[...] = (dm * s1).astype(BF16)
        duf_ref[...] = (dm * s2).astype(BF16)
        dzg_ref[:, 0:d_model] = (dm * ur_ref[...] * s1 * (1.0 - s1)).astype(BF16)
        dzg_ref[:, d_model:2 * d_model] = (dm * uf_ref[...] * s2 * (1.0 - s2)).astype(BF16)
        if sec > 2 * d_model:
            dzg_ref[:, 2 * d_model:sec] = jnp.zeros((dm.shape[0], sec - 2 * d_model), BF16)

    du_rw, du_fox, dz_gate = _rowcall(merge_bwd, "merge_bwd", t_dim, bt, [dmerged, u_rw, u_fox, (z, 2, sec)], [],
                                      [(d_model, BF16), (d_model, BF16), (sec, BF16)])
    d_wur = _matmul(y_rw, du_rw, "tn", "d_w_up_rwkv", out_dtype=BF16)
    d_wuf = _matmul(y_fox, du_fox, "tn", "d_w_up_fox", out_dtype=BF16)
    dy_rw = _matmul(du_rw, wur, "nt", "d_y_rwkv")
    dy_fox = _matmul(du_fox, wuf, "nt", "d_y_fox")

    def fox_post_bwd(dy_ref, o_ref, z_ref, e_ref, do_ref, dg_ref, rd_ref):
        g = z_ref[:, 3 * c_fox:4 * c_fox]
        sg = _sigmoid(g)
        dy, o = dy_ref[...], o_ref[...]
        do = dy * g * sg
        do_ref[...] = do
        dg_ref[...] = (dy * o * sg * (1.0 + g * (1.0 - sg))).astype(BF16)
        rd_ref[...] = _headsum(do.astype(BF16).astype(F32) * o, e_ref[...])

    do_fox, dg_fox, rowdot = _rowcall(fox_post_bwd, "fox_post_bwd", t_dim, bt, [dy_fox, o_fox, (z, 1, sec)], [e_head],
                                      [(c_fox, F32), (c_fox, BF16), (c_fox, F32)])
    dq_f, dk_f, dv_f, dc_t = _fox_bwd(z, ct, cq, rowdot, do_fox, lse, q_cb, k_cb, v_cb, n_pairs, 512)
    sel = (jnp.arange(hp)[:, None] // 2 * SUBLANES + jnp.arange(hp)[:, None] % 2 == jnp.arange(rows8)[None, :]).astype(F32)
    tri_rev = (jnp.arange(bt_c)[:, None] >= jnp.arange(bt_c)[None, :]).astype(F32)
    bf_col = b_f.reshape(LANES, 1)[0:hp]
    nbc = t_dim // bt_c

    def fox_decay_bwd(dc_ref, fl_ref, sel_ref, tri_ref, bf_ref, dfl_ref, dbf_ref, carry):
        _first_step_zero(carry, dbf_ref)
        dc = jnp.dot(sel_ref[...], dc_ref[...].reshape(rows8, bt_c), precision=HI, preferred_element_type=F32)
        dlf = jnp.dot(dc, tri_ref[...], precision=HI, preferred_element_type=F32) + carry[...]
        carry[...] = dlf[:, 0:1]
        flt = jnp.transpose(fl_ref[...])[0:hp, :]
        dfl = dlf * _sigmoid(-(flt + bf_ref[...]))
        head_row = lax.broadcasted_iota(jnp.int32, (hp, bt_c), 0)
        dfl = jnp.where(head_row < h_fox, dfl, 0.0)
        dbf_ref[...] += jnp.sum(dfl, axis=1, keepdims=True)
        full = jnp.concatenate([dfl, jnp.zeros((LANES - hp, bt_c), F32)], axis=0) if hp < LANES else dfl
        dfl_ref[...] = jnp.transpose(full).astype(BF16)

    dz_fl, d_bf = pl.pallas_call(
        fox_decay_bwd, name="fox_decay_bwd", grid=(nbc,),
        in_specs=[pl.BlockSpec((n_pairs, SUBLANES, bt_c), lambda i: (0, 0, nbc - 1 - i)),
                  pl.BlockSpec((bt_c, LANES), lambda i: (nbc - 1 - i, fl_cb)),
                  pl.BlockSpec(sel.shape, lambda i: (0, 0)), pl.BlockSpec((bt_c, bt_c), lambda i: (0, 0)),
                  pl.BlockSpec((hp, 1), lambda i: (0, 0))],
        out_specs=[pl.BlockSpec((bt_c, LANES), lambda i: (nbc - 1 - i, 0)), pl.BlockSpec((hp, 1), lambda i: (0, 0))],
        out_shape=[jax.ShapeDtypeStruct((t_dim, LANES), BF16), jax.ShapeDtypeStruct((hp, 1), F32)],
        scratch_shapes=[pltpu.VMEM((hp, 1), F32)], compiler_params=_params("arbitrary"),
    )(dc_t, z, sel, tri_rev, bf_col)

    def rw_post_bwd(dy_ref, y_ref, r_ref, k_ref, v_ref, g_ref, lng_ref, lnb_ref, rk_ref, e_ref,
                    dg_ref, dys_ref, dr_ref, dk_ref, dv_ref, dlng_ref, dlnb_ref, drk_ref):
        _first_step_zero(dlng_ref, dlnb_ref, drk_ref)
        e = e_ref[...]
        dy, r, k2, v, g = dy_ref[...], r_ref[...], k_ref[...], v_ref[...], g_ref[...]
        rstd, yh, rk, yo, sg = rw_post_values(y_ref[...], r, k2, v, g, lng_ref, lnb_ref, rk_ref, e)
        dg_ref[...] = dy * yo * sg * (1.0 + g * (1.0 - sg))
        dyo = dy * g * sg
        dlnb_ref[...] += _colsum(dyo)
        dlng_ref[...] += _colsum(dyo * yh)
        dyh = dyo * lng_ref[...]
        dys_ref[...] = rstd * (dyh - _headsum(dyh, e) * (1.0 / HEAD) - yh * _headsum(dyh * yh, e) * (1.0 / HEAD))
        drk = _headsum(dyo * v, e)
        dv_ref[...] = dyo * rk
        dr_ref[...] = drk * k2 * rk_ref[...]
        dk_ref[...] = drk * r * rk_ref[...]
        drk_ref[...] += _colsum(drk * r * k2)

    dg_rw, dy_s, dr_b, dk_b, dv_b, d_lng, d_lnb, d_rk = _rowcall(
        rw_post_bwd, "rwkv_post_bwd", t_dim, bt, [dy_rw, y_s, r_s, k_s, v_s, g_s], [ln_g, ln_b, r_k, e_head],
        [(c_rw, F32)] * 5, acc_outs=[(1, c_rw)] * 3)
    axis = dict(SHARDED)
    early = {"w_up_rwkv": d_wur, "w_up_fox": d_wuf, "w_out": d_wo, "ple_proj": d_pp, "ple_gate_w": d_pg}
    (dr_c, dw_c, dk_c, dv_c, dkk_c, da_c), early_recv = _scan_bwd(
        r_s, w_s, k_s, v_s, kk_s, a_s, dy_s, s_all, s_fin, 128, pairs_bwd,
        _Exchange([_split_shards(g, axis[n]) for n, g in early.items()]))
    recv = dict(zip(early, early_recv))

    def rw_prep_bwd(z_ref, dr1, dr2, dw_ref, dk1, dk2_ref, dv1, dv2, dkk_ref, da_ref, dg_ref,
                    mu_ref, w0_ref, wl_ref, a0_ref, wa_ref, kk_ref, ka_ref, e_ref,
                    dzs_ref, tw_ref, al_ref, dwr_ref, dar_ref, dmu_ref, dw0_ref, da0_ref, dkk_acc, dka_acc, carry):
        _first_step_zero(carry, dmu_ref, dw0_ref, da0_ref, dkk_acc, dka_acc)
        e = e_ref[...]
        zv = z_ref[...]
        zp = _shift_down(zv, carry)
        zs = zv + (zp - zv) * mu_ref[...]
        k, tw, al, sw, decay, a, kk0, nrm, inv, k2 = rw_values(zs, w0_ref, wl_ref, a0_ref, wa_ref, kk_ref, ka_ref, e_ref)
        dk2 = dk1[...] + dk2_ref[...]
        da = da_ref[...] + dk2 * k * ka_ref[...]
        dk = dk2 * (1.0 + (a - 1.0) * ka_ref[...])
        dka_acc[...] += _colsum(dk2 * k * (a - 1.0))
        kk = kk0 * inv
        dkk = dkk_ref[...]
        dkk0 = inv * jnp.where(nrm > 1e-12, dkk - kk * _headsum(dkk * kk, e), dkk)
        dk = dk + dkk0 * kk_ref[...]
        dkk_acc[...] += _colsum(dkk0 * k)
        da_raw = da * a * (1.0 - a)
        da0_ref[...] += _colsum(da_raw)
        dw_raw = dw_ref[...] * decay * (-decay_k) * sw * (1.0 - sw)
        dw0_ref[...] += _colsum(dw_raw)
        dar_b, dwr_b = da_raw.astype(BF16), dw_raw.astype(BF16)
        dal = lax.dot_general(dar_b, wa_ref[...], _DOT_DIMS["nt"], preferred_element_type=F32)
        dtw = lax.dot_general(dwr_b, wl_ref[...], _DOT_DIMS["nt"], preferred_element_type=F32)
        dzs_ref[:, 0:c_rw] = dr1[...] + dr2[...]
        dzs_ref[:, c_rw:2 * c_rw] = dk
        dzs_ref[:, 2 * c_rw:3 * c_rw] = dv1[...] + dv2[...]
        dzs_ref[:, 3 * c_rw:c4] = dg_ref[...]
        dzs_ref[:, c4:c4 + lora] = dtw * (1.0 - tw * tw)
        dzs_ref[:, c4 + lora:c4 + 2 * lora] = dal
        if sec > c4 + 2 * lora:
            dzs_ref[:, c4 + 2 * lora:sec] = jnp.zeros((zv.shape[0], sec - c4 - 2 * lora), F32)
        tw_ref[...] = tw.astype(BF16)
        al_ref[...] = al.astype(BF16)
        dwr_ref[...] = dwr_b
        dar_ref[...] = dar_b
        dmu_ref[...] += _colsum(dzs_ref[...] * (zp - zv))

    dzs, tw_b, al_b, dwr_b, dar_b, d_mu, d_w0, d_a0, d_kk, d_ka = _rowcall(
        rw_prep_bwd, "rwkv_prep_bwd", t_dim, bt_many,
        [(z, 0, sec), dr_c, dr_b, dw_c, dk_c, dk_b, dv_c, dv_b, dkk_c, da_c, dg_rw], rw_consts,
        [(sec, F32), (lora, BF16), (lora, BF16), (c_rw, BF16), (c_rw, BF16)],
        acc_outs=[(1, sec), (1, c_rw), (1, c_rw), (1, c_rw), (1, c_rw)], scratch=[pltpu.VMEM((1, sec), F32)])
    d_wl = _matmul(tw_b, dwr_b, "tn", "d_w_lora", out_dtype=BF16)
    d_wa = _matmul(al_b, dar_b, "tn", "d_a_lora", out_dtype=BF16)

    def shift_bwd(dzs_ref, mu_ref, dz_ref, carry):
        _first_step_zero(carry)
        d = dzs_ref[...]
        nbt = d.shape[0]
        nxt = pltpu.roll(d, nbt - 1, 0)
        rowi = lax.broadcasted_iota(jnp.int32, d.shape, 0)
        nxt = jnp.where(rowi == nbt - 1, carry[...], nxt)
        carry[...] = d[0:1, :]
        m = mu_ref[...]
        dz_ref[...] = (d * (1.0 - m) + nxt * m).astype(BF16)

    (dz_rw,) = _rowcall(shift_bwd, "shift_bwd", t_dim, bt, [dzs], [mu], [(sec, BF16)],
                        scratch=[pltpu.VMEM((1, sec), F32)], reverse=True)

    fox_parts = [dq_f.astype(BF16), dk_f.astype(BF16), dv_f.astype(BF16), dg_fox, dz_fl]
    if sec > 4 * c_fox + LANES:
        fox_parts.append(jnp.zeros((t_dim, sec - 4 * c_fox - LANES), BF16))
    dz = jnp.concatenate([dz_rw] + fox_parts + [dz_gate], axis=1)
    d_wz = _matmul(dz, h, "tn", "d_w_in", out_dtype=BF16, bm=1408)
    rw_cols, fox_cols = c4 + 2 * lora, 4 * c_fox + h_fox
    d_wi = jnp.concatenate([d_wz[:rw_cols], d_wz[sec:sec + fox_cols], d_wz[2 * sec:2 * sec + 2 * d_model]], axis=0)
    parts = d_wi.reshape(NDEV, -1, d_model)
    mine = lax.dynamic_index_in_dim(parts.reshape((NCHIP, 2) + parts.shape[1:]), lax.axis_index("c"), 1, keepdims=False)
    chip_parts = _add_pairs(mine, _sibling_swap(parts, "swap_w_in"), "add_w_in")
    dh, (recv["w_in"],) = _matmul(dz, wz, "nt", "d_h", bk=sec, exchange=_ChipExchange([chip_parts]))
    late_parts = [_split_shards(d_wl, 1), _split_shards(d_wa, 1)]

    def norm_in_bwd(dh_ref, x_ref, dx1_ref, g_ref, dx_ref, dg1_ref):
        _first_step_zero(dg1_ref)
        xv = x_ref[...]
        rms = lax.rsqrt(jnp.mean(xv * xv, axis=-1, keepdims=True) + NORM_EPS)
        xn = xv * rms
        d = dh_ref[...]
        dg1_ref[...] += _colsum(d * xn)
        gy = d * g_ref[...]
        dx_ref[...] = dx1_ref[...] + rms * (gy - xn * jnp.mean(xn * gy, axis=-1, keepdims=True))

    dx, d_g1 = _rowcall(norm_in_bwd, "norm_in_bwd", t_dim, bt, [dh, x, dx1], [norm_g], [(d_model, F32)],
                        acc_outs=[(1, d_model)])

    rep_grads = {"norm_g": d_g1, "rw_shift_mu": d_mu[:, 0:c4 + 2 * lora], "rw_w0": d_w0, "rw_a0": d_a0, "rw_k_k": d_kk,
                 "rw_k_a": d_ka, "rw_r_k": d_rk, "rw_ln_g": d_lng, "rw_ln_b": d_lnb, "fox_b_f": d_bf[0:h_fox, 0],
                 "ple_norm_g": d_g2, "final_norm_g": d_g3}
    return loss[0, 0], dx, recv, late_parts, rep_grads


def kernel(x, p, norm_g, w_in, rw_shift_mu, rw_w0, rw_w_lora_up, rw_a0, rw_a_lora_up, rw_k_k, rw_k_a, rw_r_k, rw_ln_g, rw_ln_b, fox_b_f, w_up_rwkv, w_up_fox, w_out, ple_proj, ple_gate_w, ple_norm_g, final_norm_g, loss_target, m_norm_g, m_w_in, m_rw_shift_mu, m_rw_w0, m_rw_w_lora_up, m_rw_a0, m_rw_a_lora_up, m_rw_k_k, m_rw_k_a, m_rw_r_k, m_rw_ln_g, m_rw_ln_b, m_fox_b_f, m_w_up_rwkv, m_w_up_fox, m_w_out, m_ple_proj, m_ple_gate_w, m_ple_norm_g, m_final_norm_g, v_norm_g, v_w_in, v_rw_shift_mu, v_rw_w0, v_rw_w_lora_up, v_rw_a0, v_rw_a_lora_up, v_rw_k_k, v_rw_k_a, v_rw_r_k, v_rw_ln_g, v_rw_ln_b, v_fox_b_f, v_w_up_rwkv, v_w_up_fox, v_w_out, v_ple_proj, v_ple_gate_w, v_ple_norm_g, v_final_norm_g):
    args = locals()
    w = {n: args[n] for n in WEIGHTS}
    mom = {n: args["m_" + n] for n in WEIGHTS}
    var = {n: args["v_" + n] for n in WEIGHTS}

    t_dim, d_model = x.shape[1], x.shape[2]
    c_rw, lora = rw_w0.shape[1], rw_w_lora_up.shape[1]
    h_fox = fox_b_f.shape[1]
    c_fox = h_fox * HEAD
    rw_cols, fox_cols, gate_cols = 4 * c_rw + 2 * lora, 4 * c_fox + h_fox, 2 * d_model
    sec = max(rw_cols, 4 * c_fox + LANES, _round_up(gate_cols, LANES))
    assert c_rw % LANES == 0 and c_fox % LANES == 0 and rw_cols % LANES == 0 and h_fox <= LANES and d_model % LANES == 0
    assert rw_a_lora_up.shape[1] == lora and w_in.shape[2] * NDEV == rw_cols + fox_cols + gate_cols

    assert SHARDED[0] == ("w_in", 1)
    (w_in_all,) = _all_gather([w_in[0].astype(BF16)], "gather_w_in")
    other_shards = [w[n][0].astype(BF16) for n, _ in SHARDED[1:]]

    def to_sections(wi):
        pad = lambda a, width: jnp.pad(a, ((0, 0), (0, width - a.shape[1])))
        return jnp.concatenate([pad(wi[:, :rw_cols], sec), pad(wi[:, rw_cols:rw_cols + fox_cols], sec),
                                pad(wi[:, rw_cols + fox_cols:], sec)], axis=1)

    rep = {n: w[n] for n in REPLICATED}
    dims = (t_dim, d_model, c_rw, lora, c_fox, h_fox, sec)
    loss_local, grad_x, recv, late_parts, rep_grads = _local_step(
        x[0], p[0, 0], loss_target[0], to_sections(_join_shards(w_in_all, 1)), other_shards, rep, dims)

    rep_sizes = [w[n].size for n in REPLICATED]
    rep_offs = [sum(rep_sizes[:i]) for i in range(len(rep_sizes))]
    pack_rep = lambda tree: _pack_rows(jnp.concatenate([tree[n].astype(F32).reshape(-1) for n in REPLICATED]))
    small_all, recv["rw_w_lora_up"], recv["rw_a_lora_up"] = _grad_exchange([pack_rep(rep_grads)] + late_parts,
                                                                           "exchange_small")
    kinds = ("grad", "delta", "new_m", "new_v")
    outs = {}
    for n, _ in SHARDED:
        if n == "w_in":
            update = _adamw(recv[n], w_in[0].T, m_w_in[0].T, v_w_in[0].T, "adamw_" + n)
            update = [jnp.transpose(buf, (0, 2, 1)) for buf in update]
        else:
            update = _adamw(recv[n], w[n][0], mom[n][0], var[n][0], "adamw_" + n)
        for kind, buf in zip(kinds, update):
            outs[kind, n] = buf
    for kind, buf in zip(kinds, _adamw(small_all, pack_rep(w), pack_rep(mom), pack_rep(var), "adamw_replicated")):
        flat = buf.reshape(-1)
        for n, o, s in zip(REPLICATED, rep_offs, rep_sizes):
            outs[kind, n] = flat[o:o + s].reshape(w[n].shape)
    loss = lax.psum(loss_local, MESH_AXES)
    return (loss, grad_x[None], *[outs[kind, n] for kind in ("grad", "delta", "new_m", "new_v") for n in WEIGHTS])
```

```python
import functools
import math

import jax
import jax.numpy as jnp
from jax import lax
from jax.experimental import pallas as pl
from jax.experimental.pallas import tpu as pltpu

F32, BF16 = jnp.float32, jnp.bfloat16
HI = lax.Precision.HIGHEST
LANES = 128
SUBLANES = 8
HEAD = 64
NORM_EPS = 1e-6
GN_EPS = 64e-5
VMEM_LIMIT = 56 * 1024 * 1024
NDEV = 8
PACK_W = 1024
PACK_ALIGN = 16 * PACK_W
MESH_AXES = ("x", "y", "c")
MESH = pl.DeviceIdType.MESH

ADAM_LR, ADAM_B1, ADAM_B2, ADAM_EPS, ADAM_WD, ADAM_STEP = 0.001, 0.9, 0.999, 1e-08, 0.01, 10


def _round_up(n, m):
    return (n + m - 1) // m * m


def _pick(dim, pref, align=LANES):
    if dim <= pref:
        return dim
    best = None
    for cand in range(align, pref + 1, align):
        if dim % cand == 0:
            best = cand
    return dim if best is None else best


def _params(*sem):
    return pltpu.CompilerParams(dimension_semantics=sem, vmem_limit_bytes=VMEM_LIMIT)


def _sigmoid(v):
    return jax.nn.sigmoid(v)


def _log_sigmoid(v):
    return jnp.minimum(v, 0.0) - jnp.log(1.0 + jnp.exp(-jnp.abs(v)))


_DOT_DIMS = {"nn": (((1,), (0,)), ((), ())), "nt": (((1,), (1,)), ((), ())), "tn": (((0,), (0,)), ((), ()))}


def _matmul(a, b, mode, name, out_dtype=F32, bm=512, bn=1024, bk=2048, exchange=None):
    if mode == "tn":
        k_dim, m_dim = a.shape
    else:
        m_dim, k_dim = a.shape
    n_dim = b.shape[0] if mode == "nt" else b.shape[1]
    bm, bn, bk = _pick(m_dim, bm), _pick(n_dim, bn), _pick(k_dim, bk)
    nk = k_dim // bk

    nx = 0 if exchange is None else exchange.n
    grid = (m_dim // bm, n_dim // bn, nk)

    def body(*refs):
        a_ref, b_ref, o_ref = refs[0], refs[1], refs[2 + nx]
        step = [pl.program_id(d) for d in range(3)]
        if nx:
            own_scratch = 1 if nk > 1 else 0
            ex_refs = (refs[2:2 + nx], refs[3 + nx:3 + 2 * nx], refs[3 + 2 * nx + own_scratch:])
            linear = (step[0] * grid[1] + step[1]) * grid[2] + step[2]

            @pl.when(linear == 0)
            def _():
                exchange.start(*ex_refs)

            if hasattr(exchange, "forward"):
                @pl.when(linear == (grid[0] * grid[1] * grid[2]) // 2)
                def _():
                    exchange.forward(*ex_refs)

        prod = lax.dot_general(a_ref[...].astype(BF16), b_ref[...].astype(BF16), _DOT_DIMS[mode],
                               preferred_element_type=F32)
        if nk == 1:
            o_ref[...] = prod.astype(o_ref.dtype)
        else:
            acc_ref, k = refs[3 + 2 * nx], step[2]

            @pl.when(k == 0)
            def _():
                acc_ref[...] = prod

            @pl.when(k > 0)
            def _():
                acc_ref[...] += prod

            @pl.when(k == nk - 1)
            def _():
                o_ref[...] = acc_ref[...].astype(o_ref.dtype)
        if nx:
            @pl.when(linear == grid[0] * grid[1] * grid[2] - 1)
            def _():
                exchange.finish(*ex_refs)

    if mode == "tn":
        a_spec = pl.BlockSpec((bk, bm), lambda i, j, k: (k, i))
    else:
        a_spec = pl.BlockSpec((bm, bk), lambda i, j, k: (i, k))
    if mode == "nt":
        b_spec = pl.BlockSpec((bn, bk), lambda i, j, k: (j, k))
    else:
        b_spec = pl.BlockSpec((bk, bn), lambda i, j, k: (k, j))
    out_spec = pl.BlockSpec((bm, bn), lambda i, j, k: (i, j))
    out_shape = jax.ShapeDtypeStruct((m_dim, n_dim), out_dtype)
    acc_scratch = [pltpu.VMEM((bm, bn), F32)] if nk > 1 else []
    if exchange is None:
        return pl.pallas_call(
            body, name=name, grid=grid, in_specs=[a_spec, b_spec], out_specs=out_spec, out_shape=out_shape,
            scratch_shapes=acc_scratch, compiler_params=_params("parallel", "parallel", "arbitrary"),
        )(a, b)
    outs = pl.pallas_call(
        body, name=name, grid=grid, in_specs=[a_spec, b_spec] + [HBM_SPEC] * nx,
        out_specs=[out_spec] + [HBM_SPEC] * nx, out_shape=[out_shape] + exchange.out_shape,
        scratch_shapes=acc_scratch + exchange.scratch, compiler_params=_params("arbitrary", "arbitrary", "arbitrary"),
    )(a, b, *exchange.arrays)
    return outs[0], outs[1:]


def _rowcall(body, name, t_dim, bt, row_ins, const_ins, row_outs, acc_outs=(), scratch=(), reverse=False):
    nt = t_dim // bt

    def rmap(i):
        return nt - 1 - i if reverse else i

    in_specs, args = [], []
    for item in row_ins:
        arr, cb, w = item if isinstance(item, tuple) else (item, 0, item.shape[1])
        in_specs.append(pl.BlockSpec((bt, w), lambda i, cb=cb: (rmap(i), cb)))
        args.append(arr)
    for arr in const_ins:
        in_specs.append(pl.BlockSpec(arr.shape, lambda i, nd=arr.ndim: (0,) * nd))
        args.append(arr)
    out_specs = [pl.BlockSpec((bt, w), lambda i: (rmap(i), 0)) for w, _ in row_outs]
    out_shape = [jax.ShapeDtypeStruct((t_dim, w), dt) for w, dt in row_outs]
    for shp in acc_outs:
        out_specs.append(pl.BlockSpec(shp, lambda i, nd=len(shp): (0,) * nd))
        out_shape.append(jax.ShapeDtypeStruct(shp, F32))
    return pl.pallas_call(
        body, name=name, grid=(nt,), in_specs=in_specs, out_specs=out_specs, out_shape=out_shape,
        scratch_shapes=list(scratch), compiler_params=_params("arbitrary"),
    )(*args)


def _first_step_zero(*refs):
    @pl.when(pl.program_id(0) == 0)
    def _():
        for r in refs:
            r[...] = jnp.zeros_like(r)


def _colsum(v):
    return jnp.sum(v, axis=0, keepdims=True)


def _headsum(v, e):
    parts = [jnp.dot(v[:, p * LANES:(p + 1) * LANES], e, precision=HI, preferred_element_type=F32)
             for p in range(v.shape[1] // LANES)]
    return parts[0] if len(parts) == 1 else jnp.concatenate(parts, axis=1)


def _shift_down(v, carry_ref):
    bt = v.shape[0]
    prev = pltpu.roll(v, 1, 0)
    row = lax.broadcasted_iota(jnp.int32, v.shape, 0)
    prev = jnp.where(row == 0, carry_ref[...], prev)
    carry_ref[...] = v[bt - 1:bt, :]
    return prev


def _pair_consts():
    lane = lax.broadcasted_iota(jnp.int32, (1, LANES), 1)
    m0 = (lane < HEAD).astype(F32)
    m1 = 1.0 - m0
    sub = lax.broadcasted_iota(jnp.int32, (HEAD, LANES), 0)
    lane2 = lax.broadcasted_iota(jnp.int32, (HEAD, LANES), 1)
    i0 = (lane2 == sub).astype(F32)
    i1 = (lane2 == sub + HEAD).astype(F32)
    return m0, m1, i0, i1


def _head_masks():
    lane = lax.broadcasted_iota(jnp.int32, (1, LANES), 1)
    first = (lane < HEAD).astype(F32)
    return first, 1.0 - first


def _head_ones(dtype=BF16):
    lane = jnp.arange(LANES)
    return (lane[:, None] // HEAD == lane[None, :] // HEAD).astype(dtype)


def _lanesum(v):
    return jnp.sum(v, axis=1, keepdims=True)


def _split_bf16(v):
    hi = v.astype(BF16).astype(F32)
    rest = v - hi
    mid = rest.astype(BF16).astype(F32)
    return hi, mid, (rest - mid).astype(BF16).astype(F32)


def _col_pair(parts, j, i01, e_bf16):
    lhs = jnp.concatenate([(part[j:j + 1] * i01).astype(BF16) for part in parts], axis=0)
    out = jnp.dot(lhs, e_bf16, preferred_element_type=F32)
    return (out[0:HEAD] + out[HEAD:2 * HEAD]) + out[2 * HEAD:3 * HEAD]


def _row_pair(c0, c1, i0, i1):
    return _colsum(c0 * i0 + c1 * i1)


def _scan_fwd(r, w, k, v, kk, a, tc, npb):
    t_dim, c_dim = r.shape
    wb = LANES * npb
    tc = _pick(t_dim, tc, SUBLANES)

    def body(r_ref, w_ref, k_ref, v_ref, kk_ref, a_ref, e_ref, y_ref, sall_ref, sfin_ref, s_ref):
        @pl.when(pl.program_id(1) == 0)
        def _():
            s_ref[...] = jnp.zeros_like(s_ref)

        m0, m1, i0, i1 = _pair_consts()
        i01 = i0 + i1
        e = e_ref[...]
        sub8 = lax.broadcasted_iota(jnp.int32, (SUBLANES, LANES), 0)
        lanes = [slice(q * LANES, (q + 1) * LANES) for q in range(npb)]
        ng = tc // SUBLANES

        def lanesums(tiles):
            sums = _lanesum(jnp.concatenate(tiles, axis=0))
            return [sums[i * HEAD:(i + 1) * HEAD] for i in range(len(tiles))]

        def halves(s, row):
            return [s * (row * m0), s * (row * m1)]

        def group(gi, carry):
            base = pl.multiple_of(gi * SUBLANES, SUBLANES)
            rows = pl.ds(base, SUBLANES)
            s = list(carry)
            r8 = [r_ref[rows, sl] for sl in lanes]
            w8 = [w_ref[rows, sl] for sl in lanes]
            k8 = [k_ref[rows, sl] for sl in lanes]
            kk8 = [kk_ref[rows, sl] for sl in lanes]
            b8 = [kk8[q] * a_ref[rows, lanes[q]] for q in range(npb)]
            v8 = [_split_bf16(v_ref[rows, sl]) for sl in lanes]
            y8 = [jnp.zeros((SUBLANES, LANES), F32)] * npb
            for j in range(SUBLANES + 1):
                one, before = slice(j, j + 1), slice(j - 1, j)
                tiles = []
                for q in range(npb):
                    if j < SUBLANES:
                        tiles += halves(s[q], kk8[q][one])
                    if j > 0:
                        tiles += halves(s[q], r8[q][before])
                cols = lanesums(tiles)
                per = len(tiles) // npb
                for q in range(npb):
                    mine = cols[q * per:(q + 1) * per]
                    if j > 0:
                        y8[q] = jnp.where(sub8 == j - 1, _row_pair(mine[-2], mine[-1], i0, i1), y8[q])
                    if j < SUBLANES:
                        sall_ref[base + j, :, lanes[q]] = s[q]
                        sb = mine[0] * m0 + mine[1] * m1
                        s[q] = s[q] * w8[q][one] - sb * b8[q][one] + _col_pair(v8[q], j, i01, e) * k8[q][one]
            for q in range(npb):
                y_ref[rows, lanes[q]] = y8[q]
            return tuple(s)

        init = tuple(s_ref[:, q * LANES:(q + 1) * LANES] for q in range(npb))
        fin = lax.fori_loop(0, ng, group, init)
        for q in range(npb):
            s_ref[:, q * LANES:(q + 1) * LANES] = fin[q]
            sfin_ref[:, q * LANES:(q + 1) * LANES] = fin[q]

    row = pl.BlockSpec((tc, wb), lambda p, c: (c, p))
    return pl.pallas_call(
        body, name="rwkv_scan_fwd", grid=(c_dim // wb, t_dim // tc),
        in_specs=[row] * 6 + [pl.BlockSpec((LANES, LANES), lambda p, c: (0, 0))],
        out_specs=[row, pl.BlockSpec((tc, HEAD, wb), lambda p, c: (c, 0, p)), pl.BlockSpec((HEAD, wb), lambda p, c: (0, p))],
        out_shape=[jax.ShapeDtypeStruct((t_dim, c_dim), F32), jax.ShapeDtypeStruct((t_dim, HEAD, c_dim), F32),
                   jax.ShapeDtypeStruct((HEAD, c_dim), F32)],
        scratch_shapes=[pltpu.VMEM((HEAD, wb), F32)],
        compiler_params=_params("parallel", "arbitrary"),
    )(r, w, k, v, kk, a, _head_ones())


def _scan_bwd(r, w, k, v, kk, a, dy, sall, sfin, tc, npb, exchange):
    t_dim, c_dim = r.shape
    wb = LANES * npb
    tc = _pick(t_dim, tc, SUBLANES)
    nc = t_dim // tc
    nx = exchange.n
    n_blocks = c_dim // wb

    def body(*refs):
        r_ref, w_ref, k_ref, v_ref, kk_ref, a_ref, dy_ref, sall_ref, sfin_ref, e_ref = refs[:10]
        dr_ref, dw_ref, dk_ref, dv_ref, dkk_ref, da_ref = refs[10 + nx:16 + nx]
        ds_ref, sn_ref = refs[16 + 2 * nx:18 + 2 * nx]
        here_p, here_c = pl.program_id(0), pl.program_id(1)
        ex_refs = (refs[10:10 + nx], refs[16 + nx:16 + 2 * nx], refs[18 + 2 * nx:])

        @pl.when((here_p == 0) & (here_c == 0))
        def _():
            exchange.start(*ex_refs)

        @pl.when(pl.program_id(1) == 0)
        def _():
            ds_ref[...] = jnp.zeros_like(ds_ref)
            sn_ref[...] = sfin_ref[...]

        m0, m1, i0, i1 = _pair_consts()
        i01 = i0 + i1
        e = e_ref[...]
        sub8 = lax.broadcasted_iota(jnp.int32, (SUBLANES, LANES), 0)
        lanes = [slice(q * LANES, (q + 1) * LANES) for q in range(npb)]
        ng = tc // SUBLANES

        def halves(s, row):
            return [s * (row * m0), s * (row * m1)]

        def group(gi, carry):
            base = pl.multiple_of((ng - 1 - gi) * SUBLANES, SUBLANES)
            rows = pl.ds(base, SUBLANES)
            ds = list(carry)
            r8 = [r_ref[rows, sl] for sl in lanes]
            w8 = [w_ref[rows, sl] for sl in lanes]
            k8 = [k_ref[rows, sl] for sl in lanes]
            kk8 = [kk_ref[rows, sl] for sl in lanes]
            a8 = [a_ref[rows, sl] for sl in lanes]
            v8 = [_split_bf16(v_ref[rows, sl]) for sl in lanes]
            dy8 = [_split_bf16(dy_ref[rows, sl]) for sl in lanes]
            zero8 = jnp.zeros((SUBLANES, LANES), F32)
            dr8, dw8, dk8, dv8, dkk8, da8 = ([zero8] * npb for _ in range(6))
            for j in reversed(range(SUBLANES)):
                one = slice(j, j + 1)
                here = sub8 == j
                d, s_prev, tiles = [], [], []
                for q in range(npb):
                    dyb = _col_pair(dy8[q], j, i01, e)
                    s_prev.append(sall_ref[base + j, :, lanes[q]])
                    dr8[q] = jnp.where(here, _colsum(sn_ref[:, lanes[q]] * dyb), dr8[q])
                    sn_ref[:, lanes[q]] = s_prev[q]
                    d.append(ds[q] + dyb * r8[q][one])
                for q in range(npb):
                    tiles += halves(d[q], kk8[q][one] * a8[q][one])
                for q in range(npb):
                    tiles += halves(d[q], k8[q][one]) + halves(s_prev[q], kk8[q][one])
                sums = _lanesum(jnp.concatenate(tiles, axis=0))
                cols = [sums[i * HEAD:(i + 1) * HEAD] for i in range(len(tiles))]
                for q in range(npb):
                    kkr, ar = kk8[q][one], a8[q][one]
                    dsb = -(cols[2 * q] * m0 + cols[2 * q + 1] * m1)
                    ds[q] = d[q] * w8[q][one] + dsb * kkr
                    rest = cols[2 * npb + 4 * q:2 * npb + 4 * q + 4]
                    dv8[q] = jnp.where(here, _row_pair(rest[0], rest[1], i0, i1), dv8[q])
                    sb = rest[2] * m0 + rest[3] * m1
                    db = -_colsum(d[q] * sb)
                    dk8[q] = jnp.where(here, _colsum(d[q] * _col_pair(v8[q], j, i01, e)), dk8[q])
                    dw8[q] = jnp.where(here, _colsum(d[q] * s_prev[q]), dw8[q])
                    dkk8[q] = jnp.where(here, _colsum(s_prev[q] * dsb) + db * ar, dkk8[q])
                    da8[q] = jnp.where(here, db * kkr, da8[q])
            for q in range(npb):
                sl = lanes[q]
                dr_ref[rows, sl], dw_ref[rows, sl], dk_ref[rows, sl] = dr8[q], dw8[q], dk8[q]
                dv_ref[rows, sl], dkk_ref[rows, sl], da_ref[rows, sl] = dv8[q], dkk8[q], da8[q]
            return tuple(ds)

        init = tuple(ds_ref[:, q * LANES:(q + 1) * LANES] for q in range(npb))
        fin = lax.fori_loop(0, ng, group, init)
        for q in range(npb):
            ds_ref[:, q * LANES:(q + 1) * LANES] = fin[q]

        @pl.when((here_p == n_blocks - 1) & (here_c == nc - 1))
        def _():
            exchange.finish(*ex_refs)

    row = pl.BlockSpec((tc, wb), lambda p, c: (nc - 1 - c, p))
    outs = pl.pallas_call(
        body, name="rwkv_scan_bwd", grid=(n_blocks, nc),
        in_specs=[row] * 7 + [pl.BlockSpec((tc, HEAD, wb), lambda p, c: (nc - 1 - c, 0, p)),
                              pl.BlockSpec((HEAD, wb), lambda p, c: (0, p)),
                              pl.BlockSpec((LANES, LANES), lambda p, c: (0, 0))] + [HBM_SPEC] * nx,
        out_specs=[row] * 6 + [HBM_SPEC] * nx,
        out_shape=[jax.ShapeDtypeStruct((t_dim, c_dim), F32)] * 6 + exchange.out_shape,
        scratch_shapes=[pltpu.VMEM((HEAD, wb), F32)] * 2 + exchange.scratch,
        compiler_params=_params("arbitrary", "arbitrary"),
    )(r, w, k, v, kk, a, dy, sall, sfin, _head_ones(), *exchange.arrays)
    return outs[:6], outs[6:]


def _fox_fwd(z, ct, cq, q_cb, k_cb, v_cb, n_pairs, blk):
    t_dim = z.shape[0]
    blk = _pick(t_dim, blk)
    nq = t_dim // blk
    scale = HEAD ** -0.5

    def body(q_ref, k_ref, v_ref, ct_ref, cq_ref, o_ref, lse_ref):
        i = pl.program_id(1)
        rowi = lax.broadcasted_iota(jnp.int32, (blk, blk), 0)
        coli = lax.broadcasted_iota(jnp.int32, (blk, blk), 1)
        masks = _head_masks()
        qv = q_ref[...]
        qs = [(qv * mk).astype(BF16) for mk in masks]
        cqs = [cq_ref[:, hh:hh + 1] for hh in range(2)]

        def kv_step(j, carry, masked):
            rows = pl.ds(pl.multiple_of(j * blk, blk), blk)
            kb = k_ref[rows, :].astype(BF16)
            vv = v_ref[rows, :]
            stats, acc = list(carry[:4]), carry[4]
            rescale, add = 0.0, 0.0
            for hh in range(2):
                m, l = stats[2 * hh], stats[2 * hh + 1]
                s = (lax.dot_general(qs[hh], kb, _DOT_DIMS["nt"], preferred_element_type=F32) * scale
                     + (cqs[hh] - ct_ref[hh:hh + 1, rows]))
                if masked:
                    s = jnp.where(rowi >= coli, s, -jnp.inf)
                m_new = jnp.maximum(m, jnp.max(s, axis=1, keepdims=True))
                alpha = jnp.exp(m - m_new)
                pr = jnp.exp(s - m_new)
                stats[2 * hh], stats[2 * hh + 1] = m_new, l * alpha + jnp.sum(pr, axis=1, keepdims=True)
                rescale = rescale + alpha * masks[hh]
                hi = pr.astype(BF16)
                both = jnp.concatenate([hi, (pr - hi.astype(F32)).astype(BF16)], axis=1)
                vh = (vv * masks[hh]).astype(BF16)
                add = add + jnp.dot(both, jnp.concatenate([vh, vh], axis=0), preferred_element_type=F32)
            return (*stats, acc * rescale + add)

        neg, zero = jnp.full((blk, 1), -jnp.inf, F32), jnp.zeros((blk, 1), F32)
        carry = lax.fori_loop(0, i, functools.partial(kv_step, masked=False),
                              (neg, zero, neg, zero, jnp.zeros((blk, LANES), F32)))
        m0, l0, m1, l1, acc = kv_step(i, carry, True)
        o_ref[...] = acc * (masks[0] / l0 + masks[1] / l1)
        lse_ref[:, 0:1] = m0 + jnp.log(l0)
        lse_ref[:, 1:2] = m1 + jnp.log(l1)

    full = lambda cb: pl.BlockSpec((t_dim, LANES), lambda p, i, cb=cb: (0, cb + p))
    return pl.pallas_call(
        body, name="fox_attn_fwd", grid=(n_pairs, nq),
        in_specs=[pl.BlockSpec((blk, LANES), lambda p, i: (i, q_cb + p)), full(k_cb), full(v_cb),
                  pl.BlockSpec((None, SUBLANES, t_dim), lambda p, i: (p, 0, 0)),
                  pl.BlockSpec((None, blk, 2), lambda p, i: (p, i, 0))],
        out_specs=[pl.BlockSpec((blk, LANES), lambda p, i: (i, p)), pl.BlockSpec((None, blk, 2), lambda p, i: (p, i, 0))],
        out_shape=[jax.ShapeDtypeStruct((t_dim, n_pairs * LANES), F32), jax.ShapeDtypeStruct((n_pairs, t_dim, 2), F32)],
        compiler_params=_params("parallel", "arbitrary"),
    )(z, z, z, ct, cq)


def _fox_bwd(z, ct, cq, rowdot, do, lse, q_cb, k_cb, v_cb, n_pairs, blk):
    t_dim = z.shape[0]
    blk = _pick(t_dim, blk)
    nb = t_dim // blk
    scale = HEAD ** -0.5

    def body(q_ref, k_ref, v_ref, ct_ref, cq_ref, rd_ref, do_ref, lse_ref, dq_ref, dk_ref, dv_ref, dc_ref):
        j = pl.program_id(1)

        @pl.when(j == 0)
        def _():
            dq_ref[...] = jnp.zeros_like(dq_ref)

        rowi = lax.broadcasted_iota(jnp.int32, (blk, blk), 0)
        coli = lax.broadcasted_iota(jnp.int32, (blk, blk), 1)
        krows = pl.ds(pl.multiple_of(j * blk, blk), blk)
        masks = _head_masks()
        kv, vb = k_ref[...], v_ref[...].astype(BF16)
        kb = kv.astype(BF16)
        ks = [(kv * mk).astype(BF16) for mk in masks]
        cks = [ct_ref[hh:hh + 1, krows] for hh in range(2)]

        def q_step(i, carry, masked):
            dk, dv, dcs = carry[0], carry[1], list(carry[2:])
            rows = pl.ds(pl.multiple_of(i * blk, blk), blk)
            qv, dov = q_ref[rows, :], do_ref[rows, :]
            dq = 0.0
            for hh in range(2):
                qh = (qv * masks[hh]).astype(BF16)
                doh = (dov * masks[hh]).astype(BF16)
                s = (lax.dot_general(qh, kb, _DOT_DIMS["nt"], preferred_element_type=F32) * scale
                     + (cq_ref[rows, hh:hh + 1] - cks[hh]))
                pr = jnp.exp(s - lse_ref[rows, hh:hh + 1])
                if masked:
                    pr = jnp.where(rowi >= coli, pr, 0.0)
                dv = dv + lax.dot_general(pr.astype(BF16), doh, _DOT_DIMS["tn"], preferred_element_type=F32)
                dp = lax.dot_general(doh, vb, _DOT_DIMS["nt"], preferred_element_type=F32)
                ds = pr * (dp - rd_ref[rows, hh * HEAD:hh * HEAD + 1])
                dsb = ds.astype(BF16)
                dq = dq + jnp.dot(dsb, ks[hh], preferred_element_type=F32)
                dk = dk + lax.dot_general(dsb, qh, _DOT_DIMS["tn"], preferred_element_type=F32)
                dcs[hh] = dcs[hh] - _colsum(ds)
            dq_ref[rows, :] += dq * scale
            return (dk, dv, *dcs)

        zero_row = jnp.zeros((1, blk), F32)
        init = (jnp.zeros((blk, LANES), F32), jnp.zeros((blk, LANES), F32), zero_row, zero_row)
        carry = q_step(j, init, True)
        dk, dv, dc0, dc1 = lax.fori_loop(j + 1, nb, functools.partial(q_step, masked=False), carry)
        dk_ref[...] = dk * scale
        dv_ref[...] = dv
        dc_ref[0:1, :] = dc0
        dc_ref[1:2, :] = dc1
        dc_ref[2:SUBLANES, :] = jnp.zeros((SUBLANES - 2, blk), F32)

    full = lambda: pl.BlockSpec((t_dim, LANES), lambda p, j: (0, p))
    blkspec = pl.BlockSpec((blk, LANES), lambda p, j: (j, p))
    return pl.pallas_call(
        body, name="fox_attn_bwd", grid=(n_pairs, nb),
        in_specs=[pl.BlockSpec((t_dim, LANES), lambda p, j: (0, q_cb + p)),
                  pl.BlockSpec((blk, LANES), lambda p, j: (j, k_cb + p)),
                  pl.BlockSpec((blk, LANES), lambda p, j: (j, v_cb + p)),
                  pl.BlockSpec((None, SUBLANES, t_dim), lambda p, j: (p, 0, 0)),
                  pl.BlockSpec((None, t_dim, 2), lambda p, j: (p, 0, 0)), full(), full(),
                  pl.BlockSpec((None, t_dim, 2), lambda p, j: (p, 0, 0))],
        out_specs=[full(), blkspec, blkspec, pl.BlockSpec((None, SUBLANES, blk), lambda p, j: (p, 0, j))],
        out_shape=[jax.ShapeDtypeStruct((t_dim, n_pairs * LANES), F32)] * 3
                  + [jax.ShapeDtypeStruct((n_pairs, SUBLANES, t_dim), F32)],
        compiler_params=_params("parallel", "arbitrary"),
    )(z, z, z, ct, cq, rowdot, do, lse)


HBM_SPEC = pl.BlockSpec(memory_space=pltpu.HBM)


class _Gather:
    def __init__(self, shards):
        self.arrays = list(shards)
        self.n = len(self.arrays)
        self.out_shape = [jax.ShapeDtypeStruct((NDEV,) + s.shape, s.dtype) for s in self.arrays]
        self.scratch = [pltpu.SemaphoreType.DMA((7 * self.n,)), pltpu.SemaphoreType.DMA((7 * self.n,)),
                        pltpu.SemaphoreType.DMA((self.n,))]

    def _plan(self, x_refs, out_refs, sems):
        send_sems, recv_sems, local_sems = sems
        x, y, c = lax.axis_index("x"), lax.axis_index("y"), lax.axis_index("c")
        me, sibling = (x, y, c), (x, y, 1 - c)
        chips = [(1 - x, y), (x, 1 - y), (1 - x, 1 - y)]

        def copy(a, k, block, to, from_input=False):
            px, py, pc = block
            slot = out_refs[a].at[4 * px + 2 * py + pc]
            return pltpu.make_async_remote_copy(
                src_ref=x_refs[a] if from_input else slot, dst_ref=slot,
                send_sem=send_sems.at[7 * a + k], recv_sem=recv_sems.at[7 * a + k], device_id=to, device_id_type=MESH)

        mine = [pltpu.make_async_copy(x_refs[a], out_refs[a].at[4 * x + 2 * y + c], local_sems.at[a])
                for a in range(self.n)]
        first = []
        for a in range(self.n):
            first += [copy(a, 1 + j, me, (*chip, c), from_input=True) for j, chip in enumerate(chips)]
            first.append(copy(a, 0, me, sibling, from_input=True))
        over_ici = [copy(a, 1 + j, (*chip, c), me) for a in range(self.n) for j, chip in enumerate(chips)]
        passed = [copy(a, 4 + j, (*chip, c), sibling) for a in range(self.n) for j, chip in enumerate(chips)]
        from_sibling = []
        for a in range(self.n):
            from_sibling.append(copy(a, 0, sibling, me))
            from_sibling += [copy(a, 4 + j, (*chip, 1 - c), me) for j, chip in enumerate(chips)]
        return mine, first, over_ici, passed, from_sibling

    def start(self, *refs):
        mine, first, _, _, _ = self._plan(*refs)
        for cp in mine + first:
            cp.start()

    def forward(self, *refs):
        _, _, over_ici, passed, _ = self._plan(*refs)
        for arrived, onward in zip(over_ici, passed):
            arrived.wait_recv()
            onward.start()

    def finish(self, *refs):
        mine, first, _, passed, from_sibling = self._plan(*refs)
        for cp in from_sibling:
            cp.wait_recv()
        for cp in first + passed:
            cp.wait_send()
        for cp in mine:
            cp.wait()


def _all_gather(shards, name):
    ga = _Gather(shards)

    def body(*refs):
        parts = (refs[:ga.n], refs[ga.n:2 * ga.n], refs[2 * ga.n:])
        ga.start(*parts)
        ga.forward(*parts)
        ga.finish(*parts)

    return pl.pallas_call(body, name=name, out_shape=ga.out_shape, in_specs=[HBM_SPEC] * ga.n,
                          out_specs=[HBM_SPEC] * ga.n, scratch_shapes=ga.scratch)(*ga.arrays)


class _Exchange:
    def __init__(self, arrays):
        self.arrays = list(arrays)
        self.n = len(self.arrays)
        self.per_dest = [a.ndim == 3 for a in self.arrays]
        self.out_shape = [jax.ShapeDtypeStruct(a.shape if pd else (NDEV,) + a.shape, a.dtype)
                          for a, pd in zip(self.arrays, self.per_dest)]
        self.scratch = [pltpu.SemaphoreType.DMA((7 * self.n,)), pltpu.SemaphoreType.DMA((7 * self.n,)),
                        pltpu.SemaphoreType.DMA((self.n,))]

    def _copies(self, in_refs, out_refs, sems):
        send_sems, recv_sems, local_sems = sems
        x, y, c = lax.axis_index("x"), lax.axis_index("y"), lax.axis_index("c")
        me = 4 * x + 2 * y + c
        own, sends, recvs = [], [], []
        for a in range(self.n):
            mine = in_refs[a].at[me] if self.per_dest[a] else in_refs[a]
            own.append(pltpu.make_async_copy(mine, out_refs[a].at[me], local_sems.at[a]))
            for k in range(1, NDEV):
                px = 1 - x if k & 4 else x
                py = 1 - y if k & 2 else y
                pc = 1 - c if k & 1 else c
                peer = 4 * px + 2 * py + pc
                sem = dict(send_sem=send_sems.at[7 * a + k - 1], recv_sem=recv_sems.at[7 * a + k - 1],
                           device_id=(px, py, pc), device_id_type=MESH)
                src = in_refs[a].at[peer] if self.per_dest[a] else in_refs[a]
                sends.append(pltpu.make_async_remote_copy(src_ref=src, dst_ref=out_refs[a].at[me], **sem))
                recvs.append(pltpu.make_async_remote_copy(src_ref=src, dst_ref=out_refs[a].at[peer], **sem))
        return own, sends, recvs

    def start(self, in_refs, out_refs, sems):
        own, sends, _ = self._copies(in_refs, out_refs, sems)
        for cp in own + sends:
            cp.start()

    def finish(self, in_refs, out_refs, sems):
        own, sends, recvs = self._copies(in_refs, out_refs, sems)
        for cp in recvs:
            cp.wait_recv()
        for cp in sends:
            cp.wait_send()
        for cp in own:
            cp.wait()


def _grad_exchange(arrays, name):
    ex = _Exchange(arrays)

    def body(*refs):
        in_refs, out_refs, sems = refs[:ex.n], refs[ex.n:2 * ex.n], refs[2 * ex.n:]
        ex.start(in_refs, out_refs, sems)
        ex.finish(in_refs, out_refs, sems)

    return pl.pallas_call(body, name=name, out_shape=ex.out_shape, in_specs=[HBM_SPEC] * ex.n,
                          out_specs=[HBM_SPEC] * ex.n, scratch_shapes=ex.scratch)(*ex.arrays)


NCHIP = NDEV // 2


def _sibling_swap(parts, name):
    def body(p_ref, out_ref, send_sems, recv_sems):
        x, y, c = lax.axis_index("x"), lax.axis_index("y"), lax.axis_index("c")
        copies = [pltpu.make_async_remote_copy(
            src_ref=p_ref.at[2 * i + 1 - c], dst_ref=out_ref.at[i], send_sem=send_sems.at[i], recv_sem=recv_sems.at[i],
            device_id=(x, y, 1 - c), device_id_type=MESH) for i in range(NCHIP)]
        for cp in copies:
            cp.start()
        for cp in copies:
            cp.wait_recv()
        for cp in copies:
            cp.wait_send()

    return pl.pallas_call(
        body, name=name, out_shape=jax.ShapeDtypeStruct((NCHIP,) + parts.shape[1:], parts.dtype),
        in_specs=[HBM_SPEC], out_specs=HBM_SPEC,
        scratch_shapes=[pltpu.SemaphoreType.DMA((NCHIP,)), pltpu.SemaphoreType.DMA((NCHIP,))],
    )(parts)


def _add_pairs(a, b, name):
    n, rows, width = a.shape
    br = _pick(rows, 256, 2 * SUBLANES)

    def body(a_ref, b_ref, o_ref):
        o_ref[...] = (a_ref[...].astype(F32) + b_ref[...].astype(F32)).astype(o_ref.dtype)

    blk = pl.BlockSpec((None, br, width), lambda i, j: (i, j, 0))
    return pl.pallas_call(body, name=name, grid=(n, rows // br), in_specs=[blk, blk], out_specs=blk,
                          out_shape=jax.ShapeDtypeStruct(a.shape, a.dtype),
                          compiler_params=_params("parallel", "parallel"))(a, b)


class _ChipExchange:
    FLIPS = ((0, 1), (1, 0), (1, 1))

    def __init__(self, arrays):
        self.arrays = list(arrays)
        self.n = len(self.arrays)
        self.out_shape = [jax.ShapeDtypeStruct(a.shape, a.dtype) for a in self.arrays]
        self.scratch = [pltpu.SemaphoreType.DMA((3 * self.n,)), pltpu.SemaphoreType.DMA((3 * self.n,)),
                        pltpu.SemaphoreType.DMA((self.n,))]

    def _copies(self, in_refs, out_refs, sems):
        send_sems, recv_sems, local_sems = sems
        x, y, c = lax.axis_index("x"), lax.axis_index("y"), lax.axis_index("c")
        here = 2 * x + y
        own, sends, recvs = [], [], []
        for a in range(self.n):
            own.append(pltpu.make_async_copy(in_refs[a].at[here], out_refs[a].at[here], local_sems.at[a]))
            for k, (fx, fy) in enumerate(self.FLIPS):
                px = 1 - x if fx else x
                py = 1 - y if fy else y
                there = 2 * px + py
                sem = dict(send_sem=send_sems.at[3 * a + k], recv_sem=recv_sems.at[3 * a + k],
                           device_id=(px, py, c), device_id_type=MESH)
                src = in_refs[a].at[there]
                sends.append(pltpu.make_async_remote_copy(src_ref=src, dst_ref=out_refs[a].at[here], **sem))
                recvs.append(pltpu.make_async_remote_copy(src_ref=src, dst_ref=out_refs[a].at[there], **sem))
        return own, sends, recvs

    start = _Exchange.start
    finish = _Exchange.finish


def _adamw_body(p_ref, w_ref, m_ref, v_ref, g_out, d_out, m_out, v_out):
    g = p_ref[0].astype(F32)
    for d in range(1, p_ref.shape[0]):
        g = g + p_ref[d].astype(F32)
    mn = ADAM_B1 * m_ref[...] + (1.0 - ADAM_B1) * g
    vn = ADAM_B2 * v_ref[...] + (1.0 - ADAM_B2) * jnp.square(g)
    m_hat = mn / (1.0 - ADAM_B1 ** ADAM_STEP)
    v_hat = vn / (1.0 - ADAM_B2 ** ADAM_STEP)
    g_out[...] = g
    d_out[...] = -ADAM_LR * (m_hat / (jnp.sqrt(v_hat) + ADAM_EPS) + ADAM_WD * w_ref[...])
    m_out[...] = mn
    v_out[...] = vn


def _adamw(partials, w, m, v, name):
    rows, width = w.shape
    n = partials.shape[0]
    body = functools.partial(_adamw_body)
    if rows % (2 * SUBLANES) == 0:
        br = _pick(rows, 128, 2 * SUBLANES)
        grid, shape, index, index3 = rows // br, (br, width), lambda i: (i, 0), lambda i: (0, i, 0)
    else:
        bc = _pick(width, 2 * LANES)
        grid, shape, index, index3 = width // bc, (rows, bc), lambda i: (0, i), lambda i: (0, 0, i)
    blk = pl.BlockSpec(shape, index)
    return pl.pallas_call(
        body, name=name, grid=(grid,),
        in_specs=[pl.BlockSpec((n,) + shape, index3), blk, blk, blk],
        out_specs=[pl.BlockSpec((None,) + shape, index3)] * 4,
        out_shape=[jax.ShapeDtypeStruct((1, rows, width), F32)] * 4,
        compiler_params=_params("parallel"),
    )(partials, w, m, v)


def _pack_rows(flat):
    n = flat.shape[0]
    padded = _round_up(n, PACK_ALIGN)
    return jnp.pad(flat, (0, padded - n)).reshape(padded // PACK_W, PACK_W)


def _split_shards(full, axis):
    rows, cols = full.shape
    if axis == 0:
        return full.reshape(NDEV, rows // NDEV, cols)
    width = cols // NDEV
    return jnp.stack([full[:, d * width:(d + 1) * width] for d in range(NDEV)])


def _join_shards(blocks, axis):
    if axis == 0:
        return blocks.reshape(-1, blocks.shape[2])
    return jnp.concatenate([blocks[d] for d in range(NDEV)], axis=1)


SHARDED = (("w_in", 1), ("rw_w_lora_up", 1), ("rw_a_lora_up", 1), ("w_up_rwkv", 1), ("w_up_fox", 1),
           ("w_out", 0), ("ple_proj", 1), ("ple_gate_w", 0))
REPLICATED = ("norm_g", "rw_shift_mu", "rw_w0", "rw_a0", "rw_k_k", "rw_k_a", "rw_r_k", "rw_ln_g", "rw_ln_b",
              "fox_b_f", "ple_norm_g", "final_norm_g")
WEIGHTS = ("norm_g", "w_in", "rw_shift_mu", "rw_w0", "rw_w_lora_up", "rw_a0", "rw_a_lora_up", "rw_k_k", "rw_k_a",
           "rw_r_k", "rw_ln_g", "rw_ln_b", "fox_b_f", "w_up_rwkv", "w_up_fox", "w_out", "ple_proj", "ple_gate_w",
           "ple_norm_g", "final_norm_g")


def _local_step(x, p, tgt, wz, other_shards, rep, dims):
    t_dim, d_model, c_rw, lora, c_fox, h_fox, sec = dims
    bt = _pick(t_dim, 256, 2 * SUBLANES)
    bt_many = _pick(t_dim, 128, 2 * SUBLANES)
    n_pairs = c_fox // LANES
    row = lambda a: a.reshape(1, -1)
    norm_g, mu, w0, a0 = row(rep["norm_g"]), row(rep["rw_shift_mu"]), row(rep["rw_w0"]), row(rep["rw_a0"])
    k_k, k_a, r_k = row(rep["rw_k_k"]), row(rep["rw_k_a"]), row(rep["rw_r_k"])
    ln_g, ln_b = row(rep["rw_ln_g"]), row(rep["rw_ln_b"])
    g2, g3 = row(rep["ple_norm_g"]), row(rep["final_norm_g"])
    b_f = jnp.pad(row(rep["fox_b_f"]), ((0, 0), (0, LANES - h_fox)))
    e_head = _head_ones(F32)
    c4 = 4 * c_rw
    inv_d = 1.0 / d_model
    decay_k = math.exp(-0.5)

    def norm_in(x_ref, g_ref, h_ref):
        xv = x_ref[...]
        rms = lax.rsqrt(jnp.mean(xv * xv, axis=-1, keepdims=True) + NORM_EPS)
        h_ref[...] = (xv * rms * g_ref[...]).astype(BF16)

    (h,) = _rowcall(norm_in, "norm_in", t_dim, bt, [x], [norm_g], [(d_model, BF16)])
    z, gathered = _matmul(h, wz, "nt", "proj_in", bn=1408, exchange=_Gather(other_shards))
    wl, wa, wur, wuf, wo, pp, pg = [_join_shards(g, ax) for (_, ax), g in zip(SHARDED[1:], gathered)]

    def rw_values(zs, w0_ref, wl_ref, a0_ref, wa_ref, kk_ref, ka_ref, e_ref):
        k = zs[:, c_rw:2 * c_rw]
        tw = jnp.tanh(zs[:, c4:c4 + lora])
        al = zs[:, c4 + lora:c4 + 2 * lora]
        sw = _sigmoid(w0_ref[...] + jnp.dot(tw.astype(BF16), wl_ref[...], preferred_element_type=F32))
        decay = jnp.exp(-decay_k * sw)
        a = _sigmoid(a0_ref[...] + jnp.dot(al.astype(BF16), wa_ref[...], preferred_element_type=F32))
        kk0 = k * kk_ref[...]
        nrm = jnp.sqrt(_headsum(kk0 * kk0, e_ref[...]))
        inv = 1.0 / jnp.maximum(nrm, 1e-12)
        k2 = k * (1.0 + (a - 1.0) * ka_ref[...])
        return k, tw, al, sw, decay, a, kk0, nrm, inv, k2

    def rw_prep(z_ref, mu_ref, w0_ref, wl_ref, a0_ref, wa_ref, kk_ref, ka_ref, e_ref,
                r_o, w_o, k_o, v_o, kk_o, a_o, g_o, carry):
        _first_step_zero(carry)
        zv = z_ref[...]
        zs = zv + (_shift_down(zv, carry) - zv) * mu_ref[...]
        k, tw, al, sw, decay, a, kk0, nrm, inv, k2 = rw_values(zs, w0_ref, wl_ref, a0_ref, wa_ref, kk_ref, ka_ref, e_ref)
        r_o[...] = zs[:, 0:c_rw]
        w_o[...] = decay
        k_o[...] = k2
        v_o[...] = zs[:, 2 * c_rw:3 * c_rw]
        kk_o[...] = kk0 * inv
        a_o[...] = a
        g_o[...] = zs[:, 3 * c_rw:c4]

    rw_consts = [mu, w0, wl, a0, wa, k_k, k_a, e_head]
    r_s, w_s, k_s, v_s, kk_s, a_s, g_s = _rowcall(
        rw_prep, "rwkv_prep", t_dim, bt, [(z, 0, sec)], rw_consts, [(c_rw, F32)] * 7,
        scratch=[pltpu.VMEM((1, sec), F32)])
    pairs_fwd = max(n for n in (1, 2, 4) if c_rw % (n * LANES) == 0)
    pairs_bwd = pairs_fwd
    y_s, s_all, s_fin = _scan_fwd(r_s, w_s, k_s, v_s, kk_s, a_s, 128, pairs_fwd)

    def rw_post_values(y, r, k2, v, g, lng_ref, lnb_ref, rk_ref, e):
        mean = _headsum(y, e) * (1.0 / HEAD)
        d = y - mean
        rstd = lax.rsqrt(_headsum(d * d, e) * (1.0 / HEAD) + GN_EPS)
        yh = d * rstd
        rk = _headsum(r * k2 * rk_ref[...], e)
        yo = yh * lng_ref[...] + lnb_ref[...] + rk * v
        sg = _sigmoid(g)
        return rstd, yh, rk, yo, sg

    def rw_post(y_ref, r_ref, k_ref, v_ref, g_ref, lng_ref, lnb_ref, rk_ref, e_ref, out_ref):
        g = g_ref[...]
        _, _, _, yo, sg = rw_post_values(y_ref[...], r_ref[...], k_ref[...], v_ref[...], g, lng_ref, lnb_ref, rk_ref, e_ref[...])
        out_ref[...] = (yo * g * sg).astype(BF16)

    (y_rw,) = _rowcall(rw_post, "rwkv_post", t_dim, bt, [y_s, r_s, k_s, v_s, g_s], [ln_g, ln_b, r_k, e_head], [(c_rw, BF16)])

    fl_cb = (sec + 4 * c_fox) // LANES
    hp = _round_up(h_fox, SUBLANES)
    bt_c = _pick(t_dim, 256)
    tri = (jnp.arange(bt_c)[:, None] >= jnp.arange(bt_c)[None, :]).astype(F32)

    rows8 = n_pairs * SUBLANES
    pair_rows = (jnp.arange(rows8)[:, None] // SUBLANES * 2 + jnp.arange(rows8)[:, None] % SUBLANES
                 == jnp.arange(LANES)[None, :]) & (jnp.arange(rows8)[:, None] % SUBLANES < 2)
    pair_rows = pair_rows.astype(F32)

    def fox_decay(fl_ref, bf_ref, tri_ref, sel_ref, ct_ref, cq_ref, carry):
        _first_step_zero(carry)
        lf = _log_sigmoid(fl_ref[...] + bf_ref[...])
        c = jnp.dot(tri_ref[...], lf, precision=HI, preferred_element_type=F32) + carry[...]
        carry[...] = c[bt_c - 1:bt_c, :]
        ct = jnp.dot(sel_ref[...], jnp.transpose(c), precision=HI, preferred_element_type=F32)
        ct_ref[...] = ct.reshape(n_pairs, SUBLANES, bt_c)
        for pair in range(n_pairs):
            cq_ref[pair] = c[:, 2 * pair:2 * pair + 2]

    ct, cq = pl.pallas_call(
        fox_decay, name="fox_decay", grid=(t_dim // bt_c,),
        in_specs=[pl.BlockSpec((bt_c, LANES), lambda i: (i, fl_cb)), pl.BlockSpec((1, LANES), lambda i: (0, 0)),
                  pl.BlockSpec((bt_c, bt_c), lambda i: (0, 0)), pl.BlockSpec((rows8, LANES), lambda i: (0, 0))],
        out_specs=[pl.BlockSpec((n_pairs, SUBLANES, bt_c), lambda i: (0, 0, i)),
                   pl.BlockSpec((n_pairs, bt_c, 2), lambda i: (0, i, 0))],
        out_shape=[jax.ShapeDtypeStruct((n_pairs, SUBLANES, t_dim), F32), jax.ShapeDtypeStruct((n_pairs, t_dim, 2), F32)],
        scratch_shapes=[pltpu.VMEM((1, LANES), F32)], compiler_params=_params("arbitrary"),
    )(z, b_f, tri, pair_rows)
    q_cb = sec // LANES
    k_cb, v_cb = q_cb + n_pairs, q_cb + 2 * n_pairs
    o_fox, lse = _fox_fwd(z, ct, cq, q_cb, k_cb, v_cb, n_pairs, 1024)

    def fox_post(o_ref, z_ref, out_ref):
        g = z_ref[:, 3 * c_fox:4 * c_fox]
        out_ref[...] = (o_ref[...] * g * _sigmoid(g)).astype(BF16)

    (y_fox,) = _rowcall(fox_post, "fox_post", t_dim, bt, [o_fox, (z, 1, sec)], [], [(c_fox, BF16)])

    u_rw = _matmul(y_rw, wur, "nn", "up_rwkv")
    u_fox = _matmul(y_fox, wuf, "nn", "up_fox")

    def merge(ur_ref, uf_ref, z_ref, out_ref):
        s1 = _sigmoid(z_ref[:, 0:d_model])
        s2 = _sigmoid(z_ref[:, d_model:2 * d_model])
        out_ref[...] = (s1 * ur_ref[...] + s2 * uf_ref[...]).astype(BF16)

    (merged,) = _rowcall(merge, "merge", t_dim, bt, [u_rw, u_fox, (z, 2, sec)], [], [(d_model, BF16)])
    mo = _matmul(merged, wo, "nn", "proj_out")

    def resid_norm(x_ref, mo_ref, g_ref, x1_ref, n2_ref):
        x1 = x_ref[...] + mo_ref[...]
        rms = lax.rsqrt(jnp.mean(x1 * x1, axis=-1, keepdims=True) + NORM_EPS)
        x1_ref[...] = x1
        n2_ref[...] = (x1 * rms * g_ref[...]).astype(BF16)

    x1, n2 = _rowcall(resid_norm, "resid_norm", t_dim, bt, [x, mo], [g2], [(d_model, F32), (d_model, BF16)])
    ple = _matmul(p, pp, "nn", "ple_proj")
    gl = _matmul(n2, pg, "nn", "ple_gate")

    def head(x1_ref, ple_ref, gl_ref, tgt_ref, g_ref, dx2_ref, dple_ref, dgl_ref, loss_ref, dg3_ref):
        _first_step_zero(loss_ref, dg3_ref)
        sg = _sigmoid(gl_ref[...])
        pl_v = ple_ref[...]
        x2 = x1_ref[...] + pl_v * sg
        rms = lax.rsqrt(jnp.mean(x2 * x2, axis=-1, keepdims=True) + NORM_EPS)
        xn = x2 * rms
        diff = xn * g_ref[...] - tgt_ref[...]
        loss_ref[...] += 0.5 * jnp.sum(jnp.mean(diff * diff, axis=-1, keepdims=True), axis=0, keepdims=True)
        dyf = diff * inv_d
        dg3_ref[...] += _colsum(dyf * xn)
        gy = dyf * g_ref[...]
        dx2 = rms * (gy - xn * jnp.mean(xn * gy, axis=-1, keepdims=True))
        dx2_ref[...] = dx2
        dple_ref[...] = (dx2 * sg).astype(BF16)
        dgl_ref[...] = (dx2 * pl_v * sg * (1.0 - sg)).astype(BF16)

    dx2, dple, dgl, loss, d_g3 = _rowcall(
        head, "head", t_dim, bt, [x1, ple, gl, tgt], [g3], [(d_model, F32), (d_model, BF16), (d_model, BF16)],
        acc_outs=[(1, 1), (1, d_model)])

    d_pp = _matmul(p, dple, "tn", "d_ple_proj", out_dtype=BF16)
    d_pg = _matmul(n2, dgl, "tn", "d_ple_gate", out_dtype=BF16)
    dn2 = _matmul(dgl, pg, "nt", "d_n2")

    def resid_norm_bwd(dx2_ref, dn2_ref, x1_ref, g_ref, dx1_ref, dx1b_ref, dg2_ref):
        _first_step_zero(dg2_ref)
        x1 = x1_ref[...]
        rms = lax.rsqrt(jnp.mean(x1 * x1, axis=-1, keepdims=True) + NORM_EPS)
        xn = x1 * rms
        dn = dn2_ref[...]
        dg2_ref[...] += _colsum(dn * xn)
        gy = dn * g_ref[...]
        dx1 = dx2_ref[...] + rms * (gy - xn * jnp.mean(xn * gy, axis=-1, keepdims=True))
        dx1_ref[...] = dx1
        dx1b_ref[...] = dx1.astype(BF16)

    dx1, dx1b, d_g2 = _rowcall(resid_norm_bwd, "resid_norm_bwd", t_dim, bt, [dx2, dn2, x1], [g2],
                               [(d_model, F32), (d_model, BF16)], acc_outs=[(1, d_model)])
    d_wo = _matmul(merged, dx1b, "tn", "d_w_out", out_dtype=BF16)
    dmerged = _matmul(dx1b, wo, "nt", "d_merged")

    def merge_bwd(dm_ref, ur_ref, uf_ref, z_ref, dur_ref, duf_ref, dzg_ref):
        dm = dm_ref[...]
        s1 = _sigmoid(z_ref[:, 0:d_model])
        s2 = _sigmoid(z_ref[:, d_model:2 * d_model])
        dur_ref[...] = (dm * s1).astype(BF16)
        duf_ref[...] = (dm * s2).astype(BF16)
        dzg_ref[:, 0:d_model] = (dm * ur_ref[...] * s1 * (1.0 - s1)).astype(BF16)
        dzg_ref[:, d_model:2 * d_model] = (dm * uf_ref[...] * s2 * (1.0 - s2)).astype(BF16)
        if sec > 2 * d_model:
            dzg_ref[:, 2 * d_model:sec] = jnp.zeros((dm.shape[0], sec - 2 * d_model), BF16)

    du_rw, du_fox, dz_gate = _rowcall(merge_bwd, "merge_bwd", t_dim, bt, [dmerged, u_rw, u_fox, (z, 2, sec)], [],
                                      [(d_model, BF16), (d_model, BF16), (sec, BF16)])
    d_wur = _matmul(y_rw, du_rw, "tn", "d_w_up_rwkv", out_dtype=BF16)
    d_wuf = _matmul(y_fox, du_fox, "tn", "d_w_up_fox", out_dtype=BF16)
    dy_rw = _matmul(du_rw, wur, "nt", "d_y_rwkv")
    dy_fox = _matmul(du_fox, wuf, "nt", "d_y_fox")

    def fox_post_bwd(dy_ref, o_ref, z_ref, e_ref, do_ref, dg_ref, rd_ref):
        g = z_ref[:, 3 * c_fox:4 * c_fox]
        sg = _sigmoid(g)
        dy, o = dy_ref[...], o_ref[...]
        do = dy * g * sg
        do_ref[...] = do
        dg_ref[...] = (dy * o * sg * (1.0 + g * (1.0 - sg))).astype(BF16)
        rd_ref[...] = _headsum(do.astype(BF16).astype(F32) * o, e_ref[...])

    do_fox, dg_fox, rowdot = _rowcall(fox_post_bwd, "fox_post_bwd", t_dim, bt, [dy_fox, o_fox, (z, 1, sec)], [e_head],
                                      [(c_fox, F32), (c_fox, BF16), (c_fox, F32)])
    dq_f, dk_f, dv_f, dc_t = _fox_bwd(z, ct, cq, rowdot, do_fox, lse, q_cb, k_cb, v_cb, n_pairs, 512)
    sel = (jnp.arange(hp)[:, None] // 2 * SUBLANES + jnp.arange(hp)[:, None] % 2 == jnp.arange(rows8)[None, :]).astype(F32)
    tri_rev = (jnp.arange(bt_c)[:, None] >= jnp.arange(bt_c)[None, :]).astype(F32)
    bf_col = b_f.reshape(LANES, 1)[0:hp]
    nbc = t_dim // bt_c

    def fox_decay_bwd(dc_ref, fl_ref, sel_ref, tri_ref, bf_ref, dfl_ref, dbf_ref, carry):
        _first_step_zero(carry, dbf_ref)
        dc = jnp.dot(sel_ref[...], dc_ref[...].reshape(rows8, bt_c), precision=HI, preferred_element_type=F32)
        dlf = jnp.dot(dc, tri_ref[...], precision=HI, preferred_element_type=F32) + carry[...]
        carry[...] = dlf[:, 0:1]
        flt = jnp.transpose(fl_ref[...])[0:hp, :]
        dfl = dlf * _sigmoid(-(flt + bf_ref[...]))
        head_row = lax.broadcasted_iota(jnp.int32, (hp, bt_c), 0)
        dfl = jnp.where(head_row < h_fox, dfl, 0.0)
        dbf_ref[...] += jnp.sum(dfl, axis=1, keepdims=True)
        full = jnp.concatenate([dfl, jnp.zeros((LANES - hp, bt_c), F32)], axis=0) if hp < LANES else dfl
        dfl_ref[...] = jnp.transpose(full).astype(BF16)

    dz_fl, d_bf = pl.pallas_call(
        fox_decay_bwd, name="fox_decay_bwd", grid=(nbc,),
        in_specs=[pl.BlockSpec((n_pairs, SUBLANES, bt_c), lambda i: (0, 0, nbc - 1 - i)),
                  pl.BlockSpec((bt_c, LANES), lambda i: (nbc - 1 - i, fl_cb)),
                  pl.BlockSpec(sel.shape, lambda i: (0, 0)), pl.BlockSpec((bt_c, bt_c), lambda i: (0, 0)),
                  pl.BlockSpec((hp, 1), lambda i: (0, 0))],
        out_specs=[pl.BlockSpec((bt_c, LANES), lambda i: (nbc - 1 - i, 0)), pl.BlockSpec((hp, 1), lambda i: (0, 0))],
        out_shape=[jax.ShapeDtypeStruct((t_dim, LANES), BF16), jax.ShapeDtypeStruct((hp, 1), F32)],
        scratch_shapes=[pltpu.VMEM((hp, 1), F32)], compiler_params=_params("arbitrary"),
    )(dc_t, z, sel, tri_rev, bf_col)

    def rw_post_bwd(dy_ref, y_ref, r_ref, k_ref, v_ref, g_ref, lng_ref, lnb_ref, rk_ref, e_ref,
                    dg_ref, dys_ref, dr_ref, dk_ref, dv_ref, dlng_ref, dlnb_ref, drk_ref):
        _first_step_zero(dlng_ref, dlnb_ref, drk_ref)
        e = e_ref[...]
        dy, r, k2, v, g = dy_ref[...], r_ref[...], k_ref[...], v_ref[...], g_ref[...]
        rstd, yh, rk, yo, sg = rw_post_values(y_ref[...], r, k2, v, g, lng_ref, lnb_ref, rk_ref, e)
        dg_ref[...] = dy * yo * sg * (1.0 + g * (1.0 - sg))
        dyo = dy * g * sg
        dlnb_ref[...] += _colsum(dyo)
        dlng_ref[...] += _colsum(dyo * yh)
        dyh = dyo * lng_ref[...]
        dys_ref[...] = rstd * (dyh - _headsum(dyh, e) * (1.0 / HEAD) - yh * _headsum(dyh * yh, e) * (1.0 / HEAD))
        drk = _headsum(dyo * v, e)
        dv_ref[...] = dyo * rk
        dr_ref[...] = drk * k2 * rk_ref[...]
        dk_ref[...] = drk * r * rk_ref[...]
        drk_ref[...] += _colsum(drk * r * k2)

    dg_rw, dy_s, dr_b, dk_b, dv_b, d_lng, d_lnb, d_rk = _rowcall(
        rw_post_bwd, "rwkv_post_bwd", t_dim, bt, [dy_rw, y_s, r_s, k_s, v_s, g_s], [ln_g, ln_b, r_k, e_head],
        [(c_rw, F32)] * 5, acc_outs=[(1, c_rw)] * 3)
    axis = dict(SHARDED)
    early = {"w_up_rwkv": d_wur, "w_up_fox": d_wuf, "w_out": d_wo, "ple_proj": d_pp, "ple_gate_w": d_pg}
    (dr_c, dw_c, dk_c, dv_c, dkk_c, da_c), early_recv = _scan_bwd(
        r_s, w_s, k_s, v_s, kk_s, a_s, dy_s, s_all, s_fin, 128, pairs_bwd,
        _Exchange([_split_shards(g, axis[n]) for n, g in early.items()]))
    recv = dict(zip(early, early_recv))

    def rw_prep_bwd(z_ref, dr1, dr2, dw_ref, dk1, dk2_ref, dv1, dv2, dkk_ref, da_ref, dg_ref,
                    mu_ref, w0_ref, wl_ref, a0_ref, wa_ref, kk_ref, ka_ref, e_ref,
                    dzs_ref, tw_ref, al_ref, dwr_ref, dar_ref, dmu_ref, dw0_ref, da0_ref, dkk_acc, dka_acc, carry):
        _first_step_zero(carry, dmu_ref, dw0_ref, da0_ref, dkk_acc, dka_acc)
        e = e_ref[...]
        zv = z_ref[...]
        zp = _shift_down(zv, carry)
        zs = zv + (zp - zv) * mu_ref[...]
        k, tw, al, sw, decay, a, kk0, nrm, inv, k2 = rw_values(zs, w0_ref, wl_ref, a0_ref, wa_ref, kk_ref, ka_ref, e_ref)
        dk2 = dk1[...] + dk2_ref[...]
        da = da_ref[...] + dk2 * k * ka_ref[...]
        dk = dk2 * (1.0 + (a - 1.0) * ka_ref[...])
        dka_acc[...] += _colsum(dk2 * k * (a - 1.0))
        kk = kk0 * inv
        dkk = dkk_ref[...]
        dkk0 = inv * jnp.where(nrm > 1e-12, dkk - kk * _headsum(dkk * kk, e), dkk)
        dk = dk + dkk0 * kk_ref[...]
        dkk_acc[...] += _colsum(dkk0 * k)
        da_raw = da * a * (1.0 - a)
        da0_ref[...] += _colsum(da_raw)
        dw_raw = dw_ref[...] * decay * (-decay_k) * sw * (1.0 - sw)
        dw0_ref[...] += _colsum(dw_raw)
        dar_b, dwr_b = da_raw.astype(BF16), dw_raw.astype(BF16)
        dal = lax.dot_general(dar_b, wa_ref[...], _DOT_DIMS["nt"], preferred_element_type=F32)
        dtw = lax.dot_general(dwr_b, wl_ref[...], _DOT_DIMS["nt"], preferred_element_type=F32)
        dzs_ref[:, 0:c_rw] = dr1[...] + dr2[...]
        dzs_ref[:, c_rw:2 * c_rw] = dk
        dzs_ref[:, 2 * c_rw:3 * c_rw] = dv1[...] + dv2[...]
        dzs_ref[:, 3 * c_rw:c4] = dg_ref[...]
        dzs_ref[:, c4:c4 + lora] = dtw * (1.0 - tw * tw)
        dzs_ref[:, c4 + lora:c4 + 2 * lora] = dal
        if sec > c4 + 2 * lora:
            dzs_ref[:, c4 + 2 * lora:sec] = jnp.zeros((zv.shape[0], sec - c4 - 2 * lora), F32)
        tw_ref[...] = tw.astype(BF16)
        al_ref[...] = al.astype(BF16)
        dwr_ref[...] = dwr_b
        dar_ref[...] = dar_b
        dmu_ref[...] += _colsum(dzs_ref[...] * (zp - zv))

    dzs, tw_b, al_b, dwr_b, dar_b, d_mu, d_w0, d_a0, d_kk, d_ka = _rowcall(
        rw_prep_bwd, "rwkv_prep_bwd", t_dim, bt_many,
        [(z, 0, sec), dr_c, dr_b, dw_c, dk_c, dk_b, dv_c, dv_b, dkk_c, da_c, dg_rw], rw_consts,
        [(sec, F32), (lora, BF16), (lora, BF16), (c_rw, BF16), (c_rw, BF16)],
        acc_outs=[(1, sec), (1, c_rw), (1, c_rw), (1, c_rw), (1, c_rw)], scratch=[pltpu.VMEM((1, sec), F32)])
    d_wl = _matmul(tw_b, dwr_b, "tn", "d_w_lora", out_dtype=BF16)
    d_wa = _matmul(al_b, dar_b, "tn", "d_a_lora", out_dtype=BF16)

    def shift_bwd(dzs_ref, mu_ref, dz_ref, carry):
        _first_step_zero(carry)
        d = dzs_ref[...]
        nbt = d.shape[0]
        nxt = pltpu.roll(d, nbt - 1, 0)
        rowi = lax.broadcasted_iota(jnp.int32, d.shape, 0)
        nxt = jnp.where(rowi == nbt - 1, carry[...], nxt)
        carry[...] = d[0:1, :]
        m = mu_ref[...]
        dz_ref[...] = (d * (1.0 - m) + nxt * m).astype(BF16)

    (dz_rw,) = _rowcall(shift_bwd, "shift_bwd", t_dim, bt, [dzs], [mu], [(sec, BF16)],
                        scratch=[pltpu.VMEM((1, sec), F32)], reverse=True)

    fox_parts = [dq_f.astype(BF16), dk_f.astype(BF16), dv_f.astype(BF16), dg_fox, dz_fl]
    if sec > 4 * c_fox + LANES:
        fox_parts.append(jnp.zeros((t_dim, sec - 4 * c_fox - LANES), BF16))
    dz = jnp.concatenate([dz_rw] + fox_parts + [dz_gate], axis=1)
    d_wz = _matmul(dz, h, "tn", "d_w_in", out_dtype=BF16, bm=1408)
    rw_cols, fox_cols = c4 + 2 * lora, 4 * c_fox + h_fox
    d_wi = jnp.concatenate([d_wz[:rw_cols], d_wz[sec:sec + fox_cols], d_wz[2 * sec:2 * sec + 2 * d_model]], axis=0)
    parts = d_wi.reshape(NDEV, -1, d_model)
    mine = lax.dynamic_index_in_dim(parts.reshape((NCHIP, 2) + parts.shape[1:]), lax.axis_index("c"), 1, keepdims=False)
    chip_parts = _add_pairs(mine, _sibling_swap(parts, "swap_w_in"), "add_w_in")
    dh, (recv["w_in"],) = _matmul(dz, wz, "nn", "d_h", bk=sec, exchange=_ChipExchange([chip_parts]))
    late_parts = [_split_shards(d_wl, 1), _split_shards(d_wa, 1)]

    def norm_in_bwd(dh_ref, x_ref, dx1_ref, g_ref, dx_ref, dg1_ref):
        _first_step_zero(dg1_ref)
        xv = x_ref[...]
        rms = lax.rsqrt(jnp.mean(xv * xv, axis=-1, keepdims=True) + NORM_EPS)
        xn = xv * rms
        d = dh_ref[...]
        dg1_ref[...] += _colsum(d * xn)
        gy = d * g_ref[...]
        dx_ref[...] = dx1_ref[...] + rms * (gy - xn * jnp.mean(xn * gy, axis=-1, keepdims=True))

    dx, d_g1 = _rowcall(norm_in_bwd, "norm_in_bwd", t_dim, bt, [dh, x, dx1], [norm_g], [(d_model, F32)],
                        acc_outs=[(1, d_model)])

    rep_grads = {"norm_g": d_g1, "rw_shift_mu": d_mu[:, 0:c4 + 2 * lora], "rw_w0": d_w0, "rw_a0": d_a0, "rw_k_k": d_kk,
                 "rw_k_a": d_ka, "rw_r_k": d_rk, "rw_ln_g": d_lng, "rw_ln_b": d_lnb, "fox_b_f": d_bf[0:h_fox, 0],
                 "ple_norm_g": d_g2, "final_norm_g": d_g3}
    return loss[0, 0], dx, recv, late_parts, rep_grads


def kernel(x, p, norm_g, w_in, rw_shift_mu, rw_w0, rw_w_lora_up, rw_a0, rw_a_lora_up, rw_k_k, rw_k_a, rw_r_k, rw_ln_g, rw_ln_b, fox_b_f, w_up_rwkv, w_up_fox, w_out, ple_proj, ple_gate_w, ple_norm_g, final_norm_g, loss_target, m_norm_g, m_w_in, m_rw_shift_mu, m_rw_w0, m_rw_w_lora_up, m_rw_a0, m_rw_a_lora_up, m_rw_k_k, m_rw_k_a, m_rw_r_k, m_rw_ln_g, m_rw_ln_b, m_fox_b_f, m_w_up_rwkv, m_w_up_fox, m_w_out, m_ple_proj, m_ple_gate_w, m_ple_norm_g, m_final_norm_g, v_norm_g, v_w_in, v_rw_shift_mu, v_rw_w0, v_rw_w_lora_up, v_rw_a0, v_rw_a_lora_up, v_rw_k_k, v_rw_k_a, v_rw_r_k, v_rw_ln_g, v_rw_ln_b, v_fox_b_f, v_w_up_rwkv, v_w_up_fox, v_w_out, v_ple_proj, v_ple_gate_w, v_ple_norm_g, v_final_norm_g):
    args = locals()
    w = {n: args[n] for n in WEIGHTS}
    mom = {n: args["m_" + n] for n in WEIGHTS}
    var = {n: args["v_" + n] for n in WEIGHTS}

    t_dim, d_model = x.shape[1], x.shape[2]
    c_rw, lora = rw_w0.shape[1], rw_w_lora_up.shape[1]
    h_fox = fox_b_f.shape[1]
    c_fox = h_fox * HEAD
    rw_cols, fox_cols, gate_cols = 4 * c_rw + 2 * lora, 4 * c_fox + h_fox, 2 * d_model
    sec = max(rw_cols, 4 * c_fox + LANES, _round_up(gate_cols, LANES))
    assert c_rw % LANES == 0 and c_fox % LANES == 0 and rw_cols % LANES == 0 and h_fox <= LANES and d_model % LANES == 0
    assert rw_a_lora_up.shape[1] == lora and w_in.shape[2] * NDEV == rw_cols + fox_cols + gate_cols

    assert SHARDED[0] == ("w_in", 1)
    (w_in_all,) = _all_gather([w_in[0].T.astype(BF16)], "gather_w_in")
    other_shards = [w[n][0].astype(BF16) for n, _ in SHARDED[1:]]

    def to_sections(wt):
        pad = lambda a: jnp.pad(a, ((0, sec - a.shape[0]), (0, 0)))
        return jnp.concatenate([pad(wt[:rw_cols]), pad(wt[rw_cols:rw_cols + fox_cols]), pad(wt[rw_cols + fox_cols:])])

    rep = {n: w[n] for n in REPLICATED}
    dims = (t_dim, d_model, c_rw, lora, c_fox, h_fox, sec)
    loss_local, grad_x, recv, late_parts, rep_grads = _local_step(
        x[0], p[0, 0], loss_target[0], to_sections(w_in_all.reshape(-1, d_model)), other_shards, rep, dims)

    rep_sizes = [w[n].size for n in REPLICATED]
    rep_offs = [sum(rep_sizes[:i]) for i in range(len(rep_sizes))]
    pack_rep = lambda tree: _pack_rows(jnp.concatenate([tree[n].astype(F32).reshape(-1) for n in REPLICATED]))
    small_all, recv["rw_w_lora_up"], recv["rw_a_lora_up"] = _grad_exchange([pack_rep(rep_grads)] + late_parts,
                                                                           "exchange_small")
    kinds = ("grad", "delta", "new_m", "new_v")
    outs = {}
    for n, _ in SHARDED:
        if n == "w_in":
            update = _adamw(recv[n], w_in[0].T, m_w_in[0].T, v_w_in[0].T, "adamw_" + n)
            update = [jnp.transpose(buf, (0, 2, 1)) for buf in update]
        else:
            update = _adamw(recv[n], w[n][0], mom[n][0], var[n][0], "adamw_" + n)
        for kind, buf in zip(kinds, update):
            outs[kind, n] = buf
    for kind, buf in zip(kinds, _adamw(small_all, pack_rep(w), pack_rep(mom), pack_rep(var), "adamw_replicated")):
        flat = buf.reshape(-1)
        for n, o, s in zip(REPLICATED, rep_offs, rep_sizes):
            outs[kind, n] = flat[o:o + s].reshape(w[n].shape)
    loss = lax.psum(loss_local, MESH_AXES)
    return (loss, grad_x[None], *[outs[kind, n] for kind in ("grad", "delta", "new_m", "new_v") for n in WEIGHTS])
```

```python
import functools
import math

import jax
import jax.numpy as jnp
from jax import lax
from jax.experimental import pallas as pl
from jax.experimental.pallas import tpu as pltpu

F32, BF16 = jnp.float32, jnp.bfloat16
HI = lax.Precision.HIGHEST
LANES = 128
SUBLANES = 8
HEAD = 64
NORM_EPS = 1e-6
GN_EPS = 64e-5
VMEM_LIMIT = 56 * 1024 * 1024
NDEV = 8
PACK_W = 1024
PACK_ALIGN = 16 * PACK_W
MESH_AXES = ("x", "y", "c")
MESH = pl.DeviceIdType.MESH

ADAM_LR, ADAM_B1, ADAM_B2, ADAM_EPS, ADAM_WD, ADAM_STEP = 0.001, 0.9, 0.999, 1e-08, 0.01, 10


def _round_up(n, m):
    return (n + m - 1) // m * m


def _pick(dim, pref, align=LANES):
    if dim <= pref:
        return dim
    best = None
    for cand in range(align, pref + 1, align):
        if dim % cand == 0:
            best = cand
    return dim if best is None else best


def _params(*sem):
    return pltpu.CompilerParams(dimension_semantics=sem, vmem_limit_bytes=VMEM_LIMIT)


def _sigmoid(v):
    return jax.nn.sigmoid(v)


def _log_sigmoid(v):
    return jnp.minimum(v, 0.0) - jnp.log(1.0 + jnp.exp(-jnp.abs(v)))


_DOT_DIMS = {"nn": (((1,), (0,)), ((), ())), "nt": (((1,), (1,)), ((), ())), "tn": (((0,), (0,)), ((), ()))}


def _matmul(a, b, mode, name, out_dtype=F32, bm=512, bn=1024, bk=2048, exchange=None):
    if mode == "tn":
        k_dim, m_dim = a.shape
    else:
        m_dim, k_dim = a.shape
    n_dim = b.shape[0] if mode == "nt" else b.shape[1]
    bm, bn, bk = _pick(m_dim, bm), _pick(n_dim, bn), _pick(k_dim, bk)
    nk = k_dim // bk

    nx = 0 if exchange is None else exchange.n
    grid = (m_dim // bm, n_dim // bn, nk)

    def body(*refs):
        a_ref, b_ref, o_ref = refs[0], refs[1], refs[2 + nx]
        step = [pl.program_id(d) for d in range(3)]
        if nx:
            own_scratch = 1 if nk > 1 else 0
            ex_refs = (refs[2:2 + nx], refs[3 + nx:3 + 2 * nx], refs[3 + 2 * nx + own_scratch:])
            linear = (step[0] * grid[1] + step[1]) * grid[2] + step[2]

            @pl.when(linear == 0)
            def _():
                exchange.start(*ex_refs)

            if hasattr(exchange, "forward"):
                @pl.when(linear == (grid[0] * grid[1] * grid[2]) // 2)
                def _():
                    exchange.forward(*ex_refs)

        prod = lax.dot_general(a_ref[...].astype(BF16), b_ref[...].astype(BF16), _DOT_DIMS[mode],
                               preferred_element_type=F32)
        if nk == 1:
            o_ref[...] = prod.astype(o_ref.dtype)
        else:
            acc_ref, k = refs[3 + 2 * nx], step[2]

            @pl.when(k == 0)
            def _():
                acc_ref[...] = prod

            @pl.when(k > 0)
            def _():
                acc_ref[...] += prod

            @pl.when(k == nk - 1)
            def _():
                o_ref[...] = acc_ref[...].astype(o_ref.dtype)
        if nx:
            @pl.when(linear == grid[0] * grid[1] * grid[2] - 1)
            def _():
                exchange.finish(*ex_refs)

    if mode == "tn":
        a_spec = pl.BlockSpec((bk, bm), lambda i, j, k: (k, i))
    else:
        a_spec = pl.BlockSpec((bm, bk), lambda i, j, k: (i, k))
    if mode == "nt":
        b_spec = pl.BlockSpec((bn, bk), lambda i, j, k: (j, k))
    else:
        b_spec = pl.BlockSpec((bk, bn), lambda i, j, k: (k, j))
    out_spec = pl.BlockSpec((bm, bn), lambda i, j, k: (i, j))
    out_shape = jax.ShapeDtypeStruct((m_dim, n_dim), out_dtype)
    acc_scratch = [pltpu.VMEM((bm, bn), F32)] if nk > 1 else []
    if exchange is None:
        return pl.pallas_call(
            body, name=name, grid=grid, in_specs=[a_spec, b_spec], out_specs=out_spec, out_shape=out_shape,
            scratch_shapes=acc_scratch, compiler_params=_params("parallel", "parallel", "arbitrary"),
        )(a, b)
    outs = pl.pallas_call(
        body, name=name, grid=grid, in_specs=[a_spec, b_spec] + [HBM_SPEC] * nx,
        out_specs=[out_spec] + [HBM_SPEC] * nx, out_shape=[out_shape] + exchange.out_shape,
        scratch_shapes=acc_scratch + exchange.scratch, compiler_params=_params("arbitrary", "arbitrary", "arbitrary"),
    )(a, b, *exchange.arrays)
    return outs[0], outs[1:]


def _rowcall(body, name, t_dim, bt, row_ins, const_ins, row_outs, acc_outs=(), scratch=(), reverse=False):
    nt = t_dim // bt

    def rmap(i):
        return nt - 1 - i if reverse else i

    in_specs, args = [], []
    for item in row_ins:
        arr, cb, w = item if isinstance(item, tuple) else (item, 0, item.shape[1])
        in_specs.append(pl.BlockSpec((bt, w), lambda i, cb=cb: (rmap(i), cb)))
        args.append(arr)
    for arr in const_ins:
        in_specs.append(pl.BlockSpec(arr.shape, lambda i, nd=arr.ndim: (0,) * nd))
        args.append(arr)
    out_specs = [pl.BlockSpec((bt, w), lambda i: (rmap(i), 0)) for w, _ in row_outs]
    out_shape = [jax.ShapeDtypeStruct((t_dim, w), dt) for w, dt in row_outs]
    for shp in acc_outs:
        out_specs.append(pl.BlockSpec(shp, lambda i, nd=len(shp): (0,) * nd))
        out_shape.append(jax.ShapeDtypeStruct(shp, F32))
    return pl.pallas_call(
        body, name=name, grid=(nt,), in_specs=in_specs, out_specs=out_specs, out_shape=out_shape,
        scratch_shapes=list(scratch), compiler_params=_params("arbitrary"),
    )(*args)


def _first_step_zero(*refs):
    @pl.when(pl.program_id(0) == 0)
    def _():
        for r in refs:
            r[...] = jnp.zeros_like(r)


def _colsum(v):
    return jnp.sum(v, axis=0, keepdims=True)


def _headsum(v, e):
    parts = [jnp.dot(v[:, p * LANES:(p + 1) * LANES], e, precision=HI, preferred_element_type=F32)
             for p in range(v.shape[1] // LANES)]
    return parts[0] if len(parts) == 1 else jnp.concatenate(parts, axis=1)


def _shift_down(v, carry_ref):
    bt = v.shape[0]
    prev = pltpu.roll(v, 1, 0)
    row = lax.broadcasted_iota(jnp.int32, v.shape, 0)
    prev = jnp.where(row == 0, carry_ref[...], prev)
    carry_ref[...] = v[bt - 1:bt, :]
    return prev


def _pair_consts():
    lane = lax.broadcasted_iota(jnp.int32, (1, LANES), 1)
    m0 = (lane < HEAD).astype(F32)
    m1 = 1.0 - m0
    sub = lax.broadcasted_iota(jnp.int32, (HEAD, LANES), 0)
    lane2 = lax.broadcasted_iota(jnp.int32, (HEAD, LANES), 1)
    i0 = (lane2 == sub).astype(F32)
    i1 = (lane2 == sub + HEAD).astype(F32)
    return m0, m1, i0, i1


def _head_masks():
    lane = lax.broadcasted_iota(jnp.int32, (1, LANES), 1)
    first = (lane < HEAD).astype(F32)
    return first, 1.0 - first


def _head_ones(dtype=BF16):
    lane = jnp.arange(LANES)
    return (lane[:, None] // HEAD == lane[None, :] // HEAD).astype(dtype)


def _lanesum(v):
    return jnp.sum(v, axis=1, keepdims=True)


def _split_bf16(v):
    hi = v.astype(BF16).astype(F32)
    rest = v - hi
    mid = rest.astype(BF16).astype(F32)
    return hi, mid, (rest - mid).astype(BF16).astype(F32)


def _col_pair(parts, j, i01, e_bf16):
    lhs = jnp.concatenate([(part[j:j + 1] * i01).astype(BF16) for part in parts], axis=0)
    out = jnp.dot(lhs, e_bf16, preferred_element_type=F32)
    return (out[0:HEAD] + out[HEAD:2 * HEAD]) + out[2 * HEAD:3 * HEAD]


def _row_pair(c0, c1, i0, i1):
    return _colsum(c0 * i0 + c1 * i1)


def _scan_fwd(r, w, k, v, kk, a, tc, npb):
    t_dim, c_dim = r.shape
    wb = LANES * npb
    tc = _pick(t_dim, tc, SUBLANES)

    def body(r_ref, w_ref, k_ref, v_ref, kk_ref, a_ref, e_ref, y_ref, sall_ref, sfin_ref, s_ref):
        @pl.when(pl.program_id(1) == 0)
        def _():
            s_ref[...] = jnp.zeros_like(s_ref)

        m0, m1, i0, i1 = _pair_consts()
        i01 = i0 + i1
        e = e_ref[...]
        sub8 = lax.broadcasted_iota(jnp.int32, (SUBLANES, LANES), 0)
        lanes = [slice(q * LANES, (q + 1) * LANES) for q in range(npb)]
        ng = tc // SUBLANES

        def lanesums(tiles):
            sums = _lanesum(jnp.concatenate(tiles, axis=0))
            return [sums[i * HEAD:(i + 1) * HEAD] for i in range(len(tiles))]

        def halves(s, row):
            return [s * (row * m0), s * (row * m1)]

        def group(gi, carry):
            base = pl.multiple_of(gi * SUBLANES, SUBLANES)
            rows = pl.ds(base, SUBLANES)
            s = list(carry)
            r8 = [r_ref[rows, sl] for sl in lanes]
            w8 = [w_ref[rows, sl] for sl in lanes]
            k8 = [k_ref[rows, sl] for sl in lanes]
            kk8 = [kk_ref[rows, sl] for sl in lanes]
            b8 = [kk8[q] * a_ref[rows, lanes[q]] for q in range(npb)]
            v8 = [_split_bf16(v_ref[rows, sl]) for sl in lanes]
            y8 = [jnp.zeros((SUBLANES, LANES), F32)] * npb
            for j in range(SUBLANES + 1):
                one, before = slice(j, j + 1), slice(j - 1, j)
                tiles = []
                for q in range(npb):
                    if j < SUBLANES:
                        tiles += halves(s[q], kk8[q][one])
                    if j > 0:
                        tiles += halves(s[q], r8[q][before])
                cols = lanesums(tiles)
                per = len(tiles) // npb
                for q in range(npb):
                    mine = cols[q * per:(q + 1) * per]
                    if j > 0:
                        y8[q] = jnp.where(sub8 == j - 1, _row_pair(mine[-2], mine[-1], i0, i1), y8[q])
                    if j < SUBLANES:
                        sall_ref[base + j, :, lanes[q]] = s[q]
                        sb = mine[0] * m0 + mine[1] * m1
                        s[q] = s[q] * w8[q][one] - sb * b8[q][one] + _col_pair(v8[q], j, i01, e) * k8[q][one]
            for q in range(npb):
                y_ref[rows, lanes[q]] = y8[q]
            return tuple(s)

        init = tuple(s_ref[:, q * LANES:(q + 1) * LANES] for q in range(npb))
        fin = lax.fori_loop(0, ng, group, init)
        for q in range(npb):
            s_ref[:, q * LANES:(q + 1) * LANES] = fin[q]
            sfin_ref[:, q * LANES:(q + 1) * LANES] = fin[q]

    row = pl.BlockSpec((tc, wb), lambda p, c: (c, p))
    return pl.pallas_call(
        body, name="rwkv_scan_fwd", grid=(c_dim // wb, t_dim // tc),
        in_specs=[row] * 6 + [pl.BlockSpec((LANES, LANES), lambda p, c: (0, 0))],
        out_specs=[row, pl.BlockSpec((tc, HEAD, wb), lambda p, c: (c, 0, p)), pl.BlockSpec((HEAD, wb), lambda p, c: (0, p))],
        out_shape=[jax.ShapeDtypeStruct((t_dim, c_dim), F32), jax.ShapeDtypeStruct((t_dim, HEAD, c_dim), F32),
                   jax.ShapeDtypeStruct((HEAD, c_dim), F32)],
        scratch_shapes=[pltpu.VMEM((HEAD, wb), F32)],
        compiler_params=_params("parallel", "arbitrary"),
    )(r, w, k, v, kk, a, _head_ones())


def _scan_bwd(r, w, k, v, kk, a, dy, sall, sfin, tc, npb, exchange):
    t_dim, c_dim = r.shape
    wb = LANES * npb
    tc = _pick(t_dim, tc, SUBLANES)
    nc = t_dim // tc
    nx = exchange.n
    n_blocks = c_dim // wb

    def body(*refs):
        r_ref, w_ref, k_ref, v_ref, kk_ref, a_ref, dy_ref, sall_ref, sfin_ref, e_ref = refs[:10]
        dr_ref, dw_ref, dk_ref, dv_ref, dkk_ref, da_ref = refs[10 + nx:16 + nx]
        ds_ref, sn_ref = refs[16 + 2 * nx:18 + 2 * nx]
        here_p, here_c = pl.program_id(0), pl.program_id(1)
        ex_refs = (refs[10:10 + nx], refs[16 + nx:16 + 2 * nx], refs[18 + 2 * nx:])

        @pl.when((here_p == 0) & (here_c == 0))
        def _():
            exchange.start(*ex_refs)

        @pl.when(pl.program_id(1) == 0)
        def _():
            ds_ref[...] = jnp.zeros_like(ds_ref)
            sn_ref[...] = sfin_ref[...]

        m0, m1, i0, i1 = _pair_consts()
        i01 = i0 + i1
        e = e_ref[...]
        sub8 = lax.broadcasted_iota(jnp.int32, (SUBLANES, LANES), 0)
        lanes = [slice(q * LANES, (q + 1) * LANES) for q in range(npb)]
        ng = tc // SUBLANES

        def halves(s, row):
            return [s * (row * m0), s * (row * m1)]

        def group(gi, carry):
            base = pl.multiple_of((ng - 1 - gi) * SUBLANES, SUBLANES)
            rows = pl.ds(base, SUBLANES)
            ds = list(carry)
            r8 = [r_ref[rows, sl] for sl in lanes]
            w8 = [w_ref[rows, sl] for sl in lanes]
            k8 = [k_ref[rows, sl] for sl in lanes]
            kk8 = [kk_ref[rows, sl] for sl in lanes]
            a8 = [a_ref[rows, sl] for sl in lanes]
            v8 = [_split_bf16(v_ref[rows, sl]) for sl in lanes]
            dy8 = [_split_bf16(dy_ref[rows, sl]) for sl in lanes]
            zero8 = jnp.zeros((SUBLANES, LANES), F32)
            dr8, dw8, dk8, dv8, dkk8, da8 = ([zero8] * npb for _ in range(6))
            for j in reversed(range(SUBLANES)):
                one = slice(j, j + 1)
                here = sub8 == j
                d, s_prev, tiles = [], [], []
                for q in range(npb):
                    dyb = _col_pair(dy8[q], j, i01, e)
                    s_prev.append(sall_ref[base + j, :, lanes[q]])
                    dr8[q] = jnp.where(here, _colsum(sn_ref[:, lanes[q]] * dyb), dr8[q])
                    sn_ref[:, lanes[q]] = s_prev[q]
                    d.append(ds[q] + dyb * r8[q][one])
                for q in range(npb):
                    tiles += halves(d[q], kk8[q][one] * a8[q][one])
                for q in range(npb):
                    tiles += halves(d[q], k8[q][one]) + halves(s_prev[q], kk8[q][one])
                sums = _lanesum(jnp.concatenate(tiles, axis=0))
                cols = [sums[i * HEAD:(i + 1) * HEAD] for i in range(len(tiles))]
                for q in range(npb):
                    kkr, ar = kk8[q][one], a8[q][one]
                    dsb = -(cols[2 * q] * m0 + cols[2 * q + 1] * m1)
                    ds[q] = d[q] * w8[q][one] + dsb * kkr
                    rest = cols[2 * npb + 4 * q:2 * npb + 4 * q + 4]
                    dv8[q] = jnp.where(here, _row_pair(rest[0], rest[1], i0, i1), dv8[q])
                    sb = rest[2] * m0 + rest[3] * m1
                    db = -_colsum(d[q] * sb)
                    dk8[q] = jnp.where(here, _colsum(d[q] * _col_pair(v8[q], j, i01, e)), dk8[q])
                    dw8[q] = jnp.where(here, _colsum(d[q] * s_prev[q]), dw8[q])
                    dkk8[q] = jnp.where(here, _colsum(s_prev[q] * dsb) + db * ar, dkk8[q])
                    da8[q] = jnp.where(here, db * kkr, da8[q])
            for q in range(npb):
                sl = lanes[q]
                dr_ref[rows, sl], dw_ref[rows, sl], dk_ref[rows, sl] = dr8[q], dw8[q], dk8[q]
                dv_ref[rows, sl], dkk_ref[rows, sl], da_ref[rows, sl] = dv8[q], dkk8[q], da8[q]
            return tuple(ds)

        init = tuple(ds_ref[:, q * LANES:(q + 1) * LANES] for q in range(npb))
        fin = lax.fori_loop(0, ng, group, init)
        for q in range(npb):
            ds_ref[:, q * LANES:(q + 1) * LANES] = fin[q]

        @pl.when((here_p == n_blocks - 1) & (here_c == nc - 1))
        def _():
            exchange.finish(*ex_refs)

    row = pl.BlockSpec((tc, wb), lambda p, c: (nc - 1 - c, p))
    outs = pl.pallas_call(
        body, name="rwkv_scan_bwd", grid=(n_blocks, nc),
        in_specs=[row] * 7 + [pl.BlockSpec((tc, HEAD, wb), lambda p, c: (nc - 1 - c, 0, p)),
                              pl.BlockSpec((HEAD, wb), lambda p, c: (0, p)),
                              pl.BlockSpec((LANES, LANES), lambda p, c: (0, 0))] + [HBM_SPEC] * nx,
        out_specs=[row] * 6 + [HBM_SPEC] * nx,
        out_shape=[jax.ShapeDtypeStruct((t_dim, c_dim), F32)] * 6 + exchange.out_shape,
        scratch_shapes=[pltpu.VMEM((HEAD, wb), F32)] * 2 + exchange.scratch,
        compiler_params=_params("arbitrary", "arbitrary"),
    )(r, w, k, v, kk, a, dy, sall, sfin, _head_ones(), *exchange.arrays)
    return outs[:6], outs[6:]


def _fox_fwd(z, ct, cq, q_cb, k_cb, v_cb, n_pairs, blk):
    t_dim = z.shape[0]
    blk = _pick(t_dim, blk)
    nq = t_dim // blk
    scale = HEAD ** -0.5

    def body(q_ref, k_ref, v_ref, ct_ref, cq_ref, o_ref, lse_ref):
        i = pl.program_id(1)
        rowi = lax.broadcasted_iota(jnp.int32, (blk, blk), 0)
        coli = lax.broadcasted_iota(jnp.int32, (blk, blk), 1)
        masks = _head_masks()
        qv = q_ref[...]
        qs = [(qv * mk).astype(BF16) for mk in masks]
        cqs = [cq_ref[:, hh:hh + 1] for hh in range(2)]

        def kv_step(j, carry, masked):
            rows = pl.ds(pl.multiple_of(j * blk, blk), blk)
            kb = k_ref[rows, :].astype(BF16)
            vv = v_ref[rows, :]
            stats, acc = list(carry[:4]), carry[4]
            rescale, add = 0.0, 0.0
            for hh in range(2):
                m, l = stats[2 * hh], stats[2 * hh + 1]
                s = (lax.dot_general(qs[hh], kb, _DOT_DIMS["nt"], preferred_element_type=F32) * scale
                     + (cqs[hh] - ct_ref[hh:hh + 1, rows]))
                if masked:
                    s = jnp.where(rowi >= coli, s, -jnp.inf)
                m_new = jnp.maximum(m, jnp.max(s, axis=1, keepdims=True))
                alpha = jnp.exp(m - m_new)
                pr = jnp.exp(s - m_new)
                stats[2 * hh], stats[2 * hh + 1] = m_new, l * alpha + jnp.sum(pr, axis=1, keepdims=True)
                rescale = rescale + alpha * masks[hh]
                hi = pr.astype(BF16)
                both = jnp.concatenate([hi, (pr - hi.astype(F32)).astype(BF16)], axis=1)
                vh = (vv * masks[hh]).astype(BF16)
                add = add + jnp.dot(both, jnp.concatenate([vh, vh], axis=0), preferred_element_type=F32)
            return (*stats, acc * rescale + add)

        neg, zero = jnp.full((blk, 1), -jnp.inf, F32), jnp.zeros((blk, 1), F32)
        carry = lax.fori_loop(0, i, functools.partial(kv_step, masked=False),
                              (neg, zero, neg, zero, jnp.zeros((blk, LANES), F32)))
        m0, l0, m1, l1, acc = kv_step(i, carry, True)
        o_ref[...] = acc * (masks[0] / l0 + masks[1] / l1)
        lse_ref[:, 0:1] = m0 + jnp.log(l0)
        lse_ref[:, 1:2] = m1 + jnp.log(l1)

    full = lambda cb: pl.BlockSpec((t_dim, LANES), lambda p, i, cb=cb: (0, cb + p))
    return pl.pallas_call(
        body, name="fox_attn_fwd", grid=(n_pairs, nq),
        in_specs=[pl.BlockSpec((blk, LANES), lambda p, i: (i, q_cb + p)), full(k_cb), full(v_cb),
                  pl.BlockSpec((None, SUBLANES, t_dim), lambda p, i: (p, 0, 0)),
                  pl.BlockSpec((None, blk, 2), lambda p, i: (p, i, 0))],
        out_specs=[pl.BlockSpec((blk, LANES), lambda p, i: (i, p)), pl.BlockSpec((None, blk, 2), lambda p, i: (p, i, 0))],
        out_shape=[jax.ShapeDtypeStruct((t_dim, n_pairs * LANES), F32), jax.ShapeDtypeStruct((n_pairs, t_dim, 2), F32)],
        compiler_params=_params("parallel", "arbitrary"),
    )(z, z, z, ct, cq)


def _fox_bwd(z, ct, cq, rowdot, do, lse, q_cb, k_cb, v_cb, n_pairs, blk):
    t_dim = z.shape[0]
    blk = _pick(t_dim, blk)
    nb = t_dim // blk
    scale = HEAD ** -0.5

    def body(q_ref, k_ref, v_ref, ct_ref, cq_ref, rd_ref, do_ref, lse_ref, dq_ref, dk_ref, dv_ref, dc_ref):
        j = pl.program_id(1)

        @pl.when(j == 0)
        def _():
            dq_ref[...] = jnp.zeros_like(dq_ref)

        rowi = lax.broadcasted_iota(jnp.int32, (blk, blk), 0)
        coli = lax.broadcasted_iota(jnp.int32, (blk, blk), 1)
        krows = pl.ds(pl.multiple_of(j * blk, blk), blk)
        masks = _head_masks()
        kv, vb = k_ref[...], v_ref[...].astype(BF16)
        kb = kv.astype(BF16)
        ks = [(kv * mk).astype(BF16) for mk in masks]
        cks = [ct_ref[hh:hh + 1, krows] for hh in range(2)]

        def q_step(i, carry, masked):
            dk, dv, dcs = carry[0], carry[1], list(carry[2:])
            rows = pl.ds(pl.multiple_of(i * blk, blk), blk)
            qv, dov = q_ref[rows, :], do_ref[rows, :]
            dq = 0.0
            for hh in range(2):
                qh = (qv * masks[hh]).astype(BF16)
                doh = (dov * masks[hh]).astype(BF16)
                s = (lax.dot_general(qh, kb, _DOT_DIMS["nt"], preferred_element_type=F32) * scale
                     + (cq_ref[rows, hh:hh + 1] - cks[hh]))
                pr = jnp.exp(s - lse_ref[rows, hh:hh + 1])
                if masked:
                    pr = jnp.where(rowi >= coli, pr, 0.0)
                dv = dv + lax.dot_general(pr.astype(BF16), doh, _DOT_DIMS["tn"], preferred_element_type=F32)
                dp = lax.dot_general(doh, vb, _DOT_DIMS["nt"], preferred_element_type=F32)
                ds = pr * (dp - rd_ref[rows, hh * HEAD:hh * HEAD + 1])
                dsb = ds.astype(BF16)
                dq = dq + jnp.dot(dsb, ks[hh], preferred_element_type=F32)
                dk = dk + lax.dot_general(dsb, qh, _DOT_DIMS["tn"], preferred_element_type=F32)
                dcs[hh] = dcs[hh] - _colsum(ds)
            dq_ref[rows, :] += dq * scale
            return (dk, dv, *dcs)

        zero_row = jnp.zeros((1, blk), F32)
        init = (jnp.zeros((blk, LANES), F32), jnp.zeros((blk, LANES), F32), zero_row, zero_row)
        carry = q_step(j, init, True)
        dk, dv, dc0, dc1 = lax.fori_loop(j + 1, nb, functools.partial(q_step, masked=False), carry)
        dk_ref[...] = dk * scale
        dv_ref[...] = dv
        dc_ref[0:1, :] = dc0
        dc_ref[1:2, :] = dc1
        dc_ref[2:SUBLANES, :] = jnp.zeros((SUBLANES - 2, blk), F32)

    full = lambda: pl.BlockSpec((t_dim, LANES), lambda p, j: (0, p))
    blkspec = pl.BlockSpec((blk, LANES), lambda p, j: (j, p))
    return pl.pallas_call(
        body, name="fox_attn_bwd", grid=(n_pairs, nb),
        in_specs=[pl.BlockSpec((t_dim, LANES), lambda p, j: (0, q_cb + p)),
                  pl.BlockSpec((blk, LANES), lambda p, j: (j, k_cb + p)),
                  pl.BlockSpec((blk, LANES), lambda p, j: (j, v_cb + p)),
                  pl.BlockSpec((None, SUBLANES, t_dim), lambda p, j: (p, 0, 0)),
                  pl.BlockSpec((None, t_dim, 2), lambda p, j: (p, 0, 0)), full(), full(),
                  pl.BlockSpec((None, t_dim, 2), lambda p, j: (p, 0, 0))],
        out_specs=[full(), blkspec, blkspec, pl.BlockSpec((None, SUBLANES, blk), lambda p, j: (p, 0, j))],
        out_shape=[jax.ShapeDtypeStruct((t_dim, n_pairs * LANES), F32)] * 3
                  + [jax.ShapeDtypeStruct((n_pairs, SUBLANES, t_dim), F32)],
        compiler_params=_params("parallel", "arbitrary"),
    )(z, z, z, ct, cq, rowdot, do, lse)


HBM_SPEC = pl.BlockSpec(memory_space=pltpu.HBM)


class _Gather:
    def __init__(self, shards):
        self.arrays = list(shards)
        self.n = len(self.arrays)
        self.out_shape = [jax.ShapeDtypeStruct((NDEV,) + s.shape, s.dtype) for s in self.arrays]
        self.scratch = [pltpu.SemaphoreType.DMA((7 * self.n,)), pltpu.SemaphoreType.DMA((7 * self.n,)),
                        pltpu.SemaphoreType.DMA((self.n,))]

    def _plan(self, x_refs, out_refs, sems):
        send_sems, recv_sems, local_sems = sems
        x, y, c = lax.axis_index("x"), lax.axis_index("y"), lax.axis_index("c")
        me, sibling = (x, y, c), (x, y, 1 - c)
        chips = [(1 - x, y), (x, 1 - y), (1 - x, 1 - y)]

        def copy(a, k, block, to, from_input=False):
            px, py, pc = block
            slot = out_refs[a].at[4 * px + 2 * py + pc]
            return pltpu.make_async_remote_copy(
                src_ref=x_refs[a] if from_input else slot, dst_ref=slot,
                send_sem=send_sems.at[7 * a + k], recv_sem=recv_sems.at[7 * a + k], device_id=to, device_id_type=MESH)

        mine = [pltpu.make_async_copy(x_refs[a], out_refs[a].at[4 * x + 2 * y + c], local_sems.at[a])
                for a in range(self.n)]
        first = []
        for a in range(self.n):
            first += [copy(a, 1 + j, me, (*chip, c), from_input=True) for j, chip in enumerate(chips)]
            first.append(copy(a, 0, me, sibling, from_input=True))
        over_ici = [copy(a, 1 + j, (*chip, c), me) for a in range(self.n) for j, chip in enumerate(chips)]
        passed = [copy(a, 4 + j, (*chip, c), sibling) for a in range(self.n) for j, chip in enumerate(chips)]
        from_sibling = []
        for a in range(self.n):
            from_sibling.append(copy(a, 0, sibling, me))
            from_sibling += [copy(a, 4 + j, (*chip, 1 - c), me) for j, chip in enumerate(chips)]
        return mine, first, over_ici, passed, from_sibling

    def start(self, *refs):
        mine, first, _, _, _ = self._plan(*refs)
        for cp in mine + first:
            cp.start()

    def forward(self, *refs):
        _, _, over_ici, passed, _ = self._plan(*refs)
        for arrived, onward in zip(over_ici, passed):
            arrived.wait_recv()
            onward.start()

    def finish(self, *refs):
        mine, first, _, passed, from_sibling = self._plan(*refs)
        for cp in from_sibling:
            cp.wait_recv()
        for cp in first + passed:
            cp.wait_send()
        for cp in mine:
            cp.wait()


def _all_gather(shards, name):
    ga = _Gather(shards)

    def body(*refs):
        parts = (refs[:ga.n], refs[ga.n:2 * ga.n], refs[2 * ga.n:])
        ga.start(*parts)
        ga.forward(*parts)
        ga.finish(*parts)

    return pl.pallas_call(body, name=name, out_shape=ga.out_shape, in_specs=[HBM_SPEC] * ga.n,
                          out_specs=[HBM_SPEC] * ga.n, scratch_shapes=ga.scratch)(*ga.arrays)


class _Exchange:
    def __init__(self, arrays):
        self.arrays = list(arrays)
        self.n = len(self.arrays)
        self.per_dest = [a.ndim == 3 for a in self.arrays]
        self.out_shape = [jax.ShapeDtypeStruct(a.shape if pd else (NDEV,) + a.shape, a.dtype)
                          for a, pd in zip(self.arrays, self.per_dest)]
        self.scratch = [pltpu.SemaphoreType.DMA((7 * self.n,)), pltpu.SemaphoreType.DMA((7 * self.n,)),
                        pltpu.SemaphoreType.DMA((self.n,))]

    def _copies(self, in_refs, out_refs, sems):
        send_sems, recv_sems, local_sems = sems
        x, y, c = lax.axis_index("x"), lax.axis_index("y"), lax.axis_index("c")
        me = 4 * x + 2 * y + c
        own, sends, recvs = [], [], []
        for a in range(self.n):
            mine = in_refs[a].at[me] if self.per_dest[a] else in_refs[a]
            own.append(pltpu.make_async_copy(mine, out_refs[a].at[me], local_sems.at[a]))
            for k in range(1, NDEV):
                px = 1 - x if k & 4 else x
                py = 1 - y if k & 2 else y
                pc = 1 - c if k & 1 else c
                peer = 4 * px + 2 * py + pc
                sem = dict(send_sem=send_sems.at[7 * a + k - 1], recv_sem=recv_sems.at[7 * a + k - 1],
                           device_id=(px, py, pc), device_id_type=MESH)
                src = in_refs[a].at[peer] if self.per_dest[a] else in_refs[a]
                sends.append(pltpu.make_async_remote_copy(src_ref=src, dst_ref=out_refs[a].at[me], **sem))
                recvs.append(pltpu.make_async_remote_copy(src_ref=src, dst_ref=out_refs[a].at[peer], **sem))
        return own, sends, recvs

    def start(self, in_refs, out_refs, sems):
        own, sends, _ = self._copies(in_refs, out_refs, sems)
        for cp in own + sends:
            cp.start()

    def finish(self, in_refs, out_refs, sems):
        own, sends, recvs = self._copies(in_refs, out_refs, sems)
        for cp in recvs:
            cp.wait_recv()
        for cp in sends:
            cp.wait_send()
        for cp in own:
            cp.wait()


def _grad_exchange(arrays, name):
    ex = _Exchange(arrays)

    def body(*refs):
        in_refs, out_refs, sems = refs[:ex.n], refs[ex.n:2 * ex.n], refs[2 * ex.n:]
        ex.start(in_refs, out_refs, sems)
        ex.finish(in_refs, out_refs, sems)

    return pl.pallas_call(body, name=name, out_shape=ex.out_shape, in_specs=[HBM_SPEC] * ex.n,
                          out_specs=[HBM_SPEC] * ex.n, scratch_shapes=ex.scratch)(*ex.arrays)


NCHIP = NDEV // 2


def _sibling_swap(parts, name):
    def body(p_ref, out_ref, send_sems, recv_sems):
        x, y, c = lax.axis_index("x"), lax.axis_index("y"), lax.axis_index("c")
        copies = [pltpu.make_async_remote_copy(
            src_ref=p_ref.at[2 * i + 1 - c], dst_ref=out_ref.at[i], send_sem=send_sems.at[i], recv_sem=recv_sems.at[i],
            device_id=(x, y, 1 - c), device_id_type=MESH) for i in range(NCHIP)]
        for cp in copies:
            cp.start()
        for cp in copies:
            cp.wait_recv()
        for cp in copies:
            cp.wait_send()

    return pl.pallas_call(
        body, name=name, out_shape=jax.ShapeDtypeStruct((NCHIP,) + parts.shape[1:], parts.dtype),
        in_specs=[HBM_SPEC], out_specs=HBM_SPEC,
        scratch_shapes=[pltpu.SemaphoreType.DMA((NCHIP,)), pltpu.SemaphoreType.DMA((NCHIP,))],
    )(parts)


def _add_pairs(parts, sibling, name):
    _, rows, width = parts.shape
    bc = _pick(width, 4 * LANES)

    def body(even_ref, odd_ref, sib_ref, o_ref):
        own = jnp.where(lax.axis_index("c") == 0, even_ref[...], odd_ref[...])
        o_ref[...] = (own.astype(F32) + sib_ref[...].astype(F32)).astype(o_ref.dtype)

    slot = lambda parity: pl.BlockSpec((None, rows, bc), lambda i, j: (2 * i + parity, 0, j))
    blk = pl.BlockSpec((None, rows, bc), lambda i, j: (i, 0, j))
    return pl.pallas_call(body, name=name, grid=(NCHIP, width // bc), in_specs=[slot(0), slot(1), blk], out_specs=blk,
                          out_shape=jax.ShapeDtypeStruct(sibling.shape, parts.dtype),
                          compiler_params=_params("parallel", "parallel"))(parts, parts, sibling)


class _ChipExchange:
    FLIPS = ((0, 1), (1, 0), (1, 1))

    def __init__(self, arrays):
        self.arrays = list(arrays)
        self.n = len(self.arrays)
        self.out_shape = [jax.ShapeDtypeStruct(a.shape, a.dtype) for a in self.arrays]
        self.scratch = [pltpu.SemaphoreType.DMA((3 * self.n,)), pltpu.SemaphoreType.DMA((3 * self.n,)),
                        pltpu.SemaphoreType.DMA((self.n,))]

    def _copies(self, in_refs, out_refs, sems):
        send_sems, recv_sems, local_sems = sems
        x, y, c = lax.axis_index("x"), lax.axis_index("y"), lax.axis_index("c")
        here = 2 * x + y
        own, sends, recvs = [], [], []
        for a in range(self.n):
            own.append(pltpu.make_async_copy(in_refs[a].at[here], out_refs[a].at[here], local_sems.at[a]))
            for k, (fx, fy) in enumerate(self.FLIPS):
                px = 1 - x if fx else x
                py = 1 - y if fy else y
                there = 2 * px + py
                sem = dict(send_sem=send_sems.at[3 * a + k], recv_sem=recv_sems.at[3 * a + k],
                           device_id=(px, py, c), device_id_type=MESH)
                src = in_refs[a].at[there]
                sends.append(pltpu.make_async_remote_copy(src_ref=src, dst_ref=out_refs[a].at[here], **sem))
                recvs.append(pltpu.make_async_remote_copy(src_ref=src, dst_ref=out_refs[a].at[there], **sem))
        return own, sends, recvs

    start = _Exchange.start
    finish = _Exchange.finish


def _adamw_body(p_ref, w_ref, m_ref, v_ref, g_out, d_out, m_out, v_out):
    g = p_ref[0].astype(F32)
    for d in range(1, p_ref.shape[0]):
        g = g + p_ref[d].astype(F32)
    mn = ADAM_B1 * m_ref[...] + (1.0 - ADAM_B1) * g
    vn = ADAM_B2 * v_ref[...] + (1.0 - ADAM_B2) * jnp.square(g)
    m_hat = mn / (1.0 - ADAM_B1 ** ADAM_STEP)
    v_hat = vn / (1.0 - ADAM_B2 ** ADAM_STEP)
    g_out[...] = g
    d_out[...] = -ADAM_LR * (m_hat / (jnp.sqrt(v_hat) + ADAM_EPS) + ADAM_WD * w_ref[...])
    m_out[...] = mn
    v_out[...] = vn


def _adamw(partials, w, m, v, name):
    rows, width = w.shape
    n = partials.shape[0]
    body = functools.partial(_adamw_body)
    if rows % (2 * SUBLANES) == 0:
        br = _pick(rows, 128, 2 * SUBLANES)
        grid, shape, index, index3 = rows // br, (br, width), lambda i: (i, 0), lambda i: (0, i, 0)
    else:
        bc = _pick(width, 2 * LANES)
        grid, shape, index, index3 = width // bc, (rows, bc), lambda i: (0, i), lambda i: (0, 0, i)
    blk = pl.BlockSpec(shape, index)
    return pl.pallas_call(
        body, name=name, grid=(grid,),
        in_specs=[pl.BlockSpec((n,) + shape, index3), blk, blk, blk],
        out_specs=[pl.BlockSpec((None,) + shape, index3)] * 4,
        out_shape=[jax.ShapeDtypeStruct((1, rows, width), F32)] * 4,
        compiler_params=_params("parallel"),
    )(partials, w, m, v)


def _pack_rows(flat):
    n = flat.shape[0]
    padded = _round_up(n, PACK_ALIGN)
    return jnp.pad(flat, (0, padded - n)).reshape(padded // PACK_W, PACK_W)


def _split_shards(full, axis):
    rows, cols = full.shape
    if axis == 0:
        return full.reshape(NDEV, rows // NDEV, cols)
    width = cols // NDEV
    return jnp.stack([full[:, d * width:(d + 1) * width] for d in range(NDEV)])


def _join_shards(blocks, axis):
    if axis == 0:
        return blocks.reshape(-1, blocks.shape[2])
    return jnp.concatenate([blocks[d] for d in range(NDEV)], axis=1)


SHARDED = (("w_in", 1), ("rw_w_lora_up", 1), ("rw_a_lora_up", 1), ("w_up_rwkv", 1), ("w_up_fox", 1),
           ("w_out", 0), ("ple_proj", 1), ("ple_gate_w", 0))
REPLICATED = ("norm_g", "rw_shift_mu", "rw_w0", "rw_a0", "rw_k_k", "rw_k_a", "rw_r_k", "rw_ln_g", "rw_ln_b",
              "fox_b_f", "ple_norm_g", "final_norm_g")
WEIGHTS = ("norm_g", "w_in", "rw_shift_mu", "rw_w0", "rw_w_lora_up", "rw_a0", "rw_a_lora_up", "rw_k_k", "rw_k_a",
           "rw_r_k", "rw_ln_g", "rw_ln_b", "fox_b_f", "w_up_rwkv", "w_up_fox", "w_out", "ple_proj", "ple_gate_w",
           "ple_norm_g", "final_norm_g")


def _local_step(x, p, tgt, wz, other_shards, rep, dims):
    t_dim, d_model, c_rw, lora, c_fox, h_fox, sec = dims
    bt = _pick(t_dim, 256, 2 * SUBLANES)
    bt_many = _pick(t_dim, 128, 2 * SUBLANES)
    n_pairs = c_fox // LANES
    row = lambda a: a.reshape(1, -1)
    norm_g, mu, w0, a0 = row(rep["norm_g"]), row(rep["rw_shift_mu"]), row(rep["rw_w0"]), row(rep["rw_a0"])
    k_k, k_a, r_k = row(rep["rw_k_k"]), row(rep["rw_k_a"]), row(rep["rw_r_k"])
    ln_g, ln_b = row(rep["rw_ln_g"]), row(rep["rw_ln_b"])
    g2, g3 = row(rep["ple_norm_g"]), row(rep["final_norm_g"])
    b_f = jnp.pad(row(rep["fox_b_f"]), ((0, 0), (0, LANES - h_fox)))
    e_head = _head_ones(F32)
    c4 = 4 * c_rw
    inv_d = 1.0 / d_model
    decay_k = math.exp(-0.5)

    def norm_in(x_ref, g_ref, h_ref):
        xv = x_ref[...]
        rms = lax.rsqrt(jnp.mean(xv * xv, axis=-1, keepdims=True) + NORM_EPS)
        h_ref[...] = (xv * rms * g_ref[...]).astype(BF16)

    (h,) = _rowcall(norm_in, "norm_in", t_dim, bt, [x], [norm_g], [(d_model, BF16)])
    z, gathered = _matmul(h, wz, "nt", "proj_in", bn=1408, exchange=_Gather(other_shards))
    wl, wa, wur, wuf, wo, pp, pg = [_join_shards(g, ax) for (_, ax), g in zip(SHARDED[1:], gathered)]

    def rw_values(zs, w0_ref, wl_ref, a0_ref, wa_ref, kk_ref, ka_ref, e_ref):
        k = zs[:, c_rw:2 * c_rw]
        tw = jnp.tanh(zs[:, c4:c4 + lora])
        al = zs[:, c4 + lora:c4 + 2 * lora]
        sw = _sigmoid(w0_ref[...] + jnp.dot(tw.astype(BF16), wl_ref[...], preferred_element_type=F32))
        decay = jnp.exp(-decay_k * sw)
        a = _sigmoid(a0_ref[...] + jnp.dot(al.astype(BF16), wa_ref[...], preferred_element_type=F32))
        kk0 = k * kk_ref[...]
        nrm = jnp.sqrt(_headsum(kk0 * kk0, e_ref[...]))
        inv = 1.0 / jnp.maximum(nrm, 1e-12)
        k2 = k * (1.0 + (a - 1.0) * ka_ref[...])
        return k, tw, al, sw, decay, a, kk0, nrm, inv, k2

    def rw_prep(z_ref, mu_ref, w0_ref, wl_ref, a0_ref, wa_ref, kk_ref, ka_ref, e_ref,
                r_o, w_o, k_o, v_o, kk_o, a_o, g_o, carry):
        _first_step_zero(carry)
        zv = z_ref[...]
        zs = zv + (_shift_down(zv, carry) - zv) * mu_ref[...]
        k, tw, al, sw, decay, a, kk0, nrm, inv, k2 = rw_values(zs, w0_ref, wl_ref, a0_ref, wa_ref, kk_ref, ka_ref, e_ref)
        r_o[...] = zs[:, 0:c_rw]
        w_o[...] = decay
        k_o[...] = k2
        v_o[...] = zs[:, 2 * c_rw:3 * c_rw]
        kk_o[...] = kk0 * inv
        a_o[...] = a
        g_o[...] = zs[:, 3 * c_rw:c4]

    rw_consts = [mu, w0, wl, a0, wa, k_k, k_a, e_head]
    r_s, w_s, k_s, v_s, kk_s, a_s, g_s = _rowcall(
        rw_prep, "rwkv_prep", t_dim, bt, [(z, 0, sec)], rw_consts, [(c_rw, F32)] * 7,
        scratch=[pltpu.VMEM((1, sec), F32)])
    pairs_fwd = max(n for n in (1, 2, 4) if c_rw % (n * LANES) == 0)
    pairs_bwd = pairs_fwd
    y_s, s_all, s_fin = _scan_fwd(r_s, w_s, k_s, v_s, kk_s, a_s, 128, pairs_fwd)

    def rw_post_values(y, r, k2, v, g, lng_ref, lnb_ref, rk_ref, e):
        mean = _headsum(y, e) * (1.0 / HEAD)
        d = y - mean
        rstd = lax.rsqrt(_headsum(d * d, e) * (1.0 / HEAD) + GN_EPS)
        yh = d * rstd
        rk = _headsum(r * k2 * rk_ref[...], e)
        yo = yh * lng_ref[...] + lnb_ref[...] + rk * v
        sg = _sigmoid(g)
        return rstd, yh, rk, yo, sg

    def rw_post(y_ref, r_ref, k_ref, v_ref, g_ref, lng_ref, lnb_ref, rk_ref, e_ref, out_ref):
        g = g_ref[...]
        _, _, _, yo, sg = rw_post_values(y_ref[...], r_ref[...], k_ref[...], v_ref[...], g, lng_ref, lnb_ref, rk_ref, e_ref[...])
        out_ref[...] = (yo * g * sg).astype(BF16)

    (y_rw,) = _rowcall(rw_post, "rwkv_post", t_dim, bt, [y_s, r_s, k_s, v_s, g_s], [ln_g, ln_b, r_k, e_head], [(c_rw, BF16)])

    fl_cb = (sec + 4 * c_fox) // LANES
    hp = _round_up(h_fox, SUBLANES)
    bt_c = _pick(t_dim, 256)
    tri = (jnp.arange(bt_c)[:, None] >= jnp.arange(bt_c)[None, :]).astype(F32)

    rows8 = n_pairs * SUBLANES
    pair_rows = (jnp.arange(rows8)[:, None] // SUBLANES * 2 + jnp.arange(rows8)[:, None] % SUBLANES
                 == jnp.arange(LANES)[None, :]) & (jnp.arange(rows8)[:, None] % SUBLANES < 2)
    pair_rows = pair_rows.astype(F32)

    def fox_decay(fl_ref, bf_ref, tri_ref, sel_ref, ct_ref, cq_ref, carry):
        _first_step_zero(carry)
        lf = _log_sigmoid(fl_ref[...] + bf_ref[...])
        c = jnp.dot(tri_ref[...], lf, precision=HI, preferred_element_type=F32) + carry[...]
        carry[...] = c[bt_c - 1:bt_c, :]
        ct = jnp.dot(sel_ref[...], jnp.transpose(c), precision=HI, preferred_element_type=F32)
        ct_ref[...] = ct.reshape(n_pairs, SUBLANES, bt_c)
        for pair in range(n_pairs):
            cq_ref[pair] = c[:, 2 * pair:2 * pair + 2]

    ct, cq = pl.pallas_call(
        fox_decay, name="fox_decay", grid=(t_dim // bt_c,),
        in_specs=[pl.BlockSpec((bt_c, LANES), lambda i: (i, fl_cb)), pl.BlockSpec((1, LANES), lambda i: (0, 0)),
                  pl.BlockSpec((bt_c, bt_c), lambda i: (0, 0)), pl.BlockSpec((rows8, LANES), lambda i: (0, 0))],
        out_specs=[pl.BlockSpec((n_pairs, SUBLANES, bt_c), lambda i: (0, 0, i)),
                   pl.BlockSpec((n_pairs, bt_c, 2), lambda i: (0, i, 0))],
        out_shape=[jax.ShapeDtypeStruct((n_pairs, SUBLANES, t_dim), F32), jax.ShapeDtypeStruct((n_pairs, t_dim, 2), F32)],
        scratch_shapes=[pltpu.VMEM((1, LANES), F32)], compiler_params=_params("arbitrary"),
    )(z, b_f, tri, pair_rows)
    q_cb = sec // LANES
    k_cb, v_cb = q_cb + n_pairs, q_cb + 2 * n_pairs
    o_fox, lse = _fox_fwd(z, ct, cq, q_cb, k_cb, v_cb, n_pairs, 1024)

    def fox_post(o_ref, z_ref, out_ref):
        g = z_ref[:, 3 * c_fox:4 * c_fox]
        out_ref[...] = (o_ref[...] * g * _sigmoid(g)).astype(BF16)

    (y_fox,) = _rowcall(fox_post, "fox_post", t_dim, bt, [o_fox, (z, 1, sec)], [], [(c_fox, BF16)])

    u_rw = _matmul(y_rw, wur, "nn", "up_rwkv")
    u_fox = _matmul(y_fox, wuf, "nn", "up_fox")

    def merge(ur_ref, uf_ref, z_ref, out_ref):
        s1 = _sigmoid(z_ref[:, 0:d_model])
        s2 = _sigmoid(z_ref[:, d_model:2 * d_model])
        out_ref[...] = (s1 * ur_ref[...] + s2 * uf_ref[...]).astype(BF16)

    (merged,) = _rowcall(merge, "merge", t_dim, bt, [u_rw, u_fox, (z, 2, sec)], [], [(d_model, BF16)])
    mo = _matmul(merged, wo, "nn", "proj_out")

    def resid_norm(x_ref, mo_ref, g_ref, x1_ref, n2_ref):
        x1 = x_ref[...] + mo_ref[...]
        rms = lax.rsqrt(jnp.mean(x1 * x1, axis=-1, keepdims=True) + NORM_EPS)
        x1_ref[...] = x1
        n2_ref[...] = (x1 * rms * g_ref[...]).astype(BF16)

    x1, n2 = _rowcall(resid_norm, "resid_norm", t_dim, bt, [x, mo], [g2], [(d_model, F32), (d_model, BF16)])
    ple = _matmul(p, pp, "nn", "ple_proj")
    gl = _matmul(n2, pg, "nn", "ple_gate")

    def head(x1_ref, ple_ref, gl_ref, tgt_ref, g_ref, dx2_ref, dple_ref, dgl_ref, loss_ref, dg3_ref):
        _first_step_zero(loss_ref, dg3_ref)
        sg = _sigmoid(gl_ref[...])
        pl_v = ple_ref[...]
        x2 = x1_ref[...] + pl_v * sg
        rms = lax.rsqrt(jnp.mean(x2 * x2, axis=-1, keepdims=True) + NORM_EPS)
        xn = x2 * rms
        diff = xn * g_ref[...] - tgt_ref[...]
        loss_ref[...] += 0.5 * jnp.sum(jnp.mean(diff * diff, axis=-1, keepdims=True), axis=0, keepdims=True)
        dyf = diff * inv_d
        dg3_ref[...] += _colsum(dyf * xn)
        gy = dyf * g_ref[...]
        dx2 = rms * (gy - xn * jnp.mean(xn * gy, axis=-1, keepdims=True))
        dx2_ref[...] = dx2
        dple_ref[...] = (dx2 * sg).astype(BF16)
        dgl_ref[...] = (dx2 * pl_v * sg * (1.0 - sg)).astype(BF16)

    dx2, dple, dgl, loss, d_g3 = _rowcall(
        head, "head", t_dim, bt, [x1, ple, gl, tgt], [g3], [(d_model, F32), (d_model, BF16), (d_model, BF16)],
        acc_outs=[(1, 1), (1, d_model)])

    d_pp = _matmul(p, dple, "tn", "d_ple_proj", out_dtype=BF16)
    d_pg = _matmul(n2, dgl, "tn", "d_ple_gate", out_dtype=BF16)
    dn2 = _matmul(dgl, pg, "nt", "d_n2")

    def resid_norm_bwd(dx2_ref, dn2_ref, x1_ref, g_ref, dx1_ref, dx1b_ref, dg2_ref):
        _first_step_zero(dg2_ref)
        x1 = x1_ref[...]
        rms = lax.rsqrt(jnp.mean(x1 * x1, axis=-1, keepdims=True) + NORM_EPS)
        xn = x1 * rms
        dn = dn2_ref[...]
        dg2_ref[...] += _colsum(dn * xn)
        gy = dn * g_ref[...]
        dx1 = dx2_ref[...] + rms * (gy - xn * jnp.mean(xn * gy, axis=-1, keepdims=True))
        dx1_ref[...] = dx1
        dx1b_ref[...] = dx1.astype(BF16)

    dx1, dx1b, d_g2 = _rowcall(resid_norm_bwd, "resid_norm_bwd", t_dim, bt, [dx2, dn2, x1], [g2],
                               [(d_model, F32), (d_model, BF16)], acc_outs=[(1, d_model)])
    d_wo = _matmul(merged, dx1b, "tn", "d_w_out", out_dtype=BF16)
    dmerged = _matmul(dx1b, wo, "nt", "d_merged")

    def merge_bwd(dm_ref, ur_ref, uf_ref, z_ref, dur_ref, duf_ref, dzg_ref):
        dm = dm_ref[...]
        s1 = _sigmoid(z_ref[:, 0:d_model])
        s2 = _sigmoid(z_ref[:, d_model:2 * d_model])
        dur_ref[...] = (dm * s1).astype(BF16)
        duf_ref[...] = (dm * s2).astype(BF16)
        dzg_ref[:, 0:d_model] = (dm * ur_ref[...] * s1 * (1.0 - s1)).astype(BF16)
        dzg_ref[:, d_model:2 * d_model] = (dm * uf_ref[...] * s2 * (1.0 - s2)).astype(BF16)
        if sec > 2 * d_model:
            dzg_ref[:, 2 * d_model:sec] = jnp.zeros((dm.shape[0], sec - 2 * d_model), BF16)

    du_rw, du_fox, dz_gate = _rowcall(merge_bwd, "merge_bwd", t_dim, bt, [dmerged, u_rw, u_fox, (z, 2, sec)], [],
                                      [(d_model, BF16), (d_model, BF16), (sec, BF16)])
    d_wur = _matmul(y_rw, du_rw, "tn", "d_w_up_rwkv", out_dtype=BF16)
    d_wuf = _matmul(y_fox, du_fox, "tn", "d_w_up_fox", out_dtype=BF16)
    dy_rw = _matmul(du_rw, wur, "nt", "d_y_rwkv")
    dy_fox = _matmul(du_fox, wuf, "nt", "d_y_fox")

    def fox_post_bwd(dy_ref, o_ref, z_ref, e_ref, do_ref, dg_ref, rd_ref):
        g = z_ref[:, 3 * c_fox:4 * c_fox]
        sg = _sigmoid(g)
        dy, o = dy_ref[...], o_ref[...]
        do = dy * g * sg
        do_ref[...] = do
        dg_ref[...] = (dy * o * sg * (1.0 + g * (1.0 - sg))).astype(BF16)
        rd_ref[...] = _headsum(do.astype(BF16).astype(F32) * o, e_ref[...])

    do_fox, dg_fox, rowdot = _rowcall(fox_post_bwd, "fox_post_bwd", t_dim, bt, [dy_fox, o_fox, (z, 1, sec)], [e_head],
                                      [(c_fox, F32), (c_fox, BF16), (c_fox, F32)])
    dq_f, dk_f, dv_f, dc_t = _fox_bwd(z, ct, cq, rowdot, do_fox, lse, q_cb, k_cb, v_cb, n_pairs, 512)
    sel = (jnp.arange(hp)[:, None] // 2 * SUBLANES + jnp.arange(hp)[:, None] % 2 == jnp.arange(rows8)[None, :]).astype(F32)
    tri_rev = (jnp.arange(bt_c)[:, None] >= jnp.arange(bt_c)[None, :]).astype(F32)
    bf_col = b_f.reshape(LANES, 1)[0:hp]
    nbc = t_dim // bt_c

    def fox_decay_bwd(dc_ref, fl_ref, sel_ref, tri_ref, bf_ref, dfl_ref, dbf_ref, carry):
        _first_step_zero(carry, dbf_ref)
        dc = jnp.dot(sel_ref[...], dc_ref[...].reshape(rows8, bt_c), precision=HI, preferred_element_type=F32)
        dlf = jnp.dot(dc, tri_ref[...], precision=HI, preferred_element_type=F32) + carry[...]
        carry[...] = dlf[:, 0:1]
        flt = jnp.transpose(fl_ref[...])[0:hp, :]
        dfl = dlf * _sigmoid(-(flt + bf_ref[...]))
        head_row = lax.broadcasted_iota(jnp.int32, (hp, bt_c), 0)
        dfl = jnp.where(head_row < h_fox, dfl, 0.0)
        dbf_ref[...] += jnp.sum(dfl, axis=1, keepdims=True)
        full = jnp.concatenate([dfl, jnp.zeros((LANES - hp, bt_c), F32)], axis=0) if hp < LANES else dfl
        dfl_ref[...] = jnp.transpose(full).astype(BF16)

    dz_fl, d_bf = pl.pallas_call(
        fox_decay_bwd, name="fox_decay_bwd", grid=(nbc,),
        in_specs=[pl.BlockSpec((n_pairs, SUBLANES, bt_c), lambda i: (0, 0, nbc - 1 - i)),
                  pl.BlockSpec((bt_c, LANES), lambda i: (nbc - 1 - i, fl_cb)),
                  pl.BlockSpec(sel.shape, lambda i: (0, 0)), pl.BlockSpec((bt_c, bt_c), lambda i: (0, 0)),
                  pl.BlockSpec((hp, 1), lambda i: (0, 0))],
        out_specs=[pl.BlockSpec((bt_c, LANES), lambda i: (nbc - 1 - i, 0)), pl.BlockSpec((hp, 1), lambda i: (0, 0))],
        out_shape=[jax.ShapeDtypeStruct((t_dim, LANES), BF16), jax.ShapeDtypeStruct((hp, 1), F32)],
        scratch_shapes=[pltpu.VMEM((hp, 1), F32)], compiler_params=_params("arbitrary"),
    )(dc_t, z, sel, tri_rev, bf_col)

    def rw_post_bwd(dy_ref, y_ref, r_ref, k_ref, v_ref, g_ref, lng_ref, lnb_ref, rk_ref, e_ref,
                    dg_ref, dys_ref, dr_ref, dk_ref, dv_ref, dlng_ref, dlnb_ref, drk_ref):
        _first_step_zero(dlng_ref, dlnb_ref, drk_ref)
        e = e_ref[...]
        dy, r, k2, v, g = dy_ref[...], r_ref[...], k_ref[...], v_ref[...], g_ref[...]
        rstd, yh, rk, yo, sg = rw_post_values(y_ref[...], r, k2, v, g, lng_ref, lnb_ref, rk_ref, e)
        dg_ref[...] = dy * yo * sg * (1.0 + g * (1.0 - sg))
        dyo = dy * g * sg
        dlnb_ref[...] += _colsum(dyo)
        dlng_ref[...] += _colsum(dyo * yh)
        dyh = dyo * lng_ref[...]
        dys_ref[...] = rstd * (dyh - _headsum(dyh, e) * (1.0 / HEAD) - yh * _headsum(dyh * yh, e) * (1.0 / HEAD))
        drk = _headsum(dyo * v, e)
        dv_ref[...] = dyo * rk
        dr_ref[...] = drk * k2 * rk_ref[...]
        dk_ref[...] = drk * r * rk_ref[...]
        drk_ref[...] += _colsum(drk * r * k2)

    dg_rw, dy_s, dr_b, dk_b, dv_b, d_lng, d_lnb, d_rk = _rowcall(
        rw_post_bwd, "rwkv_post_bwd", t_dim, bt, [dy_rw, y_s, r_s, k_s, v_s, g_s], [ln_g, ln_b, r_k, e_head],
        [(c_rw, F32)] * 5, acc_outs=[(1, c_rw)] * 3)
    axis = dict(SHARDED)
    early = {"w_up_rwkv": d_wur, "w_up_fox": d_wuf, "w_out": d_wo, "ple_proj": d_pp, "ple_gate_w": d_pg}
    (dr_c, dw_c, dk_c, dv_c, dkk_c, da_c), early_recv = _scan_bwd(
        r_s, w_s, k_s, v_s, kk_s, a_s, dy_s, s_all, s_fin, 128, pairs_bwd,
        _Exchange([_split_shards(g, axis[n]) for n, g in early.items()]))
    recv = dict(zip(early, early_recv))

    def rw_prep_bwd(z_ref, dr1, dr2, dw_ref, dk1, dk2_ref, dv1, dv2, dkk_ref, da_ref, dg_ref,
                    mu_ref, w0_ref, wl_ref, a0_ref, wa_ref, kk_ref, ka_ref, e_ref,
                    dzs_ref, tw_ref, al_ref, dwr_ref, dar_ref, dmu_ref, dw0_ref, da0_ref, dkk_acc, dka_acc, carry):
        _first_step_zero(carry, dmu_ref, dw0_ref, da0_ref, dkk_acc, dka_acc)
        e = e_ref[...]
        zv = z_ref[...]
        zp = _shift_down(zv, carry)
        zs = zv + (zp - zv) * mu_ref[...]
        k, tw, al, sw, decay, a, kk0, nrm, inv, k2 = rw_values(zs, w0_ref, wl_ref, a0_ref, wa_ref, kk_ref, ka_ref, e_ref)
        dk2 = dk1[...] + dk2_ref[...]
        da = da_ref[...] + dk2 * k * ka_ref[...]
        dk = dk2 * (1.0 + (a - 1.0) * ka_ref[...])
        dka_acc[...] += _colsum(dk2 * k * (a - 1.0))
        kk = kk0 * inv
        dkk = dkk_ref[...]
        dkk0 = inv * jnp.where(nrm > 1e-12, dkk - kk * _headsum(dkk * kk, e), dkk)
        dk = dk + dkk0 * kk_ref[...]
        dkk_acc[...] += _colsum(dkk0 * k)
        da_raw = da * a * (1.0 - a)
        da0_ref[...] += _colsum(da_raw)
        dw_raw = dw_ref[...] * decay * (-decay_k) * sw * (1.0 - sw)
        dw0_ref[...] += _colsum(dw_raw)
        dar_b, dwr_b = da_raw.astype(BF16), dw_raw.astype(BF16)
        dal = lax.dot_general(dar_b, wa_ref[...], _DOT_DIMS["nt"], preferred_element_type=F32)
        dtw = lax.dot_general(dwr_b, wl_ref[...], _DOT_DIMS["nt"], preferred_element_type=F32)
        dzs_ref[:, 0:c_rw] = dr1[...] + dr2[...]
        dzs_ref[:, c_rw:2 * c_rw] = dk
        dzs_ref[:, 2 * c_rw:3 * c_rw] = dv1[...] + dv2[...]
        dzs_ref[:, 3 * c_rw:c4] = dg_ref[...]
        dzs_ref[:, c4:c4 + lora] = dtw * (1.0 - tw * tw)
        dzs_ref[:, c4 + lora:c4 + 2 * lora] = dal
        if sec > c4 + 2 * lora:
            dzs_ref[:, c4 + 2 * lora:sec] = jnp.zeros((zv.shape[0], sec - c4 - 2 * lora), F32)
        tw_ref[...] = tw.astype(BF16)
        al_ref[...] = al.astype(BF16)
        dwr_ref[...] = dwr_b
        dar_ref[...] = dar_b
        dmu_ref[...] += _colsum(dzs_ref[...] * (zp - zv))

    dzs, tw_b, al_b, dwr_b, dar_b, d_mu, d_w0, d_a0, d_kk, d_ka = _rowcall(
        rw_prep_bwd, "rwkv_prep_bwd", t_dim, bt_many,
        [(z, 0, sec), dr_c, dr_b, dw_c, dk_c, dk_b, dv_c, dv_b, dkk_c, da_c, dg_rw], rw_consts,
        [(sec, F32), (lora, BF16), (lora, BF16), (c_rw, BF16), (c_rw, BF16)],
        acc_outs=[(1, sec), (1, c_rw), (1, c_rw), (1, c_rw), (1, c_rw)], scratch=[pltpu.VMEM((1, sec), F32)])
    d_wl = _matmul(tw_b, dwr_b, "tn", "d_w_lora", out_dtype=BF16)
    d_wa = _matmul(al_b, dar_b, "tn", "d_a_lora", out_dtype=BF16)

    def shift_bwd(dzs_ref, mu_ref, dz_ref, carry):
        _first_step_zero(carry)
        d = dzs_ref[...]
        nbt = d.shape[0]
        nxt = pltpu.roll(d, nbt - 1, 0)
        rowi = lax.broadcasted_iota(jnp.int32, d.shape, 0)
        nxt = jnp.where(rowi == nbt - 1, carry[...], nxt)
        carry[...] = d[0:1, :]
        m = mu_ref[...]
        dz_ref[...] = (d * (1.0 - m) + nxt * m).astype(BF16)

    (dz_rw,) = _rowcall(shift_bwd, "shift_bwd", t_dim, bt, [dzs], [mu], [(sec, BF16)],
                        scratch=[pltpu.VMEM((1, sec), F32)], reverse=True)

    fox_parts = [dq_f.astype(BF16), dk_f.astype(BF16), dv_f.astype(BF16), dg_fox, dz_fl]
    if sec > 4 * c_fox + LANES:
        fox_parts.append(jnp.zeros((t_dim, sec - 4 * c_fox - LANES), BF16))
    dz = jnp.concatenate([dz_rw] + fox_parts + [dz_gate], axis=1)
    d_wz = _matmul(dz, h, "tn", "d_w_in", out_dtype=BF16, bm=1408)
    rw_cols, fox_cols = c4 + 2 * lora, 4 * c_fox + h_fox
    d_wi = jnp.concatenate([d_wz[:rw_cols], d_wz[sec:sec + fox_cols], d_wz[2 * sec:2 * sec + 2 * d_model]], axis=0)
    parts = d_wi.reshape(NDEV, -1, d_model)
    chip_parts = _add_pairs(parts, _sibling_swap(parts, "swap_w_in"), "add_w_in")
    dh, (recv["w_in"],) = _matmul(dz, wz, "nn", "d_h", bk=sec, exchange=_ChipExchange([chip_parts]))
    late_parts = [_split_shards(d_wl, 1), _split_shards(d_wa, 1)]

    def norm_in_bwd(dh_ref, x_ref, dx1_ref, g_ref, dx_ref, dg1_ref):
        _first_step_zero(dg1_ref)
        xv = x_ref[...]
        rms = lax.rsqrt(jnp.mean(xv * xv, axis=-1, keepdims=True) + NORM_EPS)
        xn = xv * rms
        d = dh_ref[...]
        dg1_ref[...] += _colsum(d * xn)
        gy = d * g_ref[...]
        dx_ref[...] = dx1_ref[...] + rms * (gy - xn * jnp.mean(xn * gy, axis=-1, keepdims=True))

    dx, d_g1 = _rowcall(norm_in_bwd, "norm_in_bwd", t_dim, bt, [dh, x, dx1], [norm_g], [(d_model, F32)],
                        acc_outs=[(1, d_model)])

    rep_grads = {"norm_g": d_g1, "rw_shift_mu": d_mu[:, 0:c4 + 2 * lora], "rw_w0": d_w0, "rw_a0": d_a0, "rw_k_k": d_kk,
                 "rw_k_a": d_ka, "rw_r_k": d_rk, "rw_ln_g": d_lng, "rw_ln_b": d_lnb, "fox_b_f": d_bf[0:h_fox, 0],
                 "ple_norm_g": d_g2, "final_norm_g": d_g3}
    return loss[0, 0], dx, recv, late_parts, rep_grads


def kernel(x, p, norm_g, w_in, rw_shift_mu, rw_w0, rw_w_lora_up, rw_a0, rw_a_lora_up, rw_k_k, rw_k_a, rw_r_k, rw_ln_g, rw_ln_b, fox_b_f, w_up_rwkv, w_up_fox, w_out, ple_proj, ple_gate_w, ple_norm_g, final_norm_g, loss_target, m_norm_g, m_w_in, m_rw_shift_mu, m_rw_w0, m_rw_w_lora_up, m_rw_a0, m_rw_a_lora_up, m_rw_k_k, m_rw_k_a, m_rw_r_k, m_rw_ln_g, m_rw_ln_b, m_fox_b_f, m_w_up_rwkv, m_w_up_fox, m_w_out, m_ple_proj, m_ple_gate_w, m_ple_norm_g, m_final_norm_g, v_norm_g, v_w_in, v_rw_shift_mu, v_rw_w0, v_rw_w_lora_up, v_rw_a0, v_rw_a_lora_up, v_rw_k_k, v_rw_k_a, v_rw_r_k, v_rw_ln_g, v_rw_ln_b, v_fox_b_f, v_w_up_rwkv, v_w_up_fox, v_w_out, v_ple_proj, v_ple_gate_w, v_ple_norm_g, v_final_norm_g):
    args = locals()
    w = {n: args[n] for n in WEIGHTS}
    mom = {n: args["m_" + n] for n in WEIGHTS}
    var = {n: args["v_" + n] for n in WEIGHTS}

    t_dim, d_model = x.shape[1], x.shape[2]
    c_rw, lora = rw_w0.shape[1], rw_w_lora_up.shape[1]
    h_fox = fox_b_f.shape[1]
    c_fox = h_fox * HEAD
    rw_cols, fox_cols, gate_cols = 4 * c_rw + 2 * lora, 4 * c_fox + h_fox, 2 * d_model
    sec = max(rw_cols, 4 * c_fox + LANES, _round_up(gate_cols, LANES))
    assert c_rw % LANES == 0 and c_fox % LANES == 0 and rw_cols % LANES == 0 and h_fox <= LANES and d_model % LANES == 0
    assert rw_a_lora_up.shape[1] == lora and w_in.shape[2] * NDEV == rw_cols + fox_cols + gate_cols

    assert SHARDED[0] == ("w_in", 1)
    (w_in_all,) = _all_gather([w_in[0].T.astype(BF16)], "gather_w_in")
    other_shards = [w[n][0].astype(BF16) for n, _ in SHARDED[1:]]

    def to_sections(wt):
        pad = lambda a: jnp.pad(a, ((0, sec - a.shape[0]), (0, 0)))
        return jnp.concatenate([pad(wt[:rw_cols]), pad(wt[rw_cols:rw_cols + fox_cols]), pad(wt[rw_cols + fox_cols:])])

    rep = {n: w[n] for n in REPLICATED}
    dims = (t_dim, d_model, c_rw, lora, c_fox, h_fox, sec)
    loss_local, grad_x, recv, late_parts, rep_grads = _local_step(
        x[0], p[0, 0], loss_target[0], to_sections(w_in_all.reshape(-1, d_model)), other_shards, rep, dims)

    rep_sizes = [w[n].size for n in REPLICATED]
    rep_offs = [sum(rep_sizes[:i]) for i in range(len(rep_sizes))]
    pack_rep = lambda tree: _pack_rows(jnp.concatenate([tree[n].astype(F32).reshape(-1) for n in REPLICATED]))
    small_all, recv["rw_w_lora_up"], recv["rw_a_lora_up"] = _grad_exchange([pack_rep(rep_grads)] + late_parts,
                                                                           "exchange_small")
    kinds = ("grad", "delta", "new_m", "new_v")
    outs = {}
    for n, _ in SHARDED:
        if n == "w_in":
            update = _adamw(recv[n], w_in[0].T, m_w_in[0].T, v_w_in[0].T, "adamw_" + n)
            update = [jnp.transpose(buf, (0, 2, 1)) for buf in update]
        else:
            update = _adamw(recv[n], w[n][0], mom[n][0], var[n][0], "adamw_" + n)
        for kind, buf in zip(kinds, update):
            outs[kind, n] = buf
    for kind, buf in zip(kinds, _adamw(small_all, pack_rep(w), pack_rep(mom), pack_rep(var), "adamw_replicated")):
        flat = buf.reshape(-1)
        for n, o, s in zip(REPLICATED, rep_offs, rep_sizes):
            outs[kind, n] = flat[o:o + s].reshape(w[n].shape)
    loss = lax.psum(loss_local, MESH_AXES)
    return (loss, grad_x[None], *[outs[kind, n] for kind in ("grad", "delta", "new_m", "new_v") for n in WEIGHTS])
```

```python
import functools
import math

import jax
import jax.numpy as jnp
from jax import lax
from jax.experimental import pallas as pl
from jax.experimental.pallas import tpu as pltpu

F32, BF16 = jnp.float32, jnp.bfloat16
HI = lax.Precision.HIGHEST
LANES = 128
SUBLANES = 8
HEAD = 64
NORM_EPS = 1e-6
GN_EPS = 64e-5
VMEM_LIMIT = 56 * 1024 * 1024
NDEV = 8
PACK_W = 1024
PACK_ALIGN = 16 * PACK_W
MESH_AXES = ("x", "y", "c")
MESH = pl.DeviceIdType.MESH

ADAM_LR, ADAM_B1, ADAM_B2, ADAM_EPS, ADAM_WD, ADAM_STEP = 0.001, 0.9, 0.999, 1e-08, 0.01, 10


def _round_up(n, m):
    return (n + m - 1) // m * m


def _pick(dim, pref, align=LANES):
    if dim <= pref:
        return dim
    best = None
    for cand in range(align, pref + 1, align):
        if dim % cand == 0:
            best = cand
    return dim if best is None else best


def _params(*sem):
    return pltpu.CompilerParams(dimension_semantics=sem, vmem_limit_bytes=VMEM_LIMIT)


def _sigmoid(v):
    return jax.nn.sigmoid(v)


def _log_sigmoid(v):
    return jnp.minimum(v, 0.0) - jnp.log(1.0 + jnp.exp(-jnp.abs(v)))


_DOT_DIMS = {"nn": (((1,), (0,)), ((), ())), "nt": (((1,), (1,)), ((), ())), "tn": (((0,), (0,)), ((), ()))}


def _matmul(a, b, mode, name, out_dtype=F32, bm=512, bn=1024, bk=2048, exchange=None):
    if mode == "tn":
        k_dim, m_dim = a.shape
    else:
        m_dim, k_dim = a.shape
    n_dim = b.shape[0] if mode == "nt" else b.shape[1]
    bm, bn, bk = _pick(m_dim, bm), _pick(n_dim, bn), _pick(k_dim, bk)
    nk = k_dim // bk

    nx = 0 if exchange is None else exchange.n
    grid = (m_dim // bm, n_dim // bn, nk)

    def body(*refs):
        a_ref, b_ref, o_ref = refs[0], refs[1], refs[2 + nx]
        step = [pl.program_id(d) for d in range(3)]
        if nx:
            own_scratch = 1 if nk > 1 else 0
            ex_refs = (refs[2:2 + nx], refs[3 + nx:3 + 2 * nx], refs[3 + 2 * nx + own_scratch:])
            linear = (step[0] * grid[1] + step[1]) * grid[2] + step[2]

            @pl.when(linear == 0)
            def _():
                exchange.start(*ex_refs)

            if hasattr(exchange, "forward"):
                @pl.when(linear == (grid[0] * grid[1] * grid[2]) // 2)
                def _():
                    exchange.forward(*ex_refs)

        prod = lax.dot_general(a_ref[...].astype(BF16), b_ref[...].astype(BF16), _DOT_DIMS[mode],
                               preferred_element_type=F32)
        if nk == 1:
            o_ref[...] = prod.astype(o_ref.dtype)
        else:
            acc_ref, k = refs[3 + 2 * nx], step[2]

            @pl.when(k == 0)
            def _():
                acc_ref[...] = prod

            @pl.when(k > 0)
            def _():
                acc_ref[...] += prod

            @pl.when(k == nk - 1)
            def _():
                o_ref[...] = acc_ref[...].astype(o_ref.dtype)
        if nx:
            @pl.when(linear == grid[0] * grid[1] * grid[2] - 1)
            def _():
                exchange.finish(*ex_refs)

    if mode == "tn":
        a_spec = pl.BlockSpec((bk, bm), lambda i, j, k: (k, i))
    else:
        a_spec = pl.BlockSpec((bm, bk), lambda i, j, k: (i, k))
    if mode == "nt":
        b_spec = pl.BlockSpec((bn, bk), lambda i, j, k: (j, k))
    else:
        b_spec = pl.BlockSpec((bk, bn), lambda i, j, k: (k, j))
    out_spec = pl.BlockSpec((bm, bn), lambda i, j, k: (i, j))
    out_shape = jax.ShapeDtypeStruct((m_dim, n_dim), out_dtype)
    acc_scratch = [pltpu.VMEM((bm, bn), F32)] if nk > 1 else []
    if exchange is None:
        return pl.pallas_call(
            body, name=name, grid=grid, in_specs=[a_spec, b_spec], out_specs=out_spec, out_shape=out_shape,
            scratch_shapes=acc_scratch, compiler_params=_params("parallel", "parallel", "arbitrary"),
        )(a, b)
    outs = pl.pallas_call(
        body, name=name, grid=grid, in_specs=[a_spec, b_spec] + [HBM_SPEC] * nx,
        out_specs=[out_spec] + [HBM_SPEC] * nx, out_shape=[out_shape] + exchange.out_shape,
        scratch_shapes=acc_scratch + exchange.scratch, compiler_params=_params("arbitrary", "arbitrary", "arbitrary"),
    )(a, b, *exchange.arrays)
    return outs[0], outs[1:]


def _rowcall(body, name, t_dim, bt, row_ins, const_ins, row_outs, acc_outs=(), scratch=(), reverse=False):
    nt = t_dim // bt

    def rmap(i):
        return nt - 1 - i if reverse else i

    in_specs, args = [], []
    for item in row_ins:
        arr, cb, w = item if isinstance(item, tuple) else (item, 0, item.shape[1])
        in_specs.append(pl.BlockSpec((bt, w), lambda i, cb=cb: (rmap(i), cb)))
        args.append(arr)
    for arr in const_ins:
        in_specs.append(pl.BlockSpec(arr.shape, lambda i, nd=arr.ndim: (0,) * nd))
        args.append(arr)
    out_specs = [pl.BlockSpec((bt, w), lambda i: (rmap(i), 0)) for w, _ in row_outs]
    out_shape = [jax.ShapeDtypeStruct((t_dim, w), dt) for w, dt in row_outs]
    for shp in acc_outs:
        out_specs.append(pl.BlockSpec(shp, lambda i, nd=len(shp): (0,) * nd))
        out_shape.append(jax.ShapeDtypeStruct(shp, F32))
    return pl.pallas_call(
        body, name=name, grid=(nt,), in_specs=in_specs, out_specs=out_specs, out_shape=out_shape,
        scratch_shapes=list(scratch), compiler_params=_params("arbitrary"),
    )(*args)


def _first_step_zero(*refs):
    @pl.when(pl.program_id(0) == 0)
    def _():
        for r in refs:
            r[...] = jnp.zeros_like(r)


def _colsum(v):
    return jnp.sum(v, axis=0, keepdims=True)


def _headsum(v, e):
    parts = [jnp.dot(v[:, p * LANES:(p + 1) * LANES], e, precision=HI, preferred_element_type=F32)
             for p in range(v.shape[1] // LANES)]
    return parts[0] if len(parts) == 1 else jnp.concatenate(parts, axis=1)


def _shift_down(v, carry_ref):
    bt = v.shape[0]
    prev = pltpu.roll(v, 1, 0)
    row = lax.broadcasted_iota(jnp.int32, v.shape, 0)
    prev = jnp.where(row == 0, carry_ref[...], prev)
    carry_ref[...] = v[bt - 1:bt, :]
    return prev


def _pair_consts():
    lane = lax.broadcasted_iota(jnp.int32, (1, LANES), 1)
    m0 = (lane < HEAD).astype(F32)
    m1 = 1.0 - m0
    sub = lax.broadcasted_iota(jnp.int32, (HEAD, LANES), 0)
    lane2 = lax.broadcasted_iota(jnp.int32, (HEAD, LANES), 1)
    i0 = (lane2 == sub).astype(F32)
    i1 = (lane2 == sub + HEAD).astype(F32)
    return m0, m1, i0, i1


def _head_masks():
    lane = lax.broadcasted_iota(jnp.int32, (1, LANES), 1)
    first = (lane < HEAD).astype(F32)
    return first, 1.0 - first


def _head_ones(dtype=BF16):
    lane = jnp.arange(LANES)
    return (lane[:, None] // HEAD == lane[None, :] // HEAD).astype(dtype)


def _lanesum(v):
    return jnp.sum(v, axis=1, keepdims=True)


def _split_bf16(v):
    hi = v.astype(BF16).astype(F32)
    rest = v - hi
    mid = rest.astype(BF16).astype(F32)
    return hi, mid, (rest - mid).astype(BF16).astype(F32)


def _col_pair(parts, j, i01, e_bf16):
    lhs = jnp.concatenate([(part[j:j + 1] * i01).astype(BF16) for part in parts], axis=0)
    out = jnp.dot(lhs, e_bf16, preferred_element_type=F32)
    return (out[0:HEAD] + out[HEAD:2 * HEAD]) + out[2 * HEAD:3 * HEAD]


def _row_pair(c0, c1, i0, i1):
    return _colsum(c0 * i0 + c1 * i1)


def _scan_fwd(r, w, k, v, kk, a, tc, npb):
    t_dim, c_dim = r.shape
    wb = LANES * npb
    tc = _pick(t_dim, tc, SUBLANES)

    def body(r_ref, w_ref, k_ref, v_ref, kk_ref, a_ref, e_ref, y_ref, sall_ref, sfin_ref, s_ref):
        @pl.when(pl.program_id(1) == 0)
        def _():
            s_ref[...] = jnp.zeros_like(s_ref)

        m0, m1, i0, i1 = _pair_consts()
        i01 = i0 + i1
        e = e_ref[...]
        sub8 = lax.broadcasted_iota(jnp.int32, (SUBLANES, LANES), 0)
        lanes = [slice(q * LANES, (q + 1) * LANES) for q in range(npb)]
        ng = tc // SUBLANES

        def lanesums(tiles):
            sums = _lanesum(jnp.concatenate(tiles, axis=0))
            return [sums[i * HEAD:(i + 1) * HEAD] for i in range(len(tiles))]

        def halves(s, row):
            return [s * (row * m0), s * (row * m1)]

        def group(gi, carry):
            base = pl.multiple_of(gi * SUBLANES, SUBLANES)
            rows = pl.ds(base, SUBLANES)
            s = list(carry)
            r8 = [r_ref[rows, sl] for sl in lanes]
            w8 = [w_ref[rows, sl] for sl in lanes]
            k8 = [k_ref[rows, sl] for sl in lanes]
            kk8 = [kk_ref[rows, sl] for sl in lanes]
            b8 = [kk8[q] * a_ref[rows, lanes[q]] for q in range(npb)]
            v8 = [_split_bf16(v_ref[rows, sl]) for sl in lanes]
            y8 = [jnp.zeros((SUBLANES, LANES), F32)] * npb
            for j in range(SUBLANES + 1):
                one, before = slice(j, j + 1), slice(j - 1, j)
                tiles = []
                for q in range(npb):
                    if j < SUBLANES:
                        tiles += halves(s[q], kk8[q][one])
                    if j > 0:
                        tiles += halves(s[q], r8[q][before])
                cols = lanesums(tiles)
                per = len(tiles) // npb
                for q in range(npb):
                    mine = cols[q * per:(q + 1) * per]
                    if j > 0:
                        y8[q] = jnp.where(sub8 == j - 1, _row_pair(mine[-2], mine[-1], i0, i1), y8[q])
                    if j < SUBLANES:
                        sall_ref[base + j, :, lanes[q]] = s[q]
                        sb = mine[0] * m0 + mine[1] * m1
                        s[q] = s[q] * w8[q][one] - sb * b8[q][one] + _col_pair(v8[q], j, i01, e) * k8[q][one]
            for q in range(npb):
                y_ref[rows, lanes[q]] = y8[q]
            return tuple(s)

        init = tuple(s_ref[:, q * LANES:(q + 1) * LANES] for q in range(npb))
        fin = lax.fori_loop(0, ng, group, init)
        for q in range(npb):
            s_ref[:, q * LANES:(q + 1) * LANES] = fin[q]
            sfin_ref[:, q * LANES:(q + 1) * LANES] = fin[q]

    row = pl.BlockSpec((tc, wb), lambda p, c: (c, p))
    return pl.pallas_call(
        body, name="rwkv_scan_fwd", grid=(c_dim // wb, t_dim // tc),
        in_specs=[row] * 6 + [pl.BlockSpec((LANES, LANES), lambda p, c: (0, 0))],
        out_specs=[row, pl.BlockSpec((tc, HEAD, wb), lambda p, c: (c, 0, p)), pl.BlockSpec((HEAD, wb), lambda p, c: (0, p))],
        out_shape=[jax.ShapeDtypeStruct((t_dim, c_dim), F32), jax.ShapeDtypeStruct((t_dim, HEAD, c_dim), F32),
                   jax.ShapeDtypeStruct((HEAD, c_dim), F32)],
        scratch_shapes=[pltpu.VMEM((HEAD, wb), F32)],
        compiler_params=_params("parallel", "arbitrary"),
    )(r, w, k, v, kk, a, _head_ones())


def _scan_bwd(r, w, k, v, kk, a, dy, sall, sfin, tc, npb, exchange):
    t_dim, c_dim = r.shape
    wb = LANES * npb
    tc = _pick(t_dim, tc, SUBLANES)
    nc = t_dim // tc
    nx = exchange.n
    n_blocks = c_dim // wb

    def body(*refs):
        r_ref, w_ref, k_ref, v_ref, kk_ref, a_ref, dy_ref, sall_ref, sfin_ref, e_ref = refs[:10]
        dr_ref, dw_ref, dk_ref, dv_ref, dkk_ref, da_ref = refs[10 + nx:16 + nx]
        ds_ref, sn_ref = refs[16 + 2 * nx:18 + 2 * nx]
        here_p, here_c = pl.program_id(0), pl.program_id(1)
        ex_refs = (refs[10:10 + nx], refs[16 + nx:16 + 2 * nx], refs[18 + 2 * nx:])

        @pl.when((here_p == 0) & (here_c == 0))
        def _():
            exchange.start(*ex_refs)

        @pl.when(pl.program_id(1) == 0)
        def _():
            ds_ref[...] = jnp.zeros_like(ds_ref)
            sn_ref[...] = sfin_ref[...]

        m0, m1, i0, i1 = _pair_consts()
        i01 = i0 + i1
        e = e_ref[...]
        sub8 = lax.broadcasted_iota(jnp.int32, (SUBLANES, LANES), 0)
        lanes = [slice(q * LANES, (q + 1) * LANES) for q in range(npb)]
        ng = tc // SUBLANES

        def halves(s, row):
            return [s * (row * m0), s * (row * m1)]

        def group(gi, carry):
            base = pl.multiple_of((ng - 1 - gi) * SUBLANES, SUBLANES)
            rows = pl.ds(base, SUBLANES)
            ds = list(carry)
            r8 = [r_ref[rows, sl] for sl in lanes]
            w8 = [w_ref[rows, sl] for sl in lanes]
            k8 = [k_ref[rows, sl] for sl in lanes]
            kk8 = [kk_ref[rows, sl] for sl in lanes]
            a8 = [a_ref[rows, sl] for sl in lanes]
            v8 = [_split_bf16(v_ref[rows, sl]) for sl in lanes]
            dy8 = [_split_bf16(dy_ref[rows, sl]) for sl in lanes]
            zero8 = jnp.zeros((SUBLANES, LANES), F32)
            dr8, dw8, dk8, dv8, dkk8, da8 = ([zero8] * npb for _ in range(6))
            for j in reversed(range(SUBLANES)):
                one = slice(j, j + 1)
                here = sub8 == j
                d, s_prev, tiles = [], [], []
                for q in range(npb):
                    dyb = _col_pair(dy8[q], j, i01, e)
                    s_prev.append(sall_ref[base + j, :, lanes[q]])
                    dr8[q] = jnp.where(here, _colsum(sn_ref[:, lanes[q]] * dyb), dr8[q])
                    sn_ref[:, lanes[q]] = s_prev[q]
                    d.append(ds[q] + dyb * r8[q][one])
                for q in range(npb):
                    tiles += halves(d[q], kk8[q][one] * a8[q][one])
                for q in range(npb):
                    tiles += halves(d[q], k8[q][one]) + halves(s_prev[q], kk8[q][one])
                sums = _lanesum(jnp.concatenate(tiles, axis=0))
                cols = [sums[i * HEAD:(i + 1) * HEAD] for i in range(len(tiles))]
                for q in range(npb):
                    kkr, ar = kk8[q][one], a8[q][one]
                    dsb = -(cols[2 * q] * m0 + cols[2 * q + 1] * m1)
                    ds[q] = d[q] * w8[q][one] + dsb * kkr
                    rest = cols[2 * npb + 4 * q:2 * npb + 4 * q + 4]
                    dv8[q] = jnp.where(here, _row_pair(rest[0], rest[1], i0, i1), dv8[q])
                    sb = rest[2] * m0 + rest[3] * m1
                    db = -_colsum(d[q] * sb)
                    dk8[q] = jnp.where(here, _colsum(d[q] * _col_pair(v8[q], j, i01, e)), dk8[q])
                    dw8[q] = jnp.where(here, _colsum(d[q] * s_prev[q]), dw8[q])
                    dkk8[q] = jnp.where(here, _colsum(s_prev[q] * dsb) + db * ar, dkk8[q])
                    da8[q] = jnp.where(here, db * kkr, da8[q])
            for q in range(npb):
                sl = lanes[q]
                dr_ref[rows, sl], dw_ref[rows, sl], dk_ref[rows, sl] = dr8[q], dw8[q], dk8[q]
                dv_ref[rows, sl], dkk_ref[rows, sl], da_ref[rows, sl] = dv8[q], dkk8[q], da8[q]
            return tuple(ds)

        init = tuple(ds_ref[:, q * LANES:(q + 1) * LANES] for q in range(npb))
        fin = lax.fori_loop(0, ng, group, init)
        for q in range(npb):
            ds_ref[:, q * LANES:(q + 1) * LANES] = fin[q]

        @pl.when((here_p == n_blocks - 1) & (here_c == nc - 1))
        def _():
            exchange.finish(*ex_refs)

    row = pl.BlockSpec((tc, wb), lambda p, c: (nc - 1 - c, p))
    outs = pl.pallas_call(
        body, name="rwkv_scan_bwd", grid=(n_blocks, nc),
        in_specs=[row] * 7 + [pl.BlockSpec((tc, HEAD, wb), lambda p, c: (nc - 1 - c, 0, p)),
                              pl.BlockSpec((HEAD, wb), lambda p, c: (0, p)),
                              pl.BlockSpec((LANES, LANES), lambda p, c: (0, 0))] + [HBM_SPEC] * nx,
        out_specs=[row] * 6 + [HBM_SPEC] * nx,
        out_shape=[jax.ShapeDtypeStruct((t_dim, c_dim), F32)] * 6 + exchange.out_shape,
        scratch_shapes=[pltpu.VMEM((HEAD, wb), F32)] * 2 + exchange.scratch,
        compiler_params=_params("arbitrary", "arbitrary"),
    )(r, w, k, v, kk, a, dy, sall, sfin, _head_ones(), *exchange.arrays)
    return outs[:6], outs[6:]


def _fox_fwd(z, ct, cq, q_cb, k_cb, v_cb, n_pairs, blk):
    t_dim = z.shape[0]
    blk = _pick(t_dim, blk)
    nq = t_dim // blk
    scale = HEAD ** -0.5

    def body(q_ref, k_ref, v_ref, ct_ref, cq_ref, o_ref, lse_ref):
        i = pl.program_id(1)
        rowi = lax.broadcasted_iota(jnp.int32, (blk, blk), 0)
        coli = lax.broadcasted_iota(jnp.int32, (blk, blk), 1)
        masks = _head_masks()
        qv = q_ref[...]
        qs = [(qv * mk).astype(BF16) for mk in masks]
        cqs = [cq_ref[:, hh:hh + 1] for hh in range(2)]

        def kv_step(j, carry, masked):
            rows = pl.ds(pl.multiple_of(j * blk, blk), blk)
            kb = k_ref[rows, :].astype(BF16)
            vv = v_ref[rows, :]
            stats, acc = list(carry[:4]), carry[4]
            rescale, add = 0.0, 0.0
            for hh in range(2):
                m, l = stats[2 * hh], stats[2 * hh + 1]
                s = (lax.dot_general(qs[hh], kb, _DOT_DIMS["nt"], preferred_element_type=F32) * scale
                     + (cqs[hh] - ct_ref[hh:hh + 1, rows]))
                if masked:
                    s = jnp.where(rowi >= coli, s, -jnp.inf)
                m_new = jnp.maximum(m, jnp.max(s, axis=1, keepdims=True))
                alpha = jnp.exp(m - m_new)
                pr = jnp.exp(s - m_new)
                stats[2 * hh], stats[2 * hh + 1] = m_new, l * alpha + jnp.sum(pr, axis=1, keepdims=True)
                rescale = rescale + alpha * masks[hh]
                hi = pr.astype(BF16)
                both = jnp.concatenate([hi, (pr - hi.astype(F32)).astype(BF16)], axis=1)
                vh = (vv * masks[hh]).astype(BF16)
                add = add + jnp.dot(both, jnp.concatenate([vh, vh], axis=0), preferred_element_type=F32)
            return (*stats, acc * rescale + add)

        neg, zero = jnp.full((blk, 1), -jnp.inf, F32), jnp.zeros((blk, 1), F32)
        carry = lax.fori_loop(0, i, functools.partial(kv_step, masked=False),
                              (neg, zero, neg, zero, jnp.zeros((blk, LANES), F32)))
        m0, l0, m1, l1, acc = kv_step(i, carry, True)
        o_ref[...] = acc * (masks[0] / l0 + masks[1] / l1)
        lse_ref[:, 0:1] = m0 + jnp.log(l0)
        lse_ref[:, 1:2] = m1 + jnp.log(l1)

    full = lambda cb: pl.BlockSpec((t_dim, LANES), lambda p, i, cb=cb: (0, cb + p))
    return pl.pallas_call(
        body, name="fox_attn_fwd", grid=(n_pairs, nq),
        in_specs=[pl.BlockSpec((blk, LANES), lambda p, i: (i, q_cb + p)), full(k_cb), full(v_cb),
                  pl.BlockSpec((None, SUBLANES, t_dim), lambda p, i: (p, 0, 0)),
                  pl.BlockSpec((None, blk, 2), lambda p, i: (p, i, 0))],
        out_specs=[pl.BlockSpec((blk, LANES), lambda p, i: (i, p)), pl.BlockSpec((None, blk, 2), lambda p, i: (p, i, 0))],
        out_shape=[jax.ShapeDtypeStruct((t_dim, n_pairs * LANES), F32), jax.ShapeDtypeStruct((n_pairs, t_dim, 2), F32)],
        compiler_params=_params("parallel", "arbitrary"),
    )(z, z, z, ct, cq)


def _fox_bwd(z, ct, cq, rowdot, do, lse, q_cb, k_cb, v_cb, n_pairs, blk):
    t_dim = z.shape[0]
    blk = _pick(t_dim, blk)
    nb = t_dim // blk
    scale = HEAD ** -0.5

    def body(q_ref, k_ref, v_ref, ct_ref, cq_ref, rd_ref, do_ref, lse_ref, dq_ref, dk_ref, dv_ref, dc_ref):
        j = pl.program_id(1)

        @pl.when(j == 0)
        def _():
            dq_ref[...] = jnp.zeros_like(dq_ref)

        rowi = lax.broadcasted_iota(jnp.int32, (blk, blk), 0)
        coli = lax.broadcasted_iota(jnp.int32, (blk, blk), 1)
        krows = pl.ds(pl.multiple_of(j * blk, blk), blk)
        masks = _head_masks()
        kv, vb = k_ref[...], v_ref[...].astype(BF16)
        kb = kv.astype(BF16)
        ks = [(kv * mk).astype(BF16) for mk in masks]
        cks = [ct_ref[hh:hh + 1, krows] for hh in range(2)]

        def q_step(i, carry, masked):
            dk, dv, dcs = carry[0], carry[1], list(carry[2:])
            rows = pl.ds(pl.multiple_of(i * blk, blk), blk)
            qv, dov = q_ref[rows, :], do_ref[rows, :]
            dq = 0.0
            for hh in range(2):
                qh = (qv * masks[hh]).astype(BF16)
                doh = (dov * masks[hh]).astype(BF16)
                s = (lax.dot_general(qh, kb, _DOT_DIMS["nt"], preferred_element_type=F32) * scale
                     + (cq_ref[rows, hh:hh + 1] - cks[hh]))
                pr = jnp.exp(s - lse_ref[rows, hh:hh + 1])
                if masked:
                    pr = jnp.where(rowi >= coli, pr, 0.0)
                dv = dv + lax.dot_general(pr.astype(BF16), doh, _DOT_DIMS["tn"], preferred_element_type=F32)
                dp = lax.dot_general(doh, vb, _DOT_DIMS["nt"], preferred_element_type=F32)
                ds = pr * (dp - rd_ref[rows, hh * HEAD:hh * HEAD + 1])
                dsb = ds.astype(BF16)
                dq = dq + jnp.dot(dsb, ks[hh], preferred_element_type=F32)
                dk = dk + lax.dot_general(dsb, qh, _DOT_DIMS["tn"], preferred_element_type=F32)
                dcs[hh] = dcs[hh] - _colsum(ds)
            dq_ref[rows, :] += dq * scale
            return (dk, dv, *dcs)

        zero_row = jnp.zeros((1, blk), F32)
        init = (jnp.zeros((blk, LANES), F32), jnp.zeros((blk, LANES), F32), zero_row, zero_row)
        carry = q_step(j, init, True)
        dk, dv, dc0, dc1 = lax.fori_loop(j + 1, nb, functools.partial(q_step, masked=False), carry)
        dk_ref[...] = dk * scale
        dv_ref[...] = dv
        dc_ref[0:1, :] = dc0
        dc_ref[1:2, :] = dc1
        dc_ref[2:SUBLANES, :] = jnp.zeros((SUBLANES - 2, blk), F32)

    full = lambda: pl.BlockSpec((t_dim, LANES), lambda p, j: (0, p))
    blkspec = pl.BlockSpec((blk, LANES), lambda p, j: (j, p))
    return pl.pallas_call(
        body, name="fox_attn_bwd", grid=(n_pairs, nb),
        in_specs=[pl.BlockSpec((t_dim, LANES), lambda p, j: (0, q_cb + p)),
                  pl.BlockSpec((blk, LANES), lambda p, j: (j, k_cb + p)),
                  pl.BlockSpec((blk, LANES), lambda p, j: (j, v_cb + p)),
                  pl.BlockSpec((None, SUBLANES, t_dim), lambda p, j: (p, 0, 0)),
                  pl.BlockSpec((None, t_dim, 2), lambda p, j: (p, 0, 0)), full(), full(),
                  pl.BlockSpec((None, t_dim, 2), lambda p, j: (p, 0, 0))],
        out_specs=[full(), blkspec, blkspec, pl.BlockSpec((None, SUBLANES, blk), lambda p, j: (p, 0, j))],
        out_shape=[jax.ShapeDtypeStruct((t_dim, n_pairs * LANES), F32)] * 3
                  + [jax.ShapeDtypeStruct((n_pairs, SUBLANES, t_dim), F32)],
        compiler_params=_params("parallel", "arbitrary"),
    )(z, z, z, ct, cq, rowdot, do, lse)


HBM_SPEC = pl.BlockSpec(memory_space=pltpu.HBM)


class _Gather:
    def __init__(self, shards):
        self.arrays = list(shards)
        self.n = len(self.arrays)
        self.out_shape = [jax.ShapeDtypeStruct((NDEV,) + s.shape, s.dtype) for s in self.arrays]
        self.scratch = [pltpu.SemaphoreType.DMA((7 * self.n,)), pltpu.SemaphoreType.DMA((7 * self.n,)),
                        pltpu.SemaphoreType.DMA((self.n,))]

    def _plan(self, x_refs, out_refs, sems):
        send_sems, recv_sems, local_sems = sems
        x, y, c = lax.axis_index("x"), lax.axis_index("y"), lax.axis_index("c")
        me, sibling = (x, y, c), (x, y, 1 - c)
        chips = [(1 - x, y), (x, 1 - y), (1 - x, 1 - y)]

        def copy(a, k, block, to, from_input=False):
            px, py, pc = block
            slot = out_refs[a].at[4 * px + 2 * py + pc]
            return pltpu.make_async_remote_copy(
                src_ref=x_refs[a] if from_input else slot, dst_ref=slot,
                send_sem=send_sems.at[7 * a + k], recv_sem=recv_sems.at[7 * a + k], device_id=to, device_id_type=MESH)

        mine = [pltpu.make_async_copy(x_refs[a], out_refs[a].at[4 * x + 2 * y + c], local_sems.at[a])
                for a in range(self.n)]
        first = []
        for a in range(self.n):
            first += [copy(a, 1 + j, me, (*chip, c), from_input=True) for j, chip in enumerate(chips)]
            first.append(copy(a, 0, me, sibling, from_input=True))
        over_ici = [copy(a, 1 + j, (*chip, c), me) for a in range(self.n) for j, chip in enumerate(chips)]
        passed = [copy(a, 4 + j, (*chip, c), sibling) for a in range(self.n) for j, chip in enumerate(chips)]
        from_sibling = []
        for a in range(self.n):
            from_sibling.append(copy(a, 0, sibling, me))
            from_sibling += [copy(a, 4 + j, (*chip, 1 - c), me) for j, chip in enumerate(chips)]
        return mine, first, over_ici, passed, from_sibling

    def start(self, *refs):
        mine, first, _, _, _ = self._plan(*refs)
        for cp in mine + first:
            cp.start()

    def forward(self, *refs):
        _, _, over_ici, passed, _ = self._plan(*refs)
        for arrived, onward in zip(over_ici, passed):
            arrived.wait_recv()
            onward.start()

    def finish(self, *refs):
        mine, first, _, passed, from_sibling = self._plan(*refs)
        for cp in from_sibling:
            cp.wait_recv()
        for cp in first + passed:
            cp.wait_send()
        for cp in mine:
            cp.wait()


def _all_gather(shards, name):
    ga = _Gather(shards)

    def body(*refs):
        parts = (refs[:ga.n], refs[ga.n:2 * ga.n], refs[2 * ga.n:])
        ga.start(*parts)
        ga.forward(*parts)
        ga.finish(*parts)

    return pl.pallas_call(body, name=name, out_shape=ga.out_shape, in_specs=[HBM_SPEC] * ga.n,
                          out_specs=[HBM_SPEC] * ga.n, scratch_shapes=ga.scratch)(*ga.arrays)


class _Exchange:
    def __init__(self, arrays):
        self.arrays = list(arrays)
        self.n = len(self.arrays)
        self.per_dest = [a.ndim == 3 for a in self.arrays]
        self.out_shape = [jax.ShapeDtypeStruct(a.shape if pd else (NDEV,) + a.shape, a.dtype)
                          for a, pd in zip(self.arrays, self.per_dest)]
        self.scratch = [pltpu.SemaphoreType.DMA((7 * self.n,)), pltpu.SemaphoreType.DMA((7 * self.n,)),
                        pltpu.SemaphoreType.DMA((self.n,))]

    def _copies(self, in_refs, out_refs, sems):
        send_sems, recv_sems, local_sems = sems
        x, y, c = lax.axis_index("x"), lax.axis_index("y"), lax.axis_index("c")
        me = 4 * x + 2 * y + c
        own, sends, recvs = [], [], []
        for a in range(self.n):
            mine = in_refs[a].at[me] if self.per_dest[a] else in_refs[a]
            own.append(pltpu.make_async_copy(mine, out_refs[a].at[me], local_sems.at[a]))
            for k in range(1, NDEV):
                px = 1 - x if k & 4 else x
                py = 1 - y if k & 2 else y
                pc = 1 - c if k & 1 else c
                peer = 4 * px + 2 * py + pc
                sem = dict(send_sem=send_sems.at[7 * a + k - 1], recv_sem=recv_sems.at[7 * a + k - 1],
                           device_id=(px, py, pc), device_id_type=MESH)
                src = in_refs[a].at[peer] if self.per_dest[a] else in_refs[a]
                sends.append(pltpu.make_async_remote_copy(src_ref=src, dst_ref=out_refs[a].at[me], **sem))
                recvs.append(pltpu.make_async_remote_copy(src_ref=src, dst_ref=out_refs[a].at[peer], **sem))
        return own, sends, recvs

    def start(self, in_refs, out_refs, sems):
        own, sends, _ = self._copies(in_refs, out_refs, sems)
        for cp in own + sends:
            cp.start()

    def finish(self, in_refs, out_refs, sems):
        own, sends, recvs = self._copies(in_refs, out_refs, sems)
        for cp in recvs:
            cp.wait_recv()
        for cp in sends:
            cp.wait_send()
        for cp in own:
            cp.wait()


def _grad_exchange(arrays, name):
    ex = _Exchange(arrays)

    def body(*refs):
        in_refs, out_refs, sems = refs[:ex.n], refs[ex.n:2 * ex.n], refs[2 * ex.n:]
        ex.start(in_refs, out_refs, sems)
        ex.finish(in_refs, out_refs, sems)

    return pl.pallas_call(body, name=name, out_shape=ex.out_shape, in_specs=[HBM_SPEC] * ex.n,
                          out_specs=[HBM_SPEC] * ex.n, scratch_shapes=ex.scratch)(*ex.arrays)


NCHIP = NDEV // 2


def _sibling_swap(parts, name):
    def body(p_ref, out_ref, send_sems, recv_sems):
        x, y, c = lax.axis_index("x"), lax.axis_index("y"), lax.axis_index("c")
        copies = [pltpu.make_async_remote_copy(
            src_ref=p_ref.at[2 * i + 1 - c], dst_ref=out_ref.at[i], send_sem=send_sems.at[i], recv_sem=recv_sems.at[i],
            device_id=(x, y, 1 - c), device_id_type=MESH) for i in range(NCHIP)]
        for cp in copies:
            cp.start()
        for cp in copies:
            cp.wait_recv()
        for cp in copies:
            cp.wait_send()

    return pl.pallas_call(
        body, name=name, out_shape=jax.ShapeDtypeStruct((NCHIP,) + parts.shape[1:], parts.dtype),
        in_specs=[HBM_SPEC], out_specs=HBM_SPEC,
        scratch_shapes=[pltpu.SemaphoreType.DMA((NCHIP,)), pltpu.SemaphoreType.DMA((NCHIP,))],
    )(parts)


def _add_pairs(parts, sibling, name):
    _, rows, width = parts.shape
    bc = _pick(width, 4 * LANES)

    def body(even_ref, odd_ref, sib_ref, o_ref):
        own = jnp.where(lax.axis_index("c") == 0, even_ref[...], odd_ref[...])
        o_ref[...] = (own.astype(F32) + sib_ref[...].astype(F32)).astype(o_ref.dtype)

    slot = lambda parity: pl.BlockSpec((None, rows, bc), lambda i, j: (2 * i + parity, 0, j))
    blk = pl.BlockSpec((None, rows, bc), lambda i, j: (i, 0, j))
    return pl.pallas_call(body, name=name, grid=(NCHIP, width // bc), in_specs=[slot(0), slot(1), blk], out_specs=blk,
                          out_shape=jax.ShapeDtypeStruct(sibling.shape, parts.dtype),
                          compiler_params=_params("parallel", "parallel"))(parts, parts, sibling)


class _ChipExchange:
    FLIPS = ((0, 1), (1, 0), (1, 1))

    def __init__(self, arrays):
        self.arrays = list(arrays)
        self.n = len(self.arrays)
        self.out_shape = [jax.ShapeDtypeStruct(a.shape, a.dtype) for a in self.arrays]
        self.scratch = [pltpu.SemaphoreType.DMA((3 * self.n,)), pltpu.SemaphoreType.DMA((3 * self.n,)),
                        pltpu.SemaphoreType.DMA((self.n,))]

    def _copies(self, in_refs, out_refs, sems):
        send_sems, recv_sems, local_sems = sems
        x, y, c = lax.axis_index("x"), lax.axis_index("y"), lax.axis_index("c")
        here = 2 * x + y
        own, sends, recvs = [], [], []
        for a in range(self.n):
            own.append(pltpu.make_async_copy(in_refs[a].at[here], out_refs[a].at[here], local_sems.at[a]))
            for k, (fx, fy) in enumerate(self.FLIPS):
                px = 1 - x if fx else x
                py = 1 - y if fy else y
                there = 2 * px + py
                sem = dict(send_sem=send_sems.at[3 * a + k], recv_sem=recv_sems.at[3 * a + k],
                           device_id=(px, py, c), device_id_type=MESH)
                src = in_refs[a].at[there]
                sends.append(pltpu.make_async_remote_copy(src_ref=src, dst_ref=out_refs[a].at[here], **sem))
                recvs.append(pltpu.make_async_remote_copy(src_ref=src, dst_ref=out_refs[a].at[there], **sem))
        return own, sends, recvs

    start = _Exchange.start
    finish = _Exchange.finish


def _adamw_body(p_ref, w_ref, m_ref, v_ref, g_out, d_out, m_out, v_out):
    g = p_ref[0].astype(F32)
    for d in range(1, p_ref.shape[0]):
        g = g + p_ref[d].astype(F32)
    mn = ADAM_B1 * m_ref[...] + (1.0 - ADAM_B1) * g
    vn = ADAM_B2 * v_ref[...] + (1.0 - ADAM_B2) * jnp.square(g)
    m_hat = mn / (1.0 - ADAM_B1 ** ADAM_STEP)
    v_hat = vn / (1.0 - ADAM_B2 ** ADAM_STEP)
    g_out[...] = g
    d_out[...] = -ADAM_LR * (m_hat / (jnp.sqrt(v_hat) + ADAM_EPS) + ADAM_WD * w_ref[...])
    m_out[...] = mn
    v_out[...] = vn


def _adamw(partials, w, m, v, name):
    rows, width = w.shape
    n = partials.shape[0]
    body = functools.partial(_adamw_body)
    if rows % (2 * SUBLANES) == 0:
        br = _pick(rows, 128, 2 * SUBLANES)
        grid, shape, index, index3 = rows // br, (br, width), lambda i: (i, 0), lambda i: (0, i, 0)
    else:
        bc = _pick(width, 2 * LANES)
        grid, shape, index, index3 = width // bc, (rows, bc), lambda i: (0, i), lambda i: (0, 0, i)
    blk = pl.BlockSpec(shape, index)
    return pl.pallas_call(
        body, name=name, grid=(grid,),
        in_specs=[pl.BlockSpec((n,) + shape, index3), blk, blk, blk],
        out_specs=[pl.BlockSpec((None,) + shape, index3)] * 4,
        out_shape=[jax.ShapeDtypeStruct((1, rows, width), F32)] * 4,
        compiler_params=_params("parallel"),
    )(partials, w, m, v)


def _pack_rows(flat):
    n = flat.shape[0]
    padded = _round_up(n, PACK_ALIGN)
    return jnp.pad(flat, (0, padded - n)).reshape(padded // PACK_W, PACK_W)


def _split_shards(full, axis):
    rows, cols = full.shape
    if axis == 0:
        return full.reshape(NDEV, rows // NDEV, cols)
    width = cols // NDEV
    return jnp.stack([full[:, d * width:(d + 1) * width] for d in range(NDEV)])


def _join_shards(blocks, axis):
    if axis == 0:
        return blocks.reshape(-1, blocks.shape[2])
    return jnp.concatenate([blocks[d] for d in range(NDEV)], axis=1)


SHARDED = (("w_in", 1), ("rw_w_lora_up", 1), ("rw_a_lora_up", 1), ("w_up_rwkv", 1), ("w_up_fox", 1),
           ("w_out", 0), ("ple_proj", 1), ("ple_gate_w", 0))
REPLICATED = ("norm_g", "rw_shift_mu", "rw_w0", "rw_a0", "rw_k_k", "rw_k_a", "rw_r_k", "rw_ln_g", "rw_ln_b",
              "fox_b_f", "ple_norm_g", "final_norm_g")
WEIGHTS = ("norm_g", "w_in", "rw_shift_mu", "rw_w0", "rw_w_lora_up", "rw_a0", "rw_a_lora_up", "rw_k_k", "rw_k_a",
           "rw_r_k", "rw_ln_g", "rw_ln_b", "fox_b_f", "w_up_rwkv", "w_up_fox", "w_out", "ple_proj", "ple_gate_w",
           "ple_norm_g", "final_norm_g")


def _local_step(x, p, tgt, wz, other_shards, rep, dims):
    t_dim, d_model, c_rw, lora, c_fox, h_fox, sec = dims
    bt = _pick(t_dim, 256, 2 * SUBLANES)
    bt_many = _pick(t_dim, 128, 2 * SUBLANES)
    n_pairs = c_fox // LANES
    row = lambda a: a.reshape(1, -1)
    norm_g, mu, w0, a0 = row(rep["norm_g"]), row(rep["rw_shift_mu"]), row(rep["rw_w0"]), row(rep["rw_a0"])
    k_k, k_a, r_k = row(rep["rw_k_k"]), row(rep["rw_k_a"]), row(rep["rw_r_k"])
    ln_g, ln_b = row(rep["rw_ln_g"]), row(rep["rw_ln_b"])
    g2, g3 = row(rep["ple_norm_g"]), row(rep["final_norm_g"])
    b_f = jnp.pad(row(rep["fox_b_f"]), ((0, 0), (0, LANES - h_fox)))
    e_head = _head_ones(F32)
    c4 = 4 * c_rw
    inv_d = 1.0 / d_model
    decay_k = math.exp(-0.5)

    def norm_in(x_ref, g_ref, h_ref):
        xv = x_ref[...]
        rms = lax.rsqrt(jnp.mean(xv * xv, axis=-1, keepdims=True) + NORM_EPS)
        h_ref[...] = (xv * rms * g_ref[...]).astype(BF16)

    (h,) = _rowcall(norm_in, "norm_in", t_dim, bt, [x], [norm_g], [(d_model, BF16)])
    z, gathered = _matmul(h, wz, "nt", "proj_in", bn=1408, exchange=_Gather(other_shards))
    wl, wa, wur, wuf, wo, pp, pg = [_join_shards(g, ax) for (_, ax), g in zip(SHARDED[1:], gathered)]

    def rw_values(zs, w0_ref, wl_ref, a0_ref, wa_ref, kk_ref, ka_ref, e_ref):
        k = zs[:, c_rw:2 * c_rw]
        tw = jnp.tanh(zs[:, c4:c4 + lora])
        al = zs[:, c4 + lora:c4 + 2 * lora]
        sw = _sigmoid(w0_ref[...] + jnp.dot(tw.astype(BF16), wl_ref[...], preferred_element_type=F32))
        decay = jnp.exp(-decay_k * sw)
        a = _sigmoid(a0_ref[...] + jnp.dot(al.astype(BF16), wa_ref[...], preferred_element_type=F32))
        kk0 = k * kk_ref[...]
        nrm = jnp.sqrt(_headsum(kk0 * kk0, e_ref[...]))
        inv = 1.0 / jnp.maximum(nrm, 1e-12)
        k2 = k * (1.0 + (a - 1.0) * ka_ref[...])
        return k, tw, al, sw, decay, a, kk0, nrm, inv, k2

    def rw_prep(z_ref, mu_ref, w0_ref, wl_ref, a0_ref, wa_ref, kk_ref, ka_ref, e_ref,
                r_o, w_o, k_o, v_o, kk_o, a_o, g_o, carry):
        _first_step_zero(carry)
        zv = z_ref[...]
        zs = zv + (_shift_down(zv, carry) - zv) * mu_ref[...]
        k, tw, al, sw, decay, a, kk0, nrm, inv, k2 = rw_values(zs, w0_ref, wl_ref, a0_ref, wa_ref, kk_ref, ka_ref, e_ref)
        r_o[...] = zs[:, 0:c_rw]
        w_o[...] = decay
        k_o[...] = k2
        v_o[...] = zs[:, 2 * c_rw:3 * c_rw]
        kk_o[...] = kk0 * inv
        a_o[...] = a
        g_o[...] = zs[:, 3 * c_rw:c4]

    rw_consts = [mu, w0, wl, a0, wa, k_k, k_a, e_head]
    r_s, w_s, k_s, v_s, kk_s, a_s, g_s = _rowcall(
        rw_prep, "rwkv_prep", t_dim, bt, [(z, 0, sec)], rw_consts, [(c_rw, F32)] * 7,
        scratch=[pltpu.VMEM((1, sec), F32)])
    pairs_fwd = max(n for n in (1, 2, 4) if c_rw % (n * LANES) == 0)
    pairs_bwd = pairs_fwd
    y_s, s_all, s_fin = _scan_fwd(r_s, w_s, k_s, v_s, kk_s, a_s, 128, pairs_fwd)

    def rw_post_values(y, r, k2, v, g, lng_ref, lnb_ref, rk_ref, e):
        mean = _headsum(y, e) * (1.0 / HEAD)
        d = y - mean
        rstd = lax.rsqrt(_headsum(d * d, e) * (1.0 / HEAD) + GN_EPS)
        yh = d * rstd
        rk = _headsum(r * k2 * rk_ref[...], e)
        yo = yh * lng_ref[...] + lnb_ref[...] + rk * v
        sg = _sigmoid(g)
        return rstd, yh, rk, yo, sg

    def rw_post(y_ref, r_ref, k_ref, v_ref, g_ref, lng_ref, lnb_ref, rk_ref, e_ref, out_ref):
        g = g_ref[...]
        _, _, _, yo, sg = rw_post_values(y_ref[...], r_ref[...], k_ref[...], v_ref[...], g, lng_ref, lnb_ref, rk_ref, e_ref[...])
        out_ref[...] = (yo * g * sg).astype(BF16)

    (y_rw,) = _rowcall(rw_post, "rwkv_post", t_dim, bt, [y_s, r_s, k_s, v_s, g_s], [ln_g, ln_b, r_k, e_head], [(c_rw, BF16)])

    fl_cb = (sec + 4 * c_fox) // LANES
    hp = _round_up(h_fox, SUBLANES)
    bt_c = _pick(t_dim, 256)
    tri = (jnp.arange(bt_c)[:, None] >= jnp.arange(bt_c)[None, :]).astype(F32)

    rows8 = n_pairs * SUBLANES
    pair_rows = (jnp.arange(rows8)[:, None] // SUBLANES * 2 + jnp.arange(rows8)[:, None] % SUBLANES
                 == jnp.arange(LANES)[None, :]) & (jnp.arange(rows8)[:, None] % SUBLANES < 2)
    pair_rows = pair_rows.astype(F32)

    def fox_decay(fl_ref, bf_ref, tri_ref, sel_ref, ct_ref, cq_ref, carry):
        _first_step_zero(carry)
        lf = _log_sigmoid(fl_ref[...] + bf_ref[...])
        c = jnp.dot(tri_ref[...], lf, precision=HI, preferred_element_type=F32) + carry[...]
        carry[...] = c[bt_c - 1:bt_c, :]
        ct = jnp.dot(sel_ref[...], jnp.transpose(c), precision=HI, preferred_element_type=F32)
        ct_ref[...] = ct.reshape(n_pairs, SUBLANES, bt_c)
        for pair in range(n_pairs):
            cq_ref[pair] = c[:, 2 * pair:2 * pair + 2]

    ct, cq = pl.pallas_call(
        fox_decay, name="fox_decay", grid=(t_dim // bt_c,),
        in_specs=[pl.BlockSpec((bt_c, LANES), lambda i: (i, fl_cb)), pl.BlockSpec((1, LANES), lambda i: (0, 0)),
                  pl.BlockSpec((bt_c, bt_c), lambda i: (0, 0)), pl.BlockSpec((rows8, LANES), lambda i: (0, 0))],
        out_specs=[pl.BlockSpec((n_pairs, SUBLANES, bt_c), lambda i: (0, 0, i)),
                   pl.BlockSpec((n_pairs, bt_c, 2), lambda i: (0, i, 0))],
        out_shape=[jax.ShapeDtypeStruct((n_pairs, SUBLANES, t_dim), F32), jax.ShapeDtypeStruct((n_pairs, t_dim, 2), F32)],
        scratch_shapes=[pltpu.VMEM((1, LANES), F32)], compiler_params=_params("arbitrary"),
    )(z, b_f, tri, pair_rows)
    q_cb = sec // LANES
    k_cb, v_cb = q_cb + n_pairs, q_cb + 2 * n_pairs
    o_fox, lse = _fox_fwd(z, ct, cq, q_cb, k_cb, v_cb, n_pairs, 1024)

    def fox_post(o_ref, z_ref, out_ref):
        g = z_ref[:, 3 * c_fox:4 * c_fox]
        out_ref[...] = (o_ref[...] * g * _sigmoid(g)).astype(BF16)

    (y_fox,) = _rowcall(fox_post, "fox_post", t_dim, bt, [o_fox, (z, 1, sec)], [], [(c_fox, BF16)])

    u_rw = _matmul(y_rw, wur, "nn", "up_rwkv")
    u_fox = _matmul(y_fox, wuf, "nn", "up_fox")

    def merge(ur_ref, uf_ref, z_ref, out_ref):
        s1 = _sigmoid(z_ref[:, 0:d_model])
        s2 = _sigmoid(z_ref[:, d_model:2 * d_model])
        out_ref[...] = (s1 * ur_ref[...] + s2 * uf_ref[...]).astype(BF16)

    (merged,) = _rowcall(merge, "merge", t_dim, bt, [u_rw, u_fox, (z, 2, sec)], [], [(d_model, BF16)])
    mo = _matmul(merged, wo, "nn", "proj_out")

    def resid_norm(x_ref, mo_ref, g_ref, x1_ref, n2_ref):
        x1 = x_ref[...] + mo_ref[...]
        rms = lax.rsqrt(jnp.mean(x1 * x1, axis=-1, keepdims=True) + NORM_EPS)
        x1_ref[...] = x1
        n2_ref[...] = (x1 * rms * g_ref[...]).astype(BF16)

    x1, n2 = _rowcall(resid_norm, "resid_norm", t_dim, bt, [x, mo], [g2], [(d_model, F32), (d_model, BF16)])
    ple = _matmul(p, pp, "nn", "ple_proj")
    gl = _matmul(n2, pg, "nn", "ple_gate")

    def head(x1_ref, ple_ref, gl_ref, tgt_ref, g_ref, dx2_ref, dple_ref, dgl_ref, loss_ref, dg3_ref):
        _first_step_zero(loss_ref, dg3_ref)
        sg = _sigmoid(gl_ref[...])
        pl_v = ple_ref[...]
        x2 = x1_ref[...] + pl_v * sg
        rms = lax.rsqrt(jnp.mean(x2 * x2, axis=-1, keepdims=True) + NORM_EPS)
        xn = x2 * rms
        diff = xn * g_ref[...] - tgt_ref[...]
        loss_ref[...] += 0.5 * jnp.sum(jnp.mean(diff * diff, axis=-1, keepdims=True), axis=0, keepdims=True)
        dyf = diff * inv_d
        dg3_ref[...] += _colsum(dyf * xn)
        gy = dyf * g_ref[...]
        dx2 = rms * (gy - xn * jnp.mean(xn * gy, axis=-1, keepdims=True))
        dx2_ref[...] = dx2
        dple_ref[...] = (dx2 * sg).astype(BF16)
        dgl_ref[...] = (dx2 * pl_v * sg * (1.0 - sg)).astype(BF16)

    dx2, dple, dgl, loss, d_g3 = _rowcall(
        head, "head", t_dim, bt, [x1, ple, gl, tgt], [g3], [(d_model, F32), (d_model, BF16), (d_model, BF16)],
        acc_outs=[(1, 1), (1, d_model)])

    d_pp = _matmul(p, dple, "tn", "d_ple_proj", out_dtype=BF16)
    d_pg = _matmul(n2, dgl, "tn", "d_ple_gate", out_dtype=BF16)
    dn2 = _matmul(dgl, pg, "nt", "d_n2")

    def resid_norm_bwd(dx2_ref, dn2_ref, x1_ref, g_ref, dx1_ref, dx1b_ref, dg2_ref):
        _first_step_zero(dg2_ref)
        x1 = x1_ref[...]
        rms = lax.rsqrt(jnp.mean(x1 * x1, axis=-1, keepdims=True) + NORM_EPS)
        xn = x1 * rms
        dn = dn2_ref[...]
        dg2_ref[...] += _colsum(dn * xn)
        gy = dn * g_ref[...]
        dx1 = dx2_ref[...] + rms * (gy - xn * jnp.mean(xn * gy, axis=-1, keepdims=True))
        dx1_ref[...] = dx1
        dx1b_ref[...] = dx1.astype(BF16)

    dx1, dx1b, d_g2 = _rowcall(resid_norm_bwd, "resid_norm_bwd", t_dim, bt, [dx2, dn2, x1], [g2],
                               [(d_model, F32), (d_model, BF16)], acc_outs=[(1, d_model)])
    d_wo = _matmul(merged, dx1b, "tn", "d_w_out", out_dtype=BF16)
    dmerged = _matmul(dx1b, wo, "nt", "d_merged")

    def merge_bwd(dm_ref, ur_ref, uf_ref, z_ref, dur_ref, duf_ref, dzg_ref):
        dm = dm_ref[...]
        s1 = _sigmoid(z_ref[:, 0:d_model])
        s2 = _sigmoid(z_ref[:, d_model:2 * d_model])
        dur_ref[...] = (dm * s1).astype(BF16)
        duf_ref[...] = (dm * s2).astype(BF16)
        dzg_ref[:, 0:d_model] = (dm * ur_ref[...] * s1 * (1.0 - s1)).astype(BF16)
        dzg_ref[:, d_model:2 * d_model] = (dm * uf_ref[...] * s2 * (1.0 - s2)).astype(BF16)
        if sec > 2 * d_model:
            dzg_ref[:, 2 * d_model:sec] = jnp.zeros((dm.shape[0], sec - 2 * d_model), BF16)

    du_rw, du_fox, dz_gate = _rowcall(merge_bwd, "merge_bwd", t_dim, bt, [dmerged, u_rw, u_fox, (z, 2, sec)], [],
                                      [(d_model, BF16), (d_model, BF16), (sec, BF16)])
    d_wur = _matmul(y_rw, du_rw, "tn", "d_w_up_rwkv", out_dtype=BF16)
    d_wuf = _matmul(y_fox, du_fox, "tn", "d_w_up_fox", out_dtype=BF16)
    dy_rw = _matmul(du_rw, wur, "nt", "d_y_rwkv")
    dy_fox = _matmul(du_fox, wuf, "nt", "d_y_fox")

    def fox_post_bwd(dy_ref, o_ref, z_ref, e_ref, do_ref, dg_ref, rd_ref):
        g = z_ref[:, 3 * c_fox:4 * c_fox]
        sg = _sigmoid(g)
        dy, o = dy_ref[...], o_ref[...]
        do = dy * g * sg
        do_ref[...] = do
        dg_ref[...] = (dy * o * sg * (1.0 + g * (1.0 - sg))).astype(BF16)
        rd_ref[...] = _headsum(do.astype(BF16).astype(F32) * o, e_ref[...])

    do_fox, dg_fox, rowdot = _rowcall(fox_post_bwd, "fox_post_bwd", t_dim, bt, [dy_fox, o_fox, (z, 1, sec)], [e_head],
                                      [(c_fox, F32), (c_fox, BF16), (c_fox, F32)])
    dq_f, dk_f, dv_f, dc_t = _fox_bwd(z, ct, cq, rowdot, do_fox, lse, q_cb, k_cb, v_cb, n_pairs, 512)
    sel = (jnp.arange(hp)[:, None] // 2 * SUBLANES + jnp.arange(hp)[:, None] % 2 == jnp.arange(rows8)[None, :]).astype(F32)
    tri_rev = (jnp.arange(bt_c)[:, None] >= jnp.arange(bt_c)[None, :]).astype(F32)
    bf_col = b_f.reshape(LANES, 1)[0:hp]
    nbc = t_dim // bt_c

    def fox_decay_bwd(dc_ref, fl_ref, sel_ref, tri_ref, bf_ref, dfl_ref, dbf_ref, carry):
        _first_step_zero(carry, dbf_ref)
        dc = jnp.dot(sel_ref[...], dc_ref[...].reshape(rows8, bt_c), precision=HI, preferred_element_type=F32)
        dlf = jnp.dot(dc, tri_ref[...], precision=HI, preferred_element_type=F32) + carry[...]
        carry[...] = dlf[:, 0:1]
        flt = jnp.transpose(fl_ref[...])[0:hp, :]
        dfl = dlf * _sigmoid(-(flt + bf_ref[...]))
        head_row = lax.broadcasted_iota(jnp.int32, (hp, bt_c), 0)
        dfl = jnp.where(head_row < h_fox, dfl, 0.0)
        dbf_ref[...] += jnp.sum(dfl, axis=1, keepdims=True)
        full = jnp.concatenate([dfl, jnp.zeros((LANES - hp, bt_c), F32)], axis=0) if hp < LANES else dfl
        dfl_ref[...] = jnp.transpose(full).astype(BF16)

    dz_fl, d_bf = pl.pallas_call(
        fox_decay_bwd, name="fox_decay_bwd", grid=(nbc,),
        in_specs=[pl.BlockSpec((n_pairs, SUBLANES, bt_c), lambda i: (0, 0, nbc - 1 - i)),
                  pl.BlockSpec((bt_c, LANES), lambda i: (nbc - 1 - i, fl_cb)),
                  pl.BlockSpec(sel.shape, lambda i: (0, 0)), pl.BlockSpec((bt_c, bt_c), lambda i: (0, 0)),
                  pl.BlockSpec((hp, 1), lambda i: (0, 0))],
        out_specs=[pl.BlockSpec((bt_c, LANES), lambda i: (nbc - 1 - i, 0)), pl.BlockSpec((hp, 1), lambda i: (0, 0))],
        out_shape=[jax.ShapeDtypeStruct((t_dim, LANES), BF16), jax.ShapeDtypeStruct((hp, 1), F32)],
        scratch_shapes=[pltpu.VMEM((hp, 1), F32)], compiler_params=_params("arbitrary"),
    )(dc_t, z, sel, tri_rev, bf_col)

    def rw_post_bwd(dy_ref, y_ref, r_ref, k_ref, v_ref, g_ref, lng_ref, lnb_ref, rk_ref, e_ref,
                    dg_ref, dys_ref, dr_ref, dk_ref, dv_ref, dlng_ref, dlnb_ref, drk_ref):
        _first_step_zero(dlng_ref, dlnb_ref, drk_ref)
        e = e_ref[...]
        dy, r, k2, v, g = dy_ref[...], r_ref[...], k_ref[...], v_ref[...], g_ref[...]
        rstd, yh, rk, yo, sg = rw_post_values(y_ref[...], r, k2, v, g, lng_ref, lnb_ref, rk_ref, e)
        dg_ref[...] = dy * yo * sg * (1.0 + g * (1.0 - sg))
        dyo = dy * g * sg
        dlnb_ref[...] += _colsum(dyo)
        dlng_ref[...] += _colsum(dyo * yh)
        dyh = dyo * lng_ref[...]
        dys_ref[...] = rstd * (dyh - _headsum(dyh, e) * (1.0 / HEAD) - yh * _headsum(dyh * yh, e) * (1.0 / HEAD))
        drk = _headsum(dyo * v, e)
        dv_ref[...] = dyo * rk
        dr_ref[...] = drk * k2 * rk_ref[...]
        dk_ref[...] = drk * r * rk_ref[...]
        drk_ref[...] += _colsum(drk * r * k2)

    dg_rw, dy_s, dr_b, dk_b, dv_b, d_lng, d_lnb, d_rk = _rowcall(
        rw_post_bwd, "rwkv_post_bwd", t_dim, bt, [dy_rw, y_s, r_s, k_s, v_s, g_s], [ln_g, ln_b, r_k, e_head],
        [(c_rw, F32)] * 5, acc_outs=[(1, c_rw)] * 3)
    axis = dict(SHARDED)
    early = {"w_up_rwkv": d_wur, "w_up_fox": d_wuf, "w_out": d_wo, "ple_proj": d_pp, "ple_gate_w": d_pg}
    (dr_c, dw_c, dk_c, dv_c, dkk_c, da_c), early_recv = _scan_bwd(
        r_s, w_s, k_s, v_s, kk_s, a_s, dy_s, s_all, s_fin, 128, pairs_bwd,
        _Exchange([_split_shards(g, axis[n]) for n, g in early.items()]))
    recv = dict(zip(early, early_recv))

    def rw_prep_bwd(z_ref, dr1, dr2, dw_ref, dk1, dk2_ref, dv1, dv2, dkk_ref, da_ref, dg_ref,
                    mu_ref, w0_ref, wl_ref, a0_ref, wa_ref, kk_ref, ka_ref, e_ref,
                    dzs_ref, tw_ref, al_ref, dwr_ref, dar_ref, dmu_ref, dw0_ref, da0_ref, dkk_acc, dka_acc, carry):
        _first_step_zero(carry, dmu_ref, dw0_ref, da0_ref, dkk_acc, dka_acc)
        e = e_ref[...]
        zv = z_ref[...]
        zp = _shift_down(zv, carry)
        zs = zv + (zp - zv) * mu_ref[...]
        k, tw, al, sw, decay, a, kk0, nrm, inv, k2 = rw_values(zs, w0_ref, wl_ref, a0_ref, wa_ref, kk_ref, ka_ref, e_ref)
        dk2 = dk1[...] + dk2_ref[...]
        da = da_ref[...] + dk2 * k * ka_ref[...]
        dk = dk2 * (1.0 + (a - 1.0) * ka_ref[...])
        dka_acc[...] += _colsum(dk2 * k * (a - 1.0))
        kk = kk0 * inv
        dkk = dkk_ref[...]
        dkk0 = inv * jnp.where(nrm > 1e-12, dkk - kk * _headsum(dkk * kk, e), dkk)
        dk = dk + dkk0 * kk_ref[...]
        dkk_acc[...] += _colsum(dkk0 * k)
        da_raw = da * a * (1.0 - a)
        da0_ref[...] += _colsum(da_raw)
        dw_raw = dw_ref[...] * decay * (-decay_k) * sw * (1.0 - sw)
        dw0_ref[...] += _colsum(dw_raw)
        dar_b, dwr_b = da_raw.astype(BF16), dw_raw.astype(BF16)
        dal = lax.dot_general(dar_b, wa_ref[...], _DOT_DIMS["nt"], preferred_element_type=F32)
        dtw = lax.dot_general(dwr_b, wl_ref[...], _DOT_DIMS["nt"], preferred_element_type=F32)
        dzs_ref[:, 0:c_rw] = dr1[...] + dr2[...]
        dzs_ref[:, c_rw:2 * c_rw] = dk
        dzs_ref[:, 2 * c_rw:3 * c_rw] = dv1[...] + dv2[...]
        dzs_ref[:, 3 * c_rw:c4] = dg_ref[...]
        dzs_ref[:, c4:c4 + lora] = dtw * (1.0 - tw * tw)
        dzs_ref[:, c4 + lora:c4 + 2 * lora] = dal
        if sec > c4 + 2 * lora:
            dzs_ref[:, c4 + 2 * lora:sec] = jnp.zeros((zv.shape[0], sec - c4 - 2 * lora), F32)
        tw_ref[...] = tw.astype(BF16)
        al_ref[...] = al.astype(BF16)
        dwr_ref[...] = dwr_b
        dar_ref[...] = dar_b
        dmu_ref[...] += _colsum(dzs_ref[...] * (zp - zv))

    dzs, tw_b, al_b, dwr_b, dar_b, d_mu, d_w0, d_a0, d_kk, d_ka = _rowcall(
        rw_prep_bwd, "rwkv_prep_bwd", t_dim, bt_many,
        [(z, 0, sec), dr_c, dr_b, dw_c, dk_c, dk_b, dv_c, dv_b, dkk_c, da_c, dg_rw], rw_consts,
        [(sec, F32), (lora, BF16), (lora, BF16), (c_rw, BF16), (c_rw, BF16)],
        acc_outs=[(1, sec), (1, c_rw), (1, c_rw), (1, c_rw), (1, c_rw)], scratch=[pltpu.VMEM((1, sec), F32)])
    d_wl = _matmul(tw_b, dwr_b, "tn", "d_w_lora", out_dtype=BF16)
    d_wa = _matmul(al_b, dar_b, "tn", "d_a_lora", out_dtype=BF16)

    def shift_bwd(dzs_ref, mu_ref, dz_ref, carry):
        _first_step_zero(carry)
        d = dzs_ref[...]
        nbt = d.shape[0]
        nxt = pltpu.roll(d, nbt - 1, 0)
        rowi = lax.broadcasted_iota(jnp.int32, d.shape, 0)
        nxt = jnp.where(rowi == nbt - 1, carry[...], nxt)
        carry[...] = d[0:1, :]
        m = mu_ref[...]
        dz_ref[...] = (d * (1.0 - m) + nxt * m).astype(BF16)

    (dz_rw,) = _rowcall(shift_bwd, "shift_bwd", t_dim, bt, [dzs], [mu], [(sec, BF16)],
                        scratch=[pltpu.VMEM((1, sec), F32)], reverse=True)

    fox_parts = [dq_f.astype(BF16), dk_f.astype(BF16), dv_f.astype(BF16), dg_fox, dz_fl]
    if sec > 4 * c_fox + LANES:
        fox_parts.append(jnp.zeros((t_dim, sec - 4 * c_fox - LANES), BF16))
    dz = jnp.concatenate([dz_rw] + fox_parts + [dz_gate], axis=1)
    d_wz = _matmul(dz, h, "tn", "d_w_in", out_dtype=BF16, bm=1408)
    rw_cols, fox_cols = c4 + 2 * lora, 4 * c_fox + h_fox
    bounds = (0, rw_cols, rw_cols + fox_cols, rw_cols + fox_cols + 2 * d_model)
    shard_rows = bounds[3] // NDEV

    def shard(d):
        lo, hi = d * shard_rows, (d + 1) * shard_rows
        pieces = [d_wz[s * sec + max(lo, bounds[s]) - bounds[s]:s * sec + min(hi, bounds[s + 1]) - bounds[s]]
                  for s in range(3) if max(lo, bounds[s]) < min(hi, bounds[s + 1])]
        return pieces[0] if len(pieces) == 1 else jnp.concatenate(pieces, axis=0)

    parts = jnp.stack([shard(d) for d in range(NDEV)])
    chip_parts = _add_pairs(parts, _sibling_swap(parts, "swap_w_in"), "add_w_in")
    dh, (recv["w_in"],) = _matmul(dz, wz, "nn", "d_h", bk=sec, exchange=_ChipExchange([chip_parts]))
    late_parts = [_split_shards(d_wl, 1), _split_shards(d_wa, 1)]

    def norm_in_bwd(dh_ref, x_ref, dx1_ref, g_ref, dx_ref, dg1_ref):
        _first_step_zero(dg1_ref)
        xv = x_ref[...]
        rms = lax.rsqrt(jnp.mean(xv * xv, axis=-1, keepdims=True) + NORM_EPS)
        xn = xv * rms
        d = dh_ref[...]
        dg1_ref[...] += _colsum(d * xn)
        gy = d * g_ref[...]
        dx_ref[...] = dx1_ref[...] + rms * (gy - xn * jnp.mean(xn * gy, axis=-1, keepdims=True))

    dx, d_g1 = _rowcall(norm_in_bwd, "norm_in_bwd", t_dim, bt, [dh, x, dx1], [norm_g], [(d_model, F32)],
                        acc_outs=[(1, d_model)])

    rep_grads = {"norm_g": d_g1, "rw_shift_mu": d_mu[:, 0:c4 + 2 * lora], "rw_w0": d_w0, "rw_a0": d_a0, "rw_k_k": d_kk,
                 "rw_k_a": d_ka, "rw_r_k": d_rk, "rw_ln_g": d_lng, "rw_ln_b": d_lnb, "fox_b_f": d_bf[0:h_fox, 0],
                 "ple_norm_g": d_g2, "final_norm_g": d_g3}
    return loss[0, 0], dx, recv, late_parts, rep_grads


def kernel(x, p, norm_g, w_in, rw_shift_mu, rw_w0, rw_w_lora_up, rw_a0, rw_a_lora_up, rw_k_k, rw_k_a, rw_r_k, rw_ln_g, rw_ln_b, fox_b_f, w_up_rwkv, w_up_fox, w_out, ple_proj, ple_gate_w, ple_norm_g, final_norm_g, loss_target, m_norm_g, m_w_in, m_rw_shift_mu, m_rw_w0, m_rw_w_lora_up, m_rw_a0, m_rw_a_lora_up, m_rw_k_k, m_rw_k_a, m_rw_r_k, m_rw_ln_g, m_rw_ln_b, m_fox_b_f, m_w_up_rwkv, m_w_up_fox, m_w_out, m_ple_proj, m_ple_gate_w, m_ple_norm_g, m_final_norm_g, v_norm_g, v_w_in, v_rw_shift_mu, v_rw_w0, v_rw_w_lora_up, v_rw_a0, v_rw_a_lora_up, v_rw_k_k, v_rw_k_a, v_rw_r_k, v_rw_ln_g, v_rw_ln_b, v_fox_b_f, v_w_up_rwkv, v_w_up_fox, v_w_out, v_ple_proj, v_ple_gate_w, v_ple_norm_g, v_final_norm_g):
    args = locals()
    w = {n: args[n] for n in WEIGHTS}
    mom = {n: args["m_" + n] for n in WEIGHTS}
    var = {n: args["v_" + n] for n in WEIGHTS}

    t_dim, d_model = x.shape[1], x.shape[2]
    c_rw, lora = rw_w0.shape[1], rw_w_lora_up.shape[1]
    h_fox = fox_b_f.shape[1]
    c_fox = h_fox * HEAD
    rw_cols, fox_cols, gate_cols = 4 * c_rw + 2 * lora, 4 * c_fox + h_fox, 2 * d_model
    sec = max(rw_cols, 4 * c_fox + LANES, _round_up(gate_cols, LANES))
    assert c_rw % LANES == 0 and c_fox % LANES == 0 and rw_cols % LANES == 0 and h_fox <= LANES and d_model % LANES == 0
    assert rw_a_lora_up.shape[1] == lora and w_in.shape[2] * NDEV == rw_cols + fox_cols + gate_cols

    assert SHARDED[0] == ("w_in", 1)
    (w_in_all,) = _all_gather([w_in[0].T.astype(BF16)], "gather_w_in")
    other_shards = [w[n][0].astype(BF16) for n, _ in SHARDED[1:]]

    def to_sections(shards_t):
        shard_rows = shards_t.shape[1]
        bounds = (0, rw_cols, rw_cols + fox_cols, rw_cols + fox_cols + gate_cols)
        pieces = []
        for s in range(3):
            for d in range(NDEV):
                lo, hi = max(bounds[s], d * shard_rows), min(bounds[s + 1], (d + 1) * shard_rows)
                if lo < hi:
                    pieces.append(shards_t[d, lo - d * shard_rows:hi - d * shard_rows])
            if sec > bounds[s + 1] - bounds[s]:
                pieces.append(jnp.zeros((sec - (bounds[s + 1] - bounds[s]), d_model), shards_t.dtype))
        return jnp.concatenate(pieces)

    rep = {n: w[n] for n in REPLICATED}
    dims = (t_dim, d_model, c_rw, lora, c_fox, h_fox, sec)
    loss_local, grad_x, recv, late_parts, rep_grads = _local_step(
        x[0], p[0, 0], loss_target[0], to_sections(w_in_all), other_shards, rep, dims)

    rep_sizes = [w[n].size for n in REPLICATED]
    rep_offs = [sum(rep_sizes[:i]) for i in range(len(rep_sizes))]
    pack_rep = lambda tree: _pack_rows(jnp.concatenate([tree[n].astype(F32).reshape(-1) for n in REPLICATED]))
    small_all, recv["rw_w_lora_up"], recv["rw_a_lora_up"] = _grad_exchange([pack_rep(rep_grads)] + late_parts,
                                                                           "exchange_small")
    kinds = ("grad", "delta", "new_m", "new_v")
    outs = {}
    for n, _ in SHARDED:
        if n == "w_in":
            update = _adamw(recv[n], w_in[0].T, m_w_in[0].T, v_w_in[0].T, "adamw_" + n)
            update = [jnp.transpose(buf, (0, 2, 1)) for buf in update]
        else:
            update = _adamw(recv[n], w[n][0], mom[n][0], var[n][0], "adamw_" + n)
        for kind, buf in zip(kinds, update):
            outs[kind, n] = buf
    for kind, buf in zip(kinds, _adamw(small_all, pack_rep(w), pack_rep(mom), pack_rep(var), "adamw_replicated")):
        flat = buf.reshape(-1)
        for n, o, s in zip(REPLICATED, rep_offs, rep_sizes):
            outs[kind, n] = flat[o:o + s].reshape(w[n].shape)
    loss = lax.psum(loss_local, MESH_AXES)
    return (loss, grad_x[None], *[outs[kind, n] for kind in ("grad", "delta", "new_m", "new_v") for n in WEIGHTS])
```

```python
import functools
import math

import jax
import jax.numpy as jnp
from jax import lax
from jax.experimental import pallas as pl
from jax.experimental.pallas import tpu as pltpu

F32, BF16 = jnp.float32, jnp.bfloat16
HI = lax.Precision.HIGHEST
LANES = 128
SUBLANES = 8
HEAD = 64
NORM_EPS = 1e-6
GN_EPS = 64e-5
VMEM_LIMIT = 56 * 1024 * 1024
NDEV = 8
PACK_W = 1024
PACK_ALIGN = 16 * PACK_W
MESH_AXES = ("x", "y", "c")
MESH = pl.DeviceIdType.MESH

ADAM_LR, ADAM_B1, ADAM_B2, ADAM_EPS, ADAM_WD, ADAM_STEP = 0.001, 0.9, 0.999, 1e-08, 0.01, 10


def _round_up(n, m):
    return (n + m - 1) // m * m


def _pick(dim, pref, align=LANES):
    if dim <= pref:
        return dim
    best = None
    for cand in range(align, pref + 1, align):
        if dim % cand == 0:
            best = cand
    return dim if best is None else best


def _params(*sem):
    return pltpu.CompilerParams(dimension_semantics=sem, vmem_limit_bytes=VMEM_LIMIT)


def _sigmoid(v):
    return jax.nn.sigmoid(v)


def _log_sigmoid(v):
    return jnp.minimum(v, 0.0) - jnp.log(1.0 + jnp.exp(-jnp.abs(v)))


_DOT_DIMS = {"nn": (((1,), (0,)), ((), ())), "nt": (((1,), (1,)), ((), ())), "tn": (((0,), (0,)), ((), ()))}


def _matmul(a, b, mode, name, out_dtype=F32, bm=512, bn=1024, bk=2048, exchange=None):
    if mode == "tn":
        k_dim, m_dim = a.shape
    else:
        m_dim, k_dim = a.shape
    n_dim = b.shape[0] if mode == "nt" else b.shape[1]
    bm, bn, bk = _pick(m_dim, bm), _pick(n_dim, bn), _pick(k_dim, bk)
    nk = k_dim // bk

    nx = 0 if exchange is None else exchange.n
    grid = (m_dim // bm, n_dim // bn, nk)

    def body(*refs):
        a_ref, b_ref, o_ref = refs[0], refs[1], refs[2 + nx]
        step = [pl.program_id(d) for d in range(3)]
        if nx:
            own_scratch = 1 if nk > 1 else 0
            ex_refs = (refs[2:2 + nx], refs[3 + nx:3 + 2 * nx], refs[3 + 2 * nx + own_scratch:])
            linear = (step[0] * grid[1] + step[1]) * grid[2] + step[2]

            @pl.when(linear == 0)
            def _():
                exchange.start(*ex_refs)

            if hasattr(exchange, "forward"):
                @pl.when(linear == (grid[0] * grid[1] * grid[2]) // 2)
                def _():
                    exchange.forward(*ex_refs)

        prod = lax.dot_general(a_ref[...].astype(BF16), b_ref[...].astype(BF16), _DOT_DIMS[mode],
                               preferred_element_type=F32)
        if nk == 1:
            o_ref[...] = prod.astype(o_ref.dtype)
        else:
            acc_ref, k = refs[3 + 2 * nx], step[2]

            @pl.when(k == 0)
            def _():
                acc_ref[...] = prod

            @pl.when(k > 0)
            def _():
                acc_ref[...] += prod

            @pl.when(k == nk - 1)
            def _():
                o_ref[...] = acc_ref[...].astype(o_ref.dtype)
        if nx:
            @pl.when(linear == grid[0] * grid[1] * grid[2] - 1)
            def _():
                exchange.finish(*ex_refs)

    if mode == "tn":
        a_spec = pl.BlockSpec((bk, bm), lambda i, j, k: (k, i))
    else:
        a_spec = pl.BlockSpec((bm, bk), lambda i, j, k: (i, k))
    if mode == "nt":
        b_spec = pl.BlockSpec((bn, bk), lambda i, j, k: (j, k))
    else:
        b_spec = pl.BlockSpec((bk, bn), lambda i, j, k: (k, j))
    out_spec = pl.BlockSpec((bm, bn), lambda i, j, k: (i, j))
    out_shape = jax.ShapeDtypeStruct((m_dim, n_dim), out_dtype)
    acc_scratch = [pltpu.VMEM((bm, bn), F32)] if nk > 1 else []
    if exchange is None:
        return pl.pallas_call(
            body, name=name, grid=grid, in_specs=[a_spec, b_spec], out_specs=out_spec, out_shape=out_shape,
            scratch_shapes=acc_scratch, compiler_params=_params("parallel", "parallel", "arbitrary"),
        )(a, b)
    outs = pl.pallas_call(
        body, name=name, grid=grid, in_specs=[a_spec, b_spec] + [HBM_SPEC] * nx,
        out_specs=[out_spec] + [HBM_SPEC] * nx, out_shape=[out_shape] + exchange.out_shape,
        scratch_shapes=acc_scratch + exchange.scratch, compiler_params=_params("arbitrary", "arbitrary", "arbitrary"),
    )(a, b, *exchange.arrays)
    return outs[0], outs[1:]


def _rowcall(body, name, t_dim, bt, row_ins, const_ins, row_outs, acc_outs=(), scratch=(), reverse=False):
    nt = t_dim // bt

    def rmap(i):
        return nt - 1 - i if reverse else i

    in_specs, args = [], []
    for item in row_ins:
        arr, cb, w = item if isinstance(item, tuple) else (item, 0, item.shape[1])
        in_specs.append(pl.BlockSpec((bt, w), lambda i, cb=cb: (rmap(i), cb)))
        args.append(arr)
    for arr in const_ins:
        in_specs.append(pl.BlockSpec(arr.shape, lambda i, nd=arr.ndim: (0,) * nd))
        args.append(arr)
    out_specs = [pl.BlockSpec((bt, w), lambda i: (rmap(i), 0)) for w, _ in row_outs]
    out_shape = [jax.ShapeDtypeStruct((t_dim, w), dt) for w, dt in row_outs]
    for shp in acc_outs:
        out_specs.append(pl.BlockSpec(shp, lambda i, nd=len(shp): (0,) * nd))
        out_shape.append(jax.ShapeDtypeStruct(shp, F32))
    return pl.pallas_call(
        body, name=name, grid=(nt,), in_specs=in_specs, out_specs=out_specs, out_shape=out_shape,
        scratch_shapes=list(scratch), compiler_params=_params("arbitrary"),
    )(*args)


def _first_step_zero(*refs):
    @pl.when(pl.program_id(0) == 0)
    def _():
        for r in refs:
            r[...] = jnp.zeros_like(r)


def _colsum(v):
    return jnp.sum(v, axis=0, keepdims=True)


def _headsum(v, e):
    parts = [jnp.dot(v[:, p * LANES:(p + 1) * LANES], e, precision=HI, preferred_element_type=F32)
             for p in range(v.shape[1] // LANES)]
    return parts[0] if len(parts) == 1 else jnp.concatenate(parts, axis=1)


def _shift_down(v, carry_ref):
    bt = v.shape[0]
    prev = pltpu.roll(v, 1, 0)
    row = lax.broadcasted_iota(jnp.int32, v.shape, 0)
    prev = jnp.where(row == 0, carry_ref[...], prev)
    carry_ref[...] = v[bt - 1:bt, :]
    return prev


def _pair_consts():
    lane = lax.broadcasted_iota(jnp.int32, (1, LANES), 1)
    m0 = (lane < HEAD).astype(F32)
    m1 = 1.0 - m0
    sub = lax.broadcasted_iota(jnp.int32, (HEAD, LANES), 0)
    lane2 = lax.broadcasted_iota(jnp.int32, (HEAD, LANES), 1)
    i0 = (lane2 == sub).astype(F32)
    i1 = (lane2 == sub + HEAD).astype(F32)
    return m0, m1, i0, i1


def _head_masks():
    lane = lax.broadcasted_iota(jnp.int32, (1, LANES), 1)
    first = (lane < HEAD).astype(F32)
    return first, 1.0 - first


def _head_ones(dtype=BF16):
    lane = jnp.arange(LANES)
    return (lane[:, None] // HEAD == lane[None, :] // HEAD).astype(dtype)


def _lanesum(v):
    return jnp.sum(v, axis=1, keepdims=True)


def _split_bf16(v):
    hi = v.astype(BF16).astype(F32)
    rest = v - hi
    mid = rest.astype(BF16).astype(F32)
    return hi, mid, (rest - mid).astype(BF16).astype(F32)


def _col_pair(parts, j, i01, e_bf16):
    lhs = jnp.concatenate([(part[j:j + 1] * i01).astype(BF16) for part in parts], axis=0)
    out = jnp.dot(lhs, e_bf16, preferred_element_type=F32)
    return (out[0:HEAD] + out[HEAD:2 * HEAD]) + out[2 * HEAD:3 * HEAD]


class _RowsOf:
    def __init__(self, ref, rows, lanes):
        self.ref, self.rows, self.lanes = ref, rows, lanes

    def __getitem__(self, one):
        return self.ref[self.rows, self.lanes][one]


def _row_pair(c0, c1, i0, i1):
    return _colsum(c0 * i0 + c1 * i1)


def _scan_fwd(r, w, k, v, kk, a, tc, npb):
    t_dim, c_dim = r.shape
    wb = LANES * npb
    tc = _pick(t_dim, tc, SUBLANES)

    def body(r_ref, w_ref, k_ref, v_ref, kk_ref, a_ref, e_ref, y_ref, sall_ref, sfin_ref, s_ref):
        @pl.when(pl.program_id(1) == 0)
        def _():
            s_ref[...] = jnp.zeros_like(s_ref)

        m0, m1, i0, i1 = _pair_consts()
        i01 = i0 + i1
        e = e_ref[...]
        sub8 = lax.broadcasted_iota(jnp.int32, (SUBLANES, LANES), 0)
        lanes = [slice(q * LANES, (q + 1) * LANES) for q in range(npb)]
        ng = tc // SUBLANES

        def lanesums(tiles):
            sums = _lanesum(jnp.concatenate(tiles, axis=0))
            return [sums[i * HEAD:(i + 1) * HEAD] for i in range(len(tiles))]

        def halves(s, row):
            return [s * (row * m0), s * (row * m1)]

        def group(gi, carry):
            base = pl.multiple_of(gi * SUBLANES, SUBLANES)
            rows = pl.ds(base, SUBLANES)
            s = list(carry)
            r8 = [_RowsOf(r_ref, rows, sl) for sl in lanes]
            w8 = [_RowsOf(w_ref, rows, sl) for sl in lanes]
            k8 = [_RowsOf(k_ref, rows, sl) for sl in lanes]
            kk8 = [_RowsOf(kk_ref, rows, sl) for sl in lanes]
            a8 = [_RowsOf(a_ref, rows, sl) for sl in lanes]
            v8 = [_split_bf16(v_ref[rows, sl]) for sl in lanes]
            y8 = [jnp.zeros((SUBLANES, LANES), F32)] * npb
            for j in range(SUBLANES + 1):
                one, before = slice(j, j + 1), slice(j - 1, j)
                tiles = []
                for q in range(npb):
                    if j < SUBLANES:
                        tiles += halves(s[q], kk8[q][one])
                    if j > 0:
                        tiles += halves(s[q], r8[q][before])
                cols = lanesums(tiles)
                per = len(tiles) // npb
                for q in range(npb):
                    mine = cols[q * per:(q + 1) * per]
                    if j > 0:
                        y8[q] = jnp.where(sub8 == j - 1, _row_pair(mine[-2], mine[-1], i0, i1), y8[q])
                    if j < SUBLANES:
                        sall_ref[base + j, :, lanes[q]] = s[q]
                        sb = mine[0] * m0 + mine[1] * m1
                        s[q] = (s[q] * w8[q][one] - sb * (kk8[q][one] * a8[q][one])
                                + _col_pair(v8[q], j, i01, e) * k8[q][one])
            for q in range(npb):
                y_ref[rows, lanes[q]] = y8[q]
            return tuple(s)

        init = tuple(s_ref[:, q * LANES:(q + 1) * LANES] for q in range(npb))
        fin = lax.fori_loop(0, ng, group, init)
        for q in range(npb):
            s_ref[:, q * LANES:(q + 1) * LANES] = fin[q]
            sfin_ref[:, q * LANES:(q + 1) * LANES] = fin[q]

    row = pl.BlockSpec((tc, wb), lambda p, c: (c, p))
    return pl.pallas_call(
        body, name="rwkv_scan_fwd", grid=(c_dim // wb, t_dim // tc),
        in_specs=[row] * 6 + [pl.BlockSpec((LANES, LANES), lambda p, c: (0, 0))],
        out_specs=[row, pl.BlockSpec((tc, HEAD, wb), lambda p, c: (c, 0, p)), pl.BlockSpec((HEAD, wb), lambda p, c: (0, p))],
        out_shape=[jax.ShapeDtypeStruct((t_dim, c_dim), F32), jax.ShapeDtypeStruct((t_dim, HEAD, c_dim), F32),
                   jax.ShapeDtypeStruct((HEAD, c_dim), F32)],
        scratch_shapes=[pltpu.VMEM((HEAD, wb), F32)],
        compiler_params=_params("parallel", "arbitrary"),
    )(r, w, k, v, kk, a, _head_ones())


def _scan_bwd(r, w, k, v, kk, a, dy, sall, sfin, tc, npb, exchange):
    t_dim, c_dim = r.shape
    wb = LANES * npb
    tc = _pick(t_dim, tc, SUBLANES)
    nc = t_dim // tc
    nx = exchange.n
    n_blocks = c_dim // wb

    def body(*refs):
        r_ref, w_ref, k_ref, v_ref, kk_ref, a_ref, dy_ref, sall_ref, sfin_ref, e_ref = refs[:10]
        dr_ref, dw_ref, dk_ref, dv_ref, dkk_ref, da_ref = refs[10 + nx:16 + nx]
        ds_ref, sn_ref = refs[16 + 2 * nx:18 + 2 * nx]
        here_p, here_c = pl.program_id(0), pl.program_id(1)
        ex_refs = (refs[10:10 + nx], refs[16 + nx:16 + 2 * nx], refs[18 + 2 * nx:])

        @pl.when((here_p == 0) & (here_c == 0))
        def _():
            exchange.start(*ex_refs)

        @pl.when(pl.program_id(1) == 0)
        def _():
            ds_ref[...] = jnp.zeros_like(ds_ref)
            sn_ref[...] = sfin_ref[...]

        m0, m1, i0, i1 = _pair_consts()
        i01 = i0 + i1
        e = e_ref[...]
        sub8 = lax.broadcasted_iota(jnp.int32, (SUBLANES, LANES), 0)
        lanes = [slice(q * LANES, (q + 1) * LANES) for q in range(npb)]
        ng = tc // SUBLANES

        def halves(s, row):
            return [s * (row * m0), s * (row * m1)]

        def group(gi, carry):
            base = pl.multiple_of((ng - 1 - gi) * SUBLANES, SUBLANES)
            rows = pl.ds(base, SUBLANES)
            ds = list(carry)
            r8 = [_RowsOf(r_ref, rows, sl) for sl in lanes]
            w8 = [_RowsOf(w_ref, rows, sl) for sl in lanes]
            k8 = [_RowsOf(k_ref, rows, sl) for sl in lanes]
            kk8 = [_RowsOf(kk_ref, rows, sl) for sl in lanes]
            a8 = [_RowsOf(a_ref, rows, sl) for sl in lanes]
            v8 = [_split_bf16(v_ref[rows, sl]) for sl in lanes]
            dy8 = [_split_bf16(dy_ref[rows, sl]) for sl in lanes]
            zero8 = jnp.zeros((SUBLANES, LANES), F32)
            dr8, dw8, dk8, dv8, dkk8, da8 = ([zero8] * npb for _ in range(6))
            for j in reversed(range(SUBLANES)):
                one = slice(j, j + 1)
                here = sub8 == j
                d, s_prev, tiles = [], [], []
                for q in range(npb):
                    dyb = _col_pair(dy8[q], j, i01, e)
                    s_prev.append(sall_ref[base + j, :, lanes[q]])
                    dr8[q] = jnp.where(here, _colsum(sn_ref[:, lanes[q]] * dyb), dr8[q])
                    sn_ref[:, lanes[q]] = s_prev[q]
                    d.append(ds[q] + dyb * r8[q][one])
                for q in range(npb):
                    tiles += halves(d[q], kk8[q][one] * a8[q][one])
                for q in range(npb):
                    tiles += halves(d[q], k8[q][one]) + halves(s_prev[q], kk8[q][one])
                sums = _lanesum(jnp.concatenate(tiles, axis=0))
                cols = [sums[i * HEAD:(i + 1) * HEAD] for i in range(len(tiles))]
                for q in range(npb):
                    kkr, ar = kk8[q][one], a8[q][one]
                    dsb = -(cols[2 * q] * m0 + cols[2 * q + 1] * m1)
                    ds[q] = d[q] * w8[q][one] + dsb * kkr
                    rest = cols[2 * npb + 4 * q:2 * npb + 4 * q + 4]
                    dv8[q] = jnp.where(here, _row_pair(rest[0], rest[1], i0, i1), dv8[q])
                    sb = rest[2] * m0 + rest[3] * m1
                    db = -_colsum(d[q] * sb)
                    dk8[q] = jnp.where(here, _colsum(d[q] * _col_pair(v8[q], j, i01, e)), dk8[q])
                    dw8[q] = jnp.where(here, _colsum(d[q] * s_prev[q]), dw8[q])
                    dkk8[q] = jnp.where(here, _colsum(s_prev[q] * dsb) + db * ar, dkk8[q])
                    da8[q] = jnp.where(here, db * kkr, da8[q])
            for q in range(npb):
                sl = lanes[q]
                dr_ref[rows, sl], dw_ref[rows, sl], dk_ref[rows, sl] = dr8[q], dw8[q], dk8[q]
                dv_ref[rows, sl], dkk_ref[rows, sl], da_ref[rows, sl] = dv8[q], dkk8[q], da8[q]
            return tuple(ds)

        init = tuple(ds_ref[:, q * LANES:(q + 1) * LANES] for q in range(npb))
        fin = lax.fori_loop(0, ng, group, init)
        for q in range(npb):
            ds_ref[:, q * LANES:(q + 1) * LANES] = fin[q]

        @pl.when((here_p == n_blocks - 1) & (here_c == nc - 1))
        def _():
            exchange.finish(*ex_refs)

    row = pl.BlockSpec((tc, wb), lambda p, c: (nc - 1 - c, p))
    outs = pl.pallas_call(
        body, name="rwkv_scan_bwd", grid=(n_blocks, nc),
        in_specs=[row] * 7 + [pl.BlockSpec((tc, HEAD, wb), lambda p, c: (nc - 1 - c, 0, p)),
                              pl.BlockSpec((HEAD, wb), lambda p, c: (0, p)),
                              pl.BlockSpec((LANES, LANES), lambda p, c: (0, 0))] + [HBM_SPEC] * nx,
        out_specs=[row] * 6 + [HBM_SPEC] * nx,
        out_shape=[jax.ShapeDtypeStruct((t_dim, c_dim), F32)] * 6 + exchange.out_shape,
        scratch_shapes=[pltpu.VMEM((HEAD, wb), F32)] * 2 + exchange.scratch,
        compiler_params=_params("arbitrary", "arbitrary"),
    )(r, w, k, v, kk, a, dy, sall, sfin, _head_ones(), *exchange.arrays)
    return outs[:6], outs[6:]


def _fox_fwd(z, ct, cq, q_cb, k_cb, v_cb, n_pairs, blk):
    t_dim = z.shape[0]
    blk = _pick(t_dim, blk)
    nq = t_dim // blk
    scale = HEAD ** -0.5

    def body(q_ref, k_ref, v_ref, ct_ref, cq_ref, o_ref, lse_ref):
        i = pl.program_id(1)
        rowi = lax.broadcasted_iota(jnp.int32, (blk, blk), 0)
        coli = lax.broadcasted_iota(jnp.int32, (blk, blk), 1)
        masks = _head_masks()
        qv = q_ref[...]
        qs = [(qv * mk).astype(BF16) for mk in masks]
        cqs = [cq_ref[:, hh:hh + 1] for hh in range(2)]

        def kv_step(j, carry, masked):
            rows = pl.ds(pl.multiple_of(j * blk, blk), blk)
            kb = k_ref[rows, :].astype(BF16)
            vv = v_ref[rows, :]
            stats, acc = list(carry[:4]), carry[4]
            rescale, add = 0.0, 0.0
            for hh in range(2):
                m, l = stats[2 * hh], stats[2 * hh + 1]
                s = (lax.dot_general(qs[hh], kb, _DOT_DIMS["nt"], preferred_element_type=F32) * scale
                     + (cqs[hh] - ct_ref[hh:hh + 1, rows]))
                if masked:
                    s = jnp.where(rowi >= coli, s, -jnp.inf)
                m_new = jnp.maximum(m, jnp.max(s, axis=1, keepdims=True))
                alpha = jnp.exp(m - m_new)
                pr = jnp.exp(s - m_new)
                stats[2 * hh], stats[2 * hh + 1] = m_new, l * alpha + jnp.sum(pr, axis=1, keepdims=True)
                rescale = rescale + alpha * masks[hh]
                hi = pr.astype(BF16)
                both = jnp.concatenate([hi, (pr - hi.astype(F32)).astype(BF16)], axis=1)
                vh = (vv * masks[hh]).astype(BF16)
                add = add + jnp.dot(both, jnp.concatenate([vh, vh], axis=0), preferred_element_type=F32)
            return (*stats, acc * rescale + add)

        neg, zero = jnp.full((blk, 1), -jnp.inf, F32), jnp.zeros((blk, 1), F32)
        carry = lax.fori_loop(0, i, functools.partial(kv_step, masked=False),
                              (neg, zero, neg, zero, jnp.zeros((blk, LANES), F32)))
        m0, l0, m1, l1, acc = kv_step(i, carry, True)
        o_ref[...] = acc * (masks[0] / l0 + masks[1] / l1)
        lse_ref[:, 0:1] = m0 + jnp.log(l0)
        lse_ref[:, 1:2] = m1 + jnp.log(l1)

    full = lambda cb: pl.BlockSpec((t_dim, LANES), lambda p, i, cb=cb: (0, cb + p))
    return pl.pallas_call(
        body, name="fox_attn_fwd", grid=(n_pairs, nq),
        in_specs=[pl.BlockSpec((blk, LANES), lambda p, i: (i, q_cb + p)), full(k_cb), full(v_cb),
                  pl.BlockSpec((None, SUBLANES, t_dim), lambda p, i: (p, 0, 0)),
                  pl.BlockSpec((None, blk, 2), lambda p, i: (p, i, 0))],
        out_specs=[pl.BlockSpec((blk, LANES), lambda p, i: (i, p)), pl.BlockSpec((None, blk, 2), lambda p, i: (p, i, 0))],
        out_shape=[jax.ShapeDtypeStruct((t_dim, n_pairs * LANES), F32), jax.ShapeDtypeStruct((n_pairs, t_dim, 2), F32)],
        compiler_params=_params("parallel", "arbitrary"),
    )(z, z, z, ct, cq)


def _fox_bwd(z, ct, cq, rowdot, do, lse, q_cb, k_cb, v_cb, n_pairs, blk):
    t_dim = z.shape[0]
    blk = _pick(t_dim, blk)
    nb = t_dim // blk
    scale = HEAD ** -0.5

    def body(q_ref, k_ref, v_ref, ct_ref, cq_ref, rd_ref, do_ref, lse_ref, dq_ref, dk_ref, dv_ref, dc_ref):
        j = pl.program_id(1)

        @pl.when(j == 0)
        def _():
            dq_ref[...] = jnp.zeros_like(dq_ref)

        rowi = lax.broadcasted_iota(jnp.int32, (blk, blk), 0)
        coli = lax.broadcasted_iota(jnp.int32, (blk, blk), 1)
        krows = pl.ds(pl.multiple_of(j * blk, blk), blk)
        masks = _head_masks()
        kv, vb = k_ref[...], v_ref[...].astype(BF16)
        kb = kv.astype(BF16)
        ks = [(kv * mk).astype(BF16) for mk in masks]
        cks = [ct_ref[hh:hh + 1, krows] for hh in range(2)]

        def q_step(i, carry, masked):
            dk, dv, dcs = carry[0], carry[1], list(carry[2:])
            rows = pl.ds(pl.multiple_of(i * blk, blk), blk)
            qv, dov = q_ref[rows, :], do_ref[rows, :]
            dq = 0.0
            for hh in range(2):
                qh = (qv * masks[hh]).astype(BF16)
                doh = (dov * masks[hh]).astype(BF16)
                s = (lax.dot_general(qh, kb, _DOT_DIMS["nt"], preferred_element_type=F32) * scale
                     + (cq_ref[rows, hh:hh + 1] - cks[hh]))
                pr = jnp.exp(s - lse_ref[rows, hh:hh + 1])
                if masked:
                    pr = jnp.where(rowi >= coli, pr, 0.0)
                dv = dv + lax.dot_general(pr.astype(BF16), doh, _DOT_DIMS["tn"], preferred_element_type=F32)
                dp = lax.dot_general(doh, vb, _DOT_DIMS["nt"], preferred_element_type=F32)
                ds = pr * (dp - rd_ref[rows, hh * HEAD:hh * HEAD + 1])
                dsb = ds.astype(BF16)
                dq = dq + jnp.dot(dsb, ks[hh], preferred_element_type=F32)
                dk = dk + lax.dot_general(dsb, qh, _DOT_DIMS["tn"], preferred_element_type=F32)
                dcs[hh] = dcs[hh] - _colsum(ds)
            dq_ref[rows, :] += dq * scale
            return (dk, dv, *dcs)

        zero_row = jnp.zeros((1, blk), F32)
        init = (jnp.zeros((blk, LANES), F32), jnp.zeros((blk, LANES), F32), zero_row, zero_row)
        carry = q_step(j, init, True)
        dk, dv, dc0, dc1 = lax.fori_loop(j + 1, nb, functools.partial(q_step, masked=False), carry)
        dk_ref[...] = dk * scale
        dv_ref[...] = dv
        dc_ref[0:1, :] = dc0
        dc_ref[1:2, :] = dc1
        dc_ref[2:SUBLANES, :] = jnp.zeros((SUBLANES - 2, blk), F32)

    full = lambda: pl.BlockSpec((t_dim, LANES), lambda p, j: (0, p))
    blkspec = pl.BlockSpec((blk, LANES), lambda p, j: (j, p))
    return pl.pallas_call(
        body, name="fox_attn_bwd", grid=(n_pairs, nb),
        in_specs=[pl.BlockSpec((t_dim, LANES), lambda p, j: (0, q_cb + p)),
                  pl.BlockSpec((blk, LANES), lambda p, j: (j, k_cb + p)),
                  pl.BlockSpec((blk, LANES), lambda p, j: (j, v_cb + p)),
                  pl.BlockSpec((None, SUBLANES, t_dim), lambda p, j: (p, 0, 0)),
                  pl.BlockSpec((None, t_dim, 2), lambda p, j: (p, 0, 0)), full(), full(),
                  pl.BlockSpec((None, t_dim, 2), lambda p, j: (p, 0, 0))],
        out_specs=[full(), blkspec, blkspec, pl.BlockSpec((None, SUBLANES, blk), lambda p, j: (p, 0, j))],
        out_shape=[jax.ShapeDtypeStruct((t_dim, n_pairs * LANES), F32)] * 3
                  + [jax.ShapeDtypeStruct((n_pairs, SUBLANES, t_dim), F32)],
        compiler_params=_params("parallel", "arbitrary"),
    )(z, z, z, ct, cq, rowdot, do, lse)


HBM_SPEC = pl.BlockSpec(memory_space=pltpu.HBM)


class _Gather:
    def __init__(self, shards):
        self.arrays = list(shards)
        self.n = len(self.arrays)
        self.out_shape = [jax.ShapeDtypeStruct((NDEV,) + s.shape, s.dtype) for s in self.arrays]
        self.scratch = [pltpu.SemaphoreType.DMA((7 * self.n,)), pltpu.SemaphoreType.DMA((7 * self.n,)),
                        pltpu.SemaphoreType.DMA((self.n,))]

    def _plan(self, x_refs, out_refs, sems):
        send_sems, recv_sems, local_sems = sems
        x, y, c = lax.axis_index("x"), lax.axis_index("y"), lax.axis_index("c")
        me, sibling = (x, y, c), (x, y, 1 - c)
        chips = [(1 - x, y), (x, 1 - y), (1 - x, 1 - y)]

        def copy(a, k, block, to, from_input=False):
            px, py, pc = block
            slot = out_refs[a].at[4 * px + 2 * py + pc]
            return pltpu.make_async_remote_copy(
                src_ref=x_refs[a] if from_input else slot, dst_ref=slot,
                send_sem=send_sems.at[7 * a + k], recv_sem=recv_sems.at[7 * a + k], device_id=to, device_id_type=MESH)

        mine = [pltpu.make_async_copy(x_refs[a], out_refs[a].at[4 * x + 2 * y + c], local_sems.at[a])
                for a in range(self.n)]
        first = []
        for a in range(self.n):
            first += [copy(a, 1 + j, me, (*chip, c), from_input=True) for j, chip in enumerate(chips)]
            first.append(copy(a, 0, me, sibling, from_input=True))
        over_ici = [copy(a, 1 + j, (*chip, c), me) for a in range(self.n) for j, chip in enumerate(chips)]
        passed = [copy(a, 4 + j, (*chip, c), sibling) for a in range(self.n) for j, chip in enumerate(chips)]
        from_sibling = []
        for a in range(self.n):
            from_sibling.append(copy(a, 0, sibling, me))
            from_sibling += [copy(a, 4 + j, (*chip, 1 - c), me) for j, chip in enumerate(chips)]
        return mine, first, over_ici, passed, from_sibling

    def start(self, *refs):
        mine, first, _, _, _ = self._plan(*refs)
        for cp in mine + first:
            cp.start()

    def forward(self, *refs):
        _, _, over_ici, passed, _ = self._plan(*refs)
        for arrived, onward in zip(over_ici, passed):
            arrived.wait_recv()
            onward.start()

    def finish(self, *refs):
        mine, first, _, passed, from_sibling = self._plan(*refs)
        for cp in from_sibling:
            cp.wait_recv()
        for cp in first + passed:
            cp.wait_send()
        for cp in mine:
            cp.wait()


def _all_gather(shards, name):
    ga = _Gather(shards)

    def body(*refs):
        parts = (refs[:ga.n], refs[ga.n:2 * ga.n], refs[2 * ga.n:])
        ga.start(*parts)
        ga.forward(*parts)
        ga.finish(*parts)

    return pl.pallas_call(body, name=name, out_shape=ga.out_shape, in_specs=[HBM_SPEC] * ga.n,
                          out_specs=[HBM_SPEC] * ga.n, scratch_shapes=ga.scratch)(*ga.arrays)


class _Exchange:
    def __init__(self, arrays):
        self.arrays = list(arrays)
        self.n = len(self.arrays)
        self.per_dest = [a.ndim == 3 for a in self.arrays]
        self.out_shape = [jax.ShapeDtypeStruct(a.shape if pd else (NDEV,) + a.shape, a.dtype)
                          for a, pd in zip(self.arrays, self.per_dest)]
        self.scratch = [pltpu.SemaphoreType.DMA((7 * self.n,)), pltpu.SemaphoreType.DMA((7 * self.n,)),
                        pltpu.SemaphoreType.DMA((self.n,))]

    def _copies(self, in_refs, out_refs, sems):
        send_sems, recv_sems, local_sems = sems
        x, y, c = lax.axis_index("x"), lax.axis_index("y"), lax.axis_index("c")
        me = 4 * x + 2 * y + c
        own, sends, recvs = [], [], []
        for a in range(self.n):
            mine = in_refs[a].at[me] if self.per_dest[a] else in_refs[a]
            own.append(pltpu.make_async_copy(mine, out_refs[a].at[me], local_sems.at[a]))
            for k in range(1, NDEV):
                px = 1 - x if k & 4 else x
                py = 1 - y if k & 2 else y
                pc = 1 - c if k & 1 else c
                peer = 4 * px + 2 * py + pc
                sem = dict(send_sem=send_sems.at[7 * a + k - 1], recv_sem=recv_sems.at[7 * a + k - 1],
                           device_id=(px, py, pc), device_id_type=MESH)
                src = in_refs[a].at[peer] if self.per_dest[a] else in_refs[a]
                sends.append(pltpu.make_async_remote_copy(src_ref=src, dst_ref=out_refs[a].at[me], **sem))
                recvs.append(pltpu.make_async_remote_copy(src_ref=src, dst_ref=out_refs[a].at[peer], **sem))
        return own, sends, recvs

    def start(self, in_refs, out_refs, sems):
        own, sends, _ = self._copies(in_refs, out_refs, sems)
        for cp in own + sends:
            cp.start()

    def finish(self, in_refs, out_refs, sems):
        own, sends, recvs = self._copies(in_refs, out_refs, sems)
        for cp in recvs:
            cp.wait_recv()
        for cp in sends:
            cp.wait_send()
        for cp in own:
            cp.wait()


def _grad_exchange(arrays, name):
    ex = _Exchange(arrays)

    def body(*refs):
        in_refs, out_refs, sems = refs[:ex.n], refs[ex.n:2 * ex.n], refs[2 * ex.n:]
        ex.start(in_refs, out_refs, sems)
        ex.finish(in_refs, out_refs, sems)

    return pl.pallas_call(body, name=name, out_shape=ex.out_shape, in_specs=[HBM_SPEC] * ex.n,
                          out_specs=[HBM_SPEC] * ex.n, scratch_shapes=ex.scratch)(*ex.arrays)


NCHIP = NDEV // 2


def _sibling_swap(parts, name):
    def body(p_ref, out_ref, send_sems, recv_sems):
        x, y, c = lax.axis_index("x"), lax.axis_index("y"), lax.axis_index("c")
        copies = [pltpu.make_async_remote_copy(
            src_ref=p_ref.at[2 * i + 1 - c], dst_ref=out_ref.at[i], send_sem=send_sems.at[i], recv_sem=recv_sems.at[i],
            device_id=(x, y, 1 - c), device_id_type=MESH) for i in range(NCHIP)]
        for cp in copies:
            cp.start()
        for cp in copies:
            cp.wait_recv()
        for cp in copies:
            cp.wait_send()

    return pl.pallas_call(
        body, name=name, out_shape=jax.ShapeDtypeStruct((NCHIP,) + parts.shape[1:], parts.dtype),
        in_specs=[HBM_SPEC], out_specs=HBM_SPEC,
        scratch_shapes=[pltpu.SemaphoreType.DMA((NCHIP,)), pltpu.SemaphoreType.DMA((NCHIP,))],
    )(parts)


def _add_pairs(parts, sibling, name):
    _, rows, width = parts.shape
    bc = _pick(width, 4 * LANES)

    def body(even_ref, odd_ref, sib_ref, o_ref):
        own = jnp.where(lax.axis_index("c") == 0, even_ref[...], odd_ref[...])
        o_ref[...] = (own.astype(F32) + sib_ref[...].astype(F32)).astype(o_ref.dtype)

    slot = lambda parity: pl.BlockSpec((None, rows, bc), lambda i, j: (2 * i + parity, 0, j))
    blk = pl.BlockSpec((None, rows, bc), lambda i, j: (i, 0, j))
    return pl.pallas_call(body, name=name, grid=(NCHIP, width // bc), in_specs=[slot(0), slot(1), blk], out_specs=blk,
                          out_shape=jax.ShapeDtypeStruct(sibling.shape, parts.dtype),
                          compiler_params=_params("parallel", "parallel"))(parts, parts, sibling)


class _ChipExchange:
    FLIPS = ((0, 1), (1, 0), (1, 1))

    def __init__(self, arrays):
        self.arrays = list(arrays)
        self.n = len(self.arrays)
        self.out_shape = [jax.ShapeDtypeStruct(a.shape, a.dtype) for a in self.arrays]
        self.scratch = [pltpu.SemaphoreType.DMA((3 * self.n,)), pltpu.SemaphoreType.DMA((3 * self.n,)),
                        pltpu.SemaphoreType.DMA((self.n,))]

    def _copies(self, in_refs, out_refs, sems):
        send_sems, recv_sems, local_sems = sems
        x, y, c = lax.axis_index("x"), lax.axis_index("y"), lax.axis_index("c")
        here = 2 * x + y
        own, sends, recvs = [], [], []
        for a in range(self.n):
            own.append(pltpu.make_async_copy(in_refs[a].at[here], out_refs[a].at[here], local_sems.at[a]))
            for k, (fx, fy) in enumerate(self.FLIPS):
                px = 1 - x if fx else x
                py = 1 - y if fy else y
                there = 2 * px + py
                sem = dict(send_sem=send_sems.at[3 * a + k], recv_sem=recv_sems.at[3 * a + k],
                           device_id=(px, py, c), device_id_type=MESH)
                src = in_refs[a].at[there]
                sends.append(pltpu.make_async_remote_copy(src_ref=src, dst_ref=out_refs[a].at[here], **sem))
                recvs.append(pltpu.make_async_remote_copy(src_ref=src, dst_ref=out_refs[a].at[there], **sem))
        return own, sends, recvs

    start = _Exchange.start
    finish = _Exchange.finish


def _adamw_body(p_ref, w_ref, m_ref, v_ref, g_out, d_out, m_out, v_out):
    g = p_ref[0].astype(F32)
    for d in range(1, p_ref.shape[0]):
        g = g + p_ref[d].astype(F32)
    mn = ADAM_B1 * m_ref[...] + (1.0 - ADAM_B1) * g
    vn = ADAM_B2 * v_ref[...] + (1.0 - ADAM_B2) * jnp.square(g)
    m_hat = mn / (1.0 - ADAM_B1 ** ADAM_STEP)
    v_hat = vn / (1.0 - ADAM_B2 ** ADAM_STEP)
    g_out[...] = g
    d_out[...] = -ADAM_LR * (m_hat / (jnp.sqrt(v_hat) + ADAM_EPS) + ADAM_WD * w_ref[...])
    m_out[...] = mn
    v_out[...] = vn


def _adamw(partials, w, m, v, name):
    rows, width = w.shape
    n = partials.shape[0]
    body = functools.partial(_adamw_body)
    if rows % (2 * SUBLANES) == 0:
        br = _pick(rows, 128, 2 * SUBLANES)
        grid, shape, index, index3 = rows // br, (br, width), lambda i: (i, 0), lambda i: (0, i, 0)
    else:
        bc = _pick(width, 2 * LANES)
        grid, shape, index, index3 = width // bc, (rows, bc), lambda i: (0, i), lambda i: (0, 0, i)
    blk = pl.BlockSpec(shape, index)
    return pl.pallas_call(
        body, name=name, grid=(grid,),
        in_specs=[pl.BlockSpec((n,) + shape, index3), blk, blk, blk],
        out_specs=[pl.BlockSpec((None,) + shape, index3)] * 4,
        out_shape=[jax.ShapeDtypeStruct((1, rows, width), F32)] * 4,
        compiler_params=_params("parallel"),
    )(partials, w, m, v)


def _pack_rows(flat):
    n = flat.shape[0]
    padded = _round_up(n, PACK_ALIGN)
    return jnp.pad(flat, (0, padded - n)).reshape(padded // PACK_W, PACK_W)


def _split_shards(full, axis):
    rows, cols = full.shape
    if axis == 0:
        return full.reshape(NDEV, rows // NDEV, cols)
    width = cols // NDEV
    return jnp.stack([full[:, d * width:(d + 1) * width] for d in range(NDEV)])


def _join_shards(blocks, axis):
    if axis == 0:
        return blocks.reshape(-1, blocks.shape[2])
    return jnp.concatenate([blocks[d] for d in range(NDEV)], axis=1)


SHARDED = (("w_in", 1), ("rw_w_lora_up", 1), ("rw_a_lora_up", 1), ("w_up_rwkv", 1), ("w_up_fox", 1),
           ("w_out", 0), ("ple_proj", 1), ("ple_gate_w", 0))
REPLICATED = ("norm_g", "rw_shift_mu", "rw_w0", "rw_a0", "rw_k_k", "rw_k_a", "rw_r_k", "rw_ln_g", "rw_ln_b",
              "fox_b_f", "ple_norm_g", "final_norm_g")
WEIGHTS = ("norm_g", "w_in", "rw_shift_mu", "rw_w0", "rw_w_lora_up", "rw_a0", "rw_a_lora_up", "rw_k_k", "rw_k_a",
           "rw_r_k", "rw_ln_g", "rw_ln_b", "fox_b_f", "w_up_rwkv", "w_up_fox", "w_out", "ple_proj", "ple_gate_w",
           "ple_norm_g", "final_norm_g")


def _local_step(x, p, tgt, wz, other_shards, rep, dims):
    t_dim, d_model, c_rw, lora, c_fox, h_fox, sec = dims
    bt = _pick(t_dim, 256, 2 * SUBLANES)
    bt_many = _pick(t_dim, 128, 2 * SUBLANES)
    n_pairs = c_fox // LANES
    row = lambda a: a.reshape(1, -1)
    norm_g, mu, w0, a0 = row(rep["norm_g"]), row(rep["rw_shift_mu"]), row(rep["rw_w0"]), row(rep["rw_a0"])
    k_k, k_a, r_k = row(rep["rw_k_k"]), row(rep["rw_k_a"]), row(rep["rw_r_k"])
    ln_g, ln_b = row(rep["rw_ln_g"]), row(rep["rw_ln_b"])
    g2, g3 = row(rep["ple_norm_g"]), row(rep["final_norm_g"])
    b_f = jnp.pad(row(rep["fox_b_f"]), ((0, 0), (0, LANES - h_fox)))
    e_head = _head_ones(F32)
    c4 = 4 * c_rw
    inv_d = 1.0 / d_model
    decay_k = math.exp(-0.5)

    def norm_in(x_ref, g_ref, h_ref):
        xv = x_ref[...]
        rms = lax.rsqrt(jnp.mean(xv * xv, axis=-1, keepdims=True) + NORM_EPS)
        h_ref[...] = (xv * rms * g_ref[...]).astype(BF16)

    (h,) = _rowcall(norm_in, "norm_in", t_dim, bt, [x], [norm_g], [(d_model, BF16)])
    z, gathered = _matmul(h, wz, "nt", "proj_in", bn=1408, exchange=_Gather(other_shards))
    wl, wa, wur, wuf, wo, pp, pg = [_join_shards(g, ax) for (_, ax), g in zip(SHARDED[1:], gathered)]

    def rw_values(zs, w0_ref, wl_ref, a0_ref, wa_ref, kk_ref, ka_ref, e_ref):
        k = zs[:, c_rw:2 * c_rw]
        tw = jnp.tanh(zs[:, c4:c4 + lora])
        al = zs[:, c4 + lora:c4 + 2 * lora]
        sw = _sigmoid(w0_ref[...] + jnp.dot(tw.astype(BF16), wl_ref[...], preferred_element_type=F32))
        decay = jnp.exp(-decay_k * sw)
        a = _sigmoid(a0_ref[...] + jnp.dot(al.astype(BF16), wa_ref[...], preferred_element_type=F32))
        kk0 = k * kk_ref[...]
        nrm = jnp.sqrt(_headsum(kk0 * kk0, e_ref[...]))
        inv = 1.0 / jnp.maximum(nrm, 1e-12)
        k2 = k * (1.0 + (a - 1.0) * ka_ref[...])
        return k, tw, al, sw, decay, a, kk0, nrm, inv, k2

    def rw_prep(z_ref, mu_ref, w0_ref, wl_ref, a0_ref, wa_ref, kk_ref, ka_ref, e_ref,
                r_o, w_o, k_o, v_o, kk_o, a_o, g_o, carry):
        _first_step_zero(carry)
        zv = z_ref[...]
        zs = zv + (_shift_down(zv, carry) - zv) * mu_ref[...]
        k, tw, al, sw, decay, a, kk0, nrm, inv, k2 = rw_values(zs, w0_ref, wl_ref, a0_ref, wa_ref, kk_ref, ka_ref, e_ref)
        r_o[...] = zs[:, 0:c_rw]
        w_o[...] = decay
        k_o[...] = k2
        v_o[...] = zs[:, 2 * c_rw:3 * c_rw]
        kk_o[...] = kk0 * inv
        a_o[...] = a
        g_o[...] = zs[:, 3 * c_rw:c4]

    rw_consts = [mu, w0, wl, a0, wa, k_k, k_a, e_head]
    r_s, w_s, k_s, v_s, kk_s, a_s, g_s = _rowcall(
        rw_prep, "rwkv_prep", t_dim, bt, [(z, 0, sec)], rw_consts, [(c_rw, F32)] * 7,
        scratch=[pltpu.VMEM((1, sec), F32)])
    pairs_fwd = max(n for n in (1, 2, 4) if c_rw % (n * LANES) == 0)
    pairs_bwd = pairs_fwd
    y_s, s_all, s_fin = _scan_fwd(r_s, w_s, k_s, v_s, kk_s, a_s, 128, pairs_fwd)

    def rw_post_values(y, r, k2, v, g, lng_ref, lnb_ref, rk_ref, e):
        mean = _headsum(y, e) * (1.0 / HEAD)
        d = y - mean
        rstd = lax.rsqrt(_headsum(d * d, e) * (1.0 / HEAD) + GN_EPS)
        yh = d * rstd
        rk = _headsum(r * k2 * rk_ref[...], e)
        yo = yh * lng_ref[...] + lnb_ref[...] + rk * v
        sg = _sigmoid(g)
        return rstd, yh, rk, yo, sg

    def rw_post(y_ref, r_ref, k_ref, v_ref, g_ref, lng_ref, lnb_ref, rk_ref, e_ref, out_ref):
        g = g_ref[...]
        _, _, _, yo, sg = rw_post_values(y_ref[...], r_ref[...], k_ref[...], v_ref[...], g, lng_ref, lnb_ref, rk_ref, e_ref[...])
        out_ref[...] = (yo * g * sg).astype(BF16)

    (y_rw,) = _rowcall(rw_post, "rwkv_post", t_dim, bt, [y_s, r_s, k_s, v_s, g_s], [ln_g, ln_b, r_k, e_head], [(c_rw, BF16)])

    fl_cb = (sec + 4 * c_fox) // LANES
    hp = _round_up(h_fox, SUBLANES)
    bt_c = _pick(t_dim, 256)
    tri = (jnp.arange(bt_c)[:, None] >= jnp.arange(bt_c)[None, :]).astype(F32)

    rows8 = n_pairs * SUBLANES
    pair_rows = (jnp.arange(rows8)[:, None] // SUBLANES * 2 + jnp.arange(rows8)[:, None] % SUBLANES
                 == jnp.arange(LANES)[None, :]) & (jnp.arange(rows8)[:, None] % SUBLANES < 2)
    pair_rows = pair_rows.astype(F32)

    def fox_decay(fl_ref, bf_ref, tri_ref, sel_ref, ct_ref, cq_ref, carry):
        _first_step_zero(carry)
        lf = _log_sigmoid(fl_ref[...] + bf_ref[...])
        c = jnp.dot(tri_ref[...], lf, precision=HI, preferred_element_type=F32) + carry[...]
        carry[...] = c[bt_c - 1:bt_c, :]
        ct = jnp.dot(sel_ref[...], jnp.transpose(c), precision=HI, preferred_element_type=F32)
        ct_ref[...] = ct.reshape(n_pairs, SUBLANES, bt_c)
        for pair in range(n_pairs):
            cq_ref[pair] = c[:, 2 * pair:2 * pair + 2]

    ct, cq = pl.pallas_call(
        fox_decay, name="fox_decay", grid=(t_dim // bt_c,),
        in_specs=[pl.BlockSpec((bt_c, LANES), lambda i: (i, fl_cb)), pl.BlockSpec((1, LANES), lambda i: (0, 0)),
                  pl.BlockSpec((bt_c, bt_c), lambda i: (0, 0)), pl.BlockSpec((rows8, LANES), lambda i: (0, 0))],
        out_specs=[pl.BlockSpec((n_pairs, SUBLANES, bt_c), lambda i: (0, 0, i)),
                   pl.BlockSpec((n_pairs, bt_c, 2), lambda i: (0, i, 0))],
        out_shape=[jax.ShapeDtypeStruct((n_pairs, SUBLANES, t_dim), F32), jax.ShapeDtypeStruct((n_pairs, t_dim, 2), F32)],
        scratch_shapes=[pltpu.VMEM((1, LANES), F32)], compiler_params=_params("arbitrary"),
    )(z, b_f, tri, pair_rows)
    q_cb = sec // LANES
    k_cb, v_cb = q_cb + n_pairs, q_cb + 2 * n_pairs
    o_fox, lse = _fox_fwd(z, ct, cq, q_cb, k_cb, v_cb, n_pairs, 1024)

    def fox_post(o_ref, z_ref, out_ref):
        g = z_ref[:, 3 * c_fox:4 * c_fox]
        out_ref[...] = (o_ref[...] * g * _sigmoid(g)).astype(BF16)

    (y_fox,) = _rowcall(fox_post, "fox_post", t_dim, bt, [o_fox, (z, 1, sec)], [], [(c_fox, BF16)])

    u_rw = _matmul(y_rw, wur, "nn", "up_rwkv")
    u_fox = _matmul(y_fox, wuf, "nn", "up_fox")

    def merge(ur_ref, uf_ref, z_ref, out_ref):
        s1 = _sigmoid(z_ref[:, 0:d_model])
        s2 = _sigmoid(z_ref[:, d_model:2 * d_model])
        out_ref[...] = (s1 * ur_ref[...] + s2 * uf_ref[...]).astype(BF16)

    (merged,) = _rowcall(merge, "merge", t_dim, bt, [u_rw, u_fox, (z, 2, sec)], [], [(d_model, BF16)])
    mo = _matmul(merged, wo, "nn", "proj_out")

    def resid_norm(x_ref, mo_ref, g_ref, x1_ref, n2_ref):
        x1 = x_ref[...] + mo_ref[...]
        rms = lax.rsqrt(jnp.mean(x1 * x1, axis=-1, keepdims=True) + NORM_EPS)
        x1_ref[...] = x1
        n2_ref[...] = (x1 * rms * g_ref[...]).astype(BF16)

    x1, n2 = _rowcall(resid_norm, "resid_norm", t_dim, bt, [x, mo], [g2], [(d_model, F32), (d_model, BF16)])
    ple = _matmul(p, pp, "nn", "ple_proj")
    gl = _matmul(n2, pg, "nn", "ple_gate")

    def head(x1_ref, ple_ref, gl_ref, tgt_ref, g_ref, dx2_ref, dple_ref, dgl_ref, loss_ref, dg3_ref):
        _first_step_zero(loss_ref, dg3_ref)
        sg = _sigmoid(gl_ref[...])
        pl_v = ple_ref[...]
        x2 = x1_ref[...] + pl_v * sg
        rms = lax.rsqrt(jnp.mean(x2 * x2, axis=-1, keepdims=True) + NORM_EPS)
        xn = x2 * rms
        diff = xn * g_ref[...] - tgt_ref[...]
        loss_ref[...] += 0.5 * jnp.sum(jnp.mean(diff * diff, axis=-1, keepdims=True), axis=0, keepdims=True)
        dyf = diff * inv_d
        dg3_ref[...] += _colsum(dyf * xn)
        gy = dyf * g_ref[...]
        dx2 = rms * (gy - xn * jnp.mean(xn * gy, axis=-1, keepdims=True))
        dx2_ref[...] = dx2
        dple_ref[...] = (dx2 * sg).astype(BF16)
        dgl_ref[...] = (dx2 * pl_v * sg * (1.0 - sg)).astype(BF16)

    dx2, dple, dgl, loss, d_g3 = _rowcall(
        head, "head", t_dim, bt, [x1, ple, gl, tgt], [g3], [(d_model, F32), (d_model, BF16), (d_model, BF16)],
        acc_outs=[(1, 1), (1, d_model)])

    d_pp = _matmul(p, dple, "tn", "d_ple_proj", out_dtype=BF16)
    d_pg = _matmul(n2, dgl, "tn", "d_ple_gate", out_dtype=BF16)
    dn2 = _matmul(dgl, pg, "nt", "d_n2")

    def resid_norm_bwd(dx2_ref, dn2_ref, x1_ref, g_ref, dx1_ref, dx1b_ref, dg2_ref):
        _first_step_zero(dg2_ref)
        x1 = x1_ref[...]
        rms = lax.rsqrt(jnp.mean(x1 * x1, axis=-1, keepdims=True) + NORM_EPS)
        xn = x1 * rms
        dn = dn2_ref[...]
        dg2_ref[...] += _colsum(dn * xn)
        gy = dn * g_ref[...]
        dx1 = dx2_ref[...] + rms * (gy - xn * jnp.mean(xn * gy, axis=-1, keepdims=True))
        dx1_ref[...] = dx1
        dx1b_ref[...] = dx1.astype(BF16)

    dx1, dx1b, d_g2 = _rowcall(resid_norm_bwd, "resid_norm_bwd", t_dim, bt, [dx2, dn2, x1], [g2],
                               [(d_model, F32), (d_model, BF16)], acc_outs=[(1, d_model)])
    d_wo = _matmul(merged, dx1b, "tn", "d_w_out", out_dtype=BF16)
    dmerged = _matmul(dx1b, wo, "nt", "d_merged")

    def merge_bwd(dm_ref, ur_ref, uf_ref, z_ref, dur_ref, duf_ref, dzg_ref):
        dm = dm_ref[...]
        s1 = _sigmoid(z_ref[:, 0:d_model])
        s2 = _sigmoid(z_ref[:, d_model:2 * d_model])
        dur_ref[...] = (dm * s1).astype(BF16)
        duf_ref[...] = (dm * s2).astype(BF16)
        dzg_ref[:, 0:d_model] = (dm * ur_ref[...] * s1 * (1.0 - s1)).astype(BF16)
        dzg_ref[:, d_model:2 * d_model] = (dm * uf_ref[...] * s2 * (1.0 - s2)).astype(BF16)
        if sec > 2 * d_model:
            dzg_ref[:, 2 * d_model:sec] = jnp.zeros((dm.shape[0], sec - 2 * d_model), BF16)

    du_rw, du_fox, dz_gate = _rowcall(merge_bwd, "merge_bwd", t_dim, bt, [dmerged, u_rw, u_fox, (z, 2, sec)], [],
                                      [(d_model, BF16), (d_model, BF16), (sec, BF16)])
    d_wur = _matmul(y_rw, du_rw, "tn", "d_w_up_rwkv", out_dtype=BF16)
    d_wuf = _matmul(y_fox, du_fox, "tn", "d_w_up_fox", out_dtype=BF16)
    dy_rw = _matmul(du_rw, wur, "nt", "d_y_rwkv")
    dy_fox = _matmul(du_fox, wuf, "nt", "d_y_fox")

    def fox_post_bwd(dy_ref, o_ref, z_ref, e_ref, do_ref, dg_ref, rd_ref):
        g = z_ref[:, 3 * c_fox:4 * c_fox]
        sg = _sigmoid(g)
        dy, o = dy_ref[...], o_ref[...]
        do = dy * g * sg
        do_ref[...] = do
        dg_ref[...] = (dy * o * sg * (1.0 + g * (1.0 - sg))).astype(BF16)
        rd_ref[...] = _headsum(do.astype(BF16).astype(F32) * o, e_ref[...])

    do_fox, dg_fox, rowdot = _rowcall(fox_post_bwd, "fox_post_bwd", t_dim, bt, [dy_fox, o_fox, (z, 1, sec)], [e_head],
                                      [(c_fox, F32), (c_fox, BF16), (c_fox, F32)])
    dq_f, dk_f, dv_f, dc_t = _fox_bwd(z, ct, cq, rowdot, do_fox, lse, q_cb, k_cb, v_cb, n_pairs, 512)
    sel = (jnp.arange(hp)[:, None] // 2 * SUBLANES + jnp.arange(hp)[:, None] % 2 == jnp.arange(rows8)[None, :]).astype(F32)
    tri_rev = (jnp.arange(bt_c)[:, None] >= jnp.arange(bt_c)[None, :]).astype(F32)
    bf_col = b_f.reshape(LANES, 1)[0:hp]
    nbc = t_dim // bt_c

    def fox_decay_bwd(dc_ref, fl_ref, sel_ref, tri_ref, bf_ref, dfl_ref, dbf_ref, carry):
        _first_step_zero(carry, dbf_ref)
        dc = jnp.dot(sel_ref[...], dc_ref[...].reshape(rows8, bt_c), precision=HI, preferred_element_type=F32)
        dlf = jnp.dot(dc, tri_ref[...], precision=HI, preferred_element_type=F32) + carry[...]
        carry[...] = dlf[:, 0:1]
        flt = jnp.transpose(fl_ref[...])[0:hp, :]
        dfl = dlf * _sigmoid(-(flt + bf_ref[...]))
        head_row = lax.broadcasted_iota(jnp.int32, (hp, bt_c), 0)
        dfl = jnp.where(head_row < h_fox, dfl, 0.0)
        dbf_ref[...] += jnp.sum(dfl, axis=1, keepdims=True)
        full = jnp.concatenate([dfl, jnp.zeros((LANES - hp, bt_c), F32)], axis=0) if hp < LANES else dfl
        dfl_ref[...] = jnp.transpose(full).astype(BF16)

    dz_fl, d_bf = pl.pallas_call(
        fox_decay_bwd, name="fox_decay_bwd", grid=(nbc,),
        in_specs=[pl.BlockSpec((n_pairs, SUBLANES, bt_c), lambda i: (0, 0, nbc - 1 - i)),
                  pl.BlockSpec((bt_c, LANES), lambda i: (nbc - 1 - i, fl_cb)),
                  pl.BlockSpec(sel.shape, lambda i: (0, 0)), pl.BlockSpec((bt_c, bt_c), lambda i: (0, 0)),
                  pl.BlockSpec((hp, 1), lambda i: (0, 0))],
        out_specs=[pl.BlockSpec((bt_c, LANES), lambda i: (nbc - 1 - i, 0)), pl.BlockSpec((hp, 1), lambda i: (0, 0))],
        out_shape=[jax.ShapeDtypeStruct((t_dim, LANES), BF16), jax.ShapeDtypeStruct((hp, 1), F32)],
        scratch_shapes=[pltpu.VMEM((hp, 1), F32)], compiler_params=_params("arbitrary"),
    )(dc_t, z, sel, tri_rev, bf_col)

    def rw_post_bwd(dy_ref, y_ref, r_ref, k_ref, v_ref, g_ref, lng_ref, lnb_ref, rk_ref, e_ref,
                    dg_ref, dys_ref, dr_ref, dk_ref, dv_ref, dlng_ref, dlnb_ref, drk_ref):
        _first_step_zero(dlng_ref, dlnb_ref, drk_ref)
        e = e_ref[...]
        dy, r, k2, v, g = dy_ref[...], r_ref[...], k_ref[...], v_ref[...], g_ref[...]
        rstd, yh, rk, yo, sg = rw_post_values(y_ref[...], r, k2, v, g, lng_ref, lnb_ref, rk_ref, e)
        dg_ref[...] = dy * yo * sg * (1.0 + g * (1.0 - sg))
        dyo = dy * g * sg
        dlnb_ref[...] += _colsum(dyo)
        dlng_ref[...] += _colsum(dyo * yh)
        dyh = dyo * lng_ref[...]
        dys_ref[...] = rstd * (dyh - _headsum(dyh, e) * (1.0 / HEAD) - yh * _headsum(dyh * yh, e) * (1.0 / HEAD))
        drk = _headsum(dyo * v, e)
        dv_ref[...] = dyo * rk
        dr_ref[...] = drk * k2 * rk_ref[...]
        dk_ref[...] = drk * r * rk_ref[...]
        drk_ref[...] += _colsum(drk * r * k2)

    dg_rw, dy_s, dr_b, dk_b, dv_b, d_lng, d_lnb, d_rk = _rowcall(
        rw_post_bwd, "rwkv_post_bwd", t_dim, bt, [dy_rw, y_s, r_s, k_s, v_s, g_s], [ln_g, ln_b, r_k, e_head],
        [(c_rw, F32)] * 5, acc_outs=[(1, c_rw)] * 3)
    axis = dict(SHARDED)
    early = {"w_up_rwkv": d_wur, "w_up_fox": d_wuf, "w_out": d_wo, "ple_proj": d_pp, "ple_gate_w": d_pg}
    (dr_c, dw_c, dk_c, dv_c, dkk_c, da_c), early_recv = _scan_bwd(
        r_s, w_s, k_s, v_s, kk_s, a_s, dy_s, s_all, s_fin, 128, pairs_bwd,
        _Exchange([_split_shards(g, axis[n]) for n, g in early.items()]))
    recv = dict(zip(early, early_recv))

    def rw_prep_bwd(z_ref, dr1, dr2, dw_ref, dk1, dk2_ref, dv1, dv2, dkk_ref, da_ref, dg_ref,
                    mu_ref, w0_ref, wl_ref, a0_ref, wa_ref, kk_ref, ka_ref, e_ref,
                    dzs_ref, tw_ref, al_ref, dwr_ref, dar_ref, dmu_ref, dw0_ref, da0_ref, dkk_acc, dka_acc, carry):
        _first_step_zero(carry, dmu_ref, dw0_ref, da0_ref, dkk_acc, dka_acc)
        e = e_ref[...]
        zv = z_ref[...]
        zp = _shift_down(zv, carry)
        zs = zv + (zp - zv) * mu_ref[...]
        k, tw, al, sw, decay, a, kk0, nrm, inv, k2 = rw_values(zs, w0_ref, wl_ref, a0_ref, wa_ref, kk_ref, ka_ref, e_ref)
        dk2 = dk1[...] + dk2_ref[...]
        da = da_ref[...] + dk2 * k * ka_ref[...]
        dk = dk2 * (1.0 + (a - 1.0) * ka_ref[...])
        dka_acc[...] += _colsum(dk2 * k * (a - 1.0))
        kk = kk0 * inv
        dkk = dkk_ref[...]
        dkk0 = inv * jnp.where(nrm > 1e-12, dkk - kk * _headsum(dkk * kk, e), dkk)
        dk = dk + dkk0 * kk_ref[...]
        dkk_acc[...] += _colsum(dkk0 * k)
        da_raw = da * a * (1.0 - a)
        da0_ref[...] += _colsum(da_raw)
        dw_raw = dw_ref[...] * decay * (-decay_k) * sw * (1.0 - sw)
        dw0_ref[...] += _colsum(dw_raw)
        dar_b, dwr_b = da_raw.astype(BF16), dw_raw.astype(BF16)
        dal = lax.dot_general(dar_b, wa_ref[...], _DOT_DIMS["nt"], preferred_element_type=F32)
        dtw = lax.dot_general(dwr_b, wl_ref[...], _DOT_DIMS["nt"], preferred_element_type=F32)
        dzs_ref[:, 0:c_rw] = dr1[...] + dr2[...]
        dzs_ref[:, c_rw:2 * c_rw] = dk
        dzs_ref[:, 2 * c_rw:3 * c_rw] = dv1[...] + dv2[...]
        dzs_ref[:, 3 * c_rw:c4] = dg_ref[...]
        dzs_ref[:, c4:c4 + lora] = dtw * (1.0 - tw * tw)
        dzs_ref[:, c4 + lora:c4 + 2 * lora] = dal
        if sec > c4 + 2 * lora:
            dzs_ref[:, c4 + 2 * lora:sec] = jnp.zeros((zv.shape[0], sec - c4 - 2 * lora), F32)
        tw_ref[...] = tw.astype(BF16)
        al_ref[...] = al.astype(BF16)
        dwr_ref[...] = dwr_b
        dar_ref[...] = dar_b
        dmu_ref[...] += _colsum(dzs_ref[...] * (zp - zv))

    dzs, tw_b, al_b, dwr_b, dar_b, d_mu, d_w0, d_a0, d_kk, d_ka = _rowcall(
        rw_prep_bwd, "rwkv_prep_bwd", t_dim, bt_many,
        [(z, 0, sec), dr_c, dr_b, dw_c, dk_c, dk_b, dv_c, dv_b, dkk_c, da_c, dg_rw], rw_consts,
        [(sec, F32), (lora, BF16), (lora, BF16), (c_rw, BF16), (c_rw, BF16)],
        acc_outs=[(1, sec), (1, c_rw), (1, c_rw), (1, c_rw), (1, c_rw)], scratch=[pltpu.VMEM((1, sec), F32)])
    d_wl = _matmul(tw_b, dwr_b, "tn", "d_w_lora", out_dtype=BF16)
    d_wa = _matmul(al_b, dar_b, "tn", "d_a_lora", out_dtype=BF16)

    def shift_bwd(dzs_ref, mu_ref, dz_ref, carry):
        _first_step_zero(carry)
        d = dzs_ref[...]
        nbt = d.shape[0]
        nxt = pltpu.roll(d, nbt - 1, 0)
        rowi = lax.broadcasted_iota(jnp.int32, d.shape, 0)
        nxt = jnp.where(rowi == nbt - 1, carry[...], nxt)
        carry[...] = d[0:1, :]
        m = mu_ref[...]
        dz_ref[...] = (d * (1.0 - m) + nxt * m).astype(BF16)

    (dz_rw,) = _rowcall(shift_bwd, "shift_bwd", t_dim, bt, [dzs], [mu], [(sec, BF16)],
                        scratch=[pltpu.VMEM((1, sec), F32)], reverse=True)

    fox_parts = [dq_f.astype(BF16), dk_f.astype(BF16), dv_f.astype(BF16), dg_fox, dz_fl]
    if sec > 4 * c_fox + LANES:
        fox_parts.append(jnp.zeros((t_dim, sec - 4 * c_fox - LANES), BF16))
    dz = jnp.concatenate([dz_rw] + fox_parts + [dz_gate], axis=1)
    d_wz = _matmul(dz, h, "tn", "d_w_in", out_dtype=BF16, bm=1408)
    rw_cols, fox_cols = c4 + 2 * lora, 4 * c_fox + h_fox
    bounds = (0, rw_cols, rw_cols + fox_cols, rw_cols + fox_cols + 2 * d_model)
    shard_rows = bounds[3] // NDEV

    def shard(d):
        lo, hi = d * shard_rows, (d + 1) * shard_rows
        pieces = [d_wz[s * sec + max(lo, bounds[s]) - bounds[s]:s * sec + min(hi, bounds[s + 1]) - bounds[s]]
                  for s in range(3) if max(lo, bounds[s]) < min(hi, bounds[s + 1])]
        return pieces[0] if len(pieces) == 1 else jnp.concatenate(pieces, axis=0)

    parts = jnp.stack([shard(d) for d in range(NDEV)])
    chip_parts = _add_pairs(parts, _sibling_swap(parts, "swap_w_in"), "add_w_in")
    dh, (recv["w_in"],) = _matmul(dz, wz, "nn", "d_h", bk=sec, exchange=_ChipExchange([chip_parts]))
    late_parts = [_split_shards(d_wl, 1), _split_shards(d_wa, 1)]

    def norm_in_bwd(dh_ref, x_ref, dx1_ref, g_ref, dx_ref, dg1_ref):
        _first_step_zero(dg1_ref)
        xv = x_ref[...]
        rms = lax.rsqrt(jnp.mean(xv * xv, axis=-1, keepdims=True) + NORM_EPS)
        xn = xv * rms
        d = dh_ref[...]
        dg1_ref[...] += _colsum(d * xn)
        gy = d * g_ref[...]
        dx_ref[...] = dx1_ref[...] + rms * (gy - xn * jnp.mean(xn * gy, axis=-1, keepdims=True))

    dx, d_g1 = _rowcall(norm_in_bwd, "norm_in_bwd", t_dim, bt, [dh, x, dx1], [norm_g], [(d_model, F32)],
                        acc_outs=[(1, d_model)])

    rep_grads = {"norm_g": d_g1, "rw_shift_mu": d_mu[:, 0:c4 + 2 * lora], "rw_w0": d_w0, "rw_a0": d_a0, "rw_k_k": d_kk,
                 "rw_k_a": d_ka, "rw_r_k": d_rk, "rw_ln_g": d_lng, "rw_ln_b": d_lnb, "fox_b_f": d_bf[0:h_fox, 0],
                 "ple_norm_g": d_g2, "final_norm_g": d_g3}
    return loss[0, 0], dx, recv, late_parts, rep_grads


def kernel(x, p, norm_g, w_in, rw_shift_mu, rw_w0, rw_w_lora_up, rw_a0, rw_a_lora_up, rw_k_k, rw_k_a, rw_r_k, rw_ln_g, rw_ln_b, fox_b_f, w_up_rwkv, w_up_fox, w_out, ple_proj, ple_gate_w, ple_norm_g, final_norm_g, loss_target, m_norm_g, m_w_in, m_rw_shift_mu, m_rw_w0, m_rw_w_lora_up, m_rw_a0, m_rw_a_lora_up, m_rw_k_k, m_rw_k_a, m_rw_r_k, m_rw_ln_g, m_rw_ln_b, m_fox_b_f, m_w_up_rwkv, m_w_up_fox, m_w_out, m_ple_proj, m_ple_gate_w, m_ple_norm_g, m_final_norm_g, v_norm_g, v_w_in, v_rw_shift_mu, v_rw_w0, v_rw_w_lora_up, v_rw_a0, v_rw_a_lora_up, v_rw_k_k, v_rw_k_a, v_rw_r_k, v_rw_ln_g, v_rw_ln_b, v_fox_b_f, v_w_up_rwkv, v_w_up_fox, v_w_out, v_ple_proj, v_ple_gate_w, v_ple_norm_g, v_final_norm_g):
    args = locals()
    w = {n: args[n] for n in WEIGHTS}
    mom = {n: args["m_" + n] for n in WEIGHTS}
    var = {n: args["v_" + n] for n in WEIGHTS}

    t_dim, d_model = x.shape[1], x.shape[2]
    c_rw, lora = rw_w0.shape[1], rw_w_lora_up.shape[1]
    h_fox = fox_b_f.shape[1]
    c_fox = h_fox * HEAD
    rw_cols, fox_cols, gate_cols = 4 * c_rw + 2 * lora, 4 * c_fox + h_fox, 2 * d_model
    sec = max(rw_cols, 4 * c_fox + LANES, _round_up(gate_cols, LANES))
    assert c_rw % LANES == 0 and c_fox % LANES == 0 and rw_cols % LANES == 0 and h_fox <= LANES and d_model % LANES == 0
    assert rw_a_lora_up.shape[1] == lora and w_in.shape[2] * NDEV == rw_cols + fox_cols + gate_cols

    assert SHARDED[0] == ("w_in", 1)
    (w_in_all,) = _all_gather([w_in[0].T.astype(BF16)], "gather_w_in")
    other_shards = [w[n][0].astype(BF16) for n, _ in SHARDED[1:]]

    def to_sections(shards_t):
        shard_rows = shards_t.shape[1]
        bounds = (0, rw_cols, rw_cols + fox_cols, rw_cols + fox_cols + gate_cols)
        pieces = []
        for s in range(3):
            for d in range(NDEV):
                lo, hi = max(bounds[s], d * shard_rows), min(bounds[s + 1], (d + 1) * shard_rows)
                if lo < hi:
                    pieces.append(shards_t[d, lo - d * shard_rows:hi - d * shard_rows])
            if sec > bounds[s + 1] - bounds[s]:
                pieces.append(jnp.zeros((sec - (bounds[s + 1] - bounds[s]), d_model), shards_t.dtype))
        return jnp.concatenate(pieces)

    rep = {n: w[n] for n in REPLICATED}
    dims = (t_dim, d_model, c_rw, lora, c_fox, h_fox, sec)
    loss_local, grad_x, recv, late_parts, rep_grads = _local_step(
        x[0], p[0, 0], loss_target[0], to_sections(w_in_all), other_shards, rep, dims)

    rep_sizes = [w[n].size for n in REPLICATED]
    rep_offs = [sum(rep_sizes[:i]) for i in range(len(rep_sizes))]
    pack_rep = lambda tree: _pack_rows(jnp.concatenate([tree[n].astype(F32).reshape(-1) for n in REPLICATED]))
    small_all, recv["rw_w_lora_up"], recv["rw_a_lora_up"] = _grad_exchange([pack_rep(rep_grads)] + late_parts,
                                                                           "exchange_small")
    kinds = ("grad", "delta", "new_m", "new_v")
    outs = {}
    for n, _ in SHARDED:
        if n == "w_in":
            update = _adamw(recv[n], w_in[0].T, m_w_in[0].T, v_w_in[0].T, "adamw_" + n)
            update = [jnp.transpose(buf, (0, 2, 1)) for buf in update]
        else:
            update = _adamw(recv[n], w[n][0], mom[n][0], var[n][0], "adamw_" + n)
        for kind, buf in zip(kinds, update):
            outs[kind, n] = buf
    for kind, buf in zip(kinds, _adamw(small_all, pack_rep(w), pack_rep(mom), pack_rep(var), "adamw_replicated")):
        flat = buf.reshape(-1)
        for n, o, s in zip(REPLICATED, rep_offs, rep_sizes):
            outs[kind, n] = flat[o:o + s].reshape(w[n].shape)
    loss = lax.psum(loss_local, MESH_AXES)
    return (loss, grad_x[None], *[outs[kind, n] for kind in ("grad", "delta", "new_m", "new_v") for n in WEIGHTS])
```
